```python
import math
import jax, jax.numpy as jnp
from jax import lax
import numpy as np

D_MODEL = 1024
BATCH = 8
SEQ = 8192
DEPTH = 4

SSM_GROUP = 16
N_GROUPS = D_MODEL // SSM_GROUP
SSM_STATE = 64
N_HEADS = 16
HEAD_DIM = D_MODEL // N_HEADS
ATTN_DIM = N_HEADS * HEAD_DIM
D_FF = 4 * D_MODEL
Q_BLOCK = 128
N_A_LAYERS = DEPTH // 2
N_B_LAYERS = DEPTH - N_A_LAYERS
RMS_EPS = 1e-6
DT_MIN = 1e-3
DT_MAX = 1e-1

kernel_name = "s5_fox_yoco_hybrid_trunk"


def rmsnorm(x, g):
    xf = x.astype(jnp.float32)
    y = xf * lax.rsqrt(jnp.mean(xf * xf, axis=-1, keepdims=True) + RMS_EPS)
    return (y * g.astype(jnp.float32)).astype(x.dtype)


def sqrelu_mlp(h, w1, w2):
    a = jnp.square(jax.nn.relu(h @ w1))
    return a @ w2


def _ssm_binop(e1, e2):
    a1, b1 = e1
    a2, b2 = e2
    return a2 * a1, a2 * b1 + b2


def s5_mixer(u, log_dt, a_re, a_im, b_re, b_im, c_re, c_im, d_skip, w_glu):
    f32 = jnp.float32
    bsz, length, _ = u.shape
    uf = u.astype(f32).reshape(bsz, length, N_GROUPS, SSM_GROUP)
    lam = lax.complex(a_re.astype(f32), a_im.astype(f32))
    dt = jnp.exp(log_dt.astype(f32))[:, None]
    lam_bar = jnp.exp(lam * dt)
    b = lax.complex(b_re.astype(f32), b_im.astype(f32))
    b_bar = ((lam_bar - 1.0) / lam)[..., None] * b
    bu = jnp.einsum('blgc,gpc->blgp', uf.astype(jnp.complex64), b_bar)
    a_elems = jnp.broadcast_to(lam_bar, bu.shape)
    _, states = lax.associative_scan(_ssm_binop, (a_elems, bu), axis=1)
    c = lax.complex(c_re.astype(f32), c_im.astype(f32))
    y = jnp.real(jnp.einsum('blgp,gcp->blgc', states, c))
    y = y + d_skip.astype(f32).reshape(N_GROUPS, SSM_GROUP) * uf
    z = jax.nn.gelu(y.reshape(bsz, length, D_MODEL)).astype(u.dtype)
    zw = z @ w_glu
    val, gate = zw[..., :D_MODEL], zw[..., D_MODEL:]
    return val * jax.nn.sigmoid(gate)


def shared_kv(h, kv_norm, w_kvf, b_f):
    bsz, length, _ = h.shape
    kvf = rmsnorm(h, kv_norm) @ w_kvf
    k = kvf[..., :ATTN_DIM].reshape(bsz, length, N_HEADS, HEAD_DIM).transpose(0, 2, 1, 3)
    v = kvf[..., ATTN_DIM:2 * ATTN_DIM].reshape(bsz, length, N_HEADS, HEAD_DIM).transpose(0, 2, 1, 3)
    f_logit = kvf[..., 2 * ATTN_DIM:].astype(jnp.float32) + b_f.astype(jnp.float32)
    log_f = jax.nn.log_sigmoid(f_logit)
    cum_log_f = jnp.cumsum(log_f, axis=1).transpose(0, 2, 1)
    return k, v, cum_log_f


def fox_attention(hn, wq, wo, k, v, cum_log_f):
    bsz, length, _ = hn.shape
    nb = length // Q_BLOCK
    q = (hn @ wq) * (HEAD_DIM ** -0.5)
    q_blocks = q.reshape(bsz, nb, Q_BLOCK, N_HEADS, HEAD_DIM).transpose(1, 0, 3, 2, 4)
    f_blocks = cum_log_f.reshape(bsz, N_HEADS, nb, Q_BLOCK).transpose(2, 0, 1, 3)
    pos_q = jnp.arange(length, dtype=jnp.int32).reshape(nb, Q_BLOCK)
    pos_k = jnp.arange(length, dtype=jnp.int32)

    def one_block(args):
        qb, fq, pq = args
        s = jnp.einsum('bhqd,bhkd->bhqk', qb, k).astype(jnp.float32)
        s = s + fq[..., None] - cum_log_f[:, :, None, :]
        mask = pq[:, None] >= pos_k[None, :]
        s = jnp.where(mask[None, None], s, -jnp.inf)
        p = jax.nn.softmax(s, axis=-1)
        return jnp.einsum('bhqk,bhkd->bhqd', p.astype(v.dtype), v)

    o = lax.map(one_block, (q_blocks, f_blocks, pos_q))
    o = o.transpose(1, 0, 3, 2, 4).reshape(bsz, length, ATTN_DIM)
    return o @ wo


def _fwd_setup_inputs(seed: int = 0) -> dict:
    key = jax.random.key(seed)
    ks = jax.random.split(key, 20)
    f32 = jnp.float32

    def nrm(k, shape, scale):
        return scale * jax.random.normal(k, shape, f32)

    x = nrm(ks[0], (BATCH, SEQ, D_MODEL), 1.0)
    mix_norm = 1.0 + nrm(ks[1], (DEPTH, D_MODEL), 0.05)
    mlp_norm = 1.0 + nrm(ks[2], (DEPTH, D_MODEL), 0.05)
    mlp_w1 = nrm(ks[3], (DEPTH, D_MODEL, D_FF), D_MODEL ** -0.5)
    mlp_w2 = nrm(ks[4], (DEPTH, D_FF, D_MODEL), 0.5 * D_FF ** -0.5)
    ssm_log_dt = jax.random.uniform(ks[5], (N_A_LAYERS, N_GROUPS), f32,
                                    math.log(DT_MIN), math.log(DT_MAX))
    ssm_a_re = -0.5 + nrm(ks[6], (N_A_LAYERS, N_GROUPS, SSM_STATE), 0.01)
    ssm_a_im = math.pi * jnp.arange(SSM_STATE, dtype=f32) + nrm(ks[7], (N_A_LAYERS, N_GROUPS, SSM_STATE), 0.01)
    ssm_b_re = nrm(ks[8], (N_A_LAYERS, N_GROUPS, SSM_STATE, SSM_GROUP), (2 * SSM_GROUP) ** -0.5)
    ssm_b_im = nrm(ks[9], (N_A_LAYERS, N_GROUPS, SSM_STATE, SSM_GROUP), (2 * SSM_GROUP) ** -0.5)
    ssm_c_re = nrm(ks[10], (N_A_LAYERS, N_GROUPS, SSM_GROUP, SSM_STATE), (2 * SSM_STATE) ** -0.5)
    ssm_c_im = nrm(ks[11], (N_A_LAYERS, N_GROUPS, SSM_GROUP, SSM_STATE), (2 * SSM_STATE) ** -0.5)
    ssm_d = nrm(ks[12], (N_A_LAYERS, D_MODEL), 1.0)
    ssm_w_glu = nrm(ks[13], (N_A_LAYERS, D_MODEL, 2 * D_MODEL), D_MODEL ** -0.5)
    kv_norm = 1.0 + nrm(ks[14], (D_MODEL,), 0.05)
    w_kvf = nrm(ks[15], (D_MODEL, 2 * ATTN_DIM + N_HEADS), D_MODEL ** -0.5)
    b_f = jax.random.uniform(ks[16], (N_HEADS,), f32, 0.5, 3.0)
    attn_wq = nrm(ks[17], (N_B_LAYERS, D_MODEL, ATTN_DIM), D_MODEL ** -0.5)
    attn_wo = nrm(ks[18], (N_B_LAYERS, ATTN_DIM, D_MODEL), ATTN_DIM ** -0.5)
    final_norm = 1.0 + nrm(ks[19], (D_MODEL,), 0.05)
    return {"x": x, "mix_norm": mix_norm, "mlp_norm": mlp_norm, "mlp_w1": mlp_w1, "mlp_w2": mlp_w2,
            "ssm_log_dt": ssm_log_dt, "ssm_a_re": ssm_a_re, "ssm_a_im": ssm_a_im,
            "ssm_b_re": ssm_b_re, "ssm_b_im": ssm_b_im, "ssm_c_re": ssm_c_re, "ssm_c_im": ssm_c_im,
            "ssm_d": ssm_d, "ssm_w_glu": ssm_w_glu, "kv_norm": kv_norm, "w_kvf": w_kvf, "b_f": b_f,
            "attn_wq": attn_wq, "attn_wo": attn_wo, "final_norm": final_norm}


def _fwd_reference(x, mix_norm, mlp_norm, mlp_w1, mlp_w2, ssm_log_dt, ssm_a_re, ssm_a_im,
              ssm_b_re, ssm_b_im, ssm_c_re, ssm_c_im, ssm_d, ssm_w_glu, kv_norm, w_kvf, b_f,
              attn_wq, attn_wo, final_norm):
    h = x
    k = v = cum_log_f = None
    for i in range(DEPTH):
        hn = rmsnorm(h, mix_norm[i])
        if i < N_A_LAYERS:
            h = h + s5_mixer(hn, ssm_log_dt[i], ssm_a_re[i], ssm_a_im[i], ssm_b_re[i], ssm_b_im[i],
                             ssm_c_re[i], ssm_c_im[i], ssm_d[i], ssm_w_glu[i])
        else:
            j = i - N_A_LAYERS
            h = h + fox_attention(hn, attn_wq[j], attn_wo[j], k, v, cum_log_f)
        h = h + sqrelu_mlp(rmsnorm(h, mlp_norm[i]), mlp_w1[i], mlp_w2[i])
        if i == N_A_LAYERS - 1:
            k, v, cum_log_f = shared_kv(h, kv_norm, w_kvf, b_f)
    return rmsnorm(h, final_norm)


import jax as _jax
import jax.numpy as _jnp

TWIN_FORMAT = 'train_step'
FWD_PARAMS = ['x', 'mix_norm', 'mlp_norm', 'mlp_w1', 'mlp_w2', 'ssm_log_dt', 'ssm_a_re', 'ssm_a_im', 'ssm_b_re', 'ssm_b_im', 'ssm_c_re', 'ssm_c_im', 'ssm_d', 'ssm_w_glu', 'kv_norm', 'w_kvf', 'b_f', 'attn_wq', 'attn_wo', 'final_norm']
TWIN_WEIGHTS = ['mix_norm', 'mlp_norm', 'mlp_w1', 'mlp_w2', 'ssm_log_dt', 'ssm_a_re', 'ssm_a_im', 'ssm_b_re', 'ssm_b_im', 'ssm_c_re', 'ssm_c_im', 'ssm_d', 'ssm_w_glu', 'kv_norm', 'w_kvf', 'b_f', 'attn_wq', 'attn_wo', 'final_norm']
TWIN_DIFF_INPUT = 'x'
TWIN_INPUTS = ['x', 'mix_norm', 'mlp_norm', 'mlp_w1', 'mlp_w2', 'ssm_log_dt', 'ssm_a_re', 'ssm_a_im', 'ssm_b_re', 'ssm_b_im', 'ssm_c_re', 'ssm_c_im', 'ssm_d', 'ssm_w_glu', 'kv_norm', 'w_kvf', 'b_f', 'attn_wq', 'attn_wo', 'final_norm', 'loss_target', 'm_mix_norm', 'm_mlp_norm', 'm_mlp_w1', 'm_mlp_w2', 'm_ssm_log_dt', 'm_ssm_a_re', 'm_ssm_a_im', 'm_ssm_b_re', 'm_ssm_b_im', 'm_ssm_c_re', 'm_ssm_c_im', 'm_ssm_d', 'm_ssm_w_glu', 'm_kv_norm', 'm_w_kvf', 'm_b_f', 'm_attn_wq', 'm_attn_wo', 'm_final_norm', 'v_mix_norm', 'v_mlp_norm', 'v_mlp_w1', 'v_mlp_w2', 'v_ssm_log_dt', 'v_ssm_a_re', 'v_ssm_a_im', 'v_ssm_b_re', 'v_ssm_b_im', 'v_ssm_c_re', 'v_ssm_c_im', 'v_ssm_d', 'v_ssm_w_glu', 'v_kv_norm', 'v_w_kvf', 'v_b_f', 'v_attn_wq', 'v_attn_wo', 'v_final_norm']
TWIN_OUTPUTS = ['loss', 'grad_x', 'grad_mix_norm', 'grad_mlp_norm', 'grad_mlp_w1', 'grad_mlp_w2', 'grad_ssm_log_dt', 'grad_ssm_a_re', 'grad_ssm_a_im', 'grad_ssm_b_re', 'grad_ssm_b_im', 'grad_ssm_c_re', 'grad_ssm_c_im', 'grad_ssm_d', 'grad_ssm_w_glu', 'grad_kv_norm', 'grad_w_kvf', 'grad_b_f', 'grad_attn_wq', 'grad_attn_wo', 'grad_final_norm', 'delta_mix_norm', 'delta_mlp_norm', 'delta_mlp_w1', 'delta_mlp_w2', 'delta_ssm_log_dt', 'delta_ssm_a_re', 'delta_ssm_a_im', 'delta_ssm_b_re', 'delta_ssm_b_im', 'delta_ssm_c_re', 'delta_ssm_c_im', 'delta_ssm_d', 'delta_ssm_w_glu', 'delta_kv_norm', 'delta_w_kvf', 'delta_b_f', 'delta_attn_wq', 'delta_attn_wo', 'delta_final_norm', 'new_m_mix_norm', 'new_m_mlp_norm', 'new_m_mlp_w1', 'new_m_mlp_w2', 'new_m_ssm_log_dt', 'new_m_ssm_a_re', 'new_m_ssm_a_im', 'new_m_ssm_b_re', 'new_m_ssm_b_im', 'new_m_ssm_c_re', 'new_m_ssm_c_im', 'new_m_ssm_d', 'new_m_ssm_w_glu', 'new_m_kv_norm', 'new_m_w_kvf', 'new_m_b_f', 'new_m_attn_wq', 'new_m_attn_wo', 'new_m_final_norm', 'new_v_mix_norm', 'new_v_mlp_norm', 'new_v_mlp_w1', 'new_v_mlp_w2', 'new_v_ssm_log_dt', 'new_v_ssm_a_re', 'new_v_ssm_a_im', 'new_v_ssm_b_re', 'new_v_ssm_b_im', 'new_v_ssm_c_re', 'new_v_ssm_c_im', 'new_v_ssm_d', 'new_v_ssm_w_glu', 'new_v_kv_norm', 'new_v_w_kvf', 'new_v_b_f', 'new_v_attn_wq', 'new_v_attn_wo', 'new_v_final_norm']
TWIN_LEAF_KINDS = {'loss': 'loss', 'grad_x': 'grad_x', 'grad_mix_norm': 'grad_w', 'grad_mlp_norm': 'grad_w', 'grad_mlp_w1': 'grad_w', 'grad_mlp_w2': 'grad_w', 'grad_ssm_log_dt': 'grad_w', 'grad_ssm_a_re': 'grad_w', 'grad_ssm_a_im': 'grad_w', 'grad_ssm_b_re': 'grad_w', 'grad_ssm_b_im': 'grad_w', 'grad_ssm_c_re': 'grad_w', 'grad_ssm_c_im': 'grad_w', 'grad_ssm_d': 'grad_w', 'grad_ssm_w_glu': 'grad_w', 'grad_kv_norm': 'grad_w', 'grad_w_kvf': 'grad_w', 'grad_b_f': 'grad_w', 'grad_attn_wq': 'grad_w', 'grad_attn_wo': 'grad_w', 'grad_final_norm': 'grad_w', 'delta_mix_norm': 'delta_w', 'delta_mlp_norm': 'delta_w', 'delta_mlp_w1': 'delta_w', 'delta_mlp_w2': 'delta_w', 'delta_ssm_log_dt': 'delta_w', 'delta_ssm_a_re': 'delta_w', 'delta_ssm_a_im': 'delta_w', 'delta_ssm_b_re': 'delta_w', 'delta_ssm_b_im': 'delta_w', 'delta_ssm_c_re': 'delta_w', 'delta_ssm_c_im': 'delta_w', 'delta_ssm_d': 'delta_w', 'delta_ssm_w_glu': 'delta_w', 'delta_kv_norm': 'delta_w', 'delta_w_kvf': 'delta_w', 'delta_b_f': 'delta_w', 'delta_attn_wq': 'delta_w', 'delta_attn_wo': 'delta_w', 'delta_final_norm': 'delta_w', 'new_m_mix_norm': 'new_m', 'new_m_mlp_norm': 'new_m', 'new_m_mlp_w1': 'new_m', 'new_m_mlp_w2': 'new_m', 'new_m_ssm_log_dt': 'new_m', 'new_m_ssm_a_re': 'new_m', 'new_m_ssm_a_im': 'new_m', 'new_m_ssm_b_re': 'new_m', 'new_m_ssm_b_im': 'new_m', 'new_m_ssm_c_re': 'new_m', 'new_m_ssm_c_im': 'new_m', 'new_m_ssm_d': 'new_m', 'new_m_ssm_w_glu': 'new_m', 'new_m_kv_norm': 'new_m', 'new_m_w_kvf': 'new_m', 'new_m_b_f': 'new_m', 'new_m_attn_wq': 'new_m', 'new_m_attn_wo': 'new_m', 'new_m_final_norm': 'new_m', 'new_v_mix_norm': 'new_v', 'new_v_mlp_norm': 'new_v', 'new_v_mlp_w1': 'new_v', 'new_v_mlp_w2': 'new_v', 'new_v_ssm_log_dt': 'new_v', 'new_v_ssm_a_re': 'new_v', 'new_v_ssm_a_im': 'new_v', 'new_v_ssm_b_re': 'new_v', 'new_v_ssm_b_im': 'new_v', 'new_v_ssm_c_re': 'new_v', 'new_v_ssm_c_im': 'new_v', 'new_v_ssm_d': 'new_v', 'new_v_ssm_w_glu': 'new_v', 'new_v_kv_norm': 'new_v', 'new_v_w_kvf': 'new_v', 'new_v_b_f': 'new_v', 'new_v_attn_wq': 'new_v', 'new_v_attn_wo': 'new_v', 'new_v_final_norm': 'new_v'}


def _forward(args):
    return _fwd_reference(*[args[k] for k in FWD_PARAMS])


def _output_shape():
    def fwd():
        inp = _fwd_setup_inputs(0)
        return _fwd_reference(*[inp[k] for k in FWD_PARAMS])
    out = _jax.eval_shape(fwd)
    return out.shape, out.dtype

N_MICROBATCH = 1
ADAM_LR = 0.001
ADAM_B1 = 0.9
ADAM_B2 = 0.999
ADAM_EPS = 1e-08
ADAM_WD = 0.01
ADAM_STEP = 10
PER_EXAMPLE_BATCH_AXIS = {'x': 0, 'loss_target': 0}
SHARED_INPUTS = []
_WEIGHT_DTYPES = {'mix_norm': _jnp.float32, 'mlp_norm': _jnp.float32, 'mlp_w1': _jnp.float32, 'mlp_w2': _jnp.float32, 'ssm_log_dt': _jnp.float32, 'ssm_a_re': _jnp.float32, 'ssm_a_im': _jnp.float32, 'ssm_b_re': _jnp.float32, 'ssm_b_im': _jnp.float32, 'ssm_c_re': _jnp.float32, 'ssm_c_im': _jnp.float32, 'ssm_d': _jnp.float32, 'ssm_w_glu': _jnp.float32, 'kv_norm': _jnp.float32, 'w_kvf': _jnp.float32, 'b_f': _jnp.float32, 'attn_wq': _jnp.float32, 'attn_wo': _jnp.float32, 'final_norm': _jnp.float32}
MOMENT_SCALE = {'mix_norm': 2.626386e-01, 'mlp_norm': 1.825658e-01, 'mlp_w1': 9.428320e-02, 'mlp_w2': 8.254387e-01, 'ssm_log_dt': 3.200502e+00, 'ssm_a_re': 1.939843e-02, 'ssm_a_im': 1.112129e-02, 'ssm_b_re': 6.799249e-03, 'ssm_b_im': 7.183964e-03, 'ssm_c_re': 1.302519e-02, 'ssm_c_im': 1.325192e-02, 'ssm_d': 3.262037e-01, 'ssm_w_glu': 2.218674e-01, 'kv_norm': 5.357192e-01, 'w_kvf': 3.893045e-01, 'b_f': 8.186238e-01, 'attn_wq': 4.639435e-02, 'attn_wo': 3.895813e-01, 'final_norm': 6.583227e+01}


def _to_microbatches(a, axis):
    t = _jnp.moveaxis(a, axis, 0)
    t = t.reshape((N_MICROBATCH, t.shape[0] // N_MICROBATCH) + t.shape[1:])
    return _jnp.moveaxis(t, 1, axis + 1)


def setup_inputs(seed: int = 0) -> dict:
    inp = _fwd_setup_inputs(seed)
    key = _jax.random.fold_in(_jax.random.key(seed), 7919)
    shape, _ = _output_shape()
    out = dict(inp)
    out["loss_target"] = _jax.random.normal(_jax.random.fold_in(key, 0), shape, _jnp.float32)
    for i, name in enumerate(TWIN_WEIGHTS):
        w = inp[name].astype(_jnp.float32)
        if MOMENT_SCALE is None:
            s = _jnp.sqrt(_jnp.mean(_jnp.square(w)) + 1e-30)
        else:
            s = MOMENT_SCALE[name]
        km, kv = _jax.random.split(_jax.random.fold_in(key, i + 1))
        out[name] = w
        out["m_" + name] = s * _jax.random.normal(km, w.shape, _jnp.float32)
        out["v_" + name] = (s * s) * _jax.random.uniform(kv, w.shape, _jnp.float32, 0.5, 1.5)
    if N_MICROBATCH > 1:
        for name, axis in PER_EXAMPLE_BATCH_AXIS.items():
            out[name] = _to_microbatches(out[name], axis)
    return {'x': out['x'], 'mix_norm': out['mix_norm'], 'mlp_norm': out['mlp_norm'], 'mlp_w1': out['mlp_w1'], 'mlp_w2': out['mlp_w2'], 'ssm_log_dt': out['ssm_log_dt'], 'ssm_a_re': out['ssm_a_re'], 'ssm_a_im': out['ssm_a_im'], 'ssm_b_re': out['ssm_b_re'], 'ssm_b_im': out['ssm_b_im'], 'ssm_c_re': out['ssm_c_re'], 'ssm_c_im': out['ssm_c_im'], 'ssm_d': out['ssm_d'], 'ssm_w_glu': out['ssm_w_glu'], 'kv_norm': out['kv_norm'], 'w_kvf': out['w_kvf'], 'b_f': out['b_f'], 'attn_wq': out['attn_wq'], 'attn_wo': out['attn_wo'], 'final_norm': out['final_norm'], 'loss_target': out['loss_target'], 'm_mix_norm': out['m_mix_norm'], 'm_mlp_norm': out['m_mlp_norm'], 'm_mlp_w1': out['m_mlp_w1'], 'm_mlp_w2': out['m_mlp_w2'], 'm_ssm_log_dt': out['m_ssm_log_dt'], 'm_ssm_a_re': out['m_ssm_a_re'], 'm_ssm_a_im': out['m_ssm_a_im'], 'm_ssm_b_re': out['m_ssm_b_re'], 'm_ssm_b_im': out['m_ssm_b_im'], 'm_ssm_c_re': out['m_ssm_c_re'], 'm_ssm_c_im': out['m_ssm_c_im'], 'm_ssm_d': out['m_ssm_d'], 'm_ssm_w_glu': out['m_ssm_w_glu'], 'm_kv_norm': out['m_kv_norm'], 'm_w_kvf': out['m_w_kvf'], 'm_b_f': out['m_b_f'], 'm_attn_wq': out['m_attn_wq'], 'm_attn_wo': out['m_attn_wo'], 'm_final_norm': out['m_final_norm'], 'v_mix_norm': out['v_mix_norm'], 'v_mlp_norm': out['v_mlp_norm'], 'v_mlp_w1': out['v_mlp_w1'], 'v_mlp_w2': out['v_mlp_w2'], 'v_ssm_log_dt': out['v_ssm_log_dt'], 'v_ssm_a_re': out['v_ssm_a_re'], 'v_ssm_a_im': out['v_ssm_a_im'], 'v_ssm_b_re': out['v_ssm_b_re'], 'v_ssm_b_im': out['v_ssm_b_im'], 'v_ssm_c_re': out['v_ssm_c_re'], 'v_ssm_c_im': out['v_ssm_c_im'], 'v_ssm_d': out['v_ssm_d'], 'v_ssm_w_glu': out['v_ssm_w_glu'], 'v_kv_norm': out['v_kv_norm'], 'v_w_kvf': out['v_w_kvf'], 'v_b_f': out['v_b_f'], 'v_attn_wq': out['v_attn_wq'], 'v_attn_wo': out['v_attn_wo'], 'v_final_norm': out['v_final_norm']}


def _loss(weights, diff, rest, loss_target):
    with _jax.named_scope("forward"):
        args = {**rest, TWIN_DIFF_INPUT: diff, **{k: w.astype(_WEIGHT_DTYPES[k]) for k, w in weights.items()}}
        y = _forward(args)
    with _jax.named_scope("loss_head"):
        err = _jnp.square(y.astype(_jnp.float32) - loss_target)
        return 0.5 * _jnp.sum(_jnp.mean(err, axis=-1)) if err.ndim else 0.5 * err


def _adamw(w, g, m, v):
    m = ADAM_B1 * m + (1.0 - ADAM_B1) * g
    v = ADAM_B2 * v + (1.0 - ADAM_B2) * _jnp.square(g)
    m_hat = m / (1.0 - ADAM_B1 ** ADAM_STEP)
    v_hat = v / (1.0 - ADAM_B2 ** ADAM_STEP)
    delta = -ADAM_LR * (m_hat / (_jnp.sqrt(v_hat) + ADAM_EPS) + ADAM_WD * w)
    return delta, m, v


def reference(x, mix_norm, mlp_norm, mlp_w1, mlp_w2, ssm_log_dt, ssm_a_re, ssm_a_im, ssm_b_re, ssm_b_im, ssm_c_re, ssm_c_im, ssm_d, ssm_w_glu, kv_norm, w_kvf, b_f, attn_wq, attn_wo, final_norm, loss_target, m_mix_norm, m_mlp_norm, m_mlp_w1, m_mlp_w2, m_ssm_log_dt, m_ssm_a_re, m_ssm_a_im, m_ssm_b_re, m_ssm_b_im, m_ssm_c_re, m_ssm_c_im, m_ssm_d, m_ssm_w_glu, m_kv_norm, m_w_kvf, m_b_f, m_attn_wq, m_attn_wo, m_final_norm, v_mix_norm, v_mlp_norm, v_mlp_w1, v_mlp_w2, v_ssm_log_dt, v_ssm_a_re, v_ssm_a_im, v_ssm_b_re, v_ssm_b_im, v_ssm_c_re, v_ssm_c_im, v_ssm_d, v_ssm_w_glu, v_kv_norm, v_w_kvf, v_b_f, v_attn_wq, v_attn_wo, v_final_norm):
    given = dict(x=x, mix_norm=mix_norm, mlp_norm=mlp_norm, mlp_w1=mlp_w1, mlp_w2=mlp_w2, ssm_log_dt=ssm_log_dt, ssm_a_re=ssm_a_re, ssm_a_im=ssm_a_im, ssm_b_re=ssm_b_re, ssm_b_im=ssm_b_im, ssm_c_re=ssm_c_re, ssm_c_im=ssm_c_im, ssm_d=ssm_d, ssm_w_glu=ssm_w_glu, kv_norm=kv_norm, w_kvf=w_kvf, b_f=b_f, attn_wq=attn_wq, attn_wo=attn_wo, final_norm=final_norm, loss_target=loss_target, m_mix_norm=m_mix_norm, m_mlp_norm=m_mlp_norm, m_mlp_w1=m_mlp_w1, m_mlp_w2=m_mlp_w2, m_ssm_log_dt=m_ssm_log_dt, m_ssm_a_re=m_ssm_a_re, m_ssm_a_im=m_ssm_a_im, m_ssm_b_re=m_ssm_b_re, m_ssm_b_im=m_ssm_b_im, m_ssm_c_re=m_ssm_c_re, m_ssm_c_im=m_ssm_c_im, m_ssm_d=m_ssm_d, m_ssm_w_glu=m_ssm_w_glu, m_kv_norm=m_kv_norm, m_w_kvf=m_w_kvf, m_b_f=m_b_f, m_attn_wq=m_attn_wq, m_attn_wo=m_attn_wo, m_final_norm=m_final_norm, v_mix_norm=v_mix_norm, v_mlp_norm=v_mlp_norm, v_mlp_w1=v_mlp_w1, v_mlp_w2=v_mlp_w2, v_ssm_log_dt=v_ssm_log_dt, v_ssm_a_re=v_ssm_a_re, v_ssm_a_im=v_ssm_a_im, v_ssm_b_re=v_ssm_b_re, v_ssm_b_im=v_ssm_b_im, v_ssm_c_re=v_ssm_c_re, v_ssm_c_im=v_ssm_c_im, v_ssm_d=v_ssm_d, v_ssm_w_glu=v_ssm_w_glu, v_kv_norm=v_kv_norm, v_w_kvf=v_w_kvf, v_b_f=v_b_f, v_attn_wq=v_attn_wq, v_attn_wo=v_attn_wo, v_final_norm=v_final_norm)
    weights = {n: given[n] for n in TWIN_WEIGHTS}
    shared = {n: given[n] for n in SHARED_INPUTS}
    per_example = {n: given[n] for n in ['x']}
    grad_fn = _jax.value_and_grad(_loss, argnums=(0, 1))

    def one_microbatch(ex, loss_target):
        ex = dict(ex)
        diff = ex.pop(TWIN_DIFF_INPUT)
        return grad_fn(weights, diff, {**shared, **ex}, loss_target)

    if N_MICROBATCH == 1:
        loss, (grad_w, grad_x) = one_microbatch(per_example, given["loss_target"])
    else:
        def body(carry, xs):
            loss_sum, grad_sum = carry
            l_k, (gw_k, gx_k) = one_microbatch(xs[0], xs[1])
            with _jax.named_scope("update"):
                return (loss_sum + l_k, _jax.tree.map(_jnp.add, grad_sum, gw_k)), gx_k

        init = (_jnp.zeros((), _jnp.float32), _jax.tree.map(_jnp.zeros_like, weights))
        (loss, grad_w), grad_x = _jax.lax.scan(body, init, (per_example, given["loss_target"]))
    with _jax.named_scope("update"):
        delta_w, new_m, new_v = {}, {}, {}
        for n in TWIN_WEIGHTS:
            delta_w[n], new_m[n], new_v[n] = _adamw(weights[n], grad_w[n], given["m_" + n], given["v_" + n])
    return (loss, grad_x, *[grad_w[n] for n in TWIN_WEIGHTS], *[delta_w[n] for n in TWIN_WEIGHTS],
            *[new_m[n] for n in TWIN_WEIGHTS], *[new_v[n] for n in TWIN_WEIGHTS])
```

```python
import functools
import math

import jax
import jax.numpy as jnp
from jax import lax
from jax.experimental import pallas as pl
from jax.experimental.pallas import tpu as pltpu

F32 = jnp.float32
BF16 = jnp.bfloat16

RMS_EPS = 1e-6
SSM_GROUP = 16
SSM_STATE = 64
HEAD_DIM = 64
HEAD_PAIR = 2 * HEAD_DIM
LANES = 128
SUBLANES = 8
N_CHIPS = 4
ADAM_LR = 0.001
ADAM_B1 = 0.9
ADAM_B2 = 0.999
ADAM_EPS = 1e-08
ADAM_WD = 0.01
ADAM_STEP = 10
GELU_C = math.sqrt(2.0 / math.pi)
GELU_A = 0.044715
NEG = -1e30
VMEM_LIMIT = 56 * 1024 * 1024
MESH = pl.DeviceIdType.MESH

NT_DIMS = (((1,), (1,)), ((), ()))
TN_DIMS = (((0,), (0,)), ((), ()))


def _cp(*sem):
    return pltpu.CompilerParams(dimension_semantics=sem if sem else None, vmem_limit_bytes=VMEM_LIMIT)


def _zero_idx(nd, *_):
    return (0,) * nd


def _tile(n, t):
    if n <= t:
        return n
    for cand in range(t - t % SUBLANES, 0, -SUBLANES):
        if n % cand == 0:
            return cand
    raise ValueError((n, t))


def _rms_fwd(h, g):
    r = lax.rsqrt(jnp.mean(h * h, axis=-1, keepdims=True) + RMS_EPS)
    hhat = h * r
    return hhat * g, hhat, r


def _rms_bwd(du, hhat, r, g):
    dhh = du * g
    dh = r * (dhh - hhat * jnp.mean(dhh * hhat, axis=-1, keepdims=True))
    return dh, du * hhat


def _sigmoid(x):
    return 1.0 / (1.0 + jnp.exp(-x))


def _gelu(x):
    t = jnp.tanh(GELU_C * (x + GELU_A * x * x * x))
    return 0.5 * x * (1.0 + t)


def _gelu_grad(x):
    t = jnp.tanh(GELU_C * (x + GELU_A * x * x * x))
    return 0.5 * (1.0 + t) + 0.5 * x * (1.0 - t * t) * GELU_C * (1.0 + 3.0 * GELU_A * x * x)


def _row_fold(x):
    tm, w = x.shape
    return jnp.sum(x.reshape(tm // SUBLANES, SUBLANES, w), axis=0)


def _split3(x):
    hi = x.astype(BF16)
    r1 = x - hi.astype(F32)
    mid = r1.astype(BF16)
    lo = (r1 - mid.astype(F32)).astype(BF16)
    return hi, mid, lo


def _exact_dot(ones_mat, x):
    hi, mid, lo = _split3(x)
    d = functools.partial(jnp.dot, preferred_element_type=F32)
    return d(ones_mat, hi) + d(ones_mat, mid) + d(ones_mat, lo)


def _rows_call(body, name, tm, row_ins, const_ins, row_outs, acc_outs=(), scratch=(), reverse=False):
    n = row_ins[0].shape[0]
    nb = n // tm
    if reverse:
        ridx = lambda i: (nb - 1 - i, 0)
    else:
        ridx = lambda i: (i, 0)
    in_specs = [pl.BlockSpec((tm, a.shape[1]), ridx) for a in row_ins]
    in_specs += [pl.BlockSpec(a.shape, functools.partial(_zero_idx, a.ndim), pipeline_mode=pl.Buffered(1))
                 for a in const_ins]
    out_shape = [jax.ShapeDtypeStruct((n, w), dt) for (w, dt) in row_outs]
    out_shape += [jax.ShapeDtypeStruct(s, dt) for (s, dt) in acc_outs]
    out_specs = [pl.BlockSpec((tm, w), ridx) for (w, dt) in row_outs]
    out_specs += [pl.BlockSpec(s, functools.partial(_zero_idx, len(s))) for (s, dt) in acc_outs]
    return pl.pallas_call(
        body, name=name, grid=(nb,), in_specs=in_specs, out_specs=out_specs, out_shape=out_shape,
        scratch_shapes=list(scratch), compiler_params=_cp("arbitrary"),
    )(*row_ins, *const_ins)


def _norm_fwd(h, g, name):
    n, d = h.shape
    tm = _tile(n, 512)

    def body(h_ref, g_ref, u_ref):
        u_ref[...] = _rms_fwd(h_ref[...], g_ref[...])[0]

    return _rows_call(body, name, tm, [h], [g], [(d, F32)])[0]


def _norm_bwd_add(dh, du, h, g, name):
    n, d = h.shape
    tm = _tile(n, 512)
    nb = n // tm

    def body(dh_ref, du_ref, h_ref, g_ref, o_ref, dg_ref, acc):
        i = pl.program_id(0)

        @pl.when(i == 0)
        def _():
            acc[...] = jnp.zeros_like(acc)

        gain = g_ref[...]
        _, hhat, r = _rms_fwd(h_ref[...], gain)
        dhn, dgr = _rms_bwd(du_ref[...], hhat, r, gain)
        o_ref[...] = dh_ref[...] + dhn
        acc[...] += _row_fold(dgr)

        @pl.when(i == nb - 1)
        def _():
            dg_ref[...] = jnp.sum(acc[...], axis=0, keepdims=True)

    return _rows_call(body, name, tm, [dh, du, h], [g], [(d, F32)], [((1, d), F32)],
                      [pltpu.VMEM((SUBLANES, d), F32)])


def _mlp_fwd(h, g, w1, w2, name):
    n, d = h.shape
    ff = w1.shape[1]
    tm = _tile(n, 256)
    fc = _tile(ff, 1024)

    def body(h_ref, g_ref, w1_ref, w2_ref, o_ref, ap_ref):
        hin = h_ref[...]
        hb = _rms_fwd(hin, g_ref[...])[0].astype(BF16)
        acc = hin
        for c in range(ff // fc):
            cs = slice(c * fc, (c + 1) * fc)
            ap = jnp.dot(hb, w1_ref[:, cs], preferred_element_type=F32)
            ap_ref[:, cs] = ap.astype(BF16)
            rl = jnp.maximum(ap, 0.0)
            acc = acc + jnp.dot((rl * rl).astype(BF16), w2_ref[cs, :], preferred_element_type=F32)
        o_ref[...] = acc

    return _rows_call(body, name, tm, [h], [g, w1, w2], [(d, F32), (ff, BF16)])


def _mlp_bwd(dh, h, ap, g, w1, w2, name):
    n, d = h.shape
    ff = w1.shape[1]
    tm = _tile(n, 256)
    nb = n // tm
    fc = _tile(ff, 1024)

    def body(dh_ref, h_ref, ap_ref, g_ref, w1_ref, w2_ref, o_ref, hm_ref, a_ref, dap_ref, dg_ref, acc):
        i = pl.program_id(0)

        @pl.when(i == 0)
        def _():
            acc[...] = jnp.zeros_like(acc)

        gain = g_ref[...]
        dhv = dh_ref[...]
        hm, hhat, r = _rms_fwd(h_ref[...], gain)
        hm_ref[...] = hm.astype(BF16)
        dhb = dhv.astype(BF16)
        dhm = jnp.zeros((tm, d), F32)
        for c in range(ff // fc):
            cs = slice(c * fc, (c + 1) * fc)
            rl = jnp.maximum(ap_ref[:, cs].astype(F32), 0.0)
            a_ref[:, cs] = (rl * rl).astype(BF16)
            da = lax.dot_general(dhb, w2_ref[cs, :], NT_DIMS, preferred_element_type=F32)
            dap = (da * (2.0 * rl)).astype(BF16)
            dap_ref[:, cs] = dap
            dhm = dhm + lax.dot_general(dap, w1_ref[:, cs], NT_DIMS, preferred_element_type=F32)
        dhn, dgr = _rms_bwd(dhm, hhat, r, gain)
        o_ref[...] = dhv + dhn
        acc[...] += _row_fold(dgr)

        @pl.when(i == nb - 1)
        def _():
            dg_ref[...] = jnp.sum(acc[...], axis=0, keepdims=True)

    return _rows_call(body, name, tm, [dh, h, ap], [g, w1, w2],
                      [(d, F32), (d, BF16), (ff, BF16), (ff, BF16)], [((1, d), F32)],
                      [pltpu.VMEM((SUBLANES, d), F32)])


def _s5_post_fwd(h, y, w_glu, name):
    n, d = h.shape
    tm = _tile(n, 512)

    def body(h_ref, y_ref, w_ref, o_ref, zw_ref):
        z = _gelu(y_ref[...]).astype(BF16)
        zw = jnp.dot(z, w_ref[...], preferred_element_type=F32)
        zw_ref[...] = zw.astype(BF16)
        o_ref[...] = h_ref[...] + zw[:, :d] * _sigmoid(zw[:, d:])

    return _rows_call(body, name, tm, [h, y], [w_glu], [(d, F32), (2 * d, BF16)])


def _s5_post_bwd(dh, y, zw, w_glu, name):
    n, d = dh.shape
    tm = _tile(n, 512)

    def body(dh_ref, y_ref, zw_ref, w_ref, dy_ref, z_ref, dzw_ref):
        dhv = dh_ref[...]
        yv = y_ref[...]
        val = zw_ref[:, :d].astype(F32)
        sg = _sigmoid(zw_ref[:, d:].astype(F32))
        dzw = jnp.concatenate([dhv * sg, dhv * val * sg * (1.0 - sg)], axis=1).astype(BF16)
        dzw_ref[...] = dzw
        dz = lax.dot_general(dzw, w_ref[...], NT_DIMS, preferred_element_type=F32)
        dy_ref[...] = dz * _gelu_grad(yv)
        z_ref[...] = _gelu(yv).astype(BF16)

    return _rows_call(body, name, tm, [dh, y, zw], [w_glu], [(d, F32), (d, BF16), (2 * d, BF16)])


def _q_fwd(h, g, wq, name):
    n, d = h.shape
    tm = _tile(n, 512)
    scale = HEAD_DIM ** -0.5

    def body(h_ref, g_ref, w_ref, q_ref):
        hb = _rms_fwd(h_ref[...], g_ref[...])[0].astype(BF16)
        q_ref[...] = (jnp.dot(hb, w_ref[...], preferred_element_type=F32) * scale).astype(BF16)

    return _rows_call(body, name, tm, [h], [g, wq], [(wq.shape[1], BF16)])[0]


def _q_bwd(dh, h, dq, g, wq, name):
    n, d = h.shape
    tm = _tile(n, 512)
    nb = n // tm
    scale = HEAD_DIM ** -0.5

    def body(dh_ref, h_ref, dq_ref, g_ref, w_ref, o_ref, hn_ref, dqs_ref, dg_ref, acc):
        i = pl.program_id(0)

        @pl.when(i == 0)
        def _():
            acc[...] = jnp.zeros_like(acc)

        gain = g_ref[...]
        hn, hhat, r = _rms_fwd(h_ref[...], gain)
        hn_ref[...] = hn.astype(BF16)
        dqs = (dq_ref[...] * scale).astype(BF16)
        dqs_ref[...] = dqs
        dhn = lax.dot_general(dqs, w_ref[...], NT_DIMS, preferred_element_type=F32)
        dhi, dgr = _rms_bwd(dhn, hhat, r, gain)
        o_ref[...] = dh_ref[...] + dhi
        acc[...] += _row_fold(dgr)

        @pl.when(i == nb - 1)
        def _():
            dg_ref[...] = jnp.sum(acc[...], axis=0, keepdims=True)

    return _rows_call(body, name, tm, [dh, h, dq], [g, wq], [(d, F32), (d, BF16), (wq.shape[1], BF16)],
                      [((1, d), F32)], [pltpu.VMEM((SUBLANES, d), F32)])


def _o_fwd(h, o, wo, name):
    n, d = h.shape
    tm = _tile(n, 512)

    def body(h_ref, o_ref, w_ref, out_ref):
        out_ref[...] = h_ref[...] + jnp.dot(o_ref[...], w_ref[...], preferred_element_type=F32)

    return _rows_call(body, name, tm, [h, o], [wo], [(d, F32)])[0]


def _o_bwd(dh, o, wo, head_ones, name):
    n, d = dh.shape
    tm = _tile(n, 512)

    def body(dh_ref, o_ref, w_ref, e_ref, do_ref, dl_ref):
        do = lax.dot_general(dh_ref[...].astype(BF16), w_ref[...], NT_DIMS, preferred_element_type=F32).astype(BF16)
        do_ref[...] = do
        dl_ref[...] = _exact_dot_rhs(do.astype(F32) * o_ref[...].astype(F32), e_ref[...])

    return _rows_call(body, name, tm, [dh, o], [wo, head_ones], [(wo.shape[0], BF16), (wo.shape[0], F32)])


def _exact_dot_rhs(x, ones_mat):
    hi, mid, lo = _split3(x)
    d = functools.partial(jnp.dot, preferred_element_type=F32)
    return d(hi, ones_mat) + d(mid, ones_mat) + d(lo, ones_mat)


def _kvf_fwd(h, g, wk, wv, wf, bf, name):
    n, d = h.shape
    tm = _tile(n, 512)

    def body(h_ref, g_ref, wk_ref, wv_ref, wf_ref, bf_ref, k_ref, v_ref, fl_ref, cum_ref, carry):
        i = pl.program_id(0)

        @pl.when(i == 0)
        def _():
            carry[...] = jnp.zeros_like(carry)

        hb = _rms_fwd(h_ref[...], g_ref[...])[0].astype(BF16)
        k_ref[...] = jnp.dot(hb, wk_ref[...], preferred_element_type=F32).astype(BF16)
        v_ref[...] = jnp.dot(hb, wv_ref[...], preferred_element_type=F32).astype(BF16)
        fl = jnp.dot(hb, wf_ref[...], preferred_element_type=F32) + bf_ref[...]
        fl_ref[...] = fl
        logf = jnp.minimum(fl, 0.0) - jnp.log(1.0 + jnp.exp(-jnp.abs(fl)))
        rows = lax.broadcasted_iota(jnp.int32, (tm, tm), 0)
        cols = lax.broadcasted_iota(jnp.int32, (tm, tm), 1)
        lower = (rows >= cols).astype(BF16)
        cum = _exact_dot(lower, logf) + carry[0:1, :]
        cum_ref[...] = cum
        carry[...] = jnp.broadcast_to(cum[tm - 1:tm, :], carry.shape)

    return _rows_call(body, name, tm, [h], [g, wk, wv, wf, bf],
                      [(wk.shape[1], BF16), (wv.shape[1], BF16), (LANES, F32), (LANES, F32)],
                      scratch=[pltpu.VMEM((SUBLANES, LANES), F32)])


def _kvf_bwd(dh, h, dk1, dk2, dv1, dv2, dcum, fl, g, wk, wv, wf, name):
    n, d = h.shape
    tm = _tile(n, 512)
    nb = n // tm

    def body(dh_ref, h_ref, dk1_ref, dk2_ref, dv1_ref, dv2_ref, dc_ref, fl_ref, g_ref, wk_ref, wv_ref, wf_ref,
             o_ref, hk_ref, dk_ref, dv_ref, df_ref, dg_ref, db_ref, acc, bacc, carry):
        i = pl.program_id(0)

        @pl.when(i == 0)
        def _():
            acc[...] = jnp.zeros_like(acc)
            bacc[...] = jnp.zeros_like(bacc)
            carry[...] = jnp.zeros_like(carry)

        rows = lax.broadcasted_iota(jnp.int32, (tm, tm), 0)
        cols = lax.broadcasted_iota(jnp.int32, (tm, tm), 1)
        upper = (rows <= cols).astype(BF16)
        dlogf = _exact_dot(upper, dc_ref[...]) + carry[0:1, :]
        carry[...] = jnp.broadcast_to(dlogf[0:1, :], carry.shape)
        df = dlogf / (1.0 + jnp.exp(fl_ref[...]))
        dfb = df.astype(BF16)
        df_ref[...] = dfb
        bacc[...] += _row_fold(df)
        dkb = (dk1_ref[...] + dk2_ref[...]).astype(BF16)
        dvb = (dv1_ref[...] + dv2_ref[...]).astype(BF16)
        dk_ref[...] = dkb
        dv_ref[...] = dvb
        gain = g_ref[...]
        hk, hhat, r = _rms_fwd(h_ref[...], gain)
        hk_ref[...] = hk.astype(BF16)
        dhk = lax.dot_general(dkb, wk_ref[...], NT_DIMS, preferred_element_type=F32)
        dhk = dhk + lax.dot_general(dvb, wv_ref[...], NT_DIMS, preferred_element_type=F32)
        dhk = dhk + lax.dot_general(dfb, wf_ref[...], NT_DIMS, preferred_element_type=F32)
        dhi, dgr = _rms_bwd(dhk, hhat, r, gain)
        o_ref[...] = dh_ref[...] + dhi
        acc[...] += _row_fold(dgr)

        @pl.when(i == nb - 1)
        def _():
            dg_ref[...] = jnp.sum(acc[...], axis=0, keepdims=True)
            db_ref[...] = jnp.sum(bacc[...], axis=0, keepdims=True)

    return _rows_call(body, name, tm, [dh, h, dk1, dk2, dv1, dv2, dcum, fl], [g, wk, wv, wf],
                      [(d, F32), (d, BF16), (wk.shape[1], BF16), (wv.shape[1], BF16), (LANES, BF16)],
                      [((1, d), F32), ((1, LANES), F32)],
                      [pltpu.VMEM((SUBLANES, d), F32), pltpu.VMEM((SUBLANES, LANES), F32),
                       pltpu.VMEM((SUBLANES, LANES), F32)], reverse=True)


def _loss_head(h, target, g, name):
    n, d = h.shape
    tm = _tile(n, 512)
    nb = n // tm

    def body(h_ref, t_ref, g_ref, dh_ref, loss_ref, dg_ref, lacc, gacc):
        i = pl.program_id(0)

        @pl.when(i == 0)
        def _():
            lacc[...] = jnp.zeros_like(lacc)
            gacc[...] = jnp.zeros_like(gacc)

        gain = g_ref[...]
        yv, hhat, r = _rms_fwd(h_ref[...], gain)
        e = yv - t_ref[...]
        lacc[...] += _row_fold(e * e)
        dhv, dgr = _rms_bwd(e * (1.0 / d), hhat, r, gain)
        dh_ref[...] = dhv
        gacc[...] += _row_fold(dgr)

        @pl.when(i == nb - 1)
        def _():
            loss_ref[...] = jnp.full((1, LANES), jnp.sum(lacc[...]) * (0.5 / d), F32)
            dg_ref[...] = jnp.sum(gacc[...], axis=0, keepdims=True)

    return _rows_call(body, name, tm, [h, target], [g], [(d, F32)], [((1, LANES), F32), ((1, d), F32)],
                      [pltpu.VMEM((SUBLANES, d), F32), pltpu.VMEM((SUBLANES, d), F32)])


def _matmul_tn(a, b, name, out_dtype=BF16):
    l, m = a.shape
    n = b.shape[1]
    tl = _tile(l, 1024)
    tmm = _tile(m, 512)
    tn = _tile(n, 1024)
    nl = l // tl

    def body(a_ref, b_ref, o_ref, acc):
        k = pl.program_id(2)

        @pl.when(k == 0)
        def _():
            acc[...] = jnp.zeros_like(acc)

        acc[...] += lax.dot_general(a_ref[...].astype(BF16), b_ref[...].astype(BF16), TN_DIMS,
                                    preferred_element_type=F32)

        @pl.when(k == nl - 1)
        def _():
            o_ref[...] = acc[...].astype(out_dtype)

    return pl.pallas_call(
        body, name=name, grid=(m // tmm, n // tn, nl),
        in_specs=[pl.BlockSpec((tl, tmm), lambda i, j, k: (k, i)), pl.BlockSpec((tl, tn), lambda i, j, k: (k, j))],
        out_specs=pl.BlockSpec((tmm, tn), lambda i, j, k: (i, j)),
        out_shape=jax.ShapeDtypeStruct((m, n), out_dtype),
        scratch_shapes=[pltpu.VMEM((tmm, tn), F32)],
        compiler_params=_cp("parallel", "parallel", "arbitrary"),
    )(a, b)


def _scan_fwd(u, bblk, cblk, tabs, dskip, name):
    l, d = u.shape
    nj, gb, n2 = bblk.shape
    n = n2 // 2
    tm = _tile(l, 512)
    nb = l // tm

    def body(u_ref, b_ref, c_ref, t_ref, d_ref, st_ref, y_ref, carry):
        i = pl.program_id(1)

        @pl.when(i == 0)
        def _():
            carry[...] = jnp.zeros_like(carry)

        uf = u_ref[...]
        st_ref[...] = jnp.dot(uf.astype(BF16), b_ref[0], preferred_element_type=F32)

        def step(rb, c):
            cr, ci = c
            rows = pl.ds(pl.multiple_of(rb * SUBLANES, SUBLANES), SUBLANES)
            xr = st_ref[rows, 0:n]
            xi = st_ref[rows, n:n2]
            for lvl, sh in enumerate((1, 2, 4)):
                ar = t_ref[0, 2 * lvl]
                ai = t_ref[0, 2 * lvl + 1]
                sr = pltpu.roll(xr, sh, 0)
                si = pltpu.roll(xi, sh, 0)
                xr, xi = xr + ar * sr - ai * si, xi + ar * si + ai * sr
            lr = t_ref[0, 6]
            li = t_ref[0, 7]
            xr, xi = xr + lr * cr - li * ci, xi + lr * ci + li * cr
            st_ref[rows, 0:n] = xr
            st_ref[rows, n:n2] = xi
            return (jnp.broadcast_to(xr[SUBLANES - 1:SUBLANES, :], (SUBLANES, n)),
                    jnp.broadcast_to(xi[SUBLANES - 1:SUBLANES, :], (SUBLANES, n)))

        cr, ci = lax.fori_loop(0, tm // SUBLANES, step, (carry[:, 0:n], carry[:, n:n2]))
        carry[:, 0:n] = cr
        carry[:, n:n2] = ci
        y_ref[...] = jnp.dot(st_ref[...].astype(BF16), c_ref[0], preferred_element_type=F32) + d_ref[...] * uf

    return pl.pallas_call(
        body, name=name, grid=(nj, nb),
        in_specs=[pl.BlockSpec((tm, gb), lambda j, i: (i, j)),
                  pl.BlockSpec((1, gb, n2), lambda j, i: (j, 0, 0)),
                  pl.BlockSpec((1, n2, gb), lambda j, i: (j, 0, 0)),
                  pl.BlockSpec((1, 8, SUBLANES, n), lambda j, i: (j, 0, 0, 0)),
                  pl.BlockSpec((1, gb), lambda j, i: (0, j))],
        out_specs=[pl.BlockSpec((tm, n2), lambda j, i: (i, j)), pl.BlockSpec((tm, gb), lambda j, i: (i, j))],
        out_shape=[jax.ShapeDtypeStruct((l, nj * n2), F32), jax.ShapeDtypeStruct((l, d), F32)],
        scratch_shapes=[pltpu.VMEM((SUBLANES, n2), F32)],
        compiler_params=_cp("parallel", "arbitrary"),
    )(u, bblk, cblk, tabs, dskip)


def _scan_bwd(dy, u, states, bblk_t, cblk_t, tabs, dskip, name):
    l, d = u.shape
    nj, n2, gb = bblk_t.shape
    n = n2 // 2
    tm = _tile(l, 512)
    nb = l // tm
    nr = tm // SUBLANES

    def body(dy_ref, u_ref, st_ref, prev_ref, bt_ref, ct_ref, t_ref, d_ref,
             du_ref, glam_ref, gd_ref, gb_ref, gc_ref, gx, carry):
        i = pl.program_id(1)
        ib = nb - 1 - i

        @pl.when(i == 0)
        def _():
            carry[...] = jnp.zeros_like(carry)
            glam_ref[...] = jnp.zeros_like(glam_ref)
            gd_ref[...] = jnp.zeros_like(gd_ref)
            gb_ref[...] = jnp.zeros_like(gb_ref)
            gc_ref[...] = jnp.zeros_like(gc_ref)

        dyv = dy_ref[...]
        uv = u_ref[...]
        dyb = dyv.astype(BF16)
        gx[...] = jnp.dot(dyb, ct_ref[0], preferred_element_type=F32)
        last_row = lax.broadcasted_iota(jnp.int32, (SUBLANES, n), 0) == SUBLANES - 1

        def block(rows, xp_r, xp_i, c):
            cr, ci = c
            gr = gx[rows, 0:n]
            gi = gx[rows, n:n2]
            for lvl, sh in enumerate((1, 2, 4)):
                ar = t_ref[0, 2 * lvl]
                ai = t_ref[0, 2 * lvl + 1]
                sr = pltpu.roll(gr, SUBLANES - sh, 0)
                si = pltpu.roll(gi, SUBLANES - sh, 0)
                gr, gi = gr + ar * sr - ai * si, gi + ar * si + ai * sr
            lr = t_ref[0, 6]
            li = t_ref[0, 7]
            gr, gi = gr + lr * cr - li * ci, gi + lr * ci + li * cr
            gx[rows, 0:n] = gr
            gx[rows, n:n2] = gi
            xs_r = pltpu.roll(jnp.where(last_row, xp_r, st_ref[rows, 0:n]), 1, 0)
            xs_i = pltpu.roll(jnp.where(last_row, xp_i, st_ref[rows, n:n2]), 1, 0)
            glam_ref[0, :, 0:n] += gr * xs_r + gi * xs_i
            glam_ref[0, :, n:n2] += gi * xs_r - gr * xs_i
            return (jnp.broadcast_to(gr[0:1, :], (SUBLANES, n)), jnp.broadcast_to(gi[0:1, :], (SUBLANES, n)))

        def step(k, c):
            rb = nr - 1 - k
            rows = pl.ds(pl.multiple_of(rb * SUBLANES, SUBLANES), SUBLANES)
            before = pl.ds(pl.multiple_of(rb * SUBLANES - SUBLANES, SUBLANES), SUBLANES)
            return block(rows, st_ref[before, 0:n], st_ref[before, n:n2], c)

        c = lax.fori_loop(0, nr - 1, step, (carry[:, 0:n], carry[:, n:n2]))
        live = (ib > 0).astype(F32)
        cr, ci = block(pl.ds(0, SUBLANES), prev_ref[:, 0:n] * live, prev_ref[:, n:n2] * live, c)
        carry[:, 0:n] = cr
        carry[:, n:n2] = ci

        gxb = gx[...].astype(BF16)
        du_ref[...] = jnp.dot(gxb, bt_ref[0], preferred_element_type=F32) + d_ref[...] * dyv
        gd_ref[0] += _row_fold(dyv * uv)
        gb_ref[0] += lax.dot_general(uv.astype(BF16), gxb, TN_DIMS, preferred_element_type=F32)
        gc_ref[0] += lax.dot_general(st_ref[...].astype(BF16), dyb, TN_DIMS, preferred_element_type=F32)

    rpb = tm // SUBLANES
    return pl.pallas_call(
        body, name=name, grid=(nj, nb),
        in_specs=[pl.BlockSpec((tm, gb), lambda j, i: (nb - 1 - i, j)),
                  pl.BlockSpec((tm, gb), lambda j, i: (nb - 1 - i, j)),
                  pl.BlockSpec((tm, n2), lambda j, i: (nb - 1 - i, j)),
                  pl.BlockSpec((SUBLANES, n2), lambda j, i: (jnp.maximum((nb - 1 - i) * rpb - 1, 0), j)),
                  pl.BlockSpec((1, n2, gb), lambda j, i: (j, 0, 0)),
                  pl.BlockSpec((1, gb, n2), lambda j, i: (j, 0, 0)),
                  pl.BlockSpec((1, 8, SUBLANES, n), lambda j, i: (j, 0, 0, 0)),
                  pl.BlockSpec((1, gb), lambda j, i: (0, j))],
        out_specs=[pl.BlockSpec((tm, gb), lambda j, i: (nb - 1 - i, j)),
                   pl.BlockSpec((1, SUBLANES, n2), lambda j, i: (j, 0, 0)),
                   pl.BlockSpec((1, SUBLANES, gb), lambda j, i: (j, 0, 0)),
                   pl.BlockSpec((1, gb, n2), lambda j, i: (j, 0, 0)),
                   pl.BlockSpec((1, n2, gb), lambda j, i: (j, 0, 0))],
        out_shape=[jax.ShapeDtypeStruct((l, d), F32),
                   jax.ShapeDtypeStruct((nj, SUBLANES, n2), F32),
                   jax.ShapeDtypeStruct((nj, SUBLANES, gb), F32),
                   jax.ShapeDtypeStruct((nj, gb, n2), F32),
                   jax.ShapeDtypeStruct((nj, n2, gb), F32)],
        scratch_shapes=[pltpu.VMEM((tm, n2), F32), pltpu.VMEM((SUBLANES, n2), F32)],
        compiler_params=_cp("parallel", "arbitrary"),
    )(dy, u, states, states, bblk_t, cblk_t, tabs, dskip)


def _flash_fwd(q, k, v, ft, name):
    l, d = q.shape
    npair = d // HEAD_PAIR
    tq = _tile(l, 512)
    tk = tq
    nq, nk = l // tq, l // tk

    def body(q_ref, k_ref, v_ref, f_ref, o_ref, lse_ref, m_sc, l_sc, acc_sc):
        i = pl.program_id(1)
        j = pl.program_id(2)

        @pl.when(j == 0)
        def _():
            m_sc[...] = jnp.full_like(m_sc, NEG)
            l_sc[...] = jnp.zeros_like(l_sc)
            acc_sc[...] = jnp.zeros_like(acc_sc)

        @pl.when(j <= i)
        def _():
            rows = i * tq + lax.broadcasted_iota(jnp.int32, (tq, tk), 0)
            cols = j * tk + lax.broadcasted_iota(jnp.int32, (tq, tk), 1)
            keep = rows >= cols
            for hh in range(2):
                sl = slice(hh * HEAD_DIM, (hh + 1) * HEAD_DIM)
                s = lax.dot_general(q_ref[:, sl], k_ref[:, sl], NT_DIMS, preferred_element_type=F32)
                s = jnp.where(keep, s - f_ref[0, hh:hh + 1, :], NEG)
                m_prev = m_sc[hh]
                m_new = jnp.maximum(m_prev, jnp.max(s, axis=-1, keepdims=True))
                alpha = jnp.exp(m_prev - m_new)
                p = jnp.exp(s - m_new)
                l_sc[hh] = alpha * l_sc[hh] + jnp.sum(p, axis=-1, keepdims=True)
                acc_sc[hh] = alpha * acc_sc[hh] + jnp.dot(p.astype(BF16), v_ref[:, sl], preferred_element_type=F32)
                m_sc[hh] = m_new

        @pl.when(j == nk - 1)
        def _():
            outs, lses = [], []
            for hh in range(2):
                lsum = l_sc[hh]
                outs.append(acc_sc[hh] / lsum)
                lses.append(jnp.broadcast_to(m_sc[hh] + jnp.log(lsum), (tq, HEAD_DIM)))
            o_ref[...] = jnp.concatenate(outs, axis=1).astype(BF16)
            lse_ref[...] = jnp.concatenate(lses, axis=1)

    return pl.pallas_call(
        body, name=name, grid=(npair, nq, nk),
        in_specs=[pl.BlockSpec((tq, HEAD_PAIR), lambda h, i, j: (i, h)),
                  pl.BlockSpec((tk, HEAD_PAIR), lambda h, i, j: (jnp.minimum(j, i), h)),
                  pl.BlockSpec((tk, HEAD_PAIR), lambda h, i, j: (jnp.minimum(j, i), h)),
                  pl.BlockSpec((1, 2, tk), lambda h, i, j: (h, 0, jnp.minimum(j, i)))],
        out_specs=[pl.BlockSpec((tq, HEAD_PAIR), lambda h, i, j: (i, h)),
                   pl.BlockSpec((tq, HEAD_PAIR), lambda h, i, j: (i, h))],
        out_shape=[jax.ShapeDtypeStruct((l, d), BF16), jax.ShapeDtypeStruct((l, d), F32)],
        scratch_shapes=[pltpu.VMEM((2, tq, 1), F32), pltpu.VMEM((2, tq, 1), F32), pltpu.VMEM((2, tq, HEAD_DIM), F32)],
        compiler_params=_cp("parallel", "parallel", "arbitrary"),
    )(q, k, v, ft)


def _flash_bwd(q, k, v, ft, do, lse, delta, name):
    l, d = q.shape
    npair = d // HEAD_PAIR
    tq = _tile(l, 512)
    tk = tq
    nq, nk = l // tq, l // tk

    def body(q_ref, k_ref, v_ref, f_ref, do_ref, lse_ref, dl_ref, dq_ref, dk_ref, dv_ref, df_ref, dfq_ref):
        j = pl.program_id(1)
        i = pl.program_id(2)

        @pl.when((j == 0) & (i == 0))
        def _():
            dq_ref[...] = jnp.zeros_like(dq_ref)
            dfq_ref[...] = jnp.zeros_like(dfq_ref)

        @pl.when(i == 0)
        def _():
            dk_ref[...] = jnp.zeros_like(dk_ref)
            dv_ref[...] = jnp.zeros_like(dv_ref)
            df_ref[...] = jnp.zeros_like(df_ref)

        @pl.when(i >= j)
        def _():
            rows = i * tq + lax.broadcasted_iota(jnp.int32, (tq, tk), 0)
            cols = j * tk + lax.broadcasted_iota(jnp.int32, (tq, tk), 1)
            keep = rows >= cols
            dqs, dks, dvs, dfs, dfqs = [], [], [], [], []
            for hh in range(2):
                sl = slice(hh * HEAD_DIM, (hh + 1) * HEAD_DIM)
                one = slice(hh * HEAD_DIM, hh * HEAD_DIM + 1)
                qh, kh, vh, doh = q_ref[:, sl], k_ref[:, sl], v_ref[:, sl], do_ref[:, sl]
                s = lax.dot_general(qh, kh, NT_DIMS, preferred_element_type=F32) - f_ref[0, hh:hh + 1, :]
                p = jnp.where(keep, jnp.exp(s - lse_ref[:, one]), 0.0)
                dp = lax.dot_general(doh, vh, NT_DIMS, preferred_element_type=F32)
                ds = p * (dp - dl_ref[:, one])
                pb = p.astype(BF16)
                dsb = ds.astype(BF16)
                dvs.append(lax.dot_general(pb, doh, TN_DIMS, preferred_element_type=F32))
                dks.append(lax.dot_general(dsb, qh, TN_DIMS, preferred_element_type=F32))
                dqs.append(jnp.dot(dsb, kh, preferred_element_type=F32))
                dfs.append(-jnp.sum(ds, axis=0, keepdims=True))
                dfqs.append(jnp.broadcast_to(jnp.sum(ds, axis=1, keepdims=True), (tq, HEAD_DIM)))
            q_rows = pl.ds(pl.multiple_of(i * tq, tq), tq)
            dv_ref[...] += jnp.concatenate(dvs, axis=1)
            dk_ref[...] += jnp.concatenate(dks, axis=1)
            dq_ref[q_rows, :] += jnp.concatenate(dqs, axis=1)
            dfq_ref[q_rows, :] += jnp.concatenate(dfqs, axis=1)
            df_ref[0] += jnp.concatenate(dfs, axis=0)

    qmap = lambda h, j, i: (jnp.maximum(i, j), h)
    kmap = lambda h, j, i: (j, h)
    return pl.pallas_call(
        body, name=name, grid=(npair, nk, nq),
        in_specs=[pl.BlockSpec((tq, HEAD_PAIR), qmap), pl.BlockSpec((tk, HEAD_PAIR), kmap),
                  pl.BlockSpec((tk, HEAD_PAIR), kmap), pl.BlockSpec((1, 2, tk), lambda h, j, i: (h, 0, j)),
                  pl.BlockSpec((tq, HEAD_PAIR), qmap), pl.BlockSpec((tq, HEAD_PAIR), qmap),
                  pl.BlockSpec((tq, HEAD_PAIR), qmap)],
        out_specs=[pl.BlockSpec((l, HEAD_PAIR), lambda h, j, i: (0, h)), pl.BlockSpec((tk, HEAD_PAIR), kmap),
                   pl.BlockSpec((tk, HEAD_PAIR), kmap), pl.BlockSpec((1, 2, tk), lambda h, j, i: (h, 0, j)),
                   pl.BlockSpec((l, HEAD_PAIR), lambda h, j, i: (0, h))],
        out_shape=[jax.ShapeDtypeStruct((l, d), F32), jax.ShapeDtypeStruct((l, d), F32),
                   jax.ShapeDtypeStruct((l, d), F32), jax.ShapeDtypeStruct((npair, 2, l), F32),
                   jax.ShapeDtypeStruct((l, d), F32)],
        compiler_params=_cp("parallel", "arbitrary", "arbitrary"),
    )(q, k, v, ft, do, lse, delta)


def _my_place():
    return lax.axis_index("x"), lax.axis_index("y"), lax.axis_index("c")


def _chip_exchange(srcs, out_meta, plan, name):
    n_src, n_out, n_plan = len(srcs), len(out_meta), len(plan)

    def body(*refs):
        src_refs = refs[:n_src]
        out_refs = refs[n_src:n_src + n_out]
        send_sems, recv_sems, local_sems = refs[n_src + n_out:]
        x, y, c = _my_place()
        me = 2 * x + y
        copies = []
        for n, (si, oi, src_view, dst_view) in enumerate(plan):
            local = pltpu.make_async_copy(src_view(src_refs[si], me), dst_view(out_refs[oi], me), local_sems.at[n])
            local.start()
            copies.append(local)
            for k in (1, 2, 3):
                peer = me ^ k
                rc = pltpu.make_async_remote_copy(
                    src_ref=src_view(src_refs[si], peer), dst_ref=dst_view(out_refs[oi], me),
                    send_sem=send_sems.at[n, k - 1], recv_sem=recv_sems.at[n, k - 1],
                    device_id=(peer >> 1, peer & 1, c), device_id_type=MESH)
                rc.start()
                copies.append(rc)
        for cp in copies:
            cp.wait()

    any_spec = pl.BlockSpec(memory_space=pl.ANY)
    return pl.pallas_call(
        body, name=name,
        in_specs=[any_spec] * n_src, out_specs=[any_spec] * n_out,
        out_shape=[jax.ShapeDtypeStruct(shape, dt) for (shape, dt) in out_meta],
        scratch_shapes=[pltpu.SemaphoreType.DMA((n_plan, 3)), pltpu.SemaphoreType.DMA((n_plan, 3)),
                        pltpu.SemaphoreType.DMA((n_plan,))],
    )(*srcs)


def _core_exchange(arrays, name):
    n_items = len(arrays)

    def body(*refs):
        srcs = refs[:n_items]
        outs = refs[n_items:2 * n_items]
        send_sems, recv_sems = refs[2 * n_items:]
        x, y, c = _my_place()
        copies = []
        for n in range(n_items):
            rc = pltpu.make_async_remote_copy(
                src_ref=srcs[n], dst_ref=outs[n], send_sem=send_sems.at[n], recv_sem=recv_sems.at[n],
                device_id=(x, y, 1 - c), device_id_type=MESH)
            rc.start()
            copies.append(rc)
        for cp in copies:
            cp.wait()

    any_spec = pl.BlockSpec(memory_space=pl.ANY)
    return pl.pallas_call(
        body, name=name,
        in_specs=[any_spec] * n_items, out_specs=[any_spec] * n_items,
        out_shape=[jax.ShapeDtypeStruct(a.shape, a.dtype) for a in arrays],
        scratch_shapes=[pltpu.SemaphoreType.DMA((n_items,)), pltpu.SemaphoreType.DMA((n_items,))],
    )(*arrays)


def _sum_chips(parts, name):
    _, rows, cols = parts.shape
    tm = _tile(rows, 512)

    def body(p_ref, o_ref):
        acc = p_ref[0].astype(F32)
        for s in range(1, N_CHIPS):
            acc = acc + p_ref[s].astype(F32)
        o_ref[...] = acc

    return pl.pallas_call(
        body, name=name, grid=(rows // tm,),
        in_specs=[pl.BlockSpec((N_CHIPS, tm, cols), lambda i: (0, i, 0))],
        out_specs=pl.BlockSpec((tm, cols), lambda i: (i, 0)),
        out_shape=jax.ShapeDtypeStruct((rows, cols), F32),
        compiler_params=_cp("parallel"),
    )(parts)


def _adamw(ga, gb, w, m, v, name):
    rows, cols = w.shape
    tm = _tile(rows, 512)
    c1 = 1.0 - ADAM_B1 ** ADAM_STEP
    c2 = 1.0 - ADAM_B2 ** ADAM_STEP

    def body(ga_ref, gb_ref, w_ref, m_ref, v_ref, g_ref, d_ref, nm_ref, nv_ref):
        g = ga_ref[...] + gb_ref[...]
        nm = ADAM_B1 * m_ref[...] + (1.0 - ADAM_B1) * g
        nv = ADAM_B2 * v_ref[...] + (1.0 - ADAM_B2) * (g * g)
        g_ref[...] = g
        nm_ref[...] = nm
        nv_ref[...] = nv
        d_ref[...] = -ADAM_LR * ((nm / c1) / (jnp.sqrt(nv / c2) + ADAM_EPS) + ADAM_WD * w_ref[...])

    spec = pl.BlockSpec((tm, cols), lambda i: (i, 0))
    return pl.pallas_call(
        body, name=name, grid=(rows // tm,), in_specs=[spec] * 5, out_specs=[spec] * 4,
        out_shape=[jax.ShapeDtypeStruct((rows, cols), F32)] * 4, compiler_params=_cp("parallel"),
    )(ga, gb, w, m, v)


def _ssm_discretise(log_dt, a_re, a_im, b_re, b_im):
    dt = jnp.exp(log_dt)[:, None]
    mag = jnp.exp(a_re * dt)
    lbr = mag * jnp.cos(a_im * dt)
    lbi = mag * jnp.sin(a_im * dt)
    den = a_re * a_re + a_im * a_im
    nr, ni = lbr - 1.0, lbi
    qr = (nr * a_re + ni * a_im) / den
    qi = (ni * a_re - nr * a_im) / den
    bbr = qr[..., None] * b_re - qi[..., None] * b_im
    bbi = qr[..., None] * b_im + qi[..., None] * b_re
    return lbr, lbi, bbr, bbi


def _cmul(ar, ai, br, bi):
    return ar * br - ai * bi, ar * bi + ai * br


def _scan_tables(lr, li, nj, reverse):
    lr = lr.reshape(nj, 1, -1)
    li = li.reshape(nj, 1, -1)
    if reverse:
        li = -li
    pows = [(lr, li)]
    for _ in range(7):
        pows.append(_cmul(*pows[-1], lr, li))
    r = jnp.arange(SUBLANES).reshape(1, SUBLANES, 1)
    if reverse:
        r = SUBLANES - 1 - r
    out = []
    for k in (1, 2, 4):
        pr, pi = pows[k - 1]
        keep = (r >= k).astype(F32)
        out += [pr * keep, pi * keep]
    shape = (nj, SUBLANES, lr.shape[-1])
    cr = jnp.zeros(shape, F32)
    ci = jnp.zeros(shape, F32)
    for e in range(SUBLANES):
        sel = (r == e).astype(F32)
        cr = cr + sel * pows[e][0]
        ci = ci + sel * pows[e][1]
    out += [cr, ci]
    return jnp.stack(out, axis=1)


def _group_eye(gl):
    return jnp.eye(gl, dtype=F32)


def _block_diag_in(bbr, bbi, nj):
    g, p, c = bbr.shape
    gl = g // nj
    eye = _group_eye(gl)[None, :, None, :, None]

    def one(b):
        t = b.reshape(nj, gl, p, c).transpose(0, 1, 3, 2)[:, :, :, None, :]
        return (t * eye).reshape(nj, gl * c, gl * p)

    return jnp.concatenate([one(bbr), one(bbi)], axis=2)


def _block_diag_in_grad(gmat, nj, p, c):
    gl = gmat.shape[1] // c
    n = gl * p
    eye = _group_eye(gl)[None, :, None, :, None]

    def one(m):
        t = jnp.sum(m.reshape(nj, gl, c, gl, p) * eye, axis=3)
        return t.transpose(0, 1, 3, 2).reshape(nj * gl, p, c)

    return one(gmat[:, :, :n]), one(gmat[:, :, n:])


def _block_diag_out(c_re, c_im, nj):
    g, c, p = c_re.shape
    gl = g // nj
    eye = _group_eye(gl)[None, :, None, :, None]

    def one(m):
        t = m.reshape(nj, gl, c, p).transpose(0, 1, 3, 2)[:, :, :, None, :]
        return (t * eye).reshape(nj, gl * p, gl * c)

    return jnp.concatenate([one(c_re), -one(c_im)], axis=1)


def _block_diag_out_grad(gmat, nj, p, c):
    gl = gmat.shape[2] // c
    n = gl * p
    eye = _group_eye(gl)[None, :, None, :, None]

    def one(m):
        t = jnp.sum(m.reshape(nj, gl, p, gl, c) * eye, axis=3)
        return t.transpose(0, 1, 3, 2).reshape(nj * gl, c, p)

    return one(gmat[:, :n, :]), -one(gmat[:, n:, :])


def _pad_rows(flat, cols):
    per = SUBLANES * cols
    n = flat.shape[0]
    total = -(-n // per) * per
    return jnp.pad(flat, (0, total - n)).reshape(total // cols, cols)


def _pack_small(arrs, cols):
    packed = jnp.concatenate([_pad_rows(a.reshape(-1), cols) for a in arrs], axis=0)
    rows = packed.shape[0]
    return jnp.pad(packed, ((0, -rows % 128), (0, 0)))


def _unpack_small(packed, shapes, cols):
    out = []
    row = 0
    for s in shapes:
        n = math.prod(s)
        rows = -(-n // (SUBLANES * cols)) * SUBLANES
        out.append(packed[row:row + rows].reshape(-1)[:n].reshape(s))
        row += rows
    return out


def kernel(x, mix_norm, mlp_norm, mlp_w1, mlp_w2, ssm_log_dt, ssm_a_re, ssm_a_im, ssm_b_re, ssm_b_im, ssm_c_re, ssm_c_im, ssm_d, ssm_w_glu, kv_norm, w_kvf, b_f, attn_wq, attn_wo, final_norm, loss_target, m_mix_norm, m_mlp_norm, m_mlp_w1, m_mlp_w2, m_ssm_log_dt, m_ssm_a_re, m_ssm_a_im, m_ssm_b_re, m_ssm_b_im, m_ssm_c_re, m_ssm_c_im, m_ssm_d, m_ssm_w_glu, m_kv_norm, m_w_kvf, m_b_f, m_attn_wq, m_attn_wo, m_final_norm, v_mix_norm, v_mlp_norm, v_mlp_w1, v_mlp_w2, v_ssm_log_dt, v_ssm_a_re, v_ssm_a_im, v_ssm_b_re, v_ssm_b_im, v_ssm_c_re, v_ssm_c_im, v_ssm_d, v_ssm_w_glu, v_kv_norm, v_w_kvf, v_b_f, v_attn_wq, v_attn_wo, v_final_norm):
    seq, d = x.shape[1], x.shape[2]
    depth = mix_norm.shape[0]
    n_a = ssm_log_dt.shape[0]
    n_b = depth - n_a
    ff = mlp_w1.shape[2] * N_CHIPS
    n_heads = d // HEAD_DIM
    n_groups = d // SSM_GROUP
    p_state = ssm_a_re.shape[2]
    gb = min(d, 256)
    nj = d // gb
    kvf_cols = w_kvf.shape[1]
    ds4, dq4 = d // N_CHIPS, d // (2 * N_CHIPS)
    chip = 2 * lax.axis_index("x") + lax.axis_index("y")

    def cols_of(width):
        return lambda ref, s: ref.at[:, :, pl.ds(pl.multiple_of(s * width, LANES), width)]

    def rows_of(height):
        return lambda ref, s: ref.at[:, pl.ds(pl.multiple_of(s * height, SUBLANES), height), :]

    whole = lambda ref, s: ref
    slot = lambda ref, s: ref.at[s]
    shards = [mlp_w1.astype(BF16), mlp_w2.astype(BF16), ssm_w_glu.astype(BF16), w_kvf.astype(BF16),
              attn_wq.astype(BF16), attn_wo.astype(BF16), ssm_d]
    gathered = _chip_exchange(
        shards,
        [((depth, d, ff), BF16), ((depth, ff, d), BF16), ((n_a, d, 2 * d), BF16), ((N_CHIPS, d, kvf_cols), BF16),
         ((n_b, d, d), BF16), ((n_b, d, d), BF16), ((N_CHIPS, n_a, ds4), F32)],
        [(0, 0, whole, cols_of(d)), (1, 1, whole, rows_of(d)), (2, 2, whole, cols_of(2 * ds4)), (3, 3, whole, slot),
         (4, 4, whole, rows_of(ds4)), (5, 5, whole, rows_of(ds4)), (6, 6, whole, slot)],
        "gather_weights")
    w1_all, w2_all, wglu_all, kvf_parts, wq_all, wo_all, skip_parts = gathered
    skip_all = skip_parts.transpose(1, 0, 2).reshape(n_a, d)
    kvf_all = jnp.concatenate([kvf_parts[s] for s in range(N_CHIPS)], axis=1)
    wk = kvf_all[:, :d]
    wv = kvf_all[:, d:2 * d]
    wf = jnp.pad(kvf_all[:, 2 * d:], ((0, 0), (0, LANES - n_heads)))
    bf_row = jnp.pad(b_f, (0, LANES - n_heads)).reshape(1, LANES)

    h = x[0]
    target = loss_target[0]

    saved = []
    for i in range(n_a):
        lbr, lbi, bbr, bbi = _ssm_discretise(ssm_log_dt[i], ssm_a_re[i], ssm_a_im[i], ssm_b_re[i], ssm_b_im[i])
        bblk = _block_diag_in(bbr, bbi, nj)
        cblk = _block_diag_out(ssm_c_re[i], ssm_c_im[i], nj)
        rec = dict(h0=h, lam=(lbr, lbi), bblk=bblk, cblk=cblk)
        u = _norm_fwd(h, mix_norm[i:i + 1], f"s5_norm_{i}")
        rec["u"] = u
        dskip = rec["dskip"] = skip_all[i:i + 1]
        states, y = _scan_fwd(u, bblk.astype(BF16), cblk.astype(BF16), _scan_tables(lbr, lbi, nj, False), dskip,
                              f"s5_scan_{i}")
        rec["states"], rec["y"] = states, y
        h, rec["zw"] = _s5_post_fwd(h, y, wglu_all[i], f"s5_glu_{i}")
        rec["h1"] = h
        h, rec["ap"] = _mlp_fwd(h, mlp_norm[i:i + 1], w1_all[i], w2_all[i], f"mlp_{i}")
        saved.append(rec)
    h_kv = h
    k, v, flog, cum = _kvf_fwd(h, kv_norm.reshape(1, d), wk, wv, wf, bf_row, "kvf")
    ft = cum[:, :n_heads].T.reshape(n_heads // 2, 2, seq)
    for jb in range(n_b):
        i = n_a + jb
        rec = dict(h0=h)
        q = _q_fwd(h, mix_norm[i:i + 1], wq_all[jb], f"attn_q_{jb}")
        o, lse = _flash_fwd(q, k, v, ft, f"attn_core_{jb}")
        rec["q"], rec["o"], rec["lse"] = q, o, lse
        h = _o_fwd(h, o, wo_all[jb], f"attn_out_{jb}")
        rec["h1"] = h
        h, rec["ap"] = _mlp_fwd(h, mlp_norm[i:i + 1], w1_all[i], w2_all[i], f"mlp_{i}")
        saved.append(rec)
    dh, loss_row, g_final = _loss_head(h, target, final_norm.reshape(1, d), "loss_head")
    loss = lax.psum(loss_row[0, 0], ("x", "y", "c"))

    head_ones = (jnp.arange(d)[:, None] // HEAD_DIM == jnp.arange(d)[None, :] // HEAD_DIM).astype(BF16)
    g_mix = [None] * depth
    g_mlp = [None] * depth
    g_w1 = [None] * depth
    g_w2 = [None] * depth
    g_wq = [None] * n_b
    g_wo = [None] * n_b
    g_glu = [None] * n_a
    g_ssm = [None] * n_a
    dk_parts, dv_parts, df_parts = [], [], []

    def mlp_back(dh, i, rec):
        dh_in, hm, a, dap, g_mlp[i] = _mlp_bwd(dh, rec["h1"], rec["ap"], mlp_norm[i:i + 1], w1_all[i], w2_all[i],
                                               f"mlp_bwd_{i}")
        g_w2[i] = _matmul_tn(a, dh, f"mlp_dw2_{i}")
        g_w1[i] = _matmul_tn(hm, dap, f"mlp_dw1_{i}")
        return dh_in

    for jb in reversed(range(n_b)):
        i = n_a + jb
        rec = saved[i]
        dh = mlp_back(dh, i, rec)
        do, delta = _o_bwd(dh, rec["o"], wo_all[jb], head_ones, f"attn_out_bwd_{jb}")
        g_wo[jb] = _matmul_tn(rec["o"], dh, f"attn_dwo_{jb}")
        dq, dk, dv, dft, dfq = _flash_bwd(rec["q"], k, v, ft, do, rec["lse"], delta, f"attn_core_bwd_{jb}")
        dk_parts.append(dk)
        dv_parts.append(dv)
        df_parts.append(dft.reshape(n_heads, seq).T + dfq[:, ::HEAD_DIM])
        dh, hn, dqs, g_mix[i] = _q_bwd(dh, rec["h0"], dq, mix_norm[i:i + 1], wq_all[jb], f"attn_q_bwd_{jb}")
        g_wq[jb] = _matmul_tn(hn, dqs, f"attn_dwq_{jb}")

    dft = df_parts[0]
    for extra in df_parts[1:]:
        dft = dft + extra
    dcum = jnp.pad(dft, ((0, 0), (0, LANES - n_heads)))
    dh, hk, dkb, dvb, dfb, g_kvn, g_bf = _kvf_bwd(dh, h_kv, dk_parts[0], dk_parts[1], dv_parts[0], dv_parts[1],
                                                  dcum, flog, kv_norm.reshape(1, d), wk, wv, wf, "kvf_bwd")
    g_kvf = jnp.concatenate([_matmul_tn(hk, dkb, "kvf_dwk"), _matmul_tn(hk, dvb, "kvf_dwv"),
                             _matmul_tn(hk, dfb, "kvf_dwf")[:, :n_heads]], axis=1)

    for i in reversed(range(n_a)):
        rec = saved[i]
        dh = mlp_back(dh, i, rec)
        dy, z, dzw = _s5_post_bwd(dh, rec["y"], rec["zw"], wglu_all[i], f"s5_glu_bwd_{i}")
        g_glu[i] = _matmul_tn(z, dzw, f"s5_dwglu_{i}")
        lbr, lbi = rec["lam"]
        bblk_t = rec["bblk"].transpose(0, 2, 1).astype(BF16)
        cblk_t = rec["cblk"].transpose(0, 2, 1).astype(BF16)
        du, glam8, gd8, gbblk, gcblk = _scan_bwd(dy, rec["u"], rec["states"], bblk_t, cblk_t,
                                                 _scan_tables(lbr, lbi, nj, True), rec["dskip"], f"s5_scan_bwd_{i}")
        dh, g_mix[i] = _norm_bwd_add(dh, du, rec["h0"], mix_norm[i:i + 1], f"s5_norm_bwd_{i}")
        glam = jnp.sum(glam8, axis=1)
        n_st = glam.shape[1] // 2
        g_lbr = glam[:, :n_st].reshape(n_groups, p_state)
        g_lbi = glam[:, n_st:].reshape(n_groups, p_state)
        g_bbr, g_bbi = _block_diag_in_grad(gbblk, nj, p_state, SSM_GROUP)
        g_cre, g_cim = _block_diag_out_grad(gcblk, nj, p_state, SSM_GROUP)
        _, pull = jax.vjp(_ssm_discretise, ssm_log_dt[i], ssm_a_re[i], ssm_a_im[i], ssm_b_re[i], ssm_b_im[i])
        g_ldt, g_are, g_aim, g_bre, g_bim = pull((g_lbr, g_lbi, g_bbr, g_bbi))
        g_ssm[i] = dict(log_dt=g_ldt, a_re=g_are, a_im=g_aim, b_re=g_bre, b_im=g_bim, c_re=g_cre, c_im=g_cim,
                        d=jnp.sum(gd8, axis=1).reshape(d))
    grad_x = dh[None]

    def stack_small(key):
        return jnp.stack([g_ssm[i][key] for i in range(n_a)])

    small_grads = [jnp.concatenate(g_mix, axis=0), jnp.concatenate(g_mlp, axis=0), stack_small("log_dt"),
                   stack_small("a_re"), stack_small("a_im"), stack_small("b_re"), stack_small("b_im"),
                   stack_small("c_re"), stack_small("c_im"), stack_small("d"), g_kvn.reshape(d),
                   g_bf[0, :n_heads], g_final.reshape(d)]
    small_w = [mix_norm, mlp_norm, ssm_log_dt, ssm_a_re, ssm_a_im, ssm_b_re, ssm_b_im, ssm_c_re, ssm_c_im,
               ssm_d, kv_norm, b_f, final_norm]
    small_m = [m_mix_norm, m_mlp_norm, m_ssm_log_dt, m_ssm_a_re, m_ssm_a_im, m_ssm_b_re, m_ssm_b_im, m_ssm_c_re,
               m_ssm_c_im, m_ssm_d, m_kv_norm, m_b_f, m_final_norm]
    small_v = [v_mix_norm, v_mlp_norm, v_ssm_log_dt, v_ssm_a_re, v_ssm_a_im, v_ssm_b_re, v_ssm_b_im, v_ssm_c_re,
               v_ssm_c_im, v_ssm_d, v_kv_norm, v_b_f, v_final_norm]
    skip_at = 9

    def widen_skip(part):
        return lax.dynamic_update_slice(jnp.zeros((n_a, d), F32), part, (0, chip * ds4))

    small_shapes = [a.shape for a in small_grads]
    pcols = 1024 if d >= 1024 else LANES
    g_small = _pack_small(small_grads, pcols)
    expand = lambda lst: _pack_small([widen_skip(a) if n == skip_at else a for n, a in enumerate(lst)], pcols)
    w_small, m_small, v_small = expand(small_w), expand(small_m), expand(small_v)
    srows = g_small.shape[0]

    kvf_send = g_kvf.reshape(d, N_CHIPS, kvf_cols).transpose(1, 0, 2)

    def cols2(width):
        return lambda ref, s: ref.at[:, pl.ds(pl.multiple_of(s * width, LANES), width)]

    def rows2(height):
        return lambda ref, s: ref.at[pl.ds(pl.multiple_of(s * height, SUBLANES), height), :]

    def into(layer):
        return lambda ref, s: ref.at[s, layer]

    assert n_b == 2
    red_srcs, red_plan = [], []

    def send(arr, out_index, src_view, dst_view):
        red_plan.append((len(red_srcs), out_index, src_view, dst_view))
        red_srcs.append(arr)

    for i in range(depth):
        send(g_w1[i], 0, cols2(d), into(i))
    for i in range(depth):
        send(g_w2[i], 1, rows2(d), into(i))
    for i in range(n_a):
        send(g_glu[i], 2, cols2(2 * ds4), into(i))
    send(kvf_send, 3, slot, slot)
    for jb in range(n_b):
        send(g_wq[jb], 4, rows2(ds4), into(jb))
    for jb in range(n_b):
        send(g_wo[jb], 5, rows2(ds4), into(jb))
    send(g_small, 6, whole, slot)
    received = _chip_exchange(
        red_srcs,
        [((N_CHIPS, depth, d, d), BF16), ((N_CHIPS, depth, d, d), BF16), ((N_CHIPS, n_a, d, 2 * ds4), BF16),
         ((N_CHIPS, d, kvf_cols), BF16), ((N_CHIPS, n_b, ds4, d), BF16), ((N_CHIPS, n_b, ds4, d), BF16),
         ((N_CHIPS, srows, pcols), F32)],
        red_plan, "reduce_chips")
    r_w1, r_w2, r_glu, r_kvf, r_wq, r_wo, r_small = received

    def flat(a):
        return a.reshape(N_CHIPS, -1, a.shape[-1])

    sums = [_sum_chips(flat(r), f"sum_chips_{n}") for n, r in
            enumerate((r_w1, r_w2, r_glu, r_kvf, r_wq, r_wo, r_small))]
    others = _core_exchange(sums, "reduce_cores")

    def two(a):
        return a.reshape(-1, a.shape[-1])

    big_w = [(mlp_w1, m_mlp_w1, v_mlp_w1), (mlp_w2, m_mlp_w2, v_mlp_w2), (ssm_w_glu, m_ssm_w_glu, v_ssm_w_glu),
             (w_kvf, m_w_kvf, v_w_kvf), (attn_wq, m_attn_wq, v_attn_wq), (attn_wo, m_attn_wo, v_attn_wo)]
    big_out = []
    for n, (w, m, v) in enumerate(big_w):
        res = _adamw(sums[n], others[n], two(w), two(m), two(v), f"adamw_{n}")
        big_out.append([r.reshape(w.shape) for r in res])
    small_out = _adamw(sums[6], others[6], w_small, m_small, v_small, "adamw_small")

    def narrow_skip(a):
        return lax.dynamic_slice(a, (0, chip * ds4), (n_a, ds4))

    unpacked = []
    for packed in small_out:
        parts = _unpack_small(packed, small_shapes, pcols)
        parts[skip_at] = narrow_skip(parts[skip_at])
        unpacked.append(parts)

    order = ["mix_norm", "mlp_norm", "mlp_w1", "mlp_w2", "ssm_log_dt", "ssm_a_re", "ssm_a_im", "ssm_b_re",
             "ssm_b_im", "ssm_c_re", "ssm_c_im", "ssm_d", "ssm_w_glu", "kv_norm", "w_kvf", "b_f", "attn_wq",
             "attn_wo", "final_norm"]
    small_names = ["mix_norm", "mlp_norm", "ssm_log_dt", "ssm_a_re", "ssm_a_im", "ssm_b_re", "ssm_b_im",
                   "ssm_c_re", "ssm_c_im", "ssm_d", "kv_norm", "b_f", "final_norm"]
    big_names = ["mlp_w1", "mlp_w2", "ssm_w_glu", "w_kvf", "attn_wq", "attn_wo"]
    outs = [loss, grad_x]
    for kind in range(4):
        for name in order:
            if name in big_names:
                outs.append(big_out[big_names.index(name)][kind])
            else:
                outs.append(unpacked[kind][small_names.index(name)])
    return tuple(outs)
```

```python
import functools
import math

import jax
import jax.numpy as jnp
from jax import lax
from jax.experimental import pallas as pl
from jax.experimental.pallas import tpu as pltpu

F32 = jnp.float32
BF16 = jnp.bfloat16

RMS_EPS = 1e-6
SSM_GROUP = 16
SSM_STATE = 64
HEAD_DIM = 64
HEAD_PAIR = 2 * HEAD_DIM
LANES = 128
SUBLANES = 8
N_CHIPS = 4
ADAM_LR = 0.001
ADAM_B1 = 0.9
ADAM_B2 = 0.999
ADAM_EPS = 1e-08
ADAM_WD = 0.01
ADAM_STEP = 10
GELU_C = math.sqrt(2.0 / math.pi)
GELU_A = 0.044715
NEG = -1e30
LN2 = math.log(2.0)
LOG2E = 1.0 / LN2
VMEM_LIMIT = 56 * 1024 * 1024
MESH = pl.DeviceIdType.MESH

NT_DIMS = (((1,), (1,)), ((), ()))
TN_DIMS = (((0,), (0,)), ((), ()))


def _cp(*sem):
    return pltpu.CompilerParams(dimension_semantics=sem if sem else None, vmem_limit_bytes=VMEM_LIMIT)


def _zero_idx(nd, *_):
    return (0,) * nd


def _tile(n, t):
    if n <= t:
        return n
    for cand in range(t - t % SUBLANES, 0, -SUBLANES):
        if n % cand == 0:
            return cand
    raise ValueError((n, t))


def _rms_fwd(h, g):
    r = lax.rsqrt(jnp.mean(h * h, axis=-1, keepdims=True) + RMS_EPS)
    hhat = h * r
    return hhat * g, hhat, r


def _rms_bwd(du, hhat, r, g):
    dhh = du * g
    dh = r * (dhh - hhat * jnp.mean(dhh * hhat, axis=-1, keepdims=True))
    return dh, du * hhat


def _sigmoid(x):
    return 1.0 / (1.0 + jnp.exp(-x))


def _gelu(x):
    t = jnp.tanh(GELU_C * (x + GELU_A * x * x * x))
    return 0.5 * x * (1.0 + t)


def _gelu_grad(x):
    t = jnp.tanh(GELU_C * (x + GELU_A * x * x * x))
    return 0.5 * (1.0 + t) + 0.5 * x * (1.0 - t * t) * GELU_C * (1.0 + 3.0 * GELU_A * x * x)


def _row_fold(x):
    tm, w = x.shape
    return jnp.sum(x.reshape(tm // SUBLANES, SUBLANES, w), axis=0)


def _split3(x):
    hi = x.astype(BF16)
    r1 = x - hi.astype(F32)
    mid = r1.astype(BF16)
    lo = (r1 - mid.astype(F32)).astype(BF16)
    return hi, mid, lo


def _exact_dot(ones_mat, x):
    hi, mid, lo = _split3(x)
    d = functools.partial(jnp.dot, preferred_element_type=F32)
    return d(ones_mat, hi) + d(ones_mat, mid) + d(ones_mat, lo)


def _rows_call(body, name, tm, row_ins, const_ins, row_outs, acc_outs=(), scratch=(), reverse=False):
    n = row_ins[0].shape[0]
    nb = n // tm
    if reverse:
        ridx = lambda i: (nb - 1 - i, 0)
    else:
        ridx = lambda i: (i, 0)
    in_specs = [pl.BlockSpec((tm, a.shape[1]), ridx) for a in row_ins]
    in_specs += [pl.BlockSpec(a.shape, functools.partial(_zero_idx, a.ndim), pipeline_mode=pl.Buffered(1))
                 for a in const_ins]
    out_shape = [jax.ShapeDtypeStruct((n, w), dt) for (w, dt) in row_outs]
    out_shape += [jax.ShapeDtypeStruct(s, dt) for (s, dt) in acc_outs]
    out_specs = [pl.BlockSpec((tm, w), ridx) for (w, dt) in row_outs]
    out_specs += [pl.BlockSpec(s, functools.partial(_zero_idx, len(s))) for (s, dt) in acc_outs]
    return pl.pallas_call(
        body, name=name, grid=(nb,), in_specs=in_specs, out_specs=out_specs, out_shape=out_shape,
        scratch_shapes=list(scratch), compiler_params=_cp("arbitrary"),
    )(*row_ins, *const_ins)


def _norm_fwd(h, g, name):
    n, d = h.shape
    tm = _tile(n, 512)

    def body(h_ref, g_ref, u_ref):
        u_ref[...] = _rms_fwd(h_ref[...], g_ref[...])[0]

    return _rows_call(body, name, tm, [h], [g], [(d, F32)])[0]


def _norm_bwd_add(dh, du, h, g, name):
    n, d = h.shape
    tm = _tile(n, 512)
    nb = n // tm

    def body(dh_ref, du_ref, h_ref, g_ref, o_ref, dg_ref, acc):
        i = pl.program_id(0)

        @pl.when(i == 0)
        def _():
            acc[...] = jnp.zeros_like(acc)

        gain = g_ref[...]
        _, hhat, r = _rms_fwd(h_ref[...], gain)
        dhn, dgr = _rms_bwd(du_ref[...], hhat, r, gain)
        o_ref[...] = dh_ref[...] + dhn
        acc[...] += _row_fold(dgr)

        @pl.when(i == nb - 1)
        def _():
            dg_ref[...] = jnp.sum(acc[...], axis=0, keepdims=True)

    return _rows_call(body, name, tm, [dh, du, h], [g], [(d, F32)], [((1, d), F32)],
                      [pltpu.VMEM((SUBLANES, d), F32)])


def _mlp_fwd(h, g, w1, w2, name):
    n, d = h.shape
    ff = w1.shape[1]
    tm = _tile(n, 256)
    fc = _tile(ff, 1024)

    def body(h_ref, g_ref, w1_ref, w2_ref, o_ref, ap_ref):
        hin = h_ref[...]
        hb = _rms_fwd(hin, g_ref[...])[0].astype(BF16)
        acc = hin
        for c in range(ff // fc):
            cs = slice(c * fc, (c + 1) * fc)
            ap = jnp.dot(hb, w1_ref[:, cs], preferred_element_type=F32)
            ap_ref[:, cs] = ap.astype(BF16)
            rl = jnp.maximum(ap, 0.0)
            acc = acc + jnp.dot((rl * rl).astype(BF16), w2_ref[cs, :], preferred_element_type=F32)
        o_ref[...] = acc

    return _rows_call(body, name, tm, [h], [g, w1, w2], [(d, F32), (ff, BF16)])


def _mlp_bwd(dh, h, ap, g, w1, w2, name):
    n, d = h.shape
    ff = w1.shape[1]
    tm = _tile(n, 256)
    nb = n // tm
    fc = _tile(ff, 1024)

    def body(dh_ref, h_ref, ap_ref, g_ref, w1_ref, w2_ref, o_ref, hm_ref, a_ref, dap_ref, dg_ref, acc):
        i = pl.program_id(0)

        @pl.when(i == 0)
        def _():
            acc[...] = jnp.zeros_like(acc)

        gain = g_ref[...]
        dhv = dh_ref[...]
        hm, hhat, r = _rms_fwd(h_ref[...], gain)
        hm_ref[...] = hm.astype(BF16)
        dhb = dhv.astype(BF16)
        dhm = jnp.zeros((tm, d), F32)
        for c in range(ff // fc):
            cs = slice(c * fc, (c + 1) * fc)
            rl = jnp.maximum(ap_ref[:, cs].astype(F32), 0.0)
            a_ref[:, cs] = (rl * rl).astype(BF16)
            da = lax.dot_general(dhb, w2_ref[cs, :], NT_DIMS, preferred_element_type=F32)
            dap = (da * (2.0 * rl)).astype(BF16)
            dap_ref[:, cs] = dap
            dhm = dhm + lax.dot_general(dap, w1_ref[:, cs], NT_DIMS, preferred_element_type=F32)
        dhn, dgr = _rms_bwd(dhm, hhat, r, gain)
        o_ref[...] = dhv + dhn
        acc[...] += _row_fold(dgr)

        @pl.when(i == nb - 1)
        def _():
            dg_ref[...] = jnp.sum(acc[...], axis=0, keepdims=True)

    return _rows_call(body, name, tm, [dh, h, ap], [g, w1, w2],
                      [(d, F32), (d, BF16), (ff, BF16), (ff, BF16)], [((1, d), F32)],
                      [pltpu.VMEM((SUBLANES, d), F32)])


def _s5_post_fwd(h, y, w_glu, name):
    n, d = h.shape
    tm = _tile(n, 512)

    def body(h_ref, y_ref, w_ref, o_ref, zw_ref):
        z = _gelu(y_ref[...]).astype(BF16)
        zw = jnp.dot(z, w_ref[...], preferred_element_type=F32)
        zw_ref[...] = zw.astype(BF16)
        o_ref[...] = h_ref[...] + zw[:, :d] * _sigmoid(zw[:, d:])

    return _rows_call(body, name, tm, [h, y], [w_glu], [(d, F32), (2 * d, BF16)])


def _s5_post_bwd(dh, y, zw, w_glu, name):
    n, d = dh.shape
    tm = _tile(n, 512)

    def body(dh_ref, y_ref, zw_ref, w_ref, dy_ref, z_ref, dzw_ref):
        dhv = dh_ref[...]
        yv = y_ref[...]
        val = zw_ref[:, :d].astype(F32)
        sg = _sigmoid(zw_ref[:, d:].astype(F32))
        dzw = jnp.concatenate([dhv * sg, dhv * val * sg * (1.0 - sg)], axis=1).astype(BF16)
        dzw_ref[...] = dzw
        dz = lax.dot_general(dzw, w_ref[...], NT_DIMS, preferred_element_type=F32)
        dy_ref[...] = dz * _gelu_grad(yv)
        z_ref[...] = _gelu(yv).astype(BF16)

    return _rows_call(body, name, tm, [dh, y, zw], [w_glu], [(d, F32), (d, BF16), (2 * d, BF16)])


def _q_fwd(h, g, wq, name):
    n, d = h.shape
    tm = _tile(n, 512)
    scale = LOG2E * HEAD_DIM ** -0.5

    def body(h_ref, g_ref, w_ref, q_ref):
        hb = _rms_fwd(h_ref[...], g_ref[...])[0].astype(BF16)
        q_ref[...] = (jnp.dot(hb, w_ref[...], preferred_element_type=F32) * scale).astype(BF16)

    return _rows_call(body, name, tm, [h], [g, wq], [(wq.shape[1], BF16)])[0]


def _q_bwd(dh, h, dq, g, wq, name):
    n, d = h.shape
    tm = _tile(n, 512)
    nb = n // tm
    scale = HEAD_DIM ** -0.5

    def body(dh_ref, h_ref, dq_ref, g_ref, w_ref, o_ref, hn_ref, dqs_ref, dg_ref, acc):
        i = pl.program_id(0)

        @pl.when(i == 0)
        def _():
            acc[...] = jnp.zeros_like(acc)

        gain = g_ref[...]
        hn, hhat, r = _rms_fwd(h_ref[...], gain)
        hn_ref[...] = hn.astype(BF16)
        dqs = (dq_ref[...] * scale).astype(BF16)
        dqs_ref[...] = dqs
        dhn = lax.dot_general(dqs, w_ref[...], NT_DIMS, preferred_element_type=F32)
        dhi, dgr = _rms_bwd(dhn, hhat, r, gain)
        o_ref[...] = dh_ref[...] + dhi
        acc[...] += _row_fold(dgr)

        @pl.when(i == nb - 1)
        def _():
            dg_ref[...] = jnp.sum(acc[...], axis=0, keepdims=True)

    return _rows_call(body, name, tm, [dh, h, dq], [g, wq], [(d, F32), (d, BF16), (wq.shape[1], BF16)],
                      [((1, d), F32)], [pltpu.VMEM((SUBLANES, d), F32)])


def _o_fwd(h, o, wo, name):
    n, d = h.shape
    tm = _tile(n, 512)

    def body(h_ref, o_ref, w_ref, out_ref):
        out_ref[...] = h_ref[...] + jnp.dot(o_ref[...], w_ref[...], preferred_element_type=F32)

    return _rows_call(body, name, tm, [h, o], [wo], [(d, F32)])[0]


def _o_bwd(dh, o, wo, head_ones, name):
    n, d = dh.shape
    tm = _tile(n, 512)

    def body(dh_ref, o_ref, w_ref, e_ref, do_ref, dl_ref):
        do = lax.dot_general(dh_ref[...].astype(BF16), w_ref[...], NT_DIMS, preferred_element_type=F32).astype(BF16)
        do_ref[...] = do
        dl_ref[...] = _exact_dot_rhs(do.astype(F32) * o_ref[...].astype(F32), e_ref[...])

    return _rows_call(body, name, tm, [dh, o], [wo, head_ones], [(wo.shape[0], BF16), (wo.shape[0], F32)])


def _exact_dot_rhs(x, ones_mat):
    hi, mid, lo = _split3(x)
    d = functools.partial(jnp.dot, preferred_element_type=F32)
    return d(hi, ones_mat) + d(mid, ones_mat) + d(lo, ones_mat)


def _kvf_fwd(h, g, wk, wv, wf, bf, name):
    n, d = h.shape
    tm = _tile(n, 512)

    def body(h_ref, g_ref, wk_ref, wv_ref, wf_ref, bf_ref, k_ref, v_ref, fl_ref, cum_ref, carry):
        i = pl.program_id(0)

        @pl.when(i == 0)
        def _():
            carry[...] = jnp.zeros_like(carry)

        hb = _rms_fwd(h_ref[...], g_ref[...])[0].astype(BF16)
        k_ref[...] = jnp.dot(hb, wk_ref[...], preferred_element_type=F32).astype(BF16)
        v_ref[...] = jnp.dot(hb, wv_ref[...], preferred_element_type=F32).astype(BF16)
        fl = jnp.dot(hb, wf_ref[...], preferred_element_type=F32) + bf_ref[...]
        fl_ref[...] = fl
        logf = jnp.minimum(fl, 0.0) - jnp.log(1.0 + jnp.exp(-jnp.abs(fl)))
        rows = lax.broadcasted_iota(jnp.int32, (tm, tm), 0)
        cols = lax.broadcasted_iota(jnp.int32, (tm, tm), 1)
        lower = (rows >= cols).astype(BF16)
        cum = _exact_dot(lower, logf) + carry[0:1, :]
        cum_ref[...] = cum
        carry[...] = jnp.broadcast_to(cum[tm - 1:tm, :], carry.shape)

    return _rows_call(body, name, tm, [h], [g, wk, wv, wf, bf],
                      [(wk.shape[1], BF16), (wv.shape[1], BF16), (LANES, F32), (LANES, F32)],
                      scratch=[pltpu.VMEM((SUBLANES, LANES), F32)])


def _kvf_bwd(dh, h, dk1, dk2, dv1, dv2, dcum, fl, g, wk, wv, wf, name):
    n, d = h.shape
    tm = _tile(n, 512)
    nb = n // tm

    def body(dh_ref, h_ref, dk1_ref, dk2_ref, dv1_ref, dv2_ref, dc_ref, fl_ref, g_ref, wk_ref, wv_ref, wf_ref,
             o_ref, hk_ref, dk_ref, dv_ref, df_ref, dg_ref, db_ref, acc, bacc, carry):
        i = pl.program_id(0)

        @pl.when(i == 0)
        def _():
            acc[...] = jnp.zeros_like(acc)
            bacc[...] = jnp.zeros_like(bacc)
            carry[...] = jnp.zeros_like(carry)

        rows = lax.broadcasted_iota(jnp.int32, (tm, tm), 0)
        cols = lax.broadcasted_iota(jnp.int32, (tm, tm), 1)
        upper = (rows <= cols).astype(BF16)
        dlogf = _exact_dot(upper, dc_ref[...]) + carry[0:1, :]
        carry[...] = jnp.broadcast_to(dlogf[0:1, :], carry.shape)
        df = dlogf / (1.0 + jnp.exp(fl_ref[...]))
        dfb = df.astype(BF16)
        df_ref[...] = dfb
        bacc[...] += _row_fold(df)
        dkb = ((dk1_ref[...] + dk2_ref[...]) * LN2).astype(BF16)
        dvb = (dv1_ref[...] + dv2_ref[...]).astype(BF16)
        dk_ref[...] = dkb
        dv_ref[...] = dvb
        gain = g_ref[...]
        hk, hhat, r = _rms_fwd(h_ref[...], gain)
        hk_ref[...] = hk.astype(BF16)
        dhk = lax.dot_general(dkb, wk_ref[...], NT_DIMS, preferred_element_type=F32)
        dhk = dhk + lax.dot_general(dvb, wv_ref[...], NT_DIMS, preferred_element_type=F32)
        dhk = dhk + lax.dot_general(dfb, wf_ref[...], NT_DIMS, preferred_element_type=F32)
        dhi, dgr = _rms_bwd(dhk, hhat, r, gain)
        o_ref[...] = dh_ref[...] + dhi
        acc[...] += _row_fold(dgr)

        @pl.when(i == nb - 1)
        def _():
            dg_ref[...] = jnp.sum(acc[...], axis=0, keepdims=True)
            db_ref[...] = jnp.sum(bacc[...], axis=0, keepdims=True)

    return _rows_call(body, name, tm, [dh, h, dk1, dk2, dv1, dv2, dcum, fl], [g, wk, wv, wf],
                      [(d, F32), (d, BF16), (wk.shape[1], BF16), (wv.shape[1], BF16), (LANES, BF16)],
                      [((1, d), F32), ((1, LANES), F32)],
                      [pltpu.VMEM((SUBLANES, d), F32), pltpu.VMEM((SUBLANES, LANES), F32),
                       pltpu.VMEM((SUBLANES, LANES), F32)], reverse=True)


def _loss_head(h, target, g, name):
    n, d = h.shape
    tm = _tile(n, 512)
    nb = n // tm

    def body(h_ref, t_ref, g_ref, dh_ref, loss_ref, dg_ref, lacc, gacc):
        i = pl.program_id(0)

        @pl.when(i == 0)
        def _():
            lacc[...] = jnp.zeros_like(lacc)
            gacc[...] = jnp.zeros_like(gacc)

        gain = g_ref[...]
        yv, hhat, r = _rms_fwd(h_ref[...], gain)
        e = yv - t_ref[...]
        lacc[...] += _row_fold(e * e)
        dhv, dgr = _rms_bwd(e * (1.0 / d), hhat, r, gain)
        dh_ref[...] = dhv
        gacc[...] += _row_fold(dgr)

        @pl.when(i == nb - 1)
        def _():
            loss_ref[...] = jnp.full((1, LANES), jnp.sum(lacc[...]) * (0.5 / d), F32)
            dg_ref[...] = jnp.sum(gacc[...], axis=0, keepdims=True)

    return _rows_call(body, name, tm, [h, target], [g], [(d, F32)], [((1, LANES), F32), ((1, d), F32)],
                      [pltpu.VMEM((SUBLANES, d), F32), pltpu.VMEM((SUBLANES, d), F32)])


def _matmul_tn(a, b, name, out_dtype=BF16):
    l, m = a.shape
    n = b.shape[1]
    tl = _tile(l, 1024)
    tmm = _tile(m, 512)
    tn = _tile(n, 1024)
    nl = l // tl

    def body(a_ref, b_ref, o_ref, acc):
        k = pl.program_id(2)

        @pl.when(k == 0)
        def _():
            acc[...] = jnp.zeros_like(acc)

        acc[...] += lax.dot_general(a_ref[...].astype(BF16), b_ref[...].astype(BF16), TN_DIMS,
                                    preferred_element_type=F32)

        @pl.when(k == nl - 1)
        def _():
            o_ref[...] = acc[...].astype(out_dtype)

    return pl.pallas_call(
        body, name=name, grid=(m // tmm, n // tn, nl),
        in_specs=[pl.BlockSpec((tl, tmm), lambda i, j, k: (k, i)), pl.BlockSpec((tl, tn), lambda i, j, k: (k, j))],
        out_specs=pl.BlockSpec((tmm, tn), lambda i, j, k: (i, j)),
        out_shape=jax.ShapeDtypeStruct((m, n), out_dtype),
        scratch_shapes=[pltpu.VMEM((tmm, tn), F32)],
        compiler_params=_cp("parallel", "parallel", "arbitrary"),
    )(a, b)


def _scan_fwd(u, bblk, cblk, tabs, dskip, name):
    l, d = u.shape
    nj, gb, n2 = bblk.shape
    n = n2 // 2
    tm = _tile(l, 512)
    nb = l // tm

    def body(u_ref, b_ref, c_ref, t_ref, d_ref, st_ref, y_ref, carry):
        i = pl.program_id(1)

        @pl.when(i == 0)
        def _():
            carry[...] = jnp.zeros_like(carry)

        uf = u_ref[...]
        st_ref[...] = jnp.dot(uf.astype(BF16), b_ref[0], preferred_element_type=F32)

        def step(rb, c):
            cr, ci = c
            rows = pl.ds(pl.multiple_of(rb * SUBLANES, SUBLANES), SUBLANES)
            xr = st_ref[rows, 0:n]
            xi = st_ref[rows, n:n2]
            for lvl, sh in enumerate((1, 2, 4)):
                ar = t_ref[0, 2 * lvl]
                ai = t_ref[0, 2 * lvl + 1]
                sr = pltpu.roll(xr, sh, 0)
                si = pltpu.roll(xi, sh, 0)
                xr, xi = xr + ar * sr - ai * si, xi + ar * si + ai * sr
            lr = t_ref[0, 6]
            li = t_ref[0, 7]
            xr, xi = xr + lr * cr - li * ci, xi + lr * ci + li * cr
            st_ref[rows, 0:n] = xr
            st_ref[rows, n:n2] = xi
            return (jnp.broadcast_to(xr[SUBLANES - 1:SUBLANES, :], (SUBLANES, n)),
                    jnp.broadcast_to(xi[SUBLANES - 1:SUBLANES, :], (SUBLANES, n)))

        cr, ci = lax.fori_loop(0, tm // SUBLANES, step, (carry[:, 0:n], carry[:, n:n2]))
        carry[:, 0:n] = cr
        carry[:, n:n2] = ci
        y_ref[...] = jnp.dot(st_ref[...].astype(BF16), c_ref[0], preferred_element_type=F32) + d_ref[...] * uf

    return pl.pallas_call(
        body, name=name, grid=(nj, nb),
        in_specs=[pl.BlockSpec((tm, gb), lambda j, i: (i, j)),
                  pl.BlockSpec((1, gb, n2), lambda j, i: (j, 0, 0)),
                  pl.BlockSpec((1, n2, gb), lambda j, i: (j, 0, 0)),
                  pl.BlockSpec((1, 8, SUBLANES, n), lambda j, i: (j, 0, 0, 0)),
                  pl.BlockSpec((1, gb), lambda j, i: (0, j))],
        out_specs=[pl.BlockSpec((tm, n2), lambda j, i: (i, j)), pl.BlockSpec((tm, gb), lambda j, i: (i, j))],
        out_shape=[jax.ShapeDtypeStruct((l, nj * n2), F32), jax.ShapeDtypeStruct((l, d), F32)],
        scratch_shapes=[pltpu.VMEM((SUBLANES, n2), F32)],
        compiler_params=_cp("parallel", "arbitrary"),
    )(u, bblk, cblk, tabs, dskip)


def _scan_bwd(dy, u, states, bblk_t, cblk_t, tabs, dskip, name):
    l, d = u.shape
    nj, n2, gb = bblk_t.shape
    n = n2 // 2
    tm = _tile(l, 512)
    nb = l // tm
    nr = tm // SUBLANES

    def body(dy_ref, u_ref, st_ref, prev_ref, bt_ref, ct_ref, t_ref, d_ref,
             du_ref, glam_ref, gd_ref, gb_ref, gc_ref, gx, carry):
        i = pl.program_id(1)
        ib = nb - 1 - i

        @pl.when(i == 0)
        def _():
            carry[...] = jnp.zeros_like(carry)
            glam_ref[...] = jnp.zeros_like(glam_ref)
            gd_ref[...] = jnp.zeros_like(gd_ref)
            gb_ref[...] = jnp.zeros_like(gb_ref)
            gc_ref[...] = jnp.zeros_like(gc_ref)

        dyv = dy_ref[...]
        uv = u_ref[...]
        dyb = dyv.astype(BF16)
        gx[...] = jnp.dot(dyb, ct_ref[0], preferred_element_type=F32)
        last_row = lax.broadcasted_iota(jnp.int32, (SUBLANES, n), 0) == SUBLANES - 1

        def block(rows, xp_r, xp_i, c):
            cr, ci = c
            gr = gx[rows, 0:n]
            gi = gx[rows, n:n2]
            for lvl, sh in enumerate((1, 2, 4)):
                ar = t_ref[0, 2 * lvl]
                ai = t_ref[0, 2 * lvl + 1]
                sr = pltpu.roll(gr, SUBLANES - sh, 0)
                si = pltpu.roll(gi, SUBLANES - sh, 0)
                gr, gi = gr + ar * sr - ai * si, gi + ar * si + ai * sr
            lr = t_ref[0, 6]
            li = t_ref[0, 7]
            gr, gi = gr + lr * cr - li * ci, gi + lr * ci + li * cr
            gx[rows, 0:n] = gr
            gx[rows, n:n2] = gi
            xs_r = pltpu.roll(jnp.where(last_row, xp_r, st_ref[rows, 0:n]), 1, 0)
            xs_i = pltpu.roll(jnp.where(last_row, xp_i, st_ref[rows, n:n2]), 1, 0)
            glam_ref[0, :, 0:n] += gr * xs_r + gi * xs_i
            glam_ref[0, :, n:n2] += gi * xs_r - gr * xs_i
            return (jnp.broadcast_to(gr[0:1, :], (SUBLANES, n)), jnp.broadcast_to(gi[0:1, :], (SUBLANES, n)))

        def step(k, c):
            rb = nr - 1 - k
            rows = pl.ds(pl.multiple_of(rb * SUBLANES, SUBLANES), SUBLANES)
            before = pl.ds(pl.multiple_of(rb * SUBLANES - SUBLANES, SUBLANES), SUBLANES)
            return block(rows, st_ref[before, 0:n], st_ref[before, n:n2], c)

        c = lax.fori_loop(0, nr - 1, step, (carry[:, 0:n], carry[:, n:n2]))
        live = (ib > 0).astype(F32)
        cr, ci = block(pl.ds(0, SUBLANES), prev_ref[:, 0:n] * live, prev_ref[:, n:n2] * live, c)
        carry[:, 0:n] = cr
        carry[:, n:n2] = ci

        gxb = gx[...].astype(BF16)
        du_ref[...] = jnp.dot(gxb, bt_ref[0], preferred_element_type=F32) + d_ref[...] * dyv
        gd_ref[0] += _row_fold(dyv * uv)
        gb_ref[0] += lax.dot_general(uv.astype(BF16), gxb, TN_DIMS, preferred_element_type=F32)
        gc_ref[0] += lax.dot_general(st_ref[...].astype(BF16), dyb, TN_DIMS, preferred_element_type=F32)

    rpb = tm // SUBLANES
    return pl.pallas_call(
        body, name=name, grid=(nj, nb),
        in_specs=[pl.BlockSpec((tm, gb), lambda j, i: (nb - 1 - i, j)),
                  pl.BlockSpec((tm, gb), lambda j, i: (nb - 1 - i, j)),
                  pl.BlockSpec((tm, n2), lambda j, i: (nb - 1 - i, j)),
                  pl.BlockSpec((SUBLANES, n2), lambda j, i: (jnp.maximum((nb - 1 - i) * rpb - 1, 0), j)),
                  pl.BlockSpec((1, n2, gb), lambda j, i: (j, 0, 0)),
                  pl.BlockSpec((1, gb, n2), lambda j, i: (j, 0, 0)),
                  pl.BlockSpec((1, 8, SUBLANES, n), lambda j, i: (j, 0, 0, 0)),
                  pl.BlockSpec((1, gb), lambda j, i: (0, j))],
        out_specs=[pl.BlockSpec((tm, gb), lambda j, i: (nb - 1 - i, j)),
                   pl.BlockSpec((1, SUBLANES, n2), lambda j, i: (j, 0, 0)),
                   pl.BlockSpec((1, SUBLANES, gb), lambda j, i: (j, 0, 0)),
                   pl.BlockSpec((1, gb, n2), lambda j, i: (j, 0, 0)),
                   pl.BlockSpec((1, n2, gb), lambda j, i: (j, 0, 0))],
        out_shape=[jax.ShapeDtypeStruct((l, d), F32),
                   jax.ShapeDtypeStruct((nj, SUBLANES, n2), F32),
                   jax.ShapeDtypeStruct((nj, SUBLANES, gb), F32),
                   jax.ShapeDtypeStruct((nj, gb, n2), F32),
                   jax.ShapeDtypeStruct((nj, n2, gb), F32)],
        scratch_shapes=[pltpu.VMEM((tm, n2), F32), pltpu.VMEM((SUBLANES, n2), F32)],
        compiler_params=_cp("parallel", "arbitrary"),
    )(dy, u, states, states, bblk_t, cblk_t, tabs, dskip)


HEAD_SLOT = 128
PAIR_SLOT = 2 * HEAD_SLOT
LANE_ROWSUM_P = HEAD_DIM
LANE_COLSUM_DS = HEAD_DIM
LANE_ROWSUM_DS = HEAD_DIM + 1


def _head_slots(a, extra):
    l, d = a.shape
    nh = d // HEAD_DIM
    pad = jnp.zeros((l, nh, HEAD_SLOT - HEAD_DIM - extra.shape[2]), a.dtype)
    return jnp.concatenate([a.reshape(l, nh, HEAD_DIM), extra.astype(a.dtype), pad], axis=2).reshape(l, nh * HEAD_SLOT)


def _flash_fwd(qx, kx, vx, f2_rows, name):
    l = qx.shape[0]
    npair = qx.shape[1] // PAIR_SLOT
    d = npair * HEAD_PAIR
    tq = _tile(l, 512)
    tk = tq
    nq, nk = l // tq, l // tk

    def body(q_ref, k_ref, v_ref, f_ref, o_ref, lse_ref, m_sc, acc_sc):
        i = pl.program_id(1)
        j = pl.program_id(2)

        @pl.when(j == 0)
        def _():
            m_sc[...] = jnp.full_like(m_sc, NEG)
            acc_sc[...] = jnp.zeros_like(acc_sc)

        def tile(on_diagonal):
            for hh in range(2):
                hs = slice(hh * HEAD_SLOT, (hh + 1) * HEAD_SLOT)
                s = lax.dot_general(q_ref[:, hs], k_ref[:, hs], NT_DIMS, preferred_element_type=F32)
                s = s - f_ref[0, hh:hh + 1, :]
                if on_diagonal:
                    keep = (lax.broadcasted_iota(jnp.int32, (tq, tk), 0)
                            >= lax.broadcasted_iota(jnp.int32, (tq, tk), 1))
                    s = jnp.where(keep, s, NEG)
                m_prev = m_sc[hh]
                m_new = jnp.maximum(m_prev, jnp.max(s, axis=-1, keepdims=True))
                alpha = jnp.exp2(m_prev - m_new)
                p = jnp.exp2(s - jnp.concatenate([m_new] * (tk // LANES), axis=1)).astype(BF16)
                acc_sc[hh] = alpha * acc_sc[hh] + jnp.dot(p, v_ref[:, hs], preferred_element_type=F32)
                m_sc[hh] = m_new

        @pl.when(j < i)
        def _():
            tile(False)

        @pl.when(j == i)
        def _():
            tile(True)

        @pl.when(j == nk - 1)
        def _():
            outs, lses = [], []
            for hh in range(2):
                acc = acc_sc[hh]
                lsum = acc[:, LANE_ROWSUM_P:LANE_ROWSUM_P + 1]
                outs.append(acc[:, :HEAD_DIM] / lsum)
                lses.append(jnp.broadcast_to(m_sc[hh][:, 0:1] + jnp.log2(lsum), (tq, HEAD_DIM)))
            o_ref[...] = jnp.concatenate(outs, axis=1).astype(BF16)
            lse_ref[...] = jnp.concatenate(lses, axis=1)

    kv_map = lambda h, i, j: (jnp.minimum(j, i), h)
    return pl.pallas_call(
        body, name=name, grid=(npair, nq, nk),
        in_specs=[pl.BlockSpec((tq, PAIR_SLOT), lambda h, i, j: (i, h)),
                  pl.BlockSpec((tk, PAIR_SLOT), kv_map), pl.BlockSpec((tk, PAIR_SLOT), kv_map),
                  pl.BlockSpec((1, 2, tk), lambda h, i, j: (h, 0, jnp.minimum(j, i)))],
        out_specs=[pl.BlockSpec((tq, HEAD_PAIR), lambda h, i, j: (i, h)),
                   pl.BlockSpec((tq, HEAD_PAIR), lambda h, i, j: (i, h))],
        out_shape=[jax.ShapeDtypeStruct((l, d), BF16), jax.ShapeDtypeStruct((l, d), F32)],
        scratch_shapes=[pltpu.VMEM((2, tq, LANES), F32), pltpu.VMEM((2, tq, HEAD_SLOT), F32)],
        compiler_params=_cp("parallel", "parallel", "arbitrary"),
    )(qx, kx, vx, f2_rows)


def _flash_bwd(qx, kx, vx, f2_rep, do, lse_rows, delta_rows, name):
    l = qx.shape[0]
    npair = qx.shape[1] // PAIR_SLOT
    d = npair * HEAD_PAIR
    tq = _tile(l, 512)
    tk = tq
    nq, nk = l // tq, l // tk

    def body(q_ref, k_ref, v_ref, f_ref, do_ref, lse_ref, dl_ref, dq_ref, dk_ref, dv_ref):
        j = pl.program_id(1)
        i = pl.program_id(2)

        @pl.when((j == 0) & (i == 0))
        def _():
            dq_ref[...] = jnp.zeros_like(dq_ref)

        @pl.when(i == 0)
        def _():
            dk_ref[...] = jnp.zeros_like(dk_ref)
            dv_ref[...] = jnp.zeros_like(dv_ref)

        def tile(on_diagonal):
            dqs, dks, dvs = [], [], []
            for hh in range(2):
                hs = slice(hh * HEAD_SLOT, (hh + 1) * HEAD_SLOT)
                qh, kh = q_ref[:, hs], k_ref[:, hs]
                vh = v_ref[:, hh * HEAD_SLOT:hh * HEAD_SLOT + HEAD_DIM]
                doh = do_ref[:, hh * HEAD_DIM:(hh + 1) * HEAD_DIM]
                st = lax.dot_general(kh, qh, NT_DIMS, preferred_element_type=F32)
                st = st - jnp.concatenate([f_ref[:, hs]] * (tq // HEAD_SLOT), axis=1)
                pt = jnp.exp2(st - lse_ref[0, hh:hh + 1, :])
                if on_diagonal:
                    keep = (lax.broadcasted_iota(jnp.int32, (tk, tq), 1)
                            >= lax.broadcasted_iota(jnp.int32, (tk, tq), 0))
                    pt = jnp.where(keep, pt, 0.0)
                dpt = lax.dot_general(vh, doh, NT_DIMS, preferred_element_type=F32)
                dsb = (pt * (dpt - dl_ref[0, hh:hh + 1, :])).astype(BF16)
                dvs.append(jnp.dot(pt.astype(BF16), doh, preferred_element_type=F32))
                dks.append(jnp.dot(dsb, qh, preferred_element_type=F32))
                dqs.append(lax.dot_general(dsb, kh, TN_DIMS, preferred_element_type=F32))
            dv_ref[...] += jnp.concatenate(dvs, axis=1)
            dk_ref[...] += jnp.concatenate(dks, axis=1)
            dq_ref[pl.ds(pl.multiple_of(i * tq, tq), tq), :] += jnp.concatenate(dqs, axis=1)

        @pl.when(i > j)
        def _():
            tile(False)

        @pl.when(i == j)
        def _():
            tile(True)

    qmap = lambda h, j, i: (jnp.maximum(i, j), h)
    kmap = lambda h, j, i: (j, h)
    row_map = lambda h, j, i: (h, 0, jnp.maximum(i, j))
    return pl.pallas_call(
        body, name=name, grid=(npair, nk, nq),
        in_specs=[pl.BlockSpec((tq, PAIR_SLOT), qmap), pl.BlockSpec((tk, PAIR_SLOT), kmap),
                  pl.BlockSpec((tk, PAIR_SLOT), kmap), pl.BlockSpec((tk, PAIR_SLOT), kmap),
                  pl.BlockSpec((tq, HEAD_PAIR), qmap),
                  pl.BlockSpec((1, 2, tq), row_map), pl.BlockSpec((1, 2, tq), row_map)],
        out_specs=[pl.BlockSpec((l, PAIR_SLOT), lambda h, j, i: (0, h)), pl.BlockSpec((tk, PAIR_SLOT), kmap),
                   pl.BlockSpec((tk, HEAD_PAIR), kmap)],
        out_shape=[jax.ShapeDtypeStruct((l, npair * PAIR_SLOT), F32), jax.ShapeDtypeStruct((l, npair * PAIR_SLOT), F32),
                   jax.ShapeDtypeStruct((l, d), F32)],
        compiler_params=_cp("parallel", "arbitrary", "arbitrary"),
    )(qx, kx, vx, f2_rep, do, lse_rows, delta_rows)


def _my_place():
    return lax.axis_index("x"), lax.axis_index("y"), lax.axis_index("c")


def _chip_exchange(srcs, out_meta, plan, name):
    n_src, n_out, n_plan = len(srcs), len(out_meta), len(plan)

    def body(*refs):
        src_refs = refs[:n_src]
        out_refs = refs[n_src:n_src + n_out]
        send_sems, recv_sems, local_sems = refs[n_src + n_out:]
        x, y, c = _my_place()
        me = 2 * x + y
        copies = []
        for n, (si, oi, src_view, dst_view) in enumerate(plan):
            local = pltpu.make_async_copy(src_view(src_refs[si], me), dst_view(out_refs[oi], me), local_sems.at[n])
            local.start()
            copies.append(local)
            for k in (1, 2, 3):
                peer = me ^ k
                rc = pltpu.make_async_remote_copy(
                    src_ref=src_view(src_refs[si], peer), dst_ref=dst_view(out_refs[oi], me),
                    send_sem=send_sems.at[n, k - 1], recv_sem=recv_sems.at[n, k - 1],
                    device_id=(peer >> 1, peer & 1, c), device_id_type=MESH)
                rc.start()
                copies.append(rc)
        for cp in copies:
            cp.wait()

    any_spec = pl.BlockSpec(memory_space=pl.ANY)
    return pl.pallas_call(
        body, name=name,
        in_specs=[any_spec] * n_src, out_specs=[any_spec] * n_out,
        out_shape=[jax.ShapeDtypeStruct(shape, dt) for (shape, dt) in out_meta],
        scratch_shapes=[pltpu.SemaphoreType.DMA((n_plan, 3)), pltpu.SemaphoreType.DMA((n_plan, 3)),
                        pltpu.SemaphoreType.DMA((n_plan,))],
    )(*srcs)


def _core_exchange(arrays, name):
    n_items = len(arrays)

    def body(*refs):
        srcs = refs[:n_items]
        outs = refs[n_items:2 * n_items]
        send_sems, recv_sems = refs[2 * n_items:]
        x, y, c = _my_place()
        copies = []
        for n in range(n_items):
            rc = pltpu.make_async_remote_copy(
                src_ref=srcs[n], dst_ref=outs[n], send_sem=send_sems.at[n], recv_sem=recv_sems.at[n],
                device_id=(x, y, 1 - c), device_id_type=MESH)
            rc.start()
            copies.append(rc)
        for cp in copies:
            cp.wait()

    any_spec = pl.BlockSpec(memory_space=pl.ANY)
    return pl.pallas_call(
        body, name=name,
        in_specs=[any_spec] * n_items, out_specs=[any_spec] * n_items,
        out_shape=[jax.ShapeDtypeStruct(a.shape, a.dtype) for a in arrays],
        scratch_shapes=[pltpu.SemaphoreType.DMA((n_items,)), pltpu.SemaphoreType.DMA((n_items,))],
    )(*arrays)


def _sum_chips(parts, name):
    _, rows, cols = parts.shape
    tm = _tile(rows, 512)

    def body(p_ref, o_ref):
        acc = p_ref[0].astype(F32)
        for s in range(1, N_CHIPS):
            acc = acc + p_ref[s].astype(F32)
        o_ref[...] = acc

    return pl.pallas_call(
        body, name=name, grid=(rows // tm,),
        in_specs=[pl.BlockSpec((N_CHIPS, tm, cols), lambda i: (0, i, 0))],
        out_specs=pl.BlockSpec((tm, cols), lambda i: (i, 0)),
        out_shape=jax.ShapeDtypeStruct((rows, cols), F32),
        compiler_params=_cp("parallel"),
    )(parts)


def _adamw(ga, gb, w, m, v, name):
    rows, cols = w.shape
    tm = _tile(rows, 512)
    c1 = 1.0 - ADAM_B1 ** ADAM_STEP
    c2 = 1.0 - ADAM_B2 ** ADAM_STEP

    def body(ga_ref, gb_ref, w_ref, m_ref, v_ref, g_ref, d_ref, nm_ref, nv_ref):
        g = ga_ref[...] + gb_ref[...]
        nm = ADAM_B1 * m_ref[...] + (1.0 - ADAM_B1) * g
        nv = ADAM_B2 * v_ref[...] + (1.0 - ADAM_B2) * (g * g)
        g_ref[...] = g
        nm_ref[...] = nm
        nv_ref[...] = nv
        d_ref[...] = -ADAM_LR * ((nm / c1) / (jnp.sqrt(nv / c2) + ADAM_EPS) + ADAM_WD * w_ref[...])

    spec = pl.BlockSpec((tm, cols), lambda i: (i, 0))
    return pl.pallas_call(
        body, name=name, grid=(rows // tm,), in_specs=[spec] * 5, out_specs=[spec] * 4,
        out_shape=[jax.ShapeDtypeStruct((rows, cols), F32)] * 4, compiler_params=_cp("parallel"),
    )(ga, gb, w, m, v)


def _ssm_discretise(log_dt, a_re, a_im, b_re, b_im):
    dt = jnp.exp(log_dt)[:, None]
    mag = jnp.exp(a_re * dt)
    lbr = mag * jnp.cos(a_im * dt)
    lbi = mag * jnp.sin(a_im * dt)
    den = a_re * a_re + a_im * a_im
    nr, ni = lbr - 1.0, lbi
    qr = (nr * a_re + ni * a_im) / den
    qi = (ni * a_re - nr * a_im) / den
    bbr = qr[..., None] * b_re - qi[..., None] * b_im
    bbi = qr[..., None] * b_im + qi[..., None] * b_re
    return lbr, lbi, bbr, bbi


def _cmul(ar, ai, br, bi):
    return ar * br - ai * bi, ar * bi + ai * br


def _scan_tables(lr, li, nj, reverse):
    lr = lr.reshape(nj, 1, -1)
    li = li.reshape(nj, 1, -1)
    if reverse:
        li = -li
    pows = [(lr, li)]
    for _ in range(7):
        pows.append(_cmul(*pows[-1], lr, li))
    r = jnp.arange(SUBLANES).reshape(1, SUBLANES, 1)
    if reverse:
        r = SUBLANES - 1 - r
    out = []
    for k in (1, 2, 4):
        pr, pi = pows[k - 1]
        keep = (r >= k).astype(F32)
        out += [pr * keep, pi * keep]
    shape = (nj, SUBLANES, lr.shape[-1])
    cr = jnp.zeros(shape, F32)
    ci = jnp.zeros(shape, F32)
    for e in range(SUBLANES):
        sel = (r == e).astype(F32)
        cr = cr + sel * pows[e][0]
        ci = ci + sel * pows[e][1]
    out += [cr, ci]
    return jnp.stack(out, axis=1)


def _group_eye(gl):
    return jnp.eye(gl, dtype=F32)


def _block_diag_in(bbr, bbi, nj):
    g, p, c = bbr.shape
    gl = g // nj
    eye = _group_eye(gl)[None, :, None, :, None]

    def one(b):
        t = b.reshape(nj, gl, p, c).transpose(0, 1, 3, 2)[:, :, :, None, :]
        return (t * eye).reshape(nj, gl * c, gl * p)

    return jnp.concatenate([one(bbr), one(bbi)], axis=2)


def _block_diag_in_grad(gmat, nj, p, c):
    gl = gmat.shape[1] // c
    n = gl * p
    eye = _group_eye(gl)[None, :, None, :, None]

    def one(m):
        t = jnp.sum(m.reshape(nj, gl, c, gl, p) * eye, axis=3)
        return t.transpose(0, 1, 3, 2).reshape(nj * gl, p, c)

    return one(gmat[:, :, :n]), one(gmat[:, :, n:])


def _block_diag_out(c_re, c_im, nj):
    g, c, p = c_re.shape
    gl = g // nj
    eye = _group_eye(gl)[None, :, None, :, None]

    def one(m):
        t = m.reshape(nj, gl, c, p).transpose(0, 1, 3, 2)[:, :, :, None, :]
        return (t * eye).reshape(nj, gl * p, gl * c)

    return jnp.concatenate([one(c_re), -one(c_im)], axis=1)


def _block_diag_out_grad(gmat, nj, p, c):
    gl = gmat.shape[2] // c
    n = gl * p
    eye = _group_eye(gl)[None, :, None, :, None]

    def one(m):
        t = jnp.sum(m.reshape(nj, gl, p, gl, c) * eye, axis=3)
        return t.transpose(0, 1, 3, 2).reshape(nj * gl, c, p)

    return one(gmat[:, :n, :]), -one(gmat[:, n:, :])


def _pad_rows(flat, cols):
    per = SUBLANES * cols
    n = flat.shape[0]
    total = -(-n // per) * per
    return jnp.pad(flat, (0, total - n)).reshape(total // cols, cols)


def _pack_small(arrs, cols):
    packed = jnp.concatenate([_pad_rows(a.reshape(-1), cols) for a in arrs], axis=0)
    rows = packed.shape[0]
    return jnp.pad(packed, ((0, -rows % 128), (0, 0)))


def _unpack_small(packed, shapes, cols):
    out = []
    row = 0
    for s in shapes:
        n = math.prod(s)
        rows = -(-n // (SUBLANES * cols)) * SUBLANES
        out.append(packed[row:row + rows].reshape(-1)[:n].reshape(s))
        row += rows
    return out


def kernel(x, mix_norm, mlp_norm, mlp_w1, mlp_w2, ssm_log_dt, ssm_a_re, ssm_a_im, ssm_b_re, ssm_b_im, ssm_c_re, ssm_c_im, ssm_d, ssm_w_glu, kv_norm, w_kvf, b_f, attn_wq, attn_wo, final_norm, loss_target, m_mix_norm, m_mlp_norm, m_mlp_w1, m_mlp_w2, m_ssm_log_dt, m_ssm_a_re, m_ssm_a_im, m_ssm_b_re, m_ssm_b_im, m_ssm_c_re, m_ssm_c_im, m_ssm_d, m_ssm_w_glu, m_kv_norm, m_w_kvf, m_b_f, m_attn_wq, m_attn_wo, m_final_norm, v_mix_norm, v_mlp_norm, v_mlp_w1, v_mlp_w2, v_ssm_log_dt, v_ssm_a_re, v_ssm_a_im, v_ssm_b_re, v_ssm_b_im, v_ssm_c_re, v_ssm_c_im, v_ssm_d, v_ssm_w_glu, v_kv_norm, v_w_kvf, v_b_f, v_attn_wq, v_attn_wo, v_final_norm):
    seq, d = x.shape[1], x.shape[2]
    depth = mix_norm.shape[0]
    n_a = ssm_log_dt.shape[0]
    n_b = depth - n_a
    ff = mlp_w1.shape[2] * N_CHIPS
    n_heads = d // HEAD_DIM
    n_groups = d // SSM_GROUP
    p_state = ssm_a_re.shape[2]
    gb = min(d, 256)
    nj = d // gb
    kvf_cols = w_kvf.shape[1]
    ds4, dq4 = d // N_CHIPS, d // (2 * N_CHIPS)
    chip = 2 * lax.axis_index("x") + lax.axis_index("y")

    def cols_of(width):
        return lambda ref, s: ref.at[:, :, pl.ds(pl.multiple_of(s * width, LANES), width)]

    def rows_of(height):
        return lambda ref, s: ref.at[:, pl.ds(pl.multiple_of(s * height, SUBLANES), height), :]

    whole = lambda ref, s: ref
    slot = lambda ref, s: ref.at[s]
    shards = [mlp_w1.astype(BF16), mlp_w2.astype(BF16), ssm_w_glu.astype(BF16), w_kvf.astype(BF16),
              attn_wq.astype(BF16), attn_wo.astype(BF16), ssm_d]
    gathered = _chip_exchange(
        shards,
        [((depth, d, ff), BF16), ((depth, ff, d), BF16), ((n_a, d, 2 * d), BF16), ((N_CHIPS, d, kvf_cols), BF16),
         ((n_b, d, d), BF16), ((n_b, d, d), BF16), ((N_CHIPS, n_a, ds4), F32)],
        [(0, 0, whole, cols_of(d)), (1, 1, whole, rows_of(d)), (2, 2, whole, cols_of(2 * ds4)), (3, 3, whole, slot),
         (4, 4, whole, rows_of(ds4)), (5, 5, whole, rows_of(ds4)), (6, 6, whole, slot)],
        "gather_weights")
    w1_all, w2_all, wglu_all, kvf_parts, wq_all, wo_all, skip_parts = gathered
    skip_all = skip_parts.transpose(1, 0, 2).reshape(n_a, d)
    kvf_all = jnp.concatenate([kvf_parts[s] for s in range(N_CHIPS)], axis=1)
    wk = kvf_all[:, :d]
    wv = kvf_all[:, d:2 * d]
    wf = jnp.pad(kvf_all[:, 2 * d:], ((0, 0), (0, LANES - n_heads)))
    bf_row = jnp.pad(b_f, (0, LANES - n_heads)).reshape(1, LANES)

    h = x[0]
    target = loss_target[0]

    saved = []
    for i in range(n_a):
        lbr, lbi, bbr, bbi = _ssm_discretise(ssm_log_dt[i], ssm_a_re[i], ssm_a_im[i], ssm_b_re[i], ssm_b_im[i])
        bblk = _block_diag_in(bbr, bbi, nj)
        cblk = _block_diag_out(ssm_c_re[i], ssm_c_im[i], nj)
        rec = dict(h0=h, lam=(lbr, lbi), bblk=bblk, cblk=cblk)
        u = _norm_fwd(h, mix_norm[i:i + 1], f"s5_norm_{i}")
        rec["u"] = u
        dskip = rec["dskip"] = skip_all[i:i + 1]
        states, y = _scan_fwd(u, bblk.astype(BF16), cblk.astype(BF16), _scan_tables(lbr, lbi, nj, False), dskip,
                              f"s5_scan_{i}")
        rec["states"], rec["y"] = states, y
        h, rec["zw"] = _s5_post_fwd(h, y, wglu_all[i], f"s5_glu_{i}")
        rec["h1"] = h
        h, rec["ap"] = _mlp_fwd(h, mlp_norm[i:i + 1], w1_all[i], w2_all[i], f"mlp_{i}")
        saved.append(rec)
    h_kv = h
    k, v, flog, cum = _kvf_fwd(h, kv_norm.reshape(1, d), wk, wv, wf, bf_row, "kvf")
    f2 = cum[:, :n_heads] * LOG2E
    f2_rows = f2.T.reshape(n_heads // 2, 2, seq)
    f2_rep = jnp.broadcast_to(f2[:, :, None], (seq, n_heads, HEAD_SLOT)).reshape(seq, n_heads * HEAD_SLOT)
    one_h = jnp.ones((seq, n_heads, 1), BF16)
    zero_h = jnp.zeros((seq, n_heads, 1), BF16)
    kx = _head_slots(k, jnp.concatenate([zero_h, one_h], axis=2))
    vx = _head_slots(v, one_h)
    for jb in range(n_b):
        i = n_a + jb
        rec = dict(h0=h)
        q = _q_fwd(h, mix_norm[i:i + 1], wq_all[jb], f"attn_q_{jb}")
        qx = _head_slots(q, one_h)
        o, lse = _flash_fwd(qx, kx, vx, f2_rows, f"attn_core_{jb}")
        rec["qx"], rec["o"], rec["lse"] = qx, o, lse
        h = _o_fwd(h, o, wo_all[jb], f"attn_out_{jb}")
        rec["h1"] = h
        h, rec["ap"] = _mlp_fwd(h, mlp_norm[i:i + 1], w1_all[i], w2_all[i], f"mlp_{i}")
        saved.append(rec)
    dh, loss_row, g_final = _loss_head(h, target, final_norm.reshape(1, d), "loss_head")
    loss = lax.psum(loss_row[0, 0], ("x", "y", "c"))

    head_ones = (jnp.arange(d)[:, None] // HEAD_DIM == jnp.arange(d)[None, :] // HEAD_DIM).astype(BF16)
    g_mix = [None] * depth
    g_mlp = [None] * depth
    g_w1 = [None] * depth
    g_w2 = [None] * depth
    g_wq = [None] * n_b
    g_wo = [None] * n_b
    g_glu = [None] * n_a
    g_ssm = [None] * n_a
    dk_parts, dv_parts, df_parts = [], [], []

    def head_rows(rep):
        return rep[:, ::HEAD_DIM].T.reshape(n_heads // 2, 2, seq)

    def mlp_back(dh, i, rec):
        dh_in, hm, a, dap, g_mlp[i] = _mlp_bwd(dh, rec["h1"], rec["ap"], mlp_norm[i:i + 1], w1_all[i], w2_all[i],
                                               f"mlp_bwd_{i}")
        g_w2[i] = _matmul_tn(a, dh, f"mlp_dw2_{i}")
        g_w1[i] = _matmul_tn(hm, dap, f"mlp_dw1_{i}")
        return dh_in

    for jb in reversed(range(n_b)):
        i = n_a + jb
        rec = saved[i]
        dh = mlp_back(dh, i, rec)
        do, delta = _o_bwd(dh, rec["o"], wo_all[jb], head_ones, f"attn_out_bwd_{jb}")
        g_wo[jb] = _matmul_tn(rec["o"], dh, f"attn_dwo_{jb}")
        dqx, dkx, dv = _flash_bwd(rec["qx"], kx, vx, f2_rep, do, head_rows(rec["lse"]), head_rows(delta),
                                  f"attn_core_bwd_{jb}")
        dqx = dqx.reshape(seq, n_heads, HEAD_SLOT)
        dkx = dkx.reshape(seq, n_heads, HEAD_SLOT)
        dq = dqx[:, :, :HEAD_DIM].reshape(seq, d)
        dk_parts.append(dkx[:, :, :HEAD_DIM].reshape(seq, d))
        dv_parts.append(dv)
        df_parts.append(dqx[:, :, LANE_ROWSUM_DS] - dkx[:, :, LANE_COLSUM_DS])
        dh, hn, dqs, g_mix[i] = _q_bwd(dh, rec["h0"], dq, mix_norm[i:i + 1], wq_all[jb], f"attn_q_bwd_{jb}")
        g_wq[jb] = _matmul_tn(hn, dqs, f"attn_dwq_{jb}")

    dft = df_parts[0]
    for extra in df_parts[1:]:
        dft = dft + extra
    dcum = jnp.pad(dft, ((0, 0), (0, LANES - n_heads)))
    dh, hk, dkb, dvb, dfb, g_kvn, g_bf = _kvf_bwd(dh, h_kv, dk_parts[0], dk_parts[1], dv_parts[0], dv_parts[1],
                                                  dcum, flog, kv_norm.reshape(1, d), wk, wv, wf, "kvf_bwd")
    g_kvf = jnp.concatenate([_matmul_tn(hk, dkb, "kvf_dwk"), _matmul_tn(hk, dvb, "kvf_dwv"),
                             _matmul_tn(hk, dfb, "kvf_dwf")[:, :n_heads]], axis=1)

    for i in reversed(range(n_a)):
        rec = saved[i]
        dh = mlp_back(dh, i, rec)
        dy, z, dzw = _s5_post_bwd(dh, rec["y"], rec["zw"], wglu_all[i], f"s5_glu_bwd_{i}")
        g_glu[i] = _matmul_tn(z, dzw, f"s5_dwglu_{i}")
        lbr, lbi = rec["lam"]
        bblk_t = rec["bblk"].transpose(0, 2, 1).astype(BF16)
        cblk_t = rec["cblk"].transpose(0, 2, 1).astype(BF16)
        du, glam8, gd8, gbblk, gcblk = _scan_bwd(dy, rec["u"], rec["states"], bblk_t, cblk_t,
                                                 _scan_tables(lbr, lbi, nj, True), rec["dskip"], f"s5_scan_bwd_{i}")
        dh, g_mix[i] = _norm_bwd_add(dh, du, rec["h0"], mix_norm[i:i + 1], f"s5_norm_bwd_{i}")
        glam = jnp.sum(glam8, axis=1)
        n_st = glam.shape[1] // 2
        g_lbr = glam[:, :n_st].reshape(n_groups, p_state)
        g_lbi = glam[:, n_st:].reshape(n_groups, p_state)
        g_bbr, g_bbi = _block_diag_in_grad(gbblk, nj, p_state, SSM_GROUP)
        g_cre, g_cim = _block_diag_out_grad(gcblk, nj, p_state, SSM_GROUP)
        _, pull = jax.vjp(_ssm_discretise, ssm_log_dt[i], ssm_a_re[i], ssm_a_im[i], ssm_b_re[i], ssm_b_im[i])
        g_ldt, g_are, g_aim, g_bre, g_bim = pull((g_lbr, g_lbi, g_bbr, g_bbi))
        g_ssm[i] = dict(log_dt=g_ldt, a_re=g_are, a_im=g_aim, b_re=g_bre, b_im=g_bim, c_re=g_cre, c_im=g_cim,
                        d=jnp.sum(gd8, axis=1).reshape(d))
    grad_x = dh[None]

    def stack_small(key):
        return jnp.stack([g_ssm[i][key] for i in range(n_a)])

    small_grads = [jnp.concatenate(g_mix, axis=0), jnp.concatenate(g_mlp, axis=0), stack_small("log_dt"),
                   stack_small("a_re"), stack_small("a_im"), stack_small("b_re"), stack_small("b_im"),
                   stack_small("c_re"), stack_small("c_im"), stack_small("d"), g_kvn.reshape(d),
                   g_bf[0, :n_heads], g_final.reshape(d)]
    small_w = [mix_norm, mlp_norm, ssm_log_dt, ssm_a_re, ssm_a_im, ssm_b_re, ssm_b_im, ssm_c_re, ssm_c_im,
               ssm_d, kv_norm, b_f, final_norm]
    small_m = [m_mix_norm, m_mlp_norm, m_ssm_log_dt, m_ssm_a_re, m_ssm_a_im, m_ssm_b_re, m_ssm_b_im, m_ssm_c_re,
               m_ssm_c_im, m_ssm_d, m_kv_norm, m_b_f, m_final_norm]
    small_v = [v_mix_norm, v_mlp_norm, v_ssm_log_dt, v_ssm_a_re, v_ssm_a_im, v_ssm_b_re, v_ssm_b_im, v_ssm_c_re,
               v_ssm_c_im, v_ssm_d, v_kv_norm, v_b_f, v_final_norm]
    skip_at = 9

    def widen_skip(part):
        return lax.dynamic_update_slice(jnp.zeros((n_a, d), F32), part, (0, chip * ds4))

    small_shapes = [a.shape for a in small_grads]
    pcols = 1024 if d >= 1024 else LANES
    g_small = _pack_small(small_grads, pcols)
    expand = lambda lst: _pack_small([widen_skip(a) if n == skip_at else a for n, a in enumerate(lst)], pcols)
    w_small, m_small, v_small = expand(small_w), expand(small_m), expand(small_v)
    srows = g_small.shape[0]

    kvf_send = g_kvf.reshape(d, N_CHIPS, kvf_cols).transpose(1, 0, 2)

    def cols2(width):
        return lambda ref, s: ref.at[:, pl.ds(pl.multiple_of(s * width, LANES), width)]

    def rows2(height):
        return lambda ref, s: ref.at[pl.ds(pl.multiple_of(s * height, SUBLANES), height), :]

    def into(layer):
        return lambda ref, s: ref.at[s, layer]

    assert n_b == 2
    red_srcs, red_plan = [], []

    def send(arr, out_index, src_view, dst_view):
        red_plan.append((len(red_srcs), out_index, src_view, dst_view))
        red_srcs.append(arr)

    for i in range(depth):
        send(g_w1[i], 0, cols2(d), into(i))
    for i in range(depth):
        send(g_w2[i], 1, rows2(d), into(i))
    for i in range(n_a):
        send(g_glu[i], 2, cols2(2 * ds4), into(i))
    send(kvf_send, 3, slot, slot)
    for jb in range(n_b):
        send(g_wq[jb], 4, rows2(ds4), into(jb))
    for jb in range(n_b):
        send(g_wo[jb], 5, rows2(ds4), into(jb))
    send(g_small, 6, whole, slot)
    received = _chip_exchange(
        red_srcs,
        [((N_CHIPS, depth, d, d), BF16), ((N_CHIPS, depth, d, d), BF16), ((N_CHIPS, n_a, d, 2 * ds4), BF16),
         ((N_CHIPS, d, kvf_cols), BF16), ((N_CHIPS, n_b, ds4, d), BF16), ((N_CHIPS, n_b, ds4, d), BF16),
         ((N_CHIPS, srows, pcols), F32)],
        red_plan, "reduce_chips")
    r_w1, r_w2, r_glu, r_kvf, r_wq, r_wo, r_small = received

    def flat(a):
        return a.reshape(N_CHIPS, -1, a.shape[-1])

    sums = [_sum_chips(flat(r), f"sum_chips_{n}") for n, r in
            enumerate((r_w1, r_w2, r_glu, r_kvf, r_wq, r_wo, r_small))]
    others = _core_exchange(sums, "reduce_cores")

    def two(a):
        return a.reshape(-1, a.shape[-1])

    big_w = [(mlp_w1, m_mlp_w1, v_mlp_w1), (mlp_w2, m_mlp_w2, v_mlp_w2), (ssm_w_glu, m_ssm_w_glu, v_ssm_w_glu),
             (w_kvf, m_w_kvf, v_w_kvf), (attn_wq, m_attn_wq, v_attn_wq), (attn_wo, m_attn_wo, v_attn_wo)]
    big_out = []
    for n, (w, m, v) in enumerate(big_w):
        res = _adamw(sums[n], others[n], two(w), two(m), two(v), f"adamw_{n}")
        big_out.append([r.reshape(w.shape) for r in res])
    small_out = _adamw(sums[6], others[6], w_small, m_small, v_small, "adamw_small")

    def narrow_skip(a):
        return lax.dynamic_slice(a, (0, chip * ds4), (n_a, ds4))

    unpacked = []
    for packed in small_out:
        parts = _unpack_small(packed, small_shapes, pcols)
        parts[skip_at] = narrow_skip(parts[skip_at])
        unpacked.append(parts)

    order = ["mix_norm", "mlp_norm", "mlp_w1", "mlp_w2", "ssm_log_dt", "ssm_a_re", "ssm_a_im", "ssm_b_re",
             "ssm_b_im", "ssm_c_re", "ssm_c_im", "ssm_d", "ssm_w_glu", "kv_norm", "w_kvf", "b_f", "attn_wq",
             "attn_wo", "final_norm"]
    small_names = ["mix_norm", "mlp_norm", "ssm_log_dt", "ssm_a_re", "ssm_a_im", "ssm_b_re", "ssm_b_im",
                   "ssm_c_re", "ssm_c_im", "ssm_d", "kv_norm", "b_f", "final_norm"]
    big_names = ["mlp_w1", "mlp_w2", "ssm_w_glu", "w_kvf", "attn_wq", "attn_wo"]
    outs = [loss, grad_x]
    for kind in range(4):
        for name in order:
            if name in big_names:
                outs.append(big_out[big_names.index(name)][kind])
            else:
                outs.append(unpacked[kind][small_names.index(name)])
    return tuple(outs)
```

```python
import functools
import math

import jax
import jax.numpy as jnp
from jax import lax
from jax.experimental import pallas as pl
from jax.experimental.pallas import tpu as pltpu

F32 = jnp.float32
BF16 = jnp.bfloat16

RMS_EPS = 1e-6
SSM_GROUP = 16
SSM_STATE = 64
HEAD_DIM = 64
HEAD_PAIR = 2 * HEAD_DIM
LANES = 128
SUBLANES = 8
N_CHIPS = 4
ADAM_LR = 0.001
ADAM_B1 = 0.9
ADAM_B2 = 0.999
ADAM_EPS = 1e-08
ADAM_WD = 0.01
ADAM_STEP = 10
GELU_C = math.sqrt(2.0 / math.pi)
GELU_A = 0.044715
NEG = -1e30
LN2 = math.log(2.0)
LOG2E = 1.0 / LN2
VMEM_LIMIT = 56 * 1024 * 1024
MESH = pl.DeviceIdType.MESH

NT_DIMS = (((1,), (1,)), ((), ()))
TN_DIMS = (((0,), (0,)), ((), ()))


def _cp(*sem):
    return pltpu.CompilerParams(dimension_semantics=sem if sem else None, vmem_limit_bytes=VMEM_LIMIT)


def _zero_idx(nd, *_):
    return (0,) * nd


def _tile(n, t):
    if n <= t:
        return n
    for cand in range(t - t % SUBLANES, 0, -SUBLANES):
        if n % cand == 0:
            return cand
    raise ValueError((n, t))


def _rms_fwd(h, g):
    r = lax.rsqrt(jnp.mean(h * h, axis=-1, keepdims=True) + RMS_EPS)
    hhat = h * r
    return hhat * g, hhat, r


def _rms_bwd(du, hhat, r, g):
    dhh = du * g
    dh = r * (dhh - hhat * jnp.mean(dhh * hhat, axis=-1, keepdims=True))
    return dh, du * hhat


def _sigmoid(x):
    return 1.0 / (1.0 + jnp.exp(-x))


def _gelu(x):
    t = jnp.tanh(GELU_C * (x + GELU_A * x * x * x))
    return 0.5 * x * (1.0 + t)


def _gelu_grad(x):
    t = jnp.tanh(GELU_C * (x + GELU_A * x * x * x))
    return 0.5 * (1.0 + t) + 0.5 * x * (1.0 - t * t) * GELU_C * (1.0 + 3.0 * GELU_A * x * x)


def _row_fold(x):
    tm, w = x.shape
    return jnp.sum(x.reshape(tm // SUBLANES, SUBLANES, w), axis=0)


def _split3(x):
    hi = x.astype(BF16)
    r1 = x - hi.astype(F32)
    mid = r1.astype(BF16)
    lo = (r1 - mid.astype(F32)).astype(BF16)
    return hi, mid, lo


def _exact_dot(ones_mat, x):
    hi, mid, lo = _split3(x)
    d = functools.partial(jnp.dot, preferred_element_type=F32)
    return d(ones_mat, hi) + d(ones_mat, mid) + d(ones_mat, lo)


def _rows_call(body, name, tm, row_ins, const_ins, row_outs, acc_outs=(), scratch=(), reverse=False):
    n = row_ins[0].shape[0]
    nb = n // tm
    if reverse:
        ridx = lambda i: (nb - 1 - i, 0)
    else:
        ridx = lambda i: (i, 0)
    in_specs = [pl.BlockSpec((tm, a.shape[1]), ridx) for a in row_ins]
    in_specs += [pl.BlockSpec(a.shape, functools.partial(_zero_idx, a.ndim), pipeline_mode=pl.Buffered(1))
                 for a in const_ins]
    out_shape = [jax.ShapeDtypeStruct((n, w), dt) for (w, dt) in row_outs]
    out_shape += [jax.ShapeDtypeStruct(s, dt) for (s, dt) in acc_outs]
    out_specs = [pl.BlockSpec((tm, w), ridx) for (w, dt) in row_outs]
    out_specs += [pl.BlockSpec(s, functools.partial(_zero_idx, len(s))) for (s, dt) in acc_outs]
    return pl.pallas_call(
        body, name=name, grid=(nb,), in_specs=in_specs, out_specs=out_specs, out_shape=out_shape,
        scratch_shapes=list(scratch), compiler_params=_cp("arbitrary"),
    )(*row_ins, *const_ins)


def _norm_fwd(h, g, name):
    n, d = h.shape
    tm = _tile(n, 512)

    def body(h_ref, g_ref, u_ref):
        u_ref[...] = _rms_fwd(h_ref[...], g_ref[...])[0]

    return _rows_call(body, name, tm, [h], [g], [(d, F32)])[0]


def _norm_bwd_add(dh, du, h, g, name):
    n, d = h.shape
    tm = _tile(n, 512)
    nb = n // tm

    def body(dh_ref, du_ref, h_ref, g_ref, o_ref, dg_ref, acc):
        i = pl.program_id(0)

        @pl.when(i == 0)
        def _():
            acc[...] = jnp.zeros_like(acc)

        gain = g_ref[...]
        _, hhat, r = _rms_fwd(h_ref[...], gain)
        dhn, dgr = _rms_bwd(du_ref[...], hhat, r, gain)
        o_ref[...] = dh_ref[...] + dhn
        acc[...] += _row_fold(dgr)

        @pl.when(i == nb - 1)
        def _():
            dg_ref[...] = jnp.sum(acc[...], axis=0, keepdims=True)

    return _rows_call(body, name, tm, [dh, du, h], [g], [(d, F32)], [((1, d), F32)],
                      [pltpu.VMEM((SUBLANES, d), F32)])


def _mlp_fwd(h, g, w1, w2, name):
    n, d = h.shape
    ff = w1.shape[1]
    tm = _tile(n, 256)
    fc = _tile(ff, 1024)

    def body(h_ref, g_ref, w1_ref, w2_ref, o_ref, ap_ref):
        hin = h_ref[...]
        hb = _rms_fwd(hin, g_ref[...])[0].astype(BF16)
        acc = hin
        for c in range(ff // fc):
            cs = slice(c * fc, (c + 1) * fc)
            ap = jnp.dot(hb, w1_ref[:, cs], preferred_element_type=F32)
            ap_ref[:, cs] = ap.astype(BF16)
            rl = jnp.maximum(ap, 0.0)
            acc = acc + jnp.dot((rl * rl).astype(BF16), w2_ref[cs, :], preferred_element_type=F32)
        o_ref[...] = acc

    return _rows_call(body, name, tm, [h], [g, w1, w2], [(d, F32), (ff, BF16)])


def _mlp_bwd(dh, h, ap, g, w1, w2, name):
    n, d = h.shape
    ff = w1.shape[1]
    tm = _tile(n, 256)
    nb = n // tm
    fc = _tile(ff, 1024)

    def body(dh_ref, h_ref, ap_ref, g_ref, w1_ref, w2_ref, o_ref, hm_ref, a_ref, dap_ref, dg_ref, acc):
        i = pl.program_id(0)

        @pl.when(i == 0)
        def _():
            acc[...] = jnp.zeros_like(acc)

        gain = g_ref[...]
        dhv = dh_ref[...]
        hm, hhat, r = _rms_fwd(h_ref[...], gain)
        hm_ref[...] = hm.astype(BF16)
        dhb = dhv.astype(BF16)
        dhm = jnp.zeros((tm, d), F32)
        for c in range(ff // fc):
            cs = slice(c * fc, (c + 1) * fc)
            rl = jnp.maximum(ap_ref[:, cs].astype(F32), 0.0)
            a_ref[:, cs] = (rl * rl).astype(BF16)
            da = lax.dot_general(dhb, w2_ref[cs, :], NT_DIMS, preferred_element_type=F32)
            dap = (da * (2.0 * rl)).astype(BF16)
            dap_ref[:, cs] = dap
            dhm = dhm + lax.dot_general(dap, w1_ref[:, cs], NT_DIMS, preferred_element_type=F32)
        dhn, dgr = _rms_bwd(dhm, hhat, r, gain)
        o_ref[...] = dhv + dhn
        acc[...] += _row_fold(dgr)

        @pl.when(i == nb - 1)
        def _():
            dg_ref[...] = jnp.sum(acc[...], axis=0, keepdims=True)

    return _rows_call(body, name, tm, [dh, h, ap], [g, w1, w2],
                      [(d, F32), (d, BF16), (ff, BF16), (ff, BF16)], [((1, d), F32)],
                      [pltpu.VMEM((SUBLANES, d), F32)])


def _s5_post_fwd(h, y, w_glu, name):
    n, d = h.shape
    tm = _tile(n, 512)

    def body(h_ref, y_ref, w_ref, o_ref, zw_ref):
        z = _gelu(y_ref[...]).astype(BF16)
        zw = jnp.dot(z, w_ref[...], preferred_element_type=F32)
        zw_ref[...] = zw.astype(BF16)
        o_ref[...] = h_ref[...] + zw[:, :d] * _sigmoid(zw[:, d:])

    return _rows_call(body, name, tm, [h, y], [w_glu], [(d, F32), (2 * d, BF16)])


def _s5_post_bwd(dh, y, zw, w_glu, name):
    n, d = dh.shape
    tm = _tile(n, 512)

    def body(dh_ref, y_ref, zw_ref, w_ref, dy_ref, z_ref, dzw_ref):
        dhv = dh_ref[...]
        yv = y_ref[...]
        val = zw_ref[:, :d].astype(F32)
        sg = _sigmoid(zw_ref[:, d:].astype(F32))
        dzw = jnp.concatenate([dhv * sg, dhv * val * sg * (1.0 - sg)], axis=1).astype(BF16)
        dzw_ref[...] = dzw
        dz = lax.dot_general(dzw, w_ref[...], NT_DIMS, preferred_element_type=F32)
        dy_ref[...] = dz * _gelu_grad(yv)
        z_ref[...] = _gelu(yv).astype(BF16)

    return _rows_call(body, name, tm, [dh, y, zw], [w_glu], [(d, F32), (d, BF16), (2 * d, BF16)])


def _q_fwd(h, g, wq_x, ones_x, name):
    n, d = h.shape
    tm = _tile(n, 512)
    scale = LOG2E * HEAD_DIM ** -0.5

    def body(h_ref, g_ref, w_ref, one_ref, q_ref):
        hb = _rms_fwd(h_ref[...], g_ref[...])[0].astype(BF16)
        q_ref[...] = (jnp.dot(hb, w_ref[...], preferred_element_type=F32) * scale + one_ref[...]).astype(BF16)

    return _rows_call(body, name, tm, [h], [g, wq_x, ones_x], [(wq_x.shape[1], BF16)])[0]


def _q_bwd(dh, h, dq, g, wq, name):
    n, d = h.shape
    tm = _tile(n, 512)
    nb = n // tm
    scale = HEAD_DIM ** -0.5

    def body(dh_ref, h_ref, dq_ref, g_ref, w_ref, o_ref, hn_ref, dqs_ref, dg_ref, acc):
        i = pl.program_id(0)

        @pl.when(i == 0)
        def _():
            acc[...] = jnp.zeros_like(acc)

        gain = g_ref[...]
        hn, hhat, r = _rms_fwd(h_ref[...], gain)
        hn_ref[...] = hn.astype(BF16)
        dqs = (dq_ref[...] * scale).astype(BF16)
        dqs_ref[...] = dqs
        dhn = lax.dot_general(dqs, w_ref[...], NT_DIMS, preferred_element_type=F32)
        dhi, dgr = _rms_bwd(dhn, hhat, r, gain)
        o_ref[...] = dh_ref[...] + dhi
        acc[...] += _row_fold(dgr)

        @pl.when(i == nb - 1)
        def _():
            dg_ref[...] = jnp.sum(acc[...], axis=0, keepdims=True)

    return _rows_call(body, name, tm, [dh, h, dq], [g, wq], [(d, F32), (d, BF16), (wq.shape[1], BF16)],
                      [((1, d), F32)], [pltpu.VMEM((SUBLANES, d), F32)])


def _o_fwd(h, o, wo, name):
    n, d = h.shape
    tm = _tile(n, 512)

    def body(h_ref, o_ref, w_ref, out_ref):
        out_ref[...] = h_ref[...] + jnp.dot(o_ref[...], w_ref[...], preferred_element_type=F32)

    return _rows_call(body, name, tm, [h, o], [wo], [(d, F32)])[0]


def _o_bwd(dh, o, wo, head_ones, name):
    n, d = dh.shape
    tm = _tile(n, 512)

    def body(dh_ref, o_ref, w_ref, e_ref, do_ref, dl_ref):
        do = lax.dot_general(dh_ref[...].astype(BF16), w_ref[...], NT_DIMS, preferred_element_type=F32).astype(BF16)
        do_ref[...] = do
        dl_ref[...] = _exact_dot_rhs(do.astype(F32) * o_ref[...].astype(F32), e_ref[...])

    return _rows_call(body, name, tm, [dh, o], [wo, head_ones], [(wo.shape[0], BF16), (wo.shape[0], F32)])


def _exact_dot_rhs(x, ones_mat):
    hi, mid, lo = _split3(x)
    d = functools.partial(jnp.dot, preferred_element_type=F32)
    return d(hi, ones_mat) + d(mid, ones_mat) + d(lo, ones_mat)


def _kvf_fwd(h, g, wk, wv, wf, bf, k_ones, v_ones, name):
    n, d = h.shape
    tm = _tile(n, 512)

    def body(h_ref, g_ref, wk_ref, wv_ref, wf_ref, bf_ref, ko_ref, vo_ref, k_ref, v_ref, fl_ref, cum_ref, carry):
        i = pl.program_id(0)

        @pl.when(i == 0)
        def _():
            carry[...] = jnp.zeros_like(carry)

        hb = _rms_fwd(h_ref[...], g_ref[...])[0].astype(BF16)
        k_ref[...] = (jnp.dot(hb, wk_ref[...], preferred_element_type=F32) + ko_ref[...]).astype(BF16)
        v_ref[...] = (jnp.dot(hb, wv_ref[...], preferred_element_type=F32) + vo_ref[...]).astype(BF16)
        fl = jnp.dot(hb, wf_ref[...], preferred_element_type=F32) + bf_ref[...]
        fl_ref[...] = fl
        logf = jnp.minimum(fl, 0.0) - jnp.log(1.0 + jnp.exp(-jnp.abs(fl)))
        rows = lax.broadcasted_iota(jnp.int32, (tm, tm), 0)
        cols = lax.broadcasted_iota(jnp.int32, (tm, tm), 1)
        lower = (rows >= cols).astype(BF16)
        cum = _exact_dot(lower, logf) + carry[0:1, :]
        cum_ref[...] = cum
        carry[...] = jnp.broadcast_to(cum[tm - 1:tm, :], carry.shape)

    return _rows_call(body, name, tm, [h], [g, wk, wv, wf, bf, k_ones, v_ones],
                      [(wk.shape[1], BF16), (wv.shape[1], BF16), (LANES, F32), (LANES, F32)],
                      scratch=[pltpu.VMEM((SUBLANES, LANES), F32)])


def _kvf_bwd(dh, h, dk1, dk2, dv1, dv2, dcum, fl, g, wk, wv, wf, name):
    n, d = h.shape
    tm = _tile(n, 512)
    nb = n // tm

    def body(dh_ref, h_ref, dk1_ref, dk2_ref, dv1_ref, dv2_ref, dc_ref, fl_ref, g_ref, wk_ref, wv_ref, wf_ref,
             o_ref, hk_ref, dk_ref, dv_ref, df_ref, dg_ref, db_ref, acc, bacc, carry):
        i = pl.program_id(0)

        @pl.when(i == 0)
        def _():
            acc[...] = jnp.zeros_like(acc)
            bacc[...] = jnp.zeros_like(bacc)
            carry[...] = jnp.zeros_like(carry)

        rows = lax.broadcasted_iota(jnp.int32, (tm, tm), 0)
        cols = lax.broadcasted_iota(jnp.int32, (tm, tm), 1)
        upper = (rows <= cols).astype(BF16)
        dlogf = _exact_dot(upper, dc_ref[...]) + carry[0:1, :]
        carry[...] = jnp.broadcast_to(dlogf[0:1, :], carry.shape)
        df = dlogf / (1.0 + jnp.exp(fl_ref[...]))
        dfb = df.astype(BF16)
        df_ref[...] = dfb
        bacc[...] += _row_fold(df)
        dkb = ((dk1_ref[...] + dk2_ref[...]) * LN2).astype(BF16)
        dvb = (dv1_ref[...] + dv2_ref[...]).astype(BF16)
        dk_ref[...] = dkb
        dv_ref[...] = dvb
        gain = g_ref[...]
        hk, hhat, r = _rms_fwd(h_ref[...], gain)
        hk_ref[...] = hk.astype(BF16)
        dhk = lax.dot_general(dkb, wk_ref[...], NT_DIMS, preferred_element_type=F32)
        dhk = dhk + lax.dot_general(dvb, wv_ref[...], NT_DIMS, preferred_element_type=F32)
        dhk = dhk + lax.dot_general(dfb, wf_ref[...], NT_DIMS, preferred_element_type=F32)
        dhi, dgr = _rms_bwd(dhk, hhat, r, gain)
        o_ref[...] = dh_ref[...] + dhi
        acc[...] += _row_fold(dgr)

        @pl.when(i == nb - 1)
        def _():
            dg_ref[...] = jnp.sum(acc[...], axis=0, keepdims=True)
            db_ref[...] = jnp.sum(bacc[...], axis=0, keepdims=True)

    return _rows_call(body, name, tm, [dh, h, dk1, dk2, dv1, dv2, dcum, fl], [g, wk, wv, wf],
                      [(d, F32), (d, BF16), (wk.shape[1], BF16), (wv.shape[1], BF16), (LANES, BF16)],
                      [((1, d), F32), ((1, LANES), F32)],
                      [pltpu.VMEM((SUBLANES, d), F32), pltpu.VMEM((SUBLANES, LANES), F32),
                       pltpu.VMEM((SUBLANES, LANES), F32)], reverse=True)


def _loss_head(h, target, g, name):
    n, d = h.shape
    tm = _tile(n, 512)
    nb = n // tm

    def body(h_ref, t_ref, g_ref, dh_ref, loss_ref, dg_ref, lacc, gacc):
        i = pl.program_id(0)

        @pl.when(i == 0)
        def _():
            lacc[...] = jnp.zeros_like(lacc)
            gacc[...] = jnp.zeros_like(gacc)

        gain = g_ref[...]
        yv, hhat, r = _rms_fwd(h_ref[...], gain)
        e = yv - t_ref[...]
        lacc[...] += _row_fold(e * e)
        dhv, dgr = _rms_bwd(e * (1.0 / d), hhat, r, gain)
        dh_ref[...] = dhv
        gacc[...] += _row_fold(dgr)

        @pl.when(i == nb - 1)
        def _():
            loss_ref[...] = jnp.full((1, LANES), jnp.sum(lacc[...]) * (0.5 / d), F32)
            dg_ref[...] = jnp.sum(gacc[...], axis=0, keepdims=True)

    return _rows_call(body, name, tm, [h, target], [g], [(d, F32)], [((1, LANES), F32), ((1, d), F32)],
                      [pltpu.VMEM((SUBLANES, d), F32), pltpu.VMEM((SUBLANES, d), F32)])


def _matmul_tn(a, b, name, out_dtype=BF16):
    l, m = a.shape
    n = b.shape[1]
    tl = _tile(l, 1024)
    tmm = _tile(m, 512)
    tn = _tile(n, 1024)
    nl = l // tl

    def body(a_ref, b_ref, o_ref, acc):
        k = pl.program_id(2)

        @pl.when(k == 0)
        def _():
            acc[...] = jnp.zeros_like(acc)

        acc[...] += lax.dot_general(a_ref[...].astype(BF16), b_ref[...].astype(BF16), TN_DIMS,
                                    preferred_element_type=F32)

        @pl.when(k == nl - 1)
        def _():
            o_ref[...] = acc[...].astype(out_dtype)

    return pl.pallas_call(
        body, name=name, grid=(m // tmm, n // tn, nl),
        in_specs=[pl.BlockSpec((tl, tmm), lambda i, j, k: (k, i)), pl.BlockSpec((tl, tn), lambda i, j, k: (k, j))],
        out_specs=pl.BlockSpec((tmm, tn), lambda i, j, k: (i, j)),
        out_shape=jax.ShapeDtypeStruct((m, n), out_dtype),
        scratch_shapes=[pltpu.VMEM((tmm, tn), F32)],
        compiler_params=_cp("parallel", "parallel", "arbitrary"),
    )(a, b)


def _scan_fwd(u, bblk, cblk, tabs, dskip, name):
    l, d = u.shape
    nj, gb, n2 = bblk.shape
    n = n2 // 2
    tm = _tile(l, 512)
    nb = l // tm

    def body(u_ref, b_ref, c_ref, t_ref, d_ref, st_ref, y_ref, carry):
        i = pl.program_id(1)

        @pl.when(i == 0)
        def _():
            carry[...] = jnp.zeros_like(carry)

        uf = u_ref[...]
        st_ref[...] = jnp.dot(uf.astype(BF16), b_ref[0], preferred_element_type=F32)

        def step(rb, c):
            cr, ci = c
            rows = pl.ds(pl.multiple_of(rb * SUBLANES, SUBLANES), SUBLANES)
            xr = st_ref[rows, 0:n]
            xi = st_ref[rows, n:n2]
            for lvl, sh in enumerate((1, 2, 4)):
                ar = t_ref[0, 2 * lvl]
                ai = t_ref[0, 2 * lvl + 1]
                sr = pltpu.roll(xr, sh, 0)
                si = pltpu.roll(xi, sh, 0)
                xr, xi = xr + ar * sr - ai * si, xi + ar * si + ai * sr
            lr = t_ref[0, 6]
            li = t_ref[0, 7]
            xr, xi = xr + lr * cr - li * ci, xi + lr * ci + li * cr
            st_ref[rows, 0:n] = xr
            st_ref[rows, n:n2] = xi
            return (jnp.broadcast_to(xr[SUBLANES - 1:SUBLANES, :], (SUBLANES, n)),
                    jnp.broadcast_to(xi[SUBLANES - 1:SUBLANES, :], (SUBLANES, n)))

        cr, ci = lax.fori_loop(0, tm // SUBLANES, step, (carry[:, 0:n], carry[:, n:n2]))
        carry[:, 0:n] = cr
        carry[:, n:n2] = ci
        y_ref[...] = jnp.dot(st_ref[...].astype(BF16), c_ref[0], preferred_element_type=F32) + d_ref[...] * uf

    return pl.pallas_call(
        body, name=name, grid=(nj, nb),
        in_specs=[pl.BlockSpec((tm, gb), lambda j, i: (i, j)),
                  pl.BlockSpec((1, gb, n2), lambda j, i: (j, 0, 0)),
                  pl.BlockSpec((1, n2, gb), lambda j, i: (j, 0, 0)),
                  pl.BlockSpec((1, 8, SUBLANES, n), lambda j, i: (j, 0, 0, 0)),
                  pl.BlockSpec((1, gb), lambda j, i: (0, j))],
        out_specs=[pl.BlockSpec((tm, n2), lambda j, i: (i, j)), pl.BlockSpec((tm, gb), lambda j, i: (i, j))],
        out_shape=[jax.ShapeDtypeStruct((l, nj * n2), F32), jax.ShapeDtypeStruct((l, d), F32)],
        scratch_shapes=[pltpu.VMEM((SUBLANES, n2), F32)],
        compiler_params=_cp("parallel", "arbitrary"),
    )(u, bblk, cblk, tabs, dskip)


def _scan_bwd(dy, u, states, bblk_t, cblk_t, tabs, dskip, name):
    l, d = u.shape
    nj, n2, gb = bblk_t.shape
    n = n2 // 2
    tm = _tile(l, 512)
    nb = l // tm
    nr = tm // SUBLANES

    def body(dy_ref, u_ref, st_ref, prev_ref, bt_ref, ct_ref, t_ref, d_ref,
             du_ref, glam_ref, gd_ref, gb_ref, gc_ref, gx, carry):
        i = pl.program_id(1)
        ib = nb - 1 - i

        @pl.when(i == 0)
        def _():
            carry[...] = jnp.zeros_like(carry)
            glam_ref[...] = jnp.zeros_like(glam_ref)
            gd_ref[...] = jnp.zeros_like(gd_ref)
            gb_ref[...] = jnp.zeros_like(gb_ref)
            gc_ref[...] = jnp.zeros_like(gc_ref)

        dyv = dy_ref[...]
        uv = u_ref[...]
        dyb = dyv.astype(BF16)
        gx[...] = jnp.dot(dyb, ct_ref[0], preferred_element_type=F32)
        last_row = lax.broadcasted_iota(jnp.int32, (SUBLANES, n), 0) == SUBLANES - 1

        def block(rows, xp_r, xp_i, c):
            cr, ci = c
            gr = gx[rows, 0:n]
            gi = gx[rows, n:n2]
            for lvl, sh in enumerate((1, 2, 4)):
                ar = t_ref[0, 2 * lvl]
                ai = t_ref[0, 2 * lvl + 1]
                sr = pltpu.roll(gr, SUBLANES - sh, 0)
                si = pltpu.roll(gi, SUBLANES - sh, 0)
                gr, gi = gr + ar * sr - ai * si, gi + ar * si + ai * sr
            lr = t_ref[0, 6]
            li = t_ref[0, 7]
            gr, gi = gr + lr * cr - li * ci, gi + lr * ci + li * cr
            gx[rows, 0:n] = gr
            gx[rows, n:n2] = gi
            xs_r = pltpu.roll(jnp.where(last_row, xp_r, st_ref[rows, 0:n]), 1, 0)
            xs_i = pltpu.roll(jnp.where(last_row, xp_i, st_ref[rows, n:n2]), 1, 0)
            glam_ref[0, :, 0:n] += gr * xs_r + gi * xs_i
            glam_ref[0, :, n:n2] += gi * xs_r - gr * xs_i
            return (jnp.broadcast_to(gr[0:1, :], (SUBLANES, n)), jnp.broadcast_to(gi[0:1, :], (SUBLANES, n)))

        def step(k, c):
            rb = nr - 1 - k
            rows = pl.ds(pl.multiple_of(rb * SUBLANES, SUBLANES), SUBLANES)
            before = pl.ds(pl.multiple_of(rb * SUBLANES - SUBLANES, SUBLANES), SUBLANES)
            return block(rows, st_ref[before, 0:n], st_ref[before, n:n2], c)

        c = lax.fori_loop(0, nr - 1, step, (carry[:, 0:n], carry[:, n:n2]))
        live = (ib > 0).astype(F32)
        cr, ci = block(pl.ds(0, SUBLANES), prev_ref[:, 0:n] * live, prev_ref[:, n:n2] * live, c)
        carry[:, 0:n] = cr
        carry[:, n:n2] = ci

        gxb = gx[...].astype(BF16)
        du_ref[...] = jnp.dot(gxb, bt_ref[0], preferred_element_type=F32) + d_ref[...] * dyv
        gd_ref[0] += _row_fold(dyv * uv)
        gb_ref[0] += lax.dot_general(uv.astype(BF16), gxb, TN_DIMS, preferred_element_type=F32)
        gc_ref[0] += lax.dot_general(st_ref[...].astype(BF16), dyb, TN_DIMS, preferred_element_type=F32)

    rpb = tm // SUBLANES
    return pl.pallas_call(
        body, name=name, grid=(nj, nb),
        in_specs=[pl.BlockSpec((tm, gb), lambda j, i: (nb - 1 - i, j)),
                  pl.BlockSpec((tm, gb), lambda j, i: (nb - 1 - i, j)),
                  pl.BlockSpec((tm, n2), lambda j, i: (nb - 1 - i, j)),
                  pl.BlockSpec((SUBLANES, n2), lambda j, i: (jnp.maximum((nb - 1 - i) * rpb - 1, 0), j)),
                  pl.BlockSpec((1, n2, gb), lambda j, i: (j, 0, 0)),
                  pl.BlockSpec((1, gb, n2), lambda j, i: (j, 0, 0)),
                  pl.BlockSpec((1, 8, SUBLANES, n), lambda j, i: (j, 0, 0, 0)),
                  pl.BlockSpec((1, gb), lambda j, i: (0, j))],
        out_specs=[pl.BlockSpec((tm, gb), lambda j, i: (nb - 1 - i, j)),
                   pl.BlockSpec((1, SUBLANES, n2), lambda j, i: (j, 0, 0)),
                   pl.BlockSpec((1, SUBLANES, gb), lambda j, i: (j, 0, 0)),
                   pl.BlockSpec((1, gb, n2), lambda j, i: (j, 0, 0)),
                   pl.BlockSpec((1, n2, gb), lambda j, i: (j, 0, 0))],
        out_shape=[jax.ShapeDtypeStruct((l, d), F32),
                   jax.ShapeDtypeStruct((nj, SUBLANES, n2), F32),
                   jax.ShapeDtypeStruct((nj, SUBLANES, gb), F32),
                   jax.ShapeDtypeStruct((nj, gb, n2), F32),
                   jax.ShapeDtypeStruct((nj, n2, gb), F32)],
        scratch_shapes=[pltpu.VMEM((tm, n2), F32), pltpu.VMEM((SUBLANES, n2), F32)],
        compiler_params=_cp("parallel", "arbitrary"),
    )(dy, u, states, states, bblk_t, cblk_t, tabs, dskip)


HEAD_SLOT = 128
PAIR_SLOT = 2 * HEAD_SLOT
LANE_ROWSUM_P = HEAD_DIM
LANE_COLSUM_DS = HEAD_DIM
LANE_ROWSUM_DS = HEAD_DIM + 1


def _slot_cols(w):
    r, c = w.shape
    nh = c // HEAD_DIM
    return jnp.pad(w.reshape(r, nh, HEAD_DIM), ((0, 0), (0, 0), (0, HEAD_SLOT - HEAD_DIM))).reshape(r, nh * HEAD_SLOT)


def _unslot_cols(w):
    r, c = w.shape
    nh = c // HEAD_SLOT
    return w.reshape(r, nh, HEAD_SLOT)[:, :, :HEAD_DIM].reshape(r, nh * HEAD_DIM)


def _slot_ones(nh, lane):
    return jnp.tile((jnp.arange(HEAD_SLOT) == lane).astype(F32), nh).reshape(1, nh * HEAD_SLOT)


def _causal_tiles(n, by_query):
    if by_query:
        tiles = [(i, j) for i in range(n) for j in range(i + 1)]
    else:
        tiles = [(i, j) for j in range(n) for i in range(j, n)]
    return (jnp.asarray([t[0] for t in tiles], jnp.int32), jnp.asarray([t[1] for t in tiles], jnp.int32))


def _flash_fwd(qx, kx, vx, f2_rows, name):
    l = qx.shape[0]
    npair = qx.shape[1] // PAIR_SLOT
    d = npair * HEAD_PAIR
    tq = _tile(l, 512)
    tk = tq
    i_of, j_of = _causal_tiles(l // tq, by_query=True)

    def body(i_ref, j_ref, q_ref, k_ref, v_ref, f_ref, o_ref, lse_ref, m_sc, acc_sc):
        t = pl.program_id(1)
        i = i_ref[t]
        j = j_ref[t]

        @pl.when(j == 0)
        def _():
            m_sc[...] = jnp.full_like(m_sc, NEG)
            acc_sc[...] = jnp.zeros_like(acc_sc)

        def tile(on_diagonal):
            for hh in range(2):
                hs = slice(hh * HEAD_SLOT, (hh + 1) * HEAD_SLOT)
                s = lax.dot_general(q_ref[:, hs], k_ref[:, hs], NT_DIMS, preferred_element_type=F32)
                s = s - f_ref[0, hh:hh + 1, :]
                if on_diagonal:
                    keep = (lax.broadcasted_iota(jnp.int32, (tq, tk), 0)
                            >= lax.broadcasted_iota(jnp.int32, (tq, tk), 1))
                    s = jnp.where(keep, s, NEG)
                m_prev = m_sc[hh]
                m_new = jnp.maximum(m_prev, jnp.max(s, axis=-1, keepdims=True))
                alpha = jnp.exp2(m_prev - m_new)
                p = jnp.exp2(s - jnp.concatenate([m_new] * (tk // LANES), axis=1)).astype(BF16)
                acc_sc[hh] = alpha * acc_sc[hh] + jnp.dot(p, v_ref[:, hs], preferred_element_type=F32)
                m_sc[hh] = m_new

        @pl.when(j < i)
        def _():
            tile(False)

        @pl.when(j == i)
        def _():
            tile(True)
            outs, lses = [], []
            for hh in range(2):
                acc = acc_sc[hh]
                lsum = acc[:, LANE_ROWSUM_P:LANE_ROWSUM_P + 1]
                outs.append(acc[:, :HEAD_DIM] / lsum)
                lses.append(jnp.broadcast_to(m_sc[hh][:, 0:1] + jnp.log2(lsum), (tq, HEAD_DIM)))
            o_ref[...] = jnp.concatenate(outs, axis=1).astype(BF16)
            lse_ref[...] = jnp.concatenate(lses, axis=1)

    q_map = lambda h, t, i_ref, j_ref: (i_ref[t], h)
    kv_map = lambda h, t, i_ref, j_ref: (j_ref[t], h)
    return pl.pallas_call(
        body, name=name,
        grid_spec=pltpu.PrefetchScalarGridSpec(
            num_scalar_prefetch=2, grid=(npair, i_of.shape[0]),
            in_specs=[pl.BlockSpec((tq, PAIR_SLOT), q_map), pl.BlockSpec((tk, PAIR_SLOT), kv_map),
                      pl.BlockSpec((tk, PAIR_SLOT), kv_map),
                      pl.BlockSpec((1, 2, tk), lambda h, t, i_ref, j_ref: (h, 0, j_ref[t]))],
            out_specs=[pl.BlockSpec((tq, HEAD_PAIR), q_map), pl.BlockSpec((tq, HEAD_PAIR), q_map)],
            scratch_shapes=[pltpu.VMEM((2, tq, LANES), F32), pltpu.VMEM((2, tq, HEAD_SLOT), F32)]),
        out_shape=[jax.ShapeDtypeStruct((l, d), BF16), jax.ShapeDtypeStruct((l, d), F32)],
        compiler_params=_cp("parallel", "arbitrary"),
    )(i_of, j_of, qx, kx, vx, f2_rows)


def _flash_bwd(qx, kx, vx, f2_rep, do, lse_rows, delta_rows, name):
    l = qx.shape[0]
    npair = qx.shape[1] // PAIR_SLOT
    d = npair * HEAD_PAIR
    tq = _tile(l, 512)
    tk = tq
    i_of, j_of = _causal_tiles(l // tq, by_query=False)

    def body(i_ref, j_ref, q_ref, k_ref, v_ref, f_ref, do_ref, lse_ref, dl_ref, dq_ref, dk_ref, dv_ref):
        t = pl.program_id(1)
        i = i_ref[t]
        j = j_ref[t]

        @pl.when(t == 0)
        def _():
            dq_ref[...] = jnp.zeros_like(dq_ref)

        @pl.when(i == j)
        def _():
            dk_ref[...] = jnp.zeros_like(dk_ref)
            dv_ref[...] = jnp.zeros_like(dv_ref)

        def tile(on_diagonal):
            dqs, dks, dvs = [], [], []
            for hh in range(2):
                hs = slice(hh * HEAD_SLOT, (hh + 1) * HEAD_SLOT)
                qh, kh = q_ref[:, hs], k_ref[:, hs]
                vh = v_ref[:, hh * HEAD_SLOT:hh * HEAD_SLOT + HEAD_DIM]
                doh = do_ref[:, hh * HEAD_DIM:(hh + 1) * HEAD_DIM]
                st = lax.dot_general(kh, qh, NT_DIMS, preferred_element_type=F32)
                st = st - jnp.concatenate([f_ref[:, hs]] * (tq // HEAD_SLOT), axis=1)
                pt = jnp.exp2(st - lse_ref[0, hh:hh + 1, :])
                if on_diagonal:
                    keep = (lax.broadcasted_iota(jnp.int32, (tk, tq), 1)
                            >= lax.broadcasted_iota(jnp.int32, (tk, tq), 0))
                    pt = jnp.where(keep, pt, 0.0)
                dpt = lax.dot_general(vh, doh, NT_DIMS, preferred_element_type=F32)
                dsb = (pt * (dpt - dl_ref[0, hh:hh + 1, :])).astype(BF16)
                dvs.append(jnp.dot(pt.astype(BF16), doh, preferred_element_type=F32))
                dks.append(jnp.dot(dsb, qh, preferred_element_type=F32))
                dqs.append(lax.dot_general(dsb, kh, TN_DIMS, preferred_element_type=F32))
            dv_ref[...] += jnp.concatenate(dvs, axis=1)
            dk_ref[...] += jnp.concatenate(dks, axis=1)
            dq_ref[pl.ds(pl.multiple_of(i * tq, tq), tq), :] += jnp.concatenate(dqs, axis=1)

        @pl.when(i > j)
        def _():
            tile(False)

        @pl.when(i == j)
        def _():
            tile(True)

    qmap = lambda h, t, i_ref, j_ref: (i_ref[t], h)
    kmap = lambda h, t, i_ref, j_ref: (j_ref[t], h)
    row_map = lambda h, t, i_ref, j_ref: (h, 0, i_ref[t])
    return pl.pallas_call(
        body, name=name,
        grid_spec=pltpu.PrefetchScalarGridSpec(
            num_scalar_prefetch=2, grid=(npair, i_of.shape[0]),
            in_specs=[pl.BlockSpec((tq, PAIR_SLOT), qmap), pl.BlockSpec((tk, PAIR_SLOT), kmap),
                      pl.BlockSpec((tk, PAIR_SLOT), kmap), pl.BlockSpec((tk, PAIR_SLOT), kmap),
                      pl.BlockSpec((tq, HEAD_PAIR), qmap),
                      pl.BlockSpec((1, 2, tq), row_map), pl.BlockSpec((1, 2, tq), row_map)],
            out_specs=[pl.BlockSpec((l, PAIR_SLOT), lambda h, t, i_ref, j_ref: (0, h)),
                       pl.BlockSpec((tk, PAIR_SLOT), kmap), pl.BlockSpec((tk, HEAD_PAIR), kmap)]),
        out_shape=[jax.ShapeDtypeStruct((l, npair * PAIR_SLOT), F32), jax.ShapeDtypeStruct((l, npair * PAIR_SLOT), F32),
                   jax.ShapeDtypeStruct((l, d), F32)],
        compiler_params=_cp("parallel", "arbitrary"),
    )(i_of, j_of, qx, kx, vx, f2_rep, do, lse_rows, delta_rows)


def _my_place():
    return lax.axis_index("x"), lax.axis_index("y"), lax.axis_index("c")


def _chip_exchange(srcs, out_meta, plan, name):
    n_src, n_out, n_plan = len(srcs), len(out_meta), len(plan)

    def body(*refs):
        src_refs = refs[:n_src]
        out_refs = refs[n_src:n_src + n_out]
        send_sems, recv_sems, local_sems = refs[n_src + n_out:]
        x, y, c = _my_place()
        me = 2 * x + y
        copies = []
        for n, (si, oi, src_view, dst_view) in enumerate(plan):
            local = pltpu.make_async_copy(src_view(src_refs[si], me), dst_view(out_refs[oi], me), local_sems.at[n])
            local.start()
            copies.append(local)
            for k in (1, 2, 3):
                peer = me ^ k
                rc = pltpu.make_async_remote_copy(
                    src_ref=src_view(src_refs[si], peer), dst_ref=dst_view(out_refs[oi], me),
                    send_sem=send_sems.at[n, k - 1], recv_sem=recv_sems.at[n, k - 1],
                    device_id=(peer >> 1, peer & 1, c), device_id_type=MESH)
                rc.start()
                copies.append(rc)
        for cp in copies:
            cp.wait()

    any_spec = pl.BlockSpec(memory_space=pl.ANY)
    return pl.pallas_call(
        body, name=name,
        in_specs=[any_spec] * n_src, out_specs=[any_spec] * n_out,
        out_shape=[jax.ShapeDtypeStruct(shape, dt) for (shape, dt) in out_meta],
        scratch_shapes=[pltpu.SemaphoreType.DMA((n_plan, 3)), pltpu.SemaphoreType.DMA((n_plan, 3)),
                        pltpu.SemaphoreType.DMA((n_plan,))],
    )(*srcs)


def _core_exchange(arrays, name):
    n_items = len(arrays)

    def body(*refs):
        srcs = refs[:n_items]
        outs = refs[n_items:2 * n_items]
        send_sems, recv_sems = refs[2 * n_items:]
        x, y, c = _my_place()
        copies = []
        for n in range(n_items):
            rc = pltpu.make_async_remote_copy(
                src_ref=srcs[n], dst_ref=outs[n], send_sem=send_sems.at[n], recv_sem=recv_sems.at[n],
                device_id=(x, y, 1 - c), device_id_type=MESH)
            rc.start()
            copies.append(rc)
        for cp in copies:
            cp.wait()

    any_spec = pl.BlockSpec(memory_space=pl.ANY)
    return pl.pallas_call(
        body, name=name,
        in_specs=[any_spec] * n_items, out_specs=[any_spec] * n_items,
        out_shape=[jax.ShapeDtypeStruct(a.shape, a.dtype) for a in arrays],
        scratch_shapes=[pltpu.SemaphoreType.DMA((n_items,)), pltpu.SemaphoreType.DMA((n_items,))],
    )(*arrays)


def _sum_chips(parts, name):
    _, rows, cols = parts.shape
    tm = _tile(rows, 512)

    def body(p_ref, o_ref):
        acc = p_ref[0].astype(F32)
        for s in range(1, N_CHIPS):
            acc = acc + p_ref[s].astype(F32)
        o_ref[...] = acc

    return pl.pallas_call(
        body, name=name, grid=(rows // tm,),
        in_specs=[pl.BlockSpec((N_CHIPS, tm, cols), lambda i: (0, i, 0))],
        out_specs=pl.BlockSpec((tm, cols), lambda i: (i, 0)),
        out_shape=jax.ShapeDtypeStruct((rows, cols), F32),
        compiler_params=_cp("parallel"),
    )(parts)


def _adamw(ga, gb, w, m, v, name):
    rows, cols = w.shape
    tm = _tile(rows, 512)
    c1 = 1.0 - ADAM_B1 ** ADAM_STEP
    c2 = 1.0 - ADAM_B2 ** ADAM_STEP

    def body(ga_ref, gb_ref, w_ref, m_ref, v_ref, g_ref, d_ref, nm_ref, nv_ref):
        g = ga_ref[...] + gb_ref[...]
        nm = ADAM_B1 * m_ref[...] + (1.0 - ADAM_B1) * g
        nv = ADAM_B2 * v_ref[...] + (1.0 - ADAM_B2) * (g * g)
        g_ref[...] = g
        nm_ref[...] = nm
        nv_ref[...] = nv
        d_ref[...] = -ADAM_LR * ((nm / c1) / (jnp.sqrt(nv / c2) + ADAM_EPS) + ADAM_WD * w_ref[...])

    spec = pl.BlockSpec((tm, cols), lambda i: (i, 0))
    return pl.pallas_call(
        body, name=name, grid=(rows // tm,), in_specs=[spec] * 5, out_specs=[spec] * 4,
        out_shape=[jax.ShapeDtypeStruct((rows, cols), F32)] * 4, compiler_params=_cp("parallel"),
    )(ga, gb, w, m, v)


def _ssm_discretise(log_dt, a_re, a_im, b_re, b_im):
    dt = jnp.exp(log_dt)[:, None]
    mag = jnp.exp(a_re * dt)
    lbr = mag * jnp.cos(a_im * dt)
    lbi = mag * jnp.sin(a_im * dt)
    den = a_re * a_re + a_im * a_im
    nr, ni = lbr - 1.0, lbi
    qr = (nr * a_re + ni * a_im) / den
    qi = (ni * a_re - nr * a_im) / den
    bbr = qr[..., None] * b_re - qi[..., None] * b_im
    bbi = qr[..., None] * b_im + qi[..., None] * b_re
    return lbr, lbi, bbr, bbi


def _cmul(ar, ai, br, bi):
    return ar * br - ai * bi, ar * bi + ai * br


def _scan_tables(lr, li, nj, reverse):
    lr = lr.reshape(nj, 1, -1)
    li = li.reshape(nj, 1, -1)
    if reverse:
        li = -li
    pows = [(lr, li)]
    for _ in range(7):
        pows.append(_cmul(*pows[-1], lr, li))
    r = jnp.arange(SUBLANES).reshape(1, SUBLANES, 1)
    if reverse:
        r = SUBLANES - 1 - r
    out = []
    for k in (1, 2, 4):
        pr, pi = pows[k - 1]
        keep = (r >= k).astype(F32)
        out += [pr * keep, pi * keep]
    shape = (nj, SUBLANES, lr.shape[-1])
    cr = jnp.zeros(shape, F32)
    ci = jnp.zeros(shape, F32)
    for e in range(SUBLANES):
        sel = (r == e).astype(F32)
        cr = cr + sel * pows[e][0]
        ci = ci + sel * pows[e][1]
    out += [cr, ci]
    return jnp.stack(out, axis=1)


def _group_eye(gl):
    return jnp.eye(gl, dtype=F32)


def _block_diag_in(bbr, bbi, nj):
    g, p, c = bbr.shape
    gl = g // nj
    eye = _group_eye(gl)[None, :, None, :, None]

    def one(b):
        t = b.reshape(nj, gl, p, c).transpose(0, 1, 3, 2)[:, :, :, None, :]
        return (t * eye).reshape(nj, gl * c, gl * p)

    return jnp.concatenate([one(bbr), one(bbi)], axis=2)


def _block_diag_in_grad(gmat, nj, p, c):
    gl = gmat.shape[1] // c
    n = gl * p
    eye = _group_eye(gl)[None, :, None, :, None]

    def one(m):
        t = jnp.sum(m.reshape(nj, gl, c, gl, p) * eye, axis=3)
        return t.transpose(0, 1, 3, 2).reshape(nj * gl, p, c)

    return one(gmat[:, :, :n]), one(gmat[:, :, n:])


def _block_diag_out(c_re, c_im, nj):
    g, c, p = c_re.shape
    gl = g // nj
    eye = _group_eye(gl)[None, :, None, :, None]

    def one(m):
        t = m.reshape(nj, gl, c, p).transpose(0, 1, 3, 2)[:, :, :, None, :]
        return (t * eye).reshape(nj, gl * p, gl * c)

    return jnp.concatenate([one(c_re), -one(c_im)], axis=1)


def _block_diag_out_grad(gmat, nj, p, c):
    gl = gmat.shape[2] // c
    n = gl * p
    eye = _group_eye(gl)[None, :, None, :, None]

    def one(m):
        t = jnp.sum(m.reshape(nj, gl, p, gl, c) * eye, axis=3)
        return t.transpose(0, 1, 3, 2).reshape(nj * gl, c, p)

    return one(gmat[:, :n, :]), -one(gmat[:, n:, :])


def _pad_rows(flat, cols):
    per = SUBLANES * cols
    n = flat.shape[0]
    total = -(-n // per) * per
    return jnp.pad(flat, (0, total - n)).reshape(total // cols, cols)


def _pack_small(arrs, cols):
    packed = jnp.concatenate([_pad_rows(a.reshape(-1), cols) for a in arrs], axis=0)
    rows = packed.shape[0]
    return jnp.pad(packed, ((0, -rows % 128), (0, 0)))


def _unpack_small(packed, shapes, cols):
    out = []
    row = 0
    for s in shapes:
        n = math.prod(s)
        rows = -(-n // (SUBLANES * cols)) * SUBLANES
        out.append(packed[row:row + rows].reshape(-1)[:n].reshape(s))
        row += rows
    return out


def kernel(x, mix_norm, mlp_norm, mlp_w1, mlp_w2, ssm_log_dt, ssm_a_re, ssm_a_im, ssm_b_re, ssm_b_im, ssm_c_re, ssm_c_im, ssm_d, ssm_w_glu, kv_norm, w_kvf, b_f, attn_wq, attn_wo, final_norm, loss_target, m_mix_norm, m_mlp_norm, m_mlp_w1, m_mlp_w2, m_ssm_log_dt, m_ssm_a_re, m_ssm_a_im, m_ssm_b_re, m_ssm_b_im, m_ssm_c_re, m_ssm_c_im, m_ssm_d, m_ssm_w_glu, m_kv_norm, m_w_kvf, m_b_f, m_attn_wq, m_attn_wo, m_final_norm, v_mix_norm, v_mlp_norm, v_mlp_w1, v_mlp_w2, v_ssm_log_dt, v_ssm_a_re, v_ssm_a_im, v_ssm_b_re, v_ssm_b_im, v_ssm_c_re, v_ssm_c_im, v_ssm_d, v_ssm_w_glu, v_kv_norm, v_w_kvf, v_b_f, v_attn_wq, v_attn_wo, v_final_norm):
    seq, d = x.shape[1], x.shape[2]
    depth = mix_norm.shape[0]
    n_a = ssm_log_dt.shape[0]
    n_b = depth - n_a
    ff = mlp_w1.shape[2] * N_CHIPS
    n_heads = d // HEAD_DIM
    n_groups = d // SSM_GROUP
    p_state = ssm_a_re.shape[2]
    gb = min(d, 256)
    nj = d // gb
    kvf_cols = w_kvf.shape[1]
    ds4, dq4 = d // N_CHIPS, d // (2 * N_CHIPS)
    chip = 2 * lax.axis_index("x") + lax.axis_index("y")

    def cols_of(width):
        return lambda ref, s: ref.at[:, :, pl.ds(pl.multiple_of(s * width, LANES), width)]

    def rows_of(height):
        return lambda ref, s: ref.at[:, pl.ds(pl.multiple_of(s * height, SUBLANES), height), :]

    whole = lambda ref, s: ref
    slot = lambda ref, s: ref.at[s]
    shards = [mlp_w1.astype(BF16), mlp_w2.astype(BF16), ssm_w_glu.astype(BF16), w_kvf.astype(BF16),
              attn_wq.astype(BF16), attn_wo.astype(BF16), ssm_d]
    gathered = _chip_exchange(
        shards,
        [((depth, d, ff), BF16), ((depth, ff, d), BF16), ((n_a, d, 2 * d), BF16), ((N_CHIPS, d, kvf_cols), BF16),
         ((n_b, d, d), BF16), ((n_b, d, d), BF16), ((N_CHIPS, n_a, ds4), F32)],
        [(0, 0, whole, cols_of(d)), (1, 1, whole, rows_of(d)), (2, 2, whole, cols_of(2 * ds4)), (3, 3, whole, slot),
         (4, 4, whole, rows_of(ds4)), (5, 5, whole, rows_of(ds4)), (6, 6, whole, slot)],
        "gather_weights")
    w1_all, w2_all, wglu_all, kvf_parts, wq_all, wo_all, skip_parts = gathered
    skip_all = skip_parts.transpose(1, 0, 2).reshape(n_a, d)
    kvf_all = jnp.concatenate([kvf_parts[s] for s in range(N_CHIPS)], axis=1)
    wk = kvf_all[:, :d]
    wv = kvf_all[:, d:2 * d]
    wf = jnp.pad(kvf_all[:, 2 * d:], ((0, 0), (0, LANES - n_heads)))
    bf_row = jnp.pad(b_f, (0, LANES - n_heads)).reshape(1, LANES)

    h = x[0]
    target = loss_target[0]

    saved = []
    for i in range(n_a):
        lbr, lbi, bbr, bbi = _ssm_discretise(ssm_log_dt[i], ssm_a_re[i], ssm_a_im[i], ssm_b_re[i], ssm_b_im[i])
        bblk = _block_diag_in(bbr, bbi, nj)
        cblk = _block_diag_out(ssm_c_re[i], ssm_c_im[i], nj)
        rec = dict(h0=h, lam=(lbr, lbi), bblk=bblk, cblk=cblk)
        u = _norm_fwd(h, mix_norm[i:i + 1], f"s5_norm_{i}")
        rec["u"] = u
        dskip = rec["dskip"] = skip_all[i:i + 1]
        states, y = _scan_fwd(u, bblk.astype(BF16), cblk.astype(BF16), _scan_tables(lbr, lbi, nj, False), dskip,
                              f"s5_scan_{i}")
        rec["states"], rec["y"] = states, y
        h, rec["zw"] = _s5_post_fwd(h, y, wglu_all[i], f"s5_glu_{i}")
        rec["h1"] = h
        h, rec["ap"] = _mlp_fwd(h, mlp_norm[i:i + 1], w1_all[i], w2_all[i], f"mlp_{i}")
        saved.append(rec)
    h_kv = h
    wk_x = _slot_cols(wk)
    kx, vx, flog, cum = _kvf_fwd(h, kv_norm.reshape(1, d), wk_x, _slot_cols(wv), wf, bf_row,
                                 _slot_ones(n_heads, LANE_ROWSUM_DS), _slot_ones(n_heads, LANE_ROWSUM_P), "kvf")
    f2 = cum[:, :n_heads] * LOG2E
    f2_rows = f2.T.reshape(n_heads // 2, 2, seq)
    f2_rep = jnp.broadcast_to(f2[:, :, None], (seq, n_heads, HEAD_SLOT)).reshape(seq, n_heads * HEAD_SLOT)
    wq_x = [_slot_cols(wq_all[jb]) for jb in range(n_b)]
    for jb in range(n_b):
        i = n_a + jb
        rec = dict(h0=h)
        qx = _q_fwd(h, mix_norm[i:i + 1], wq_x[jb], _slot_ones(n_heads, LANE_COLSUM_DS), f"attn_q_{jb}")
        o, lse = _flash_fwd(qx, kx, vx, f2_rows, f"attn_core_{jb}")
        rec["qx"], rec["o"], rec["lse"] = qx, o, lse
        h = _o_fwd(h, o, wo_all[jb], f"attn_out_{jb}")
        rec["h1"] = h
        h, rec["ap"] = _mlp_fwd(h, mlp_norm[i:i + 1], w1_all[i], w2_all[i], f"mlp_{i}")
        saved.append(rec)
    dh, loss_row, g_final = _loss_head(h, target, final_norm.reshape(1, d), "loss_head")
    loss = lax.psum(loss_row[0, 0], ("x", "y", "c"))

    head_ones = (jnp.arange(d)[:, None] // HEAD_DIM == jnp.arange(d)[None, :] // HEAD_DIM).astype(BF16)
    g_mix = [None] * depth
    g_mlp = [None] * depth
    g_w1 = [None] * depth
    g_w2 = [None] * depth
    g_wq = [None] * n_b
    g_wo = [None] * n_b
    g_glu = [None] * n_a
    g_ssm = [None] * n_a
    dk_parts, dv_parts, df_parts = [], [], []

    def head_rows(rep):
        return rep[:, ::HEAD_DIM].T.reshape(n_heads // 2, 2, seq)

    def mlp_back(dh, i, rec):
        dh_in, hm, a, dap, g_mlp[i] = _mlp_bwd(dh, rec["h1"], rec["ap"], mlp_norm[i:i + 1], w1_all[i], w2_all[i],
                                               f"mlp_bwd_{i}")
        g_w2[i] = _matmul_tn(a, dh, f"mlp_dw2_{i}")
        g_w1[i] = _matmul_tn(hm, dap, f"mlp_dw1_{i}")
        return dh_in

    for jb in reversed(range(n_b)):
        i = n_a + jb
        rec = saved[i]
        dh = mlp_back(dh, i, rec)
        do, delta = _o_bwd(dh, rec["o"], wo_all[jb], head_ones, f"attn_out_bwd_{jb}")
        g_wo[jb] = _matmul_tn(rec["o"], dh, f"attn_dwo_{jb}")
        dqx, dkx, dv = _flash_bwd(rec["qx"], kx, vx, f2_rep, do, head_rows(rec["lse"]), head_rows(delta),
                                  f"attn_core_bwd_{jb}")
        dk_parts.append(dkx)
        dv_parts.append(dv)
        df_parts.append(dqx[:, LANE_ROWSUM_DS::HEAD_SLOT] - dkx[:, LANE_COLSUM_DS::HEAD_SLOT])
        dh, hn, dqs, g_mix[i] = _q_bwd(dh, rec["h0"], dqx, mix_norm[i:i + 1], wq_x[jb], f"attn_q_bwd_{jb}")
        g_wq[jb] = _unslot_cols(_matmul_tn(hn, dqs, f"attn_dwq_{jb}"))

    dft = df_parts[0]
    for extra in df_parts[1:]:
        dft = dft + extra
    dcum = jnp.pad(dft, ((0, 0), (0, LANES - n_heads)))
    dh, hk, dkb, dvb, dfb, g_kvn, g_bf = _kvf_bwd(dh, h_kv, dk_parts[0], dk_parts[1], dv_parts[0], dv_parts[1],
                                                  dcum, flog, kv_norm.reshape(1, d), wk_x, wv, wf, "kvf_bwd")
    g_kvf = jnp.concatenate([_unslot_cols(_matmul_tn(hk, dkb, "kvf_dwk")), _matmul_tn(hk, dvb, "kvf_dwv"),
                             _matmul_tn(hk, dfb, "kvf_dwf")[:, :n_heads]], axis=1)

    for i in reversed(range(n_a)):
        rec = saved[i]
        dh = mlp_back(dh, i, rec)
        dy, z, dzw = _s5_post_bwd(dh, rec["y"], rec["zw"], wglu_all[i], f"s5_glu_bwd_{i}")
        g_glu[i] = _matmul_tn(z, dzw, f"s5_dwglu_{i}")
        lbr, lbi = rec["lam"]
        bblk_t = rec["bblk"].transpose(0, 2, 1).astype(BF16)
        cblk_t = rec["cblk"].transpose(0, 2, 1).astype(BF16)
        du, glam8, gd8, gbblk, gcblk = _scan_bwd(dy, rec["u"], rec["states"], bblk_t, cblk_t,
                                                 _scan_tables(lbr, lbi, nj, True), rec["dskip"], f"s5_scan_bwd_{i}")
        dh, g_mix[i] = _norm_bwd_add(dh, du, rec["h0"], mix_norm[i:i + 1], f"s5_norm_bwd_{i}")
        glam = jnp.sum(glam8, axis=1)
        n_st = glam.shape[1] // 2
        g_lbr = glam[:, :n_st].reshape(n_groups, p_state)
        g_lbi = glam[:, n_st:].reshape(n_groups, p_state)
        g_bbr, g_bbi = _block_diag_in_grad(gbblk, nj, p_state, SSM_GROUP)
        g_cre, g_cim = _block_diag_out_grad(gcblk, nj, p_state, SSM_GROUP)
        _, pull = jax.vjp(_ssm_discretise, ssm_log_dt[i], ssm_a_re[i], ssm_a_im[i], ssm_b_re[i], ssm_b_im[i])
        g_ldt, g_are, g_aim, g_bre, g_bim = pull((g_lbr, g_lbi, g_bbr, g_bbi))
        g_ssm[i] = dict(log_dt=g_ldt, a_re=g_are, a_im=g_aim, b_re=g_bre, b_im=g_bim, c_re=g_cre, c_im=g_cim,
                        d=jnp.sum(gd8, axis=1).reshape(d))
    grad_x = dh[None]

    def stack_small(key):
        return jnp.stack([g_ssm[i][key] for i in range(n_a)])

    small_grads = [jnp.concatenate(g_mix, axis=0), jnp.concatenate(g_mlp, axis=0), stack_small("log_dt"),
                   stack_small("a_re"), stack_small("a_im"), stack_small("b_re"), stack_small("b_im"),
                   stack_small("c_re"), stack_small("c_im"), stack_small("d"), g_kvn.reshape(d),
                   g_bf[0, :n_heads], g_final.reshape(d)]
    small_w = [mix_norm, mlp_norm, ssm_log_dt, ssm_a_re, ssm_a_im, ssm_b_re, ssm_b_im, ssm_c_re, ssm_c_im,
               ssm_d, kv_norm, b_f, final_norm]
    small_m = [m_mix_norm, m_mlp_norm, m_ssm_log_dt, m_ssm_a_re, m_ssm_a_im, m_ssm_b_re, m_ssm_b_im, m_ssm_c_re,
               m_ssm_c_im, m_ssm_d, m_kv_norm, m_b_f, m_final_norm]
    small_v = [v_mix_norm, v_mlp_norm, v_ssm_log_dt, v_ssm_a_re, v_ssm_a_im, v_ssm_b_re, v_ssm_b_im, v_ssm_c_re,
               v_ssm_c_im, v_ssm_d, v_kv_norm, v_b_f, v_final_norm]
    skip_at = 9

    def widen_skip(part):
        return lax.dynamic_update_slice(jnp.zeros((n_a, d), F32), part, (0, chip * ds4))

    small_shapes = [a.shape for a in small_grads]
    pcols = 1024 if d >= 1024 else LANES
    g_small = _pack_small(small_grads, pcols)
    expand = lambda lst: _pack_small([widen_skip(a) if n == skip_at else a for n, a in enumerate(lst)], pcols)
    w_small, m_small, v_small = expand(small_w), expand(small_m), expand(small_v)
    srows = g_small.shape[0]

    kvf_send = g_kvf.reshape(d, N_CHIPS, kvf_cols).transpose(1, 0, 2)

    def cols2(width):
        return lambda ref, s: ref.at[:, pl.ds(pl.multiple_of(s * width, LANES), width)]

    def rows2(height):
        return lambda ref, s: ref.at[pl.ds(pl.multiple_of(s * height, SUBLANES), height), :]

    def into(layer):
        return lambda ref, s: ref.at[s, layer]

    assert n_b == 2
    red_srcs, red_plan = [], []

    def send(arr, out_index, src_view, dst_view):
        red_plan.append((len(red_srcs), out_index, src_view, dst_view))
        red_srcs.append(arr)

    for i in range(depth):
        send(g_w1[i], 0, cols2(d), into(i))
    for i in range(depth):
        send(g_w2[i], 1, rows2(d), into(i))
    for i in range(n_a):
        send(g_glu[i], 2, cols2(2 * ds4), into(i))
    send(kvf_send, 3, slot, slot)
    for jb in range(n_b):
        send(g_wq[jb], 4, rows2(ds4), into(jb))
    for jb in range(n_b):
        send(g_wo[jb], 5, rows2(ds4), into(jb))
    send(g_small, 6, whole, slot)
    received = _chip_exchange(
        red_srcs,
        [((N_CHIPS, depth, d, d), BF16), ((N_CHIPS, depth, d, d), BF16), ((N_CHIPS, n_a, d, 2 * ds4), BF16),
         ((N_CHIPS, d, kvf_cols), BF16), ((N_CHIPS, n_b, ds4, d), BF16), ((N_CHIPS, n_b, ds4, d), BF16),
         ((N_CHIPS, srows, pcols), F32)],
        red_plan, "reduce_chips")
    r_w1, r_w2, r_glu, r_kvf, r_wq, r_wo, r_small = received

    def flat(a):
        return a.reshape(N_CHIPS, -1, a.shape[-1])

    sums = [_sum_chips(flat(r), f"sum_chips_{n}") for n, r in
            enumerate((r_w1, r_w2, r_glu, r_kvf, r_wq, r_wo, r_small))]
    others = _core_exchange(sums, "reduce_cores")

    def two(a):
        return a.reshape(-1, a.shape[-1])

    big_w = [(mlp_w1, m_mlp_w1, v_mlp_w1), (mlp_w2, m_mlp_w2, v_mlp_w2), (ssm_w_glu, m_ssm_w_glu, v_ssm_w_glu),
             (w_kvf, m_w_kvf, v_w_kvf), (attn_wq, m_attn_wq, v_attn_wq), (attn_wo, m_attn_wo, v_attn_wo)]
    big_out = []
    for n, (w, m, v) in enumerate(big_w):
        res = _adamw(sums[n], others[n], two(w), two(m), two(v), f"adamw_{n}")
        big_out.append([r.reshape(w.shape) for r in res])
    small_out = _adamw(sums[6], others[6], w_small, m_small, v_small, "adamw_small")

    def narrow_skip(a):
        return lax.dynamic_slice(a, (0, chip * ds4), (n_a, ds4))

    unpacked = []
    for packed in small_out:
        parts = _unpack_small(packed, small_shapes, pcols)
        parts[skip_at] = narrow_skip(parts[skip_at])
        unpacked.append(parts)

    order = ["mix_norm", "mlp_norm", "mlp_w1", "mlp_w2", "ssm_log_dt", "ssm_a_re", "ssm_a_im", "ssm_b_re",
             "ssm_b_im", "ssm_c_re", "ssm_c_im", "ssm_d", "ssm_w_glu", "kv_norm", "w_kvf", "b_f", "attn_wq",
             "attn_wo", "final_norm"]
    small_names = ["mix_norm", "mlp_norm", "ssm_log_dt", "ssm_a_re", "ssm_a_im", "ssm_b_re", "ssm_b_im",
                   "ssm_c_re", "ssm_c_im", "ssm_d", "kv_norm", "b_f", "final_norm"]
    big_names = ["mlp_w1", "mlp_w2", "ssm_w_glu", "w_kvf", "attn_wq", "attn_wo"]
    outs = [loss, grad_x]
    for kind in range(4):
        for name in order:
            if name in big_names:
                outs.append(big_out[big_names.index(name)][kind])
            else:
                outs.append(unpacked[kind][small_names.index(name)])
    return tuple(outs)
```

```python
import functools
import math

import jax
import jax.numpy as jnp
from jax import lax
from jax.experimental import pallas as pl
from jax.experimental.pallas import tpu as pltpu

F32 = jnp.float32
BF16 = jnp.bfloat16

RMS_EPS = 1e-6
SSM_GROUP = 16
SSM_STATE = 64
HEAD_DIM = 64
HEAD_PAIR = 2 * HEAD_DIM
LANES = 128
SUBLANES = 8
N_CHIPS = 4
ADAM_LR = 0.001
ADAM_B1 = 0.9
ADAM_B2 = 0.999
ADAM_EPS = 1e-08
ADAM_WD = 0.01
ADAM_STEP = 10
GELU_C = math.sqrt(2.0 / math.pi)
GELU_A = 0.044715
NEG = -1e30
LN2 = math.log(2.0)
LOG2E = 1.0 / LN2
VMEM_LIMIT = 56 * 1024 * 1024
MESH = pl.DeviceIdType.MESH

NT_DIMS = (((1,), (1,)), ((), ()))
TN_DIMS = (((0,), (0,)), ((), ()))


def _cp(*sem):
    return pltpu.CompilerParams(dimension_semantics=sem if sem else None, vmem_limit_bytes=VMEM_LIMIT)


def _zero_idx(nd, *_):
    return (0,) * nd


def _tile(n, t):
    if n <= t:
        return n
    for cand in range(t - t % SUBLANES, 0, -SUBLANES):
        if n % cand == 0:
            return cand
    raise ValueError((n, t))


def _rms_fwd(h, g):
    r = lax.rsqrt(jnp.mean(h * h, axis=-1, keepdims=True) + RMS_EPS)
    hhat = h * r
    return hhat * g, hhat, r


def _rms_bwd(du, hhat, r, g):
    dhh = du * g
    dh = r * (dhh - hhat * jnp.mean(dhh * hhat, axis=-1, keepdims=True))
    return dh, du * hhat


def _sigmoid(x):
    return 1.0 / (1.0 + jnp.exp(-x))


def _gelu(x):
    t = jnp.tanh(GELU_C * (x + GELU_A * x * x * x))
    return 0.5 * x * (1.0 + t)


def _gelu_grad(x):
    t = jnp.tanh(GELU_C * (x + GELU_A * x * x * x))
    return 0.5 * (1.0 + t) + 0.5 * x * (1.0 - t * t) * GELU_C * (1.0 + 3.0 * GELU_A * x * x)


def _row_fold(x):
    tm, w = x.shape
    return jnp.sum(x.reshape(tm // SUBLANES, SUBLANES, w), axis=0)


def _split3(x):
    hi = x.astype(BF16)
    r1 = x - hi.astype(F32)
    mid = r1.astype(BF16)
    lo = (r1 - mid.astype(F32)).astype(BF16)
    return hi, mid, lo


def _exact_dot(ones_mat, x):
    hi, mid, lo = _split3(x)
    d = functools.partial(jnp.dot, preferred_element_type=F32)
    return d(ones_mat, hi) + d(ones_mat, mid) + d(ones_mat, lo)


def _rows_call(body, name, tm, row_ins, const_ins, row_outs, acc_outs=(), scratch=(), reverse=False):
    n = row_ins[0].shape[0]
    nb = n // tm
    if reverse:
        ridx = lambda i: (nb - 1 - i, 0)
    else:
        ridx = lambda i: (i, 0)
    in_specs = [pl.BlockSpec((tm, a.shape[1]), ridx) for a in row_ins]
    in_specs += [pl.BlockSpec(a.shape, functools.partial(_zero_idx, a.ndim), pipeline_mode=pl.Buffered(1))
                 for a in const_ins]
    out_shape = [jax.ShapeDtypeStruct((n, w), dt) for (w, dt) in row_outs]
    out_shape += [jax.ShapeDtypeStruct(s, dt) for (s, dt) in acc_outs]
    out_specs = [pl.BlockSpec((tm, w), ridx) for (w, dt) in row_outs]
    out_specs += [pl.BlockSpec(s, functools.partial(_zero_idx, len(s))) for (s, dt) in acc_outs]
    return pl.pallas_call(
        body, name=name, grid=(nb,), in_specs=in_specs, out_specs=out_specs, out_shape=out_shape,
        scratch_shapes=list(scratch), compiler_params=_cp("arbitrary"),
    )(*row_ins, *const_ins)


def _norm_fwd(h, g, name):
    n, d = h.shape
    tm = _tile(n, 512)

    def body(h_ref, g_ref, u_ref):
        u_ref[...] = _rms_fwd(h_ref[...], g_ref[...])[0]

    return _rows_call(body, name, tm, [h], [g], [(d, F32)])[0]


def _norm_bwd_add(dh, du, h, g, name):
    n, d = h.shape
    tm = _tile(n, 512)
    nb = n // tm

    def body(dh_ref, du_ref, h_ref, g_ref, o_ref, dg_ref, acc):
        i = pl.program_id(0)

        @pl.when(i == 0)
        def _():
            acc[...] = jnp.zeros_like(acc)

        gain = g_ref[...]
        _, hhat, r = _rms_fwd(h_ref[...], gain)
        dhn, dgr = _rms_bwd(du_ref[...], hhat, r, gain)
        o_ref[...] = dh_ref[...] + dhn
        acc[...] += _row_fold(dgr)

        @pl.when(i == nb - 1)
        def _():
            dg_ref[...] = jnp.sum(acc[...], axis=0, keepdims=True)

    return _rows_call(body, name, tm, [dh, du, h], [g], [(d, F32)], [((1, d), F32)],
                      [pltpu.VMEM((SUBLANES, d), F32)])


def _mlp_fwd(h, g, w1, w2, name):
    n, d = h.shape
    ff = w1.shape[1]
    tm = _tile(n, 256)
    fc = _tile(ff, 1024)

    def body(h_ref, g_ref, w1_ref, w2_ref, o_ref, ap_ref):
        hin = h_ref[...]
        hb = _rms_fwd(hin, g_ref[...])[0].astype(BF16)
        acc = hin
        for c in range(ff // fc):
            cs = slice(c * fc, (c + 1) * fc)
            ap = jnp.dot(hb, w1_ref[:, cs], preferred_element_type=F32)
            ap_ref[:, cs] = ap.astype(BF16)
            rl = jnp.maximum(ap, 0.0)
            acc = acc + jnp.dot((rl * rl).astype(BF16), w2_ref[cs, :], preferred_element_type=F32)
        o_ref[...] = acc

    return _rows_call(body, name, tm, [h], [g, w1, w2], [(d, F32), (ff, BF16)])


def _mlp_bwd(dh, h, ap, g, w1, w2, name):
    n, d = h.shape
    ff = w1.shape[1]
    tm = _tile(n, 256)
    nb = n // tm
    fc = _tile(ff, 1024)

    def body(dh_ref, h_ref, ap_ref, g_ref, w1_ref, w2_ref, o_ref, hm_ref, a_ref, dap_ref, dg_ref, acc):
        i = pl.program_id(0)

        @pl.when(i == 0)
        def _():
            acc[...] = jnp.zeros_like(acc)

        gain = g_ref[...]
        dhv = dh_ref[...]
        hm, hhat, r = _rms_fwd(h_ref[...], gain)
        hm_ref[...] = hm.astype(BF16)
        dhb = dhv.astype(BF16)
        dhm = jnp.zeros((tm, d), F32)
        for c in range(ff // fc):
            cs = slice(c * fc, (c + 1) * fc)
            rl = jnp.maximum(ap_ref[:, cs].astype(F32), 0.0)
            a_ref[:, cs] = (rl * rl).astype(BF16)
            da = lax.dot_general(dhb, w2_ref[cs, :], NT_DIMS, preferred_element_type=F32)
            dap = (da * (2.0 * rl)).astype(BF16)
            dap_ref[:, cs] = dap
            dhm = dhm + lax.dot_general(dap, w1_ref[:, cs], NT_DIMS, preferred_element_type=F32)
        dhn, dgr = _rms_bwd(dhm, hhat, r, gain)
        o_ref[...] = dhv + dhn
        acc[...] += _row_fold(dgr)

        @pl.when(i == nb - 1)
        def _():
            dg_ref[...] = jnp.sum(acc[...], axis=0, keepdims=True)

    return _rows_call(body, name, tm, [dh, h, ap], [g, w1, w2],
                      [(d, F32), (d, BF16), (ff, BF16), (ff, BF16)], [((1, d), F32)],
                      [pltpu.VMEM((SUBLANES, d), F32)])


def _s5_post_fwd(h, y, w_glu, name):
    n, d = h.shape
    tm = _tile(n, 512)

    def body(h_ref, y_ref, w_ref, o_ref, zw_ref):
        z = _gelu(y_ref[...]).astype(BF16)
        zw = jnp.dot(z, w_ref[...], preferred_element_type=F32)
        zw_ref[...] = zw.astype(BF16)
        o_ref[...] = h_ref[...] + zw[:, :d] * _sigmoid(zw[:, d:])

    return _rows_call(body, name, tm, [h, y], [w_glu], [(d, F32), (2 * d, BF16)])


def _s5_post_bwd(dh, y, zw, w_glu, name):
    n, d = dh.shape
    tm = _tile(n, 512)

    def body(dh_ref, y_ref, zw_ref, w_ref, dy_ref, z_ref, dzw_ref):
        dhv = dh_ref[...]
        yv = y_ref[...]
        val = zw_ref[:, :d].astype(F32)
        sg = _sigmoid(zw_ref[:, d:].astype(F32))
        dzw = jnp.concatenate([dhv * sg, dhv * val * sg * (1.0 - sg)], axis=1).astype(BF16)
        dzw_ref[...] = dzw
        dz = lax.dot_general(dzw, w_ref[...], NT_DIMS, preferred_element_type=F32)
        dy_ref[...] = dz * _gelu_grad(yv)
        z_ref[...] = _gelu(yv).astype(BF16)

    return _rows_call(body, name, tm, [dh, y, zw], [w_glu], [(d, F32), (d, BF16), (2 * d, BF16)])


def _q_fwd(h, g, wq_x, ones_x, name):
    n, d = h.shape
    tm = _tile(n, 512)
    scale = LOG2E * HEAD_DIM ** -0.5

    def body(h_ref, g_ref, w_ref, one_ref, q_ref):
        hb = _rms_fwd(h_ref[...], g_ref[...])[0].astype(BF16)
        q_ref[...] = (jnp.dot(hb, w_ref[...], preferred_element_type=F32) * scale + one_ref[...]).astype(BF16)

    return _rows_call(body, name, tm, [h], [g, wq_x, ones_x], [(wq_x.shape[1], BF16)])[0]


def _q_bwd(dh, h, dq, g, wq, name):
    n, d = h.shape
    tm = _tile(n, 512)
    nb = n // tm
    scale = HEAD_DIM ** -0.5

    def body(dh_ref, h_ref, dq_ref, g_ref, w_ref, o_ref, hn_ref, dqs_ref, dg_ref, acc):
        i = pl.program_id(0)

        @pl.when(i == 0)
        def _():
            acc[...] = jnp.zeros_like(acc)

        gain = g_ref[...]
        hn, hhat, r = _rms_fwd(h_ref[...], gain)
        hn_ref[...] = hn.astype(BF16)
        dqs = (dq_ref[...] * scale).astype(BF16)
        dqs_ref[...] = dqs
        dhn = lax.dot_general(dqs, w_ref[...], NT_DIMS, preferred_element_type=F32)
        dhi, dgr = _rms_bwd(dhn, hhat, r, gain)
        o_ref[...] = dh_ref[...] + dhi
        acc[...] += _row_fold(dgr)

        @pl.when(i == nb - 1)
        def _():
            dg_ref[...] = jnp.sum(acc[...], axis=0, keepdims=True)

    return _rows_call(body, name, tm, [dh, h, dq], [g, wq], [(d, F32), (d, BF16), (wq.shape[1], BF16)],
                      [((1, d), F32)], [pltpu.VMEM((SUBLANES, d), F32)])


def _o_fwd(h, o, wo, name):
    n, d = h.shape
    tm = _tile(n, 512)

    def body(h_ref, o_ref, w_ref, out_ref):
        out_ref[...] = h_ref[...] + jnp.dot(o_ref[...], w_ref[...], preferred_element_type=F32)

    return _rows_call(body, name, tm, [h, o], [wo], [(d, F32)])[0]


def _o_bwd(dh, o, wo, head_ones, name):
    n, d = dh.shape
    tm = _tile(n, 512)

    def body(dh_ref, o_ref, w_ref, e_ref, do_ref, dl_ref):
        do = lax.dot_general(dh_ref[...].astype(BF16), w_ref[...], NT_DIMS, preferred_element_type=F32).astype(BF16)
        do_ref[...] = do
        dl_ref[...] = _exact_dot_rhs(do.astype(F32) * o_ref[...].astype(F32), e_ref[...])

    return _rows_call(body, name, tm, [dh, o], [wo, head_ones], [(wo.shape[0], BF16), (wo.shape[0], F32)])


def _exact_dot_rhs(x, ones_mat):
    hi, mid, lo = _split3(x)
    d = functools.partial(jnp.dot, preferred_element_type=F32)
    return d(hi, ones_mat) + d(mid, ones_mat) + d(lo, ones_mat)


def _kvf_fwd(h, g, wk, wv, wf, bf, k_ones, v_ones, name):
    n, d = h.shape
    tm = _tile(n, 512)

    def body(h_ref, g_ref, wk_ref, wv_ref, wf_ref, bf_ref, ko_ref, vo_ref, k_ref, v_ref, fl_ref, cum_ref, carry):
        i = pl.program_id(0)

        @pl.when(i == 0)
        def _():
            carry[...] = jnp.zeros_like(carry)

        hb = _rms_fwd(h_ref[...], g_ref[...])[0].astype(BF16)
        k_ref[...] = (jnp.dot(hb, wk_ref[...], preferred_element_type=F32) + ko_ref[...]).astype(BF16)
        v_ref[...] = (jnp.dot(hb, wv_ref[...], preferred_element_type=F32) + vo_ref[...]).astype(BF16)
        fl = jnp.dot(hb, wf_ref[...], preferred_element_type=F32) + bf_ref[...]
        fl_ref[...] = fl
        logf = jnp.minimum(fl, 0.0) - jnp.log(1.0 + jnp.exp(-jnp.abs(fl)))
        rows = lax.broadcasted_iota(jnp.int32, (tm, tm), 0)
        cols = lax.broadcasted_iota(jnp.int32, (tm, tm), 1)
        lower = (rows >= cols).astype(BF16)
        cum = _exact_dot(lower, logf) + carry[0:1, :]
        cum_ref[...] = cum
        carry[...] = jnp.broadcast_to(cum[tm - 1:tm, :], carry.shape)

    return _rows_call(body, name, tm, [h], [g, wk, wv, wf, bf, k_ones, v_ones],
                      [(wk.shape[1], BF16), (wv.shape[1], BF16), (LANES, F32), (LANES, F32)],
                      scratch=[pltpu.VMEM((SUBLANES, LANES), F32)])


def _kvf_bwd(dh, h, dk1, dk2, dv1, dv2, dcum, fl, g, wk, wv, wf, name):
    n, d = h.shape
    tm = _tile(n, 512)
    nb = n // tm

    def body(dh_ref, h_ref, dk1_ref, dk2_ref, dv1_ref, dv2_ref, dc_ref, fl_ref, g_ref, wk_ref, wv_ref, wf_ref,
             o_ref, hk_ref, dk_ref, dv_ref, df_ref, dg_ref, db_ref, acc, bacc, carry):
        i = pl.program_id(0)

        @pl.when(i == 0)
        def _():
            acc[...] = jnp.zeros_like(acc)
            bacc[...] = jnp.zeros_like(bacc)
            carry[...] = jnp.zeros_like(carry)

        rows = lax.broadcasted_iota(jnp.int32, (tm, tm), 0)
        cols = lax.broadcasted_iota(jnp.int32, (tm, tm), 1)
        upper = (rows <= cols).astype(BF16)
        dlogf = _exact_dot(upper, dc_ref[...]) + carry[0:1, :]
        carry[...] = jnp.broadcast_to(dlogf[0:1, :], carry.shape)
        df = dlogf / (1.0 + jnp.exp(fl_ref[...]))
        dfb = df.astype(BF16)
        df_ref[...] = dfb
        bacc[...] += _row_fold(df)
        dkb = ((dk1_ref[...] + dk2_ref[...]) * LN2).astype(BF16)
        dvb = (dv1_ref[...] + dv2_ref[...]).astype(BF16)
        dk_ref[...] = dkb
        dv_ref[...] = dvb
        gain = g_ref[...]
        hk, hhat, r = _rms_fwd(h_ref[...], gain)
        hk_ref[...] = hk.astype(BF16)
        dhk = lax.dot_general(dkb, wk_ref[...], NT_DIMS, preferred_element_type=F32)
        dhk = dhk + lax.dot_general(dvb, wv_ref[...], NT_DIMS, preferred_element_type=F32)
        dhk = dhk + lax.dot_general(dfb, wf_ref[...], NT_DIMS, preferred_element_type=F32)
        dhi, dgr = _rms_bwd(dhk, hhat, r, gain)
        o_ref[...] = dh_ref[...] + dhi
        acc[...] += _row_fold(dgr)

        @pl.when(i == nb - 1)
        def _():
            dg_ref[...] = jnp.sum(acc[...], axis=0, keepdims=True)
            db_ref[...] = jnp.sum(bacc[...], axis=0, keepdims=True)

    return _rows_call(body, name, tm, [dh, h, dk1, dk2, dv1, dv2, dcum, fl], [g, wk, wv, wf],
                      [(d, F32), (d, BF16), (wk.shape[1], BF16), (wv.shape[1], BF16), (LANES, BF16)],
                      [((1, d), F32), ((1, LANES), F32)],
                      [pltpu.VMEM((SUBLANES, d), F32), pltpu.VMEM((SUBLANES, LANES), F32),
                       pltpu.VMEM((SUBLANES, LANES), F32)], reverse=True)


def _loss_head(h, target, g, name):
    n, d = h.shape
    tm = _tile(n, 512)
    nb = n // tm

    def body(h_ref, t_ref, g_ref, dh_ref, loss_ref, dg_ref, lacc, gacc):
        i = pl.program_id(0)

        @pl.when(i == 0)
        def _():
            lacc[...] = jnp.zeros_like(lacc)
            gacc[...] = jnp.zeros_like(gacc)

        gain = g_ref[...]
        yv, hhat, r = _rms_fwd(h_ref[...], gain)
        e = yv - t_ref[...]
        lacc[...] += _row_fold(e * e)
        dhv, dgr = _rms_bwd(e * (1.0 / d), hhat, r, gain)
        dh_ref[...] = dhv
        gacc[...] += _row_fold(dgr)

        @pl.when(i == nb - 1)
        def _():
            loss_ref[...] = jnp.full((1, LANES), jnp.sum(lacc[...]) * (0.5 / d), F32)
            dg_ref[...] = jnp.sum(gacc[...], axis=0, keepdims=True)

    return _rows_call(body, name, tm, [h, target], [g], [(d, F32)], [((1, LANES), F32), ((1, d), F32)],
                      [pltpu.VMEM((SUBLANES, d), F32), pltpu.VMEM((SUBLANES, d), F32)])


def _matmul_tn(a, b, name, out_dtype=BF16):
    l, m = a.shape
    n = b.shape[1]
    tl = _tile(l, 1024)
    tmm = _tile(m, 512)
    tn = _tile(n, 1024)
    nl = l // tl

    def body(a_ref, b_ref, o_ref, acc):
        k = pl.program_id(2)

        @pl.when(k == 0)
        def _():
            acc[...] = jnp.zeros_like(acc)

        acc[...] += lax.dot_general(a_ref[...].astype(BF16), b_ref[...].astype(BF16), TN_DIMS,
                                    preferred_element_type=F32)

        @pl.when(k == nl - 1)
        def _():
            o_ref[...] = acc[...].astype(out_dtype)

    return pl.pallas_call(
        body, name=name, grid=(m // tmm, n // tn, nl),
        in_specs=[pl.BlockSpec((tl, tmm), lambda i, j, k: (k, i)), pl.BlockSpec((tl, tn), lambda i, j, k: (k, j))],
        out_specs=pl.BlockSpec((tmm, tn), lambda i, j, k: (i, j)),
        out_shape=jax.ShapeDtypeStruct((m, n), out_dtype),
        scratch_shapes=[pltpu.VMEM((tmm, tn), F32)],
        compiler_params=_cp("parallel", "parallel", "arbitrary"),
    )(a, b)


def _scan_fwd(u, bblk, cblk, tabs, dskip, name):
    l, d = u.shape
    nj, gb, n2 = bblk.shape
    n = n2 // 2
    tm = _tile(l, 512)
    nb = l // tm

    def body(u_ref, b_ref, c_ref, t_ref, d_ref, st_ref, y_ref, carry):
        i = pl.program_id(1)

        @pl.when(i == 0)
        def _():
            carry[...] = jnp.zeros_like(carry)

        uf = u_ref[...]
        st_ref[...] = jnp.dot(uf.astype(BF16), b_ref[0], preferred_element_type=F32)

        def step(rb, c):
            cr, ci = c
            rows = pl.ds(pl.multiple_of(rb * SUBLANES, SUBLANES), SUBLANES)
            xr = st_ref[rows, 0:n]
            xi = st_ref[rows, n:n2]
            for lvl, sh in enumerate((1, 2, 4)):
                ar = t_ref[0, 2 * lvl]
                ai = t_ref[0, 2 * lvl + 1]
                sr = pltpu.roll(xr, sh, 0)
                si = pltpu.roll(xi, sh, 0)
                xr, xi = xr + ar * sr - ai * si, xi + ar * si + ai * sr
            lr = t_ref[0, 6]
            li = t_ref[0, 7]
            xr, xi = xr + lr * cr - li * ci, xi + lr * ci + li * cr
            st_ref[rows, 0:n] = xr
            st_ref[rows, n:n2] = xi
            return (jnp.broadcast_to(xr[SUBLANES - 1:SUBLANES, :], (SUBLANES, n)),
                    jnp.broadcast_to(xi[SUBLANES - 1:SUBLANES, :], (SUBLANES, n)))

        cr, ci = lax.fori_loop(0, tm // SUBLANES, step, (carry[:, 0:n], carry[:, n:n2]))
        carry[:, 0:n] = cr
        carry[:, n:n2] = ci
        y_ref[...] = jnp.dot(st_ref[...].astype(BF16), c_ref[0], preferred_element_type=F32) + d_ref[...] * uf

    return pl.pallas_call(
        body, name=name, grid=(nj, nb),
        in_specs=[pl.BlockSpec((tm, gb), lambda j, i: (i, j)),
                  pl.BlockSpec((1, gb, n2), lambda j, i: (j, 0, 0)),
                  pl.BlockSpec((1, n2, gb), lambda j, i: (j, 0, 0)),
                  pl.BlockSpec((1, 8, SUBLANES, n), lambda j, i: (j, 0, 0, 0)),
                  pl.BlockSpec((1, gb), lambda j, i: (0, j))],
        out_specs=[pl.BlockSpec((tm, n2), lambda j, i: (i, j)), pl.BlockSpec((tm, gb), lambda j, i: (i, j))],
        out_shape=[jax.ShapeDtypeStruct((l, nj * n2), F32), jax.ShapeDtypeStruct((l, d), F32)],
        scratch_shapes=[pltpu.VMEM((SUBLANES, n2), F32)],
        compiler_params=_cp("parallel", "arbitrary"),
    )(u, bblk, cblk, tabs, dskip)


def _scan_bwd(dy, u, states, bblk_t, cblk_t, tabs, dskip, name):
    l, d = u.shape
    nj, n2, gb = bblk_t.shape
    n = n2 // 2
    tm = _tile(l, 512)
    nb = l // tm
    nr = tm // SUBLANES

    def body(dy_ref, u_ref, st_ref, prev_ref, bt_ref, ct_ref, t_ref, d_ref,
             du_ref, glam_ref, gd_ref, gb_ref, gc_ref, gx, carry):
        i = pl.program_id(1)
        ib = nb - 1 - i

        @pl.when(i == 0)
        def _():
            carry[...] = jnp.zeros_like(carry)
            glam_ref[...] = jnp.zeros_like(glam_ref)
            gd_ref[...] = jnp.zeros_like(gd_ref)
            gb_ref[...] = jnp.zeros_like(gb_ref)
            gc_ref[...] = jnp.zeros_like(gc_ref)

        dyv = dy_ref[...]
        uv = u_ref[...]
        dyb = dyv.astype(BF16)
        gx[...] = jnp.dot(dyb, ct_ref[0], preferred_element_type=F32)
        last_row = lax.broadcasted_iota(jnp.int32, (SUBLANES, n), 0) == SUBLANES - 1

        def block(rows, xp_r, xp_i, c):
            cr, ci = c
            gr = gx[rows, 0:n]
            gi = gx[rows, n:n2]
            for lvl, sh in enumerate((1, 2, 4)):
                ar = t_ref[0, 2 * lvl]
                ai = t_ref[0, 2 * lvl + 1]
                sr = pltpu.roll(gr, SUBLANES - sh, 0)
                si = pltpu.roll(gi, SUBLANES - sh, 0)
                gr, gi = gr + ar * sr - ai * si, gi + ar * si + ai * sr
            lr = t_ref[0, 6]
            li = t_ref[0, 7]
            gr, gi = gr + lr * cr - li * ci, gi + lr * ci + li * cr
            gx[rows, 0:n] = gr
            gx[rows, n:n2] = gi
            xs_r = pltpu.roll(jnp.where(last_row, xp_r, st_ref[rows, 0:n]), 1, 0)
            xs_i = pltpu.roll(jnp.where(last_row, xp_i, st_ref[rows, n:n2]), 1, 0)
            glam_ref[0, :, 0:n] += gr * xs_r + gi * xs_i
            glam_ref[0, :, n:n2] += gi * xs_r - gr * xs_i
            return (jnp.broadcast_to(gr[0:1, :], (SUBLANES, n)), jnp.broadcast_to(gi[0:1, :], (SUBLANES, n)))

        def step(k, c):
            rb = nr - 1 - k
            rows = pl.ds(pl.multiple_of(rb * SUBLANES, SUBLANES), SUBLANES)
            before = pl.ds(pl.multiple_of(rb * SUBLANES - SUBLANES, SUBLANES), SUBLANES)
            return block(rows, st_ref[before, 0:n], st_ref[before, n:n2], c)

        c = lax.fori_loop(0, nr - 1, step, (carry[:, 0:n], carry[:, n:n2]))
        live = (ib > 0).astype(F32)
        cr, ci = block(pl.ds(0, SUBLANES), prev_ref[:, 0:n] * live, prev_ref[:, n:n2] * live, c)
        carry[:, 0:n] = cr
        carry[:, n:n2] = ci

        gxb = gx[...].astype(BF16)
        du_ref[...] = jnp.dot(gxb, bt_ref[0], preferred_element_type=F32) + d_ref[...] * dyv
        gd_ref[0] += _row_fold(dyv * uv)
        gb_ref[0] += lax.dot_general(uv.astype(BF16), gxb, TN_DIMS, preferred_element_type=F32)
        gc_ref[0] += lax.dot_general(st_ref[...].astype(BF16), dyb, TN_DIMS, preferred_element_type=F32)

    rpb = tm // SUBLANES
    return pl.pallas_call(
        body, name=name, grid=(nj, nb),
        in_specs=[pl.BlockSpec((tm, gb), lambda j, i: (nb - 1 - i, j)),
                  pl.BlockSpec((tm, gb), lambda j, i: (nb - 1 - i, j)),
                  pl.BlockSpec((tm, n2), lambda j, i: (nb - 1 - i, j)),
                  pl.BlockSpec((SUBLANES, n2), lambda j, i: (jnp.maximum((nb - 1 - i) * rpb - 1, 0), j)),
                  pl.BlockSpec((1, n2, gb), lambda j, i: (j, 0, 0)),
                  pl.BlockSpec((1, gb, n2), lambda j, i: (j, 0, 0)),
                  pl.BlockSpec((1, 8, SUBLANES, n), lambda j, i: (j, 0, 0, 0)),
                  pl.BlockSpec((1, gb), lambda j, i: (0, j))],
        out_specs=[pl.BlockSpec((tm, gb), lambda j, i: (nb - 1 - i, j)),
                   pl.BlockSpec((1, SUBLANES, n2), lambda j, i: (j, 0, 0)),
                   pl.BlockSpec((1, SUBLANES, gb), lambda j, i: (j, 0, 0)),
                   pl.BlockSpec((1, gb, n2), lambda j, i: (j, 0, 0)),
                   pl.BlockSpec((1, n2, gb), lambda j, i: (j, 0, 0))],
        out_shape=[jax.ShapeDtypeStruct((l, d), F32),
                   jax.ShapeDtypeStruct((nj, SUBLANES, n2), F32),
                   jax.ShapeDtypeStruct((nj, SUBLANES, gb), F32),
                   jax.ShapeDtypeStruct((nj, gb, n2), F32),
                   jax.ShapeDtypeStruct((nj, n2, gb), F32)],
        scratch_shapes=[pltpu.VMEM((tm, n2), F32), pltpu.VMEM((SUBLANES, n2), F32)],
        compiler_params=_cp("parallel", "arbitrary"),
    )(dy, u, states, states, bblk_t, cblk_t, tabs, dskip)


HEAD_SLOT = 128
PAIR_SLOT = 2 * HEAD_SLOT
ATTN_TILE = 1024
LANE_ROWSUM_P = HEAD_DIM
LANE_COLSUM_DS = HEAD_DIM
LANE_ROWSUM_DS = HEAD_DIM + 1


def _slot_cols(w):
    r, c = w.shape
    nh = c // HEAD_DIM
    return jnp.pad(w.reshape(r, nh, HEAD_DIM), ((0, 0), (0, 0), (0, HEAD_SLOT - HEAD_DIM))).reshape(r, nh * HEAD_SLOT)


def _unslot_cols(w):
    r, c = w.shape
    nh = c // HEAD_SLOT
    return w.reshape(r, nh, HEAD_SLOT)[:, :, :HEAD_DIM].reshape(r, nh * HEAD_DIM)


def _slot_ones(nh, lane):
    return jnp.tile((jnp.arange(HEAD_SLOT) == lane).astype(F32), nh).reshape(1, nh * HEAD_SLOT)


def _causal_tiles(n, by_query):
    if by_query:
        tiles = [(i, j) for i in range(n) for j in range(i + 1)]
    else:
        tiles = [(i, j) for j in range(n) for i in range(j, n)]
    return (jnp.asarray([t[0] for t in tiles], jnp.int32), jnp.asarray([t[1] for t in tiles], jnp.int32))


def _flash_fwd(qx, kx, vx, f2_rows, name):
    l = qx.shape[0]
    npair = qx.shape[1] // PAIR_SLOT
    d = npair * HEAD_PAIR
    tq = _tile(l, ATTN_TILE)
    tk = tq
    i_of, j_of = _causal_tiles(l // tq, by_query=True)

    def body(i_ref, j_ref, q_ref, k_ref, v_ref, f_ref, o_ref, lse_ref, m_sc, acc_sc):
        t = pl.program_id(1)
        i = i_ref[t]
        j = j_ref[t]

        @pl.when(j == 0)
        def _():
            m_sc[...] = jnp.full_like(m_sc, NEG)
            acc_sc[...] = jnp.zeros_like(acc_sc)

        def tile(on_diagonal):
            for hh in range(2):
                hs = slice(hh * HEAD_SLOT, (hh + 1) * HEAD_SLOT)
                s = lax.dot_general(q_ref[:, hs], k_ref[:, hs], NT_DIMS, preferred_element_type=F32)
                s = s - f_ref[0, hh:hh + 1, :]
                if on_diagonal:
                    keep = (lax.broadcasted_iota(jnp.int32, (tq, tk), 0)
                            >= lax.broadcasted_iota(jnp.int32, (tq, tk), 1))
                    s = jnp.where(keep, s, NEG)
                m_prev = m_sc[hh]
                m_new = jnp.maximum(m_prev, jnp.max(s, axis=-1, keepdims=True))
                alpha = jnp.exp2(m_prev - m_new)
                p = jnp.exp2(s - jnp.concatenate([m_new] * (tk // LANES), axis=1)).astype(BF16)
                acc_sc[hh] = alpha * acc_sc[hh] + jnp.dot(p, v_ref[:, hs], preferred_element_type=F32)
                m_sc[hh] = m_new

        @pl.when(j < i)
        def _():
            tile(False)

        @pl.when(j == i)
        def _():
            tile(True)
            outs, lses = [], []
            for hh in range(2):
                acc = acc_sc[hh]
                lsum = acc[:, LANE_ROWSUM_P:LANE_ROWSUM_P + 1]
                outs.append(acc[:, :HEAD_DIM] / lsum)
                lses.append(jnp.broadcast_to(m_sc[hh][:, 0:1] + jnp.log2(lsum), (tq, HEAD_DIM)))
            o_ref[...] = jnp.concatenate(outs, axis=1).astype(BF16)
            lse_ref[...] = jnp.concatenate(lses, axis=1)

    q_map = lambda h, t, i_ref, j_ref: (i_ref[t], h)
    kv_map = lambda h, t, i_ref, j_ref: (j_ref[t], h)
    return pl.pallas_call(
        body, name=name,
        grid_spec=pltpu.PrefetchScalarGridSpec(
            num_scalar_prefetch=2, grid=(npair, i_of.shape[0]),
            in_specs=[pl.BlockSpec((tq, PAIR_SLOT), q_map), pl.BlockSpec((tk, PAIR_SLOT), kv_map),
                      pl.BlockSpec((tk, PAIR_SLOT), kv_map),
                      pl.BlockSpec((1, 2, tk), lambda h, t, i_ref, j_ref: (h, 0, j_ref[t]))],
            out_specs=[pl.BlockSpec((tq, HEAD_PAIR), q_map), pl.BlockSpec((tq, HEAD_PAIR), q_map)],
            scratch_shapes=[pltpu.VMEM((2, tq, LANES), F32), pltpu.VMEM((2, tq, HEAD_SLOT), F32)]),
        out_shape=[jax.ShapeDtypeStruct((l, d), BF16), jax.ShapeDtypeStruct((l, d), F32)],
        compiler_params=_cp("parallel", "arbitrary"),
    )(i_of, j_of, qx, kx, vx, f2_rows)


def _flash_bwd(qx, kx, vx, f2_rep, do, lse_rows, delta_rows, name):
    l = qx.shape[0]
    npair = qx.shape[1] // PAIR_SLOT
    d = npair * HEAD_PAIR
    tq = _tile(l, ATTN_TILE)
    tk = tq
    i_of, j_of = _causal_tiles(l // tq, by_query=False)

    def body(i_ref, j_ref, q_ref, k_ref, v_ref, f_ref, do_ref, lse_ref, dl_ref, dq_ref, dk_ref, dv_ref):
        t = pl.program_id(1)
        i = i_ref[t]
        j = j_ref[t]

        @pl.when(t == 0)
        def _():
            dq_ref[...] = jnp.zeros_like(dq_ref)

        @pl.when(i == j)
        def _():
            dk_ref[...] = jnp.zeros_like(dk_ref)
            dv_ref[...] = jnp.zeros_like(dv_ref)

        def tile(on_diagonal):
            dqs, dks, dvs = [], [], []
            for hh in range(2):
                hs = slice(hh * HEAD_SLOT, (hh + 1) * HEAD_SLOT)
                qh, kh = q_ref[:, hs], k_ref[:, hs]
                vh = v_ref[:, hh * HEAD_SLOT:hh * HEAD_SLOT + HEAD_DIM]
                doh = do_ref[:, hh * HEAD_DIM:(hh + 1) * HEAD_DIM]
                st = lax.dot_general(kh, qh, NT_DIMS, preferred_element_type=F32)
                st = st - jnp.concatenate([f_ref[:, hs]] * (tq // HEAD_SLOT), axis=1)
                pt = jnp.exp2(st - lse_ref[0, hh:hh + 1, :])
                if on_diagonal:
                    keep = (lax.broadcasted_iota(jnp.int32, (tk, tq), 1)
                            >= lax.broadcasted_iota(jnp.int32, (tk, tq), 0))
                    pt = jnp.where(keep, pt, 0.0)
                dpt = lax.dot_general(vh, doh, NT_DIMS, preferred_element_type=F32)
                dsb = (pt * (dpt - dl_ref[0, hh:hh + 1, :])).astype(BF16)
                dvs.append(jnp.dot(pt.astype(BF16), doh, preferred_element_type=F32))
                dks.append(jnp.dot(dsb, qh, preferred_element_type=F32))
                dqs.append(lax.dot_general(dsb, kh, TN_DIMS, preferred_element_type=F32))
            dv_ref[...] += jnp.concatenate(dvs, axis=1)
            dk_ref[...] += jnp.concatenate(dks, axis=1)
            dq_ref[pl.ds(pl.multiple_of(i * tq, tq), tq), :] += jnp.concatenate(dqs, axis=1)

        @pl.when(i > j)
        def _():
            tile(False)

        @pl.when(i == j)
        def _():
            tile(True)

    qmap = lambda h, t, i_ref, j_ref: (i_ref[t], h)
    kmap = lambda h, t, i_ref, j_ref: (j_ref[t], h)
    row_map = lambda h, t, i_ref, j_ref: (h, 0, i_ref[t])
    return pl.pallas_call(
        body, name=name,
        grid_spec=pltpu.PrefetchScalarGridSpec(
            num_scalar_prefetch=2, grid=(npair, i_of.shape[0]),
            in_specs=[pl.BlockSpec((tq, PAIR_SLOT), qmap), pl.BlockSpec((tk, PAIR_SLOT), kmap),
                      pl.BlockSpec((tk, PAIR_SLOT), kmap), pl.BlockSpec((tk, PAIR_SLOT), kmap),
                      pl.BlockSpec((tq, HEAD_PAIR), qmap),
                      pl.BlockSpec((1, 2, tq), row_map), pl.BlockSpec((1, 2, tq), row_map)],
            out_specs=[pl.BlockSpec((l, PAIR_SLOT), lambda h, t, i_ref, j_ref: (0, h)),
                       pl.BlockSpec((tk, PAIR_SLOT), kmap), pl.BlockSpec((tk, HEAD_PAIR), kmap)]),
        out_shape=[jax.ShapeDtypeStruct((l, npair * PAIR_SLOT), F32), jax.ShapeDtypeStruct((l, npair * PAIR_SLOT), F32),
                   jax.ShapeDtypeStruct((l, d), F32)],
        compiler_params=_cp("parallel", "arbitrary"),
    )(i_of, j_of, qx, kx, vx, f2_rep, do, lse_rows, delta_rows)


def _my_place():
    return lax.axis_index("x"), lax.axis_index("y"), lax.axis_index("c")


def _chip_exchange(srcs, out_meta, plan, name):
    n_src, n_out, n_plan = len(srcs), len(out_meta), len(plan)

    def body(*refs):
        src_refs = refs[:n_src]
        out_refs = refs[n_src:n_src + n_out]
        send_sems, recv_sems, local_sems = refs[n_src + n_out:]
        x, y, c = _my_place()
        me = 2 * x + y
        copies = []
        for n, (si, oi, src_view, dst_view) in enumerate(plan):
            local = pltpu.make_async_copy(src_view(src_refs[si], me), dst_view(out_refs[oi], me), local_sems.at[n])
            local.start()
            copies.append(local)
            for k in (1, 2, 3):
                peer = me ^ k
                rc = pltpu.make_async_remote_copy(
                    src_ref=src_view(src_refs[si], peer), dst_ref=dst_view(out_refs[oi], me),
                    send_sem=send_sems.at[n, k - 1], recv_sem=recv_sems.at[n, k - 1],
                    device_id=(peer >> 1, peer & 1, c), device_id_type=MESH)
                rc.start()
                copies.append(rc)
        for cp in copies:
            cp.wait()

    any_spec = pl.BlockSpec(memory_space=pl.ANY)
    return pl.pallas_call(
        body, name=name,
        in_specs=[any_spec] * n_src, out_specs=[any_spec] * n_out,
        out_shape=[jax.ShapeDtypeStruct(shape, dt) for (shape, dt) in out_meta],
        scratch_shapes=[pltpu.SemaphoreType.DMA((n_plan, 3)), pltpu.SemaphoreType.DMA((n_plan, 3)),
                        pltpu.SemaphoreType.DMA((n_plan,))],
    )(*srcs)


def _core_exchange(arrays, name):
    n_items = len(arrays)

    def body(*refs):
        srcs = refs[:n_items]
        outs = refs[n_items:2 * n_items]
        send_sems, recv_sems = refs[2 * n_items:]
        x, y, c = _my_place()
        copies = []
        for n in range(n_items):
            rc = pltpu.make_async_remote_copy(
                src_ref=srcs[n], dst_ref=outs[n], send_sem=send_sems.at[n], recv_sem=recv_sems.at[n],
                device_id=(x, y, 1 - c), device_id_type=MESH)
            rc.start()
            copies.append(rc)
        for cp in copies:
            cp.wait()

    any_spec = pl.BlockSpec(memory_space=pl.ANY)
    return pl.pallas_call(
        body, name=name,
        in_specs=[any_spec] * n_items, out_specs=[any_spec] * n_items,
        out_shape=[jax.ShapeDtypeStruct(a.shape, a.dtype) for a in arrays],
        scratch_shapes=[pltpu.SemaphoreType.DMA((n_items,)), pltpu.SemaphoreType.DMA((n_items,))],
    )(*arrays)


def _sum_chips(parts, name):
    _, rows, cols = parts.shape
    tm = _tile(rows, 512)

    def body(p_ref, o_ref):
        acc = p_ref[0].astype(F32)
        for s in range(1, N_CHIPS):
            acc = acc + p_ref[s].astype(F32)
        o_ref[...] = acc

    return pl.pallas_call(
        body, name=name, grid=(rows // tm,),
        in_specs=[pl.BlockSpec((N_CHIPS, tm, cols), lambda i: (0, i, 0))],
        out_specs=pl.BlockSpec((tm, cols), lambda i: (i, 0)),
        out_shape=jax.ShapeDtypeStruct((rows, cols), F32),
        compiler_params=_cp("parallel"),
    )(parts)


def _adamw(ga, gb, w, m, v, name):
    rows, cols = w.shape
    tm = _tile(rows, 512)
    c1 = 1.0 - ADAM_B1 ** ADAM_STEP
    c2 = 1.0 - ADAM_B2 ** ADAM_STEP

    def body(ga_ref, gb_ref, w_ref, m_ref, v_ref, g_ref, d_ref, nm_ref, nv_ref):
        g = ga_ref[...] + gb_ref[...]
        nm = ADAM_B1 * m_ref[...] + (1.0 - ADAM_B1) * g
        nv = ADAM_B2 * v_ref[...] + (1.0 - ADAM_B2) * (g * g)
        g_ref[...] = g
        nm_ref[...] = nm
        nv_ref[...] = nv
        d_ref[...] = -ADAM_LR * ((nm / c1) / (jnp.sqrt(nv / c2) + ADAM_EPS) + ADAM_WD * w_ref[...])

    spec = pl.BlockSpec((tm, cols), lambda i: (i, 0))
    return pl.pallas_call(
        body, name=name, grid=(rows // tm,), in_specs=[spec] * 5, out_specs=[spec] * 4,
        out_shape=[jax.ShapeDtypeStruct((rows, cols), F32)] * 4, compiler_params=_cp("parallel"),
    )(ga, gb, w, m, v)


def _ssm_discretise(log_dt, a_re, a_im, b_re, b_im):
    dt = jnp.exp(log_dt)[:, None]
    mag = jnp.exp(a_re * dt)
    lbr = mag * jnp.cos(a_im * dt)
    lbi = mag * jnp.sin(a_im * dt)
    den = a_re * a_re + a_im * a_im
    nr, ni = lbr - 1.0, lbi
    qr = (nr * a_re + ni * a_im) / den
    qi = (ni * a_re - nr * a_im) / den
    bbr = qr[..., None] * b_re - qi[..., None] * b_im
    bbi = qr[..., None] * b_im + qi[..., None] * b_re
    return lbr, lbi, bbr, bbi


def _cmul(ar, ai, br, bi):
    return ar * br - ai * bi, ar * bi + ai * br


def _scan_tables(lr, li, nj, reverse):
    lr = lr.reshape(nj, 1, -1)
    li = li.reshape(nj, 1, -1)
    if reverse:
        li = -li
    pows = [(lr, li)]
    for _ in range(7):
        pows.append(_cmul(*pows[-1], lr, li))
    r = jnp.arange(SUBLANES).reshape(1, SUBLANES, 1)
    if reverse:
        r = SUBLANES - 1 - r
    out = []
    for k in (1, 2, 4):
        pr, pi = pows[k - 1]
        keep = (r >= k).astype(F32)
        out += [pr * keep, pi * keep]
    shape = (nj, SUBLANES, lr.shape[-1])
    cr = jnp.zeros(shape, F32)
    ci = jnp.zeros(shape, F32)
    for e in range(SUBLANES):
        sel = (r == e).astype(F32)
        cr = cr + sel * pows[e][0]
        ci = ci + sel * pows[e][1]
    out += [cr, ci]
    return jnp.stack(out, axis=1)


def _group_eye(gl):
    return jnp.eye(gl, dtype=F32)


def _block_diag_in(bbr, bbi, nj):
    g, p, c = bbr.shape
    gl = g // nj
    eye = _group_eye(gl)[None, :, None, :, None]

    def one(b):
        t = b.reshape(nj, gl, p, c).transpose(0, 1, 3, 2)[:, :, :, None, :]
        return (t * eye).reshape(nj, gl * c, gl * p)

    return jnp.concatenate([one(bbr), one(bbi)], axis=2)


def _block_diag_in_grad(gmat, nj, p, c):
    gl = gmat.shape[1] // c
    n = gl * p
    eye = _group_eye(gl)[None, :, None, :, None]

    def one(m):
        t = jnp.sum(m.reshape(nj, gl, c, gl, p) * eye, axis=3)
        return t.transpose(0, 1, 3, 2).reshape(nj * gl, p, c)

    return one(gmat[:, :, :n]), one(gmat[:, :, n:])


def _block_diag_out(c_re, c_im, nj):
    g, c, p = c_re.shape
    gl = g // nj
    eye = _group_eye(gl)[None, :, None, :, None]

    def one(m):
        t = m.reshape(nj, gl, c, p).transpose(0, 1, 3, 2)[:, :, :, None, :]
        return (t * eye).reshape(nj, gl * p, gl * c)

    return jnp.concatenate([one(c_re), -one(c_im)], axis=1)


def _block_diag_out_grad(gmat, nj, p, c):
    gl = gmat.shape[2] // c
    n = gl * p
    eye = _group_eye(gl)[None, :, None, :, None]

    def one(m):
        t = jnp.sum(m.reshape(nj, gl, p, gl, c) * eye, axis=3)
        return t.transpose(0, 1, 3, 2).reshape(nj * gl, c, p)

    return one(gmat[:, :n, :]), -one(gmat[:, n:, :])


def _pad_rows(flat, cols):
    per = SUBLANES * cols
    n = flat.shape[0]
    total = -(-n // per) * per
    return jnp.pad(flat, (0, total - n)).reshape(total // cols, cols)


def _pack_small(arrs, cols):
    packed = jnp.concatenate([_pad_rows(a.reshape(-1), cols) for a in arrs], axis=0)
    rows = packed.shape[0]
    return jnp.pad(packed, ((0, -rows % 128), (0, 0)))


def _unpack_small(packed, shapes, cols):
    out = []
    row = 0
    for s in shapes:
        n = math.prod(s)
        rows = -(-n // (SUBLANES * cols)) * SUBLANES
        out.append(packed[row:row + rows].reshape(-1)[:n].reshape(s))
        row += rows
    return out


def kernel(x, mix_norm, mlp_norm, mlp_w1, mlp_w2, ssm_log_dt, ssm_a_re, ssm_a_im, ssm_b_re, ssm_b_im, ssm_c_re, ssm_c_im, ssm_d, ssm_w_glu, kv_norm, w_kvf, b_f, attn_wq, attn_wo, final_norm, loss_target, m_mix_norm, m_mlp_norm, m_mlp_w1, m_mlp_w2, m_ssm_log_dt, m_ssm_a_re, m_ssm_a_im, m_ssm_b_re, m_ssm_b_im, m_ssm_c_re, m_ssm_c_im, m_ssm_d, m_ssm_w_glu, m_kv_norm, m_w_kvf, m_b_f, m_attn_wq, m_attn_wo, m_final_norm, v_mix_norm, v_mlp_norm, v_mlp_w1, v_mlp_w2, v_ssm_log_dt, v_ssm_a_re, v_ssm_a_im, v_ssm_b_re, v_ssm_b_im, v_ssm_c_re, v_ssm_c_im, v_ssm_d, v_ssm_w_glu, v_kv_norm, v_w_kvf, v_b_f, v_attn_wq, v_attn_wo, v_final_norm):
    seq, d = x.shape[1], x.shape[2]
    depth = mix_norm.shape[0]
    n_a = ssm_log_dt.shape[0]
    n_b = depth - n_a
    ff = mlp_w1.shape[2] * N_CHIPS
    n_heads = d // HEAD_DIM
    n_groups = d // SSM_GROUP
    p_state = ssm_a_re.shape[2]
    gb = min(d, 256)
    nj = d // gb
    kvf_cols = w_kvf.shape[1]
    ds4, dq4 = d // N_CHIPS, d // (2 * N_CHIPS)
    chip = 2 * lax.axis_index("x") + lax.axis_index("y")

    def cols_of(width):
        return lambda ref, s: ref.at[:, :, pl.ds(pl.multiple_of(s * width, LANES), width)]

    def rows_of(height):
        return lambda ref, s: ref.at[:, pl.ds(pl.multiple_of(s * height, SUBLANES), height), :]

    whole = lambda ref, s: ref
    slot = lambda ref, s: ref.at[s]
    shards = [mlp_w1.astype(BF16), mlp_w2.astype(BF16), ssm_w_glu.astype(BF16), w_kvf.astype(BF16),
              attn_wq.astype(BF16), attn_wo.astype(BF16), ssm_d]
    gathered = _chip_exchange(
        shards,
        [((depth, d, ff), BF16), ((depth, ff, d), BF16), ((n_a, d, 2 * d), BF16), ((N_CHIPS, d, kvf_cols), BF16),
         ((n_b, d, d), BF16), ((n_b, d, d), BF16), ((N_CHIPS, n_a, ds4), F32)],
        [(0, 0, whole, cols_of(d)), (1, 1, whole, rows_of(d)), (2, 2, whole, cols_of(2 * ds4)), (3, 3, whole, slot),
         (4, 4, whole, rows_of(ds4)), (5, 5, whole, rows_of(ds4)), (6, 6, whole, slot)],
        "gather_weights")
    w1_all, w2_all, wglu_all, kvf_parts, wq_all, wo_all, skip_parts = gathered
    skip_all = skip_parts.transpose(1, 0, 2).reshape(n_a, d)
    kvf_all = jnp.concatenate([kvf_parts[s] for s in range(N_CHIPS)], axis=1)
    wk = kvf_all[:, :d]
    wv = kvf_all[:, d:2 * d]
    wf = jnp.pad(kvf_all[:, 2 * d:], ((0, 0), (0, LANES - n_heads)))
    bf_row = jnp.pad(b_f, (0, LANES - n_heads)).reshape(1, LANES)

    h = x[0]
    target = loss_target[0]

    saved = []
    for i in range(n_a):
        lbr, lbi, bbr, bbi = _ssm_discretise(ssm_log_dt[i], ssm_a_re[i], ssm_a_im[i], ssm_b_re[i], ssm_b_im[i])
        bblk = _block_diag_in(bbr, bbi, nj)
        cblk = _block_diag_out(ssm_c_re[i], ssm_c_im[i], nj)
        rec = dict(h0=h, lam=(lbr, lbi), bblk=bblk, cblk=cblk)
        u = _norm_fwd(h, mix_norm[i:i + 1], f"s5_norm_{i}")
        rec["u"] = u
        dskip = rec["dskip"] = skip_all[i:i + 1]
        states, y = _scan_fwd(u, bblk.astype(BF16), cblk.astype(BF16), _scan_tables(lbr, lbi, nj, False), dskip,
                              f"s5_scan_{i}")
        rec["states"], rec["y"] = states, y
        h, rec["zw"] = _s5_post_fwd(h, y, wglu_all[i], f"s5_glu_{i}")
        rec["h1"] = h
        h, rec["ap"] = _mlp_fwd(h, mlp_norm[i:i + 1], w1_all[i], w2_all[i], f"mlp_{i}")
        saved.append(rec)
    h_kv = h
    wk_x = _slot_cols(wk)
    kx, vx, flog, cum = _kvf_fwd(h, kv_norm.reshape(1, d), wk_x, _slot_cols(wv), wf, bf_row,
                                 _slot_ones(n_heads, LANE_ROWSUM_DS), _slot_ones(n_heads, LANE_ROWSUM_P), "kvf")
    f2 = cum[:, :n_heads] * LOG2E
    f2_rows = f2.T.reshape(n_heads // 2, 2, seq)
    f2_rep = jnp.broadcast_to(f2[:, :, None], (seq, n_heads, HEAD_SLOT)).reshape(seq, n_heads * HEAD_SLOT)
    wq_x = [_slot_cols(wq_all[jb]) for jb in range(n_b)]
    for jb in range(n_b):
        i = n_a + jb
        rec = dict(h0=h)
        qx = _q_fwd(h, mix_norm[i:i + 1], wq_x[jb], _slot_ones(n_heads, LANE_COLSUM_DS), f"attn_q_{jb}")
        o, lse = _flash_fwd(qx, kx, vx, f2_rows, f"attn_core_{jb}")
        rec["qx"], rec["o"], rec["lse"] = qx, o, lse
        h = _o_fwd(h, o, wo_all[jb], f"attn_out_{jb}")
        rec["h1"] = h
        h, rec["ap"] = _mlp_fwd(h, mlp_norm[i:i + 1], w1_all[i], w2_all[i], f"mlp_{i}")
        saved.append(rec)
    dh, loss_row, g_final = _loss_head(h, target, final_norm.reshape(1, d), "loss_head")
    loss = lax.psum(loss_row[0, 0], ("x", "y", "c"))

    head_ones = (jnp.arange(d)[:, None] // HEAD_DIM == jnp.arange(d)[None, :] // HEAD_DIM).astype(BF16)
    g_mix = [None] * depth
    g_mlp = [None] * depth
    g_w1 = [None] * depth
    g_w2 = [None] * depth
    g_wq = [None] * n_b
    g_wo = [None] * n_b
    g_glu = [None] * n_a
    g_ssm = [None] * n_a
    dk_parts, dv_parts, df_parts = [], [], []

    def head_rows(rep):
        return rep[:, ::HEAD_DIM].T.reshape(n_heads // 2, 2, seq)

    def mlp_back(dh, i, rec):
        dh_in, hm, a, dap, g_mlp[i] = _mlp_bwd(dh, rec["h1"], rec["ap"], mlp_norm[i:i + 1], w1_all[i], w2_all[i],
                                               f"mlp_bwd_{i}")
        g_w2[i] = _matmul_tn(a, dh, f"mlp_dw2_{i}")
        g_w1[i] = _matmul_tn(hm, dap, f"mlp_dw1_{i}")
        return dh_in

    for jb in reversed(range(n_b)):
        i = n_a + jb
        rec = saved[i]
        dh = mlp_back(dh, i, rec)
        do, delta = _o_bwd(dh, rec["o"], wo_all[jb], head_ones, f"attn_out_bwd_{jb}")
        g_wo[jb] = _matmul_tn(rec["o"], dh, f"attn_dwo_{jb}")
        dqx, dkx, dv = _flash_bwd(rec["qx"], kx, vx, f2_rep, do, head_rows(rec["lse"]), head_rows(delta),
                                  f"attn_core_bwd_{jb}")
        dk_parts.append(dkx)
        dv_parts.append(dv)
        df_parts.append(dqx[:, LANE_ROWSUM_DS::HEAD_SLOT] - dkx[:, LANE_COLSUM_DS::HEAD_SLOT])
        dh, hn, dqs, g_mix[i] = _q_bwd(dh, rec["h0"], dqx, mix_norm[i:i + 1], wq_x[jb], f"attn_q_bwd_{jb}")
        g_wq[jb] = _unslot_cols(_matmul_tn(hn, dqs, f"attn_dwq_{jb}"))

    dft = df_parts[0]
    for extra in df_parts[1:]:
        dft = dft + extra
    dcum = jnp.pad(dft, ((0, 0), (0, LANES - n_heads)))
    dh, hk, dkb, dvb, dfb, g_kvn, g_bf = _kvf_bwd(dh, h_kv, dk_parts[0], dk_parts[1], dv_parts[0], dv_parts[1],
                                                  dcum, flog, kv_norm.reshape(1, d), wk_x, wv, wf, "kvf_bwd")
    g_kvf = jnp.concatenate([_unslot_cols(_matmul_tn(hk, dkb, "kvf_dwk")), _matmul_tn(hk, dvb, "kvf_dwv"),
                             _matmul_tn(hk, dfb, "kvf_dwf")[:, :n_heads]], axis=1)

    for i in reversed(range(n_a)):
        rec = saved[i]
        dh = mlp_back(dh, i, rec)
        dy, z, dzw = _s5_post_bwd(dh, rec["y"], rec["zw"], wglu_all[i], f"s5_glu_bwd_{i}")
        g_glu[i] = _matmul_tn(z, dzw, f"s5_dwglu_{i}")
        lbr, lbi = rec["lam"]
        bblk_t = rec["bblk"].transpose(0, 2, 1).astype(BF16)
        cblk_t = rec["cblk"].transpose(0, 2, 1).astype(BF16)
        du, glam8, gd8, gbblk, gcblk = _scan_bwd(dy, rec["u"], rec["states"], bblk_t, cblk_t,
                                                 _scan_tables(lbr, lbi, nj, True), rec["dskip"], f"s5_scan_bwd_{i}")
        dh, g_mix[i] = _norm_bwd_add(dh, du, rec["h0"], mix_norm[i:i + 1], f"s5_norm_bwd_{i}")
        glam = jnp.sum(glam8, axis=1)
        n_st = glam.shape[1] // 2
        g_lbr = glam[:, :n_st].reshape(n_groups, p_state)
        g_lbi = glam[:, n_st:].reshape(n_groups, p_state)
        g_bbr, g_bbi = _block_diag_in_grad(gbblk, nj, p_state, SSM_GROUP)
        g_cre, g_cim = _block_diag_out_grad(gcblk, nj, p_state, SSM_GROUP)
        _, pull = jax.vjp(_ssm_discretise, ssm_log_dt[i], ssm_a_re[i], ssm_a_im[i], ssm_b_re[i], ssm_b_im[i])
        g_ldt, g_are, g_aim, g_bre, g_bim = pull((g_lbr, g_lbi, g_bbr, g_bbi))
        g_ssm[i] = dict(log_dt=g_ldt, a_re=g_are, a_im=g_aim, b_re=g_bre, b_im=g_bim, c_re=g_cre, c_im=g_cim,
                        d=jnp.sum(gd8, axis=1).reshape(d))
    grad_x = dh[None]

    def stack_small(key):
        return jnp.stack([g_ssm[i][key] for i in range(n_a)])

    small_grads = [jnp.concatenate(g_mix, axis=0), jnp.concatenate(g_mlp, axis=0), stack_small("log_dt"),
                   stack_small("a_re"), stack_small("a_im"), stack_small("b_re"), stack_small("b_im"),
                   stack_small("c_re"), stack_small("c_im"), stack_small("d"), g_kvn.reshape(d),
                   g_bf[0, :n_heads], g_final.reshape(d)]
    small_w = [mix_norm, mlp_norm, ssm_log_dt, ssm_a_re, ssm_a_im, ssm_b_re, ssm_b_im, ssm_c_re, ssm_c_im,
               ssm_d, kv_norm, b_f, final_norm]
    small_m = [m_mix_norm, m_mlp_norm, m_ssm_log_dt, m_ssm_a_re, m_ssm_a_im, m_ssm_b_re, m_ssm_b_im, m_ssm_c_re,
               m_ssm_c_im, m_ssm_d, m_kv_norm, m_b_f, m_final_norm]
    small_v = [v_mix_norm, v_mlp_norm, v_ssm_log_dt, v_ssm_a_re, v_ssm_a_im, v_ssm_b_re, v_ssm_b_im, v_ssm_c_re,
               v_ssm_c_im, v_ssm_d, v_kv_norm, v_b_f, v_final_norm]
    skip_at = 9

    def widen_skip(part):
        return lax.dynamic_update_slice(jnp.zeros((n_a, d), F32), part, (0, chip * ds4))

    small_shapes = [a.shape for a in small_grads]
    pcols = 1024 if d >= 1024 else LANES
    g_small = _pack_small(small_grads, pcols)
    expand = lambda lst: _pack_small([widen_skip(a) if n == skip_at else a for n, a in enumerate(lst)], pcols)
    w_small, m_small, v_small = expand(small_w), expand(small_m), expand(small_v)
    srows = g_small.shape[0]

    kvf_send = g_kvf.reshape(d, N_CHIPS, kvf_cols).transpose(1, 0, 2)

    def cols2(width):
        return lambda ref, s: ref.at[:, pl.ds(pl.multiple_of(s * width, LANES), width)]

    def rows2(height):
        return lambda ref, s: ref.at[pl.ds(pl.multiple_of(s * height, SUBLANES), height), :]

    def into(layer):
        return lambda ref, s: ref.at[s, layer]

    assert n_b == 2
    red_srcs, red_plan = [], []

    def send(arr, out_index, src_view, dst_view):
        red_plan.append((len(red_srcs), out_index, src_view, dst_view))
        red_srcs.append(arr)

    for i in range(depth):
        send(g_w1[i], 0, cols2(d), into(i))
    for i in range(depth):
        send(g_w2[i], 1, rows2(d), into(i))
    for i in range(n_a):
        send(g_glu[i], 2, cols2(2 * ds4), into(i))
    send(kvf_send, 3, slot, slot)
    for jb in range(n_b):
        send(g_wq[jb], 4, rows2(ds4), into(jb))
    for jb in range(n_b):
        send(g_wo[jb], 5, rows2(ds4), into(jb))
    send(g_small, 6, whole, slot)
    received = _chip_exchange(
        red_srcs,
        [((N_CHIPS, depth, d, d), BF16), ((N_CHIPS, depth, d, d), BF16), ((N_CHIPS, n_a, d, 2 * ds4), BF16),
         ((N_CHIPS, d, kvf_cols), BF16), ((N_CHIPS, n_b, ds4, d), BF16), ((N_CHIPS, n_b, ds4, d), BF16),
         ((N_CHIPS, srows, pcols), F32)],
        red_plan, "reduce_chips")
    r_w1, r_w2, r_glu, r_kvf, r_wq, r_wo, r_small = received

    def flat(a):
        return a.reshape(N_CHIPS, -1, a.shape[-1])

    sums = [_sum_chips(flat(r), f"sum_chips_{n}") for n, r in
            enumerate((r_w1, r_w2, r_glu, r_kvf, r_wq, r_wo, r_small))]
    others = _core_exchange(sums, "reduce_cores")

    def two(a):
        return a.reshape(-1, a.shape[-1])

    big_w = [(mlp_w1, m_mlp_w1, v_mlp_w1), (mlp_w2, m_mlp_w2, v_mlp_w2), (ssm_w_glu, m_ssm_w_glu, v_ssm_w_glu),
             (w_kvf, m_w_kvf, v_w_kvf), (attn_wq, m_attn_wq, v_attn_wq), (attn_wo, m_attn_wo, v_attn_wo)]
    big_out = []
    for n, (w, m, v) in enumerate(big_w):
        res = _adamw(sums[n], others[n], two(w), two(m), two(v), f"adamw_{n}")
        big_out.append([r.reshape(w.shape) for r in res])
    small_out = _adamw(sums[6], others[6], w_small, m_small, v_small, "adamw_small")

    def narrow_skip(a):
        return lax.dynamic_slice(a, (0, chip * ds4), (n_a, ds4))

    unpacked = []
    for packed in small_out:
        parts = _unpack_small(packed, small_shapes, pcols)
        parts[skip_at] = narrow_skip(parts[skip_at])
        unpacked.append(parts)

    order = ["mix_norm", "mlp_norm", "mlp_w1", "mlp_w2", "ssm_log_dt", "ssm_a_re", "ssm_a_im", "ssm_b_re",
             "ssm_b_im", "ssm_c_re", "ssm_c_im", "ssm_d", "ssm_w_glu", "kv_norm", "w_kvf", "b_f", "attn_wq",
             "attn_wo", "final_norm"]
    small_names = ["mix_norm", "mlp_norm", "ssm_log_dt", "ssm_a_re", "ssm_a_im", "ssm_b_re", "ssm_b_im",
                   "ssm_c_re", "ssm_c_im", "ssm_d", "kv_norm", "b_f", "final_norm"]
    big_names = ["mlp_w1", "mlp_w2", "ssm_w_glu", "w_kvf", "attn_wq", "attn_wo"]
    outs = [loss, grad_x]
    for kind in range(4):
        for name in order:
            if name in big_names:
                outs.append(big_out[big_names.index(name)][kind])
            else:
                outs.append(unpacked[kind][small_names.index(name)])
    return tuple(outs)
```

```python
import functools
import math

import jax
import jax.numpy as jnp
from jax import lax
from jax.experimental import pallas as pl
from jax.experimental.pallas import tpu as pltpu

F32 = jnp.float32
BF16 = jnp.bfloat16

RMS_EPS = 1e-6
SSM_GROUP = 16
SSM_STATE = 64
HEAD_DIM = 64
HEAD_PAIR = 2 * HEAD_DIM
LANES = 128
SUBLANES = 8
N_CHIPS = 4
ADAM_LR = 0.001
ADAM_B1 = 0.9
ADAM_B2 = 0.999
ADAM_EPS = 1e-08
ADAM_WD = 0.01
ADAM_STEP = 10
GELU_C = math.sqrt(2.0 / math.pi)
GELU_A = 0.044715
NEG = -1e30
LN2 = math.log(2.0)
LOG2E = 1.0 / LN2
VMEM_LIMIT = 56 * 1024 * 1024
MESH = pl.DeviceIdType.MESH

NT_DIMS = (((1,), (1,)), ((), ()))
TN_DIMS = (((0,), (0,)), ((), ()))


def _cp(*sem):
    return pltpu.CompilerParams(dimension_semantics=sem if sem else None, vmem_limit_bytes=VMEM_LIMIT)


def _zero_idx(nd, *_):
    return (0,) * nd


def _tile(n, t):
    if n <= t:
        return n
    for cand in range(t - t % SUBLANES, 0, -SUBLANES):
        if n % cand == 0:
            return cand
    raise ValueError((n, t))


def _rms_fwd(h, g):
    r = lax.rsqrt(jnp.mean(h * h, axis=-1, keepdims=True) + RMS_EPS)
    hhat = h * r
    return hhat * g, hhat, r


def _rms_bwd(du, hhat, r, g):
    dhh = du * g
    dh = r * (dhh - hhat * jnp.mean(dhh * hhat, axis=-1, keepdims=True))
    return dh, du * hhat


def _sigmoid(x):
    return 1.0 / (1.0 + jnp.exp(-x))


def _gelu(x):
    t = jnp.tanh(GELU_C * (x + GELU_A * x * x * x))
    return 0.5 * x * (1.0 + t)


def _gelu_grad(x):
    t = jnp.tanh(GELU_C * (x + GELU_A * x * x * x))
    return 0.5 * (1.0 + t) + 0.5 * x * (1.0 - t * t) * GELU_C * (1.0 + 3.0 * GELU_A * x * x)


def _row_fold(x):
    tm, w = x.shape
    return jnp.sum(x.reshape(tm // SUBLANES, SUBLANES, w), axis=0)


def _split3(x):
    hi = x.astype(BF16)
    r1 = x - hi.astype(F32)
    mid = r1.astype(BF16)
    lo = (r1 - mid.astype(F32)).astype(BF16)
    return hi, mid, lo


def _exact_dot(ones_mat, x):
    hi, mid, lo = _split3(x)
    d = functools.partial(jnp.dot, preferred_element_type=F32)
    return d(ones_mat, hi) + d(ones_mat, mid) + d(ones_mat, lo)


def _rows_call(body, name, tm, row_ins, const_ins, row_outs, acc_outs=(), scratch=(), reverse=False):
    n = row_ins[0].shape[0]
    nb = n // tm
    if reverse:
        ridx = lambda i: (nb - 1 - i, 0)
    else:
        ridx = lambda i: (i, 0)
    in_specs = [pl.BlockSpec((tm, a.shape[1]), ridx) for a in row_ins]
    in_specs += [pl.BlockSpec(a.shape, functools.partial(_zero_idx, a.ndim), pipeline_mode=pl.Buffered(1))
                 for a in const_ins]
    out_shape = [jax.ShapeDtypeStruct((n, w), dt) for (w, dt) in row_outs]
    out_shape += [jax.ShapeDtypeStruct(s, dt) for (s, dt) in acc_outs]
    out_specs = [pl.BlockSpec((tm, w), ridx) for (w, dt) in row_outs]
    out_specs += [pl.BlockSpec(s, functools.partial(_zero_idx, len(s))) for (s, dt) in acc_outs]
    return pl.pallas_call(
        body, name=name, grid=(nb,), in_specs=in_specs, out_specs=out_specs, out_shape=out_shape,
        scratch_shapes=list(scratch), compiler_params=_cp("arbitrary"),
    )(*row_ins, *const_ins)


def _norm_fwd(h, g, name):
    n, d = h.shape
    tm = _tile(n, 512)

    def body(h_ref, g_ref, u_ref):
        u_ref[...] = _rms_fwd(h_ref[...], g_ref[...])[0]

    return _rows_call(body, name, tm, [h], [g], [(d, F32)])[0]


def _norm_bwd_add(dh, du, h, g, name):
    n, d = h.shape
    tm = _tile(n, 512)
    nb = n // tm

    def body(dh_ref, du_ref, h_ref, g_ref, o_ref, dg_ref, acc):
        i = pl.program_id(0)

        @pl.when(i == 0)
        def _():
            acc[...] = jnp.zeros_like(acc)

        gain = g_ref[...]
        _, hhat, r = _rms_fwd(h_ref[...], gain)
        dhn, dgr = _rms_bwd(du_ref[...], hhat, r, gain)
        o_ref[...] = dh_ref[...] + dhn
        acc[...] += _row_fold(dgr)

        @pl.when(i == nb - 1)
        def _():
            dg_ref[...] = jnp.sum(acc[...], axis=0, keepdims=True)

    return _rows_call(body, name, tm, [dh, du, h], [g], [(d, F32)], [((1, d), F32)],
                      [pltpu.VMEM((SUBLANES, d), F32)])


def _mlp_fwd(h, g, w1, w2, name):
    n, d = h.shape
    ff = w1.shape[1]
    tm = _tile(n, 256)
    fc = _tile(ff, 1024)

    def body(h_ref, g_ref, w1_ref, w2_ref, o_ref, ap_ref):
        hin = h_ref[...]
        hb = _rms_fwd(hin, g_ref[...])[0].astype(BF16)
        acc = hin
        for c in range(ff // fc):
            cs = slice(c * fc, (c + 1) * fc)
            ap = jnp.dot(hb, w1_ref[:, cs], preferred_element_type=F32)
            ap_ref[:, cs] = ap.astype(BF16)
            rl = jnp.maximum(ap, 0.0)
            acc = acc + jnp.dot((rl * rl).astype(BF16), w2_ref[cs, :], preferred_element_type=F32)
        o_ref[...] = acc

    return _rows_call(body, name, tm, [h], [g, w1, w2], [(d, F32), (ff, BF16)])


def _mlp_bwd(dh, h, ap, g, w1, w2, name):
    n, d = h.shape
    ff = w1.shape[1]
    tm = _tile(n, 256)
    nb = n // tm
    fc = _tile(ff, 1024)

    def body(dh_ref, h_ref, ap_ref, g_ref, w1_ref, w2_ref, o_ref, hm_ref, a_ref, dap_ref, dg_ref, acc):
        i = pl.program_id(0)

        @pl.when(i == 0)
        def _():
            acc[...] = jnp.zeros_like(acc)

        gain = g_ref[...]
        dhv = dh_ref[...]
        hm, hhat, r = _rms_fwd(h_ref[...], gain)
        hm_ref[...] = hm.astype(BF16)
        dhb = dhv.astype(BF16)
        dhm = jnp.zeros((tm, d), F32)
        for c in range(ff // fc):
            cs = slice(c * fc, (c + 1) * fc)
            rl = jnp.maximum(ap_ref[:, cs].astype(F32), 0.0)
            a_ref[:, cs] = (rl * rl).astype(BF16)
            da = lax.dot_general(dhb, w2_ref[cs, :], NT_DIMS, preferred_element_type=F32)
            dap = (da * (2.0 * rl)).astype(BF16)
            dap_ref[:, cs] = dap
            dhm = dhm + lax.dot_general(dap, w1_ref[:, cs], NT_DIMS, preferred_element_type=F32)
        dhn, dgr = _rms_bwd(dhm, hhat, r, gain)
        o_ref[...] = dhv + dhn
        acc[...] += _row_fold(dgr)

        @pl.when(i == nb - 1)
        def _():
            dg_ref[...] = jnp.sum(acc[...], axis=0, keepdims=True)

    return _rows_call(body, name, tm, [dh, h, ap], [g, w1, w2],
                      [(d, F32), (d, BF16), (ff, BF16), (ff, BF16)], [((1, d), F32)],
                      [pltpu.VMEM((SUBLANES, d), F32)])


def _s5_post_fwd(h, y, w_glu, name):
    n, d = h.shape
    tm = _tile(n, 512)

    def body(h_ref, y_ref, w_ref, o_ref, zw_ref):
        z = _gelu(y_ref[...]).astype(BF16)
        zw = jnp.dot(z, w_ref[...], preferred_element_type=F32)
        zw_ref[...] = zw.astype(BF16)
        o_ref[...] = h_ref[...] + zw[:, :d] * _sigmoid(zw[:, d:])

    return _rows_call(body, name, tm, [h, y], [w_glu], [(d, F32), (2 * d, BF16)])


def _s5_post_bwd(dh, y, zw, w_glu, name):
    n, d = dh.shape
    tm = _tile(n, 512)

    def body(dh_ref, y_ref, zw_ref, w_ref, dy_ref, z_ref, dzw_ref):
        dhv = dh_ref[...]
        yv = y_ref[...]
        val = zw_ref[:, :d].astype(F32)
        sg = _sigmoid(zw_ref[:, d:].astype(F32))
        dzw = jnp.concatenate([dhv * sg, dhv * val * sg * (1.0 - sg)], axis=1).astype(BF16)
        dzw_ref[...] = dzw
        dz = lax.dot_general(dzw, w_ref[...], NT_DIMS, preferred_element_type=F32)
        dy_ref[...] = dz * _gelu_grad(yv)
        z_ref[...] = _gelu(yv).astype(BF16)

    return _rows_call(body, name, tm, [dh, y, zw], [w_glu], [(d, F32), (d, BF16), (2 * d, BF16)])


def _q_fwd(h, g, wq_x, ones_x, name):
    n, d = h.shape
    tm = _tile(n, 512)
    scale = LOG2E * HEAD_DIM ** -0.5

    def body(h_ref, g_ref, w_ref, one_ref, q_ref):
        hb = _rms_fwd(h_ref[...], g_ref[...])[0].astype(BF16)
        q_ref[...] = (jnp.dot(hb, w_ref[...], preferred_element_type=F32) * scale + one_ref[...]).astype(BF16)

    return _rows_call(body, name, tm, [h], [g, wq_x, ones_x], [(wq_x.shape[1], BF16)])[0]


def _q_bwd(dh, h, dq, g, wq, name):
    n, d = h.shape
    tm = _tile(n, 512)
    nb = n // tm
    scale = HEAD_DIM ** -0.5

    def body(dh_ref, h_ref, dq_ref, g_ref, w_ref, o_ref, hn_ref, dqs_ref, dg_ref, acc):
        i = pl.program_id(0)

        @pl.when(i == 0)
        def _():
            acc[...] = jnp.zeros_like(acc)

        gain = g_ref[...]
        hn, hhat, r = _rms_fwd(h_ref[...], gain)
        hn_ref[...] = hn.astype(BF16)
        dqs = (dq_ref[...] * scale).astype(BF16)
        dqs_ref[...] = dqs
        dhn = lax.dot_general(dqs, w_ref[...], NT_DIMS, preferred_element_type=F32)
        dhi, dgr = _rms_bwd(dhn, hhat, r, gain)
        o_ref[...] = dh_ref[...] + dhi
        acc[...] += _row_fold(dgr)

        @pl.when(i == nb - 1)
        def _():
            dg_ref[...] = jnp.sum(acc[...], axis=0, keepdims=True)

    return _rows_call(body, name, tm, [dh, h, dq], [g, wq], [(d, F32), (d, BF16), (wq.shape[1], BF16)],
                      [((1, d), F32)], [pltpu.VMEM((SUBLANES, d), F32)])


def _o_fwd(h, o, wo, name):
    n, d = h.shape
    tm = _tile(n, 512)

    def body(h_ref, o_ref, w_ref, out_ref):
        out_ref[...] = h_ref[...] + jnp.dot(o_ref[...], w_ref[...], preferred_element_type=F32)

    return _rows_call(body, name, tm, [h, o], [wo], [(d, F32)])[0]


def _o_bwd(dh, o, wo, head_ones, name):
    n, d = dh.shape
    tm = _tile(n, 512)

    def body(dh_ref, o_ref, w_ref, e_ref, do_ref, dl_ref):
        do = lax.dot_general(dh_ref[...].astype(BF16), w_ref[...], NT_DIMS, preferred_element_type=F32).astype(BF16)
        do_ref[...] = do
        dl_ref[...] = _exact_dot_rhs(do.astype(F32) * o_ref[...].astype(F32), e_ref[...])

    return _rows_call(body, name, tm, [dh, o], [wo, head_ones], [(wo.shape[0], BF16), (wo.shape[0], F32)])


def _exact_dot_rhs(x, ones_mat):
    hi, mid, lo = _split3(x)
    d = functools.partial(jnp.dot, preferred_element_type=F32)
    return d(hi, ones_mat) + d(mid, ones_mat) + d(lo, ones_mat)


def _kvf_fwd(h, g, wk, wv, wf, bf, k_ones, v_ones, spread, name):
    n, d = h.shape
    tm = _tile(n, 512)

    def body(h_ref, g_ref, wk_ref, wv_ref, wf_ref, bf_ref, ko_ref, vo_ref, sp_ref,
             k_ref, v_ref, fl_ref, cum_ref, rep_ref, carry):
        i = pl.program_id(0)

        @pl.when(i == 0)
        def _():
            carry[...] = jnp.zeros_like(carry)

        hb = _rms_fwd(h_ref[...], g_ref[...])[0].astype(BF16)
        k_ref[...] = (jnp.dot(hb, wk_ref[...], preferred_element_type=F32) + ko_ref[...]).astype(BF16)
        v_ref[...] = (jnp.dot(hb, wv_ref[...], preferred_element_type=F32) + vo_ref[...]).astype(BF16)
        fl = jnp.dot(hb, wf_ref[...], preferred_element_type=F32) + bf_ref[...]
        fl_ref[...] = fl
        logf = jnp.minimum(fl, 0.0) - jnp.log(1.0 + jnp.exp(-jnp.abs(fl)))
        rows = lax.broadcasted_iota(jnp.int32, (tm, tm), 0)
        cols = lax.broadcasted_iota(jnp.int32, (tm, tm), 1)
        lower = (rows >= cols).astype(BF16)
        cum = _exact_dot(lower, logf) + carry[0:1, :]
        cum_ref[...] = cum
        rep_ref[...] = _exact_dot_rhs(cum * LOG2E, sp_ref[...])
        carry[...] = jnp.broadcast_to(cum[tm - 1:tm, :], carry.shape)

    return _rows_call(body, name, tm, [h], [g, wk, wv, wf, bf, k_ones, v_ones, spread],
                      [(wk.shape[1], BF16), (wv.shape[1], BF16), (LANES, F32), (LANES, F32), (spread.shape[1], F32)],
                      scratch=[pltpu.VMEM((SUBLANES, LANES), F32)])


def _kvf_bwd(dh, h, dk1, dk2, dv1, dv2, dcum, fl, g, wk, wv, wf, name):
    n, d = h.shape
    tm = _tile(n, 512)
    nb = n // tm

    def body(dh_ref, h_ref, dk1_ref, dk2_ref, dv1_ref, dv2_ref, dc_ref, fl_ref, g_ref, wk_ref, wv_ref, wf_ref,
             o_ref, hk_ref, dk_ref, dv_ref, df_ref, dg_ref, db_ref, acc, bacc, carry):
        i = pl.program_id(0)

        @pl.when(i == 0)
        def _():
            acc[...] = jnp.zeros_like(acc)
            bacc[...] = jnp.zeros_like(bacc)
            carry[...] = jnp.zeros_like(carry)

        rows = lax.broadcasted_iota(jnp.int32, (tm, tm), 0)
        cols = lax.broadcasted_iota(jnp.int32, (tm, tm), 1)
        upper = (rows <= cols).astype(BF16)
        dlogf = _exact_dot(upper, dc_ref[...]) + carry[0:1, :]
        carry[...] = jnp.broadcast_to(dlogf[0:1, :], carry.shape)
        df = dlogf / (1.0 + jnp.exp(fl_ref[...]))
        dfb = df.astype(BF16)
        df_ref[...] = dfb
        bacc[...] += _row_fold(df)
        dkb = ((dk1_ref[...] + dk2_ref[...]) * LN2).astype(BF16)
        dvb = (dv1_ref[...] + dv2_ref[...]).astype(BF16)
        dk_ref[...] = dkb
        dv_ref[...] = dvb
        gain = g_ref[...]
        hk, hhat, r = _rms_fwd(h_ref[...], gain)
        hk_ref[...] = hk.astype(BF16)
        dhk = lax.dot_general(dkb, wk_ref[...], NT_DIMS, preferred_element_type=F32)
        dhk = dhk + lax.dot_general(dvb, wv_ref[...], NT_DIMS, preferred_element_type=F32)
        dhk = dhk + lax.dot_general(dfb, wf_ref[...], NT_DIMS, preferred_element_type=F32)
        dhi, dgr = _rms_bwd(dhk, hhat, r, gain)
        o_ref[...] = dh_ref[...] + dhi
        acc[...] += _row_fold(dgr)

        @pl.when(i == nb - 1)
        def _():
            dg_ref[...] = jnp.sum(acc[...], axis=0, keepdims=True)
            db_ref[...] = jnp.sum(bacc[...], axis=0, keepdims=True)

    return _rows_call(body, name, tm, [dh, h, dk1, dk2, dv1, dv2, dcum, fl], [g, wk, wv, wf],
                      [(d, F32), (d, BF16), (wk.shape[1], BF16), (wv.shape[1], BF16), (LANES, BF16)],
                      [((1, d), F32), ((1, LANES), F32)],
                      [pltpu.VMEM((SUBLANES, d), F32), pltpu.VMEM((SUBLANES, LANES), F32),
                       pltpu.VMEM((SUBLANES, LANES), F32)], reverse=True)


def _loss_head(h, target, g, name):
    n, d = h.shape
    tm = _tile(n, 512)
    nb = n // tm

    def body(h_ref, t_ref, g_ref, dh_ref, loss_ref, dg_ref, lacc, gacc):
        i = pl.program_id(0)

        @pl.when(i == 0)
        def _():
            lacc[...] = jnp.zeros_like(lacc)
            gacc[...] = jnp.zeros_like(gacc)

        gain = g_ref[...]
        yv, hhat, r = _rms_fwd(h_ref[...], gain)
        e = yv - t_ref[...]
        lacc[...] += _row_fold(e * e)
        dhv, dgr = _rms_bwd(e * (1.0 / d), hhat, r, gain)
        dh_ref[...] = dhv
        gacc[...] += _row_fold(dgr)

        @pl.when(i == nb - 1)
        def _():
            loss_ref[...] = jnp.full((1, LANES), jnp.sum(lacc[...]) * (0.5 / d), F32)
            dg_ref[...] = jnp.sum(gacc[...], axis=0, keepdims=True)

    return _rows_call(body, name, tm, [h, target], [g], [(d, F32)], [((1, LANES), F32), ((1, d), F32)],
                      [pltpu.VMEM((SUBLANES, d), F32), pltpu.VMEM((SUBLANES, d), F32)])


def _matmul_tn(a, b, name, out_dtype=BF16):
    l, m = a.shape
    n = b.shape[1]
    tl = _tile(l, 1024)
    tmm = _tile(m, 512)
    tn = _tile(n, 1024)
    nl = l // tl

    def body(a_ref, b_ref, o_ref, acc):
        k = pl.program_id(2)

        @pl.when(k == 0)
        def _():
            acc[...] = jnp.zeros_like(acc)

        acc[...] += lax.dot_general(a_ref[...].astype(BF16), b_ref[...].astype(BF16), TN_DIMS,
                                    preferred_element_type=F32)

        @pl.when(k == nl - 1)
        def _():
            o_ref[...] = acc[...].astype(out_dtype)

    return pl.pallas_call(
        body, name=name, grid=(m // tmm, n // tn, nl),
        in_specs=[pl.BlockSpec((tl, tmm), lambda i, j, k: (k, i)), pl.BlockSpec((tl, tn), lambda i, j, k: (k, j))],
        out_specs=pl.BlockSpec((tmm, tn), lambda i, j, k: (i, j)),
        out_shape=jax.ShapeDtypeStruct((m, n), out_dtype),
        scratch_shapes=[pltpu.VMEM((tmm, tn), F32)],
        compiler_params=_cp("parallel", "parallel", "arbitrary"),
    )(a, b)


def _scan_fwd(u, bblk, cblk, tabs, dskip, name):
    l, d = u.shape
    nj, gb, n2 = bblk.shape
    n = n2 // 2
    tm = _tile(l, 512)
    nb = l // tm

    def body(u_ref, b_ref, c_ref, t_ref, d_ref, st_ref, y_ref, carry):
        i = pl.program_id(1)

        @pl.when(i == 0)
        def _():
            carry[...] = jnp.zeros_like(carry)

        uf = u_ref[...]
        st_ref[...] = jnp.dot(uf.astype(BF16), b_ref[0], preferred_element_type=F32)

        def step(rb, c):
            cr, ci = c
            rows = pl.ds(pl.multiple_of(rb * SUBLANES, SUBLANES), SUBLANES)
            xr = st_ref[rows, 0:n]
            xi = st_ref[rows, n:n2]
            for lvl, sh in enumerate((1, 2, 4)):
                ar = t_ref[0, 2 * lvl]
                ai = t_ref[0, 2 * lvl + 1]
                sr = pltpu.roll(xr, sh, 0)
                si = pltpu.roll(xi, sh, 0)
                xr, xi = xr + ar * sr - ai * si, xi + ar * si + ai * sr
            lr = t_ref[0, 6]
            li = t_ref[0, 7]
            xr, xi = xr + lr * cr - li * ci, xi + lr * ci + li * cr
            st_ref[rows, 0:n] = xr
            st_ref[rows, n:n2] = xi
            return (jnp.broadcast_to(xr[SUBLANES - 1:SUBLANES, :], (SUBLANES, n)),
                    jnp.broadcast_to(xi[SUBLANES - 1:SUBLANES, :], (SUBLANES, n)))

        cr, ci = lax.fori_loop(0, tm // SUBLANES, step, (carry[:, 0:n], carry[:, n:n2]))
        carry[:, 0:n] = cr
        carry[:, n:n2] = ci
        y_ref[...] = jnp.dot(st_ref[...].astype(BF16), c_ref[0], preferred_element_type=F32) + d_ref[...] * uf

    return pl.pallas_call(
        body, name=name, grid=(nj, nb),
        in_specs=[pl.BlockSpec((tm, gb), lambda j, i: (i, j)),
                  pl.BlockSpec((1, gb, n2), lambda j, i: (j, 0, 0)),
                  pl.BlockSpec((1, n2, gb), lambda j, i: (j, 0, 0)),
                  pl.BlockSpec((1, 8, SUBLANES, n), lambda j, i: (j, 0, 0, 0)),
                  pl.BlockSpec((1, gb), lambda j, i: (0, j))],
        out_specs=[pl.BlockSpec((tm, n2), lambda j, i: (i, j)), pl.BlockSpec((tm, gb), lambda j, i: (i, j))],
        out_shape=[jax.ShapeDtypeStruct((l, nj * n2), F32), jax.ShapeDtypeStruct((l, d), F32)],
        scratch_shapes=[pltpu.VMEM((SUBLANES, n2), F32)],
        compiler_params=_cp("parallel", "arbitrary"),
    )(u, bblk, cblk, tabs, dskip)


def _scan_bwd(dy, u, states, bblk_t, cblk_t, tabs, dskip, name):
    l, d = u.shape
    nj, n2, gb = bblk_t.shape
    n = n2 // 2
    tm = _tile(l, 512)
    nb = l // tm
    nr = tm // SUBLANES

    def body(dy_ref, u_ref, st_ref, prev_ref, bt_ref, ct_ref, t_ref, d_ref,
             du_ref, glam_ref, gd_ref, gb_ref, gc_ref, gx, carry):
        i = pl.program_id(1)
        ib = nb - 1 - i

        @pl.when(i == 0)
        def _():
            carry[...] = jnp.zeros_like(carry)
            glam_ref[...] = jnp.zeros_like(glam_ref)
            gd_ref[...] = jnp.zeros_like(gd_ref)
            gb_ref[...] = jnp.zeros_like(gb_ref)
            gc_ref[...] = jnp.zeros_like(gc_ref)

        dyv = dy_ref[...]
        uv = u_ref[...]
        dyb = dyv.astype(BF16)
        gx[...] = jnp.dot(dyb, ct_ref[0], preferred_element_type=F32)
        last_row = lax.broadcasted_iota(jnp.int32, (SUBLANES, n), 0) == SUBLANES - 1

        def block(rows, xp_r, xp_i, c):
            cr, ci = c
            gr = gx[rows, 0:n]
            gi = gx[rows, n:n2]
            for lvl, sh in enumerate((1, 2, 4)):
                ar = t_ref[0, 2 * lvl]
                ai = t_ref[0, 2 * lvl + 1]
                sr = pltpu.roll(gr, SUBLANES - sh, 0)
                si = pltpu.roll(gi, SUBLANES - sh, 0)
                gr, gi = gr + ar * sr - ai * si, gi + ar * si + ai * sr
            lr = t_ref[0, 6]
            li = t_ref[0, 7]
            gr, gi = gr + lr * cr - li * ci, gi + lr * ci + li * cr
            gx[rows, 0:n] = gr
            gx[rows, n:n2] = gi
            xs_r = pltpu.roll(jnp.where(last_row, xp_r, st_ref[rows, 0:n]), 1, 0)
            xs_i = pltpu.roll(jnp.where(last_row, xp_i, st_ref[rows, n:n2]), 1, 0)
            glam_ref[0, :, 0:n] += gr * xs_r + gi * xs_i
            glam_ref[0, :, n:n2] += gi * xs_r - gr * xs_i
            return (jnp.broadcast_to(gr[0:1, :], (SUBLANES, n)), jnp.broadcast_to(gi[0:1, :], (SUBLANES, n)))

        def step(k, c):
            rb = nr - 1 - k
            rows = pl.ds(pl.multiple_of(rb * SUBLANES, SUBLANES), SUBLANES)
            before = pl.ds(pl.multiple_of(rb * SUBLANES - SUBLANES, SUBLANES), SUBLANES)
            return block(rows, st_ref[before, 0:n], st_ref[before, n:n2], c)

        c = lax.fori_loop(0, nr - 1, step, (carry[:, 0:n], carry[:, n:n2]))
        live = (ib > 0).astype(F32)
        cr, ci = block(pl.ds(0, SUBLANES), prev_ref[:, 0:n] * live, prev_ref[:, n:n2] * live, c)
        carry[:, 0:n] = cr
        carry[:, n:n2] = ci

        gxb = gx[...].astype(BF16)
        du_ref[...] = jnp.dot(gxb, bt_ref[0], preferred_element_type=F32) + d_ref[...] * dyv
        gd_ref[0] += _row_fold(dyv * uv)
        gb_ref[0] += lax.dot_general(uv.astype(BF16), gxb, TN_DIMS, preferred_element_type=F32)
        gc_ref[0] += lax.dot_general(st_ref[...].astype(BF16), dyb, TN_DIMS, preferred_element_type=F32)

    rpb = tm // SUBLANES
    return pl.pallas_call(
        body, name=name, grid=(nj, nb),
        in_specs=[pl.BlockSpec((tm, gb), lambda j, i: (nb - 1 - i, j)),
                  pl.BlockSpec((tm, gb), lambda j, i: (nb - 1 - i, j)),
                  pl.BlockSpec((tm, n2), lambda j, i: (nb - 1 - i, j)),
                  pl.BlockSpec((SUBLANES, n2), lambda j, i: (jnp.maximum((nb - 1 - i) * rpb - 1, 0), j)),
                  pl.BlockSpec((1, n2, gb), lambda j, i: (j, 0, 0)),
                  pl.BlockSpec((1, gb, n2), lambda j, i: (j, 0, 0)),
                  pl.BlockSpec((1, 8, SUBLANES, n), lambda j, i: (j, 0, 0, 0)),
                  pl.BlockSpec((1, gb), lambda j, i: (0, j))],
        out_specs=[pl.BlockSpec((tm, gb), lambda j, i: (nb - 1 - i, j)),
                   pl.BlockSpec((1, SUBLANES, n2), lambda j, i: (j, 0, 0)),
                   pl.BlockSpec((1, SUBLANES, gb), lambda j, i: (j, 0, 0)),
                   pl.BlockSpec((1, gb, n2), lambda j, i: (j, 0, 0)),
                   pl.BlockSpec((1, n2, gb), lambda j, i: (j, 0, 0))],
        out_shape=[jax.ShapeDtypeStruct((l, d), F32),
                   jax.ShapeDtypeStruct((nj, SUBLANES, n2), F32),
                   jax.ShapeDtypeStruct((nj, SUBLANES, gb), F32),
                   jax.ShapeDtypeStruct((nj, gb, n2), F32),
                   jax.ShapeDtypeStruct((nj, n2, gb), F32)],
        scratch_shapes=[pltpu.VMEM((tm, n2), F32), pltpu.VMEM((SUBLANES, n2), F32)],
        compiler_params=_cp("parallel", "arbitrary"),
    )(dy, u, states, states, bblk_t, cblk_t, tabs, dskip)


HEAD_SLOT = 128
PAIR_SLOT = 2 * HEAD_SLOT
ATTN_TILE = 1024
LANE_ROWSUM_P = HEAD_DIM
LANE_COLSUM_DS = HEAD_DIM
LANE_ROWSUM_DS = HEAD_DIM + 1


def _slot_cols(w):
    r, c = w.shape
    nh = c // HEAD_DIM
    return jnp.pad(w.reshape(r, nh, HEAD_DIM), ((0, 0), (0, 0), (0, HEAD_SLOT - HEAD_DIM))).reshape(r, nh * HEAD_SLOT)


def _unslot_cols(w):
    r, c = w.shape
    nh = c // HEAD_SLOT
    return w.reshape(r, nh, HEAD_SLOT)[:, :, :HEAD_DIM].reshape(r, nh * HEAD_DIM)


def _slot_ones(nh, lane):
    return jnp.tile((jnp.arange(HEAD_SLOT) == lane).astype(F32), nh).reshape(1, nh * HEAD_SLOT)


def _causal_tiles(n, by_query):
    if by_query:
        tiles = [(i, j) for i in range(n) for j in range(i + 1)]
    else:
        tiles = [(i, j) for j in range(n) for i in range(j, n)]
    return (jnp.asarray([t[0] for t in tiles], jnp.int32), jnp.asarray([t[1] for t in tiles], jnp.int32))


def _flash_fwd(qx, kx, vx, f2_rows, name):
    l = qx.shape[0]
    npair = qx.shape[1] // PAIR_SLOT
    d = npair * HEAD_PAIR
    tq = _tile(l, ATTN_TILE)
    tk = tq
    i_of, j_of = _causal_tiles(l // tq, by_query=True)

    def body(i_ref, j_ref, q_ref, k_ref, v_ref, f_ref, o_ref, lse_ref, m_sc, acc_sc):
        t = pl.program_id(1)
        i = i_ref[t]
        j = j_ref[t]

        @pl.when(j == 0)
        def _():
            m_sc[...] = jnp.full_like(m_sc, NEG)
            acc_sc[...] = jnp.zeros_like(acc_sc)

        def tile(on_diagonal):
            for hh in range(2):
                hs = slice(hh * HEAD_SLOT, (hh + 1) * HEAD_SLOT)
                s = lax.dot_general(q_ref[:, hs], k_ref[:, hs], NT_DIMS, preferred_element_type=F32)
                s = s - f_ref[0, hh:hh + 1, :]
                if on_diagonal:
                    keep = (lax.broadcasted_iota(jnp.int32, (tq, tk), 0)
                            >= lax.broadcasted_iota(jnp.int32, (tq, tk), 1))
                    s = jnp.where(keep, s, NEG)
                m_prev = m_sc[hh]
                m_new = jnp.maximum(m_prev, jnp.max(s, axis=-1, keepdims=True))
                alpha = jnp.exp2(m_prev - m_new)
                p = jnp.exp2(s - jnp.concatenate([m_new] * (tk // LANES), axis=1)).astype(BF16)
                acc_sc[hh] = alpha * acc_sc[hh] + jnp.dot(p, v_ref[:, hs], preferred_element_type=F32)
                m_sc[hh] = m_new

        @pl.when(j < i)
        def _():
            tile(False)

        @pl.when(j == i)
        def _():
            tile(True)
            outs, lses = [], []
            for hh in range(2):
                acc = acc_sc[hh]
                lsum = acc[:, LANE_ROWSUM_P:LANE_ROWSUM_P + 1]
                outs.append(acc[:, :HEAD_DIM] / lsum)
                lses.append(jnp.broadcast_to(m_sc[hh][:, 0:1] + jnp.log2(lsum), (tq, HEAD_DIM)))
            o_ref[...] = jnp.concatenate(outs, axis=1).astype(BF16)
            lse_ref[...] = jnp.concatenate(lses, axis=1)

    q_map = lambda h, t, i_ref, j_ref: (i_ref[t], h)
    kv_map = lambda h, t, i_ref, j_ref: (j_ref[t], h)
    return pl.pallas_call(
        body, name=name,
        grid_spec=pltpu.PrefetchScalarGridSpec(
            num_scalar_prefetch=2, grid=(npair, i_of.shape[0]),
            in_specs=[pl.BlockSpec((tq, PAIR_SLOT), q_map), pl.BlockSpec((tk, PAIR_SLOT), kv_map),
                      pl.BlockSpec((tk, PAIR_SLOT), kv_map),
                      pl.BlockSpec((1, 2, tk), lambda h, t, i_ref, j_ref: (h, 0, j_ref[t]))],
            out_specs=[pl.BlockSpec((tq, HEAD_PAIR), q_map), pl.BlockSpec((tq, HEAD_PAIR), q_map)],
            scratch_shapes=[pltpu.VMEM((2, tq, LANES), F32), pltpu.VMEM((2, tq, HEAD_SLOT), F32)]),
        out_shape=[jax.ShapeDtypeStruct((l, d), BF16), jax.ShapeDtypeStruct((l, d), F32)],
        compiler_params=_cp("parallel", "arbitrary"),
    )(i_of, j_of, qx, kx, vx, f2_rows)


def _flash_bwd(qx, kx, vx, f2_rep, do, lse_rows, delta_rows, name):
    l = qx.shape[0]
    npair = qx.shape[1] // PAIR_SLOT
    d = npair * HEAD_PAIR
    tq = _tile(l, ATTN_TILE)
    tk = tq
    i_of, j_of = _causal_tiles(l // tq, by_query=False)

    def body(i_ref, j_ref, q_ref, k_ref, v_ref, f_ref, do_ref, lse_ref, dl_ref, dq_ref, dk_ref, dv_ref):
        t = pl.program_id(1)
        i = i_ref[t]
        j = j_ref[t]

        @pl.when(t == 0)
        def _():
            dq_ref[...] = jnp.zeros_like(dq_ref)

        @pl.when(i == j)
        def _():
            dk_ref[...] = jnp.zeros_like(dk_ref)
            dv_ref[...] = jnp.zeros_like(dv_ref)

        def tile(on_diagonal):
            dqs, dks, dvs = [], [], []
            for hh in range(2):
                hs = slice(hh * HEAD_SLOT, (hh + 1) * HEAD_SLOT)
                qh, kh = q_ref[:, hs], k_ref[:, hs]
                vh = v_ref[:, hh * HEAD_SLOT:hh * HEAD_SLOT + HEAD_DIM]
                doh = do_ref[:, hh * HEAD_DIM:(hh + 1) * HEAD_DIM]
                st = lax.dot_general(kh, qh, NT_DIMS, preferred_element_type=F32)
                st = st - jnp.concatenate([f_ref[:, hs]] * (tq // HEAD_SLOT), axis=1)
                pt = jnp.exp2(st - lse_ref[0, hh:hh + 1, :])
                if on_diagonal:
                    keep = (lax.broadcasted_iota(jnp.int32, (tk, tq), 1)
                            >= lax.broadcasted_iota(jnp.int32, (tk, tq), 0))
                    pt = jnp.where(keep, pt, 0.0)
                dpt = lax.dot_general(vh, doh, NT_DIMS, preferred_element_type=F32)
                dsb = (pt * (dpt - dl_ref[0, hh:hh + 1, :])).astype(BF16)
                dvs.append(jnp.dot(pt.astype(BF16), doh, preferred_element_type=F32))
                dks.append(jnp.dot(dsb, qh, preferred_element_type=F32))
                dqs.append(lax.dot_general(dsb, kh, TN_DIMS, preferred_element_type=F32))
            dv_ref[...] += jnp.concatenate(dvs, axis=1)
            dk_ref[...] += jnp.concatenate(dks, axis=1)
            dq_ref[pl.ds(pl.multiple_of(i * tq, tq), tq), :] += jnp.concatenate(dqs, axis=1)

        @pl.when(i > j)
        def _():
            tile(False)

        @pl.when(i == j)
        def _():
            tile(True)

    qmap = lambda h, t, i_ref, j_ref: (i_ref[t], h)
    kmap = lambda h, t, i_ref, j_ref: (j_ref[t], h)
    row_map = lambda h, t, i_ref, j_ref: (h, 0, i_ref[t])
    return pl.pallas_call(
        body, name=name,
        grid_spec=pltpu.PrefetchScalarGridSpec(
            num_scalar_prefetch=2, grid=(npair, i_of.shape[0]),
            in_specs=[pl.BlockSpec((tq, PAIR_SLOT), qmap), pl.BlockSpec((tk, PAIR_SLOT), kmap),
                      pl.BlockSpec((tk, PAIR_SLOT), kmap), pl.BlockSpec((tk, PAIR_SLOT), kmap),
                      pl.BlockSpec((tq, HEAD_PAIR), qmap),
                      pl.BlockSpec((1, 2, tq), row_map), pl.BlockSpec((1, 2, tq), row_map)],
            out_specs=[pl.BlockSpec((l, PAIR_SLOT), lambda h, t, i_ref, j_ref: (0, h)),
                       pl.BlockSpec((tk, PAIR_SLOT), kmap), pl.BlockSpec((tk, HEAD_PAIR), kmap)]),
        out_shape=[jax.ShapeDtypeStruct((l, npair * PAIR_SLOT), F32), jax.ShapeDtypeStruct((l, npair * PAIR_SLOT), F32),
                   jax.ShapeDtypeStruct((l, d), F32)],
        compiler_params=_cp("parallel", "arbitrary"),
    )(i_of, j_of, qx, kx, vx, f2_rep, do, lse_rows, delta_rows)


def _my_place():
    return lax.axis_index("x"), lax.axis_index("y"), lax.axis_index("c")


def _chip_exchange(srcs, out_meta, plan, name):
    n_src, n_out, n_plan = len(srcs), len(out_meta), len(plan)

    def body(*refs):
        src_refs = refs[:n_src]
        out_refs = refs[n_src:n_src + n_out]
        send_sems, recv_sems, local_sems = refs[n_src + n_out:]
        x, y, c = _my_place()
        me = 2 * x + y
        copies = []
        for n, (si, oi, src_view, dst_view) in enumerate(plan):
            local = pltpu.make_async_copy(src_view(src_refs[si], me), dst_view(out_refs[oi], me), local_sems.at[n])
            local.start()
            copies.append(local)
            for k in (1, 2, 3):
                peer = me ^ k
                rc = pltpu.make_async_remote_copy(
                    src_ref=src_view(src_refs[si], peer), dst_ref=dst_view(out_refs[oi], me),
                    send_sem=send_sems.at[n, k - 1], recv_sem=recv_sems.at[n, k - 1],
                    device_id=(peer >> 1, peer & 1, c), device_id_type=MESH)
                rc.start()
                copies.append(rc)
        for cp in copies:
            cp.wait()

    any_spec = pl.BlockSpec(memory_space=pl.ANY)
    return pl.pallas_call(
        body, name=name,
        in_specs=[any_spec] * n_src, out_specs=[any_spec] * n_out,
        out_shape=[jax.ShapeDtypeStruct(shape, dt) for (shape, dt) in out_meta],
        scratch_shapes=[pltpu.SemaphoreType.DMA((n_plan, 3)), pltpu.SemaphoreType.DMA((n_plan, 3)),
                        pltpu.SemaphoreType.DMA((n_plan,))],
    )(*srcs)


def _plan_copies(src_refs, land_refs, plan, send_sems, recv_sems):
    x, y, c = _my_place()
    me = 2 * x + y
    copies = []
    for n, (si, oi, src_view, dst_view) in enumerate(plan):
        for k in (1, 2, 3):
            peer = me ^ k
            copies.append(pltpu.make_async_remote_copy(
                src_ref=src_view(src_refs[si], peer), dst_ref=dst_view(land_refs[oi], me),
                send_sem=send_sems.at[3 * n + k - 1], recv_sem=recv_sems.at[3 * n + k - 1],
                device_id=(peer >> 1, peer & 1, c), device_id_type=MESH))
    return copies


def _hbm(a):
    return pltpu.HBM(a.shape, a.dtype)


def _exchange_start(srcs, lands, plan, name):
    n_src, n_land = len(srcs), len(lands)
    n_buf = n_src + n_land

    def body(*refs):
        send_sems, recv_sems = refs[n_buf], refs[n_buf + 1]
        token = refs[-1]
        for cp in _plan_copies(refs[:n_src], refs[n_src:n_buf], plan, send_sems, recv_sems):
            cp.start()
        token[...] = jnp.zeros_like(token)

    bufs = [pltpu.with_memory_space_constraint(a, pltpu.HBM) for a in (*srcs, *lands)]
    hbm = pl.BlockSpec(memory_space=pltpu.HBM)
    sem = pl.BlockSpec(memory_space=pltpu.SEMAPHORE)
    res = pl.pallas_call(
        body, name=name,
        out_shape=(pltpu.SemaphoreType.DMA((3 * len(plan),)), pltpu.SemaphoreType.DMA((3 * len(plan),)),
                   *[_hbm(a) for a in bufs], jax.ShapeDtypeStruct((SUBLANES, LANES), F32)),
        in_specs=[hbm] * n_buf, out_specs=(sem, sem, *[hbm] * n_buf, pl.BlockSpec(memory_space=pltpu.VMEM)),
        input_output_aliases={n: 2 + n for n in range(n_buf)},
        compiler_params=pltpu.CompilerParams(has_side_effects=pltpu.SideEffectType.DATAFLOW_SIDE_EFFECTING),
    )(*bufs)
    return (res[0], res[1]), list(res[2:2 + n_src]), list(res[2 + n_src:2 + n_buf]), res[-1]


def _exchange_wait(sems, srcs, lands, plan, after, name):
    n_src, n_land = len(srcs), len(lands)
    n_buf = n_src + n_land

    def body(*refs):
        send_sems, recv_sems = refs[n_buf], refs[n_buf + 1]
        for cp in _plan_copies(refs[:n_src], refs[n_src:n_buf], plan, send_sems, recv_sems):
            cp.wait_send()
            cp.wait_recv()

    hbm = pl.BlockSpec(memory_space=pltpu.HBM)
    sem = pl.BlockSpec(memory_space=pltpu.SEMAPHORE)
    res = pl.pallas_call(
        body, name=name, out_shape=tuple(_hbm(a) for a in (*srcs, *lands)),
        in_specs=[hbm] * n_buf + [sem, sem, pl.BlockSpec(memory_space=pl.ANY)], out_specs=tuple([hbm] * n_buf),
        input_output_aliases={n: n for n in range(n_buf)},
        compiler_params=pltpu.CompilerParams(has_side_effects=pltpu.SideEffectType.DATAFLOW_SIDE_EFFECTING),
    )(*srcs, *lands, sems[0], sems[1], after)
    return list(res[n_src:])


def _place_own(srcs, lands, plan, name):
    n_src, n_land = len(srcs), len(lands)

    def body(*refs):
        src_refs = refs[:n_src]
        out_refs = refs[n_src + n_land:n_src + 2 * n_land]
        sems = refs[-1]
        x, y, _ = _my_place()
        me = 2 * x + y
        copies = []
        for n, (si, oi, src_view, dst_view) in enumerate(plan):
            cp = pltpu.make_async_copy(src_view(src_refs[si], me), dst_view(out_refs[oi], me), sems.at[n])
            cp.start()
            copies.append(cp)
        for cp in copies:
            cp.wait()

    any_spec = pl.BlockSpec(memory_space=pl.ANY)
    return pl.pallas_call(
        body, name=name, in_specs=[any_spec] * (n_src + n_land), out_specs=[any_spec] * n_land,
        out_shape=[jax.ShapeDtypeStruct(a.shape, a.dtype) for a in lands],
        input_output_aliases={n_src + j: j for j in range(n_land)},
        scratch_shapes=[pltpu.SemaphoreType.DMA((len(plan),))],
    )(*srcs, *lands)


def _core_exchange(arrays, name):
    n_items = len(arrays)

    def body(*refs):
        srcs = refs[:n_items]
        outs = refs[n_items:2 * n_items]
        send_sems, recv_sems = refs[2 * n_items:]
        x, y, c = _my_place()
        copies = []
        for n in range(n_items):
            rc = pltpu.make_async_remote_copy(
                src_ref=srcs[n], dst_ref=outs[n], send_sem=send_sems.at[n], recv_sem=recv_sems.at[n],
                device_id=(x, y, 1 - c), device_id_type=MESH)
            rc.start()
            copies.append(rc)
        for cp in copies:
            cp.wait()

    any_spec = pl.BlockSpec(memory_space=pl.ANY)
    return pl.pallas_call(
        body, name=name,
        in_specs=[any_spec] * n_items, out_specs=[any_spec] * n_items,
        out_shape=[jax.ShapeDtypeStruct(a.shape, a.dtype) for a in arrays],
        scratch_shapes=[pltpu.SemaphoreType.DMA((n_items,)), pltpu.SemaphoreType.DMA((n_items,))],
    )(*arrays)


def _sum_chips(parts, name):
    _, rows, cols = parts.shape
    tm = _tile(rows, 512)

    def body(p_ref, o_ref):
        acc = p_ref[0].astype(F32)
        for s in range(1, N_CHIPS):
            acc = acc + p_ref[s].astype(F32)
        o_ref[...] = acc

    return pl.pallas_call(
        body, name=name, grid=(rows // tm,),
        in_specs=[pl.BlockSpec((N_CHIPS, tm, cols), lambda i: (0, i, 0))],
        out_specs=pl.BlockSpec((tm, cols), lambda i: (i, 0)),
        out_shape=jax.ShapeDtypeStruct((rows, cols), F32),
        compiler_params=_cp("parallel"),
    )(parts)


def _adamw(ga, gb, w, m, v, name):
    rows, cols = w.shape
    tm = _tile(rows, 512)
    c1 = 1.0 - ADAM_B1 ** ADAM_STEP
    c2 = 1.0 - ADAM_B2 ** ADAM_STEP

    def body(ga_ref, gb_ref, w_ref, m_ref, v_ref, g_ref, d_ref, nm_ref, nv_ref):
        g = ga_ref[...] + gb_ref[...]
        nm = ADAM_B1 * m_ref[...] + (1.0 - ADAM_B1) * g
        nv = ADAM_B2 * v_ref[...] + (1.0 - ADAM_B2) * (g * g)
        g_ref[...] = g
        nm_ref[...] = nm
        nv_ref[...] = nv
        d_ref[...] = -ADAM_LR * ((nm / c1) / (jnp.sqrt(nv / c2) + ADAM_EPS) + ADAM_WD * w_ref[...])

    spec = pl.BlockSpec((tm, cols), lambda i: (i, 0))
    return pl.pallas_call(
        body, name=name, grid=(rows // tm,), in_specs=[spec] * 5, out_specs=[spec] * 4,
        out_shape=[jax.ShapeDtypeStruct((rows, cols), F32)] * 4, compiler_params=_cp("parallel"),
    )(ga, gb, w, m, v)


def _ssm_discretise(log_dt, a_re, a_im, b_re, b_im):
    dt = jnp.exp(log_dt)[:, None]
    mag = jnp.exp(a_re * dt)
    lbr = mag * jnp.cos(a_im * dt)
    lbi = mag * jnp.sin(a_im * dt)
    den = a_re * a_re + a_im * a_im
    nr, ni = lbr - 1.0, lbi
    qr = (nr * a_re + ni * a_im) / den
    qi = (ni * a_re - nr * a_im) / den
    bbr = qr[..., None] * b_re - qi[..., None] * b_im
    bbi = qr[..., None] * b_im + qi[..., None] * b_re
    return lbr, lbi, bbr, bbi


def _cmul(ar, ai, br, bi):
    return ar * br - ai * bi, ar * bi + ai * br


def _scan_tables(lr, li, nj, reverse):
    lr = lr.reshape(nj, 1, -1)
    li = li.reshape(nj, 1, -1)
    if reverse:
        li = -li
    pows = [(lr, li)]
    for _ in range(7):
        pows.append(_cmul(*pows[-1], lr, li))
    r = jnp.arange(SUBLANES).reshape(1, SUBLANES, 1)
    if reverse:
        r = SUBLANES - 1 - r
    out = []
    for k in (1, 2, 4):
        pr, pi = pows[k - 1]
        keep = (r >= k).astype(F32)
        out += [pr * keep, pi * keep]
    shape = (nj, SUBLANES, lr.shape[-1])
    cr = jnp.zeros(shape, F32)
    ci = jnp.zeros(shape, F32)
    for e in range(SUBLANES):
        sel = (r == e).astype(F32)
        cr = cr + sel * pows[e][0]
        ci = ci + sel * pows[e][1]
    out += [cr, ci]
    return jnp.stack(out, axis=1)


def _group_eye(gl):
    return jnp.eye(gl, dtype=F32)


def _block_diag_in(bbr, bbi, nj):
    g, p, c = bbr.shape
    gl = g // nj
    eye = _group_eye(gl)[None, :, None, :, None]

    def one(b):
        t = b.reshape(nj, gl, p, c).transpose(0, 1, 3, 2)[:, :, :, None, :]
        return (t * eye).reshape(nj, gl * c, gl * p)

    return jnp.concatenate([one(bbr), one(bbi)], axis=2)


def _block_diag_in_grad(gmat, nj, p, c):
    gl = gmat.shape[1] // c
    n = gl * p
    eye = _group_eye(gl)[None, :, None, :, None]

    def one(m):
        t = jnp.sum(m.reshape(nj, gl, c, gl, p) * eye, axis=3)
        return t.transpose(0, 1, 3, 2).reshape(nj * gl, p, c)

    return one(gmat[:, :, :n]), one(gmat[:, :, n:])


def _block_diag_out(c_re, c_im, nj):
    g, c, p = c_re.shape
    gl = g // nj
    eye = _group_eye(gl)[None, :, None, :, None]

    def one(m):
        t = m.reshape(nj, gl, c, p).transpose(0, 1, 3, 2)[:, :, :, None, :]
        return (t * eye).reshape(nj, gl * p, gl * c)

    return jnp.concatenate([one(c_re), -one(c_im)], axis=1)


def _block_diag_out_grad(gmat, nj, p, c):
    gl = gmat.shape[2] // c
    n = gl * p
    eye = _group_eye(gl)[None, :, None, :, None]

    def one(m):
        t = jnp.sum(m.reshape(nj, gl, p, gl, c) * eye, axis=3)
        return t.transpose(0, 1, 3, 2).reshape(nj * gl, c, p)

    return one(gmat[:, :n, :]), -one(gmat[:, n:, :])


def _pad_rows(flat, cols):
    per = SUBLANES * cols
    n = flat.shape[0]
    total = -(-n // per) * per
    return jnp.pad(flat, (0, total - n)).reshape(total // cols, cols)


def _pack_small(arrs, cols):
    packed = jnp.concatenate([_pad_rows(a.reshape(-1), cols) for a in arrs], axis=0)
    rows = packed.shape[0]
    return jnp.pad(packed, ((0, -rows % 128), (0, 0)))


def _unpack_small(packed, shapes, cols):
    out = []
    row = 0
    for s in shapes:
        n = math.prod(s)
        rows = -(-n // (SUBLANES * cols)) * SUBLANES
        out.append(packed[row:row + rows].reshape(-1)[:n].reshape(s))
        row += rows
    return out


def kernel(x, mix_norm, mlp_norm, mlp_w1, mlp_w2, ssm_log_dt, ssm_a_re, ssm_a_im, ssm_b_re, ssm_b_im, ssm_c_re, ssm_c_im, ssm_d, ssm_w_glu, kv_norm, w_kvf, b_f, attn_wq, attn_wo, final_norm, loss_target, m_mix_norm, m_mlp_norm, m_mlp_w1, m_mlp_w2, m_ssm_log_dt, m_ssm_a_re, m_ssm_a_im, m_ssm_b_re, m_ssm_b_im, m_ssm_c_re, m_ssm_c_im, m_ssm_d, m_ssm_w_glu, m_kv_norm, m_w_kvf, m_b_f, m_attn_wq, m_attn_wo, m_final_norm, v_mix_norm, v_mlp_norm, v_mlp_w1, v_mlp_w2, v_ssm_log_dt, v_ssm_a_re, v_ssm_a_im, v_ssm_b_re, v_ssm_b_im, v_ssm_c_re, v_ssm_c_im, v_ssm_d, v_ssm_w_glu, v_kv_norm, v_w_kvf, v_b_f, v_attn_wq, v_attn_wo, v_final_norm):
    seq, d = x.shape[1], x.shape[2]
    depth = mix_norm.shape[0]
    n_a = ssm_log_dt.shape[0]
    n_b = depth - n_a
    ff = mlp_w1.shape[2] * N_CHIPS
    n_heads = d // HEAD_DIM
    n_groups = d // SSM_GROUP
    p_state = ssm_a_re.shape[2]
    gb = min(d, 256)
    nj = d // gb
    kvf_cols = w_kvf.shape[1]
    ds4, dq4 = d // N_CHIPS, d // (2 * N_CHIPS)
    chip = 2 * lax.axis_index("x") + lax.axis_index("y")

    def cols_of(width):
        return lambda ref, s: ref.at[:, :, pl.ds(pl.multiple_of(s * width, LANES), width)]

    def rows_of(height):
        return lambda ref, s: ref.at[:, pl.ds(pl.multiple_of(s * height, SUBLANES), height), :]

    whole = lambda ref, s: ref
    slot = lambda ref, s: ref.at[s]
    def cols2(width):
        return lambda ref, s: ref.at[:, pl.ds(pl.multiple_of(s * width, LANES), width)]

    def rows2(height):
        return lambda ref, s: ref.at[pl.ds(pl.multiple_of(s * height, SUBLANES), height), :]

    assert n_a >= 2
    (skip_parts,) = _chip_exchange([ssm_d], [((N_CHIPS, n_a, ds4), F32)], [(0, 0, whole, slot)], "gather_skip")
    skip_all = skip_parts.transpose(1, 0, 2).reshape(n_a, d)
    w1_s, w2_s, glu_s = mlp_w1.astype(BF16), mlp_w2.astype(BF16), ssm_w_glu.astype(BF16)
    src_a = [w1_s[0], w2_s[0], glu_s[0]]
    plan_a = [(0, 0, whole, cols2(d)), (1, 1, whole, rows2(d)), (2, 2, whole, cols2(2 * ds4))]
    land_a = [lax.empty((d, ff), BF16), lax.empty((ff, d), BF16), lax.empty((d, 2 * d), BF16)]
    src_b = [w1_s[1:], w2_s[1:], glu_s[1:], w_kvf.astype(BF16), attn_wq.astype(BF16), attn_wo.astype(BF16)]
    plan_b = [(0, 0, whole, cols_of(d)), (1, 1, whole, rows_of(d)), (2, 2, whole, cols_of(2 * ds4)),
              (3, 3, whole, slot), (4, 4, whole, rows_of(ds4)), (5, 5, whole, rows_of(ds4))]
    land_b = [lax.empty((depth - 1, d, ff), BF16), lax.empty((depth - 1, ff, d), BF16),
              lax.empty((n_a - 1, d, 2 * d), BF16), lax.empty((N_CHIPS, d, kvf_cols), BF16),
              lax.empty((n_b, d, d), BF16), lax.empty((n_b, d, d), BF16)]
    land_a = _place_own(src_a, land_a, plan_a, "gather_own_a")
    land_b = _place_own(src_b, land_b, plan_b, "gather_own_b")
    sems_a, src_a, land_a, token_a = _exchange_start(src_a, land_a, plan_a, "gather_start_a")
    sems_b, src_b, land_b, token_b = _exchange_start(src_b, land_b, plan_b, "gather_start_b")
    started = token_a[0:1, 0:1] + token_b[0:1, 0:1]

    def layer_w1(i):
        return w1_0 if i == 0 else w1_rest[i - 1]

    def layer_w2(i):
        return w2_0 if i == 0 else w2_rest[i - 1]

    def layer_glu(i):
        return glu_0 if i == 0 else glu_rest[i - 1]

    h = x[0]
    target = loss_target[0]

    saved = []
    for i in range(n_a):
        lbr, lbi, bbr, bbi = _ssm_discretise(ssm_log_dt[i], ssm_a_re[i], ssm_a_im[i], ssm_b_re[i], ssm_b_im[i])
        bblk = _block_diag_in(bbr, bbi, nj)
        cblk = _block_diag_out(ssm_c_re[i], ssm_c_im[i], nj)
        rec = dict(h0=h, lam=(lbr, lbi), bblk=bblk, cblk=cblk)
        gain = mix_norm[i:i + 1] + started if i == 0 else mix_norm[i:i + 1]
        u = _norm_fwd(h, gain, f"s5_norm_{i}")
        rec["u"] = u
        dskip = rec["dskip"] = skip_all[i:i + 1]
        states, y = _scan_fwd(u, bblk.astype(BF16), cblk.astype(BF16), _scan_tables(lbr, lbi, nj, False), dskip,
                              f"s5_scan_{i}")
        rec["states"], rec["y"] = states, y
        if i == 0:
            w1_0, w2_0, glu_0 = _exchange_wait(sems_a, src_a, land_a, plan_a, y, "gather_wait_a")
        if i == 1:
            w1_rest, w2_rest, glu_rest, kvf_parts, wq_all, wo_all = _exchange_wait(
                sems_b, src_b, land_b, plan_b, y, "gather_wait_b")
        h, rec["zw"] = _s5_post_fwd(h, y, layer_glu(i), f"s5_glu_{i}")
        rec["h1"] = h
        h, rec["ap"] = _mlp_fwd(h, mlp_norm[i:i + 1], layer_w1(i), layer_w2(i), f"mlp_{i}")
        saved.append(rec)
    h_kv = h
    kvf_all = jnp.concatenate([kvf_parts[s] for s in range(N_CHIPS)], axis=1)
    wk = kvf_all[:, :d]
    wv = kvf_all[:, d:2 * d]
    wf = jnp.pad(kvf_all[:, 2 * d:], ((0, 0), (0, LANES - n_heads)))
    bf_row = jnp.pad(b_f, (0, LANES - n_heads)).reshape(1, LANES)
    wk_x = _slot_cols(wk)
    spread = (jnp.arange(LANES)[:, None] == jnp.arange(n_heads * HEAD_SLOT)[None, :] // HEAD_SLOT).astype(BF16)
    kx, vx, flog, cum, f2_rep = _kvf_fwd(h, kv_norm.reshape(1, d), wk_x, _slot_cols(wv), wf, bf_row,
                                         _slot_ones(n_heads, LANE_ROWSUM_DS), _slot_ones(n_heads, LANE_ROWSUM_P),
                                         spread, "kvf")
    f2_rows = (cum[:, :n_heads] * LOG2E).T.reshape(n_heads // 2, 2, seq)
    wq_x = [_slot_cols(wq_all[jb]) for jb in range(n_b)]
    for jb in range(n_b):
        i = n_a + jb
        rec = dict(h0=h)
        qx = _q_fwd(h, mix_norm[i:i + 1], wq_x[jb], _slot_ones(n_heads, LANE_COLSUM_DS), f"attn_q_{jb}")
        o, lse = _flash_fwd(qx, kx, vx, f2_rows, f"attn_core_{jb}")
        rec["qx"], rec["o"], rec["lse"] = qx, o, lse
        h = _o_fwd(h, o, wo_all[jb], f"attn_out_{jb}")
        rec["h1"] = h
        h, rec["ap"] = _mlp_fwd(h, mlp_norm[i:i + 1], layer_w1(i), layer_w2(i), f"mlp_{i}")
        saved.append(rec)
    dh, loss_row, g_final = _loss_head(h, target, final_norm.reshape(1, d), "loss_head")
    loss = lax.psum(loss_row[0, 0], ("x", "y", "c"))

    head_ones = (jnp.arange(d)[:, None] // HEAD_DIM == jnp.arange(d)[None, :] // HEAD_DIM).astype(BF16)
    g_mix = [None] * depth
    g_mlp = [None] * depth
    g_w1 = [None] * depth
    g_w2 = [None] * depth
    g_wq = [None] * n_b
    g_wo = [None] * n_b
    g_glu = [None] * n_a
    g_ssm = [None] * n_a
    dk_parts, dv_parts, df_parts = [], [], []

    def head_rows(rep):
        return rep[:, ::HEAD_DIM].T.reshape(n_heads // 2, 2, seq)

    def mlp_back(dh, i, rec):
        dh_in, hm, a, dap, g_mlp[i] = _mlp_bwd(dh, rec["h1"], rec["ap"], mlp_norm[i:i + 1], layer_w1(i),
                                               layer_w2(i), f"mlp_bwd_{i}")
        g_w2[i] = _matmul_tn(a, dh, f"mlp_dw2_{i}")
        g_w1[i] = _matmul_tn(hm, dap, f"mlp_dw1_{i}")
        return dh_in

    for jb in reversed(range(n_b)):
        i = n_a + jb
        rec = saved[i]
        dh = mlp_back(dh, i, rec)
        do, delta = _o_bwd(dh, rec["o"], wo_all[jb], head_ones, f"attn_out_bwd_{jb}")
        g_wo[jb] = _matmul_tn(rec["o"], dh, f"attn_dwo_{jb}")
        dqx, dkx, dv = _flash_bwd(rec["qx"], kx, vx, f2_rep, do, head_rows(rec["lse"]), head_rows(delta),
                                  f"attn_core_bwd_{jb}")
        dk_parts.append(dkx)
        dv_parts.append(dv)
        df_parts.append(dqx[:, LANE_ROWSUM_DS::HEAD_SLOT] - dkx[:, LANE_COLSUM_DS::HEAD_SLOT])
        dh, hn, dqs, g_mix[i] = _q_bwd(dh, rec["h0"], dqx, mix_norm[i:i + 1], wq_x[jb], f"attn_q_bwd_{jb}")
        g_wq[jb] = _unslot_cols(_matmul_tn(hn, dqs, f"attn_dwq_{jb}"))

    dft = df_parts[0]
    for extra in df_parts[1:]:
        dft = dft + extra
    dcum = jnp.pad(dft, ((0, 0), (0, LANES - n_heads)))
    dh, hk, dkb, dvb, dfb, g_kvn, g_bf = _kvf_bwd(dh, h_kv, dk_parts[0], dk_parts[1], dv_parts[0], dv_parts[1],
                                                  dcum, flog, kv_norm.reshape(1, d), wk_x, wv, wf, "kvf_bwd")
    g_kvf = jnp.concatenate([_unslot_cols(_matmul_tn(hk, dkb, "kvf_dwk")), _matmul_tn(hk, dvb, "kvf_dwv"),
                             _matmul_tn(hk, dfb, "kvf_dwf")[:, :n_heads]], axis=1)

    for i in reversed(range(n_a)):
        rec = saved[i]
        dh = mlp_back(dh, i, rec)
        dy, z, dzw = _s5_post_bwd(dh, rec["y"], rec["zw"], layer_glu(i), f"s5_glu_bwd_{i}")
        g_glu[i] = _matmul_tn(z, dzw, f"s5_dwglu_{i}")
        lbr, lbi = rec["lam"]
        bblk_t = rec["bblk"].transpose(0, 2, 1).astype(BF16)
        cblk_t = rec["cblk"].transpose(0, 2, 1).astype(BF16)
        du, glam8, gd8, gbblk, gcblk = _scan_bwd(dy, rec["u"], rec["states"], bblk_t, cblk_t,
                                                 _scan_tables(lbr, lbi, nj, True), rec["dskip"], f"s5_scan_bwd_{i}")
        dh, g_mix[i] = _norm_bwd_add(dh, du, rec["h0"], mix_norm[i:i + 1], f"s5_norm_bwd_{i}")
        glam = jnp.sum(glam8, axis=1)
        n_st = glam.shape[1] // 2
        g_lbr = glam[:, :n_st].reshape(n_groups, p_state)
        g_lbi = glam[:, n_st:].reshape(n_groups, p_state)
        g_bbr, g_bbi = _block_diag_in_grad(gbblk, nj, p_state, SSM_GROUP)
        g_cre, g_cim = _block_diag_out_grad(gcblk, nj, p_state, SSM_GROUP)
        _, pull = jax.vjp(_ssm_discretise, ssm_log_dt[i], ssm_a_re[i], ssm_a_im[i], ssm_b_re[i], ssm_b_im[i])
        g_ldt, g_are, g_aim, g_bre, g_bim = pull((g_lbr, g_lbi, g_bbr, g_bbi))
        g_ssm[i] = dict(log_dt=g_ldt, a_re=g_are, a_im=g_aim, b_re=g_bre, b_im=g_bim, c_re=g_cre, c_im=g_cim,
                        d=jnp.sum(gd8, axis=1).reshape(d))
    grad_x = dh[None]

    def stack_small(key):
        return jnp.stack([g_ssm[i][key] for i in range(n_a)])

    small_grads = [jnp.concatenate(g_mix, axis=0), jnp.concatenate(g_mlp, axis=0), stack_small("log_dt"),
                   stack_small("a_re"), stack_small("a_im"), stack_small("b_re"), stack_small("b_im"),
                   stack_small("c_re"), stack_small("c_im"), stack_small("d"), g_kvn.reshape(d),
                   g_bf[0, :n_heads], g_final.reshape(d)]
    small_w = [mix_norm, mlp_norm, ssm_log_dt, ssm_a_re, ssm_a_im, ssm_b_re, ssm_b_im, ssm_c_re, ssm_c_im,
               ssm_d, kv_norm, b_f, final_norm]
    small_m = [m_mix_norm, m_mlp_norm, m_ssm_log_dt, m_ssm_a_re, m_ssm_a_im, m_ssm_b_re, m_ssm_b_im, m_ssm_c_re,
               m_ssm_c_im, m_ssm_d, m_kv_norm, m_b_f, m_final_norm]
    small_v = [v_mix_norm, v_mlp_norm, v_ssm_log_dt, v_ssm_a_re, v_ssm_a_im, v_ssm_b_re, v_ssm_b_im, v_ssm_c_re,
               v_ssm_c_im, v_ssm_d, v_kv_norm, v_b_f, v_final_norm]
    skip_at = 9

    def widen_skip(part):
        return lax.dynamic_update_slice(jnp.zeros((n_a, d), F32), part, (0, chip * ds4))

    small_shapes = [a.shape for a in small_grads]
    pcols = 1024 if d >= 1024 else LANES
    g_small = _pack_small(small_grads, pcols)
    expand = lambda lst: _pack_small([widen_skip(a) if n == skip_at else a for n, a in enumerate(lst)], pcols)
    w_small, m_small, v_small = expand(small_w), expand(small_m), expand(small_v)
    srows = g_small.shape[0]

    kvf_send = g_kvf.reshape(d, N_CHIPS, kvf_cols).transpose(1, 0, 2)

    def cols2(width):
        return lambda ref, s: ref.at[:, pl.ds(pl.multiple_of(s * width, LANES), width)]

    def rows2(height):
        return lambda ref, s: ref.at[pl.ds(pl.multiple_of(s * height, SUBLANES), height), :]

    def into(layer):
        return lambda ref, s: ref.at[s, layer]

    assert n_b == 2
    red_srcs, red_plan = [], []

    def send(arr, out_index, src_view, dst_view):
        red_plan.append((len(red_srcs), out_index, src_view, dst_view))
        red_srcs.append(arr)

    for i in range(depth):
        send(g_w1[i], 0, cols2(d), into(i))
    for i in range(depth):
        send(g_w2[i], 1, rows2(d), into(i))
    for i in range(n_a):
        send(g_glu[i], 2, cols2(2 * ds4), into(i))
    send(kvf_send, 3, slot, slot)
    for jb in range(n_b):
        send(g_wq[jb], 4, rows2(ds4), into(jb))
    for jb in range(n_b):
        send(g_wo[jb], 5, rows2(ds4), into(jb))
    send(g_small, 6, whole, slot)
    received = _chip_exchange(
        red_srcs,
        [((N_CHIPS, depth, d, d), BF16), ((N_CHIPS, depth, d, d), BF16), ((N_CHIPS, n_a, d, 2 * ds4), BF16),
         ((N_CHIPS, d, kvf_cols), BF16), ((N_CHIPS, n_b, ds4, d), BF16), ((N_CHIPS, n_b, ds4, d), BF16),
         ((N_CHIPS, srows, pcols), F32)],
        red_plan, "reduce_chips")
    r_w1, r_w2, r_glu, r_kvf, r_wq, r_wo, r_small = received

    def flat(a):
        return a.reshape(N_CHIPS, -1, a.shape[-1])

    sums = [_sum_chips(flat(r), f"sum_chips_{n}") for n, r in
            enumerate((r_w1, r_w2, r_glu, r_kvf, r_wq, r_wo, r_small))]
    others = _core_exchange(sums, "reduce_cores")

    def two(a):
        return a.reshape(-1, a.shape[-1])

    big_w = [(mlp_w1, m_mlp_w1, v_mlp_w1), (mlp_w2, m_mlp_w2, v_mlp_w2), (ssm_w_glu, m_ssm_w_glu, v_ssm_w_glu),
             (w_kvf, m_w_kvf, v_w_kvf), (attn_wq, m_attn_wq, v_attn_wq), (attn_wo, m_attn_wo, v_attn_wo)]
    big_out = []
    for n, (w, m, v) in enumerate(big_w):
        res = _adamw(sums[n], others[n], two(w), two(m), two(v), f"adamw_{n}")
        big_out.append([r.reshape(w.shape) for r in res])
    small_out = _adamw(sums[6], others[6], w_small, m_small, v_small, "adamw_small")

    def narrow_skip(a):
        return lax.dynamic_slice(a, (0, chip * ds4), (n_a, ds4))

    unpacked = []
    for packed in small_out:
        parts = _unpack_small(packed, small_shapes, pcols)
        parts[skip_at] = narrow_skip(parts[skip_at])
        unpacked.append(parts)

    order = ["mix_norm", "mlp_norm", "mlp_w1", "mlp_w2", "ssm_log_dt", "ssm_a_re", "ssm_a_im", "ssm_b_re",
             "ssm_b_im", "ssm_c_re", "ssm_c_im", "ssm_d", "ssm_w_glu", "kv_norm", "w_kvf", "b_f", "attn_wq",
             "attn_wo", "final_norm"]
    small_names = ["mix_norm", "mlp_norm", "ssm_log_dt", "ssm_a_re", "ssm_a_im", "ssm_b_re", "ssm_b_im",
                   "ssm_c_re", "ssm_c_im", "ssm_d", "kv_norm", "b_f", "final_norm"]
    big_names = ["mlp_w1", "mlp_w2", "ssm_w_glu", "w_kvf", "attn_wq", "attn_wo"]
    outs = [loss, grad_x]
    for kind in range(4):
        for name in order:
            if name in big_names:
                outs.append(big_out[big_names.index(name)][kind])
            else:
                outs.append(unpacked[kind][small_names.index(name)])
    return tuple(outs)
```

```python
import functools
import math

import jax
import jax.numpy as jnp
from jax import lax
from jax.experimental import pallas as pl
from jax.experimental.pallas import tpu as pltpu

F32 = jnp.float32
BF16 = jnp.bfloat16

RMS_EPS = 1e-6
SSM_GROUP = 16
SSM_STATE = 64
HEAD_DIM = 64
HEAD_PAIR = 2 * HEAD_DIM
LANES = 128
SUBLANES = 8
N_CHIPS = 4
ADAM_LR = 0.001
ADAM_B1 = 0.9
ADAM_B2 = 0.999
ADAM_EPS = 1e-08
ADAM_WD = 0.01
ADAM_STEP = 10
GELU_C = math.sqrt(2.0 / math.pi)
GELU_A = 0.044715
NEG = -1e30
LN2 = math.log(2.0)
LOG2E = 1.0 / LN2
VMEM_LIMIT = 56 * 1024 * 1024
MESH = pl.DeviceIdType.MESH

NT_DIMS = (((1,), (1,)), ((), ()))
TN_DIMS = (((0,), (0,)), ((), ()))


def _cp(*sem):
    return pltpu.CompilerParams(dimension_semantics=sem if sem else None, vmem_limit_bytes=VMEM_LIMIT)


def _zero_idx(nd, *_):
    return (0,) * nd


def _tile(n, t):
    if n <= t:
        return n
    for cand in range(t - t % SUBLANES, 0, -SUBLANES):
        if n % cand == 0:
            return cand
    raise ValueError((n, t))


def _rms_fwd(h, g):
    r = lax.rsqrt(jnp.mean(h * h, axis=-1, keepdims=True) + RMS_EPS)
    hhat = h * r
    return hhat * g, hhat, r


def _rms_bwd(du, hhat, r, g):
    dhh = du * g
    dh = r * (dhh - hhat * jnp.mean(dhh * hhat, axis=-1, keepdims=True))
    return dh, du * hhat


def _sigmoid(x):
    return 1.0 / (1.0 + jnp.exp(-x))


def _gelu(x):
    t = jnp.tanh(GELU_C * (x + GELU_A * x * x * x))
    return 0.5 * x * (1.0 + t)


def _gelu_grad(x):
    t = jnp.tanh(GELU_C * (x + GELU_A * x * x * x))
    return 0.5 * (1.0 + t) + 0.5 * x * (1.0 - t * t) * GELU_C * (1.0 + 3.0 * GELU_A * x * x)


def _row_fold(x):
    tm, w = x.shape
    return jnp.sum(x.reshape(tm // SUBLANES, SUBLANES, w), axis=0)


def _split3(x):
    hi = x.astype(BF16)
    r1 = x - hi.astype(F32)
    mid = r1.astype(BF16)
    lo = (r1 - mid.astype(F32)).astype(BF16)
    return hi, mid, lo


def _exact_dot(ones_mat, x):
    hi, mid, lo = _split3(x)
    d = functools.partial(jnp.dot, preferred_element_type=F32)
    return d(ones_mat, hi) + d(ones_mat, mid) + d(ones_mat, lo)


def _rows_call(body, name, tm, row_ins, const_ins, row_outs, acc_outs=(), scratch=(), reverse=False):
    n = row_ins[0].shape[0]
    nb = n // tm
    if reverse:
        ridx = lambda i: (nb - 1 - i, 0)
    else:
        ridx = lambda i: (i, 0)
    in_specs = [pl.BlockSpec((tm, a.shape[1]), ridx) for a in row_ins]
    in_specs += [pl.BlockSpec(a.shape, functools.partial(_zero_idx, a.ndim), pipeline_mode=pl.Buffered(1))
                 for a in const_ins]
    out_shape = [jax.ShapeDtypeStruct((n, w), dt) for (w, dt) in row_outs]
    out_shape += [jax.ShapeDtypeStruct(s, dt) for (s, dt) in acc_outs]
    out_specs = [pl.BlockSpec((tm, w), ridx) for (w, dt) in row_outs]
    out_specs += [pl.BlockSpec(s, functools.partial(_zero_idx, len(s))) for (s, dt) in acc_outs]
    return pl.pallas_call(
        body, name=name, grid=(nb,), in_specs=in_specs, out_specs=out_specs, out_shape=out_shape,
        scratch_shapes=list(scratch), compiler_params=_cp("arbitrary"),
    )(*row_ins, *const_ins)


def _norm_fwd(h, g, name):
    n, d = h.shape
    tm = _tile(n, 512)

    def body(h_ref, g_ref, u_ref):
        u_ref[...] = _rms_fwd(h_ref[...], g_ref[...])[0]

    return _rows_call(body, name, tm, [h], [g], [(d, F32)])[0]


def _norm_bwd_add(dh, du, h, g, name):
    n, d = h.shape
    tm = _tile(n, 512)
    nb = n // tm

    def body(dh_ref, du_ref, h_ref, g_ref, o_ref, dg_ref, acc):
        i = pl.program_id(0)

        @pl.when(i == 0)
        def _():
            acc[...] = jnp.zeros_like(acc)

        gain = g_ref[...]
        _, hhat, r = _rms_fwd(h_ref[...], gain)
        dhn, dgr = _rms_bwd(du_ref[...], hhat, r, gain)
        o_ref[...] = dh_ref[...] + dhn
        acc[...] += _row_fold(dgr)

        @pl.when(i == nb - 1)
        def _():
            dg_ref[...] = jnp.sum(acc[...], axis=0, keepdims=True)

    return _rows_call(body, name, tm, [dh, du, h], [g], [(d, F32)], [((1, d), F32)],
                      [pltpu.VMEM((SUBLANES, d), F32)])


def _mlp_fwd(h, g, w1, w2, name):
    n, d = h.shape
    ff = w1.shape[1]
    tm = _tile(n, 256)
    fc = _tile(ff, 1024)

    def body(h_ref, g_ref, w1_ref, w2_ref, o_ref, ap_ref):
        hin = h_ref[...]
        hb = _rms_fwd(hin, g_ref[...])[0].astype(BF16)
        acc = hin
        for c in range(ff // fc):
            cs = slice(c * fc, (c + 1) * fc)
            ap = jnp.dot(hb, w1_ref[:, cs], preferred_element_type=F32)
            ap_ref[:, cs] = ap.astype(BF16)
            rl = jnp.maximum(ap, 0.0)
            acc = acc + jnp.dot((rl * rl).astype(BF16), w2_ref[cs, :], preferred_element_type=F32)
        o_ref[...] = acc

    return _rows_call(body, name, tm, [h], [g, w1, w2], [(d, F32), (ff, BF16)])


def _mlp_bwd(dh, h, ap, g, w1, w2, name):
    n, d = h.shape
    ff = w1.shape[1]
    tm = _tile(n, 256)
    nb = n // tm
    fc = _tile(ff, 1024)

    def body(dh_ref, h_ref, ap_ref, g_ref, w1_ref, w2_ref, o_ref, hm_ref, a_ref, dap_ref, dg_ref, acc):
        i = pl.program_id(0)

        @pl.when(i == 0)
        def _():
            acc[...] = jnp.zeros_like(acc)

        gain = g_ref[...]
        dhv = dh_ref[...]
        hm, hhat, r = _rms_fwd(h_ref[...], gain)
        hm_ref[...] = hm.astype(BF16)
        dhb = dhv.astype(BF16)
        dhm = jnp.zeros((tm, d), F32)
        for c in range(ff // fc):
            cs = slice(c * fc, (c + 1) * fc)
            rl = jnp.maximum(ap_ref[:, cs].astype(F32), 0.0)
            a_ref[:, cs] = (rl * rl).astype(BF16)
            da = lax.dot_general(dhb, w2_ref[cs, :], NT_DIMS, preferred_element_type=F32)
            dap = (da * (2.0 * rl)).astype(BF16)
            dap_ref[:, cs] = dap
            dhm = dhm + lax.dot_general(dap, w1_ref[:, cs], NT_DIMS, preferred_element_type=F32)
        dhn, dgr = _rms_bwd(dhm, hhat, r, gain)
        o_ref[...] = dhv + dhn
        acc[...] += _row_fold(dgr)

        @pl.when(i == nb - 1)
        def _():
            dg_ref[...] = jnp.sum(acc[...], axis=0, keepdims=True)

    return _rows_call(body, name, tm, [dh, h, ap], [g, w1, w2],
                      [(d, F32), (d, BF16), (ff, BF16), (ff, BF16)], [((1, d), F32)],
                      [pltpu.VMEM((SUBLANES, d), F32)])


def _s5_post_fwd(h, y, w_glu, name):
    n, d = h.shape
    tm = _tile(n, 512)

    def body(h_ref, y_ref, w_ref, o_ref, zw_ref):
        z = _gelu(y_ref[...]).astype(BF16)
        zw = jnp.dot(z, w_ref[...], preferred_element_type=F32)
        zw_ref[...] = zw.astype(BF16)
        o_ref[...] = h_ref[...] + zw[:, :d] * _sigmoid(zw[:, d:])

    return _rows_call(body, name, tm, [h, y], [w_glu], [(d, F32), (2 * d, BF16)])


def _s5_post_bwd(dh, y, zw, w_glu, name):
    n, d = dh.shape
    tm = _tile(n, 512)

    def body(dh_ref, y_ref, zw_ref, w_ref, dy_ref, z_ref, dzw_ref):
        dhv = dh_ref[...]
        yv = y_ref[...]
        val = zw_ref[:, :d].astype(F32)
        sg = _sigmoid(zw_ref[:, d:].astype(F32))
        dzw = jnp.concatenate([dhv * sg, dhv * val * sg * (1.0 - sg)], axis=1).astype(BF16)
        dzw_ref[...] = dzw
        dz = lax.dot_general(dzw, w_ref[...], NT_DIMS, preferred_element_type=F32)
        dy_ref[...] = dz * _gelu_grad(yv)
        z_ref[...] = _gelu(yv).astype(BF16)

    return _rows_call(body, name, tm, [dh, y, zw], [w_glu], [(d, F32), (d, BF16), (2 * d, BF16)])


def _q_fwd(h, g, wq_x, ones_x, name):
    n, d = h.shape
    tm = _tile(n, 512)
    scale = LOG2E * HEAD_DIM ** -0.5

    def body(h_ref, g_ref, w_ref, one_ref, q_ref):
        hb = _rms_fwd(h_ref[...], g_ref[...])[0].astype(BF16)
        q_ref[...] = (jnp.dot(hb, w_ref[...], preferred_element_type=F32) * scale + one_ref[...]).astype(BF16)

    return _rows_call(body, name, tm, [h], [g, wq_x, ones_x], [(wq_x.shape[1], BF16)])[0]


def _q_bwd(dh, h, dq, g, wq, name):
    n, d = h.shape
    tm = _tile(n, 512)
    nb = n // tm
    scale = HEAD_DIM ** -0.5

    def body(dh_ref, h_ref, dq_ref, g_ref, w_ref, o_ref, hn_ref, dqs_ref, dg_ref, acc):
        i = pl.program_id(0)

        @pl.when(i == 0)
        def _():
            acc[...] = jnp.zeros_like(acc)

        gain = g_ref[...]
        hn, hhat, r = _rms_fwd(h_ref[...], gain)
        hn_ref[...] = hn.astype(BF16)
        dqs = (dq_ref[...] * scale).astype(BF16)
        dqs_ref[...] = dqs
        dhn = lax.dot_general(dqs, w_ref[...], NT_DIMS, preferred_element_type=F32)
        dhi, dgr = _rms_bwd(dhn, hhat, r, gain)
        o_ref[...] = dh_ref[...] + dhi
        acc[...] += _row_fold(dgr)

        @pl.when(i == nb - 1)
        def _():
            dg_ref[...] = jnp.sum(acc[...], axis=0, keepdims=True)

    return _rows_call(body, name, tm, [dh, h, dq], [g, wq], [(d, F32), (d, BF16), (wq.shape[1], BF16)],
                      [((1, d), F32)], [pltpu.VMEM((SUBLANES, d), F32)])


def _o_fwd(h, o, wo, name):
    n, d = h.shape
    tm = _tile(n, 512)

    def body(h_ref, o_ref, w_ref, out_ref):
        out_ref[...] = h_ref[...] + jnp.dot(o_ref[...], w_ref[...], preferred_element_type=F32)

    return _rows_call(body, name, tm, [h, o], [wo], [(d, F32)])[0]


def _o_bwd(dh, o, wo, head_ones, name):
    n, d = dh.shape
    tm = _tile(n, 512)

    def body(dh_ref, o_ref, w_ref, e_ref, do_ref, dl_ref):
        do = lax.dot_general(dh_ref[...].astype(BF16), w_ref[...], NT_DIMS, preferred_element_type=F32).astype(BF16)
        do_ref[...] = do
        dl_ref[...] = _exact_dot_rhs(do.astype(F32) * o_ref[...].astype(F32), e_ref[...])

    return _rows_call(body, name, tm, [dh, o], [wo, head_ones], [(wo.shape[0], BF16), (wo.shape[0], F32)])


def _exact_dot_rhs(x, ones_mat):
    hi, mid, lo = _split3(x)
    d = functools.partial(jnp.dot, preferred_element_type=F32)
    return d(hi, ones_mat) + d(mid, ones_mat) + d(lo, ones_mat)


def _kvf_fwd(h, g, wk, wv, wf, bf, k_ones, v_ones, spread, name):
    n, d = h.shape
    tm = _tile(n, 512)

    def body(h_ref, g_ref, wk_ref, wv_ref, wf_ref, bf_ref, ko_ref, vo_ref, sp_ref,
             k_ref, v_ref, fl_ref, cum_ref, rep_ref, carry):
        i = pl.program_id(0)

        @pl.when(i == 0)
        def _():
            carry[...] = jnp.zeros_like(carry)

        hb = _rms_fwd(h_ref[...], g_ref[...])[0].astype(BF16)
        k_ref[...] = (jnp.dot(hb, wk_ref[...], preferred_element_type=F32) + ko_ref[...]).astype(BF16)
        v_ref[...] = (jnp.dot(hb, wv_ref[...], preferred_element_type=F32) + vo_ref[...]).astype(BF16)
        fl = jnp.dot(hb, wf_ref[...], preferred_element_type=F32) + bf_ref[...]
        fl_ref[...] = fl
        logf = jnp.minimum(fl, 0.0) - jnp.log(1.0 + jnp.exp(-jnp.abs(fl)))
        rows = lax.broadcasted_iota(jnp.int32, (tm, tm), 0)
        cols = lax.broadcasted_iota(jnp.int32, (tm, tm), 1)
        lower = (rows >= cols).astype(BF16)
        cum = _exact_dot(lower, logf) + carry[0:1, :]
        cum_ref[...] = cum
        rep_ref[...] = _exact_dot_rhs(cum * LOG2E, sp_ref[...])
        carry[...] = jnp.broadcast_to(cum[tm - 1:tm, :], carry.shape)

    return _rows_call(body, name, tm, [h], [g, wk, wv, wf, bf, k_ones, v_ones, spread],
                      [(wk.shape[1], BF16), (wv.shape[1], BF16), (LANES, F32), (LANES, F32), (spread.shape[1], F32)],
                      scratch=[pltpu.VMEM((SUBLANES, LANES), F32)])


def _kvf_bwd(dh, h, dk1, dk2, dv1, dv2, dcum, fl, g, wk, wv, wf, name):
    n, d = h.shape
    tm = _tile(n, 512)
    nb = n // tm

    def body(dh_ref, h_ref, dk1_ref, dk2_ref, dv1_ref, dv2_ref, dc_ref, fl_ref, g_ref, wk_ref, wv_ref, wf_ref,
             o_ref, hk_ref, dk_ref, dv_ref, df_ref, dg_ref, db_ref, acc, bacc, carry):
        i = pl.program_id(0)

        @pl.when(i == 0)
        def _():
            acc[...] = jnp.zeros_like(acc)
            bacc[...] = jnp.zeros_like(bacc)
            carry[...] = jnp.zeros_like(carry)

        rows = lax.broadcasted_iota(jnp.int32, (tm, tm), 0)
        cols = lax.broadcasted_iota(jnp.int32, (tm, tm), 1)
        upper = (rows <= cols).astype(BF16)
        dlogf = _exact_dot(upper, dc_ref[...]) + carry[0:1, :]
        carry[...] = jnp.broadcast_to(dlogf[0:1, :], carry.shape)
        df = dlogf / (1.0 + jnp.exp(fl_ref[...]))
        dfb = df.astype(BF16)
        df_ref[...] = dfb
        bacc[...] += _row_fold(df)
        dkb = ((dk1_ref[...] + dk2_ref[...]) * LN2).astype(BF16)
        dvb = (dv1_ref[...] + dv2_ref[...]).astype(BF16)
        dk_ref[...] = dkb
        dv_ref[...] = dvb
        gain = g_ref[...]
        hk, hhat, r = _rms_fwd(h_ref[...], gain)
        hk_ref[...] = hk.astype(BF16)
        dhk = lax.dot_general(dkb, wk_ref[...], NT_DIMS, preferred_element_type=F32)
        dhk = dhk + lax.dot_general(dvb, wv_ref[...], NT_DIMS, preferred_element_type=F32)
        dhk = dhk + lax.dot_general(dfb, wf_ref[...], NT_DIMS, preferred_element_type=F32)
        dhi, dgr = _rms_bwd(dhk, hhat, r, gain)
        o_ref[...] = dh_ref[...] + dhi
        acc[...] += _row_fold(dgr)

        @pl.when(i == nb - 1)
        def _():
            dg_ref[...] = jnp.sum(acc[...], axis=0, keepdims=True)
            db_ref[...] = jnp.sum(bacc[...], axis=0, keepdims=True)

    return _rows_call(body, name, tm, [dh, h, dk1, dk2, dv1, dv2, dcum, fl], [g, wk, wv, wf],
                      [(d, F32), (d, BF16), (wk.shape[1], BF16), (wv.shape[1], BF16), (LANES, BF16)],
                      [((1, d), F32), ((1, LANES), F32)],
                      [pltpu.VMEM((SUBLANES, d), F32), pltpu.VMEM((SUBLANES, LANES), F32),
                       pltpu.VMEM((SUBLANES, LANES), F32)], reverse=True)


def _loss_head(h, target, g, name):
    n, d = h.shape
    tm = _tile(n, 512)
    nb = n // tm

    def body(h_ref, t_ref, g_ref, dh_ref, loss_ref, dg_ref, lacc, gacc):
        i = pl.program_id(0)

        @pl.when(i == 0)
        def _():
            lacc[...] = jnp.zeros_like(lacc)
            gacc[...] = jnp.zeros_like(gacc)

        gain = g_ref[...]
        yv, hhat, r = _rms_fwd(h_ref[...], gain)
        e = yv - t_ref[...]
        lacc[...] += _row_fold(e * e)
        dhv, dgr = _rms_bwd(e * (1.0 / d), hhat, r, gain)
        dh_ref[...] = dhv
        gacc[...] += _row_fold(dgr)

        @pl.when(i == nb - 1)
        def _():
            loss_ref[...] = jnp.full((1, LANES), jnp.sum(lacc[...]) * (0.5 / d), F32)
            dg_ref[...] = jnp.sum(gacc[...], axis=0, keepdims=True)

    return _rows_call(body, name, tm, [h, target], [g], [(d, F32)], [((1, LANES), F32), ((1, d), F32)],
                      [pltpu.VMEM((SUBLANES, d), F32), pltpu.VMEM((SUBLANES, d), F32)])


def _matmul_tn(a, b, name, out_dtype=BF16):
    l, m = a.shape
    n = b.shape[1]
    tl = _tile(l, 1024)
    tmm = _tile(m, 512)
    tn = _tile(n, 1024)
    nl = l // tl

    def body(a_ref, b_ref, o_ref, acc):
        k = pl.program_id(2)

        @pl.when(k == 0)
        def _():
            acc[...] = jnp.zeros_like(acc)

        acc[...] += lax.dot_general(a_ref[...].astype(BF16), b_ref[...].astype(BF16), TN_DIMS,
                                    preferred_element_type=F32)

        @pl.when(k == nl - 1)
        def _():
            o_ref[...] = acc[...].astype(out_dtype)

    return pl.pallas_call(
        body, name=name, grid=(m // tmm, n // tn, nl),
        in_specs=[pl.BlockSpec((tl, tmm), lambda i, j, k: (k, i)), pl.BlockSpec((tl, tn), lambda i, j, k: (k, j))],
        out_specs=pl.BlockSpec((tmm, tn), lambda i, j, k: (i, j)),
        out_shape=jax.ShapeDtypeStruct((m, n), out_dtype),
        scratch_shapes=[pltpu.VMEM((tmm, tn), F32)],
        compiler_params=_cp("parallel", "parallel", "arbitrary"),
    )(a, b)


def _scan_fwd(u, bblk, cblk, tabs, dskip, name):
    l, d = u.shape
    nj, gb, n2 = bblk.shape
    n = n2 // 2
    tm = _tile(l, 512)
    nb = l // tm

    def body(u_ref, b_ref, c_ref, t_ref, d_ref, st_ref, y_ref, carry):
        i = pl.program_id(1)

        @pl.when(i == 0)
        def _():
            carry[...] = jnp.zeros_like(carry)

        uf = u_ref[...]
        st_ref[...] = jnp.dot(uf.astype(BF16), b_ref[0], preferred_element_type=F32)

        def step(rb, c):
            cr, ci = c
            rows = pl.ds(pl.multiple_of(rb * SUBLANES, SUBLANES), SUBLANES)
            xr = st_ref[rows, 0:n]
            xi = st_ref[rows, n:n2]
            for lvl, sh in enumerate((1, 2, 4)):
                ar = t_ref[0, 2 * lvl]
                ai = t_ref[0, 2 * lvl + 1]
                sr = pltpu.roll(xr, sh, 0)
                si = pltpu.roll(xi, sh, 0)
                xr, xi = xr + ar * sr - ai * si, xi + ar * si + ai * sr
            lr = t_ref[0, 6]
            li = t_ref[0, 7]
            xr, xi = xr + lr * cr - li * ci, xi + lr * ci + li * cr
            st_ref[rows, 0:n] = xr
            st_ref[rows, n:n2] = xi
            return (jnp.broadcast_to(xr[SUBLANES - 1:SUBLANES, :], (SUBLANES, n)),
                    jnp.broadcast_to(xi[SUBLANES - 1:SUBLANES, :], (SUBLANES, n)))

        cr, ci = lax.fori_loop(0, tm // SUBLANES, step, (carry[:, 0:n], carry[:, n:n2]))
        carry[:, 0:n] = cr
        carry[:, n:n2] = ci
        y_ref[...] = jnp.dot(st_ref[...].astype(BF16), c_ref[0], preferred_element_type=F32) + d_ref[...] * uf

    return pl.pallas_call(
        body, name=name, grid=(nj, nb),
        in_specs=[pl.BlockSpec((tm, gb), lambda j, i: (i, j)),
                  pl.BlockSpec((1, gb, n2), lambda j, i: (j, 0, 0)),
                  pl.BlockSpec((1, n2, gb), lambda j, i: (j, 0, 0)),
                  pl.BlockSpec((1, 8, SUBLANES, n), lambda j, i: (j, 0, 0, 0)),
                  pl.BlockSpec((1, gb), lambda j, i: (0, j))],
        out_specs=[pl.BlockSpec((tm, n2), lambda j, i: (i, j)), pl.BlockSpec((tm, gb), lambda j, i: (i, j))],
        out_shape=[jax.ShapeDtypeStruct((l, nj * n2), F32), jax.ShapeDtypeStruct((l, d), F32)],
        scratch_shapes=[pltpu.VMEM((SUBLANES, n2), F32)],
        compiler_params=_cp("parallel", "arbitrary"),
    )(u, bblk, cblk, tabs, dskip)


def _scan_bwd(dy, u, states, bblk_t, cblk_t, tabs, dskip, name):
    l, d = u.shape
    nj, n2, gb = bblk_t.shape
    n = n2 // 2
    tm = _tile(l, 512)
    nb = l // tm
    nr = tm // SUBLANES

    def body(dy_ref, u_ref, st_ref, prev_ref, bt_ref, ct_ref, t_ref, d_ref,
             du_ref, glam_ref, gd_ref, gb_ref, gc_ref, gx, carry):
        i = pl.program_id(1)
        ib = nb - 1 - i

        @pl.when(i == 0)
        def _():
            carry[...] = jnp.zeros_like(carry)
            glam_ref[...] = jnp.zeros_like(glam_ref)
            gd_ref[...] = jnp.zeros_like(gd_ref)
            gb_ref[...] = jnp.zeros_like(gb_ref)
            gc_ref[...] = jnp.zeros_like(gc_ref)

        dyv = dy_ref[...]
        uv = u_ref[...]
        dyb = dyv.astype(BF16)
        gx[...] = jnp.dot(dyb, ct_ref[0], preferred_element_type=F32)
        last_row = lax.broadcasted_iota(jnp.int32, (SUBLANES, n), 0) == SUBLANES - 1

        def block(rows, xp_r, xp_i, c):
            cr, ci = c
            gr = gx[rows, 0:n]
            gi = gx[rows, n:n2]
            for lvl, sh in enumerate((1, 2, 4)):
                ar = t_ref[0, 2 * lvl]
                ai = t_ref[0, 2 * lvl + 1]
                sr = pltpu.roll(gr, SUBLANES - sh, 0)
                si = pltpu.roll(gi, SUBLANES - sh, 0)
                gr, gi = gr + ar * sr - ai * si, gi + ar * si + ai * sr
            lr = t_ref[0, 6]
            li = t_ref[0, 7]
            gr, gi = gr + lr * cr - li * ci, gi + lr * ci + li * cr
            gx[rows, 0:n] = gr
            gx[rows, n:n2] = gi
            xs_r = pltpu.roll(jnp.where(last_row, xp_r, st_ref[rows, 0:n]), 1, 0)
            xs_i = pltpu.roll(jnp.where(last_row, xp_i, st_ref[rows, n:n2]), 1, 0)
            glam_ref[0, :, 0:n] += gr * xs_r + gi * xs_i
            glam_ref[0, :, n:n2] += gi * xs_r - gr * xs_i
            return (jnp.broadcast_to(gr[0:1, :], (SUBLANES, n)), jnp.broadcast_to(gi[0:1, :], (SUBLANES, n)))

        def step(k, c):
            rb = nr - 1 - k
            rows = pl.ds(pl.multiple_of(rb * SUBLANES, SUBLANES), SUBLANES)
            before = pl.ds(pl.multiple_of(rb * SUBLANES - SUBLANES, SUBLANES), SUBLANES)
            return block(rows, st_ref[before, 0:n], st_ref[before, n:n2], c)

        c = lax.fori_loop(0, nr - 1, step, (carry[:, 0:n], carry[:, n:n2]))
        live = (ib > 0).astype(F32)
        cr, ci = block(pl.ds(0, SUBLANES), prev_ref[:, 0:n] * live, prev_ref[:, n:n2] * live, c)
        carry[:, 0:n] = cr
        carry[:, n:n2] = ci

        gxb = gx[...].astype(BF16)
        du_ref[...] = jnp.dot(gxb, bt_ref[0], preferred_element_type=F32) + d_ref[...] * dyv
        gd_ref[0] += _row_fold(dyv * uv)
        gb_ref[0] += lax.dot_general(uv.astype(BF16), gxb, TN_DIMS, preferred_element_type=F32)
        gc_ref[0] += lax.dot_general(st_ref[...].astype(BF16), dyb, TN_DIMS, preferred_element_type=F32)

    rpb = tm // SUBLANES
    return pl.pallas_call(
        body, name=name, grid=(nj, nb),
        in_specs=[pl.BlockSpec((tm, gb), lambda j, i: (nb - 1 - i, j)),
                  pl.BlockSpec((tm, gb), lambda j, i: (nb - 1 - i, j)),
                  pl.BlockSpec((tm, n2), lambda j, i: (nb - 1 - i, j)),
                  pl.BlockSpec((SUBLANES, n2), lambda j, i: (jnp.maximum((nb - 1 - i) * rpb - 1, 0), j)),
                  pl.BlockSpec((1, n2, gb), lambda j, i: (j, 0, 0)),
                  pl.BlockSpec((1, gb, n2), lambda j, i: (j, 0, 0)),
                  pl.BlockSpec((1, 8, SUBLANES, n), lambda j, i: (j, 0, 0, 0)),
                  pl.BlockSpec((1, gb), lambda j, i: (0, j))],
        out_specs=[pl.BlockSpec((tm, gb), lambda j, i: (nb - 1 - i, j)),
                   pl.BlockSpec((1, SUBLANES, n2), lambda j, i: (j, 0, 0)),
                   pl.BlockSpec((1, SUBLANES, gb), lambda j, i: (j, 0, 0)),
                   pl.BlockSpec((1, gb, n2), lambda j, i: (j, 0, 0)),
                   pl.BlockSpec((1, n2, gb), lambda j, i: (j, 0, 0))],
        out_shape=[jax.ShapeDtypeStruct((l, d), F32),
                   jax.ShapeDtypeStruct((nj, SUBLANES, n2), F32),
                   jax.ShapeDtypeStruct((nj, SUBLANES, gb), F32),
                   jax.ShapeDtypeStruct((nj, gb, n2), F32),
                   jax.ShapeDtypeStruct((nj, n2, gb), F32)],
        scratch_shapes=[pltpu.VMEM((tm, n2), F32), pltpu.VMEM((SUBLANES, n2), F32)],
        compiler_params=_cp("parallel", "arbitrary"),
    )(dy, u, states, states, bblk_t, cblk_t, tabs, dskip)


HEAD_SLOT = 128
PAIR_SLOT = 2 * HEAD_SLOT
ATTN_TILE = 1024
LANE_ROWSUM_P = HEAD_DIM
LANE_COLSUM_DS = HEAD_DIM
LANE_ROWSUM_DS = HEAD_DIM + 1


def _slot_cols(w):
    r, c = w.shape
    nh = c // HEAD_DIM
    return jnp.pad(w.reshape(r, nh, HEAD_DIM), ((0, 0), (0, 0), (0, HEAD_SLOT - HEAD_DIM))).reshape(r, nh * HEAD_SLOT)


def _unslot_cols(w):
    r, c = w.shape
    nh = c // HEAD_SLOT
    return w.reshape(r, nh, HEAD_SLOT)[:, :, :HEAD_DIM].reshape(r, nh * HEAD_DIM)


def _slot_ones(nh, lane):
    return jnp.tile((jnp.arange(HEAD_SLOT) == lane).astype(F32), nh).reshape(1, nh * HEAD_SLOT)


def _causal_tiles(n, by_query):
    if by_query:
        tiles = [(i, j) for i in range(n) for j in range(i + 1)]
    else:
        tiles = [(i, j) for j in range(n) for i in range(j, n)]
    return (jnp.asarray([t[0] for t in tiles], jnp.int32), jnp.asarray([t[1] for t in tiles], jnp.int32))


def _flash_fwd(qx, kx, vx, f2_rows, name):
    l = qx.shape[0]
    npair = qx.shape[1] // PAIR_SLOT
    d = npair * HEAD_PAIR
    tq = _tile(l, ATTN_TILE)
    tk = tq
    i_of, j_of = _causal_tiles(l // tq, by_query=True)

    def body(i_ref, j_ref, q_ref, k_ref, v_ref, f_ref, o_ref, lse_ref, m_sc, acc_sc):
        t = pl.program_id(1)
        i = i_ref[t]
        j = j_ref[t]

        @pl.when(j == 0)
        def _():
            m_sc[...] = jnp.full_like(m_sc, NEG)
            acc_sc[...] = jnp.zeros_like(acc_sc)

        def tile(on_diagonal):
            for hh in range(2):
                hs = slice(hh * HEAD_SLOT, (hh + 1) * HEAD_SLOT)
                s = lax.dot_general(q_ref[:, hs], k_ref[:, hs], NT_DIMS, preferred_element_type=F32)
                s = s - f_ref[0, hh:hh + 1, :]
                if on_diagonal:
                    keep = (lax.broadcasted_iota(jnp.int32, (tq, tk), 0)
                            >= lax.broadcasted_iota(jnp.int32, (tq, tk), 1))
                    s = jnp.where(keep, s, NEG)
                m_prev = m_sc[hh]
                m_new = jnp.maximum(m_prev, jnp.max(s, axis=-1, keepdims=True))
                alpha = jnp.exp2(m_prev - m_new)
                p = jnp.exp2(s - jnp.concatenate([m_new] * (tk // LANES), axis=1)).astype(BF16)
                acc_sc[hh] = alpha * acc_sc[hh] + jnp.dot(p, v_ref[:, hs], preferred_element_type=F32)
                m_sc[hh] = m_new

        @pl.when(j < i)
        def _():
            tile(False)

        @pl.when(j == i)
        def _():
            tile(True)
            outs, lses = [], []
            for hh in range(2):
                acc = acc_sc[hh]
                lsum = acc[:, LANE_ROWSUM_P:LANE_ROWSUM_P + 1]
                outs.append(acc[:, :HEAD_DIM] / lsum)
                lses.append(jnp.broadcast_to(m_sc[hh][:, 0:1] + jnp.log2(lsum), (tq, HEAD_DIM)))
            o_ref[...] = jnp.concatenate(outs, axis=1).astype(BF16)
            lse_ref[...] = jnp.concatenate(lses, axis=1)

    q_map = lambda h, t, i_ref, j_ref: (i_ref[t], h)
    kv_map = lambda h, t, i_ref, j_ref: (j_ref[t], h)
    return pl.pallas_call(
        body, name=name,
        grid_spec=pltpu.PrefetchScalarGridSpec(
            num_scalar_prefetch=2, grid=(npair, i_of.shape[0]),
            in_specs=[pl.BlockSpec((tq, PAIR_SLOT), q_map), pl.BlockSpec((tk, PAIR_SLOT), kv_map),
                      pl.BlockSpec((tk, PAIR_SLOT), kv_map),
                      pl.BlockSpec((1, 2, tk), lambda h, t, i_ref, j_ref: (h, 0, j_ref[t]))],
            out_specs=[pl.BlockSpec((tq, HEAD_PAIR), q_map), pl.BlockSpec((tq, HEAD_PAIR), q_map)],
            scratch_shapes=[pltpu.VMEM((2, tq, LANES), F32), pltpu.VMEM((2, tq, HEAD_SLOT), F32)]),
        out_shape=[jax.ShapeDtypeStruct((l, d), BF16), jax.ShapeDtypeStruct((l, d), F32)],
        compiler_params=_cp("parallel", "arbitrary"),
    )(i_of, j_of, qx, kx, vx, f2_rows)


def _flash_bwd(qx, kx, vx, f2_rep, do, lse_rows, delta_rows, name):
    l = qx.shape[0]
    npair = qx.shape[1] // PAIR_SLOT
    d = npair * HEAD_PAIR
    tq = _tile(l, ATTN_TILE)
    tk = tq
    i_of, j_of = _causal_tiles(l // tq, by_query=False)

    def body(i_ref, j_ref, q_ref, k_ref, v_ref, f_ref, do_ref, lse_ref, dl_ref, dq_ref, dk_ref, dv_ref):
        t = pl.program_id(1)
        i = i_ref[t]
        j = j_ref[t]

        @pl.when(t == 0)
        def _():
            dq_ref[...] = jnp.zeros_like(dq_ref)

        @pl.when(i == j)
        def _():
            dk_ref[...] = jnp.zeros_like(dk_ref)
            dv_ref[...] = jnp.zeros_like(dv_ref)

        def tile(on_diagonal):
            dqs, dks, dvs = [], [], []
            for hh in range(2):
                hs = slice(hh * HEAD_SLOT, (hh + 1) * HEAD_SLOT)
                qh, kh = q_ref[:, hs], k_ref[:, hs]
                vh = v_ref[:, hh * HEAD_SLOT:hh * HEAD_SLOT + HEAD_DIM]
                doh = do_ref[:, hh * HEAD_DIM:(hh + 1) * HEAD_DIM]
                st = lax.dot_general(kh, qh, NT_DIMS, preferred_element_type=F32)
                st = st - jnp.concatenate([f_ref[:, hs]] * (tq // HEAD_SLOT), axis=1)
                pt = jnp.exp2(st - lse_ref[0, hh:hh + 1, :])
                if on_diagonal:
                    keep = (lax.broadcasted_iota(jnp.int32, (tk, tq), 1)
                            >= lax.broadcasted_iota(jnp.int32, (tk, tq), 0))
                    pt = jnp.where(keep, pt, 0.0)
                dpt = lax.dot_general(vh, doh, NT_DIMS, preferred_element_type=F32)
                dsb = (pt * (dpt - dl_ref[0, hh:hh + 1, :])).astype(BF16)
                dvs.append(jnp.dot(pt.astype(BF16), doh, preferred_element_type=F32))
                dks.append(jnp.dot(dsb, qh, preferred_element_type=F32))
                dqs.append(lax.dot_general(dsb, kh, TN_DIMS, preferred_element_type=F32))
            dv_ref[...] += jnp.concatenate(dvs, axis=1)
            dk_ref[...] += jnp.concatenate(dks, axis=1)
            dq_ref[pl.ds(pl.multiple_of(i * tq, tq), tq), :] += jnp.concatenate(dqs, axis=1)

        @pl.when(i > j)
        def _():
            tile(False)

        @pl.when(i == j)
        def _():
            tile(True)

    qmap = lambda h, t, i_ref, j_ref: (i_ref[t], h)
    kmap = lambda h, t, i_ref, j_ref: (j_ref[t], h)
    row_map = lambda h, t, i_ref, j_ref: (h, 0, i_ref[t])
    return pl.pallas_call(
        body, name=name,
        grid_spec=pltpu.PrefetchScalarGridSpec(
            num_scalar_prefetch=2, grid=(npair, i_of.shape[0]),
            in_specs=[pl.BlockSpec((tq, PAIR_SLOT), qmap), pl.BlockSpec((tk, PAIR_SLOT), kmap),
                      pl.BlockSpec((tk, PAIR_SLOT), kmap), pl.BlockSpec((tk, PAIR_SLOT), kmap),
                      pl.BlockSpec((tq, HEAD_PAIR), qmap),
                      pl.BlockSpec((1, 2, tq), row_map), pl.BlockSpec((1, 2, tq), row_map)],
            out_specs=[pl.BlockSpec((l, PAIR_SLOT), lambda h, t, i_ref, j_ref: (0, h)),
                       pl.BlockSpec((tk, PAIR_SLOT), kmap), pl.BlockSpec((tk, HEAD_PAIR), kmap)]),
        out_shape=[jax.ShapeDtypeStruct((l, npair * PAIR_SLOT), F32), jax.ShapeDtypeStruct((l, npair * PAIR_SLOT), F32),
                   jax.ShapeDtypeStruct((l, d), F32)],
        compiler_params=_cp("parallel", "arbitrary"),
    )(i_of, j_of, qx, kx, vx, f2_rep, do, lse_rows, delta_rows)


def _my_place():
    return lax.axis_index("x"), lax.axis_index("y"), lax.axis_index("c")


def _chip_exchange(srcs, out_meta, plan, name):
    n_src, n_out, n_plan = len(srcs), len(out_meta), len(plan)

    def body(*refs):
        src_refs = refs[:n_src]
        out_refs = refs[n_src:n_src + n_out]
        send_sems, recv_sems, local_sems = refs[n_src + n_out:]
        x, y, c = _my_place()
        me = 2 * x + y
        copies = []
        for n, (si, oi, src_view, dst_view) in enumerate(plan):
            local = pltpu.make_async_copy(src_view(src_refs[si], me), dst_view(out_refs[oi], me), local_sems.at[n])
            local.start()
            copies.append(local)
            for k in (1, 2, 3):
                peer = me ^ k
                rc = pltpu.make_async_remote_copy(
                    src_ref=src_view(src_refs[si], peer), dst_ref=dst_view(out_refs[oi], me),
                    send_sem=send_sems.at[n, k - 1], recv_sem=recv_sems.at[n, k - 1],
                    device_id=(peer >> 1, peer & 1, c), device_id_type=MESH)
                rc.start()
                copies.append(rc)
        for cp in copies:
            cp.wait()

    any_spec = pl.BlockSpec(memory_space=pl.ANY)
    return pl.pallas_call(
        body, name=name,
        in_specs=[any_spec] * n_src, out_specs=[any_spec] * n_out,
        out_shape=[jax.ShapeDtypeStruct(shape, dt) for (shape, dt) in out_meta],
        scratch_shapes=[pltpu.SemaphoreType.DMA((n_plan, 3)), pltpu.SemaphoreType.DMA((n_plan, 3)),
                        pltpu.SemaphoreType.DMA((n_plan,))],
    )(*srcs)


def _plan_copies(src_refs, land_refs, plan, send_sems, recv_sems):
    x, y, c = _my_place()
    me = 2 * x + y
    copies = []
    for n, (si, oi, src_view, dst_view) in enumerate(plan):
        for k in (1, 2, 3):
            peer = me ^ k
            copies.append(pltpu.make_async_remote_copy(
                src_ref=src_view(src_refs[si], peer), dst_ref=dst_view(land_refs[oi], me),
                send_sem=send_sems.at[3 * n + k - 1], recv_sem=recv_sems.at[3 * n + k - 1],
                device_id=(peer >> 1, peer & 1, c), device_id_type=MESH))
    return copies


def _hbm(a):
    return pltpu.HBM(a.shape, a.dtype)


def _exchange_start(srcs, lands, plan, name):
    n_src, n_land = len(srcs), len(lands)
    n_buf = n_src + n_land

    def body(*refs):
        send_sems, recv_sems = refs[n_buf], refs[n_buf + 1]
        token = refs[-1]
        for cp in _plan_copies(refs[:n_src], refs[n_src:n_buf], plan, send_sems, recv_sems):
            cp.start()
        token[...] = jnp.zeros_like(token)

    bufs = [pltpu.with_memory_space_constraint(a, pltpu.HBM) for a in (*srcs, *lands)]
    hbm = pl.BlockSpec(memory_space=pltpu.HBM)
    sem = pl.BlockSpec(memory_space=pltpu.SEMAPHORE)
    res = pl.pallas_call(
        body, name=name,
        out_shape=(pltpu.SemaphoreType.DMA((3 * len(plan),)), pltpu.SemaphoreType.DMA((3 * len(plan),)),
                   *[_hbm(a) for a in bufs], jax.ShapeDtypeStruct((SUBLANES, LANES), F32)),
        in_specs=[hbm] * n_buf, out_specs=(sem, sem, *[hbm] * n_buf, pl.BlockSpec(memory_space=pltpu.VMEM)),
        input_output_aliases={n: 2 + n for n in range(n_buf)},
        compiler_params=pltpu.CompilerParams(has_side_effects=pltpu.SideEffectType.DATAFLOW_SIDE_EFFECTING),
    )(*bufs)
    return (res[0], res[1]), list(res[2:2 + n_src]), list(res[2 + n_src:2 + n_buf]), res[-1]


def _exchange_wait(sems, srcs, lands, plan, after, name):
    n_src, n_land = len(srcs), len(lands)
    n_buf = n_src + n_land

    def body(*refs):
        send_sems, recv_sems = refs[n_buf], refs[n_buf + 1]
        for cp in _plan_copies(refs[:n_src], refs[n_src:n_buf], plan, send_sems, recv_sems):
            cp.wait_send()
            cp.wait_recv()

    hbm = pl.BlockSpec(memory_space=pltpu.HBM)
    sem = pl.BlockSpec(memory_space=pltpu.SEMAPHORE)
    res = pl.pallas_call(
        body, name=name, out_shape=tuple(_hbm(a) for a in (*srcs, *lands)),
        in_specs=[hbm] * n_buf + [sem, sem, pl.BlockSpec(memory_space=pl.ANY)], out_specs=tuple([hbm] * n_buf),
        input_output_aliases={n: n for n in range(n_buf)},
        compiler_params=pltpu.CompilerParams(has_side_effects=pltpu.SideEffectType.DATAFLOW_SIDE_EFFECTING),
    )(*srcs, *lands, sems[0], sems[1], after)
    return list(res[n_src:])


def _core_exchange(arrays, name):
    n_items = len(arrays)

    def body(*refs):
        srcs = refs[:n_items]
        outs = refs[n_items:2 * n_items]
        send_sems, recv_sems = refs[2 * n_items:]
        x, y, c = _my_place()
        copies = []
        for n in range(n_items):
            rc = pltpu.make_async_remote_copy(
                src_ref=srcs[n], dst_ref=outs[n], send_sem=send_sems.at[n], recv_sem=recv_sems.at[n],
                device_id=(x, y, 1 - c), device_id_type=MESH)
            rc.start()
            copies.append(rc)
        for cp in copies:
            cp.wait()

    any_spec = pl.BlockSpec(memory_space=pl.ANY)
    return pl.pallas_call(
        body, name=name,
        in_specs=[any_spec] * n_items, out_specs=[any_spec] * n_items,
        out_shape=[jax.ShapeDtypeStruct(a.shape, a.dtype) for a in arrays],
        scratch_shapes=[pltpu.SemaphoreType.DMA((n_items,)), pltpu.SemaphoreType.DMA((n_items,))],
    )(*arrays)


def _sum_chips(parts, name):
    _, rows, cols = parts.shape
    tm = _tile(rows, 512)

    def body(p_ref, o_ref):
        acc = p_ref[0].astype(F32)
        for s in range(1, N_CHIPS):
            acc = acc + p_ref[s].astype(F32)
        o_ref[...] = acc

    return pl.pallas_call(
        body, name=name, grid=(rows // tm,),
        in_specs=[pl.BlockSpec((N_CHIPS, tm, cols), lambda i: (0, i, 0))],
        out_specs=pl.BlockSpec((tm, cols), lambda i: (i, 0)),
        out_shape=jax.ShapeDtypeStruct((rows, cols), F32),
        compiler_params=_cp("parallel"),
    )(parts)


def _adamw(ga, gb, w, m, v, name):
    rows, cols = w.shape
    tm = _tile(rows, 512)
    c1 = 1.0 - ADAM_B1 ** ADAM_STEP
    c2 = 1.0 - ADAM_B2 ** ADAM_STEP

    def body(ga_ref, gb_ref, w_ref, m_ref, v_ref, g_ref, d_ref, nm_ref, nv_ref):
        g = ga_ref[...] + gb_ref[...]
        nm = ADAM_B1 * m_ref[...] + (1.0 - ADAM_B1) * g
        nv = ADAM_B2 * v_ref[...] + (1.0 - ADAM_B2) * (g * g)
        g_ref[...] = g
        nm_ref[...] = nm
        nv_ref[...] = nv
        d_ref[...] = -ADAM_LR * ((nm / c1) / (jnp.sqrt(nv / c2) + ADAM_EPS) + ADAM_WD * w_ref[...])

    spec = pl.BlockSpec((tm, cols), lambda i: (i, 0))
    return pl.pallas_call(
        body, name=name, grid=(rows // tm,), in_specs=[spec] * 5, out_specs=[spec] * 4,
        out_shape=[jax.ShapeDtypeStruct((rows, cols), F32)] * 4, compiler_params=_cp("parallel"),
    )(ga, gb, w, m, v)


def _ssm_discretise(log_dt, a_re, a_im, b_re, b_im):
    dt = jnp.exp(log_dt)[:, None]
    mag = jnp.exp(a_re * dt)
    lbr = mag * jnp.cos(a_im * dt)
    lbi = mag * jnp.sin(a_im * dt)
    den = a_re * a_re + a_im * a_im
    nr, ni = lbr - 1.0, lbi
    qr = (nr * a_re + ni * a_im) / den
    qi = (ni * a_re - nr * a_im) / den
    bbr = qr[..., None] * b_re - qi[..., None] * b_im
    bbi = qr[..., None] * b_im + qi[..., None] * b_re
    return lbr, lbi, bbr, bbi


def _cmul(ar, ai, br, bi):
    return ar * br - ai * bi, ar * bi + ai * br


def _scan_tables(lr, li, nj, reverse):
    lr = lr.reshape(nj, 1, -1)
    li = li.reshape(nj, 1, -1)
    if reverse:
        li = -li
    pows = [(lr, li)]
    for _ in range(7):
        pows.append(_cmul(*pows[-1], lr, li))
    r = jnp.arange(SUBLANES).reshape(1, SUBLANES, 1)
    if reverse:
        r = SUBLANES - 1 - r
    out = []
    for k in (1, 2, 4):
        pr, pi = pows[k - 1]
        keep = (r >= k).astype(F32)
        out += [pr * keep, pi * keep]
    shape = (nj, SUBLANES, lr.shape[-1])
    cr = jnp.zeros(shape, F32)
    ci = jnp.zeros(shape, F32)
    for e in range(SUBLANES):
        sel = (r == e).astype(F32)
        cr = cr + sel * pows[e][0]
        ci = ci + sel * pows[e][1]
    out += [cr, ci]
    return jnp.stack(out, axis=1)


def _group_eye(gl):
    return jnp.eye(gl, dtype=F32)


def _block_diag_in(bbr, bbi, nj):
    g, p, c = bbr.shape
    gl = g // nj
    eye = _group_eye(gl)[None, :, None, :, None]

    def one(b):
        t = b.reshape(nj, gl, p, c).transpose(0, 1, 3, 2)[:, :, :, None, :]
        return (t * eye).reshape(nj, gl * c, gl * p)

    return jnp.concatenate([one(bbr), one(bbi)], axis=2)


def _block_diag_in_grad(gmat, nj, p, c):
    gl = gmat.shape[1] // c
    n = gl * p
    eye = _group_eye(gl)[None, :, None, :, None]

    def one(m):
        t = jnp.sum(m.reshape(nj, gl, c, gl, p) * eye, axis=3)
        return t.transpose(0, 1, 3, 2).reshape(nj * gl, p, c)

    return one(gmat[:, :, :n]), one(gmat[:, :, n:])


def _block_diag_out(c_re, c_im, nj):
    g, c, p = c_re.shape
    gl = g // nj
    eye = _group_eye(gl)[None, :, None, :, None]

    def one(m):
        t = m.reshape(nj, gl, c, p).transpose(0, 1, 3, 2)[:, :, :, None, :]
        return (t * eye).reshape(nj, gl * p, gl * c)

    return jnp.concatenate([one(c_re), -one(c_im)], axis=1)


def _block_diag_out_grad(gmat, nj, p, c):
    gl = gmat.shape[2] // c
    n = gl * p
    eye = _group_eye(gl)[None, :, None, :, None]

    def one(m):
        t = jnp.sum(m.reshape(nj, gl, p, gl, c) * eye, axis=3)
        return t.transpose(0, 1, 3, 2).reshape(nj * gl, c, p)

    return one(gmat[:, :n, :]), -one(gmat[:, n:, :])


def _pad_rows(flat, cols):
    per = SUBLANES * cols
    n = flat.shape[0]
    total = -(-n // per) * per
    return jnp.pad(flat, (0, total - n)).reshape(total // cols, cols)


def _pack_small(arrs, cols):
    packed = jnp.concatenate([_pad_rows(a.reshape(-1), cols) for a in arrs], axis=0)
    rows = packed.shape[0]
    return jnp.pad(packed, ((0, -rows % 128), (0, 0)))


def _unpack_small(packed, shapes, cols):
    out = []
    row = 0
    for s in shapes:
        n = math.prod(s)
        rows = -(-n // (SUBLANES * cols)) * SUBLANES
        out.append(packed[row:row + rows].reshape(-1)[:n].reshape(s))
        row += rows
    return out


def kernel(x, mix_norm, mlp_norm, mlp_w1, mlp_w2, ssm_log_dt, ssm_a_re, ssm_a_im, ssm_b_re, ssm_b_im, ssm_c_re, ssm_c_im, ssm_d, ssm_w_glu, kv_norm, w_kvf, b_f, attn_wq, attn_wo, final_norm, loss_target, m_mix_norm, m_mlp_norm, m_mlp_w1, m_mlp_w2, m_ssm_log_dt, m_ssm_a_re, m_ssm_a_im, m_ssm_b_re, m_ssm_b_im, m_ssm_c_re, m_ssm_c_im, m_ssm_d, m_ssm_w_glu, m_kv_norm, m_w_kvf, m_b_f, m_attn_wq, m_attn_wo, m_final_norm, v_mix_norm, v_mlp_norm, v_mlp_w1, v_mlp_w2, v_ssm_log_dt, v_ssm_a_re, v_ssm_a_im, v_ssm_b_re, v_ssm_b_im, v_ssm_c_re, v_ssm_c_im, v_ssm_d, v_ssm_w_glu, v_kv_norm, v_w_kvf, v_b_f, v_attn_wq, v_attn_wo, v_final_norm):
    seq, d = x.shape[1], x.shape[2]
    depth = mix_norm.shape[0]
    n_a = ssm_log_dt.shape[0]
    n_b = depth - n_a
    ff = mlp_w1.shape[2] * N_CHIPS
    n_heads = d // HEAD_DIM
    n_groups = d // SSM_GROUP
    p_state = ssm_a_re.shape[2]
    gb = min(d, 256)
    nj = d // gb
    kvf_cols = w_kvf.shape[1]
    ds4, dq4 = d // N_CHIPS, d // (2 * N_CHIPS)
    chip = 2 * lax.axis_index("x") + lax.axis_index("y")

    def cols_of(width):
        return lambda ref, s: ref.at[:, :, pl.ds(pl.multiple_of(s * width, LANES), width)]

    def rows_of(height):
        return lambda ref, s: ref.at[:, pl.ds(pl.multiple_of(s * height, SUBLANES), height), :]

    whole = lambda ref, s: ref
    slot = lambda ref, s: ref.at[s]
    def cols2(width):
        return lambda ref, s: ref.at[:, pl.ds(pl.multiple_of(s * width, LANES), width)]

    def rows2(height):
        return lambda ref, s: ref.at[pl.ds(pl.multiple_of(s * height, SUBLANES), height), :]

    assert n_a >= 2
    (skip_parts,) = _chip_exchange([ssm_d], [((N_CHIPS, n_a, ds4), F32)], [(0, 0, whole, slot)], "gather_skip")
    skip_all = skip_parts.transpose(1, 0, 2).reshape(n_a, d)
    w1_s, w2_s, glu_s = mlp_w1.astype(BF16), mlp_w2.astype(BF16), ssm_w_glu.astype(BF16)
    src_a = [w1_s[0], w2_s[0], glu_s[0]]
    plan_a = [(0, 0, whole, cols2(d)), (1, 1, whole, rows2(d)), (2, 2, whole, cols2(2 * ds4))]
    land_a = [lax.empty((d, ff), BF16), lax.empty((ff, d), BF16), lax.empty((d, 2 * d), BF16)]
    src_b = [w1_s[1:], w2_s[1:], glu_s[1:], w_kvf.astype(BF16), attn_wq.astype(BF16), attn_wo.astype(BF16)]
    plan_b = [(0, 0, whole, cols_of(d)), (1, 1, whole, rows_of(d)), (2, 2, whole, cols_of(2 * ds4)),
              (3, 3, whole, slot), (4, 4, whole, rows_of(ds4)), (5, 5, whole, rows_of(ds4))]
    land_b = [lax.empty((depth - 1, d, ff), BF16), lax.empty((depth - 1, ff, d), BF16),
              lax.empty((n_a - 1, d, 2 * d), BF16), lax.empty((N_CHIPS, d, kvf_cols), BF16),
              lax.empty((n_b, d, d), BF16), lax.empty((n_b, d, d), BF16)]
    def idx(*at):
        return [jnp.asarray(v, jnp.int32) for v in at]

    put = lax.dynamic_update_slice
    land_a = [put(land_a[0], src_a[0], idx(0, chip * d)), put(land_a[1], src_a[1], idx(chip * d, 0)),
              put(land_a[2], src_a[2], idx(0, chip * 2 * ds4))]
    land_b = [put(land_b[0], src_b[0], idx(0, 0, chip * d)), put(land_b[1], src_b[1], idx(0, chip * d, 0)),
              put(land_b[2], src_b[2], idx(0, 0, chip * 2 * ds4)), put(land_b[3], src_b[3][None], idx(chip, 0, 0)),
              put(land_b[4], src_b[4], idx(0, chip * ds4, 0)), put(land_b[5], src_b[5], idx(0, chip * ds4, 0))]
    sems_a, src_a, land_a, token_a = _exchange_start(src_a, land_a, plan_a, "gather_start_a")
    sems_b, src_b, land_b, token_b = _exchange_start(src_b, land_b, plan_b, "gather_start_b")
    started = token_a[0:1, 0:1] + token_b[0:1, 0:1]

    def layer_w1(i):
        return w1_0 if i == 0 else w1_rest[i - 1]

    def layer_w2(i):
        return w2_0 if i == 0 else w2_rest[i - 1]

    def layer_glu(i):
        return glu_0 if i == 0 else glu_rest[i - 1]

    h = x[0]
    target = loss_target[0]

    saved = []
    for i in range(n_a):
        lbr, lbi, bbr, bbi = _ssm_discretise(ssm_log_dt[i], ssm_a_re[i], ssm_a_im[i], ssm_b_re[i], ssm_b_im[i])
        bblk = _block_diag_in(bbr, bbi, nj)
        cblk = _block_diag_out(ssm_c_re[i], ssm_c_im[i], nj)
        rec = dict(h0=h, lam=(lbr, lbi), bblk=bblk, cblk=cblk)
        gain = mix_norm[i:i + 1] + started if i == 0 else mix_norm[i:i + 1]
        u = _norm_fwd(h, gain, f"s5_norm_{i}")
        rec["u"] = u
        dskip = rec["dskip"] = skip_all[i:i + 1]
        states, y = _scan_fwd(u, bblk.astype(BF16), cblk.astype(BF16), _scan_tables(lbr, lbi, nj, False), dskip,
                              f"s5_scan_{i}")
        rec["states"], rec["y"] = states, y
        if i == 0:
            w1_0, w2_0, glu_0 = _exchange_wait(sems_a, src_a, land_a, plan_a, y, "gather_wait_a")
        if i == 1:
            w1_rest, w2_rest, glu_rest, kvf_parts, wq_all, wo_all = _exchange_wait(
                sems_b, src_b, land_b, plan_b, y, "gather_wait_b")
        h, rec["zw"] = _s5_post_fwd(h, y, layer_glu(i), f"s5_glu_{i}")
        rec["h1"] = h
        h, rec["ap"] = _mlp_fwd(h, mlp_norm[i:i + 1], layer_w1(i), layer_w2(i), f"mlp_{i}")
        saved.append(rec)
    h_kv = h
    kvf_all = jnp.concatenate([kvf_parts[s] for s in range(N_CHIPS)], axis=1)
    wk = kvf_all[:, :d]
    wv = kvf_all[:, d:2 * d]
    wf = jnp.pad(kvf_all[:, 2 * d:], ((0, 0), (0, LANES - n_heads)))
    bf_row = jnp.pad(b_f, (0, LANES - n_heads)).reshape(1, LANES)
    wk_x = _slot_cols(wk)
    spread = (jnp.arange(LANES)[:, None] == jnp.arange(n_heads * HEAD_SLOT)[None, :] // HEAD_SLOT).astype(BF16)
    kx, vx, flog, cum, f2_rep = _kvf_fwd(h, kv_norm.reshape(1, d), wk_x, _slot_cols(wv), wf, bf_row,
                                         _slot_ones(n_heads, LANE_ROWSUM_DS), _slot_ones(n_heads, LANE_ROWSUM_P),
                                         spread, "kvf")
    f2_rows = (cum[:, :n_heads] * LOG2E).T.reshape(n_heads // 2, 2, seq)
    wq_x = [_slot_cols(wq_all[jb]) for jb in range(n_b)]
    for jb in range(n_b):
        i = n_a + jb
        rec = dict(h0=h)
        qx = _q_fwd(h, mix_norm[i:i + 1], wq_x[jb], _slot_ones(n_heads, LANE_COLSUM_DS), f"attn_q_{jb}")
        o, lse = _flash_fwd(qx, kx, vx, f2_rows, f"attn_core_{jb}")
        rec["qx"], rec["o"], rec["lse"] = qx, o, lse
        h = _o_fwd(h, o, wo_all[jb], f"attn_out_{jb}")
        rec["h1"] = h
        h, rec["ap"] = _mlp_fwd(h, mlp_norm[i:i + 1], layer_w1(i), layer_w2(i), f"mlp_{i}")
        saved.append(rec)
    dh, loss_row, g_final = _loss_head(h, target, final_norm.reshape(1, d), "loss_head")
    loss = lax.psum(loss_row[0, 0], ("x", "y", "c"))

    head_ones = (jnp.arange(d)[:, None] // HEAD_DIM == jnp.arange(d)[None, :] // HEAD_DIM).astype(BF16)
    g_mix = [None] * depth
    g_mlp = [None] * depth
    g_w1 = [None] * depth
    g_w2 = [None] * depth
    g_wq = [None] * n_b
    g_wo = [None] * n_b
    g_glu = [None] * n_a
    g_ssm = [None] * n_a
    dk_parts, dv_parts, df_parts = [], [], []

    def head_rows(rep):
        return rep[:, ::HEAD_DIM].T.reshape(n_heads // 2, 2, seq)

    red_waits = []

    def reduce_start(entries, name):
        numbers = sorted({e[1] for e in entries})
        lands = {}
        for (_, ln, shape, dt, _, _, _, _) in entries:
            if ln not in lands:
                lands[ln] = lax.empty(shape, dt)
        for (_, ln, _, _, _, _, own, at) in entries:
            lands[ln] = lax.dynamic_update_slice(lands[ln], own, idx(*at))
        plan = [(n, numbers.index(e[1]), e[4], e[5]) for n, e in enumerate(entries)]
        sems, srcs, lands_t, token = _exchange_start([e[0] for e in entries], [lands[ln] for ln in numbers], plan,
                                                     name + "_start")
        red_waits.append((sems, srcs, lands_t, plan, name + "_wait"))
        return token[0:1, 0:1]

    def w1_entry(i, ln, local, n_layers):
        own = lax.dynamic_slice(g_w1[i], idx(0, chip * d), (d, d))[None, None]
        return (g_w1[i], ln, (N_CHIPS, n_layers, d, d), BF16, cols2(d), into(local), own, (chip, local, 0, 0))

    def w2_entry(i, ln, local, n_layers):
        own = lax.dynamic_slice(g_w2[i], idx(chip * d, 0), (d, d))[None, None]
        return (g_w2[i], ln, (N_CHIPS, n_layers, d, d), BF16, rows2(d), into(local), own, (chip, local, 0, 0))

    def glu_entry(i, ln, local, n_layers):
        own = lax.dynamic_slice(g_glu[i], idx(0, chip * 2 * ds4), (d, 2 * ds4))[None, None]
        return (g_glu[i], ln, (N_CHIPS, n_layers, d, 2 * ds4), BF16, cols2(2 * ds4), into(local), own,
                (chip, local, 0, 0))

    def rows_entry(g, ln, local):
        own = lax.dynamic_slice(g, idx(chip * ds4, 0), (ds4, d))[None, None]
        return (g, ln, (N_CHIPS, n_b, ds4, d), BF16, rows2(ds4), into(local), own, (chip, local, 0, 0))

    def into(layer):
        return lambda ref, s: ref.at[s, layer]

    def mlp_back(dh, i, rec, tie=None):
        gain = mlp_norm[i:i + 1] if tie is None else mlp_norm[i:i + 1] + tie
        dh_in, hm, a, dap, g_mlp[i] = _mlp_bwd(dh, rec["h1"], rec["ap"], gain, layer_w1(i),
                                               layer_w2(i), f"mlp_bwd_{i}")
        g_w2[i] = _matmul_tn(a, dh, f"mlp_dw2_{i}")
        g_w1[i] = _matmul_tn(hm, dap, f"mlp_dw1_{i}")
        return dh_in

    for jb in reversed(range(n_b)):
        i = n_a + jb
        rec = saved[i]
        dh = mlp_back(dh, i, rec)
        do, delta = _o_bwd(dh, rec["o"], wo_all[jb], head_ones, f"attn_out_bwd_{jb}")
        g_wo[jb] = _matmul_tn(rec["o"], dh, f"attn_dwo_{jb}")
        dqx, dkx, dv = _flash_bwd(rec["qx"], kx, vx, f2_rep, do, head_rows(rec["lse"]), head_rows(delta),
                                  f"attn_core_bwd_{jb}")
        dk_parts.append(dkx)
        dv_parts.append(dv)
        df_parts.append(dqx[:, LANE_ROWSUM_DS::HEAD_SLOT] - dkx[:, LANE_COLSUM_DS::HEAD_SLOT])
        dh, hn, dqs, g_mix[i] = _q_bwd(dh, rec["h0"], dqx, mix_norm[i:i + 1], wq_x[jb], f"attn_q_bwd_{jb}")
        g_wq[jb] = _unslot_cols(_matmul_tn(hn, dqs, f"attn_dwq_{jb}"))

    dft = df_parts[0]
    for extra in df_parts[1:]:
        dft = dft + extra
    dcum = jnp.pad(dft, ((0, 0), (0, LANES - n_heads)))
    dh, hk, dkb, dvb, dfb, g_kvn, g_bf = _kvf_bwd(dh, h_kv, dk_parts[0], dk_parts[1], dv_parts[0], dv_parts[1],
                                                  dcum, flog, kv_norm.reshape(1, d), wk_x, wv, wf, "kvf_bwd")
    g_kvf = jnp.concatenate([_unslot_cols(_matmul_tn(hk, dkb, "kvf_dwk")), _matmul_tn(hk, dvb, "kvf_dwv"),
                             _matmul_tn(hk, dfb, "kvf_dwf")[:, :n_heads]], axis=1)
    kvf_send = g_kvf.reshape(d, N_CHIPS, kvf_cols).transpose(1, 0, 2)
    group = [w1_entry(n_a + jb, 0, jb, n_b) for jb in range(n_b)]
    group += [w2_entry(n_a + jb, 1, jb, n_b) for jb in range(n_b)]
    group.append((kvf_send, 2, (N_CHIPS, d, kvf_cols), BF16, slot, slot,
                  lax.dynamic_index_in_dim(kvf_send, chip, 0, keepdims=True), (chip, 0, 0)))
    group += [rows_entry(g_wq[jb], 3, jb) for jb in range(n_b)]
    group += [rows_entry(g_wo[jb], 4, jb) for jb in range(n_b)]
    tie = reduce_start(group, "reduce_attn")

    for i in reversed(range(n_a)):
        rec = saved[i]
        if i == 0:
            group = [w1_entry(l, 0, l - 1, n_a - 1) for l in range(1, n_a)]
            group += [w2_entry(l, 1, l - 1, n_a - 1) for l in range(1, n_a)]
            group += [glu_entry(l, 2, l - 1, n_a - 1) for l in range(1, n_a)]
            tie = reduce_start(group, "reduce_s5")
        dh = mlp_back(dh, i, rec, tie if i in (0, n_a - 1) else None)
        dy, z, dzw = _s5_post_bwd(dh, rec["y"], rec["zw"], layer_glu(i), f"s5_glu_bwd_{i}")
        g_glu[i] = _matmul_tn(z, dzw, f"s5_dwglu_{i}")
        lbr, lbi = rec["lam"]
        bblk_t = rec["bblk"].transpose(0, 2, 1).astype(BF16)
        cblk_t = rec["cblk"].transpose(0, 2, 1).astype(BF16)
        du, glam8, gd8, gbblk, gcblk = _scan_bwd(dy, rec["u"], rec["states"], bblk_t, cblk_t,
                                                 _scan_tables(lbr, lbi, nj, True), rec["dskip"], f"s5_scan_bwd_{i}")
        dh, g_mix[i] = _norm_bwd_add(dh, du, rec["h0"], mix_norm[i:i + 1], f"s5_norm_bwd_{i}")
        glam = jnp.sum(glam8, axis=1)
        n_st = glam.shape[1] // 2
        g_lbr = glam[:, :n_st].reshape(n_groups, p_state)
        g_lbi = glam[:, n_st:].reshape(n_groups, p_state)
        g_bbr, g_bbi = _block_diag_in_grad(gbblk, nj, p_state, SSM_GROUP)
        g_cre, g_cim = _block_diag_out_grad(gcblk, nj, p_state, SSM_GROUP)
        _, pull = jax.vjp(_ssm_discretise, ssm_log_dt[i], ssm_a_re[i], ssm_a_im[i], ssm_b_re[i], ssm_b_im[i])
        g_ldt, g_are, g_aim, g_bre, g_bim = pull((g_lbr, g_lbi, g_bbr, g_bbi))
        g_ssm[i] = dict(log_dt=g_ldt, a_re=g_are, a_im=g_aim, b_re=g_bre, b_im=g_bim, c_re=g_cre, c_im=g_cim,
                        d=jnp.sum(gd8, axis=1).reshape(d))
    grad_x = dh[None]

    def stack_small(key):
        return jnp.stack([g_ssm[i][key] for i in range(n_a)])

    small_grads = [jnp.concatenate(g_mix, axis=0), jnp.concatenate(g_mlp, axis=0), stack_small("log_dt"),
                   stack_small("a_re"), stack_small("a_im"), stack_small("b_re"), stack_small("b_im"),
                   stack_small("c_re"), stack_small("c_im"), stack_small("d"), g_kvn.reshape(d),
                   g_bf[0, :n_heads], g_final.reshape(d)]
    small_w = [mix_norm, mlp_norm, ssm_log_dt, ssm_a_re, ssm_a_im, ssm_b_re, ssm_b_im, ssm_c_re, ssm_c_im,
               ssm_d, kv_norm, b_f, final_norm]
    small_m = [m_mix_norm, m_mlp_norm, m_ssm_log_dt, m_ssm_a_re, m_ssm_a_im, m_ssm_b_re, m_ssm_b_im, m_ssm_c_re,
               m_ssm_c_im, m_ssm_d, m_kv_norm, m_b_f, m_final_norm]
    small_v = [v_mix_norm, v_mlp_norm, v_ssm_log_dt, v_ssm_a_re, v_ssm_a_im, v_ssm_b_re, v_ssm_b_im, v_ssm_c_re,
               v_ssm_c_im, v_ssm_d, v_kv_norm, v_b_f, v_final_norm]
    skip_at = 9

    def widen_skip(part):
        return lax.dynamic_update_slice(jnp.zeros((n_a, d), F32), part, (0, chip * ds4))

    small_shapes = [a.shape for a in small_grads]
    pcols = 1024 if d >= 1024 else LANES
    g_small = _pack_small(small_grads, pcols)
    expand = lambda lst: _pack_small([widen_skip(a) if n == skip_at else a for n, a in enumerate(lst)], pcols)
    w_small, m_small, v_small = expand(small_w), expand(small_m), expand(small_v)
    srows = g_small.shape[0]

    assert n_b == 2
    group = [w1_entry(0, 0, 0, 1), w2_entry(0, 1, 0, 1), glu_entry(0, 2, 0, 1),
             (g_small, 3, (N_CHIPS, srows, pcols), F32, whole, slot, g_small[None], (chip, 0, 0))]
    reduce_start(group, "reduce_first")
    landed = [_exchange_wait(sems, srcs, lands, plan, dh, name) for (sems, srcs, lands, plan, name) in red_waits]
    (a_w1, a_w2, a_kvf, a_wq, a_wo), (s_w1, s_w2, s_glu), (f_w1, f_w2, f_glu, r_small) = landed

    def chip_sum(r, name):
        return _sum_chips(r.reshape(N_CHIPS, -1, r.shape[-1]), name)

    sums = [jnp.concatenate([chip_sum(f_w1, "sum_w1_first"), chip_sum(s_w1, "sum_w1_s5"),
                             chip_sum(a_w1, "sum_w1_attn")], axis=0),
            jnp.concatenate([chip_sum(f_w2, "sum_w2_first"), chip_sum(s_w2, "sum_w2_s5"),
                             chip_sum(a_w2, "sum_w2_attn")], axis=0),
            jnp.concatenate([chip_sum(f_glu, "sum_glu_first"), chip_sum(s_glu, "sum_glu_s5")], axis=0),
            chip_sum(a_kvf, "sum_kvf"), chip_sum(a_wq, "sum_wq"), chip_sum(a_wo, "sum_wo"),
            chip_sum(r_small, "sum_small")]
    others = _core_exchange(sums, "reduce_cores")

    def two(a):
        return a.reshape(-1, a.shape[-1])

    big_w = [(mlp_w1, m_mlp_w1, v_mlp_w1), (mlp_w2, m_mlp_w2, v_mlp_w2), (ssm_w_glu, m_ssm_w_glu, v_ssm_w_glu),
             (w_kvf, m_w_kvf, v_w_kvf), (attn_wq, m_attn_wq, v_attn_wq), (attn_wo, m_attn_wo, v_attn_wo)]
    big_out = []
    for n, (w, m, v) in enumerate(big_w):
        res = _adamw(sums[n], others[n], two(w), two(m), two(v), f"adamw_{n}")
        big_out.append([r.reshape(w.shape) for r in res])
    small_out = _adamw(sums[6], others[6], w_small, m_small, v_small, "adamw_small")

    def narrow_skip(a):
        return lax.dynamic_slice(a, (0, chip * ds4), (n_a, ds4))

    unpacked = []
    for packed in small_out:
        parts = _unpack_small(packed, small_shapes, pcols)
        parts[skip_at] = narrow_skip(parts[skip_at])
        unpacked.append(parts)

    order = ["mix_norm", "mlp_norm", "mlp_w1", "mlp_w2", "ssm_log_dt", "ssm_a_re", "ssm_a_im", "ssm_b_re",
             "ssm_b_im", "ssm_c_re", "ssm_c_im", "ssm_d", "ssm_w_glu", "kv_norm", "w_kvf", "b_f", "attn_wq",
             "attn_wo", "final_norm"]
    small_names = ["mix_norm", "mlp_norm", "ssm_log_dt", "ssm_a_re", "ssm_a_im", "ssm_b_re", "ssm_b_im",
                   "ssm_c_re", "ssm_c_im", "ssm_d", "kv_norm", "b_f", "final_norm"]
    big_names = ["mlp_w1", "mlp_w2", "ssm_w_glu", "w_kvf", "attn_wq", "attn_wo"]
    outs = [loss, grad_x]
    for kind in range(4):
        for name in order:
            if name in big_names:
                outs.append(big_out[big_names.index(name)][kind])
            else:
                outs.append(unpacked[kind][small_names.index(name)])
    return tuple(outs)
```

```python
import functools
import math

import jax
import jax.numpy as jnp
from jax import lax
from jax.experimental import pallas as pl
from jax.experimental.pallas import tpu as pltpu

F32 = jnp.float32
BF16 = jnp.bfloat16

RMS_EPS = 1e-6
SSM_GROUP = 16
SSM_STATE = 64
HEAD_DIM = 64
HEAD_PAIR = 2 * HEAD_DIM
LANES = 128
SUBLANES = 8
N_CHIPS = 4
ADAM_LR = 0.001
ADAM_B1 = 0.9
ADAM_B2 = 0.999
ADAM_EPS = 1e-08
ADAM_WD = 0.01
ADAM_STEP = 10
GELU_C = math.sqrt(2.0 / math.pi)
GELU_A = 0.044715
NEG = -1e30
LN2 = math.log(2.0)
LOG2E = 1.0 / LN2
VMEM_LIMIT = 56 * 1024 * 1024
MESH = pl.DeviceIdType.MESH

NT_DIMS = (((1,), (1,)), ((), ()))
TN_DIMS = (((0,), (0,)), ((), ()))


def _cp(*sem):
    return pltpu.CompilerParams(dimension_semantics=sem if sem else None, vmem_limit_bytes=VMEM_LIMIT)


def _zero_idx(nd, *_):
    return (0,) * nd


def _tile(n, t):
    if n <= t:
        return n
    for cand in range(t - t % SUBLANES, 0, -SUBLANES):
        if n % cand == 0:
            return cand
    raise ValueError((n, t))


def _rms_fwd(h, g):
    r = lax.rsqrt(jnp.mean(h * h, axis=-1, keepdims=True) + RMS_EPS)
    hhat = h * r
    return hhat * g, hhat, r


def _rms_bwd(du, hhat, r, g):
    dhh = du * g
    dh = r * (dhh - hhat * jnp.mean(dhh * hhat, axis=-1, keepdims=True))
    return dh, du * hhat


def _sigmoid(x):
    return 1.0 / (1.0 + jnp.exp(-x))


def _gelu(x):
    t = jnp.tanh(GELU_C * (x + GELU_A * x * x * x))
    return 0.5 * x * (1.0 + t)


def _gelu_grad(x):
    t = jnp.tanh(GELU_C * (x + GELU_A * x * x * x))
    return 0.5 * (1.0 + t) + 0.5 * x * (1.0 - t * t) * GELU_C * (1.0 + 3.0 * GELU_A * x * x)


def _row_fold(x):
    tm, w = x.shape
    return jnp.sum(x.reshape(tm // SUBLANES, SUBLANES, w), axis=0)


def _split3(x):
    hi = x.astype(BF16)
    r1 = x - hi.astype(F32)
    mid = r1.astype(BF16)
    lo = (r1 - mid.astype(F32)).astype(BF16)
    return hi, mid, lo


def _exact_dot(ones_mat, x):
    hi, mid, lo = _split3(x)
    d = functools.partial(jnp.dot, preferred_element_type=F32)
    return d(ones_mat, hi) + d(ones_mat, mid) + d(ones_mat, lo)


def _rows_call(body, name, tm, row_ins, const_ins, row_outs, acc_outs=(), scratch=(), reverse=False, col_outs=()):
    n = row_ins[0].shape[0]
    nb = n // tm
    if reverse:
        ridx = lambda i: (nb - 1 - i, 0)
        cidx = lambda i: (0, nb - 1 - i)
    else:
        ridx = lambda i: (i, 0)
        cidx = lambda i: (0, i)
    in_specs = [pl.BlockSpec((tm, a.shape[1]), ridx) for a in row_ins]
    in_specs += [pl.BlockSpec(a.shape, functools.partial(_zero_idx, a.ndim), pipeline_mode=pl.Buffered(1))
                 for a in const_ins]
    out_shape = [jax.ShapeDtypeStruct((n, w), dt) for (w, dt) in row_outs]
    out_shape += [jax.ShapeDtypeStruct(s, dt) for (s, dt) in acc_outs]
    out_shape += [jax.ShapeDtypeStruct((r, n), dt) for (r, dt) in col_outs]
    out_specs = [pl.BlockSpec((tm, w), ridx) for (w, dt) in row_outs]
    out_specs += [pl.BlockSpec(s, functools.partial(_zero_idx, len(s))) for (s, dt) in acc_outs]
    out_specs += [pl.BlockSpec((r, tm), cidx) for (r, dt) in col_outs]
    return pl.pallas_call(
        body, name=name, grid=(nb,), in_specs=in_specs, out_specs=out_specs, out_shape=out_shape,
        scratch_shapes=list(scratch), compiler_params=_cp("arbitrary"),
    )(*row_ins, *const_ins)


def _norm_fwd(h, g, name):
    n, d = h.shape
    tm = _tile(n, 512)

    def body(h_ref, g_ref, u_ref):
        u_ref[...] = _rms_fwd(h_ref[...], g_ref[...])[0]

    return _rows_call(body, name, tm, [h], [g], [(d, F32)])[0]


def _norm_bwd_add(dh, du, h, g, name):
    n, d = h.shape
    tm = _tile(n, 512)
    nb = n // tm

    def body(dh_ref, du_ref, h_ref, g_ref, o_ref, dg_ref, acc):
        i = pl.program_id(0)

        @pl.when(i == 0)
        def _():
            acc[...] = jnp.zeros_like(acc)

        gain = g_ref[...]
        _, hhat, r = _rms_fwd(h_ref[...], gain)
        dhn, dgr = _rms_bwd(du_ref[...], hhat, r, gain)
        o_ref[...] = dh_ref[...] + dhn
        acc[...] += _row_fold(dgr)

        @pl.when(i == nb - 1)
        def _():
            dg_ref[...] = jnp.sum(acc[...], axis=0, keepdims=True)

    return _rows_call(body, name, tm, [dh, du, h], [g], [(d, F32)], [((1, d), F32)],
                      [pltpu.VMEM((SUBLANES, d), F32)])


def _mlp_fwd(h, g, w1, w2, name):
    n, d = h.shape
    ff = w1.shape[1]
    tm = _tile(n, 256)
    fc = _tile(ff, 1024)

    def body(h_ref, g_ref, w1_ref, w2_ref, o_ref, ap_ref):
        hin = h_ref[...]
        hb = _rms_fwd(hin, g_ref[...])[0].astype(BF16)
        acc = hin
        for c in range(ff // fc):
            cs = slice(c * fc, (c + 1) * fc)
            ap = jnp.dot(hb, w1_ref[:, cs], preferred_element_type=F32)
            ap_ref[:, cs] = ap.astype(BF16)
            rl = jnp.maximum(ap, 0.0)
            acc = acc + jnp.dot((rl * rl).astype(BF16), w2_ref[cs, :], preferred_element_type=F32)
        o_ref[...] = acc

    return _rows_call(body, name, tm, [h], [g, w1, w2], [(d, F32), (ff, BF16)])


def _mlp_bwd(dh, h, ap, g, w1, w2, name):
    n, d = h.shape
    ff = w1.shape[1]
    tm = _tile(n, 256)
    nb = n // tm
    fc = _tile(ff, 1024)

    def body(dh_ref, h_ref, ap_ref, g_ref, w1_ref, w2_ref, o_ref, hm_ref, a_ref, dap_ref, dg_ref, acc):
        i = pl.program_id(0)

        @pl.when(i == 0)
        def _():
            acc[...] = jnp.zeros_like(acc)

        gain = g_ref[...]
        dhv = dh_ref[...]
        hm, hhat, r = _rms_fwd(h_ref[...], gain)
        hm_ref[...] = hm.astype(BF16)
        dhb = dhv.astype(BF16)
        dhm = jnp.zeros((tm, d), F32)
        for c in range(ff // fc):
            cs = slice(c * fc, (c + 1) * fc)
            rl = jnp.maximum(ap_ref[:, cs].astype(F32), 0.0)
            a_ref[:, cs] = (rl * rl).astype(BF16)
            da = lax.dot_general(dhb, w2_ref[cs, :], NT_DIMS, preferred_element_type=F32)
            dap = (da * (2.0 * rl)).astype(BF16)
            dap_ref[:, cs] = dap
            dhm = dhm + lax.dot_general(dap, w1_ref[:, cs], NT_DIMS, preferred_element_type=F32)
        dhn, dgr = _rms_bwd(dhm, hhat, r, gain)
        o_ref[...] = dhv + dhn
        acc[...] += _row_fold(dgr)

        @pl.when(i == nb - 1)
        def _():
            dg_ref[...] = jnp.sum(acc[...], axis=0, keepdims=True)

    return _rows_call(body, name, tm, [dh, h, ap], [g, w1, w2],
                      [(d, F32), (d, BF16), (ff, BF16), (ff, BF16)], [((1, d), F32)],
                      [pltpu.VMEM((SUBLANES, d), F32)])


def _s5_post_fwd(h, y, w_glu, name):
    n, d = h.shape
    tm = _tile(n, 512)

    def body(h_ref, y_ref, w_ref, o_ref, zw_ref):
        z = _gelu(y_ref[...]).astype(BF16)
        zw = jnp.dot(z, w_ref[...], preferred_element_type=F32)
        zw_ref[...] = zw.astype(BF16)
        o_ref[...] = h_ref[...] + zw[:, :d] * _sigmoid(zw[:, d:])

    return _rows_call(body, name, tm, [h, y], [w_glu], [(d, F32), (2 * d, BF16)])


def _s5_post_bwd(dh, y, zw, w_glu, name):
    n, d = dh.shape
    tm = _tile(n, 512)

    def body(dh_ref, y_ref, zw_ref, w_ref, dy_ref, z_ref, dzw_ref):
        dhv = dh_ref[...]
        yv = y_ref[...]
        val = zw_ref[:, :d].astype(F32)
        sg = _sigmoid(zw_ref[:, d:].astype(F32))
        dzw = jnp.concatenate([dhv * sg, dhv * val * sg * (1.0 - sg)], axis=1).astype(BF16)
        dzw_ref[...] = dzw
        dz = lax.dot_general(dzw, w_ref[...], NT_DIMS, preferred_element_type=F32)
        dy_ref[...] = dz * _gelu_grad(yv)
        z_ref[...] = _gelu(yv).astype(BF16)

    return _rows_call(body, name, tm, [dh, y, zw], [w_glu], [(d, F32), (d, BF16), (2 * d, BF16)])


def _q_fwd(h, g, wq_x, ones_x, name):
    n, d = h.shape
    tm = _tile(n, 512)
    scale = LOG2E * HEAD_DIM ** -0.5

    def body(h_ref, g_ref, w_ref, one_ref, q_ref):
        hb = _rms_fwd(h_ref[...], g_ref[...])[0].astype(BF16)
        q_ref[...] = (jnp.dot(hb, w_ref[...], preferred_element_type=F32) * scale + one_ref[...]).astype(BF16)

    return _rows_call(body, name, tm, [h], [g, wq_x, ones_x], [(wq_x.shape[1], BF16)])[0]


def _q_bwd(dh, h, dq, g, wq, name):
    n, d = h.shape
    tm = _tile(n, 512)
    nb = n // tm
    scale = HEAD_DIM ** -0.5

    def body(dh_ref, h_ref, dq_ref, g_ref, w_ref, o_ref, hn_ref, dqs_ref, dg_ref, acc):
        i = pl.program_id(0)

        @pl.when(i == 0)
        def _():
            acc[...] = jnp.zeros_like(acc)

        gain = g_ref[...]
        hn, hhat, r = _rms_fwd(h_ref[...], gain)
        hn_ref[...] = hn.astype(BF16)
        dqs = (dq_ref[...] * scale).astype(BF16)
        dqs_ref[...] = dqs
        dhn = lax.dot_general(dqs, w_ref[...], NT_DIMS, preferred_element_type=F32)
        dhi, dgr = _rms_bwd(dhn, hhat, r, gain)
        o_ref[...] = dh_ref[...] + dhi
        acc[...] += _row_fold(dgr)

        @pl.when(i == nb - 1)
        def _():
            dg_ref[...] = jnp.sum(acc[...], axis=0, keepdims=True)

    return _rows_call(body, name, tm, [dh, h, dq], [g, wq], [(d, F32), (d, BF16), (wq.shape[1], BF16)],
                      [((1, d), F32)], [pltpu.VMEM((SUBLANES, d), F32)])


def _o_fwd(h, o, wo, name):
    n, d = h.shape
    tm = _tile(n, 512)

    def body(h_ref, o_ref, w_ref, out_ref):
        out_ref[...] = h_ref[...] + jnp.dot(o_ref[...], w_ref[...], preferred_element_type=F32)

    return _rows_call(body, name, tm, [h, o], [wo], [(d, F32)])[0]


def _exact_dot_nt(ones_mat, x):
    hi, mid, lo = _split3(x)
    d = functools.partial(lax.dot_general, dimension_numbers=NT_DIMS, preferred_element_type=F32)
    return d(ones_mat, hi) + d(ones_mat, mid) + d(ones_mat, lo)


def _o_bwd(dh, o, wo, head_sel, name):
    n, d = dh.shape
    tm = _tile(n, 512)

    def body(dh_ref, o_ref, w_ref, e_ref, do_ref, dl_ref):
        do = lax.dot_general(dh_ref[...].astype(BF16), w_ref[...], NT_DIMS, preferred_element_type=F32).astype(BF16)
        do_ref[...] = do
        dl_ref[...] = _exact_dot_nt(e_ref[...], do.astype(F32) * o_ref[...].astype(F32))

    return _rows_call(body, name, tm, [dh, o], [wo, head_sel], [(wo.shape[0], BF16)],
                      col_outs=[(head_sel.shape[0], F32)])


def _exact_dot_rhs(x, ones_mat):
    hi, mid, lo = _split3(x)
    d = functools.partial(jnp.dot, preferred_element_type=F32)
    return d(hi, ones_mat) + d(mid, ones_mat) + d(lo, ones_mat)


def _kvf_fwd(h, g, wk, wv, wf, bf, k_ones, v_ones, spread, name):
    n, d = h.shape
    tm = _tile(n, 512)

    def body(h_ref, g_ref, wk_ref, wv_ref, wf_ref, bf_ref, ko_ref, vo_ref, sp_ref,
             k_ref, v_ref, fl_ref, cum_ref, rep_ref, carry):
        i = pl.program_id(0)

        @pl.when(i == 0)
        def _():
            carry[...] = jnp.zeros_like(carry)

        hb = _rms_fwd(h_ref[...], g_ref[...])[0].astype(BF16)
        k_ref[...] = (jnp.dot(hb, wk_ref[...], preferred_element_type=F32) + ko_ref[...]).astype(BF16)
        v_ref[...] = (jnp.dot(hb, wv_ref[...], preferred_element_type=F32) + vo_ref[...]).astype(BF16)
        fl = jnp.dot(hb, wf_ref[...], preferred_element_type=F32) + bf_ref[...]
        fl_ref[...] = fl
        logf = jnp.minimum(fl, 0.0) - jnp.log(1.0 + jnp.exp(-jnp.abs(fl)))
        rows = lax.broadcasted_iota(jnp.int32, (tm, tm), 0)
        cols = lax.broadcasted_iota(jnp.int32, (tm, tm), 1)
        lower = (rows >= cols).astype(BF16)
        cum = _exact_dot(lower, logf) + carry[0:1, :]
        cum_ref[...] = cum
        rep_ref[...] = _exact_dot_rhs(cum * LOG2E, sp_ref[...])
        carry[...] = jnp.broadcast_to(cum[tm - 1:tm, :], carry.shape)

    return _rows_call(body, name, tm, [h], [g, wk, wv, wf, bf, k_ones, v_ones, spread],
                      [(wk.shape[1], BF16), (wv.shape[1], BF16), (LANES, F32), (LANES, F32), (spread.shape[1], F32)],
                      scratch=[pltpu.VMEM((SUBLANES, LANES), F32)])


def _kvf_bwd(dh, h, dk1, dk2, dv1, dv2, dq1, dq2, fl, g, wk, wv, wf, sel_q, sel_k, name):
    n, d = h.shape
    tm = _tile(n, 256)
    nb = n // tm

    def body(dh_ref, h_ref, dk1_ref, dk2_ref, dv1_ref, dv2_ref, dq1_ref, dq2_ref, fl_ref,
             g_ref, wk_ref, wv_ref, wf_ref, sq_ref, sk_ref,
             o_ref, hk_ref, dk_ref, dv_ref, df_ref, dg_ref, db_ref, acc, bacc, carry):
        i = pl.program_id(0)

        @pl.when(i == 0)
        def _():
            acc[...] = jnp.zeros_like(acc)
            bacc[...] = jnp.zeros_like(bacc)
            carry[...] = jnp.zeros_like(carry)

        dkx = dk1_ref[...] + dk2_ref[...]
        dcum = _exact_dot_rhs(dq1_ref[...] + dq2_ref[...], sq_ref[...]) - _exact_dot_rhs(dkx, sk_ref[...])
        rows = lax.broadcasted_iota(jnp.int32, (tm, tm), 0)
        cols = lax.broadcasted_iota(jnp.int32, (tm, tm), 1)
        upper = (rows <= cols).astype(BF16)
        dlogf = _exact_dot(upper, dcum) + carry[0:1, :]
        carry[...] = jnp.broadcast_to(dlogf[0:1, :], carry.shape)
        df = dlogf / (1.0 + jnp.exp(fl_ref[...]))
        dfb = df.astype(BF16)
        df_ref[...] = dfb
        bacc[...] += _row_fold(df)
        dkb = (dkx * LN2).astype(BF16)
        dvb = (dv1_ref[...] + dv2_ref[...]).astype(BF16)
        dk_ref[...] = dkb
        dv_ref[...] = dvb
        gain = g_ref[...]
        hk, hhat, r = _rms_fwd(h_ref[...], gain)
        hk_ref[...] = hk.astype(BF16)
        dhk = lax.dot_general(dkb, wk_ref[...], NT_DIMS, preferred_element_type=F32)
        dhk = dhk + lax.dot_general(dvb, wv_ref[...], NT_DIMS, preferred_element_type=F32)
        dhk = dhk + lax.dot_general(dfb, wf_ref[...], NT_DIMS, preferred_element_type=F32)
        dhi, dgr = _rms_bwd(dhk, hhat, r, gain)
        o_ref[...] = dh_ref[...] + dhi
        acc[...] += _row_fold(dgr)

        @pl.when(i == nb - 1)
        def _():
            dg_ref[...] = jnp.sum(acc[...], axis=0, keepdims=True)
            db_ref[...] = jnp.sum(bacc[...], axis=0, keepdims=True)

    return _rows_call(body, name, tm, [dh, h, dk1, dk2, dv1, dv2, dq1, dq2, fl], [g, wk, wv, wf, sel_q, sel_k],
                      [(d, F32), (d, BF16), (wk.shape[1], BF16), (wv.shape[1], BF16), (LANES, BF16)],
                      [((1, d), F32), ((1, LANES), F32)],
                      [pltpu.VMEM((SUBLANES, d), F32), pltpu.VMEM((SUBLANES, LANES), F32),
                       pltpu.VMEM((SUBLANES, LANES), F32)], reverse=True)


def _loss_head(h, target, g, name):
    n, d = h.shape
    tm = _tile(n, 512)
    nb = n // tm

    def body(h_ref, t_ref, g_ref, dh_ref, loss_ref, dg_ref, lacc, gacc):
        i = pl.program_id(0)

        @pl.when(i == 0)
        def _():
            lacc[...] = jnp.zeros_like(lacc)
            gacc[...] = jnp.zeros_like(gacc)

        gain = g_ref[...]
        yv, hhat, r = _rms_fwd(h_ref[...], gain)
        e = yv - t_ref[...]
        lacc[...] += _row_fold(e * e)
        dhv, dgr = _rms_bwd(e * (1.0 / d), hhat, r, gain)
        dh_ref[...] = dhv
        gacc[...] += _row_fold(dgr)

        @pl.when(i == nb - 1)
        def _():
            loss_ref[...] = jnp.full((1, LANES), jnp.sum(lacc[...]) * (0.5 / d), F32)
            dg_ref[...] = jnp.sum(gacc[...], axis=0, keepdims=True)

    return _rows_call(body, name, tm, [h, target], [g], [(d, F32)], [((1, LANES), F32), ((1, d), F32)],
                      [pltpu.VMEM((SUBLANES, d), F32), pltpu.VMEM((SUBLANES, d), F32)])


def _matmul_tn(a, b, name, out_dtype=BF16):
    l, m = a.shape
    n = b.shape[1]
    tl = _tile(l, 1024)
    tmm = _tile(m, 512)
    tn = _tile(n, 1024)
    nl = l // tl

    def body(a_ref, b_ref, o_ref, acc):
        k = pl.program_id(2)

        @pl.when(k == 0)
        def _():
            acc[...] = jnp.zeros_like(acc)

        acc[...] += lax.dot_general(a_ref[...].astype(BF16), b_ref[...].astype(BF16), TN_DIMS,
                                    preferred_element_type=F32)

        @pl.when(k == nl - 1)
        def _():
            o_ref[...] = acc[...].astype(out_dtype)

    return pl.pallas_call(
        body, name=name, grid=(m // tmm, n // tn, nl),
        in_specs=[pl.BlockSpec((tl, tmm), lambda i, j, k: (k, i)), pl.BlockSpec((tl, tn), lambda i, j, k: (k, j))],
        out_specs=pl.BlockSpec((tmm, tn), lambda i, j, k: (i, j)),
        out_shape=jax.ShapeDtypeStruct((m, n), out_dtype),
        scratch_shapes=[pltpu.VMEM((tmm, tn), F32)],
        compiler_params=_cp("parallel", "parallel", "arbitrary"),
    )(a, b)


def _scan_fwd(u, bblk, cblk, tabs, dskip, name):
    l, d = u.shape
    nj, gb, n2 = bblk.shape
    n = n2 // 2
    tm = _tile(l, 512)
    nb = l // tm

    def body(u_ref, b_ref, c_ref, t_ref, d_ref, st_ref, y_ref, carry):
        i = pl.program_id(1)

        @pl.when(i == 0)
        def _():
            carry[...] = jnp.zeros_like(carry)

        uf = u_ref[...]
        st_ref[...] = jnp.dot(uf.astype(BF16), b_ref[0], preferred_element_type=F32)

        def step(rb, c):
            cr, ci = c
            rows = pl.ds(pl.multiple_of(rb * SUBLANES, SUBLANES), SUBLANES)
            xr = st_ref[rows, 0:n]
            xi = st_ref[rows, n:n2]
            for lvl, sh in enumerate((1, 2, 4)):
                ar = t_ref[0, 2 * lvl]
                ai = t_ref[0, 2 * lvl + 1]
                sr = pltpu.roll(xr, sh, 0)
                si = pltpu.roll(xi, sh, 0)
                xr, xi = xr + ar * sr - ai * si, xi + ar * si + ai * sr
            lr = t_ref[0, 6]
            li = t_ref[0, 7]
            xr, xi = xr + lr * cr - li * ci, xi + lr * ci + li * cr
            st_ref[rows, 0:n] = xr
            st_ref[rows, n:n2] = xi
            return (jnp.broadcast_to(xr[SUBLANES - 1:SUBLANES, :], (SUBLANES, n)),
                    jnp.broadcast_to(xi[SUBLANES - 1:SUBLANES, :], (SUBLANES, n)))

        cr, ci = lax.fori_loop(0, tm // SUBLANES, step, (carry[:, 0:n], carry[:, n:n2]))
        carry[:, 0:n] = cr
        carry[:, n:n2] = ci
        y_ref[...] = jnp.dot(st_ref[...].astype(BF16), c_ref[0], preferred_element_type=F32) + d_ref[...] * uf

    return pl.pallas_call(
        body, name=name, grid=(nj, nb),
        in_specs=[pl.BlockSpec((tm, gb), lambda j, i: (i, j)),
                  pl.BlockSpec((1, gb, n2), lambda j, i: (j, 0, 0)),
                  pl.BlockSpec((1, n2, gb), lambda j, i: (j, 0, 0)),
                  pl.BlockSpec((1, 8, SUBLANES, n), lambda j, i: (j, 0, 0, 0)),
                  pl.BlockSpec((1, gb), lambda j, i: (0, j))],
        out_specs=[pl.BlockSpec((tm, n2), lambda j, i: (i, j)), pl.BlockSpec((tm, gb), lambda j, i: (i, j))],
        out_shape=[jax.ShapeDtypeStruct((l, nj * n2), F32), jax.ShapeDtypeStruct((l, d), F32)],
        scratch_shapes=[pltpu.VMEM((SUBLANES, n2), F32)],
        compiler_params=_cp("parallel", "arbitrary"),
    )(u, bblk, cblk, tabs, dskip)


def _scan_bwd(dy, u, states, bblk_t, cblk_t, tabs, dskip, name):
    l, d = u.shape
    nj, n2, gb = bblk_t.shape
    n = n2 // 2
    tm = _tile(l, 512)
    nb = l // tm
    nr = tm // SUBLANES

    def body(dy_ref, u_ref, st_ref, prev_ref, bt_ref, ct_ref, t_ref, d_ref,
             du_ref, glam_ref, gd_ref, gb_ref, gc_ref, gx, carry):
        i = pl.program_id(1)
        ib = nb - 1 - i

        @pl.when(i == 0)
        def _():
            carry[...] = jnp.zeros_like(carry)
            glam_ref[...] = jnp.zeros_like(glam_ref)
            gd_ref[...] = jnp.zeros_like(gd_ref)
            gb_ref[...] = jnp.zeros_like(gb_ref)
            gc_ref[...] = jnp.zeros_like(gc_ref)

        dyv = dy_ref[...]
        uv = u_ref[...]
        dyb = dyv.astype(BF16)
        gx[...] = jnp.dot(dyb, ct_ref[0], preferred_element_type=F32)
        last_row = lax.broadcasted_iota(jnp.int32, (SUBLANES, n), 0) == SUBLANES - 1

        def block(rows, xp_r, xp_i, c):
            cr, ci = c
            gr = gx[rows, 0:n]
            gi = gx[rows, n:n2]
            for lvl, sh in enumerate((1, 2, 4)):
                ar = t_ref[0, 2 * lvl]
                ai = t_ref[0, 2 * lvl + 1]
                sr = pltpu.roll(gr, SUBLANES - sh, 0)
                si = pltpu.roll(gi, SUBLANES - sh, 0)
                gr, gi = gr + ar * sr - ai * si, gi + ar * si + ai * sr
            lr = t_ref[0, 6]
            li = t_ref[0, 7]
            gr, gi = gr + lr * cr - li * ci, gi + lr * ci + li * cr
            gx[rows, 0:n] = gr
            gx[rows, n:n2] = gi
            xs_r = pltpu.roll(jnp.where(last_row, xp_r, st_ref[rows, 0:n]), 1, 0)
            xs_i = pltpu.roll(jnp.where(last_row, xp_i, st_ref[rows, n:n2]), 1, 0)
            glam_ref[0, :, 0:n] += gr * xs_r + gi * xs_i
            glam_ref[0, :, n:n2] += gi * xs_r - gr * xs_i
            return (jnp.broadcast_to(gr[0:1, :], (SUBLANES, n)), jnp.broadcast_to(gi[0:1, :], (SUBLANES, n)))

        def step(k, c):
            rb = nr - 1 - k
            rows = pl.ds(pl.multiple_of(rb * SUBLANES, SUBLANES), SUBLANES)
            before = pl.ds(pl.multiple_of(rb * SUBLANES - SUBLANES, SUBLANES), SUBLANES)
            return block(rows, st_ref[before, 0:n], st_ref[before, n:n2], c)

        c = lax.fori_loop(0, nr - 1, step, (carry[:, 0:n], carry[:, n:n2]))
        live = (ib > 0).astype(F32)
        cr, ci = block(pl.ds(0, SUBLANES), prev_ref[:, 0:n] * live, prev_ref[:, n:n2] * live, c)
        carry[:, 0:n] = cr
        carry[:, n:n2] = ci

        gxb = gx[...].astype(BF16)
        du_ref[...] = jnp.dot(gxb, bt_ref[0], preferred_element_type=F32) + d_ref[...] * dyv
        gd_ref[0] += _row_fold(dyv * uv)
        gb_ref[0] += lax.dot_general(uv.astype(BF16), gxb, TN_DIMS, preferred_element_type=F32)
        gc_ref[0] += lax.dot_general(st_ref[...].astype(BF16), dyb, TN_DIMS, preferred_element_type=F32)

    rpb = tm // SUBLANES
    return pl.pallas_call(
        body, name=name, grid=(nj, nb),
        in_specs=[pl.BlockSpec((tm, gb), lambda j, i: (nb - 1 - i, j)),
                  pl.BlockSpec((tm, gb), lambda j, i: (nb - 1 - i, j)),
                  pl.BlockSpec((tm, n2), lambda j, i: (nb - 1 - i, j)),
                  pl.BlockSpec((SUBLANES, n2), lambda j, i: (jnp.maximum((nb - 1 - i) * rpb - 1, 0), j)),
                  pl.BlockSpec((1, n2, gb), lambda j, i: (j, 0, 0)),
                  pl.BlockSpec((1, gb, n2), lambda j, i: (j, 0, 0)),
                  pl.BlockSpec((1, 8, SUBLANES, n), lambda j, i: (j, 0, 0, 0)),
                  pl.BlockSpec((1, gb), lambda j, i: (0, j))],
        out_specs=[pl.BlockSpec((tm, gb), lambda j, i: (nb - 1 - i, j)),
                   pl.BlockSpec((1, SUBLANES, n2), lambda j, i: (j, 0, 0)),
                   pl.BlockSpec((1, SUBLANES, gb), lambda j, i: (j, 0, 0)),
                   pl.BlockSpec((1, gb, n2), lambda j, i: (j, 0, 0)),
                   pl.BlockSpec((1, n2, gb), lambda j, i: (j, 0, 0))],
        out_shape=[jax.ShapeDtypeStruct((l, d), F32),
                   jax.ShapeDtypeStruct((nj, SUBLANES, n2), F32),
                   jax.ShapeDtypeStruct((nj, SUBLANES, gb), F32),
                   jax.ShapeDtypeStruct((nj, gb, n2), F32),
                   jax.ShapeDtypeStruct((nj, n2, gb), F32)],
        scratch_shapes=[pltpu.VMEM((tm, n2), F32), pltpu.VMEM((SUBLANES, n2), F32)],
        compiler_params=_cp("parallel", "arbitrary"),
    )(dy, u, states, states, bblk_t, cblk_t, tabs, dskip)


HEAD_SLOT = 128
PAIR_SLOT = 2 * HEAD_SLOT
ATTN_TILE = 1024
LANE_ROWSUM_P = HEAD_DIM
LANE_COLSUM_DS = HEAD_DIM
LANE_ROWSUM_DS = HEAD_DIM + 1


def _slot_cols(w):
    r, c = w.shape
    nh = c // HEAD_DIM
    return jnp.pad(w.reshape(r, nh, HEAD_DIM), ((0, 0), (0, 0), (0, HEAD_SLOT - HEAD_DIM))).reshape(r, nh * HEAD_SLOT)


def _unslot_cols(w):
    r, c = w.shape
    nh = c // HEAD_SLOT
    return w.reshape(r, nh, HEAD_SLOT)[:, :, :HEAD_DIM].reshape(r, nh * HEAD_DIM)


def _slot_ones(nh, lane):
    return jnp.tile((jnp.arange(HEAD_SLOT) == lane).astype(F32), nh).reshape(1, nh * HEAD_SLOT)


def _causal_tiles(n, by_query):
    if by_query:
        tiles = [(i, j) for i in range(n) for j in range(i + 1)]
    else:
        tiles = [(i, j) for j in range(n) for i in range(j, n)]
    return (jnp.asarray([t[0] for t in tiles], jnp.int32), jnp.asarray([t[1] for t in tiles], jnp.int32))


def _flash_fwd(qx, kx, vx, f2_rows, name):
    l = qx.shape[0]
    npair = qx.shape[1] // PAIR_SLOT
    d = npair * HEAD_PAIR
    tq = _tile(l, ATTN_TILE)
    tk = tq
    i_of, j_of = _causal_tiles(l // tq, by_query=True)

    def body(i_ref, j_ref, q_ref, k_ref, v_ref, f_ref, o_ref, lse_ref, m_sc, acc_sc):
        t = pl.program_id(1)
        i = i_ref[t]
        j = j_ref[t]

        @pl.when(j == 0)
        def _():
            m_sc[...] = jnp.full_like(m_sc, NEG)
            acc_sc[...] = jnp.zeros_like(acc_sc)

        def tile(on_diagonal):
            for hh in range(2):
                hs = slice(hh * HEAD_SLOT, (hh + 1) * HEAD_SLOT)
                s = lax.dot_general(q_ref[:, hs], k_ref[:, hs], NT_DIMS, preferred_element_type=F32)
                s = s - f_ref[0, hh:hh + 1, :]
                if on_diagonal:
                    keep = (lax.broadcasted_iota(jnp.int32, (tq, tk), 0)
                            >= lax.broadcasted_iota(jnp.int32, (tq, tk), 1))
                    s = jnp.where(keep, s, NEG)
                m_prev = m_sc[hh]
                m_new = jnp.maximum(m_prev, jnp.max(s, axis=-1, keepdims=True))
                alpha = jnp.exp2(m_prev - m_new)
                p = jnp.exp2(s - jnp.concatenate([m_new] * (tk // LANES), axis=1)).astype(BF16)
                acc_sc[hh] = alpha * acc_sc[hh] + jnp.dot(p, v_ref[:, hs], preferred_element_type=F32)
                m_sc[hh] = m_new

        @pl.when(j < i)
        def _():
            tile(False)

        @pl.when(j == i)
        def _():
            tile(True)
            lane0 = (lax.broadcasted_iota(jnp.int32, (SUBLANES, LANES), 1) == 0).astype(BF16)
            outs, lses = [], []
            for hh in range(2):
                acc = acc_sc[hh]
                lsum = acc[:, LANE_ROWSUM_P:LANE_ROWSUM_P + 1]
                outs.append(acc[:, :HEAD_DIM] / lsum)
                lse_cols = m_sc[hh] + jnp.log2(lsum)
                lses.append(_exact_dot_nt(lane0, lse_cols)[0:1, :])
            o_ref[...] = jnp.concatenate(outs, axis=1).astype(BF16)
            lse_ref[0] = jnp.concatenate(lses, axis=0)

    q_map = lambda h, t, i_ref, j_ref: (i_ref[t], h)
    kv_map = lambda h, t, i_ref, j_ref: (j_ref[t], h)
    return pl.pallas_call(
        body, name=name,
        grid_spec=pltpu.PrefetchScalarGridSpec(
            num_scalar_prefetch=2, grid=(npair, i_of.shape[0]),
            in_specs=[pl.BlockSpec((tq, PAIR_SLOT), q_map), pl.BlockSpec((tk, PAIR_SLOT), kv_map),
                      pl.BlockSpec((tk, PAIR_SLOT), kv_map),
                      pl.BlockSpec((1, 2, tk), lambda h, t, i_ref, j_ref: (h, 0, j_ref[t]))],
            out_specs=[pl.BlockSpec((tq, HEAD_PAIR), q_map),
                       pl.BlockSpec((1, 2, tq), lambda h, t, i_ref, j_ref: (h, 0, i_ref[t]))],
            scratch_shapes=[pltpu.VMEM((2, tq, LANES), F32), pltpu.VMEM((2, tq, HEAD_SLOT), F32)]),
        out_shape=[jax.ShapeDtypeStruct((l, d), BF16), jax.ShapeDtypeStruct((npair, 2, l), F32)],
        compiler_params=_cp("parallel", "arbitrary"),
    )(i_of, j_of, qx, kx, vx, f2_rows)


def _flash_bwd(qx, kx, vx, f2_rep, do, lse_rows, delta_rows, name):
    l = qx.shape[0]
    npair = qx.shape[1] // PAIR_SLOT
    d = npair * HEAD_PAIR
    tq = _tile(l, ATTN_TILE)
    tk = tq
    i_of, j_of = _causal_tiles(l // tq, by_query=False)

    def body(i_ref, j_ref, q_ref, k_ref, v_ref, f_ref, do_ref, lse_ref, dl_ref, dq_ref, dk_ref, dv_ref):
        t = pl.program_id(1)
        i = i_ref[t]
        j = j_ref[t]

        @pl.when(t == 0)
        def _():
            dq_ref[...] = jnp.zeros_like(dq_ref)

        @pl.when(i == j)
        def _():
            dk_ref[...] = jnp.zeros_like(dk_ref)
            dv_ref[...] = jnp.zeros_like(dv_ref)

        def tile(on_diagonal):
            dqs, dks, dvs = [], [], []
            for hh in range(2):
                hs = slice(hh * HEAD_SLOT, (hh + 1) * HEAD_SLOT)
                qh, kh = q_ref[:, hs], k_ref[:, hs]
                vh = v_ref[:, hh * HEAD_SLOT:hh * HEAD_SLOT + HEAD_DIM]
                doh = do_ref[:, hh * HEAD_DIM:(hh + 1) * HEAD_DIM]
                st = lax.dot_general(kh, qh, NT_DIMS, preferred_element_type=F32)
                st = st - jnp.concatenate([f_ref[:, hs]] * (tq // HEAD_SLOT), axis=1)
                pt = jnp.exp2(st - lse_ref[0, hh:hh + 1, :])
                if on_diagonal:
                    keep = (lax.broadcasted_iota(jnp.int32, (tk, tq), 1)
                            >= lax.broadcasted_iota(jnp.int32, (tk, tq), 0))
                    pt = jnp.where(keep, pt, 0.0)
                dpt = lax.dot_general(vh, doh, NT_DIMS, preferred_element_type=F32)
                dsb = (pt * (dpt - dl_ref[0, hh:hh + 1, :])).astype(BF16)
                dvs.append(jnp.dot(pt.astype(BF16), doh, preferred_element_type=F32))
                dks.append(jnp.dot(dsb, qh, preferred_element_type=F32))
                dqs.append(lax.dot_general(dsb, kh, TN_DIMS, preferred_element_type=F32))
            dv_ref[...] += jnp.concatenate(dvs, axis=1)
            dk_ref[...] += jnp.concatenate(dks, axis=1)
            dq_ref[pl.ds(pl.multiple_of(i * tq, tq), tq), :] += jnp.concatenate(dqs, axis=1)

        @pl.when(i > j)
        def _():
            tile(False)

        @pl.when(i == j)
        def _():
            tile(True)

    qmap = lambda h, t, i_ref, j_ref: (i_ref[t], h)
    kmap = lambda h, t, i_ref, j_ref: (j_ref[t], h)
    row_map = lambda h, t, i_ref, j_ref: (h, 0, i_ref[t])
    return pl.pallas_call(
        body, name=name,
        grid_spec=pltpu.PrefetchScalarGridSpec(
            num_scalar_prefetch=2, grid=(npair, i_of.shape[0]),
            in_specs=[pl.BlockSpec((tq, PAIR_SLOT), qmap), pl.BlockSpec((tk, PAIR_SLOT), kmap),
                      pl.BlockSpec((tk, PAIR_SLOT), kmap), pl.BlockSpec((tk, PAIR_SLOT), kmap),
                      pl.BlockSpec((tq, HEAD_PAIR), qmap),
                      pl.BlockSpec((1, 2, tq), row_map), pl.BlockSpec((1, 2, tq), row_map)],
            out_specs=[pl.BlockSpec((l, PAIR_SLOT), lambda h, t, i_ref, j_ref: (0, h)),
                       pl.BlockSpec((tk, PAIR_SLOT), kmap), pl.BlockSpec((tk, HEAD_PAIR), kmap)]),
        out_shape=[jax.ShapeDtypeStruct((l, npair * PAIR_SLOT), F32), jax.ShapeDtypeStruct((l, npair * PAIR_SLOT), F32),
                   jax.ShapeDtypeStruct((l, d), F32)],
        compiler_params=_cp("parallel", "arbitrary"),
    )(i_of, j_of, qx, kx, vx, f2_rep, do, lse_rows, delta_rows)


def _my_place():
    return lax.axis_index("x"), lax.axis_index("y"), lax.axis_index("c")


def _chip_exchange(srcs, out_meta, plan, name):
    n_src, n_out, n_plan = len(srcs), len(out_meta), len(plan)

    def body(*refs):
        src_refs = refs[:n_src]
        out_refs = refs[n_src:n_src + n_out]
        send_sems, recv_sems, local_sems = refs[n_src + n_out:]
        x, y, c = _my_place()
        me = 2 * x + y
        copies = []
        for n, (si, oi, src_view, dst_view) in enumerate(plan):
            local = pltpu.make_async_copy(src_view(src_refs[si], me), dst_view(out_refs[oi], me), local_sems.at[n])
            local.start()
            copies.append(local)
            for k in (1, 2, 3):
                peer = me ^ k
                rc = pltpu.make_async_remote_copy(
                    src_ref=src_view(src_refs[si], peer), dst_ref=dst_view(out_refs[oi], me),
                    send_sem=send_sems.at[n, k - 1], recv_sem=recv_sems.at[n, k - 1],
                    device_id=(peer >> 1, peer & 1, c), device_id_type=MESH)
                rc.start()
                copies.append(rc)
        for cp in copies:
            cp.wait()

    any_spec = pl.BlockSpec(memory_space=pl.ANY)
    return pl.pallas_call(
        body, name=name,
        in_specs=[any_spec] * n_src, out_specs=[any_spec] * n_out,
        out_shape=[jax.ShapeDtypeStruct(shape, dt) for (shape, dt) in out_meta],
        scratch_shapes=[pltpu.SemaphoreType.DMA((n_plan, 3)), pltpu.SemaphoreType.DMA((n_plan, 3)),
                        pltpu.SemaphoreType.DMA((n_plan,))],
    )(*srcs)


def _plan_copies(src_refs, land_refs, plan, send_sems, recv_sems):
    x, y, c = _my_place()
    me = 2 * x + y
    copies = []
    for n, (si, oi, src_view, dst_view) in enumerate(plan):
        for k in (1, 2, 3):
            peer = me ^ k
            copies.append(pltpu.make_async_remote_copy(
                src_ref=src_view(src_refs[si], peer), dst_ref=dst_view(land_refs[oi], me),
                send_sem=send_sems.at[3 * n + k - 1], recv_sem=recv_sems.at[3 * n + k - 1],
                device_id=(peer >> 1, peer & 1, c), device_id_type=MESH))
    return copies


def _hbm(a):
    return pltpu.HBM(a.shape, a.dtype)


def _exchange_start(srcs, lands, plan, name):
    n_src, n_land = len(srcs), len(lands)
    n_buf = n_src + n_land

    def body(*refs):
        send_sems, recv_sems = refs[n_buf], refs[n_buf + 1]
        token = refs[-1]
        for cp in _plan_copies(refs[:n_src], refs[n_src:n_buf], plan, send_sems, recv_sems):
            cp.start()
        token[...] = jnp.zeros_like(token)

    bufs = [pltpu.with_memory_space_constraint(a, pltpu.HBM) for a in (*srcs, *lands)]
    hbm = pl.BlockSpec(memory_space=pltpu.HBM)
    sem = pl.BlockSpec(memory_space=pltpu.SEMAPHORE)
    res = pl.pallas_call(
        body, name=name,
        out_shape=(pltpu.SemaphoreType.DMA((3 * len(plan),)), pltpu.SemaphoreType.DMA((3 * len(plan),)),
                   *[_hbm(a) for a in bufs], jax.ShapeDtypeStruct((SUBLANES, LANES), F32)),
        in_specs=[hbm] * n_buf, out_specs=(sem, sem, *[hbm] * n_buf, pl.BlockSpec(memory_space=pltpu.VMEM)),
        input_output_aliases={n: 2 + n for n in range(n_buf)},
        compiler_params=pltpu.CompilerParams(has_side_effects=pltpu.SideEffectType.DATAFLOW_SIDE_EFFECTING),
    )(*bufs)
    return (res[0], res[1]), list(res[2:2 + n_src]), list(res[2 + n_src:2 + n_buf]), res[-1]


def _exchange_wait(sems, srcs, lands, plan, after, name):
    n_src, n_land = len(srcs), len(lands)
    n_buf = n_src + n_land

    def body(*refs):
        send_sems, recv_sems = refs[n_buf], refs[n_buf + 1]
        for cp in _plan_copies(refs[:n_src], refs[n_src:n_buf], plan, send_sems, recv_sems):
            cp.wait_send()
            cp.wait_recv()

    hbm = pl.BlockSpec(memory_space=pltpu.HBM)
    sem = pl.BlockSpec(memory_space=pltpu.SEMAPHORE)
    res = pl.pallas_call(
        body, name=name, out_shape=tuple(_hbm(a) for a in (*srcs, *lands)),
        in_specs=[hbm] * n_buf + [sem, sem, pl.BlockSpec(memory_space=pl.ANY)], out_specs=tuple([hbm] * n_buf),
        input_output_aliases={n: n for n in range(n_buf)},
        compiler_params=pltpu.CompilerParams(has_side_effects=pltpu.SideEffectType.DATAFLOW_SIDE_EFFECTING),
    )(*srcs, *lands, sems[0], sems[1], after)
    return list(res[n_src:])


def _core_exchange(arrays, name):
    n_items = len(arrays)

    def body(*refs):
        srcs = refs[:n_items]
        outs = refs[n_items:2 * n_items]
        send_sems, recv_sems = refs[2 * n_items:]
        x, y, c = _my_place()
        copies = []
        for n in range(n_items):
            rc = pltpu.make_async_remote_copy(
                src_ref=srcs[n], dst_ref=outs[n], send_sem=send_sems.at[n], recv_sem=recv_sems.at[n],
                device_id=(x, y, 1 - c), device_id_type=MESH)
            rc.start()
            copies.append(rc)
        for cp in copies:
            cp.wait()

    any_spec = pl.BlockSpec(memory_space=pl.ANY)
    return pl.pallas_call(
        body, name=name,
        in_specs=[any_spec] * n_items, out_specs=[any_spec] * n_items,
        out_shape=[jax.ShapeDtypeStruct(a.shape, a.dtype) for a in arrays],
        scratch_shapes=[pltpu.SemaphoreType.DMA((n_items,)), pltpu.SemaphoreType.DMA((n_items,))],
    )(*arrays)


def _sum_chips(parts, name):
    _, rows, cols = parts.shape
    tm = _tile(rows, 512)

    def body(p_ref, o_ref):
        acc = p_ref[0].astype(F32)
        for s in range(1, N_CHIPS):
            acc = acc + p_ref[s].astype(F32)
        o_ref[...] = acc

    return pl.pallas_call(
        body, name=name, grid=(rows // tm,),
        in_specs=[pl.BlockSpec((N_CHIPS, tm, cols), lambda i: (0, i, 0))],
        out_specs=pl.BlockSpec((tm, cols), lambda i: (i, 0)),
        out_shape=jax.ShapeDtypeStruct((rows, cols), F32),
        compiler_params=_cp("parallel"),
    )(parts)


def _adamw(ga, gb, w, m, v, name):
    rows, cols = w.shape
    tm = _tile(rows, 512)
    c1 = 1.0 - ADAM_B1 ** ADAM_STEP
    c2 = 1.0 - ADAM_B2 ** ADAM_STEP

    def body(ga_ref, gb_ref, w_ref, m_ref, v_ref, g_ref, d_ref, nm_ref, nv_ref):
        g = ga_ref[...] + gb_ref[...]
        nm = ADAM_B1 * m_ref[...] + (1.0 - ADAM_B1) * g
        nv = ADAM_B2 * v_ref[...] + (1.0 - ADAM_B2) * (g * g)
        g_ref[...] = g
        nm_ref[...] = nm
        nv_ref[...] = nv
        d_ref[...] = -ADAM_LR * ((nm / c1) / (jnp.sqrt(nv / c2) + ADAM_EPS) + ADAM_WD * w_ref[...])

    spec = pl.BlockSpec((tm, cols), lambda i: (i, 0))
    return pl.pallas_call(
        body, name=name, grid=(rows // tm,), in_specs=[spec] * 5, out_specs=[spec] * 4,
        out_shape=[jax.ShapeDtypeStruct((rows, cols), F32)] * 4, compiler_params=_cp("parallel"),
    )(ga, gb, w, m, v)


def _ssm_discretise(log_dt, a_re, a_im, b_re, b_im):
    dt = jnp.exp(log_dt)[:, None]
    mag = jnp.exp(a_re * dt)
    lbr = mag * jnp.cos(a_im * dt)
    lbi = mag * jnp.sin(a_im * dt)
    den = a_re * a_re + a_im * a_im
    nr, ni = lbr - 1.0, lbi
    qr = (nr * a_re + ni * a_im) / den
    qi = (ni * a_re - nr * a_im) / den
    bbr = qr[..., None] * b_re - qi[..., None] * b_im
    bbi = qr[..., None] * b_im + qi[..., None] * b_re
    return lbr, lbi, bbr, bbi


def _cmul(ar, ai, br, bi):
    return ar * br - ai * bi, ar * bi + ai * br


def _scan_tables(lr, li, nj, reverse):
    lr = lr.reshape(nj, 1, -1)
    li = li.reshape(nj, 1, -1)
    if reverse:
        li = -li
    pows = [(lr, li)]
    for _ in range(7):
        pows.append(_cmul(*pows[-1], lr, li))
    r = jnp.arange(SUBLANES).reshape(1, SUBLANES, 1)
    if reverse:
        r = SUBLANES - 1 - r
    out = []
    for k in (1, 2, 4):
        pr, pi = pows[k - 1]
        keep = (r >= k).astype(F32)
        out += [pr * keep, pi * keep]
    shape = (nj, SUBLANES, lr.shape[-1])
    cr = jnp.zeros(shape, F32)
    ci = jnp.zeros(shape, F32)
    for e in range(SUBLANES):
        sel = (r == e).astype(F32)
        cr = cr + sel * pows[e][0]
        ci = ci + sel * pows[e][1]
    out += [cr, ci]
    return jnp.stack(out, axis=1)


def _group_eye(gl):
    return jnp.eye(gl, dtype=F32)


def _block_diag_in(bbr, bbi, nj):
    g, p, c = bbr.shape
    gl = g // nj
    eye = _group_eye(gl)[None, :, None, :, None]

    def one(b):
        t = b.reshape(nj, gl, p, c).transpose(0, 1, 3, 2)[:, :, :, None, :]
        return (t * eye).reshape(nj, gl * c, gl * p)

    return jnp.concatenate([one(bbr), one(bbi)], axis=2)


def _block_diag_in_grad(gmat, nj, p, c):
    gl = gmat.shape[1] // c
    n = gl * p
    eye = _group_eye(gl)[None, :, None, :, None]

    def one(m):
        t = jnp.sum(m.reshape(nj, gl, c, gl, p) * eye, axis=3)
        return t.transpose(0, 1, 3, 2).reshape(nj * gl, p, c)

    return one(gmat[:, :, :n]), one(gmat[:, :, n:])


def _block_diag_out(c_re, c_im, nj):
    g, c, p = c_re.shape
    gl = g // nj
    eye = _group_eye(gl)[None, :, None, :, None]

    def one(m):
        t = m.reshape(nj, gl, c, p).transpose(0, 1, 3, 2)[:, :, :, None, :]
        return (t * eye).reshape(nj, gl * p, gl * c)

    return jnp.concatenate([one(c_re), -one(c_im)], axis=1)


def _block_diag_out_grad(gmat, nj, p, c):
    gl = gmat.shape[2] // c
    n = gl * p
    eye = _group_eye(gl)[None, :, None, :, None]

    def one(m):
        t = jnp.sum(m.reshape(nj, gl, p, gl, c) * eye, axis=3)
        return t.transpose(0, 1, 3, 2).reshape(nj * gl, c, p)

    return one(gmat[:, :n, :]), -one(gmat[:, n:, :])


def _pad_rows(flat, cols):
    per = SUBLANES * cols
    n = flat.shape[0]
    total = -(-n // per) * per
    return jnp.pad(flat, (0, total - n)).reshape(total // cols, cols)


def _pack_small(arrs, cols):
    packed = jnp.concatenate([_pad_rows(a.reshape(-1), cols) for a in arrs], axis=0)
    rows = packed.shape[0]
    return jnp.pad(packed, ((0, -rows % 128), (0, 0)))


def _unpack_small(packed, shapes, cols):
    out = []
    row = 0
    for s in shapes:
        n = math.prod(s)
        rows = -(-n // (SUBLANES * cols)) * SUBLANES
        out.append(packed[row:row + rows].reshape(-1)[:n].reshape(s))
        row += rows
    return out


def kernel(x, mix_norm, mlp_norm, mlp_w1, mlp_w2, ssm_log_dt, ssm_a_re, ssm_a_im, ssm_b_re, ssm_b_im, ssm_c_re, ssm_c_im, ssm_d, ssm_w_glu, kv_norm, w_kvf, b_f, attn_wq, attn_wo, final_norm, loss_target, m_mix_norm, m_mlp_norm, m_mlp_w1, m_mlp_w2, m_ssm_log_dt, m_ssm_a_re, m_ssm_a_im, m_ssm_b_re, m_ssm_b_im, m_ssm_c_re, m_ssm_c_im, m_ssm_d, m_ssm_w_glu, m_kv_norm, m_w_kvf, m_b_f, m_attn_wq, m_attn_wo, m_final_norm, v_mix_norm, v_mlp_norm, v_mlp_w1, v_mlp_w2, v_ssm_log_dt, v_ssm_a_re, v_ssm_a_im, v_ssm_b_re, v_ssm_b_im, v_ssm_c_re, v_ssm_c_im, v_ssm_d, v_ssm_w_glu, v_kv_norm, v_w_kvf, v_b_f, v_attn_wq, v_attn_wo, v_final_norm):
    seq, d = x.shape[1], x.shape[2]
    depth = mix_norm.shape[0]
    n_a = ssm_log_dt.shape[0]
    n_b = depth - n_a
    ff = mlp_w1.shape[2] * N_CHIPS
    n_heads = d // HEAD_DIM
    n_groups = d // SSM_GROUP
    p_state = ssm_a_re.shape[2]
    gb = min(d, 256)
    nj = d // gb
    kvf_cols = w_kvf.shape[1]
    ds4, dq4 = d // N_CHIPS, d // (2 * N_CHIPS)
    chip = 2 * lax.axis_index("x") + lax.axis_index("y")

    def cols_of(width):
        return lambda ref, s: ref.at[:, :, pl.ds(pl.multiple_of(s * width, LANES), width)]

    def rows_of(height):
        return lambda ref, s: ref.at[:, pl.ds(pl.multiple_of(s * height, SUBLANES), height), :]

    whole = lambda ref, s: ref
    slot = lambda ref, s: ref.at[s]
    def cols2(width):
        return lambda ref, s: ref.at[:, pl.ds(pl.multiple_of(s * width, LANES), width)]

    def rows2(height):
        return lambda ref, s: ref.at[pl.ds(pl.multiple_of(s * height, SUBLANES), height), :]

    assert n_a >= 2
    (skip_parts,) = _chip_exchange([ssm_d], [((N_CHIPS, n_a, ds4), F32)], [(0, 0, whole, slot)], "gather_skip")
    skip_all = skip_parts.transpose(1, 0, 2).reshape(n_a, d)
    w1_s, w2_s, glu_s = mlp_w1.astype(BF16), mlp_w2.astype(BF16), ssm_w_glu.astype(BF16)
    src_a = [w1_s[0], w2_s[0], glu_s[0]]
    plan_a = [(0, 0, whole, cols2(d)), (1, 1, whole, rows2(d)), (2, 2, whole, cols2(2 * ds4))]
    land_a = [lax.empty((d, ff), BF16), lax.empty((ff, d), BF16), lax.empty((d, 2 * d), BF16)]
    src_b = [w1_s[1:], w2_s[1:], glu_s[1:], w_kvf.astype(BF16), attn_wq.astype(BF16), attn_wo.astype(BF16)]
    plan_b = [(0, 0, whole, cols_of(d)), (1, 1, whole, rows_of(d)), (2, 2, whole, cols_of(2 * ds4)),
              (3, 3, whole, slot), (4, 4, whole, rows_of(ds4)), (5, 5, whole, rows_of(ds4))]
    land_b = [lax.empty((depth - 1, d, ff), BF16), lax.empty((depth - 1, ff, d), BF16),
              lax.empty((n_a - 1, d, 2 * d), BF16), lax.empty((N_CHIPS, d, kvf_cols), BF16),
              lax.empty((n_b, d, d), BF16), lax.empty((n_b, d, d), BF16)]
    def idx(*at):
        return [jnp.asarray(v, jnp.int32) for v in at]

    put = lax.dynamic_update_slice
    land_a = [put(land_a[0], src_a[0], idx(0, chip * d)), put(land_a[1], src_a[1], idx(chip * d, 0)),
              put(land_a[2], src_a[2], idx(0, chip * 2 * ds4))]
    land_b = [put(land_b[0], src_b[0], idx(0, 0, chip * d)), put(land_b[1], src_b[1], idx(0, chip * d, 0)),
              put(land_b[2], src_b[2], idx(0, 0, chip * 2 * ds4)), put(land_b[3], src_b[3][None], idx(chip, 0, 0)),
              put(land_b[4], src_b[4], idx(0, chip * ds4, 0)), put(land_b[5], src_b[5], idx(0, chip * ds4, 0))]
    sems_a, src_a, land_a, token_a = _exchange_start(src_a, land_a, plan_a, "gather_start_a")
    sems_b, src_b, land_b, token_b = _exchange_start(src_b, land_b, plan_b, "gather_start_b")
    started = token_a[0:1, 0:1] + token_b[0:1, 0:1]

    def layer_w1(i):
        return w1_0 if i == 0 else w1_rest[i - 1]

    def layer_w2(i):
        return w2_0 if i == 0 else w2_rest[i - 1]

    def layer_glu(i):
        return glu_0 if i == 0 else glu_rest[i - 1]

    h = x[0]
    target = loss_target[0]

    saved = []
    for i in range(n_a):
        lbr, lbi, bbr, bbi = _ssm_discretise(ssm_log_dt[i], ssm_a_re[i], ssm_a_im[i], ssm_b_re[i], ssm_b_im[i])
        bblk = _block_diag_in(bbr, bbi, nj)
        cblk = _block_diag_out(ssm_c_re[i], ssm_c_im[i], nj)
        rec = dict(h0=h, lam=(lbr, lbi), bblk=bblk, cblk=cblk)
        gain = mix_norm[i:i + 1] + started if i == 0 else mix_norm[i:i + 1]
        u = _norm_fwd(h, gain, f"s5_norm_{i}")
        rec["u"] = u
        dskip = rec["dskip"] = skip_all[i:i + 1]
        states, y = _scan_fwd(u, bblk.astype(BF16), cblk.astype(BF16), _scan_tables(lbr, lbi, nj, False), dskip,
                              f"s5_scan_{i}")
        rec["states"], rec["y"] = states, y
        if i == 0:
            w1_0, w2_0, glu_0 = _exchange_wait(sems_a, src_a, land_a, plan_a, y, "gather_wait_a")
        if i == 1:
            w1_rest, w2_rest, glu_rest, kvf_parts, wq_all, wo_all = _exchange_wait(
                sems_b, src_b, land_b, plan_b, y, "gather_wait_b")
        h, rec["zw"] = _s5_post_fwd(h, y, layer_glu(i), f"s5_glu_{i}")
        rec["h1"] = h
        h, rec["ap"] = _mlp_fwd(h, mlp_norm[i:i + 1], layer_w1(i), layer_w2(i), f"mlp_{i}")
        saved.append(rec)
    h_kv = h
    kvf_all = jnp.concatenate([kvf_parts[s] for s in range(N_CHIPS)], axis=1)
    wk = kvf_all[:, :d]
    wv = kvf_all[:, d:2 * d]
    wf = jnp.pad(kvf_all[:, 2 * d:], ((0, 0), (0, LANES - n_heads)))
    bf_row = jnp.pad(b_f, (0, LANES - n_heads)).reshape(1, LANES)
    wk_x = _slot_cols(wk)
    spread = (jnp.arange(LANES)[:, None] == jnp.arange(n_heads * HEAD_SLOT)[None, :] // HEAD_SLOT).astype(BF16)
    kx, vx, flog, cum, f2_rep = _kvf_fwd(h, kv_norm.reshape(1, d), wk_x, _slot_cols(wv), wf, bf_row,
                                         _slot_ones(n_heads, LANE_ROWSUM_DS), _slot_ones(n_heads, LANE_ROWSUM_P),
                                         spread, "kvf")
    f2_rows = (cum[:, :n_heads] * LOG2E).T.reshape(n_heads // 2, 2, seq)
    wq_x = [_slot_cols(wq_all[jb]) for jb in range(n_b)]
    for jb in range(n_b):
        i = n_a + jb
        rec = dict(h0=h)
        qx = _q_fwd(h, mix_norm[i:i + 1], wq_x[jb], _slot_ones(n_heads, LANE_COLSUM_DS), f"attn_q_{jb}")
        o, lse = _flash_fwd(qx, kx, vx, f2_rows, f"attn_core_{jb}")
        rec["qx"], rec["o"], rec["lse"] = qx, o, lse
        h = _o_fwd(h, o, wo_all[jb], f"attn_out_{jb}")
        rec["h1"] = h
        h, rec["ap"] = _mlp_fwd(h, mlp_norm[i:i + 1], layer_w1(i), layer_w2(i), f"mlp_{i}")
        saved.append(rec)
    dh, loss_row, g_final = _loss_head(h, target, final_norm.reshape(1, d), "loss_head")
    loss = lax.psum(loss_row[0, 0], ("x", "y", "c"))

    head_sel = (jnp.arange(n_heads)[:, None] == jnp.arange(d)[None, :] // HEAD_DIM).astype(BF16)
    slot_lane = jnp.arange(n_heads * HEAD_SLOT)[:, None]
    in_lane = jnp.arange(LANES)[None, :]
    sel_q = (slot_lane == in_lane * HEAD_SLOT + LANE_ROWSUM_DS).astype(BF16)
    sel_k = (slot_lane == in_lane * HEAD_SLOT + LANE_COLSUM_DS).astype(BF16)
    g_mix = [None] * depth
    g_mlp = [None] * depth
    g_w1 = [None] * depth
    g_w2 = [None] * depth
    g_wq = [None] * n_b
    g_wo = [None] * n_b
    g_glu = [None] * n_a
    g_ssm = [None] * n_a
    dk_parts, dv_parts, dq_parts = [], [], []

    red_waits = []

    def reduce_start(entries, name):
        numbers = sorted({e[1] for e in entries})
        lands = {}
        for (_, ln, shape, dt, _, _, _, _) in entries:
            if ln not in lands:
                lands[ln] = lax.empty(shape, dt)
        for (_, ln, _, _, _, _, own, at) in entries:
            lands[ln] = lax.dynamic_update_slice(lands[ln], own, idx(*at))
        plan = [(n, numbers.index(e[1]), e[4], e[5]) for n, e in enumerate(entries)]
        sems, srcs, lands_t, token = _exchange_start([e[0] for e in entries], [lands[ln] for ln in numbers], plan,
                                                     name + "_start")
        red_waits.append((sems, srcs, lands_t, plan, name + "_wait"))
        return token[0:1, 0:1]

    def w1_entry(i, ln, local, n_layers):
        own = lax.dynamic_slice(g_w1[i], idx(0, chip * d), (d, d))[None, None]
        return (g_w1[i], ln, (N_CHIPS, n_layers, d, d), BF16, cols2(d), into(local), own, (chip, local, 0, 0))

    def w2_entry(i, ln, local, n_layers):
        own = lax.dynamic_slice(g_w2[i], idx(chip * d, 0), (d, d))[None, None]
        return (g_w2[i], ln, (N_CHIPS, n_layers, d, d), BF16, rows2(d), into(local), own, (chip, local, 0, 0))

    def glu_entry(i, ln, local, n_layers):
        own = lax.dynamic_slice(g_glu[i], idx(0, chip * 2 * ds4), (d, 2 * ds4))[None, None]
        return (g_glu[i], ln, (N_CHIPS, n_layers, d, 2 * ds4), BF16, cols2(2 * ds4), into(local), own,
                (chip, local, 0, 0))

    def rows_entry(g, ln, local):
        own = lax.dynamic_slice(g, idx(chip * ds4, 0), (ds4, d))[None, None]
        return (g, ln, (N_CHIPS, n_b, ds4, d), BF16, rows2(ds4), into(local), own, (chip, local, 0, 0))

    def into(layer):
        return lambda ref, s: ref.at[s, layer]

    def mlp_back(dh, i, rec, tie=None):
        gain = mlp_norm[i:i + 1] if tie is None else mlp_norm[i:i + 1] + tie
        dh_in, hm, a, dap, g_mlp[i] = _mlp_bwd(dh, rec["h1"], rec["ap"], gain, layer_w1(i),
                                               layer_w2(i), f"mlp_bwd_{i}")
        g_w2[i] = _matmul_tn(a, dh, f"mlp_dw2_{i}")
        g_w1[i] = _matmul_tn(hm, dap, f"mlp_dw1_{i}")
        return dh_in

    for jb in reversed(range(n_b)):
        i = n_a + jb
        rec = saved[i]
        dh = mlp_back(dh, i, rec)
        do, delta = _o_bwd(dh, rec["o"], wo_all[jb], head_sel, f"attn_out_bwd_{jb}")
        g_wo[jb] = _matmul_tn(rec["o"], dh, f"attn_dwo_{jb}")
        dqx, dkx, dv = _flash_bwd(rec["qx"], kx, vx, f2_rep, do, rec["lse"], delta.reshape(n_heads // 2, 2, seq),
                                  f"attn_core_bwd_{jb}")
        dk_parts.append(dkx)
        dv_parts.append(dv)
        dq_parts.append(dqx)
        dh, hn, dqs, g_mix[i] = _q_bwd(dh, rec["h0"], dqx, mix_norm[i:i + 1], wq_x[jb], f"attn_q_bwd_{jb}")
        g_wq[jb] = _unslot_cols(_matmul_tn(hn, dqs, f"attn_dwq_{jb}"))

    dh, hk, dkb, dvb, dfb, g_kvn, g_bf = _kvf_bwd(dh, h_kv, dk_parts[0], dk_parts[1], dv_parts[0], dv_parts[1],
                                                  dq_parts[0], dq_parts[1], flog, kv_norm.reshape(1, d), wk_x, wv, wf,
                                                  sel_q, sel_k, "kvf_bwd")
    g_kvf = jnp.concatenate([_unslot_cols(_matmul_tn(hk, dkb, "kvf_dwk")), _matmul_tn(hk, dvb, "kvf_dwv"),
                             _matmul_tn(hk, dfb, "kvf_dwf")[:, :n_heads]], axis=1)
    kvf_send = g_kvf.reshape(d, N_CHIPS, kvf_cols).transpose(1, 0, 2)
    group = [w1_entry(n_a + jb, 0, jb, n_b) for jb in range(n_b)]
    group += [w2_entry(n_a + jb, 1, jb, n_b) for jb in range(n_b)]
    group.append((kvf_send, 2, (N_CHIPS, d, kvf_cols), BF16, slot, slot,
                  lax.dynamic_index_in_dim(kvf_send, chip, 0, keepdims=True), (chip, 0, 0)))
    group += [rows_entry(g_wq[jb], 3, jb) for jb in range(n_b)]
    group += [rows_entry(g_wo[jb], 4, jb) for jb in range(n_b)]
    tie = reduce_start(group, "reduce_attn")

    for i in reversed(range(n_a)):
        rec = saved[i]
        if i == 0:
            group = [w1_entry(l, 0, l - 1, n_a - 1) for l in range(1, n_a)]
            group += [w2_entry(l, 1, l - 1, n_a - 1) for l in range(1, n_a)]
            group += [glu_entry(l, 2, l - 1, n_a - 1) for l in range(1, n_a)]
            tie = reduce_start(group, "reduce_s5")
        dh = mlp_back(dh, i, rec, tie if i in (0, n_a - 1) else None)
        dy, z, dzw = _s5_post_bwd(dh, rec["y"], rec["zw"], layer_glu(i), f"s5_glu_bwd_{i}")
        g_glu[i] = _matmul_tn(z, dzw, f"s5_dwglu_{i}")
        skip_gain = rec["dskip"]
        if i == 0:
            skip_gain = skip_gain + reduce_start([w1_entry(0, 0, 0, 1), w2_entry(0, 1, 0, 1), glu_entry(0, 2, 0, 1)],
                                                 "reduce_first")
        lbr, lbi = rec["lam"]
        bblk_t = rec["bblk"].transpose(0, 2, 1).astype(BF16)
        cblk_t = rec["cblk"].transpose(0, 2, 1).astype(BF16)
        du, glam8, gd8, gbblk, gcblk = _scan_bwd(dy, rec["u"], rec["states"], bblk_t, cblk_t,
                                                 _scan_tables(lbr, lbi, nj, True), skip_gain, f"s5_scan_bwd_{i}")
        dh, g_mix[i] = _norm_bwd_add(dh, du, rec["h0"], mix_norm[i:i + 1], f"s5_norm_bwd_{i}")
        glam = jnp.sum(glam8, axis=1)
        n_st = glam.shape[1] // 2
        g_lbr = glam[:, :n_st].reshape(n_groups, p_state)
        g_lbi = glam[:, n_st:].reshape(n_groups, p_state)
        g_bbr, g_bbi = _block_diag_in_grad(gbblk, nj, p_state, SSM_GROUP)
        g_cre, g_cim = _block_diag_out_grad(gcblk, nj, p_state, SSM_GROUP)
        _, pull = jax.vjp(_ssm_discretise, ssm_log_dt[i], ssm_a_re[i], ssm_a_im[i], ssm_b_re[i], ssm_b_im[i])
        g_ldt, g_are, g_aim, g_bre, g_bim = pull((g_lbr, g_lbi, g_bbr, g_bbi))
        g_ssm[i] = dict(log_dt=g_ldt, a_re=g_are, a_im=g_aim, b_re=g_bre, b_im=g_bim, c_re=g_cre, c_im=g_cim,
                        d=jnp.sum(gd8, axis=1).reshape(d))
    grad_x = dh[None]

    def stack_small(key):
        return jnp.stack([g_ssm[i][key] for i in range(n_a)])

    small_grads = [jnp.concatenate(g_mix, axis=0), jnp.concatenate(g_mlp, axis=0), stack_small("log_dt"),
                   stack_small("a_re"), stack_small("a_im"), stack_small("b_re"), stack_small("b_im"),
                   stack_small("c_re"), stack_small("c_im"), stack_small("d"), g_kvn.reshape(d),
                   g_bf[0, :n_heads], g_final.reshape(d)]
    small_w = [mix_norm, mlp_norm, ssm_log_dt, ssm_a_re, ssm_a_im, ssm_b_re, ssm_b_im, ssm_c_re, ssm_c_im,
               ssm_d, kv_norm, b_f, final_norm]
    small_m = [m_mix_norm, m_mlp_norm, m_ssm_log_dt, m_ssm_a_re, m_ssm_a_im, m_ssm_b_re, m_ssm_b_im, m_ssm_c_re,
               m_ssm_c_im, m_ssm_d, m_kv_norm, m_b_f, m_final_norm]
    small_v = [v_mix_norm, v_mlp_norm, v_ssm_log_dt, v_ssm_a_re, v_ssm_a_im, v_ssm_b_re, v_ssm_b_im, v_ssm_c_re,
               v_ssm_c_im, v_ssm_d, v_kv_norm, v_b_f, v_final_norm]
    skip_at = 9

    def widen_skip(part):
        return lax.dynamic_update_slice(jnp.zeros((n_a, d), F32), part, (0, chip * ds4))

    small_shapes = [a.shape for a in small_grads]
    pcols = 1024 if d >= 1024 else LANES
    g_small = _pack_small(small_grads, pcols)
    expand = lambda lst: _pack_small([widen_skip(a) if n == skip_at else a for n, a in enumerate(lst)], pcols)
    w_small, m_small, v_small = expand(small_w), expand(small_m), expand(small_v)
    srows = g_small.shape[0]

    assert n_b == 2
    reduce_start([(g_small, 0, (N_CHIPS, srows, pcols), F32, whole, slot, g_small[None], (chip, 0, 0))], "reduce_small")
    landed = [_exchange_wait(sems, srcs, lands, plan, dh, name) for (sems, srcs, lands, plan, name) in red_waits]
    (a_w1, a_w2, a_kvf, a_wq, a_wo), (s_w1, s_w2, s_glu), (f_w1, f_w2, f_glu), (r_small,) = landed

    def chip_sum(r, name):
        return _sum_chips(r.reshape(N_CHIPS, -1, r.shape[-1]), name)

    sums = [jnp.concatenate([chip_sum(f_w1, "sum_w1_first"), chip_sum(s_w1, "sum_w1_s5"),
                             chip_sum(a_w1, "sum_w1_attn")], axis=0),
            jnp.concatenate([chip_sum(f_w2, "sum_w2_first"), chip_sum(s_w2, "sum_w2_s5"),
                             chip_sum(a_w2, "sum_w2_attn")], axis=0),
            jnp.concatenate([chip_sum(f_glu, "sum_glu_first"), chip_sum(s_glu, "sum_glu_s5")], axis=0),
            chip_sum(a_kvf, "sum_kvf"), chip_sum(a_wq, "sum_wq"), chip_sum(a_wo, "sum_wo"),
            chip_sum(r_small, "sum_small")]
    others = _core_exchange(sums, "reduce_cores")

    def two(a):
        return a.reshape(-1, a.shape[-1])

    big_w = [(mlp_w1, m_mlp_w1, v_mlp_w1), (mlp_w2, m_mlp_w2, v_mlp_w2), (ssm_w_glu, m_ssm_w_glu, v_ssm_w_glu),
             (w_kvf, m_w_kvf, v_w_kvf), (attn_wq, m_attn_wq, v_attn_wq), (attn_wo, m_attn_wo, v_attn_wo)]
    big_out = []
    for n, (w, m, v) in enumerate(big_w):
        res = _adamw(sums[n], others[n], two(w), two(m), two(v), f"adamw_{n}")
        big_out.append([r.reshape(w.shape) for r in res])
    small_out = _adamw(sums[6], others[6], w_small, m_small, v_small, "adamw_small")

    def narrow_skip(a):
        return lax.dynamic_slice(a, (0, chip * ds4), (n_a, ds4))

    unpacked = []
    for packed in small_out:
        parts = _unpack_small(packed, small_shapes, pcols)
        parts[skip_at] = narrow_skip(parts[skip_at])
        unpacked.append(parts)

    order = ["mix_norm", "mlp_norm", "mlp_w1", "mlp_w2", "ssm_log_dt", "ssm_a_re", "ssm_a_im", "ssm_b_re",
             "ssm_b_im", "ssm_c_re", "ssm_c_im", "ssm_d", "ssm_w_glu", "kv_norm", "w_kvf", "b_f", "attn_wq",
             "attn_wo", "final_norm"]
    small_names = ["mix_norm", "mlp_norm", "ssm_log_dt", "ssm_a_re", "ssm_a_im", "ssm_b_re", "ssm_b_im",
                   "ssm_c_re", "ssm_c_im", "ssm_d", "kv_norm", "b_f", "final_norm"]
    big_names = ["mlp_w1", "mlp_w2", "ssm_w_glu", "w_kvf", "attn_wq", "attn_wo"]
    outs = [loss, grad_x]
    for kind in range(4):
        for name in order:
            if name in big_names:
                outs.append(big_out[big_names.index(name)][kind])
            else:
                outs.append(unpacked[kind][small_names.index(name)])
    return tuple(outs)
```

```python
import functools
import math

import jax
import jax.numpy as jnp
from jax import lax
from jax.experimental import pallas as pl
from jax.experimental.pallas import tpu as pltpu

F32 = jnp.float32
BF16 = jnp.bfloat16

RMS_EPS = 1e-6
SSM_GROUP = 16
SSM_STATE = 64
HEAD_DIM = 64
HEAD_PAIR = 2 * HEAD_DIM
LANES = 128
SUBLANES = 8
N_CHIPS = 4
ADAM_LR = 0.001
ADAM_B1 = 0.9
ADAM_B2 = 0.999
ADAM_EPS = 1e-08
ADAM_WD = 0.01
ADAM_STEP = 10
GELU_C = math.sqrt(2.0 / math.pi)
GELU_A = 0.044715
NEG = -1e30
LN2 = math.log(2.0)
LOG2E = 1.0 / LN2
VMEM_LIMIT = 56 * 1024 * 1024
MESH = pl.DeviceIdType.MESH

NT_DIMS = (((1,), (1,)), ((), ()))
TN_DIMS = (((0,), (0,)), ((), ()))


def _cp(*sem):
    return pltpu.CompilerParams(dimension_semantics=sem if sem else None, vmem_limit_bytes=VMEM_LIMIT)


def _zero_idx(nd, *_):
    return (0,) * nd


def _tile(n, t):
    if n <= t:
        return n
    for cand in range(t - t % SUBLANES, 0, -SUBLANES):
        if n % cand == 0:
            return cand
    raise ValueError((n, t))


def _rms_fwd(h, g):
    r = lax.rsqrt(jnp.mean(h * h, axis=-1, keepdims=True) + RMS_EPS)
    hhat = h * r
    return hhat * g, hhat, r


def _rms_bwd(du, hhat, r, g):
    dhh = du * g
    dh = r * (dhh - hhat * jnp.mean(dhh * hhat, axis=-1, keepdims=True))
    return dh, du * hhat


def _sigmoid(x):
    return 1.0 / (1.0 + jnp.exp(-x))


def _gelu(x):
    t = jnp.tanh(GELU_C * (x + GELU_A * x * x * x))
    return 0.5 * x * (1.0 + t)


def _gelu_grad(x):
    t = jnp.tanh(GELU_C * (x + GELU_A * x * x * x))
    return 0.5 * (1.0 + t) + 0.5 * x * (1.0 - t * t) * GELU_C * (1.0 + 3.0 * GELU_A * x * x)


def _row_fold(x):
    tm, w = x.shape
    return jnp.sum(x.reshape(tm // SUBLANES, SUBLANES, w), axis=0)


def _split3(x):
    hi = x.astype(BF16)
    r1 = x - hi.astype(F32)
    mid = r1.astype(BF16)
    lo = (r1 - mid.astype(F32)).astype(BF16)
    return hi, mid, lo


def _exact_dot(ones_mat, x):
    hi, mid, lo = _split3(x)
    d = functools.partial(jnp.dot, preferred_element_type=F32)
    return d(ones_mat, hi) + d(ones_mat, mid) + d(ones_mat, lo)


def _rows_call(body, name, tm, row_ins, const_ins, row_outs, acc_outs=(), scratch=(), reverse=False, col_outs=()):
    n = row_ins[0].shape[0]
    nb = n // tm
    if reverse:
        ridx = lambda i: (nb - 1 - i, 0)
        cidx = lambda i: (0, nb - 1 - i)
    else:
        ridx = lambda i: (i, 0)
        cidx = lambda i: (0, i)
    in_specs = [pl.BlockSpec((tm, a.shape[1]), ridx) for a in row_ins]
    in_specs += [pl.BlockSpec(a.shape, functools.partial(_zero_idx, a.ndim), pipeline_mode=pl.Buffered(1))
                 for a in const_ins]
    out_shape = [jax.ShapeDtypeStruct((n, w), dt) for (w, dt) in row_outs]
    out_shape += [jax.ShapeDtypeStruct(s, dt) for (s, dt) in acc_outs]
    out_shape += [jax.ShapeDtypeStruct((r, n), dt) for (r, dt) in col_outs]
    out_specs = [pl.BlockSpec((tm, w), ridx) for (w, dt) in row_outs]
    out_specs += [pl.BlockSpec(s, functools.partial(_zero_idx, len(s))) for (s, dt) in acc_outs]
    out_specs += [pl.BlockSpec((r, tm), cidx) for (r, dt) in col_outs]
    return pl.pallas_call(
        body, name=name, grid=(nb,), in_specs=in_specs, out_specs=out_specs, out_shape=out_shape,
        scratch_shapes=list(scratch), compiler_params=_cp("arbitrary"),
    )(*row_ins, *const_ins)


def _norm_fwd(h, g, name):
    n, d = h.shape
    tm = _tile(n, 512)

    def body(h_ref, g_ref, u_ref):
        u_ref[...] = _rms_fwd(h_ref[...], g_ref[...])[0]

    return _rows_call(body, name, tm, [h], [g], [(d, F32)])[0]


def _norm_bwd_add(dh, du, h, g, name):
    n, d = h.shape
    tm = _tile(n, 512)
    nb = n // tm

    def body(dh_ref, du_ref, h_ref, g_ref, o_ref, dg_ref, acc):
        i = pl.program_id(0)

        @pl.when(i == 0)
        def _():
            acc[...] = jnp.zeros_like(acc)

        gain = g_ref[...]
        _, hhat, r = _rms_fwd(h_ref[...], gain)
        dhn, dgr = _rms_bwd(du_ref[...], hhat, r, gain)
        o_ref[...] = dh_ref[...] + dhn
        acc[...] += _row_fold(dgr)

        @pl.when(i == nb - 1)
        def _():
            dg_ref[...] = jnp.sum(acc[...], axis=0, keepdims=True)

    return _rows_call(body, name, tm, [dh, du, h], [g], [(d, F32)], [((1, d), F32)],
                      [pltpu.VMEM((SUBLANES, d), F32)])


def _mlp_fwd(h, g, w1, w2, name):
    n, d = h.shape
    ff = w1.shape[1]
    tm = _tile(n, 256)
    fc = _tile(ff, 1024)

    def body(h_ref, g_ref, w1_ref, w2_ref, o_ref, ap_ref):
        hin = h_ref[...]
        hb = _rms_fwd(hin, g_ref[...])[0].astype(BF16)
        acc = hin
        for c in range(ff // fc):
            cs = slice(c * fc, (c + 1) * fc)
            ap = jnp.dot(hb, w1_ref[:, cs], preferred_element_type=F32)
            ap_ref[:, cs] = ap.astype(BF16)
            rl = jnp.maximum(ap, 0.0)
            acc = acc + jnp.dot((rl * rl).astype(BF16), w2_ref[cs, :], preferred_element_type=F32)
        o_ref[...] = acc

    return _rows_call(body, name, tm, [h], [g, w1, w2], [(d, F32), (ff, BF16)])


def _mlp_bwd(dh, h, ap, g, w1, w2, name):
    n, d = h.shape
    ff = w1.shape[1]
    tm = _tile(n, 256)
    nb = n // tm
    fc = _tile(ff, 1024)

    def body(dh_ref, h_ref, ap_ref, g_ref, w1_ref, w2_ref, o_ref, hm_ref, a_ref, dap_ref, dg_ref, acc):
        i = pl.program_id(0)

        @pl.when(i == 0)
        def _():
            acc[...] = jnp.zeros_like(acc)

        gain = g_ref[...]
        dhv = dh_ref[...]
        hm, hhat, r = _rms_fwd(h_ref[...], gain)
        hm_ref[...] = hm.astype(BF16)
        dhb = dhv.astype(BF16)
        dhm = jnp.zeros((tm, d), F32)
        for c in range(ff // fc):
            cs = slice(c * fc, (c + 1) * fc)
            rl = jnp.maximum(ap_ref[:, cs].astype(F32), 0.0)
            a_ref[:, cs] = (rl * rl).astype(BF16)
            da = lax.dot_general(dhb, w2_ref[cs, :], NT_DIMS, preferred_element_type=F32)
            dap = (da * (2.0 * rl)).astype(BF16)
            dap_ref[:, cs] = dap
            dhm = dhm + lax.dot_general(dap, w1_ref[:, cs], NT_DIMS, preferred_element_type=F32)
        dhn, dgr = _rms_bwd(dhm, hhat, r, gain)
        o_ref[...] = dhv + dhn
        acc[...] += _row_fold(dgr)

        @pl.when(i == nb - 1)
        def _():
            dg_ref[...] = jnp.sum(acc[...], axis=0, keepdims=True)

    return _rows_call(body, name, tm, [dh, h, ap], [g, w1, w2],
                      [(d, F32), (d, BF16), (ff, BF16), (ff, BF16)], [((1, d), F32)],
                      [pltpu.VMEM((SUBLANES, d), F32)])


def _s5_post_fwd(h, y, w_glu, name):
    n, d = h.shape
    tm = _tile(n, 512)

    def body(h_ref, y_ref, w_ref, o_ref, zw_ref):
        z = _gelu(y_ref[...]).astype(BF16)
        zw = jnp.dot(z, w_ref[...], preferred_element_type=F32)
        zw_ref[...] = zw.astype(BF16)
        o_ref[...] = h_ref[...] + zw[:, :d] * _sigmoid(zw[:, d:])

    return _rows_call(body, name, tm, [h, y], [w_glu], [(d, F32), (2 * d, BF16)])


def _s5_post_bwd(dh, y, zw, w_glu, name):
    n, d = dh.shape
    tm = _tile(n, 512)

    def body(dh_ref, y_ref, zw_ref, w_ref, dy_ref, z_ref, dzw_ref):
        dhv = dh_ref[...]
        yv = y_ref[...]
        val = zw_ref[:, :d].astype(F32)
        sg = _sigmoid(zw_ref[:, d:].astype(F32))
        dzw = jnp.concatenate([dhv * sg, dhv * val * sg * (1.0 - sg)], axis=1).astype(BF16)
        dzw_ref[...] = dzw
        dz = lax.dot_general(dzw, w_ref[...], NT_DIMS, preferred_element_type=F32)
        dy_ref[...] = dz * _gelu_grad(yv)
        z_ref[...] = _gelu(yv).astype(BF16)

    return _rows_call(body, name, tm, [dh, y, zw], [w_glu], [(d, F32), (d, BF16), (2 * d, BF16)])


def _q_fwd(h, g, wq_x, ones_x, name):
    n, d = h.shape
    tm = _tile(n, 512)
    scale = LOG2E * HEAD_DIM ** -0.5

    def body(h_ref, g_ref, w_ref, one_ref, q_ref):
        hb = _rms_fwd(h_ref[...], g_ref[...])[0].astype(BF16)
        q_ref[...] = (jnp.dot(hb, w_ref[...], preferred_element_type=F32) * scale + one_ref[...]).astype(BF16)

    return _rows_call(body, name, tm, [h], [g, wq_x, ones_x], [(wq_x.shape[1], BF16)])[0]


def _q_bwd(dh, h, dq, g, wq, name):
    n, d = h.shape
    tm = _tile(n, 512)
    nb = n // tm
    scale = HEAD_DIM ** -0.5

    def body(dh_ref, h_ref, dq_ref, g_ref, w_ref, o_ref, hn_ref, dqs_ref, dg_ref, acc):
        i = pl.program_id(0)

        @pl.when(i == 0)
        def _():
            acc[...] = jnp.zeros_like(acc)

        gain = g_ref[...]
        hn, hhat, r = _rms_fwd(h_ref[...], gain)
        hn_ref[...] = hn.astype(BF16)
        dqs = (dq_ref[...] * scale).astype(BF16)
        dqs_ref[...] = dqs
        dhn = lax.dot_general(dqs, w_ref[...], NT_DIMS, preferred_element_type=F32)
        dhi, dgr = _rms_bwd(dhn, hhat, r, gain)
        o_ref[...] = dh_ref[...] + dhi
        acc[...] += _row_fold(dgr)

        @pl.when(i == nb - 1)
        def _():
            dg_ref[...] = jnp.sum(acc[...], axis=0, keepdims=True)

    return _rows_call(body, name, tm, [dh, h, dq], [g, wq], [(d, F32), (d, BF16), (wq.shape[1], BF16)],
                      [((1, d), F32)], [pltpu.VMEM((SUBLANES, d), F32)])


def _o_fwd(h, o, wo, name):
    n, d = h.shape
    tm = _tile(n, 512)

    def body(h_ref, o_ref, w_ref, out_ref):
        out_ref[...] = h_ref[...] + jnp.dot(o_ref[...], w_ref[...], preferred_element_type=F32)

    return _rows_call(body, name, tm, [h, o], [wo], [(d, F32)])[0]


def _exact_dot_nt(ones_mat, x):
    hi, mid, lo = _split3(x)
    d = functools.partial(lax.dot_general, dimension_numbers=NT_DIMS, preferred_element_type=F32)
    return d(ones_mat, hi) + d(ones_mat, mid) + d(ones_mat, lo)


def _o_bwd(dh, o, wo, head_sel, name):
    n, d = dh.shape
    tm = _tile(n, 512)

    def body(dh_ref, o_ref, w_ref, e_ref, do_ref, dl_ref):
        do = lax.dot_general(dh_ref[...].astype(BF16), w_ref[...], NT_DIMS, preferred_element_type=F32).astype(BF16)
        do_ref[...] = do
        dl_ref[...] = _exact_dot_nt(e_ref[...], do.astype(F32) * o_ref[...].astype(F32))

    return _rows_call(body, name, tm, [dh, o], [wo, head_sel], [(wo.shape[0], BF16)],
                      col_outs=[(head_sel.shape[0], F32)])


def _exact_dot_rhs(x, ones_mat):
    hi, mid, lo = _split3(x)
    d = functools.partial(jnp.dot, preferred_element_type=F32)
    return d(hi, ones_mat) + d(mid, ones_mat) + d(lo, ones_mat)


def _kvf_fwd(h, g, wk, wv, wf, bf, k_ones, v_ones, spread, name):
    n, d = h.shape
    tm = _tile(n, 512)

    def body(h_ref, g_ref, wk_ref, wv_ref, wf_ref, bf_ref, ko_ref, vo_ref, sp_ref,
             k_ref, v_ref, fl_ref, cum_ref, rep_ref, carry):
        i = pl.program_id(0)

        @pl.when(i == 0)
        def _():
            carry[...] = jnp.zeros_like(carry)

        hb = _rms_fwd(h_ref[...], g_ref[...])[0].astype(BF16)
        k_ref[...] = (jnp.dot(hb, wk_ref[...], preferred_element_type=F32) + ko_ref[...]).astype(BF16)
        v_ref[...] = (jnp.dot(hb, wv_ref[...], preferred_element_type=F32) + vo_ref[...]).astype(BF16)
        fl = jnp.dot(hb, wf_ref[...], preferred_element_type=F32) + bf_ref[...]
        fl_ref[...] = fl
        logf = jnp.minimum(fl, 0.0) - jnp.log(1.0 + jnp.exp(-jnp.abs(fl)))
        rows = lax.broadcasted_iota(jnp.int32, (tm, tm), 0)
        cols = lax.broadcasted_iota(jnp.int32, (tm, tm), 1)
        lower = (rows >= cols).astype(BF16)
        cum = _exact_dot(lower, logf) + carry[0:1, :]
        cum_ref[...] = cum
        rep_ref[...] = _exact_dot_rhs(cum * LOG2E, sp_ref[...])
        carry[...] = jnp.broadcast_to(cum[tm - 1:tm, :], carry.shape)

    return _rows_call(body, name, tm, [h], [g, wk, wv, wf, bf, k_ones, v_ones, spread],
                      [(wk.shape[1], BF16), (wv.shape[1], BF16), (LANES, F32), (LANES, F32), (spread.shape[1], F32)],
                      scratch=[pltpu.VMEM((SUBLANES, LANES), F32)])


def _kvf_bwd(dh, h, dk1, dk2, dv1, dv2, dq1, dq2, fl, g, wk, wv, wf, sel_q, sel_k, name):
    n, d = h.shape
    tm = _tile(n, 256)
    nb = n // tm

    def body(dh_ref, h_ref, dk1_ref, dk2_ref, dv1_ref, dv2_ref, dq1_ref, dq2_ref, fl_ref,
             g_ref, wk_ref, wv_ref, wf_ref, sq_ref, sk_ref,
             o_ref, hk_ref, dk_ref, dv_ref, df_ref, dg_ref, db_ref, acc, bacc, carry):
        i = pl.program_id(0)

        @pl.when(i == 0)
        def _():
            acc[...] = jnp.zeros_like(acc)
            bacc[...] = jnp.zeros_like(bacc)
            carry[...] = jnp.zeros_like(carry)

        dkx = dk1_ref[...] + dk2_ref[...]
        dcum = _exact_dot_rhs(dq1_ref[...] + dq2_ref[...], sq_ref[...]) - _exact_dot_rhs(dkx, sk_ref[...])
        rows = lax.broadcasted_iota(jnp.int32, (tm, tm), 0)
        cols = lax.broadcasted_iota(jnp.int32, (tm, tm), 1)
        upper = (rows <= cols).astype(BF16)
        dlogf = _exact_dot(upper, dcum) + carry[0:1, :]
        carry[...] = jnp.broadcast_to(dlogf[0:1, :], carry.shape)
        df = dlogf / (1.0 + jnp.exp(fl_ref[...]))
        dfb = df.astype(BF16)
        df_ref[...] = dfb
        bacc[...] += _row_fold(df)
        dkb = (dkx * LN2).astype(BF16)
        dvb = (dv1_ref[...] + dv2_ref[...]).astype(BF16)
        dk_ref[...] = dkb
        dv_ref[...] = dvb
        gain = g_ref[...]
        hk, hhat, r = _rms_fwd(h_ref[...], gain)
        hk_ref[...] = hk.astype(BF16)
        dhk = lax.dot_general(dkb, wk_ref[...], NT_DIMS, preferred_element_type=F32)
        dhk = dhk + lax.dot_general(dvb, wv_ref[...], NT_DIMS, preferred_element_type=F32)
        dhk = dhk + lax.dot_general(dfb, wf_ref[...], NT_DIMS, preferred_element_type=F32)
        dhi, dgr = _rms_bwd(dhk, hhat, r, gain)
        o_ref[...] = dh_ref[...] + dhi
        acc[...] += _row_fold(dgr)

        @pl.when(i == nb - 1)
        def _():
            dg_ref[...] = jnp.sum(acc[...], axis=0, keepdims=True)
            db_ref[...] = jnp.sum(bacc[...], axis=0, keepdims=True)

    return _rows_call(body, name, tm, [dh, h, dk1, dk2, dv1, dv2, dq1, dq2, fl], [g, wk, wv, wf, sel_q, sel_k],
                      [(d, F32), (d, BF16), (wk.shape[1], BF16), (wv.shape[1], BF16), (LANES, BF16)],
                      [((1, d), F32), ((1, LANES), F32)],
                      [pltpu.VMEM((SUBLANES, d), F32), pltpu.VMEM((SUBLANES, LANES), F32),
                       pltpu.VMEM((SUBLANES, LANES), F32)], reverse=True)


def _loss_head(h, target, g, name):
    n, d = h.shape
    tm = _tile(n, 512)
    nb = n // tm

    def body(h_ref, t_ref, g_ref, dh_ref, loss_ref, dg_ref, lacc, gacc):
        i = pl.program_id(0)

        @pl.when(i == 0)
        def _():
            lacc[...] = jnp.zeros_like(lacc)
            gacc[...] = jnp.zeros_like(gacc)

        gain = g_ref[...]
        yv, hhat, r = _rms_fwd(h_ref[...], gain)
        e = yv - t_ref[...]
        lacc[...] += _row_fold(e * e)
        dhv, dgr = _rms_bwd(e * (1.0 / d), hhat, r, gain)
        dh_ref[...] = dhv
        gacc[...] += _row_fold(dgr)

        @pl.when(i == nb - 1)
        def _():
            loss_ref[...] = jnp.full((1, LANES), jnp.sum(lacc[...]) * (0.5 / d), F32)
            dg_ref[...] = jnp.sum(gacc[...], axis=0, keepdims=True)

    return _rows_call(body, name, tm, [h, target], [g], [(d, F32)], [((1, LANES), F32), ((1, d), F32)],
                      [pltpu.VMEM((SUBLANES, d), F32), pltpu.VMEM((SUBLANES, d), F32)])


def _matmul_tn(a, b, name, out_dtype=BF16):
    l, m = a.shape
    n = b.shape[1]
    tl = _tile(l, 1024)
    tmm = _tile(m, 512)
    tn = _tile(n, 1024)
    nl = l // tl

    def body(a_ref, b_ref, o_ref, acc):
        k = pl.program_id(2)

        @pl.when(k == 0)
        def _():
            acc[...] = jnp.zeros_like(acc)

        acc[...] += lax.dot_general(a_ref[...].astype(BF16), b_ref[...].astype(BF16), TN_DIMS,
                                    preferred_element_type=F32)

        @pl.when(k == nl - 1)
        def _():
            o_ref[...] = acc[...].astype(out_dtype)

    return pl.pallas_call(
        body, name=name, grid=(m // tmm, n // tn, nl),
        in_specs=[pl.BlockSpec((tl, tmm), lambda i, j, k: (k, i)), pl.BlockSpec((tl, tn), lambda i, j, k: (k, j))],
        out_specs=pl.BlockSpec((tmm, tn), lambda i, j, k: (i, j)),
        out_shape=jax.ShapeDtypeStruct((m, n), out_dtype),
        scratch_shapes=[pltpu.VMEM((tmm, tn), F32)],
        compiler_params=_cp("parallel", "parallel", "arbitrary"),
    )(a, b)


def _scan_fwd(u, bblk, cblk, tabs, dskip, name):
    l, d = u.shape
    nj, gb, n2 = bblk.shape
    n = n2 // 2
    tm = _tile(l, 512)
    nb = l // tm
    nsub = 4 if tm % (4 * SUBLANES) == 0 else 1
    sub = tm // nsub

    def body(u_ref, b_ref, c_ref, t_ref, d_ref, st_ref, y_ref, carry):
        i = pl.program_id(1)

        @pl.when(i == 0)
        def _():
            carry[...] = jnp.zeros_like(carry)

        def step(rb, c):
            cr, ci = c
            rows = pl.ds(rb * SUBLANES, SUBLANES)
            xr = st_ref[rows, 0:n]
            xi = st_ref[rows, n:n2]
            for lvl, sh in enumerate((1, 2, 4)):
                ar = t_ref[0, 2 * lvl]
                ai = t_ref[0, 2 * lvl + 1]
                sr = pltpu.roll(xr, sh, 0)
                si = pltpu.roll(xi, sh, 0)
                xr, xi = xr + ar * sr - ai * si, xi + ar * si + ai * sr
            lr = t_ref[0, 6]
            li = t_ref[0, 7]
            xr, xi = xr + lr * cr - li * ci, xi + lr * ci + li * cr
            st_ref[rows, 0:n] = xr
            st_ref[rows, n:n2] = xi
            return (jnp.broadcast_to(xr[SUBLANES - 1:SUBLANES, :], (SUBLANES, n)),
                    jnp.broadcast_to(xi[SUBLANES - 1:SUBLANES, :], (SUBLANES, n)))

        def project_in(sb):
            rows = slice(sb * sub, (sb + 1) * sub)
            st_ref[rows, :] = jnp.dot(u_ref[rows, :].astype(BF16), b_ref[0], preferred_element_type=F32)

        def project_out(sb):
            rows = slice(sb * sub, (sb + 1) * sub)
            y_ref[rows, :] = (jnp.dot(st_ref[rows, :].astype(BF16), c_ref[0], preferred_element_type=F32)
                              + d_ref[...] * u_ref[rows, :])

        c = (carry[:, 0:n], carry[:, n:n2])
        project_in(0)
        for sb in range(nsub):
            if sb + 1 < nsub:
                project_in(sb + 1)
            for rb in range(sb * sub // SUBLANES, (sb + 1) * sub // SUBLANES):
                c = step(rb, c)
            project_out(sb)
        carry[:, 0:n] = c[0]
        carry[:, n:n2] = c[1]

    return pl.pallas_call(
        body, name=name, grid=(nj, nb),
        in_specs=[pl.BlockSpec((tm, gb), lambda j, i: (i, j)),
                  pl.BlockSpec((1, gb, n2), lambda j, i: (j, 0, 0)),
                  pl.BlockSpec((1, n2, gb), lambda j, i: (j, 0, 0)),
                  pl.BlockSpec((1, 8, SUBLANES, n), lambda j, i: (j, 0, 0, 0)),
                  pl.BlockSpec((1, gb), lambda j, i: (0, j))],
        out_specs=[pl.BlockSpec((tm, n2), lambda j, i: (i, j)), pl.BlockSpec((tm, gb), lambda j, i: (i, j))],
        out_shape=[jax.ShapeDtypeStruct((l, nj * n2), F32), jax.ShapeDtypeStruct((l, d), F32)],
        scratch_shapes=[pltpu.VMEM((SUBLANES, n2), F32)],
        compiler_params=_cp("parallel", "arbitrary"),
    )(u, bblk, cblk, tabs, dskip)


def _scan_bwd(dy, u, states, bblk_t, cblk_t, tabs, dskip, name):
    l, d = u.shape
    nj, n2, gb = bblk_t.shape
    n = n2 // 2
    tm = _tile(l, 512)
    nb = l // tm
    nsub = 4 if tm % (4 * SUBLANES) == 0 else 1
    sub = tm // nsub

    def body(dy_ref, u_ref, st_ref, prev_ref, bt_ref, ct_ref, t_ref, d_ref,
             du_ref, glam_ref, gd_ref, gb_ref, gc_ref, gx, carry):
        i = pl.program_id(1)
        ib = nb - 1 - i

        @pl.when(i == 0)
        def _():
            carry[...] = jnp.zeros_like(carry)
            glam_ref[...] = jnp.zeros_like(glam_ref)
            gd_ref[...] = jnp.zeros_like(gd_ref)
            gb_ref[...] = jnp.zeros_like(gb_ref)
            gc_ref[...] = jnp.zeros_like(gc_ref)

        last_row = lax.broadcasted_iota(jnp.int32, (SUBLANES, n), 0) == SUBLANES - 1

        def block(rows, xp_r, xp_i, c):
            cr, ci = c
            gr = gx[rows, 0:n]
            gi = gx[rows, n:n2]
            for lvl, sh in enumerate((1, 2, 4)):
                ar = t_ref[0, 2 * lvl]
                ai = t_ref[0, 2 * lvl + 1]
                sr = pltpu.roll(gr, SUBLANES - sh, 0)
                si = pltpu.roll(gi, SUBLANES - sh, 0)
                gr, gi = gr + ar * sr - ai * si, gi + ar * si + ai * sr
            lr = t_ref[0, 6]
            li = t_ref[0, 7]
            gr, gi = gr + lr * cr - li * ci, gi + lr * ci + li * cr
            gx[rows, 0:n] = gr
            gx[rows, n:n2] = gi
            xs_r = pltpu.roll(jnp.where(last_row, xp_r, st_ref[rows, 0:n]), 1, 0)
            xs_i = pltpu.roll(jnp.where(last_row, xp_i, st_ref[rows, n:n2]), 1, 0)
            glam_ref[0, :, 0:n] += gr * xs_r + gi * xs_i
            glam_ref[0, :, n:n2] += gi * xs_r - gr * xs_i
            return (jnp.broadcast_to(gr[0:1, :], (SUBLANES, n)), jnp.broadcast_to(gi[0:1, :], (SUBLANES, n)))

        def project_in(sb):
            rows = slice(sb * sub, (sb + 1) * sub)
            gx[rows, :] = jnp.dot(dy_ref[rows, :].astype(BF16), ct_ref[0], preferred_element_type=F32)

        def project_out(sb):
            rows = slice(sb * sub, (sb + 1) * sub)
            dyv = dy_ref[rows, :]
            uv = u_ref[rows, :]
            gxb = gx[rows, :].astype(BF16)
            du_ref[rows, :] = jnp.dot(gxb, bt_ref[0], preferred_element_type=F32) + d_ref[...] * dyv
            gd_ref[0] += _row_fold(dyv * uv)
            gb_ref[0] += lax.dot_general(uv.astype(BF16), gxb, TN_DIMS, preferred_element_type=F32)
            gc_ref[0] += lax.dot_general(st_ref[rows, :].astype(BF16), dyv.astype(BF16), TN_DIMS,
                                         preferred_element_type=F32)

        live = (ib > 0).astype(F32)
        c = (carry[:, 0:n], carry[:, n:n2])
        project_in(nsub - 1)
        for sb in reversed(range(nsub)):
            if sb > 0:
                project_in(sb - 1)
            for rb in reversed(range(sb * sub // SUBLANES, (sb + 1) * sub // SUBLANES)):
                rows = pl.ds(rb * SUBLANES, SUBLANES)
                if rb > 0:
                    before = pl.ds((rb - 1) * SUBLANES, SUBLANES)
                    c = block(rows, st_ref[before, 0:n], st_ref[before, n:n2], c)
                else:
                    c = block(rows, prev_ref[:, 0:n] * live, prev_ref[:, n:n2] * live, c)
            project_out(sb)
        carry[:, 0:n] = c[0]
        carry[:, n:n2] = c[1]

    rpb = tm // SUBLANES
    return pl.pallas_call(
        body, name=name, grid=(nj, nb),
        in_specs=[pl.BlockSpec((tm, gb), lambda j, i: (nb - 1 - i, j)),
                  pl.BlockSpec((tm, gb), lambda j, i: (nb - 1 - i, j)),
                  pl.BlockSpec((tm, n2), lambda j, i: (nb - 1 - i, j)),
                  pl.BlockSpec((SUBLANES, n2), lambda j, i: (jnp.maximum((nb - 1 - i) * rpb - 1, 0), j)),
                  pl.BlockSpec((1, n2, gb), lambda j, i: (j, 0, 0)),
                  pl.BlockSpec((1, gb, n2), lambda j, i: (j, 0, 0)),
                  pl.BlockSpec((1, 8, SUBLANES, n), lambda j, i: (j, 0, 0, 0)),
                  pl.BlockSpec((1, gb), lambda j, i: (0, j))],
        out_specs=[pl.BlockSpec((tm, gb), lambda j, i: (nb - 1 - i, j)),
                   pl.BlockSpec((1, SUBLANES, n2), lambda j, i: (j, 0, 0)),
                   pl.BlockSpec((1, SUBLANES, gb), lambda j, i: (j, 0, 0)),
                   pl.BlockSpec((1, gb, n2), lambda j, i: (j, 0, 0)),
                   pl.BlockSpec((1, n2, gb), lambda j, i: (j, 0, 0))],
        out_shape=[jax.ShapeDtypeStruct((l, d), F32),
                   jax.ShapeDtypeStruct((nj, SUBLANES, n2), F32),
                   jax.ShapeDtypeStruct((nj, SUBLANES, gb), F32),
                   jax.ShapeDtypeStruct((nj, gb, n2), F32),
                   jax.ShapeDtypeStruct((nj, n2, gb), F32)],
        scratch_shapes=[pltpu.VMEM((tm, n2), F32), pltpu.VMEM((SUBLANES, n2), F32)],
        compiler_params=_cp("parallel", "arbitrary"),
    )(dy, u, states, states, bblk_t, cblk_t, tabs, dskip)


HEAD_SLOT = 128
PAIR_SLOT = 2 * HEAD_SLOT
ATTN_TILE = 1024
LANE_ROWSUM_P = HEAD_DIM
LANE_COLSUM_DS = HEAD_DIM
LANE_ROWSUM_DS = HEAD_DIM + 1


def _slot_cols(w):
    r, c = w.shape
    nh = c // HEAD_DIM
    return jnp.pad(w.reshape(r, nh, HEAD_DIM), ((0, 0), (0, 0), (0, HEAD_SLOT - HEAD_DIM))).reshape(r, nh * HEAD_SLOT)


def _unslot_cols(w):
    r, c = w.shape
    nh = c // HEAD_SLOT
    return w.reshape(r, nh, HEAD_SLOT)[:, :, :HEAD_DIM].reshape(r, nh * HEAD_DIM)


def _slot_ones(nh, lane):
    return jnp.tile((jnp.arange(HEAD_SLOT) == lane).astype(F32), nh).reshape(1, nh * HEAD_SLOT)


def _causal_tiles(n, by_query):
    if by_query:
        tiles = [(i, j) for i in range(n) for j in range(i + 1)]
    else:
        tiles = [(i, j) for j in range(n) for i in range(j, n)]
    return (jnp.asarray([t[0] for t in tiles], jnp.int32), jnp.asarray([t[1] for t in tiles], jnp.int32))


def _flash_fwd(qx, kx, vx, f2_rows, name):
    l = qx.shape[0]
    npair = qx.shape[1] // PAIR_SLOT
    d = npair * HEAD_PAIR
    tq = _tile(l, ATTN_TILE)
    tk = tq
    i_of, j_of = _causal_tiles(l // tq, by_query=True)

    def body(i_ref, j_ref, q_ref, k_ref, v_ref, f_ref, o_ref, lse_ref, m_sc, acc_sc):
        t = pl.program_id(1)
        i = i_ref[t]
        j = j_ref[t]

        @pl.when(j == 0)
        def _():
            m_sc[...] = jnp.full_like(m_sc, NEG)
            acc_sc[...] = jnp.zeros_like(acc_sc)

        def tile(on_diagonal):
            for hh in range(2):
                hs = slice(hh * HEAD_SLOT, (hh + 1) * HEAD_SLOT)
                s = lax.dot_general(q_ref[:, hs], k_ref[:, hs], NT_DIMS, preferred_element_type=F32)
                s = s - f_ref[0, hh:hh + 1, :]
                if on_diagonal:
                    keep = (lax.broadcasted_iota(jnp.int32, (tq, tk), 0)
                            >= lax.broadcasted_iota(jnp.int32, (tq, tk), 1))
                    s = jnp.where(keep, s, NEG)
                m_prev = m_sc[hh]
                m_new = jnp.maximum(m_prev, jnp.max(s, axis=-1, keepdims=True))
                alpha = jnp.exp2(m_prev - m_new)
                p = jnp.exp2(s - jnp.concatenate([m_new] * (tk // LANES), axis=1)).astype(BF16)
                acc_sc[hh] = alpha * acc_sc[hh] + jnp.dot(p, v_ref[:, hs], preferred_element_type=F32)
                m_sc[hh] = m_new

        @pl.when(j < i)
        def _():
            tile(False)

        @pl.when(j == i)
        def _():
            tile(True)
            lane0 = (lax.broadcasted_iota(jnp.int32, (SUBLANES, LANES), 1) == 0).astype(BF16)
            outs, lses = [], []
            for hh in range(2):
                acc = acc_sc[hh]
                lsum = acc[:, LANE_ROWSUM_P:LANE_ROWSUM_P + 1]
                outs.append(acc[:, :HEAD_DIM] / lsum)
                lse_cols = m_sc[hh] + jnp.log2(lsum)
                lses.append(_exact_dot_nt(lane0, lse_cols)[0:1, :])
            o_ref[...] = jnp.concatenate(outs, axis=1).astype(BF16)
            lse_ref[0] = jnp.concatenate(lses, axis=0)

    q_map = lambda h, t, i_ref, j_ref: (i_ref[t], h)
    kv_map = lambda h, t, i_ref, j_ref: (j_ref[t], h)
    return pl.pallas_call(
        body, name=name,
        grid_spec=pltpu.PrefetchScalarGridSpec(
            num_scalar_prefetch=2, grid=(npair, i_of.shape[0]),
            in_specs=[pl.BlockSpec((tq, PAIR_SLOT), q_map), pl.BlockSpec((tk, PAIR_SLOT), kv_map),
                      pl.BlockSpec((tk, PAIR_SLOT), kv_map),
                      pl.BlockSpec((1, 2, tk), lambda h, t, i_ref, j_ref: (h, 0, j_ref[t]))],
            out_specs=[pl.BlockSpec((tq, HEAD_PAIR), q_map),
                       pl.BlockSpec((1, 2, tq), lambda h, t, i_ref, j_ref: (h, 0, i_ref[t]))],
            scratch_shapes=[pltpu.VMEM((2, tq, LANES), F32), pltpu.VMEM((2, tq, HEAD_SLOT), F32)]),
        out_shape=[jax.ShapeDtypeStruct((l, d), BF16), jax.ShapeDtypeStruct((npair, 2, l), F32)],
        compiler_params=_cp("parallel", "arbitrary"),
    )(i_of, j_of, qx, kx, vx, f2_rows)


def _flash_bwd(qx, kx, vx, f2_rep, do, lse_rows, delta_rows, name):
    l = qx.shape[0]
    npair = qx.shape[1] // PAIR_SLOT
    d = npair * HEAD_PAIR
    tq = _tile(l, ATTN_TILE)
    tk = tq
    i_of, j_of = _causal_tiles(l // tq, by_query=False)

    def body(i_ref, j_ref, q_ref, k_ref, v_ref, f_ref, do_ref, lse_ref, dl_ref, dq_ref, dk_ref, dv_ref):
        t = pl.program_id(1)
        i = i_ref[t]
        j = j_ref[t]

        @pl.when(t == 0)
        def _():
            dq_ref[...] = jnp.zeros_like(dq_ref)

        @pl.when(i == j)
        def _():
            dk_ref[...] = jnp.zeros_like(dk_ref)
            dv_ref[...] = jnp.zeros_like(dv_ref)

        def tile(on_diagonal):
            dqs, dks, dvs = [], [], []
            for hh in range(2):
                hs = slice(hh * HEAD_SLOT, (hh + 1) * HEAD_SLOT)
                qh, kh = q_ref[:, hs], k_ref[:, hs]
                vh = v_ref[:, hh * HEAD_SLOT:hh * HEAD_SLOT + HEAD_DIM]
                doh = do_ref[:, hh * HEAD_DIM:(hh + 1) * HEAD_DIM]
                st = lax.dot_general(kh, qh, NT_DIMS, preferred_element_type=F32)
                st = st - jnp.concatenate([f_ref[:, hs]] * (tq // HEAD_SLOT), axis=1)
                pt = jnp.exp2(st - lse_ref[0, hh:hh + 1, :])
                if on_diagonal:
                    keep = (lax.broadcasted_iota(jnp.int32, (tk, tq), 1)
                            >= lax.broadcasted_iota(jnp.int32, (tk, tq), 0))
                    pt = jnp.where(keep, pt, 0.0)
                dpt = lax.dot_general(vh, doh, NT_DIMS, preferred_element_type=F32)
                dsb = (pt * (dpt - dl_ref[0, hh:hh + 1, :])).astype(BF16)
                dvs.append(jnp.dot(pt.astype(BF16), doh, preferred_element_type=F32))
                dks.append(jnp.dot(dsb, qh, preferred_element_type=F32))
                dqs.append(lax.dot_general(dsb, kh, TN_DIMS, preferred_element_type=F32))
            dv_ref[...] += jnp.concatenate(dvs, axis=1)
            dk_ref[...] += jnp.concatenate(dks, axis=1)
            dq_ref[pl.ds(pl.multiple_of(i * tq, tq), tq), :] += jnp.concatenate(dqs, axis=1)

        @pl.when(i > j)
        def _():
            tile(False)

        @pl.when(i == j)
        def _():
            tile(True)

    qmap = lambda h, t, i_ref, j_ref: (i_ref[t], h)
    kmap = lambda h, t, i_ref, j_ref: (j_ref[t], h)
    row_map = lambda h, t, i_ref, j_ref: (h, 0, i_ref[t])
    return pl.pallas_call(
        body, name=name,
        grid_spec=pltpu.PrefetchScalarGridSpec(
            num_scalar_prefetch=2, grid=(npair, i_of.shape[0]),
            in_specs=[pl.BlockSpec((tq, PAIR_SLOT), qmap), pl.BlockSpec((tk, PAIR_SLOT), kmap),
                      pl.BlockSpec((tk, PAIR_SLOT), kmap), pl.BlockSpec((tk, PAIR_SLOT), kmap),
                      pl.BlockSpec((tq, HEAD_PAIR), qmap),
                      pl.BlockSpec((1, 2, tq), row_map), pl.BlockSpec((1, 2, tq), row_map)],
            out_specs=[pl.BlockSpec((l, PAIR_SLOT), lambda h, t, i_ref, j_ref: (0, h)),
                       pl.BlockSpec((tk, PAIR_SLOT), kmap), pl.BlockSpec((tk, HEAD_PAIR), kmap)]),
        out_shape=[jax.ShapeDtypeStruct((l, npair * PAIR_SLOT), F32), jax.ShapeDtypeStruct((l, npair * PAIR_SLOT), F32),
                   jax.ShapeDtypeStruct((l, d), F32)],
        compiler_params=_cp("parallel", "arbitrary"),
    )(i_of, j_of, qx, kx, vx, f2_rep, do, lse_rows, delta_rows)


def _my_place():
    return lax.axis_index("x"), lax.axis_index("y"), lax.axis_index("c")


def _chip_exchange(srcs, out_meta, plan, name):
    n_src, n_out, n_plan = len(srcs), len(out_meta), len(plan)

    def body(*refs):
        src_refs = refs[:n_src]
        out_refs = refs[n_src:n_src + n_out]
        send_sems, recv_sems, local_sems = refs[n_src + n_out:]
        x, y, c = _my_place()
        me = 2 * x + y
        copies = []
        for n, (si, oi, src_view, dst_view) in enumerate(plan):
            local = pltpu.make_async_copy(src_view(src_refs[si], me), dst_view(out_refs[oi], me), local_sems.at[n])
            local.start()
            copies.append(local)
            for k in (1, 2, 3):
                peer = me ^ k
                rc = pltpu.make_async_remote_copy(
                    src_ref=src_view(src_refs[si], peer), dst_ref=dst_view(out_refs[oi], me),
                    send_sem=send_sems.at[n, k - 1], recv_sem=recv_sems.at[n, k - 1],
                    device_id=(peer >> 1, peer & 1, c), device_id_type=MESH)
                rc.start()
                copies.append(rc)
        for cp in copies:
            cp.wait()

    any_spec = pl.BlockSpec(memory_space=pl.ANY)
    return pl.pallas_call(
        body, name=name,
        in_specs=[any_spec] * n_src, out_specs=[any_spec] * n_out,
        out_shape=[jax.ShapeDtypeStruct(shape, dt) for (shape, dt) in out_meta],
        scratch_shapes=[pltpu.SemaphoreType.DMA((n_plan, 3)), pltpu.SemaphoreType.DMA((n_plan, 3)),
                        pltpu.SemaphoreType.DMA((n_plan,))],
    )(*srcs)


def _plan_copies(src_refs, land_refs, plan, send_sems, recv_sems):
    x, y, c = _my_place()
    me = 2 * x + y
    copies = []
    for n, (si, oi, src_view, dst_view) in enumerate(plan):
        for k in (1, 2, 3):
            peer = me ^ k
            copies.append(pltpu.make_async_remote_copy(
                src_ref=src_view(src_refs[si], peer), dst_ref=dst_view(land_refs[oi], me),
                send_sem=send_sems.at[3 * n + k - 1], recv_sem=recv_sems.at[3 * n + k - 1],
                device_id=(peer >> 1, peer & 1, c), device_id_type=MESH))
    return copies


def _hbm(a):
    return pltpu.HBM(a.shape, a.dtype)


def _exchange_start(srcs, lands, plan, name):
    n_src, n_land = len(srcs), len(lands)
    n_buf = n_src + n_land

    def body(*refs):
        send_sems, recv_sems = refs[n_buf], refs[n_buf + 1]
        token = refs[-1]
        for cp in _plan_copies(refs[:n_src], refs[n_src:n_buf], plan, send_sems, recv_sems):
            cp.start()
        token[...] = jnp.zeros_like(token)

    bufs = [pltpu.with_memory_space_constraint(a, pltpu.HBM) for a in (*srcs, *lands)]
    hbm = pl.BlockSpec(memory_space=pltpu.HBM)
    sem = pl.BlockSpec(memory_space=pltpu.SEMAPHORE)
    res = pl.pallas_call(
        body, name=name,
        out_shape=(pltpu.SemaphoreType.DMA((3 * len(plan),)), pltpu.SemaphoreType.DMA((3 * len(plan),)),
                   *[_hbm(a) for a in bufs], jax.ShapeDtypeStruct((SUBLANES, LANES), F32)),
        in_specs=[hbm] * n_buf, out_specs=(sem, sem, *[hbm] * n_buf, pl.BlockSpec(memory_space=pltpu.VMEM)),
        input_output_aliases={n: 2 + n for n in range(n_buf)},
        compiler_params=pltpu.CompilerParams(has_side_effects=pltpu.SideEffectType.DATAFLOW_SIDE_EFFECTING),
    )(*bufs)
    return (res[0], res[1]), list(res[2:2 + n_src]), list(res[2 + n_src:2 + n_buf]), res[-1]


def _exchange_wait(sems, srcs, lands, plan, after, name):
    n_src, n_land = len(srcs), len(lands)
    n_buf = n_src + n_land

    def body(*refs):
        send_sems, recv_sems = refs[n_buf], refs[n_buf + 1]
        for cp in _plan_copies(refs[:n_src], refs[n_src:n_buf], plan, send_sems, recv_sems):
            cp.wait_send()
            cp.wait_recv()

    hbm = pl.BlockSpec(memory_space=pltpu.HBM)
    sem = pl.BlockSpec(memory_space=pltpu.SEMAPHORE)
    res = pl.pallas_call(
        body, name=name, out_shape=tuple(_hbm(a) for a in (*srcs, *lands)),
        in_specs=[hbm] * n_buf + [sem, sem, pl.BlockSpec(memory_space=pl.ANY)], out_specs=tuple([hbm] * n_buf),
        input_output_aliases={n: n for n in range(n_buf)},
        compiler_params=pltpu.CompilerParams(has_side_effects=pltpu.SideEffectType.DATAFLOW_SIDE_EFFECTING),
    )(*srcs, *lands, sems[0], sems[1], after)
    return list(res[n_src:])


def _core_exchange(arrays, name):
    n_items = len(arrays)

    def body(*refs):
        srcs = refs[:n_items]
        outs = refs[n_items:2 * n_items]
        send_sems, recv_sems = refs[2 * n_items:]
        x, y, c = _my_place()
        copies = []
        for n in range(n_items):
            rc = pltpu.make_async_remote_copy(
                src_ref=srcs[n], dst_ref=outs[n], send_sem=send_sems.at[n], recv_sem=recv_sems.at[n],
                device_id=(x, y, 1 - c), device_id_type=MESH)
            rc.start()
            copies.append(rc)
        for cp in copies:
            cp.wait()

    any_spec = pl.BlockSpec(memory_space=pl.ANY)
    return pl.pallas_call(
        body, name=name,
        in_specs=[any_spec] * n_items, out_specs=[any_spec] * n_items,
        out_shape=[jax.ShapeDtypeStruct(a.shape, a.dtype) for a in arrays],
        scratch_shapes=[pltpu.SemaphoreType.DMA((n_items,)), pltpu.SemaphoreType.DMA((n_items,))],
    )(*arrays)


def _sum_chips(parts, name):
    _, rows, cols = parts.shape
    tm = _tile(rows, 512)

    def body(p_ref, o_ref):
        acc = p_ref[0].astype(F32)
        for s in range(1, N_CHIPS):
            acc = acc + p_ref[s].astype(F32)
        o_ref[...] = acc

    return pl.pallas_call(
        body, name=name, grid=(rows // tm,),
        in_specs=[pl.BlockSpec((N_CHIPS, tm, cols), lambda i: (0, i, 0))],
        out_specs=pl.BlockSpec((tm, cols), lambda i: (i, 0)),
        out_shape=jax.ShapeDtypeStruct((rows, cols), F32),
        compiler_params=_cp("parallel"),
    )(parts)


def _adamw(ga, gb, w, m, v, name):
    rows, cols = w.shape
    tm = _tile(rows, 512)
    c1 = 1.0 - ADAM_B1 ** ADAM_STEP
    c2 = 1.0 - ADAM_B2 ** ADAM_STEP

    def body(ga_ref, gb_ref, w_ref, m_ref, v_ref, g_ref, d_ref, nm_ref, nv_ref):
        g = ga_ref[...] + gb_ref[...]
        nm = ADAM_B1 * m_ref[...] + (1.0 - ADAM_B1) * g
        nv = ADAM_B2 * v_ref[...] + (1.0 - ADAM_B2) * (g * g)
        g_ref[...] = g
        nm_ref[...] = nm
        nv_ref[...] = nv
        d_ref[...] = -ADAM_LR * ((nm / c1) / (jnp.sqrt(nv / c2) + ADAM_EPS) + ADAM_WD * w_ref[...])

    spec = pl.BlockSpec((tm, cols), lambda i: (i, 0))
    return pl.pallas_call(
        body, name=name, grid=(rows // tm,), in_specs=[spec] * 5, out_specs=[spec] * 4,
        out_shape=[jax.ShapeDtypeStruct((rows, cols), F32)] * 4, compiler_params=_cp("parallel"),
    )(ga, gb, w, m, v)


def _ssm_discretise(log_dt, a_re, a_im, b_re, b_im):
    dt = jnp.exp(log_dt)[:, None]
    mag = jnp.exp(a_re * dt)
    lbr = mag * jnp.cos(a_im * dt)
    lbi = mag * jnp.sin(a_im * dt)
    den = a_re * a_re + a_im * a_im
    nr, ni = lbr - 1.0, lbi
    qr = (nr * a_re + ni * a_im) / den
    qi = (ni * a_re - nr * a_im) / den
    bbr = qr[..., None] * b_re - qi[..., None] * b_im
    bbi = qr[..., None] * b_im + qi[..., None] * b_re
    return lbr, lbi, bbr, bbi


def _cmul(ar, ai, br, bi):
    return ar * br - ai * bi, ar * bi + ai * br


def _scan_tables(lr, li, nj, reverse):
    lr = lr.reshape(nj, 1, -1)
    li = li.reshape(nj, 1, -1)
    if reverse:
        li = -li
    pows = [(lr, li)]
    for _ in range(7):
        pows.append(_cmul(*pows[-1], lr, li))
    r = jnp.arange(SUBLANES).reshape(1, SUBLANES, 1)
    if reverse:
        r = SUBLANES - 1 - r
    out = []
    for k in (1, 2, 4):
        pr, pi = pows[k - 1]
        keep = (r >= k).astype(F32)
        out += [pr * keep, pi * keep]
    shape = (nj, SUBLANES, lr.shape[-1])
    cr = jnp.zeros(shape, F32)
    ci = jnp.zeros(shape, F32)
    for e in range(SUBLANES):
        sel = (r == e).astype(F32)
        cr = cr + sel * pows[e][0]
        ci = ci + sel * pows[e][1]
    out += [cr, ci]
    return jnp.stack(out, axis=1)


def _group_eye(gl):
    return jnp.eye(gl, dtype=F32)


def _block_diag_in(bbr, bbi, nj):
    g, p, c = bbr.shape
    gl = g // nj
    eye = _group_eye(gl)[None, :, None, :, None]

    def one(b):
        t = b.reshape(nj, gl, p, c).transpose(0, 1, 3, 2)[:, :, :, None, :]
        return (t * eye).reshape(nj, gl * c, gl * p)

    return jnp.concatenate([one(bbr), one(bbi)], axis=2)


def _block_diag_in_grad(gmat, nj, p, c):
    gl = gmat.shape[1] // c
    n = gl * p
    eye = _group_eye(gl)[None, :, None, :, None]

    def one(m):
        t = jnp.sum(m.reshape(nj, gl, c, gl, p) * eye, axis=3)
        return t.transpose(0, 1, 3, 2).reshape(nj * gl, p, c)

    return one(gmat[:, :, :n]), one(gmat[:, :, n:])


def _block_diag_out(c_re, c_im, nj):
    g, c, p = c_re.shape
    gl = g // nj
    eye = _group_eye(gl)[None, :, None, :, None]

    def one(m):
        t = m.reshape(nj, gl, c, p).transpose(0, 1, 3, 2)[:, :, :, None, :]
        return (t * eye).reshape(nj, gl * p, gl * c)

    return jnp.concatenate([one(c_re), -one(c_im)], axis=1)


def _block_diag_out_grad(gmat, nj, p, c):
    gl = gmat.shape[2] // c
    n = gl * p
    eye = _group_eye(gl)[None, :, None, :, None]

    def one(m):
        t = jnp.sum(m.reshape(nj, gl, p, gl, c) * eye, axis=3)
        return t.transpose(0, 1, 3, 2).reshape(nj * gl, c, p)

    return one(gmat[:, :n, :]), -one(gmat[:, n:, :])


def _pad_rows(flat, cols):
    per = SUBLANES * cols
    n = flat.shape[0]
    total = -(-n // per) * per
    return jnp.pad(flat, (0, total - n)).reshape(total // cols, cols)


def _pack_small(arrs, cols):
    packed = jnp.concatenate([_pad_rows(a.reshape(-1), cols) for a in arrs], axis=0)
    rows = packed.shape[0]
    return jnp.pad(packed, ((0, -rows % 128), (0, 0)))


def _unpack_small(packed, shapes, cols):
    out = []
    row = 0
    for s in shapes:
        n = math.prod(s)
        rows = -(-n // (SUBLANES * cols)) * SUBLANES
        out.append(packed[row:row + rows].reshape(-1)[:n].reshape(s))
        row += rows
    return out


def kernel(x, mix_norm, mlp_norm, mlp_w1, mlp_w2, ssm_log_dt, ssm_a_re, ssm_a_im, ssm_b_re, ssm_b_im, ssm_c_re, ssm_c_im, ssm_d, ssm_w_glu, kv_norm, w_kvf, b_f, attn_wq, attn_wo, final_norm, loss_target, m_mix_norm, m_mlp_norm, m_mlp_w1, m_mlp_w2, m_ssm_log_dt, m_ssm_a_re, m_ssm_a_im, m_ssm_b_re, m_ssm_b_im, m_ssm_c_re, m_ssm_c_im, m_ssm_d, m_ssm_w_glu, m_kv_norm, m_w_kvf, m_b_f, m_attn_wq, m_attn_wo, m_final_norm, v_mix_norm, v_mlp_norm, v_mlp_w1, v_mlp_w2, v_ssm_log_dt, v_ssm_a_re, v_ssm_a_im, v_ssm_b_re, v_ssm_b_im, v_ssm_c_re, v_ssm_c_im, v_ssm_d, v_ssm_w_glu, v_kv_norm, v_w_kvf, v_b_f, v_attn_wq, v_attn_wo, v_final_norm):
    seq, d = x.shape[1], x.shape[2]
    depth = mix_norm.shape[0]
    n_a = ssm_log_dt.shape[0]
    n_b = depth - n_a
    ff = mlp_w1.shape[2] * N_CHIPS
    n_heads = d // HEAD_DIM
    n_groups = d // SSM_GROUP
    p_state = ssm_a_re.shape[2]
    gb = min(d, 256)
    nj = d // gb
    kvf_cols = w_kvf.shape[1]
    ds4, dq4 = d // N_CHIPS, d // (2 * N_CHIPS)
    chip = 2 * lax.axis_index("x") + lax.axis_index("y")

    def cols_of(width):
        return lambda ref, s: ref.at[:, :, pl.ds(pl.multiple_of(s * width, LANES), width)]

    def rows_of(height):
        return lambda ref, s: ref.at[:, pl.ds(pl.multiple_of(s * height, SUBLANES), height), :]

    whole = lambda ref, s: ref
    slot = lambda ref, s: ref.at[s]
    def cols2(width):
        return lambda ref, s: ref.at[:, pl.ds(pl.multiple_of(s * width, LANES), width)]

    def rows2(height):
        return lambda ref, s: ref.at[pl.ds(pl.multiple_of(s * height, SUBLANES), height), :]

    assert n_a >= 2
    (skip_parts,) = _chip_exchange([ssm_d], [((N_CHIPS, n_a, ds4), F32)], [(0, 0, whole, slot)], "gather_skip")
    skip_all = skip_parts.transpose(1, 0, 2).reshape(n_a, d)
    w1_s, w2_s, glu_s = mlp_w1.astype(BF16), mlp_w2.astype(BF16), ssm_w_glu.astype(BF16)
    src_a = [w1_s[0], w2_s[0], glu_s[0]]
    plan_a = [(0, 0, whole, cols2(d)), (1, 1, whole, rows2(d)), (2, 2, whole, cols2(2 * ds4))]
    land_a = [lax.empty((d, ff), BF16), lax.empty((ff, d), BF16), lax.empty((d, 2 * d), BF16)]
    src_b = [w1_s[1:], w2_s[1:], glu_s[1:], w_kvf.astype(BF16), attn_wq.astype(BF16), attn_wo.astype(BF16)]
    plan_b = [(0, 0, whole, cols_of(d)), (1, 1, whole, rows_of(d)), (2, 2, whole, cols_of(2 * ds4)),
              (3, 3, whole, slot), (4, 4, whole, rows_of(ds4)), (5, 5, whole, rows_of(ds4))]
    land_b = [lax.empty((depth - 1, d, ff), BF16), lax.empty((depth - 1, ff, d), BF16),
              lax.empty((n_a - 1, d, 2 * d), BF16), lax.empty((N_CHIPS, d, kvf_cols), BF16),
              lax.empty((n_b, d, d), BF16), lax.empty((n_b, d, d), BF16)]
    def idx(*at):
        return [jnp.asarray(v, jnp.int32) for v in at]

    put = lax.dynamic_update_slice
    land_a = [put(land_a[0], src_a[0], idx(0, chip * d)), put(land_a[1], src_a[1], idx(chip * d, 0)),
              put(land_a[2], src_a[2], idx(0, chip * 2 * ds4))]
    land_b = [put(land_b[0], src_b[0], idx(0, 0, chip * d)), put(land_b[1], src_b[1], idx(0, chip * d, 0)),
              put(land_b[2], src_b[2], idx(0, 0, chip * 2 * ds4)), put(land_b[3], src_b[3][None], idx(chip, 0, 0)),
              put(land_b[4], src_b[4], idx(0, chip * ds4, 0)), put(land_b[5], src_b[5], idx(0, chip * ds4, 0))]
    sems_a, src_a, land_a, token_a = _exchange_start(src_a, land_a, plan_a, "gather_start_a")
    sems_b, src_b, land_b, token_b = _exchange_start(src_b, land_b, plan_b, "gather_start_b")
    started = token_a[0:1, 0:1] + token_b[0:1, 0:1]

    def layer_w1(i):
        return w1_0 if i == 0 else w1_rest[i - 1]

    def layer_w2(i):
        return w2_0 if i == 0 else w2_rest[i - 1]

    def layer_glu(i):
        return glu_0 if i == 0 else glu_rest[i - 1]

    h = x[0]
    target = loss_target[0]

    saved = []
    for i in range(n_a):
        lbr, lbi, bbr, bbi = _ssm_discretise(ssm_log_dt[i], ssm_a_re[i], ssm_a_im[i], ssm_b_re[i], ssm_b_im[i])
        bblk = _block_diag_in(bbr, bbi, nj)
        cblk = _block_diag_out(ssm_c_re[i], ssm_c_im[i], nj)
        rec = dict(h0=h, lam=(lbr, lbi), bblk=bblk, cblk=cblk)
        gain = mix_norm[i:i + 1] + started if i == 0 else mix_norm[i:i + 1]
        u = _norm_fwd(h, gain, f"s5_norm_{i}")
        rec["u"] = u
        dskip = rec["dskip"] = skip_all[i:i + 1]
        states, y = _scan_fwd(u, bblk.astype(BF16), cblk.astype(BF16), _scan_tables(lbr, lbi, nj, False), dskip,
                              f"s5_scan_{i}")
        rec["states"], rec["y"] = states, y
        if i == 0:
            w1_0, w2_0, glu_0 = _exchange_wait(sems_a, src_a, land_a, plan_a, y, "gather_wait_a")
        if i == 1:
            w1_rest, w2_rest, glu_rest, kvf_parts, wq_all, wo_all = _exchange_wait(
                sems_b, src_b, land_b, plan_b, y, "gather_wait_b")
        h, rec["zw"] = _s5_post_fwd(h, y, layer_glu(i), f"s5_glu_{i}")
        rec["h1"] = h
        h, rec["ap"] = _mlp_fwd(h, mlp_norm[i:i + 1], layer_w1(i), layer_w2(i), f"mlp_{i}")
        saved.append(rec)
    h_kv = h
    kvf_all = jnp.concatenate([kvf_parts[s] for s in range(N_CHIPS)], axis=1)
    wk = kvf_all[:, :d]
    wv = kvf_all[:, d:2 * d]
    wf = jnp.pad(kvf_all[:, 2 * d:], ((0, 0), (0, LANES - n_heads)))
    bf_row = jnp.pad(b_f, (0, LANES - n_heads)).reshape(1, LANES)
    wk_x = _slot_cols(wk)
    spread = (jnp.arange(LANES)[:, None] == jnp.arange(n_heads * HEAD_SLOT)[None, :] // HEAD_SLOT).astype(BF16)
    kx, vx, flog, cum, f2_rep = _kvf_fwd(h, kv_norm.reshape(1, d), wk_x, _slot_cols(wv), wf, bf_row,
                                         _slot_ones(n_heads, LANE_ROWSUM_DS), _slot_ones(n_heads, LANE_ROWSUM_P),
                                         spread, "kvf")
    f2_rows = (cum[:, :n_heads] * LOG2E).T.reshape(n_heads // 2, 2, seq)
    wq_x = [_slot_cols(wq_all[jb]) for jb in range(n_b)]
    for jb in range(n_b):
        i = n_a + jb
        rec = dict(h0=h)
        qx = _q_fwd(h, mix_norm[i:i + 1], wq_x[jb], _slot_ones(n_heads, LANE_COLSUM_DS), f"attn_q_{jb}")
        o, lse = _flash_fwd(qx, kx, vx, f2_rows, f"attn_core_{jb}")
        rec["qx"], rec["o"], rec["lse"] = qx, o, lse
        h = _o_fwd(h, o, wo_all[jb], f"attn_out_{jb}")
        rec["h1"] = h
        h, rec["ap"] = _mlp_fwd(h, mlp_norm[i:i + 1], layer_w1(i), layer_w2(i), f"mlp_{i}")
        saved.append(rec)
    dh, loss_row, g_final = _loss_head(h, target, final_norm.reshape(1, d), "loss_head")
    loss = lax.psum(loss_row[0, 0], ("x", "y", "c"))

    head_sel = (jnp.arange(n_heads)[:, None] == jnp.arange(d)[None, :] // HEAD_DIM).astype(BF16)
    slot_lane = jnp.arange(n_heads * HEAD_SLOT)[:, None]
    in_lane = jnp.arange(LANES)[None, :]
    sel_q = (slot_lane == in_lane * HEAD_SLOT + LANE_ROWSUM_DS).astype(BF16)
    sel_k = (slot_lane == in_lane * HEAD_SLOT + LANE_COLSUM_DS).astype(BF16)
    g_mix = [None] * depth
    g_mlp = [None] * depth
    g_w1 = [None] * depth
    g_w2 = [None] * depth
    g_wq = [None] * n_b
    g_wo = [None] * n_b
    g_glu = [None] * n_a
    g_ssm = [None] * n_a
    dk_parts, dv_parts, dq_parts = [], [], []

    red_waits = []

    def reduce_start(entries, name):
        numbers = sorted({e[1] for e in entries})
        lands = {}
        for (_, ln, shape, dt, _, _, _, _) in entries:
            if ln not in lands:
                lands[ln] = lax.empty(shape, dt)
        for (_, ln, _, _, _, _, own, at) in entries:
            lands[ln] = lax.dynamic_update_slice(lands[ln], own, idx(*at))
        plan = [(n, numbers.index(e[1]), e[4], e[5]) for n, e in enumerate(entries)]
        sems, srcs, lands_t, token = _exchange_start([e[0] for e in entries], [lands[ln] for ln in numbers], plan,
                                                     name + "_start")
        red_waits.append((sems, srcs, lands_t, plan, name + "_wait"))
        return token[0:1, 0:1]

    def w1_entry(i, ln, local, n_layers):
        own = lax.dynamic_slice(g_w1[i], idx(0, chip * d), (d, d))[None, None]
        return (g_w1[i], ln, (N_CHIPS, n_layers, d, d), BF16, cols2(d), into(local), own, (chip, local, 0, 0))

    def w2_entry(i, ln, local, n_layers):
        own = lax.dynamic_slice(g_w2[i], idx(chip * d, 0), (d, d))[None, None]
        return (g_w2[i], ln, (N_CHIPS, n_layers, d, d), BF16, rows2(d), into(local), own, (chip, local, 0, 0))

    def glu_entry(i, ln, local, n_layers):
        own = lax.dynamic_slice(g_glu[i], idx(0, chip * 2 * ds4), (d, 2 * ds4))[None, None]
        return (g_glu[i], ln, (N_CHIPS, n_layers, d, 2 * ds4), BF16, cols2(2 * ds4), into(local), own,
                (chip, local, 0, 0))

    def rows_entry(g, ln, local):
        own = lax.dynamic_slice(g, idx(chip * ds4, 0), (ds4, d))[None, None]
        return (g, ln, (N_CHIPS, n_b, ds4, d), BF16, rows2(ds4), into(local), own, (chip, local, 0, 0))

    def into(layer):
        return lambda ref, s: ref.at[s, layer]

    def mlp_back(dh, i, rec, tie=None):
        gain = mlp_norm[i:i + 1] if tie is None else mlp_norm[i:i + 1] + tie
        dh_in, hm, a, dap, g_mlp[i] = _mlp_bwd(dh, rec["h1"], rec["ap"], gain, layer_w1(i),
                                               layer_w2(i), f"mlp_bwd_{i}")
        g_w2[i] = _matmul_tn(a, dh, f"mlp_dw2_{i}")
        g_w1[i] = _matmul_tn(hm, dap, f"mlp_dw1_{i}")
        return dh_in

    for jb in reversed(range(n_b)):
        i = n_a + jb
        rec = saved[i]
        dh = mlp_back(dh, i, rec)
        do, delta = _o_bwd(dh, rec["o"], wo_all[jb], head_sel, f"attn_out_bwd_{jb}")
        g_wo[jb] = _matmul_tn(rec["o"], dh, f"attn_dwo_{jb}")
        dqx, dkx, dv = _flash_bwd(rec["qx"], kx, vx, f2_rep, do, rec["lse"], delta.reshape(n_heads // 2, 2, seq),
                                  f"attn_core_bwd_{jb}")
        dk_parts.append(dkx)
        dv_parts.append(dv)
        dq_parts.append(dqx)
        dh, hn, dqs, g_mix[i] = _q_bwd(dh, rec["h0"], dqx, mix_norm[i:i + 1], wq_x[jb], f"attn_q_bwd_{jb}")
        g_wq[jb] = _unslot_cols(_matmul_tn(hn, dqs, f"attn_dwq_{jb}"))

    dh, hk, dkb, dvb, dfb, g_kvn, g_bf = _kvf_bwd(dh, h_kv, dk_parts[0], dk_parts[1], dv_parts[0], dv_parts[1],
                                                  dq_parts[0], dq_parts[1], flog, kv_norm.reshape(1, d), wk_x, wv, wf,
                                                  sel_q, sel_k, "kvf_bwd")
    g_kvf = jnp.concatenate([_unslot_cols(_matmul_tn(hk, dkb, "kvf_dwk")), _matmul_tn(hk, dvb, "kvf_dwv"),
                             _matmul_tn(hk, dfb, "kvf_dwf")[:, :n_heads]], axis=1)
    kvf_send = g_kvf.reshape(d, N_CHIPS, kvf_cols).transpose(1, 0, 2)
    group = [w1_entry(n_a + jb, 0, jb, n_b) for jb in range(n_b)]
    group += [w2_entry(n_a + jb, 1, jb, n_b) for jb in range(n_b)]
    group.append((kvf_send, 2, (N_CHIPS, d, kvf_cols), BF16, slot, slot,
                  lax.dynamic_index_in_dim(kvf_send, chip, 0, keepdims=True), (chip, 0, 0)))
    group += [rows_entry(g_wq[jb], 3, jb) for jb in range(n_b)]
    group += [rows_entry(g_wo[jb], 4, jb) for jb in range(n_b)]
    tie = reduce_start(group, "reduce_attn")

    for i in reversed(range(n_a)):
        rec = saved[i]
        if i == 0:
            group = [w1_entry(l, 0, l - 1, n_a - 1) for l in range(1, n_a)]
            group += [w2_entry(l, 1, l - 1, n_a - 1) for l in range(1, n_a)]
            group += [glu_entry(l, 2, l - 1, n_a - 1) for l in range(1, n_a)]
            tie = reduce_start(group, "reduce_s5")
        dh = mlp_back(dh, i, rec, tie if i in (0, n_a - 1) else None)
        dy, z, dzw = _s5_post_bwd(dh, rec["y"], rec["zw"], layer_glu(i), f"s5_glu_bwd_{i}")
        g_glu[i] = _matmul_tn(z, dzw, f"s5_dwglu_{i}")
        skip_gain = rec["dskip"]
        if i == 0:
            skip_gain = skip_gain + reduce_start([w1_entry(0, 0, 0, 1), w2_entry(0, 1, 0, 1), glu_entry(0, 2, 0, 1)],
                                                 "reduce_first")
        lbr, lbi = rec["lam"]
        bblk_t = rec["bblk"].transpose(0, 2, 1).astype(BF16)
        cblk_t = rec["cblk"].transpose(0, 2, 1).astype(BF16)
        du, glam8, gd8, gbblk, gcblk = _scan_bwd(dy, rec["u"], rec["states"], bblk_t, cblk_t,
                                                 _scan_tables(lbr, lbi, nj, True), skip_gain, f"s5_scan_bwd_{i}")
        dh, g_mix[i] = _norm_bwd_add(dh, du, rec["h0"], mix_norm[i:i + 1], f"s5_norm_bwd_{i}")
        glam = jnp.sum(glam8, axis=1)
        n_st = glam.shape[1] // 2
        g_lbr = glam[:, :n_st].reshape(n_groups, p_state)
        g_lbi = glam[:, n_st:].reshape(n_groups, p_state)
        g_bbr, g_bbi = _block_diag_in_grad(gbblk, nj, p_state, SSM_GROUP)
        g_cre, g_cim = _block_diag_out_grad(gcblk, nj, p_state, SSM_GROUP)
        _, pull = jax.vjp(_ssm_discretise, ssm_log_dt[i], ssm_a_re[i], ssm_a_im[i], ssm_b_re[i], ssm_b_im[i])
        g_ldt, g_are, g_aim, g_bre, g_bim = pull((g_lbr, g_lbi, g_bbr, g_bbi))
        g_ssm[i] = dict(log_dt=g_ldt, a_re=g_are, a_im=g_aim, b_re=g_bre, b_im=g_bim, c_re=g_cre, c_im=g_cim,
                        d=jnp.sum(gd8, axis=1).reshape(d))
    grad_x = dh[None]

    def stack_small(key):
        return jnp.stack([g_ssm[i][key] for i in range(n_a)])

    small_grads = [jnp.concatenate(g_mix, axis=0), jnp.concatenate(g_mlp, axis=0), stack_small("log_dt"),
                   stack_small("a_re"), stack_small("a_im"), stack_small("b_re"), stack_small("b_im"),
                   stack_small("c_re"), stack_small("c_im"), stack_small("d"), g_kvn.reshape(d),
                   g_bf[0, :n_heads], g_final.reshape(d)]
    small_w = [mix_norm, mlp_norm, ssm_log_dt, ssm_a_re, ssm_a_im, ssm_b_re, ssm_b_im, ssm_c_re, ssm_c_im,
               ssm_d, kv_norm, b_f, final_norm]
    small_m = [m_mix_norm, m_mlp_norm, m_ssm_log_dt, m_ssm_a_re, m_ssm_a_im, m_ssm_b_re, m_ssm_b_im, m_ssm_c_re,
               m_ssm_c_im, m_ssm_d, m_kv_norm, m_b_f, m_final_norm]
    small_v = [v_mix_norm, v_mlp_norm, v_ssm_log_dt, v_ssm_a_re, v_ssm_a_im, v_ssm_b_re, v_ssm_b_im, v_ssm_c_re,
               v_ssm_c_im, v_ssm_d, v_kv_norm, v_b_f, v_final_norm]
    skip_at = 9

    def widen_skip(part):
        return lax.dynamic_update_slice(jnp.zeros((n_a, d), F32), part, (0, chip * ds4))

    small_shapes = [a.shape for a in small_grads]
    pcols = 1024 if d >= 1024 else LANES
    g_small = _pack_small(small_grads, pcols)
    expand = lambda lst: _pack_small([widen_skip(a) if n == skip_at else a for n, a in enumerate(lst)], pcols)
    w_small, m_small, v_small = expand(small_w), expand(small_m), expand(small_v)
    srows = g_small.shape[0]

    assert n_b == 2
    reduce_start([(g_small, 0, (N_CHIPS, srows, pcols), F32, whole, slot, g_small[None], (chip, 0, 0))], "reduce_small")
    landed = [_exchange_wait(sems, srcs, lands, plan, dh, name) for (sems, srcs, lands, plan, name) in red_waits]
    (a_w1, a_w2, a_kvf, a_wq, a_wo), (s_w1, s_w2, s_glu), (f_w1, f_w2, f_glu), (r_small,) = landed

    def chip_sum(r, name):
        return _sum_chips(r.reshape(N_CHIPS, -1, r.shape[-1]), name)

    sums = [jnp.concatenate([chip_sum(f_w1, "sum_w1_first"), chip_sum(s_w1, "sum_w1_s5"),
                             chip_sum(a_w1, "sum_w1_attn")], axis=0),
            jnp.concatenate([chip_sum(f_w2, "sum_w2_first"), chip_sum(s_w2, "sum_w2_s5"),
                             chip_sum(a_w2, "sum_w2_attn")], axis=0),
            jnp.concatenate([chip_sum(f_glu, "sum_glu_first"), chip_sum(s_glu, "sum_glu_s5")], axis=0),
            chip_sum(a_kvf, "sum_kvf"), chip_sum(a_wq, "sum_wq"), chip_sum(a_wo, "sum_wo"),
            chip_sum(r_small, "sum_small")]
    others = _core_exchange(sums, "reduce_cores")

    def two(a):
        return a.reshape(-1, a.shape[-1])

    big_w = [(mlp_w1, m_mlp_w1, v_mlp_w1), (mlp_w2, m_mlp_w2, v_mlp_w2), (ssm_w_glu, m_ssm_w_glu, v_ssm_w_glu),
             (w_kvf, m_w_kvf, v_w_kvf), (attn_wq, m_attn_wq, v_attn_wq), (attn_wo, m_attn_wo, v_attn_wo)]
    big_out = []
    for n, (w, m, v) in enumerate(big_w):
        res = _adamw(sums[n], others[n], two(w), two(m), two(v), f"adamw_{n}")
        big_out.append([r.reshape(w.shape) for r in res])
    small_out = _adamw(sums[6], others[6], w_small, m_small, v_small, "adamw_small")

    def narrow_skip(a):
        return lax.dynamic_slice(a, (0, chip * ds4), (n_a, ds4))

    unpacked = []
    for packed in small_out:
        parts = _unpack_small(packed, small_shapes, pcols)
        parts[skip_at] = narrow_skip(parts[skip_at])
        unpacked.append(parts)

    order = ["mix_norm", "mlp_norm", "mlp_w1", "mlp_w2", "ssm_log_dt", "ssm_a_re", "ssm_a_im", "ssm_b_re",
             "ssm_b_im", "ssm_c_re", "ssm_c_im", "ssm_d", "ssm_w_glu", "kv_norm", "w_kvf", "b_f", "attn_wq",
             "attn_wo", "final_norm"]
    small_names = ["mix_norm", "mlp_norm", "ssm_log_dt", "ssm_a_re", "ssm_a_im", "ssm_b_re", "ssm_b_im",
                   "ssm_c_re", "ssm_c_im", "ssm_d", "kv_norm", "b_f", "final_norm"]
    big_names = ["mlp_w1", "mlp_w2", "ssm_w_glu", "w_kvf", "attn_wq", "attn_wo"]
    outs = [loss, grad_x]
    for kind in range(4):
        for name in order:
            if name in big_names:
                outs.append(big_out[big_names.index(name)][kind])
            else:
                outs.append(unpacked[kind][small_names.index(name)])
    return tuple(outs)
```

```python
import functools
import math

import jax
import jax.numpy as jnp
from jax import lax
from jax.experimental import pallas as pl
from jax.experimental.pallas import tpu as pltpu

F32 = jnp.float32
BF16 = jnp.bfloat16

RMS_EPS = 1e-6
SSM_GROUP = 16
SSM_STATE = 64
HEAD_DIM = 64
HEAD_PAIR = 2 * HEAD_DIM
LANES = 128
SUBLANES = 8
N_CHIPS = 4
ADAM_LR = 0.001
ADAM_B1 = 0.9
ADAM_B2 = 0.999
ADAM_EPS = 1e-08
ADAM_WD = 0.01
ADAM_STEP = 10
GELU_C = math.sqrt(2.0 / math.pi)
GELU_A = 0.044715
NEG = -1e30
LN2 = math.log(2.0)
LOG2E = 1.0 / LN2
VMEM_LIMIT = 56 * 1024 * 1024
MESH = pl.DeviceIdType.MESH

NT_DIMS = (((1,), (1,)), ((), ()))
TN_DIMS = (((0,), (0,)), ((), ()))


def _cp(*sem):
    return pltpu.CompilerParams(dimension_semantics=sem if sem else None, vmem_limit_bytes=VMEM_LIMIT)


def _zero_idx(nd, *_):
    return (0,) * nd


def _tile(n, t):
    if n <= t:
        return n
    for cand in range(t - t % SUBLANES, 0, -SUBLANES):
        if n % cand == 0:
            return cand
    raise ValueError((n, t))


def _rms_fwd(h, g):
    r = lax.rsqrt(jnp.mean(h * h, axis=-1, keepdims=True) + RMS_EPS)
    hhat = h * r
    return hhat * g, hhat, r


def _rms_bwd(du, hhat, r, g):
    dhh = du * g
    dh = r * (dhh - hhat * jnp.mean(dhh * hhat, axis=-1, keepdims=True))
    return dh, du * hhat


def _sigmoid(x):
    return 1.0 / (1.0 + jnp.exp(-x))


def _gelu(x):
    t = jnp.tanh(GELU_C * (x + GELU_A * x * x * x))
    return 0.5 * x * (1.0 + t)


def _gelu_grad(x):
    t = jnp.tanh(GELU_C * (x + GELU_A * x * x * x))
    return 0.5 * (1.0 + t) + 0.5 * x * (1.0 - t * t) * GELU_C * (1.0 + 3.0 * GELU_A * x * x)


def _row_fold(x):
    tm, w = x.shape
    return jnp.sum(x.reshape(tm // SUBLANES, SUBLANES, w), axis=0)


def _split3(x):
    hi = x.astype(BF16)
    r1 = x - hi.astype(F32)
    mid = r1.astype(BF16)
    lo = (r1 - mid.astype(F32)).astype(BF16)
    return hi, mid, lo


def _exact_dot(ones_mat, x):
    hi, mid, lo = _split3(x)
    d = functools.partial(jnp.dot, preferred_element_type=F32)
    return d(ones_mat, hi) + d(ones_mat, mid) + d(ones_mat, lo)


def _rows_call(body, name, tm, row_ins, const_ins, row_outs, acc_outs=(), scratch=(), reverse=False, col_outs=()):
    n = row_ins[0].shape[0]
    nb = n // tm
    if reverse:
        ridx = lambda i: (nb - 1 - i, 0)
        cidx = lambda i: (0, nb - 1 - i)
    else:
        ridx = lambda i: (i, 0)
        cidx = lambda i: (0, i)
    in_specs = [pl.BlockSpec((tm, a.shape[1]), ridx) for a in row_ins]
    in_specs += [pl.BlockSpec(a.shape, functools.partial(_zero_idx, a.ndim), pipeline_mode=pl.Buffered(1))
                 for a in const_ins]
    out_shape = [jax.ShapeDtypeStruct((n, w), dt) for (w, dt) in row_outs]
    out_shape += [jax.ShapeDtypeStruct(s, dt) for (s, dt) in acc_outs]
    out_shape += [jax.ShapeDtypeStruct((r, n), dt) for (r, dt) in col_outs]
    out_specs = [pl.BlockSpec((tm, w), ridx) for (w, dt) in row_outs]
    out_specs += [pl.BlockSpec(s, functools.partial(_zero_idx, len(s))) for (s, dt) in acc_outs]
    out_specs += [pl.BlockSpec((r, tm), cidx) for (r, dt) in col_outs]
    return pl.pallas_call(
        body, name=name, grid=(nb,), in_specs=in_specs, out_specs=out_specs, out_shape=out_shape,
        scratch_shapes=list(scratch), compiler_params=_cp("arbitrary"),
    )(*row_ins, *const_ins)


def _norm_fwd(h, g, name):
    n, d = h.shape
    tm = _tile(n, 512)

    def body(h_ref, g_ref, u_ref):
        u_ref[...] = _rms_fwd(h_ref[...], g_ref[...])[0]

    return _rows_call(body, name, tm, [h], [g], [(d, F32)])[0]


def _norm_bwd_add(dh, du, h, g, name):
    n, d = h.shape
    tm = _tile(n, 512)
    nb = n // tm

    def body(dh_ref, du_ref, h_ref, g_ref, o_ref, dg_ref, acc):
        i = pl.program_id(0)

        @pl.when(i == 0)
        def _():
            acc[...] = jnp.zeros_like(acc)

        gain = g_ref[...]
        _, hhat, r = _rms_fwd(h_ref[...], gain)
        dhn, dgr = _rms_bwd(du_ref[...], hhat, r, gain)
        o_ref[...] = dh_ref[...] + dhn
        acc[...] += _row_fold(dgr)

        @pl.when(i == nb - 1)
        def _():
            dg_ref[...] = jnp.sum(acc[...], axis=0, keepdims=True)

    return _rows_call(body, name, tm, [dh, du, h], [g], [(d, F32)], [((1, d), F32)],
                      [pltpu.VMEM((SUBLANES, d), F32)])


def _mlp_fwd(h, g, w1, w2, name):
    n, d = h.shape
    ff = w1.shape[1]
    tm = _tile(n, 256)
    fc = _tile(ff, 1024)

    def body(h_ref, g_ref, w1_ref, w2_ref, o_ref, ap_ref):
        hin = h_ref[...]
        hb = _rms_fwd(hin, g_ref[...])[0].astype(BF16)
        acc = hin
        for c in range(ff // fc):
            cs = slice(c * fc, (c + 1) * fc)
            ap = jnp.dot(hb, w1_ref[:, cs], preferred_element_type=F32)
            ap_ref[:, cs] = ap.astype(BF16)
            rl = jnp.maximum(ap, 0.0)
            acc = acc + jnp.dot((rl * rl).astype(BF16), w2_ref[cs, :], preferred_element_type=F32)
        o_ref[...] = acc

    return _rows_call(body, name, tm, [h], [g, w1, w2], [(d, F32), (ff, BF16)])


def _mlp_bwd(dh, h, ap, g, w1, w2, name):
    n, d = h.shape
    ff = w1.shape[1]
    tm = _tile(n, 256)
    nb = n // tm
    fc = _tile(ff, 1024)

    def body(dh_ref, h_ref, ap_ref, g_ref, w1_ref, w2_ref, o_ref, hm_ref, a_ref, dap_ref, dg_ref, acc):
        i = pl.program_id(0)

        @pl.when(i == 0)
        def _():
            acc[...] = jnp.zeros_like(acc)

        gain = g_ref[...]
        dhv = dh_ref[...]
        hm, hhat, r = _rms_fwd(h_ref[...], gain)
        hm_ref[...] = hm.astype(BF16)
        dhb = dhv.astype(BF16)
        dhm = jnp.zeros((tm, d), F32)
        for c in range(ff // fc):
            cs = slice(c * fc, (c + 1) * fc)
            rl = jnp.maximum(ap_ref[:, cs].astype(F32), 0.0)
            a_ref[:, cs] = (rl * rl).astype(BF16)
            da = lax.dot_general(dhb, w2_ref[cs, :], NT_DIMS, preferred_element_type=F32)
            dap = (da * (2.0 * rl)).astype(BF16)
            dap_ref[:, cs] = dap
            dhm = dhm + lax.dot_general(dap, w1_ref[:, cs], NT_DIMS, preferred_element_type=F32)
        dhn, dgr = _rms_bwd(dhm, hhat, r, gain)
        o_ref[...] = dhv + dhn
        acc[...] += _row_fold(dgr)

        @pl.when(i == nb - 1)
        def _():
            dg_ref[...] = jnp.sum(acc[...], axis=0, keepdims=True)

    return _rows_call(body, name, tm, [dh, h, ap], [g, w1, w2],
                      [(d, F32), (d, BF16), (ff, BF16), (ff, BF16)], [((1, d), F32)],
                      [pltpu.VMEM((SUBLANES, d), F32)])


def _s5_post_fwd(h, y, w_glu, name):
    n, d = h.shape
    tm = _tile(n, 512)

    def body(h_ref, y_ref, w_ref, o_ref, zw_ref):
        z = _gelu(y_ref[...]).astype(BF16)
        zw = jnp.dot(z, w_ref[...], preferred_element_type=F32)
        zw_ref[...] = zw.astype(BF16)
        o_ref[...] = h_ref[...] + zw[:, :d] * _sigmoid(zw[:, d:])

    return _rows_call(body, name, tm, [h, y], [w_glu], [(d, F32), (2 * d, BF16)])


def _s5_post_bwd(dh, y, zw, w_glu, name):
    n, d = dh.shape
    tm = _tile(n, 512)

    def body(dh_ref, y_ref, zw_ref, w_ref, dy_ref, z_ref, dzw_ref):
        dhv = dh_ref[...]
        yv = y_ref[...]
        val = zw_ref[:, :d].astype(F32)
        sg = _sigmoid(zw_ref[:, d:].astype(F32))
        dzw = jnp.concatenate([dhv * sg, dhv * val * sg * (1.0 - sg)], axis=1).astype(BF16)
        dzw_ref[...] = dzw
        dz = lax.dot_general(dzw, w_ref[...], NT_DIMS, preferred_element_type=F32)
        dy_ref[...] = dz * _gelu_grad(yv)
        z_ref[...] = _gelu(yv).astype(BF16)

    return _rows_call(body, name, tm, [dh, y, zw], [w_glu], [(d, F32), (d, BF16), (2 * d, BF16)])


def _q_fwd(h, g, wq_x, ones_x, name):
    n, d = h.shape
    tm = _tile(n, 512)
    scale = LOG2E * HEAD_DIM ** -0.5

    def body(h_ref, g_ref, w_ref, one_ref, q_ref):
        hb = _rms_fwd(h_ref[...], g_ref[...])[0].astype(BF16)
        q_ref[...] = (jnp.dot(hb, w_ref[...], preferred_element_type=F32) * scale + one_ref[...]).astype(BF16)

    return _rows_call(body, name, tm, [h], [g, wq_x, ones_x], [(wq_x.shape[1], BF16)])[0]


def _q_bwd(dh, h, dq, g, wq, name):
    n, d = h.shape
    tm = _tile(n, 512)
    nb = n // tm
    scale = HEAD_DIM ** -0.5

    def body(dh_ref, h_ref, dq_ref, g_ref, w_ref, o_ref, hn_ref, dqs_ref, dg_ref, acc):
        i = pl.program_id(0)

        @pl.when(i == 0)
        def _():
            acc[...] = jnp.zeros_like(acc)

        gain = g_ref[...]
        hn, hhat, r = _rms_fwd(h_ref[...], gain)
        hn_ref[...] = hn.astype(BF16)
        dqs = (dq_ref[...] * scale).astype(BF16)
        dqs_ref[...] = dqs
        dhn = lax.dot_general(dqs, w_ref[...], NT_DIMS, preferred_element_type=F32)
        dhi, dgr = _rms_bwd(dhn, hhat, r, gain)
        o_ref[...] = dh_ref[...] + dhi
        acc[...] += _row_fold(dgr)

        @pl.when(i == nb - 1)
        def _():
            dg_ref[...] = jnp.sum(acc[...], axis=0, keepdims=True)

    return _rows_call(body, name, tm, [dh, h, dq], [g, wq], [(d, F32), (d, BF16), (wq.shape[1], BF16)],
                      [((1, d), F32)], [pltpu.VMEM((SUBLANES, d), F32)])


def _o_fwd(h, o, wo, name):
    n, d = h.shape
    tm = _tile(n, 512)

    def body(h_ref, o_ref, w_ref, out_ref):
        out_ref[...] = h_ref[...] + jnp.dot(o_ref[...], w_ref[...], preferred_element_type=F32)

    return _rows_call(body, name, tm, [h, o], [wo], [(d, F32)])[0]


def _exact_dot_nt(ones_mat, x):
    hi, mid, lo = _split3(x)
    d = functools.partial(lax.dot_general, dimension_numbers=NT_DIMS, preferred_element_type=F32)
    return d(ones_mat, hi) + d(ones_mat, mid) + d(ones_mat, lo)


def _o_bwd(dh, o, wo, head_sel, name):
    n, d = dh.shape
    tm = _tile(n, 512)

    def body(dh_ref, o_ref, w_ref, e_ref, do_ref, dl_ref):
        do = lax.dot_general(dh_ref[...].astype(BF16), w_ref[...], NT_DIMS, preferred_element_type=F32).astype(BF16)
        do_ref[...] = do
        dl_ref[...] = _exact_dot_nt(e_ref[...], do.astype(F32) * o_ref[...].astype(F32))

    return _rows_call(body, name, tm, [dh, o], [wo, head_sel], [(wo.shape[0], BF16)],
                      col_outs=[(head_sel.shape[0], F32)])


def _exact_dot_rhs(x, ones_mat):
    hi, mid, lo = _split3(x)
    d = functools.partial(jnp.dot, preferred_element_type=F32)
    return d(hi, ones_mat) + d(mid, ones_mat) + d(lo, ones_mat)


def _kvf_fwd(h, g, wk, wv, wf, bf, k_ones, v_ones, spread, name):
    n, d = h.shape
    tm = _tile(n, 512)

    def body(h_ref, g_ref, wk_ref, wv_ref, wf_ref, bf_ref, ko_ref, vo_ref, sp_ref,
             k_ref, v_ref, fl_ref, cum_ref, rep_ref, carry):
        i = pl.program_id(0)

        @pl.when(i == 0)
        def _():
            carry[...] = jnp.zeros_like(carry)

        hb = _rms_fwd(h_ref[...], g_ref[...])[0].astype(BF16)
        k_ref[...] = (jnp.dot(hb, wk_ref[...], preferred_element_type=F32) + ko_ref[...]).astype(BF16)
        v_ref[...] = (jnp.dot(hb, wv_ref[...], preferred_element_type=F32) + vo_ref[...]).astype(BF16)
        fl = jnp.dot(hb, wf_ref[...], preferred_element_type=F32) + bf_ref[...]
        fl_ref[...] = fl
        logf = jnp.minimum(fl, 0.0) - jnp.log(1.0 + jnp.exp(-jnp.abs(fl)))
        rows = lax.broadcasted_iota(jnp.int32, (tm, tm), 0)
        cols = lax.broadcasted_iota(jnp.int32, (tm, tm), 1)
        lower = (rows >= cols).astype(BF16)
        cum = _exact_dot(lower, logf) + carry[0:1, :]
        cum_ref[...] = cum
        rep_ref[...] = _exact_dot_rhs(cum * LOG2E, sp_ref[...])
        carry[...] = jnp.broadcast_to(cum[tm - 1:tm, :], carry.shape)

    return _rows_call(body, name, tm, [h], [g, wk, wv, wf, bf, k_ones, v_ones, spread],
                      [(wk.shape[1], BF16), (wv.shape[1], BF16), (LANES, F32), (LANES, F32), (spread.shape[1], F32)],
                      scratch=[pltpu.VMEM((SUBLANES, LANES), F32)])


def _kvf_bwd(dh, h, dk1, dk2, dv1, dv2, dq1, dq2, fl, g, wk, wv, wf, sel_q, sel_k, name):
    n, d = h.shape
    tm = _tile(n, 256)
    nb = n // tm

    def body(dh_ref, h_ref, dk1_ref, dk2_ref, dv1_ref, dv2_ref, dq1_ref, dq2_ref, fl_ref,
             g_ref, wk_ref, wv_ref, wf_ref, sq_ref, sk_ref,
             o_ref, hk_ref, dk_ref, dv_ref, df_ref, dg_ref, db_ref, acc, bacc, carry):
        i = pl.program_id(0)

        @pl.when(i == 0)
        def _():
            acc[...] = jnp.zeros_like(acc)
            bacc[...] = jnp.zeros_like(bacc)
            carry[...] = jnp.zeros_like(carry)

        dkx = dk1_ref[...] + dk2_ref[...]
        dcum = _exact_dot_rhs(dq1_ref[...] + dq2_ref[...], sq_ref[...]) - _exact_dot_rhs(dkx, sk_ref[...])
        rows = lax.broadcasted_iota(jnp.int32, (tm, tm), 0)
        cols = lax.broadcasted_iota(jnp.int32, (tm, tm), 1)
        upper = (rows <= cols).astype(BF16)
        dlogf = _exact_dot(upper, dcum) + carry[0:1, :]
        carry[...] = jnp.broadcast_to(dlogf[0:1, :], carry.shape)
        df = dlogf / (1.0 + jnp.exp(fl_ref[...]))
        dfb = df.astype(BF16)
        df_ref[...] = dfb
        bacc[...] += _row_fold(df)
        dkb = (dkx * LN2).astype(BF16)
        dvb = (dv1_ref[...] + dv2_ref[...]).astype(BF16)
        dk_ref[...] = dkb
        dv_ref[...] = dvb
        gain = g_ref[...]
        hk, hhat, r = _rms_fwd(h_ref[...], gain)
        hk_ref[...] = hk.astype(BF16)
        dhk = lax.dot_general(dkb, wk_ref[...], NT_DIMS, preferred_element_type=F32)
        dhk = dhk + lax.dot_general(dvb, wv_ref[...], NT_DIMS, preferred_element_type=F32)
        dhk = dhk + lax.dot_general(dfb, wf_ref[...], NT_DIMS, preferred_element_type=F32)
        dhi, dgr = _rms_bwd(dhk, hhat, r, gain)
        o_ref[...] = dh_ref[...] + dhi
        acc[...] += _row_fold(dgr)

        @pl.when(i == nb - 1)
        def _():
            dg_ref[...] = jnp.sum(acc[...], axis=0, keepdims=True)
            db_ref[...] = jnp.sum(bacc[...], axis=0, keepdims=True)

    return _rows_call(body, name, tm, [dh, h, dk1, dk2, dv1, dv2, dq1, dq2, fl], [g, wk, wv, wf, sel_q, sel_k],
                      [(d, F32), (d, BF16), (wk.shape[1], BF16), (wv.shape[1], BF16), (LANES, BF16)],
                      [((1, d), F32), ((1, LANES), F32)],
                      [pltpu.VMEM((SUBLANES, d), F32), pltpu.VMEM((SUBLANES, LANES), F32),
                       pltpu.VMEM((SUBLANES, LANES), F32)], reverse=True)


def _loss_head(h, target, g, name):
    n, d = h.shape
    tm = _tile(n, 512)
    nb = n // tm

    def body(h_ref, t_ref, g_ref, dh_ref, loss_ref, dg_ref, lacc, gacc):
        i = pl.program_id(0)

        @pl.when(i == 0)
        def _():
            lacc[...] = jnp.zeros_like(lacc)
            gacc[...] = jnp.zeros_like(gacc)

        gain = g_ref[...]
        yv, hhat, r = _rms_fwd(h_ref[...], gain)
        e = yv - t_ref[...]
        lacc[...] += _row_fold(e * e)
        dhv, dgr = _rms_bwd(e * (1.0 / d), hhat, r, gain)
        dh_ref[...] = dhv
        gacc[...] += _row_fold(dgr)

        @pl.when(i == nb - 1)
        def _():
            loss_ref[...] = jnp.full((1, LANES), jnp.sum(lacc[...]) * (0.5 / d), F32)
            dg_ref[...] = jnp.sum(gacc[...], axis=0, keepdims=True)

    return _rows_call(body, name, tm, [h, target], [g], [(d, F32)], [((1, LANES), F32), ((1, d), F32)],
                      [pltpu.VMEM((SUBLANES, d), F32), pltpu.VMEM((SUBLANES, d), F32)])


def _matmul_tn(a, b, name, out_dtype=BF16):
    l, m = a.shape
    n = b.shape[1]
    tl = _tile(l, 2048)
    tmm = _tile(m, 512)
    tn = _tile(n, 1024)
    nl = l // tl

    def body(a_ref, b_ref, o_ref, acc):
        k = pl.program_id(2)

        @pl.when(k == 0)
        def _():
            acc[...] = jnp.zeros_like(acc)

        acc[...] += lax.dot_general(a_ref[...].astype(BF16), b_ref[...].astype(BF16), TN_DIMS,
                                    preferred_element_type=F32)

        @pl.when(k == nl - 1)
        def _():
            o_ref[...] = acc[...].astype(out_dtype)

    return pl.pallas_call(
        body, name=name, grid=(m // tmm, n // tn, nl),
        in_specs=[pl.BlockSpec((tl, tmm), lambda i, j, k: (k, i)), pl.BlockSpec((tl, tn), lambda i, j, k: (k, j))],
        out_specs=pl.BlockSpec((tmm, tn), lambda i, j, k: (i, j)),
        out_shape=jax.ShapeDtypeStruct((m, n), out_dtype),
        scratch_shapes=[pltpu.VMEM((tmm, tn), F32)],
        compiler_params=_cp("parallel", "parallel", "arbitrary"),
    )(a, b)


def _scan_fwd(u, bblk, cblk, tabs, dskip, name):
    l, d = u.shape
    nj, gb, n2 = bblk.shape
    n = n2 // 2
    tm = _tile(l, 512)
    nb = l // tm
    nsub = 4 if tm % (4 * SUBLANES) == 0 else 1
    sub = tm // nsub

    def body(u_ref, b_ref, c_ref, t_ref, d_ref, st_ref, y_ref, carry):
        i = pl.program_id(1)

        @pl.when(i == 0)
        def _():
            carry[...] = jnp.zeros_like(carry)

        def step(rb, c):
            cr, ci = c
            rows = pl.ds(rb * SUBLANES, SUBLANES)
            xr = st_ref[rows, 0:n]
            xi = st_ref[rows, n:n2]
            for lvl, sh in enumerate((1, 2, 4)):
                ar = t_ref[0, 2 * lvl]
                ai = t_ref[0, 2 * lvl + 1]
                sr = pltpu.roll(xr, sh, 0)
                si = pltpu.roll(xi, sh, 0)
                xr, xi = xr + ar * sr - ai * si, xi + ar * si + ai * sr
            lr = t_ref[0, 6]
            li = t_ref[0, 7]
            xr, xi = xr + lr * cr - li * ci, xi + lr * ci + li * cr
            st_ref[rows, 0:n] = xr
            st_ref[rows, n:n2] = xi
            return (jnp.broadcast_to(xr[SUBLANES - 1:SUBLANES, :], (SUBLANES, n)),
                    jnp.broadcast_to(xi[SUBLANES - 1:SUBLANES, :], (SUBLANES, n)))

        def project_in(sb):
            rows = slice(sb * sub, (sb + 1) * sub)
            st_ref[rows, :] = jnp.dot(u_ref[rows, :].astype(BF16), b_ref[0], preferred_element_type=F32)

        def project_out(sb):
            rows = slice(sb * sub, (sb + 1) * sub)
            y_ref[rows, :] = (jnp.dot(st_ref[rows, :].astype(BF16), c_ref[0], preferred_element_type=F32)
                              + d_ref[...] * u_ref[rows, :])

        c = (carry[:, 0:n], carry[:, n:n2])
        project_in(0)
        for sb in range(nsub):
            if sb + 1 < nsub:
                project_in(sb + 1)
            for rb in range(sb * sub // SUBLANES, (sb + 1) * sub // SUBLANES):
                c = step(rb, c)
            project_out(sb)
        carry[:, 0:n] = c[0]
        carry[:, n:n2] = c[1]

    return pl.pallas_call(
        body, name=name, grid=(nj, nb),
        in_specs=[pl.BlockSpec((tm, gb), lambda j, i: (i, j)),
                  pl.BlockSpec((1, gb, n2), lambda j, i: (j, 0, 0)),
                  pl.BlockSpec((1, n2, gb), lambda j, i: (j, 0, 0)),
                  pl.BlockSpec((1, 8, SUBLANES, n), lambda j, i: (j, 0, 0, 0)),
                  pl.BlockSpec((1, gb), lambda j, i: (0, j))],
        out_specs=[pl.BlockSpec((tm, n2), lambda j, i: (i, j)), pl.BlockSpec((tm, gb), lambda j, i: (i, j))],
        out_shape=[jax.ShapeDtypeStruct((l, nj * n2), F32), jax.ShapeDtypeStruct((l, d), F32)],
        scratch_shapes=[pltpu.VMEM((SUBLANES, n2), F32)],
        compiler_params=_cp("parallel", "arbitrary"),
    )(u, bblk, cblk, tabs, dskip)


def _scan_bwd(dy, u, states, bblk_t, cblk_t, tabs, dskip, name):
    l, d = u.shape
    nj, n2, gb = bblk_t.shape
    n = n2 // 2
    tm = _tile(l, 512)
    nb = l // tm
    nsub = 4 if tm % (4 * SUBLANES) == 0 else 1
    sub = tm // nsub

    def body(dy_ref, u_ref, st_ref, prev_ref, bt_ref, ct_ref, t_ref, d_ref,
             du_ref, glam_ref, gd_ref, gb_ref, gc_ref, gx, carry):
        i = pl.program_id(1)
        ib = nb - 1 - i

        @pl.when(i == 0)
        def _():
            carry[...] = jnp.zeros_like(carry)
            glam_ref[...] = jnp.zeros_like(glam_ref)
            gd_ref[...] = jnp.zeros_like(gd_ref)
            gb_ref[...] = jnp.zeros_like(gb_ref)
            gc_ref[...] = jnp.zeros_like(gc_ref)

        last_row = lax.broadcasted_iota(jnp.int32, (SUBLANES, n), 0) == SUBLANES - 1

        def block(rows, xp_r, xp_i, c):
            cr, ci = c
            gr = gx[rows, 0:n]
            gi = gx[rows, n:n2]
            for lvl, sh in enumerate((1, 2, 4)):
                ar = t_ref[0, 2 * lvl]
                ai = t_ref[0, 2 * lvl + 1]
                sr = pltpu.roll(gr, SUBLANES - sh, 0)
                si = pltpu.roll(gi, SUBLANES - sh, 0)
                gr, gi = gr + ar * sr - ai * si, gi + ar * si + ai * sr
            lr = t_ref[0, 6]
            li = t_ref[0, 7]
            gr, gi = gr + lr * cr - li * ci, gi + lr * ci + li * cr
            gx[rows, 0:n] = gr
            gx[rows, n:n2] = gi
            xs_r = pltpu.roll(jnp.where(last_row, xp_r, st_ref[rows, 0:n]), 1, 0)
            xs_i = pltpu.roll(jnp.where(last_row, xp_i, st_ref[rows, n:n2]), 1, 0)
            glam_ref[0, :, 0:n] += gr * xs_r + gi * xs_i
            glam_ref[0, :, n:n2] += gi * xs_r - gr * xs_i
            return (jnp.broadcast_to(gr[0:1, :], (SUBLANES, n)), jnp.broadcast_to(gi[0:1, :], (SUBLANES, n)))

        def project_in(sb):
            rows = slice(sb * sub, (sb + 1) * sub)
            gx[rows, :] = jnp.dot(dy_ref[rows, :].astype(BF16), ct_ref[0], preferred_element_type=F32)

        def project_out(sb):
            rows = slice(sb * sub, (sb + 1) * sub)
            dyv = dy_ref[rows, :]
            uv = u_ref[rows, :]
            gxb = gx[rows, :].astype(BF16)
            du_ref[rows, :] = jnp.dot(gxb, bt_ref[0], preferred_element_type=F32) + d_ref[...] * dyv
            gd_ref[0] += _row_fold(dyv * uv)
            gb_ref[0] += lax.dot_general(uv.astype(BF16), gxb, TN_DIMS, preferred_element_type=F32)
            gc_ref[0] += lax.dot_general(st_ref[rows, :].astype(BF16), dyv.astype(BF16), TN_DIMS,
                                         preferred_element_type=F32)

        live = (ib > 0).astype(F32)
        c = (carry[:, 0:n], carry[:, n:n2])
        project_in(nsub - 1)
        for sb in reversed(range(nsub)):
            if sb > 0:
                project_in(sb - 1)
            for rb in reversed(range(sb * sub // SUBLANES, (sb + 1) * sub // SUBLANES)):
                rows = pl.ds(rb * SUBLANES, SUBLANES)
                if rb > 0:
                    before = pl.ds((rb - 1) * SUBLANES, SUBLANES)
                    c = block(rows, st_ref[before, 0:n], st_ref[before, n:n2], c)
                else:
                    c = block(rows, prev_ref[:, 0:n] * live, prev_ref[:, n:n2] * live, c)
            project_out(sb)
        carry[:, 0:n] = c[0]
        carry[:, n:n2] = c[1]

    rpb = tm // SUBLANES
    return pl.pallas_call(
        body, name=name, grid=(nj, nb),
        in_specs=[pl.BlockSpec((tm, gb), lambda j, i: (nb - 1 - i, j)),
                  pl.BlockSpec((tm, gb), lambda j, i: (nb - 1 - i, j)),
                  pl.BlockSpec((tm, n2), lambda j, i: (nb - 1 - i, j)),
                  pl.BlockSpec((SUBLANES, n2), lambda j, i: (jnp.maximum((nb - 1 - i) * rpb - 1, 0), j)),
                  pl.BlockSpec((1, n2, gb), lambda j, i: (j, 0, 0)),
                  pl.BlockSpec((1, gb, n2), lambda j, i: (j, 0, 0)),
                  pl.BlockSpec((1, 8, SUBLANES, n), lambda j, i: (j, 0, 0, 0)),
                  pl.BlockSpec((1, gb), lambda j, i: (0, j))],
        out_specs=[pl.BlockSpec((tm, gb), lambda j, i: (nb - 1 - i, j)),
                   pl.BlockSpec((1, SUBLANES, n2), lambda j, i: (j, 0, 0)),
                   pl.BlockSpec((1, SUBLANES, gb), lambda j, i: (j, 0, 0)),
                   pl.BlockSpec((1, gb, n2), lambda j, i: (j, 0, 0)),
                   pl.BlockSpec((1, n2, gb), lambda j, i: (j, 0, 0))],
        out_shape=[jax.ShapeDtypeStruct((l, d), F32),
                   jax.ShapeDtypeStruct((nj, SUBLANES, n2), F32),
                   jax.ShapeDtypeStruct((nj, SUBLANES, gb), F32),
                   jax.ShapeDtypeStruct((nj, gb, n2), F32),
                   jax.ShapeDtypeStruct((nj, n2, gb), F32)],
        scratch_shapes=[pltpu.VMEM((tm, n2), F32), pltpu.VMEM((SUBLANES, n2), F32)],
        compiler_params=_cp("parallel", "arbitrary"),
    )(dy, u, states, states, bblk_t, cblk_t, tabs, dskip)


HEAD_SLOT = 128
PAIR_SLOT = 2 * HEAD_SLOT
ATTN_TILE = 1024
LANE_ROWSUM_P = HEAD_DIM
LANE_COLSUM_DS = HEAD_DIM
LANE_ROWSUM_DS = HEAD_DIM + 1


def _slot_cols(w):
    r, c = w.shape
    nh = c // HEAD_DIM
    return jnp.pad(w.reshape(r, nh, HEAD_DIM), ((0, 0), (0, 0), (0, HEAD_SLOT - HEAD_DIM))).reshape(r, nh * HEAD_SLOT)


def _unslot_cols(w):
    r, c = w.shape
    nh = c // HEAD_SLOT
    return w.reshape(r, nh, HEAD_SLOT)[:, :, :HEAD_DIM].reshape(r, nh * HEAD_DIM)


def _slot_ones(nh, lane):
    return jnp.tile((jnp.arange(HEAD_SLOT) == lane).astype(F32), nh).reshape(1, nh * HEAD_SLOT)


def _causal_tiles(n, by_query):
    if by_query:
        tiles = [(i, j) for i in range(n) for j in range(i + 1)]
    else:
        tiles = [(i, j) for j in range(n) for i in range(j, n)]
    return (jnp.asarray([t[0] for t in tiles], jnp.int32), jnp.asarray([t[1] for t in tiles], jnp.int32))


def _flash_fwd(qx, kx, vx, f2_rows, name):
    l = qx.shape[0]
    npair = qx.shape[1] // PAIR_SLOT
    d = npair * HEAD_PAIR
    tq = _tile(l, ATTN_TILE)
    tk = tq
    i_of, j_of = _causal_tiles(l // tq, by_query=True)

    def body(i_ref, j_ref, q_ref, k_ref, v_ref, f_ref, o_ref, lse_ref, m_sc, acc_sc):
        t = pl.program_id(1)
        i = i_ref[t]
        j = j_ref[t]

        @pl.when(j == 0)
        def _():
            m_sc[...] = jnp.full_like(m_sc, NEG)
            acc_sc[...] = jnp.zeros_like(acc_sc)

        def tile(on_diagonal):
            for hh in range(2):
                hs = slice(hh * HEAD_SLOT, (hh + 1) * HEAD_SLOT)
                s = lax.dot_general(q_ref[:, hs], k_ref[:, hs], NT_DIMS, preferred_element_type=F32)
                s = s - f_ref[0, hh:hh + 1, :]
                if on_diagonal:
                    keep = (lax.broadcasted_iota(jnp.int32, (tq, tk), 0)
                            >= lax.broadcasted_iota(jnp.int32, (tq, tk), 1))
                    s = jnp.where(keep, s, NEG)
                m_prev = m_sc[hh]
                m_new = jnp.maximum(m_prev, jnp.max(s, axis=-1, keepdims=True))
                alpha = jnp.exp2(m_prev - m_new)
                p = jnp.exp2(s - jnp.concatenate([m_new] * (tk // LANES), axis=1)).astype(BF16)
                acc_sc[hh] = alpha * acc_sc[hh] + jnp.dot(p, v_ref[:, hs], preferred_element_type=F32)
                m_sc[hh] = m_new

        @pl.when(j < i)
        def _():
            tile(False)

        @pl.when(j == i)
        def _():
            tile(True)
            lane0 = (lax.broadcasted_iota(jnp.int32, (SUBLANES, LANES), 1) == 0).astype(BF16)
            outs, lses = [], []
            for hh in range(2):
                acc = acc_sc[hh]
                lsum = acc[:, LANE_ROWSUM_P:LANE_ROWSUM_P + 1]
                outs.append(acc[:, :HEAD_DIM] / lsum)
                lse_cols = m_sc[hh] + jnp.log2(lsum)
                lses.append(_exact_dot_nt(lane0, lse_cols)[0:1, :])
            o_ref[...] = jnp.concatenate(outs, axis=1).astype(BF16)
            lse_ref[0] = jnp.concatenate(lses, axis=0)

    q_map = lambda h, t, i_ref, j_ref: (i_ref[t], h)
    kv_map = lambda h, t, i_ref, j_ref: (j_ref[t], h)
    return pl.pallas_call(
        body, name=name,
        grid_spec=pltpu.PrefetchScalarGridSpec(
            num_scalar_prefetch=2, grid=(npair, i_of.shape[0]),
            in_specs=[pl.BlockSpec((tq, PAIR_SLOT), q_map), pl.BlockSpec((tk, PAIR_SLOT), kv_map),
                      pl.BlockSpec((tk, PAIR_SLOT), kv_map),
                      pl.BlockSpec((1, 2, tk), lambda h, t, i_ref, j_ref: (h, 0, j_ref[t]))],
            out_specs=[pl.BlockSpec((tq, HEAD_PAIR), q_map),
                       pl.BlockSpec((1, 2, tq), lambda h, t, i_ref, j_ref: (h, 0, i_ref[t]))],
            scratch_shapes=[pltpu.VMEM((2, tq, LANES), F32), pltpu.VMEM((2, tq, HEAD_SLOT), F32)]),
        out_shape=[jax.ShapeDtypeStruct((l, d), BF16), jax.ShapeDtypeStruct((npair, 2, l), F32)],
        compiler_params=_cp("parallel", "arbitrary"),
    )(i_of, j_of, qx, kx, vx, f2_rows)


def _flash_bwd(qx, kx, vx, f2_rep, do, lse_rows, delta_rows, name):
    l = qx.shape[0]
    npair = qx.shape[1] // PAIR_SLOT
    d = npair * HEAD_PAIR
    tq = _tile(l, ATTN_TILE)
    tk = tq
    i_of, j_of = _causal_tiles(l // tq, by_query=False)

    def body(i_ref, j_ref, q_ref, k_ref, v_ref, f_ref, do_ref, lse_ref, dl_ref, dq_ref, dk_ref, dv_ref):
        t = pl.program_id(1)
        i = i_ref[t]
        j = j_ref[t]

        @pl.when(t == 0)
        def _():
            dq_ref[...] = jnp.zeros_like(dq_ref)

        @pl.when(i == j)
        def _():
            dk_ref[...] = jnp.zeros_like(dk_ref)
            dv_ref[...] = jnp.zeros_like(dv_ref)

        def tile(on_diagonal):
            dqs, dks, dvs = [], [], []
            for hh in range(2):
                hs = slice(hh * HEAD_SLOT, (hh + 1) * HEAD_SLOT)
                qh, kh = q_ref[:, hs], k_ref[:, hs]
                vh = v_ref[:, hh * HEAD_SLOT:hh * HEAD_SLOT + HEAD_DIM]
                doh = do_ref[:, hh * HEAD_DIM:(hh + 1) * HEAD_DIM]
                st = lax.dot_general(kh, qh, NT_DIMS, preferred_element_type=F32)
                st = st - jnp.concatenate([f_ref[:, hs]] * (tq // HEAD_SLOT), axis=1)
                pt = jnp.exp2(st - lse_ref[0, hh:hh + 1, :])
                if on_diagonal:
                    keep = (lax.broadcasted_iota(jnp.int32, (tk, tq), 1)
                            >= lax.broadcasted_iota(jnp.int32, (tk, tq), 0))
                    pt = jnp.where(keep, pt, 0.0)
                dpt = lax.dot_general(vh, doh, NT_DIMS, preferred_element_type=F32)
                dsb = (pt * (dpt - dl_ref[0, hh:hh + 1, :])).astype(BF16)
                dvs.append(jnp.dot(pt.astype(BF16), doh, preferred_element_type=F32))
                dks.append(jnp.dot(dsb, qh, preferred_element_type=F32))
                dqs.append(lax.dot_general(dsb, kh, TN_DIMS, preferred_element_type=F32))
            dv_ref[...] += jnp.concatenate(dvs, axis=1)
            dk_ref[...] += jnp.concatenate(dks, axis=1)
            dq_ref[pl.ds(pl.multiple_of(i * tq, tq), tq), :] += jnp.concatenate(dqs, axis=1)

        @pl.when(i > j)
        def _():
            tile(False)

        @pl.when(i == j)
        def _():
            tile(True)

    qmap = lambda h, t, i_ref, j_ref: (i_ref[t], h)
    kmap = lambda h, t, i_ref, j_ref: (j_ref[t], h)
    row_map = lambda h, t, i_ref, j_ref: (h, 0, i_ref[t])
    return pl.pallas_call(
        body, name=name,
        grid_spec=pltpu.PrefetchScalarGridSpec(
            num_scalar_prefetch=2, grid=(npair, i_of.shape[0]),
            in_specs=[pl.BlockSpec((tq, PAIR_SLOT), qmap), pl.BlockSpec((tk, PAIR_SLOT), kmap),
                      pl.BlockSpec((tk, PAIR_SLOT), kmap), pl.BlockSpec((tk, PAIR_SLOT), kmap),
                      pl.BlockSpec((tq, HEAD_PAIR), qmap),
                      pl.BlockSpec((1, 2, tq), row_map), pl.BlockSpec((1, 2, tq), row_map)],
            out_specs=[pl.BlockSpec((l, PAIR_SLOT), lambda h, t, i_ref, j_ref: (0, h)),
                       pl.BlockSpec((tk, PAIR_SLOT), kmap), pl.BlockSpec((tk, HEAD_PAIR), kmap)]),
        out_shape=[jax.ShapeDtypeStruct((l, npair * PAIR_SLOT), F32), jax.ShapeDtypeStruct((l, npair * PAIR_SLOT), F32),
                   jax.ShapeDtypeStruct((l, d), F32)],
        compiler_params=_cp("parallel", "arbitrary"),
    )(i_of, j_of, qx, kx, vx, f2_rep, do, lse_rows, delta_rows)


def _my_place():
    return lax.axis_index("x"), lax.axis_index("y"), lax.axis_index("c")


def _chip_exchange(srcs, out_meta, plan, name):
    n_src, n_out, n_plan = len(srcs), len(out_meta), len(plan)

    def body(*refs):
        src_refs = refs[:n_src]
        out_refs = refs[n_src:n_src + n_out]
        send_sems, recv_sems, local_sems = refs[n_src + n_out:]
        x, y, c = _my_place()
        me = 2 * x + y
        copies = []
        for n, (si, oi, src_view, dst_view) in enumerate(plan):
            local = pltpu.make_async_copy(src_view(src_refs[si], me), dst_view(out_refs[oi], me), local_sems.at[n])
            local.start()
            copies.append(local)
            for k in (1, 2, 3):
                peer = me ^ k
                rc = pltpu.make_async_remote_copy(
                    src_ref=src_view(src_refs[si], peer), dst_ref=dst_view(out_refs[oi], me),
                    send_sem=send_sems.at[n, k - 1], recv_sem=recv_sems.at[n, k - 1],
                    device_id=(peer >> 1, peer & 1, c), device_id_type=MESH)
                rc.start()
                copies.append(rc)
        for cp in copies:
            cp.wait()

    any_spec = pl.BlockSpec(memory_space=pl.ANY)
    return pl.pallas_call(
        body, name=name,
        in_specs=[any_spec] * n_src, out_specs=[any_spec] * n_out,
        out_shape=[jax.ShapeDtypeStruct(shape, dt) for (shape, dt) in out_meta],
        scratch_shapes=[pltpu.SemaphoreType.DMA((n_plan, 3)), pltpu.SemaphoreType.DMA((n_plan, 3)),
                        pltpu.SemaphoreType.DMA((n_plan,))],
    )(*srcs)


def _plan_copies(src_refs, land_refs, plan, send_sems, recv_sems):
    x, y, c = _my_place()
    me = 2 * x + y
    copies = []
    for n, (si, oi, src_view, dst_view) in enumerate(plan):
        for k in (1, 2, 3):
            peer = me ^ k
            copies.append(pltpu.make_async_remote_copy(
                src_ref=src_view(src_refs[si], peer), dst_ref=dst_view(land_refs[oi], me),
                send_sem=send_sems.at[3 * n + k - 1], recv_sem=recv_sems.at[3 * n + k - 1],
                device_id=(peer >> 1, peer & 1, c), device_id_type=MESH))
    return copies


def _hbm(a):
    return pltpu.HBM(a.shape, a.dtype)


def _exchange_start(srcs, lands, plan, name):
    n_src, n_land = len(srcs), len(lands)
    n_buf = n_src + n_land

    def body(*refs):
        send_sems, recv_sems = refs[n_buf], refs[n_buf + 1]
        token = refs[-1]
        for cp in _plan_copies(refs[:n_src], refs[n_src:n_buf], plan, send_sems, recv_sems):
            cp.start()
        token[...] = jnp.zeros_like(token)

    bufs = [pltpu.with_memory_space_constraint(a, pltpu.HBM) for a in (*srcs, *lands)]
    hbm = pl.BlockSpec(memory_space=pltpu.HBM)
    sem = pl.BlockSpec(memory_space=pltpu.SEMAPHORE)
    res = pl.pallas_call(
        body, name=name,
        out_shape=(pltpu.SemaphoreType.DMA((3 * len(plan),)), pltpu.SemaphoreType.DMA((3 * len(plan),)),
                   *[_hbm(a) for a in bufs], jax.ShapeDtypeStruct((SUBLANES, LANES), F32)),
        in_specs=[hbm] * n_buf, out_specs=(sem, sem, *[hbm] * n_buf, pl.BlockSpec(memory_space=pltpu.VMEM)),
        input_output_aliases={n: 2 + n for n in range(n_buf)},
        compiler_params=pltpu.CompilerParams(has_side_effects=pltpu.SideEffectType.DATAFLOW_SIDE_EFFECTING),
    )(*bufs)
    return (res[0], res[1]), list(res[2:2 + n_src]), list(res[2 + n_src:2 + n_buf]), res[-1]


def _exchange_wait(sems, srcs, lands, plan, after, name):
    n_src, n_land = len(srcs), len(lands)
    n_buf = n_src + n_land

    def body(*refs):
        send_sems, recv_sems = refs[n_buf], refs[n_buf + 1]
        for cp in _plan_copies(refs[:n_src], refs[n_src:n_buf], plan, send_sems, recv_sems):
            cp.wait_send()
            cp.wait_recv()

    hbm = pl.BlockSpec(memory_space=pltpu.HBM)
    sem = pl.BlockSpec(memory_space=pltpu.SEMAPHORE)
    res = pl.pallas_call(
        body, name=name, out_shape=tuple(_hbm(a) for a in (*srcs, *lands)),
        in_specs=[hbm] * n_buf + [sem, sem, pl.BlockSpec(memory_space=pl.ANY)], out_specs=tuple([hbm] * n_buf),
        input_output_aliases={n: n for n in range(n_buf)},
        compiler_params=pltpu.CompilerParams(has_side_effects=pltpu.SideEffectType.DATAFLOW_SIDE_EFFECTING),
    )(*srcs, *lands, sems[0], sems[1], after)
    return list(res[n_src:])


def _core_exchange(arrays, name):
    n_items = len(arrays)

    def body(*refs):
        srcs = refs[:n_items]
        outs = refs[n_items:2 * n_items]
        send_sems, recv_sems = refs[2 * n_items:]
        x, y, c = _my_place()
        copies = []
        for n in range(n_items):
            rc = pltpu.make_async_remote_copy(
                src_ref=srcs[n], dst_ref=outs[n], send_sem=send_sems.at[n], recv_sem=recv_sems.at[n],
                device_id=(x, y, 1 - c), device_id_type=MESH)
            rc.start()
            copies.append(rc)
        for cp in copies:
            cp.wait()

    any_spec = pl.BlockSpec(memory_space=pl.ANY)
    return pl.pallas_call(
        body, name=name,
        in_specs=[any_spec] * n_items, out_specs=[any_spec] * n_items,
        out_shape=[jax.ShapeDtypeStruct(a.shape, a.dtype) for a in arrays],
        scratch_shapes=[pltpu.SemaphoreType.DMA((n_items,)), pltpu.SemaphoreType.DMA((n_items,))],
    )(*arrays)


def _sum_chips(parts, name):
    _, rows, cols = parts.shape
    tm = _tile(rows, 512)

    def body(p_ref, o_ref):
        acc = p_ref[0].astype(F32)
        for s in range(1, N_CHIPS):
            acc = acc + p_ref[s].astype(F32)
        o_ref[...] = acc

    return pl.pallas_call(
        body, name=name, grid=(rows // tm,),
        in_specs=[pl.BlockSpec((N_CHIPS, tm, cols), lambda i: (0, i, 0))],
        out_specs=pl.BlockSpec((tm, cols), lambda i: (i, 0)),
        out_shape=jax.ShapeDtypeStruct((rows, cols), F32),
        compiler_params=_cp("parallel"),
    )(parts)


def _adamw(ga, gb, w, m, v, name):
    rows, cols = w.shape
    tm = _tile(rows, 512)
    c1 = 1.0 - ADAM_B1 ** ADAM_STEP
    c2 = 1.0 - ADAM_B2 ** ADAM_STEP

    def body(ga_ref, gb_ref, w_ref, m_ref, v_ref, g_ref, d_ref, nm_ref, nv_ref):
        g = ga_ref[...] + gb_ref[...]
        nm = ADAM_B1 * m_ref[...] + (1.0 - ADAM_B1) * g
        nv = ADAM_B2 * v_ref[...] + (1.0 - ADAM_B2) * (g * g)
        g_ref[...] = g
        nm_ref[...] = nm
        nv_ref[...] = nv
        d_ref[...] = -ADAM_LR * ((nm / c1) / (jnp.sqrt(nv / c2) + ADAM_EPS) + ADAM_WD * w_ref[...])

    spec = pl.BlockSpec((tm, cols), lambda i: (i, 0))
    return pl.pallas_call(
        body, name=name, grid=(rows // tm,), in_specs=[spec] * 5, out_specs=[spec] * 4,
        out_shape=[jax.ShapeDtypeStruct((rows, cols), F32)] * 4, compiler_params=_cp("parallel"),
    )(ga, gb, w, m, v)


def _ssm_discretise(log_dt, a_re, a_im, b_re, b_im):
    dt = jnp.exp(log_dt)[:, None]
    mag = jnp.exp(a_re * dt)
    lbr = mag * jnp.cos(a_im * dt)
    lbi = mag * jnp.sin(a_im * dt)
    den = a_re * a_re + a_im * a_im
    nr, ni = lbr - 1.0, lbi
    qr = (nr * a_re + ni * a_im) / den
    qi = (ni * a_re - nr * a_im) / den
    bbr = qr[..., None] * b_re - qi[..., None] * b_im
    bbi = qr[..., None] * b_im + qi[..., None] * b_re
    return lbr, lbi, bbr, bbi


def _cmul(ar, ai, br, bi):
    return ar * br - ai * bi, ar * bi + ai * br


def _scan_tables(lr, li, nj, reverse):
    lr = lr.reshape(nj, 1, -1)
    li = li.reshape(nj, 1, -1)
    if reverse:
        li = -li
    pows = [(lr, li)]
    for _ in range(7):
        pows.append(_cmul(*pows[-1], lr, li))
    r = jnp.arange(SUBLANES).reshape(1, SUBLANES, 1)
    if reverse:
        r = SUBLANES - 1 - r
    out = []
    for k in (1, 2, 4):
        pr, pi = pows[k - 1]
        keep = (r >= k).astype(F32)
        out += [pr * keep, pi * keep]
    shape = (nj, SUBLANES, lr.shape[-1])
    cr = jnp.zeros(shape, F32)
    ci = jnp.zeros(shape, F32)
    for e in range(SUBLANES):
        sel = (r == e).astype(F32)
        cr = cr + sel * pows[e][0]
        ci = ci + sel * pows[e][1]
    out += [cr, ci]
    return jnp.stack(out, axis=1)


def _group_eye(gl):
    return jnp.eye(gl, dtype=F32)


def _block_diag_in(bbr, bbi, nj):
    g, p, c = bbr.shape
    gl = g // nj
    eye = _group_eye(gl)[None, :, None, :, None]

    def one(b):
        t = b.reshape(nj, gl, p, c).transpose(0, 1, 3, 2)[:, :, :, None, :]
        return (t * eye).reshape(nj, gl * c, gl * p)

    return jnp.concatenate([one(bbr), one(bbi)], axis=2)


def _block_diag_in_grad(gmat, nj, p, c):
    gl = gmat.shape[1] // c
    n = gl * p
    eye = _group_eye(gl)[None, :, None, :, None]

    def one(m):
        t = jnp.sum(m.reshape(nj, gl, c, gl, p) * eye, axis=3)
        return t.transpose(0, 1, 3, 2).reshape(nj * gl, p, c)

    return one(gmat[:, :, :n]), one(gmat[:, :, n:])


def _block_diag_out(c_re, c_im, nj):
    g, c, p = c_re.shape
    gl = g // nj
    eye = _group_eye(gl)[None, :, None, :, None]

    def one(m):
        t = m.reshape(nj, gl, c, p).transpose(0, 1, 3, 2)[:, :, :, None, :]
        return (t * eye).reshape(nj, gl * p, gl * c)

    return jnp.concatenate([one(c_re), -one(c_im)], axis=1)


def _block_diag_out_grad(gmat, nj, p, c):
    gl = gmat.shape[2] // c
    n = gl * p
    eye = _group_eye(gl)[None, :, None, :, None]

    def one(m):
        t = jnp.sum(m.reshape(nj, gl, p, gl, c) * eye, axis=3)
        return t.transpose(0, 1, 3, 2).reshape(nj * gl, c, p)

    return one(gmat[:, :n, :]), -one(gmat[:, n:, :])


def _pad_rows(flat, cols):
    per = SUBLANES * cols
    n = flat.shape[0]
    total = -(-n // per) * per
    return jnp.pad(flat, (0, total - n)).reshape(total // cols, cols)


def _pack_small(arrs, cols):
    packed = jnp.concatenate([_pad_rows(a.reshape(-1), cols) for a in arrs], axis=0)
    rows = packed.shape[0]
    return jnp.pad(packed, ((0, -rows % 128), (0, 0)))


def _unpack_small(packed, shapes, cols):
    out = []
    row = 0
    for s in shapes:
        n = math.prod(s)
        rows = -(-n // (SUBLANES * cols)) * SUBLANES
        out.append(packed[row:row + rows].reshape(-1)[:n].reshape(s))
        row += rows
    return out


def kernel(x, mix_norm, mlp_norm, mlp_w1, mlp_w2, ssm_log_dt, ssm_a_re, ssm_a_im, ssm_b_re, ssm_b_im, ssm_c_re, ssm_c_im, ssm_d, ssm_w_glu, kv_norm, w_kvf, b_f, attn_wq, attn_wo, final_norm, loss_target, m_mix_norm, m_mlp_norm, m_mlp_w1, m_mlp_w2, m_ssm_log_dt, m_ssm_a_re, m_ssm_a_im, m_ssm_b_re, m_ssm_b_im, m_ssm_c_re, m_ssm_c_im, m_ssm_d, m_ssm_w_glu, m_kv_norm, m_w_kvf, m_b_f, m_attn_wq, m_attn_wo, m_final_norm, v_mix_norm, v_mlp_norm, v_mlp_w1, v_mlp_w2, v_ssm_log_dt, v_ssm_a_re, v_ssm_a_im, v_ssm_b_re, v_ssm_b_im, v_ssm_c_re, v_ssm_c_im, v_ssm_d, v_ssm_w_glu, v_kv_norm, v_w_kvf, v_b_f, v_attn_wq, v_attn_wo, v_final_norm):
    seq, d = x.shape[1], x.shape[2]
    depth = mix_norm.shape[0]
    n_a = ssm_log_dt.shape[0]
    n_b = depth - n_a
    ff = mlp_w1.shape[2] * N_CHIPS
    n_heads = d // HEAD_DIM
    n_groups = d // SSM_GROUP
    p_state = ssm_a_re.shape[2]
    gb = min(d, 256)
    nj = d // gb
    kvf_cols = w_kvf.shape[1]
    ds4 = d // N_CHIPS
    chip = 2 * lax.axis_index("x") + lax.axis_index("y")

    def cols_of(width):
        return lambda ref, s: ref.at[:, :, pl.ds(pl.multiple_of(s * width, LANES), width)]

    def rows_of(height):
        return lambda ref, s: ref.at[:, pl.ds(pl.multiple_of(s * height, SUBLANES), height), :]

    whole = lambda ref, s: ref
    slot = lambda ref, s: ref.at[s]
    def cols2(width):
        return lambda ref, s: ref.at[:, pl.ds(pl.multiple_of(s * width, LANES), width)]

    def rows2(height):
        return lambda ref, s: ref.at[pl.ds(pl.multiple_of(s * height, SUBLANES), height), :]

    assert n_a >= 2
    (skip_parts,) = _chip_exchange([ssm_d], [((N_CHIPS, n_a, ds4), F32)], [(0, 0, whole, slot)], "gather_skip")
    skip_all = skip_parts.transpose(1, 0, 2).reshape(n_a, d)
    w1_s, w2_s, glu_s = mlp_w1.astype(BF16), mlp_w2.astype(BF16), ssm_w_glu.astype(BF16)
    src_a = [w1_s[0], w2_s[0], glu_s[0]]
    plan_a = [(0, 0, whole, cols2(d)), (1, 1, whole, rows2(d)), (2, 2, whole, cols2(2 * ds4))]
    land_a = [lax.empty((d, ff), BF16), lax.empty((ff, d), BF16), lax.empty((d, 2 * d), BF16)]
    src_b = [w1_s[1:], w2_s[1:], glu_s[1:], w_kvf.astype(BF16), attn_wq.astype(BF16), attn_wo.astype(BF16)]
    plan_b = [(0, 0, whole, cols_of(d)), (1, 1, whole, rows_of(d)), (2, 2, whole, cols_of(2 * ds4)),
              (3, 3, whole, slot), (4, 4, whole, rows_of(ds4)), (5, 5, whole, rows_of(ds4))]
    land_b = [lax.empty((depth - 1, d, ff), BF16), lax.empty((depth - 1, ff, d), BF16),
              lax.empty((n_a - 1, d, 2 * d), BF16), lax.empty((N_CHIPS, d, kvf_cols), BF16),
              lax.empty((n_b, d, d), BF16), lax.empty((n_b, d, d), BF16)]
    def idx(*at):
        return [jnp.asarray(v, jnp.int32) for v in at]

    put = lax.dynamic_update_slice
    land_a = [put(land_a[0], src_a[0], idx(0, chip * d)), put(land_a[1], src_a[1], idx(chip * d, 0)),
              put(land_a[2], src_a[2], idx(0, chip * 2 * ds4))]
    land_b = [put(land_b[0], src_b[0], idx(0, 0, chip * d)), put(land_b[1], src_b[1], idx(0, chip * d, 0)),
              put(land_b[2], src_b[2], idx(0, 0, chip * 2 * ds4)), put(land_b[3], src_b[3][None], idx(chip, 0, 0)),
              put(land_b[4], src_b[4], idx(0, chip * ds4, 0)), put(land_b[5], src_b[5], idx(0, chip * ds4, 0))]
    sems_a, src_a, land_a, token_a = _exchange_start(src_a, land_a, plan_a, "gather_start_a")
    sems_b, src_b, land_b, token_b = _exchange_start(src_b, land_b, plan_b, "gather_start_b")
    started = token_a[0:1, 0:1] + token_b[0:1, 0:1]

    def layer_w1(i):
        return w1_0 if i == 0 else w1_rest[i - 1]

    def layer_w2(i):
        return w2_0 if i == 0 else w2_rest[i - 1]

    def layer_glu(i):
        return glu_0 if i == 0 else glu_rest[i - 1]

    h = x[0]
    target = loss_target[0]

    saved = []
    for i in range(n_a):
        lbr, lbi, bbr, bbi = _ssm_discretise(ssm_log_dt[i], ssm_a_re[i], ssm_a_im[i], ssm_b_re[i], ssm_b_im[i])
        bblk = _block_diag_in(bbr, bbi, nj)
        cblk = _block_diag_out(ssm_c_re[i], ssm_c_im[i], nj)
        rec = dict(h0=h, lam=(lbr, lbi), bblk=bblk, cblk=cblk)
        gain = mix_norm[i:i + 1] + started if i == 0 else mix_norm[i:i + 1]
        u = _norm_fwd(h, gain, f"s5_norm_{i}")
        rec["u"] = u
        dskip = rec["dskip"] = skip_all[i:i + 1]
        states, y = _scan_fwd(u, bblk.astype(BF16), cblk.astype(BF16), _scan_tables(lbr, lbi, nj, False), dskip,
                              f"s5_scan_{i}")
        rec["states"], rec["y"] = states, y
        if i == 0:
            w1_0, w2_0, glu_0 = _exchange_wait(sems_a, src_a, land_a, plan_a, y, "gather_wait_a")
        if i == 1:
            w1_rest, w2_rest, glu_rest, kvf_parts, wq_all, wo_all = _exchange_wait(
                sems_b, src_b, land_b, plan_b, y, "gather_wait_b")
        h, rec["zw"] = _s5_post_fwd(h, y, layer_glu(i), f"s5_glu_{i}")
        rec["h1"] = h
        h, rec["ap"] = _mlp_fwd(h, mlp_norm[i:i + 1], layer_w1(i), layer_w2(i), f"mlp_{i}")
        saved.append(rec)
    h_kv = h
    kvf_all = jnp.concatenate([kvf_parts[s] for s in range(N_CHIPS)], axis=1)
    wk = kvf_all[:, :d]
    wv = kvf_all[:, d:2 * d]
    wf = jnp.pad(kvf_all[:, 2 * d:], ((0, 0), (0, LANES - n_heads)))
    bf_row = jnp.pad(b_f, (0, LANES - n_heads)).reshape(1, LANES)
    wk_x = _slot_cols(wk)
    spread = (jnp.arange(LANES)[:, None] == jnp.arange(n_heads * HEAD_SLOT)[None, :] // HEAD_SLOT).astype(BF16)
    kx, vx, flog, cum, f2_rep = _kvf_fwd(h, kv_norm.reshape(1, d), wk_x, _slot_cols(wv), wf, bf_row,
                                         _slot_ones(n_heads, LANE_ROWSUM_DS), _slot_ones(n_heads, LANE_ROWSUM_P),
                                         spread, "kvf")
    f2_rows = (cum[:, :n_heads] * LOG2E).T.reshape(n_heads // 2, 2, seq)
    wq_x = [_slot_cols(wq_all[jb]) for jb in range(n_b)]
    for jb in range(n_b):
        i = n_a + jb
        rec = dict(h0=h)
        qx = _q_fwd(h, mix_norm[i:i + 1], wq_x[jb], _slot_ones(n_heads, LANE_COLSUM_DS), f"attn_q_{jb}")
        o, lse = _flash_fwd(qx, kx, vx, f2_rows, f"attn_core_{jb}")
        rec["qx"], rec["o"], rec["lse"] = qx, o, lse
        h = _o_fwd(h, o, wo_all[jb], f"attn_out_{jb}")
        rec["h1"] = h
        h, rec["ap"] = _mlp_fwd(h, mlp_norm[i:i + 1], layer_w1(i), layer_w2(i), f"mlp_{i}")
        saved.append(rec)
    dh, loss_row, g_final = _loss_head(h, target, final_norm.reshape(1, d), "loss_head")
    loss = lax.psum(loss_row[0, 0], ("x", "y", "c"))

    head_sel = (jnp.arange(n_heads)[:, None] == jnp.arange(d)[None, :] // HEAD_DIM).astype(BF16)
    slot_lane = jnp.arange(n_heads * HEAD_SLOT)[:, None]
    in_lane = jnp.arange(LANES)[None, :]
    sel_q = (slot_lane == in_lane * HEAD_SLOT + LANE_ROWSUM_DS).astype(BF16)
    sel_k = (slot_lane == in_lane * HEAD_SLOT + LANE_COLSUM_DS).astype(BF16)
    g_mix = [None] * depth
    g_mlp = [None] * depth
    g_w1 = [None] * depth
    g_w2 = [None] * depth
    g_wq = [None] * n_b
    g_wo = [None] * n_b
    g_glu = [None] * n_a
    g_ssm = [None] * n_a
    dk_parts, dv_parts, dq_parts = [], [], []

    red_waits = []

    def reduce_start(entries, name):
        numbers = sorted({e[1] for e in entries})
        lands = {}
        for (_, ln, shape, dt, _, _, _, _) in entries:
            if ln not in lands:
                lands[ln] = lax.empty(shape, dt)
        for (_, ln, _, _, _, _, own, at) in entries:
            lands[ln] = lax.dynamic_update_slice(lands[ln], own, idx(*at))
        plan = [(n, numbers.index(e[1]), e[4], e[5]) for n, e in enumerate(entries)]
        sems, srcs, lands_t, token = _exchange_start([e[0] for e in entries], [lands[ln] for ln in numbers], plan,
                                                     name + "_start")
        red_waits.append((sems, srcs, lands_t, plan, name + "_wait"))
        return token[0:1, 0:1]

    def w1_entry(i, ln, local, n_layers):
        own = lax.dynamic_slice(g_w1[i], idx(0, chip * d), (d, d))[None, None]
        return (g_w1[i], ln, (N_CHIPS, n_layers, d, d), BF16, cols2(d), into(local), own, (chip, local, 0, 0))

    def w2_entry(i, ln, local, n_layers):
        own = lax.dynamic_slice(g_w2[i], idx(chip * d, 0), (d, d))[None, None]
        return (g_w2[i], ln, (N_CHIPS, n_layers, d, d), BF16, rows2(d), into(local), own, (chip, local, 0, 0))

    def glu_entry(i, ln, local, n_layers):
        own = lax.dynamic_slice(g_glu[i], idx(0, chip * 2 * ds4), (d, 2 * ds4))[None, None]
        return (g_glu[i], ln, (N_CHIPS, n_layers, d, 2 * ds4), BF16, cols2(2 * ds4), into(local), own,
                (chip, local, 0, 0))

    def rows_entry(g, ln, local):
        own = lax.dynamic_slice(g, idx(chip * ds4, 0), (ds4, d))[None, None]
        return (g, ln, (N_CHIPS, n_b, ds4, d), BF16, rows2(ds4), into(local), own, (chip, local, 0, 0))

    def into(layer):
        return lambda ref, s: ref.at[s, layer]

    def mlp_back(dh, i, rec, tie=None):
        gain = mlp_norm[i:i + 1] if tie is None else mlp_norm[i:i + 1] + tie
        dh_in, hm, a, dap, g_mlp[i] = _mlp_bwd(dh, rec["h1"], rec["ap"], gain, layer_w1(i),
                                               layer_w2(i), f"mlp_bwd_{i}")
        g_w2[i] = _matmul_tn(a, dh, f"mlp_dw2_{i}")
        g_w1[i] = _matmul_tn(hm, dap, f"mlp_dw1_{i}")
        return dh_in

    for jb in reversed(range(n_b)):
        i = n_a + jb
        rec = saved[i]
        dh = mlp_back(dh, i, rec)
        do, delta = _o_bwd(dh, rec["o"], wo_all[jb], head_sel, f"attn_out_bwd_{jb}")
        g_wo[jb] = _matmul_tn(rec["o"], dh, f"attn_dwo_{jb}")
        dqx, dkx, dv = _flash_bwd(rec["qx"], kx, vx, f2_rep, do, rec["lse"], delta.reshape(n_heads // 2, 2, seq),
                                  f"attn_core_bwd_{jb}")
        dk_parts.append(dkx)
        dv_parts.append(dv)
        dq_parts.append(dqx)
        dh, hn, dqs, g_mix[i] = _q_bwd(dh, rec["h0"], dqx, mix_norm[i:i + 1], wq_x[jb], f"attn_q_bwd_{jb}")
        g_wq[jb] = _unslot_cols(_matmul_tn(hn, dqs, f"attn_dwq_{jb}"))

    dh, hk, dkb, dvb, dfb, g_kvn, g_bf = _kvf_bwd(dh, h_kv, dk_parts[0], dk_parts[1], dv_parts[0], dv_parts[1],
                                                  dq_parts[0], dq_parts[1], flog, kv_norm.reshape(1, d), wk_x, wv, wf,
                                                  sel_q, sel_k, "kvf_bwd")
    g_kvf = jnp.concatenate([_unslot_cols(_matmul_tn(hk, dkb, "kvf_dwk")), _matmul_tn(hk, dvb, "kvf_dwv"),
                             _matmul_tn(hk, dfb, "kvf_dwf")[:, :n_heads]], axis=1)
    kvf_send = g_kvf.reshape(d, N_CHIPS, kvf_cols).transpose(1, 0, 2)
    group = [w1_entry(n_a + jb, 0, jb, n_b) for jb in range(n_b)]
    group += [w2_entry(n_a + jb, 1, jb, n_b) for jb in range(n_b)]
    group.append((kvf_send, 2, (N_CHIPS, d, kvf_cols), BF16, slot, slot,
                  lax.dynamic_index_in_dim(kvf_send, chip, 0, keepdims=True), (chip, 0, 0)))
    group += [rows_entry(g_wq[jb], 3, jb) for jb in range(n_b)]
    group += [rows_entry(g_wo[jb], 4, jb) for jb in range(n_b)]
    tie = reduce_start(group, "reduce_attn")

    for i in reversed(range(n_a)):
        rec = saved[i]
        if i == 0:
            group = [w1_entry(l, 0, l - 1, n_a - 1) for l in range(1, n_a)]
            group += [w2_entry(l, 1, l - 1, n_a - 1) for l in range(1, n_a)]
            group += [glu_entry(l, 2, l - 1, n_a - 1) for l in range(1, n_a)]
            tie = reduce_start(group, "reduce_s5")
        dh = mlp_back(dh, i, rec, tie if i in (0, n_a - 1) else None)
        dy, z, dzw = _s5_post_bwd(dh, rec["y"], rec["zw"], layer_glu(i), f"s5_glu_bwd_{i}")
        g_glu[i] = _matmul_tn(z, dzw, f"s5_dwglu_{i}")
        skip_gain = rec["dskip"]
        if i == 0:
            skip_gain = skip_gain + reduce_start([w1_entry(0, 0, 0, 1), w2_entry(0, 1, 0, 1), glu_entry(0, 2, 0, 1)],
                                                 "reduce_first")
        lbr, lbi = rec["lam"]
        bblk_t = rec["bblk"].transpose(0, 2, 1).astype(BF16)
        cblk_t = rec["cblk"].transpose(0, 2, 1).astype(BF16)
        du, glam8, gd8, gbblk, gcblk = _scan_bwd(dy, rec["u"], rec["states"], bblk_t, cblk_t,
                                                 _scan_tables(lbr, lbi, nj, True), skip_gain, f"s5_scan_bwd_{i}")
        dh, g_mix[i] = _norm_bwd_add(dh, du, rec["h0"], mix_norm[i:i + 1], f"s5_norm_bwd_{i}")
        glam = jnp.sum(glam8, axis=1)
        n_st = glam.shape[1] // 2
        g_lbr = glam[:, :n_st].reshape(n_groups, p_state)
        g_lbi = glam[:, n_st:].reshape(n_groups, p_state)
        g_bbr, g_bbi = _block_diag_in_grad(gbblk, nj, p_state, SSM_GROUP)
        g_cre, g_cim = _block_diag_out_grad(gcblk, nj, p_state, SSM_GROUP)
        _, pull = jax.vjp(_ssm_discretise, ssm_log_dt[i], ssm_a_re[i], ssm_a_im[i], ssm_b_re[i], ssm_b_im[i])
        g_ldt, g_are, g_aim, g_bre, g_bim = pull((g_lbr, g_lbi, g_bbr, g_bbi))
        g_ssm[i] = dict(log_dt=g_ldt, a_re=g_are, a_im=g_aim, b_re=g_bre, b_im=g_bim, c_re=g_cre, c_im=g_cim,
                        d=jnp.sum(gd8, axis=1).reshape(d))
    grad_x = dh[None]

    def stack_small(key):
        return jnp.stack([g_ssm[i][key] for i in range(n_a)])

    small_grads = [jnp.concatenate(g_mix, axis=0), jnp.concatenate(g_mlp, axis=0), stack_small("log_dt"),
                   stack_small("a_re"), stack_small("a_im"), stack_small("b_re"), stack_small("b_im"),
                   stack_small("c_re"), stack_small("c_im"), stack_small("d"), g_kvn.reshape(d),
                   g_bf[0, :n_heads], g_final.reshape(d)]
    small_w = [mix_norm, mlp_norm, ssm_log_dt, ssm_a_re, ssm_a_im, ssm_b_re, ssm_b_im, ssm_c_re, ssm_c_im,
               ssm_d, kv_norm, b_f, final_norm]
    small_m = [m_mix_norm, m_mlp_norm, m_ssm_log_dt, m_ssm_a_re, m_ssm_a_im, m_ssm_b_re, m_ssm_b_im, m_ssm_c_re,
               m_ssm_c_im, m_ssm_d, m_kv_norm, m_b_f, m_final_norm]
    small_v = [v_mix_norm, v_mlp_norm, v_ssm_log_dt, v_ssm_a_re, v_ssm_a_im, v_ssm_b_re, v_ssm_b_im, v_ssm_c_re,
               v_ssm_c_im, v_ssm_d, v_kv_norm, v_b_f, v_final_norm]
    skip_at = 9

    def widen_skip(part):
        return lax.dynamic_update_slice(jnp.zeros((n_a, d), F32), part, (0, chip * ds4))

    small_shapes = [a.shape for a in small_grads]
    pcols = 1024 if d >= 1024 else LANES
    g_small = _pack_small(small_grads, pcols)
    expand = lambda lst: _pack_small([widen_skip(a) if n == skip_at else a for n, a in enumerate(lst)], pcols)
    w_small, m_small, v_small = expand(small_w), expand(small_m), expand(small_v)
    srows = g_small.shape[0]

    assert n_b == 2
    reduce_start([(g_small, 0, (N_CHIPS, srows, pcols), F32, whole, slot, g_small[None], (chip, 0, 0))], "reduce_small")
    landed = [_exchange_wait(sems, srcs, lands, plan, dh, name) for (sems, srcs, lands, plan, name) in red_waits]
    (a_w1, a_w2, a_kvf, a_wq, a_wo), (s_w1, s_w2, s_glu), (f_w1, f_w2, f_glu), (r_small,) = landed

    def chip_sum(r, name):
        return _sum_chips(r.reshape(N_CHIPS, -1, r.shape[-1]), name)

    sums = [jnp.concatenate([chip_sum(f_w1, "sum_w1_first"), chip_sum(s_w1, "sum_w1_s5"),
                             chip_sum(a_w1, "sum_w1_attn")], axis=0),
            jnp.concatenate([chip_sum(f_w2, "sum_w2_first"), chip_sum(s_w2, "sum_w2_s5"),
                             chip_sum(a_w2, "sum_w2_attn")], axis=0),
            jnp.concatenate([chip_sum(f_glu, "sum_glu_first"), chip_sum(s_glu, "sum_glu_s5")], axis=0),
            chip_sum(a_kvf, "sum_kvf"), chip_sum(a_wq, "sum_wq"), chip_sum(a_wo, "sum_wo"),
            chip_sum(r_small, "sum_small")]
    others = _core_exchange(sums, "reduce_cores")

    def two(a):
        return a.reshape(-1, a.shape[-1])

    big_w = [(mlp_w1, m_mlp_w1, v_mlp_w1), (mlp_w2, m_mlp_w2, v_mlp_w2), (ssm_w_glu, m_ssm_w_glu, v_ssm_w_glu),
             (w_kvf, m_w_kvf, v_w_kvf), (attn_wq, m_attn_wq, v_attn_wq), (attn_wo, m_attn_wo, v_attn_wo)]
    big_out = []
    for n, (w, m, v) in enumerate(big_w):
        res = _adamw(sums[n], others[n], two(w), two(m), two(v), f"adamw_{n}")
        big_out.append([r.reshape(w.shape) for r in res])
    small_out = _adamw(sums[6], others[6], w_small, m_small, v_small, "adamw_small")

    def narrow_skip(a):
        return lax.dynamic_slice(a, (0, chip * ds4), (n_a, ds4))

    unpacked = []
    for packed in small_out:
        parts = _unpack_small(packed, small_shapes, pcols)
        parts[skip_at] = narrow_skip(parts[skip_at])
        unpacked.append(parts)

    order = ["mix_norm", "mlp_norm", "mlp_w1", "mlp_w2", "ssm_log_dt", "ssm_a_re", "ssm_a_im", "ssm_b_re",
             "ssm_b_im", "ssm_c_re", "ssm_c_im", "ssm_d", "ssm_w_glu", "kv_norm", "w_kvf", "b_f", "attn_wq",
             "attn_wo", "final_norm"]
    small_names = ["mix_norm", "mlp_norm", "ssm_log_dt", "ssm_a_re", "ssm_a_im", "ssm_b_re", "ssm_b_im",
                   "ssm_c_re", "ssm_c_im", "ssm_d", "kv_norm", "b_f", "final_norm"]
    big_names = ["mlp_w1", "mlp_w2", "ssm_w_glu", "w_kvf", "attn_wq", "attn_wo"]
    outs = [loss, grad_x]
    for kind in range(4):
        for name in order:
            if name in big_names:
                outs.append(big_out[big_names.index(name)][kind])
            else:
                outs.append(unpacked[kind][small_names.index(name)])
    return tuple(outs)
```

```python
import functools
import math

import jax
import jax.numpy as jnp
from jax import lax
from jax.experimental import pallas as pl
from jax.experimental.pallas import tpu as pltpu

F32 = jnp.float32
BF16 = jnp.bfloat16

RMS_EPS = 1e-6
SSM_GROUP = 16
SSM_STATE = 64
HEAD_DIM = 64
HEAD_PAIR = 2 * HEAD_DIM
LANES = 128
SUBLANES = 8
N_CHIPS = 4
ADAM_LR = 0.001
ADAM_B1 = 0.9
ADAM_B2 = 0.999
ADAM_EPS = 1e-08
ADAM_WD = 0.01
ADAM_STEP = 10
GELU_C = math.sqrt(2.0 / math.pi)
GELU_A = 0.044715
NEG = -1e30
LN2 = math.log(2.0)
LOG2E = 1.0 / LN2
VMEM_LIMIT = 56 * 1024 * 1024
MESH = pl.DeviceIdType.MESH

NT_DIMS = (((1,), (1,)), ((), ()))
TN_DIMS = (((0,), (0,)), ((), ()))


def _cp(*sem):
    return pltpu.CompilerParams(dimension_semantics=sem if sem else None, vmem_limit_bytes=VMEM_LIMIT)


def _zero_idx(nd, *_):
    return (0,) * nd


def _tile(n, t):
    if n <= t:
        return n
    for cand in range(t - t % SUBLANES, 0, -SUBLANES):
        if n % cand == 0:
            return cand
    raise ValueError((n, t))


def _rms_fwd(h, g):
    r = lax.rsqrt(jnp.mean(h * h, axis=-1, keepdims=True) + RMS_EPS)
    hhat = h * r
    return hhat * g, hhat, r


def _rms_bwd(du, hhat, r, g):
    dhh = du * g
    dh = r * (dhh - hhat * jnp.mean(dhh * hhat, axis=-1, keepdims=True))
    return dh, du * hhat


def _sigmoid(x):
    return 1.0 / (1.0 + jnp.exp(-x))


def _gelu(x):
    t = jnp.tanh(GELU_C * (x + GELU_A * x * x * x))
    return 0.5 * x * (1.0 + t)


def _gelu_grad(x):
    t = jnp.tanh(GELU_C * (x + GELU_A * x * x * x))
    return 0.5 * (1.0 + t) + 0.5 * x * (1.0 - t * t) * GELU_C * (1.0 + 3.0 * GELU_A * x * x)


def _row_fold(x):
    tm, w = x.shape
    return jnp.sum(x.reshape(tm // SUBLANES, SUBLANES, w), axis=0)


def _split3(x):
    hi = x.astype(BF16)
    r1 = x - hi.astype(F32)
    mid = r1.astype(BF16)
    lo = (r1 - mid.astype(F32)).astype(BF16)
    return hi, mid, lo


def _exact_dot(ones_mat, x):
    hi, mid, lo = _split3(x)
    d = functools.partial(jnp.dot, preferred_element_type=F32)
    return d(ones_mat, hi) + d(ones_mat, mid) + d(ones_mat, lo)


def _rows_call(body, name, tm, row_ins, const_ins, row_outs, acc_outs=(), scratch=(), reverse=False, col_outs=()):
    n = row_ins[0].shape[0]
    nb = n // tm
    if reverse:
        ridx = lambda i: (nb - 1 - i, 0)
        cidx = lambda i: (0, nb - 1 - i)
    else:
        ridx = lambda i: (i, 0)
        cidx = lambda i: (0, i)
    in_specs = [pl.BlockSpec((tm, a.shape[1]), ridx) for a in row_ins]
    in_specs += [pl.BlockSpec(a.shape, functools.partial(_zero_idx, a.ndim), pipeline_mode=pl.Buffered(1))
                 for a in const_ins]
    out_shape = [jax.ShapeDtypeStruct((n, w), dt) for (w, dt) in row_outs]
    out_shape += [jax.ShapeDtypeStruct(s, dt) for (s, dt) in acc_outs]
    out_shape += [jax.ShapeDtypeStruct((r, n), dt) for (r, dt) in col_outs]
    out_specs = [pl.BlockSpec((tm, w), ridx) for (w, dt) in row_outs]
    out_specs += [pl.BlockSpec(s, functools.partial(_zero_idx, len(s))) for (s, dt) in acc_outs]
    out_specs += [pl.BlockSpec((r, tm), cidx) for (r, dt) in col_outs]
    return pl.pallas_call(
        body, name=name, grid=(nb,), in_specs=in_specs, out_specs=out_specs, out_shape=out_shape,
        scratch_shapes=list(scratch), compiler_params=_cp("arbitrary"),
    )(*row_ins, *const_ins)


def _norm_fwd(h, g, name):
    n, d = h.shape
    tm = _tile(n, 512)

    def body(h_ref, g_ref, u_ref):
        u_ref[...] = _rms_fwd(h_ref[...], g_ref[...])[0]

    return _rows_call(body, name, tm, [h], [g], [(d, F32)])[0]


def _norm_bwd_add(dh, du, h, g, name):
    n, d = h.shape
    tm = _tile(n, 512)
    nb = n // tm

    def body(dh_ref, du_ref, h_ref, g_ref, o_ref, dg_ref, acc):
        i = pl.program_id(0)

        @pl.when(i == 0)
        def _():
            acc[...] = jnp.zeros_like(acc)

        gain = g_ref[...]
        _, hhat, r = _rms_fwd(h_ref[...], gain)
        dhn, dgr = _rms_bwd(du_ref[...], hhat, r, gain)
        o_ref[...] = dh_ref[...] + dhn
        acc[...] += _row_fold(dgr)

        @pl.when(i == nb - 1)
        def _():
            dg_ref[...] = jnp.sum(acc[...], axis=0, keepdims=True)

    return _rows_call(body, name, tm, [dh, du, h], [g], [(d, F32)], [((1, d), F32)],
                      [pltpu.VMEM((SUBLANES, d), F32)])


def _mlp_fwd(h, g, w1, w2, name):
    n, d = h.shape
    ff = w1.shape[1]
    tm = _tile(n, 256)
    fc = _tile(ff, 1024)

    def body(h_ref, g_ref, w1_ref, w2_ref, o_ref, ap_ref):
        hin = h_ref[...]
        hb = _rms_fwd(hin, g_ref[...])[0].astype(BF16)
        acc = hin
        for c in range(ff // fc):
            cs = slice(c * fc, (c + 1) * fc)
            ap = jnp.dot(hb, w1_ref[:, cs], preferred_element_type=F32)
            ap_ref[:, cs] = ap.astype(BF16)
            rl = jnp.maximum(ap, 0.0)
            acc = acc + jnp.dot((rl * rl).astype(BF16), w2_ref[cs, :], preferred_element_type=F32)
        o_ref[...] = acc

    return _rows_call(body, name, tm, [h], [g, w1, w2], [(d, F32), (ff, BF16)])


def _mlp_bwd(dh, h, ap, g, w1, w2, name):
    n, d = h.shape
    ff = w1.shape[1]
    tm = _tile(n, 256)
    nb = n // tm
    fc = _tile(ff, 1024)

    def body(dh_ref, h_ref, ap_ref, g_ref, w1_ref, w2_ref, o_ref, hm_ref, a_ref, dap_ref, dg_ref, acc):
        i = pl.program_id(0)

        @pl.when(i == 0)
        def _():
            acc[...] = jnp.zeros_like(acc)

        gain = g_ref[...]
        dhv = dh_ref[...]
        hm, hhat, r = _rms_fwd(h_ref[...], gain)
        hm_ref[...] = hm.astype(BF16)
        dhb = dhv.astype(BF16)
        dhm = jnp.zeros((tm, d), F32)
        for c in range(ff // fc):
            cs = slice(c * fc, (c + 1) * fc)
            rl = jnp.maximum(ap_ref[:, cs].astype(F32), 0.0)
            a_ref[:, cs] = (rl * rl).astype(BF16)
            da = lax.dot_general(dhb, w2_ref[cs, :], NT_DIMS, preferred_element_type=F32)
            dap = (da * (2.0 * rl)).astype(BF16)
            dap_ref[:, cs] = dap
            dhm = dhm + lax.dot_general(dap, w1_ref[:, cs], NT_DIMS, preferred_element_type=F32)
        dhn, dgr = _rms_bwd(dhm, hhat, r, gain)
        o_ref[...] = dhv + dhn
        acc[...] += _row_fold(dgr)

        @pl.when(i == nb - 1)
        def _():
            dg_ref[...] = jnp.sum(acc[...], axis=0, keepdims=True)

    return _rows_call(body, name, tm, [dh, h, ap], [g, w1, w2],
                      [(d, F32), (d, BF16), (ff, BF16), (ff, BF16)], [((1, d), F32)],
                      [pltpu.VMEM((SUBLANES, d), F32)])


def _s5_post_fwd(h, y, w_glu, name):
    n, d = h.shape
    tm = _tile(n, 512)

    def body(h_ref, y_ref, w_ref, o_ref, zw_ref):
        z = _gelu(y_ref[...]).astype(BF16)
        zw = jnp.dot(z, w_ref[...], preferred_element_type=F32)
        zw_ref[...] = zw.astype(BF16)
        o_ref[...] = h_ref[...] + zw[:, :d] * _sigmoid(zw[:, d:])

    return _rows_call(body, name, tm, [h, y], [w_glu], [(d, F32), (2 * d, BF16)])


def _s5_post_bwd(dh, y, zw, w_glu, name):
    n, d = dh.shape
    tm = _tile(n, 512)

    def body(dh_ref, y_ref, zw_ref, w_ref, dy_ref, z_ref, dzw_ref):
        dhv = dh_ref[...]
        yv = y_ref[...]
        val = zw_ref[:, :d].astype(F32)
        sg = _sigmoid(zw_ref[:, d:].astype(F32))
        dzw = jnp.concatenate([dhv * sg, dhv * val * sg * (1.0 - sg)], axis=1).astype(BF16)
        dzw_ref[...] = dzw
        dz = lax.dot_general(dzw, w_ref[...], NT_DIMS, preferred_element_type=F32)
        dy_ref[...] = dz * _gelu_grad(yv)
        z_ref[...] = _gelu(yv).astype(BF16)

    return _rows_call(body, name, tm, [dh, y, zw], [w_glu], [(d, F32), (d, BF16), (2 * d, BF16)])


def _q_fwd(h, g, wq_x, ones_x, name):
    n, d = h.shape
    tm = _tile(n, 512)
    scale = LOG2E * HEAD_DIM ** -0.5

    def body(h_ref, g_ref, w_ref, one_ref, q_ref):
        hb = _rms_fwd(h_ref[...], g_ref[...])[0].astype(BF16)
        q_ref[...] = (jnp.dot(hb, w_ref[...], preferred_element_type=F32) * scale + one_ref[...]).astype(BF16)

    return _rows_call(body, name, tm, [h], [g, wq_x, ones_x], [(wq_x.shape[1], BF16)])[0]


def _q_bwd(dh, h, dq, g, wq, name):
    n, d = h.shape
    tm = _tile(n, 512)
    nb = n // tm
    scale = HEAD_DIM ** -0.5

    def body(dh_ref, h_ref, dq_ref, g_ref, w_ref, o_ref, hn_ref, dqs_ref, dg_ref, acc):
        i = pl.program_id(0)

        @pl.when(i == 0)
        def _():
            acc[...] = jnp.zeros_like(acc)

        gain = g_ref[...]
        hn, hhat, r = _rms_fwd(h_ref[...], gain)
        hn_ref[...] = hn.astype(BF16)
        dqs = (dq_ref[...] * scale).astype(BF16)
        dqs_ref[...] = dqs
        dhn = lax.dot_general(dqs, w_ref[...], NT_DIMS, preferred_element_type=F32)
        dhi, dgr = _rms_bwd(dhn, hhat, r, gain)
        o_ref[...] = dh_ref[...] + dhi
        acc[...] += _row_fold(dgr)

        @pl.when(i == nb - 1)
        def _():
            dg_ref[...] = jnp.sum(acc[...], axis=0, keepdims=True)

    return _rows_call(body, name, tm, [dh, h, dq], [g, wq], [(d, F32), (d, BF16), (wq.shape[1], BF16)],
                      [((1, d), F32)], [pltpu.VMEM((SUBLANES, d), F32)])


def _o_fwd(h, o, wo, name):
    n, d = h.shape
    tm = _tile(n, 512)

    def body(h_ref, o_ref, w_ref, out_ref):
        out_ref[...] = h_ref[...] + jnp.dot(o_ref[...], w_ref[...], preferred_element_type=F32)

    return _rows_call(body, name, tm, [h, o], [wo], [(d, F32)])[0]


def _exact_dot_nt(ones_mat, x):
    hi, mid, lo = _split3(x)
    d = functools.partial(lax.dot_general, dimension_numbers=NT_DIMS, preferred_element_type=F32)
    return d(ones_mat, hi) + d(ones_mat, mid) + d(ones_mat, lo)


def _o_bwd(dh, o, wo, head_sel, name):
    n, d = dh.shape
    tm = _tile(n, 512)

    def body(dh_ref, o_ref, w_ref, e_ref, do_ref, dl_ref):
        do = lax.dot_general(dh_ref[...].astype(BF16), w_ref[...], NT_DIMS, preferred_element_type=F32).astype(BF16)
        do_ref[...] = do
        dl_ref[...] = _exact_dot_nt(e_ref[...], do.astype(F32) * o_ref[...].astype(F32))

    return _rows_call(body, name, tm, [dh, o], [wo, head_sel], [(wo.shape[0], BF16)],
                      col_outs=[(head_sel.shape[0], F32)])


def _exact_dot_rhs(x, ones_mat):
    hi, mid, lo = _split3(x)
    d = functools.partial(jnp.dot, preferred_element_type=F32)
    return d(hi, ones_mat) + d(mid, ones_mat) + d(lo, ones_mat)


def _kvf_fwd(h, g, wk, wv, wf, bf, k_ones, v_ones, spread, name):
    n, d = h.shape
    tm = _tile(n, 512)

    def body(h_ref, g_ref, wk_ref, wv_ref, wf_ref, bf_ref, ko_ref, vo_ref, sp_ref,
             k_ref, v_ref, fl_ref, cum_ref, rep_ref, carry):
        i = pl.program_id(0)

        @pl.when(i == 0)
        def _():
            carry[...] = jnp.zeros_like(carry)

        hb = _rms_fwd(h_ref[...], g_ref[...])[0].astype(BF16)
        k_ref[...] = (jnp.dot(hb, wk_ref[...], preferred_element_type=F32) + ko_ref[...]).astype(BF16)
        v_ref[...] = (jnp.dot(hb, wv_ref[...], preferred_element_type=F32) + vo_ref[...]).astype(BF16)
        fl = jnp.dot(hb, wf_ref[...], preferred_element_type=F32) + bf_ref[...]
        fl_ref[...] = fl
        logf = jnp.minimum(fl, 0.0) - jnp.log(1.0 + jnp.exp(-jnp.abs(fl)))
        rows = lax.broadcasted_iota(jnp.int32, (tm, tm), 0)
        cols = lax.broadcasted_iota(jnp.int32, (tm, tm), 1)
        lower = (rows >= cols).astype(BF16)
        cum = _exact_dot(lower, logf) + carry[0:1, :]
        cum_ref[...] = cum
        rep_ref[...] = _exact_dot_rhs(cum * LOG2E, sp_ref[...])
        carry[...] = jnp.broadcast_to(cum[tm - 1:tm, :], carry.shape)

    return _rows_call(body, name, tm, [h], [g, wk, wv, wf, bf, k_ones, v_ones, spread],
                      [(wk.shape[1], BF16), (wv.shape[1], BF16), (LANES, F32), (LANES, F32), (spread.shape[1], F32)],
                      scratch=[pltpu.VMEM((SUBLANES, LANES), F32)])


def _kvf_bwd(dh, h, dk1, dk2, dv1, dv2, dq1, dq2, fl, g, wk, wv, wf, sel_q, sel_k, name):
    n, d = h.shape
    tm = _tile(n, 256)
    nb = n // tm

    def body(dh_ref, h_ref, dk1_ref, dk2_ref, dv1_ref, dv2_ref, dq1_ref, dq2_ref, fl_ref,
             g_ref, wk_ref, wv_ref, wf_ref, sq_ref, sk_ref,
             o_ref, hk_ref, dk_ref, dv_ref, df_ref, dg_ref, db_ref, acc, bacc, carry):
        i = pl.program_id(0)

        @pl.when(i == 0)
        def _():
            acc[...] = jnp.zeros_like(acc)
            bacc[...] = jnp.zeros_like(bacc)
            carry[...] = jnp.zeros_like(carry)

        dkx = dk1_ref[...] + dk2_ref[...]
        dcum = _exact_dot_rhs(dq1_ref[...] + dq2_ref[...], sq_ref[...]) - _exact_dot_rhs(dkx, sk_ref[...])
        rows = lax.broadcasted_iota(jnp.int32, (tm, tm), 0)
        cols = lax.broadcasted_iota(jnp.int32, (tm, tm), 1)
        upper = (rows <= cols).astype(BF16)
        dlogf = _exact_dot(upper, dcum) + carry[0:1, :]
        carry[...] = jnp.broadcast_to(dlogf[0:1, :], carry.shape)
        df = dlogf / (1.0 + jnp.exp(fl_ref[...]))
        dfb = df.astype(BF16)
        df_ref[...] = dfb
        bacc[...] += _row_fold(df)
        dkb = (dkx * LN2).astype(BF16)
        dvb = (dv1_ref[...] + dv2_ref[...]).astype(BF16)
        dk_ref[...] = dkb
        dv_ref[...] = dvb
        gain = g_ref[...]
        hk, hhat, r = _rms_fwd(h_ref[...], gain)
        hk_ref[...] = hk.astype(BF16)
        dhk = lax.dot_general(dkb, wk_ref[...], NT_DIMS, preferred_element_type=F32)
        dhk = dhk + lax.dot_general(dvb, wv_ref[...], NT_DIMS, preferred_element_type=F32)
        dhk = dhk + lax.dot_general(dfb, wf_ref[...], NT_DIMS, preferred_element_type=F32)
        dhi, dgr = _rms_bwd(dhk, hhat, r, gain)
        o_ref[...] = dh_ref[...] + dhi
        acc[...] += _row_fold(dgr)

        @pl.when(i == nb - 1)
        def _():
            dg_ref[...] = jnp.sum(acc[...], axis=0, keepdims=True)
            db_ref[...] = jnp.sum(bacc[...], axis=0, keepdims=True)

    return _rows_call(body, name, tm, [dh, h, dk1, dk2, dv1, dv2, dq1, dq2, fl], [g, wk, wv, wf, sel_q, sel_k],
                      [(d, F32), (d, BF16), (wk.shape[1], BF16), (wv.shape[1], BF16), (LANES, BF16)],
                      [((1, d), F32), ((1, LANES), F32)],
                      [pltpu.VMEM((SUBLANES, d), F32), pltpu.VMEM((SUBLANES, LANES), F32),
                       pltpu.VMEM((SUBLANES, LANES), F32)], reverse=True)


def _loss_head(h, target, g, name):
    n, d = h.shape
    tm = _tile(n, 512)
    nb = n // tm

    def body(h_ref, t_ref, g_ref, dh_ref, loss_ref, dg_ref, lacc, gacc):
        i = pl.program_id(0)

        @pl.when(i == 0)
        def _():
            lacc[...] = jnp.zeros_like(lacc)
            gacc[...] = jnp.zeros_like(gacc)

        gain = g_ref[...]
        yv, hhat, r = _rms_fwd(h_ref[...], gain)
        e = yv - t_ref[...]
        lacc[...] += _row_fold(e * e)
        dhv, dgr = _rms_bwd(e * (1.0 / d), hhat, r, gain)
        dh_ref[...] = dhv
        gacc[...] += _row_fold(dgr)

        @pl.when(i == nb - 1)
        def _():
            loss_ref[...] = jnp.full((1, LANES), jnp.sum(lacc[...]) * (0.5 / d), F32)
            dg_ref[...] = jnp.sum(gacc[...], axis=0, keepdims=True)

    return _rows_call(body, name, tm, [h, target], [g], [(d, F32)], [((1, LANES), F32), ((1, d), F32)],
                      [pltpu.VMEM((SUBLANES, d), F32), pltpu.VMEM((SUBLANES, d), F32)])


def _matmul_tn(a, b, name, out_dtype=BF16):
    l, m = a.shape
    n = b.shape[1]
    tl = _tile(l, 2048)
    tmm = _tile(m, 1024)
    tn = _tile(n, 1024)
    nl = l // tl

    def body(a_ref, b_ref, o_ref, acc):
        k = pl.program_id(2)

        @pl.when(k == 0)
        def _():
            acc[...] = jnp.zeros_like(acc)

        acc[...] += lax.dot_general(a_ref[...].astype(BF16), b_ref[...].astype(BF16), TN_DIMS,
                                    preferred_element_type=F32)

        @pl.when(k == nl - 1)
        def _():
            o_ref[...] = acc[...].astype(out_dtype)

    return pl.pallas_call(
        body, name=name, grid=(m // tmm, n // tn, nl),
        in_specs=[pl.BlockSpec((tl, tmm), lambda i, j, k: (k, i)), pl.BlockSpec((tl, tn), lambda i, j, k: (k, j))],
        out_specs=pl.BlockSpec((tmm, tn), lambda i, j, k: (i, j)),
        out_shape=jax.ShapeDtypeStruct((m, n), out_dtype),
        scratch_shapes=[pltpu.VMEM((tmm, tn), F32)],
        compiler_params=_cp("parallel", "parallel", "arbitrary"),
    )(a, b)


def _scan_fwd(u, bblk, cblk, tabs, dskip, name):
    l, d = u.shape
    nj, gb, n2 = bblk.shape
    n = n2 // 2
    tm = _tile(l, 512)
    nb = l // tm
    nsub = 4 if tm % (4 * SUBLANES) == 0 else 1
    sub = tm // nsub

    def body(u_ref, b_ref, c_ref, t_ref, d_ref, st_ref, y_ref, carry):
        i = pl.program_id(1)

        @pl.when(i == 0)
        def _():
            carry[...] = jnp.zeros_like(carry)

        def step(rb, c):
            cr, ci = c
            rows = pl.ds(rb * SUBLANES, SUBLANES)
            xr = st_ref[rows, 0:n]
            xi = st_ref[rows, n:n2]
            for lvl, sh in enumerate((1, 2, 4)):
                ar = t_ref[0, 2 * lvl]
                ai = t_ref[0, 2 * lvl + 1]
                sr = pltpu.roll(xr, sh, 0)
                si = pltpu.roll(xi, sh, 0)
                xr, xi = xr + ar * sr - ai * si, xi + ar * si + ai * sr
            lr = t_ref[0, 6]
            li = t_ref[0, 7]
            xr, xi = xr + lr * cr - li * ci, xi + lr * ci + li * cr
            st_ref[rows, 0:n] = xr
            st_ref[rows, n:n2] = xi
            return (jnp.broadcast_to(xr[SUBLANES - 1:SUBLANES, :], (SUBLANES, n)),
                    jnp.broadcast_to(xi[SUBLANES - 1:SUBLANES, :], (SUBLANES, n)))

        def project_in(sb):
            rows = slice(sb * sub, (sb + 1) * sub)
            st_ref[rows, :] = jnp.dot(u_ref[rows, :].astype(BF16), b_ref[0], preferred_element_type=F32)

        def project_out(sb):
            rows = slice(sb * sub, (sb + 1) * sub)
            y_ref[rows, :] = (jnp.dot(st_ref[rows, :].astype(BF16), c_ref[0], preferred_element_type=F32)
                              + d_ref[...] * u_ref[rows, :])

        c = (carry[:, 0:n], carry[:, n:n2])
        project_in(0)
        for sb in range(nsub):
            if sb + 1 < nsub:
                project_in(sb + 1)
            for rb in range(sb * sub // SUBLANES, (sb + 1) * sub // SUBLANES):
                c = step(rb, c)
            project_out(sb)
        carry[:, 0:n] = c[0]
        carry[:, n:n2] = c[1]

    return pl.pallas_call(
        body, name=name, grid=(nj, nb),
        in_specs=[pl.BlockSpec((tm, gb), lambda j, i: (i, j)),
                  pl.BlockSpec((1, gb, n2), lambda j, i: (j, 0, 0)),
                  pl.BlockSpec((1, n2, gb), lambda j, i: (j, 0, 0)),
                  pl.BlockSpec((1, 8, SUBLANES, n), lambda j, i: (j, 0, 0, 0)),
                  pl.BlockSpec((1, gb), lambda j, i: (0, j))],
        out_specs=[pl.BlockSpec((tm, n2), lambda j, i: (i, j)), pl.BlockSpec((tm, gb), lambda j, i: (i, j))],
        out_shape=[jax.ShapeDtypeStruct((l, nj * n2), F32), jax.ShapeDtypeStruct((l, d), F32)],
        scratch_shapes=[pltpu.VMEM((SUBLANES, n2), F32)],
        compiler_params=_cp("parallel", "arbitrary"),
    )(u, bblk, cblk, tabs, dskip)


def _scan_bwd(dy, u, states, bblk_t, cblk_t, tabs, dskip, name):
    l, d = u.shape
    nj, n2, gb = bblk_t.shape
    n = n2 // 2
    tm = _tile(l, 512)
    nb = l // tm
    nsub = 4 if tm % (4 * SUBLANES) == 0 else 1
    sub = tm // nsub

    def body(dy_ref, u_ref, st_ref, prev_ref, bt_ref, ct_ref, t_ref, d_ref,
             du_ref, glam_ref, gd_ref, gb_ref, gc_ref, gx, carry):
        i = pl.program_id(1)
        ib = nb - 1 - i

        @pl.when(i == 0)
        def _():
            carry[...] = jnp.zeros_like(carry)
            glam_ref[...] = jnp.zeros_like(glam_ref)
            gd_ref[...] = jnp.zeros_like(gd_ref)
            gb_ref[...] = jnp.zeros_like(gb_ref)
            gc_ref[...] = jnp.zeros_like(gc_ref)

        last_row = lax.broadcasted_iota(jnp.int32, (SUBLANES, n), 0) == SUBLANES - 1

        def block(rows, xp_r, xp_i, c):
            cr, ci = c
            gr = gx[rows, 0:n]
            gi = gx[rows, n:n2]
            for lvl, sh in enumerate((1, 2, 4)):
                ar = t_ref[0, 2 * lvl]
                ai = t_ref[0, 2 * lvl + 1]
                sr = pltpu.roll(gr, SUBLANES - sh, 0)
                si = pltpu.roll(gi, SUBLANES - sh, 0)
                gr, gi = gr + ar * sr - ai * si, gi + ar * si + ai * sr
            lr = t_ref[0, 6]
            li = t_ref[0, 7]
            gr, gi = gr + lr * cr - li * ci, gi + lr * ci + li * cr
            gx[rows, 0:n] = gr
            gx[rows, n:n2] = gi
            xs_r = pltpu.roll(jnp.where(last_row, xp_r, st_ref[rows, 0:n]), 1, 0)
            xs_i = pltpu.roll(jnp.where(last_row, xp_i, st_ref[rows, n:n2]), 1, 0)
            glam_ref[0, :, 0:n] += gr * xs_r + gi * xs_i
            glam_ref[0, :, n:n2] += gi * xs_r - gr * xs_i
            return (jnp.broadcast_to(gr[0:1, :], (SUBLANES, n)), jnp.broadcast_to(gi[0:1, :], (SUBLANES, n)))

        def project_in(sb):
            rows = slice(sb * sub, (sb + 1) * sub)
            gx[rows, :] = jnp.dot(dy_ref[rows, :].astype(BF16), ct_ref[0], preferred_element_type=F32)

        def project_out(sb):
            rows = slice(sb * sub, (sb + 1) * sub)
            dyv = dy_ref[rows, :]
            uv = u_ref[rows, :]
            gxb = gx[rows, :].astype(BF16)
            du_ref[rows, :] = jnp.dot(gxb, bt_ref[0], preferred_element_type=F32) + d_ref[...] * dyv
            gd_ref[0] += _row_fold(dyv * uv)
            gb_ref[0] += lax.dot_general(uv.astype(BF16), gxb, TN_DIMS, preferred_element_type=F32)
            gc_ref[0] += lax.dot_general(st_ref[rows, :].astype(BF16), dyv.astype(BF16), TN_DIMS,
                                         preferred_element_type=F32)

        live = (ib > 0).astype(F32)
        c = (carry[:, 0:n], carry[:, n:n2])
        project_in(nsub - 1)
        for sb in reversed(range(nsub)):
            if sb > 0:
                project_in(sb - 1)
            for rb in reversed(range(sb * sub // SUBLANES, (sb + 1) * sub // SUBLANES)):
                rows = pl.ds(rb * SUBLANES, SUBLANES)
                if rb > 0:
                    before = pl.ds((rb - 1) * SUBLANES, SUBLANES)
                    c = block(rows, st_ref[before, 0:n], st_ref[before, n:n2], c)
                else:
                    c = block(rows, prev_ref[:, 0:n] * live, prev_ref[:, n:n2] * live, c)
            project_out(sb)
        carry[:, 0:n] = c[0]
        carry[:, n:n2] = c[1]

    rpb = tm // SUBLANES
    return pl.pallas_call(
        body, name=name, grid=(nj, nb),
        in_specs=[pl.BlockSpec((tm, gb), lambda j, i: (nb - 1 - i, j)),
                  pl.BlockSpec((tm, gb), lambda j, i: (nb - 1 - i, j)),
                  pl.BlockSpec((tm, n2), lambda j, i: (nb - 1 - i, j)),
                  pl.BlockSpec((SUBLANES, n2), lambda j, i: (jnp.maximum((nb - 1 - i) * rpb - 1, 0), j)),
                  pl.BlockSpec((1, n2, gb), lambda j, i: (j, 0, 0)),
                  pl.BlockSpec((1, gb, n2), lambda j, i: (j, 0, 0)),
                  pl.BlockSpec((1, 8, SUBLANES, n), lambda j, i: (j, 0, 0, 0)),
                  pl.BlockSpec((1, gb), lambda j, i: (0, j))],
        out_specs=[pl.BlockSpec((tm, gb), lambda j, i: (nb - 1 - i, j)),
                   pl.BlockSpec((1, SUBLANES, n2), lambda j, i: (j, 0, 0)),
                   pl.BlockSpec((1, SUBLANES, gb), lambda j, i: (j, 0, 0)),
                   pl.BlockSpec((1, gb, n2), lambda j, i: (j, 0, 0)),
                   pl.BlockSpec((1, n2, gb), lambda j, i: (j, 0, 0))],
        out_shape=[jax.ShapeDtypeStruct((l, d), F32),
                   jax.ShapeDtypeStruct((nj, SUBLANES, n2), F32),
                   jax.ShapeDtypeStruct((nj, SUBLANES, gb), F32),
                   jax.ShapeDtypeStruct((nj, gb, n2), F32),
                   jax.ShapeDtypeStruct((nj, n2, gb), F32)],
        scratch_shapes=[pltpu.VMEM((tm, n2), F32), pltpu.VMEM((SUBLANES, n2), F32)],
        compiler_params=_cp("parallel", "arbitrary"),
    )(dy, u, states, states, bblk_t, cblk_t, tabs, dskip)


HEAD_SLOT = 128
PAIR_SLOT = 2 * HEAD_SLOT
ATTN_TILE = 1024
LANE_ROWSUM_P = HEAD_DIM
LANE_COLSUM_DS = HEAD_DIM
LANE_ROWSUM_DS = HEAD_DIM + 1


def _slot_cols(w):
    r, c = w.shape
    nh = c // HEAD_DIM
    return jnp.pad(w.reshape(r, nh, HEAD_DIM), ((0, 0), (0, 0), (0, HEAD_SLOT - HEAD_DIM))).reshape(r, nh * HEAD_SLOT)


def _unslot_cols(w):
    r, c = w.shape
    nh = c // HEAD_SLOT
    return w.reshape(r, nh, HEAD_SLOT)[:, :, :HEAD_DIM].reshape(r, nh * HEAD_DIM)


def _slot_ones(nh, lane):
    return jnp.tile((jnp.arange(HEAD_SLOT) == lane).astype(F32), nh).reshape(1, nh * HEAD_SLOT)


def _causal_tiles(n, by_query):
    if by_query:
        tiles = [(i, j) for i in range(n) for j in range(i + 1)]
    else:
        tiles = [(i, j) for j in range(n) for i in range(j, n)]
    return (jnp.asarray([t[0] for t in tiles], jnp.int32), jnp.asarray([t[1] for t in tiles], jnp.int32))


def _flash_fwd(qx, kx, vx, f2_rows, name):
    l = qx.shape[0]
    npair = qx.shape[1] // PAIR_SLOT
    d = npair * HEAD_PAIR
    tq = _tile(l, ATTN_TILE)
    tk = tq
    i_of, j_of = _causal_tiles(l // tq, by_query=True)

    def body(i_ref, j_ref, q_ref, k_ref, v_ref, f_ref, o_ref, lse_ref, m_sc, acc_sc):
        t = pl.program_id(1)
        i = i_ref[t]
        j = j_ref[t]

        @pl.when(j == 0)
        def _():
            m_sc[...] = jnp.full_like(m_sc, NEG)
            acc_sc[...] = jnp.zeros_like(acc_sc)

        def tile(on_diagonal):
            for hh in range(2):
                hs = slice(hh * HEAD_SLOT, (hh + 1) * HEAD_SLOT)
                s = lax.dot_general(q_ref[:, hs], k_ref[:, hs], NT_DIMS, preferred_element_type=F32)
                s = s - f_ref[0, hh:hh + 1, :]
                if on_diagonal:
                    keep = (lax.broadcasted_iota(jnp.int32, (tq, tk), 0)
                            >= lax.broadcasted_iota(jnp.int32, (tq, tk), 1))
                    s = jnp.where(keep, s, NEG)
                m_prev = m_sc[hh]
                m_new = jnp.maximum(m_prev, jnp.max(s, axis=-1, keepdims=True))
                alpha = jnp.exp2(m_prev - m_new)
                p = jnp.exp2(s - jnp.concatenate([m_new] * (tk // LANES), axis=1)).astype(BF16)
                acc_sc[hh] = alpha * acc_sc[hh] + jnp.dot(p, v_ref[:, hs], preferred_element_type=F32)
                m_sc[hh] = m_new

        @pl.when(j < i)
        def _():
            tile(False)

        @pl.when(j == i)
        def _():
            tile(True)
            lane0 = (lax.broadcasted_iota(jnp.int32, (SUBLANES, LANES), 1) == 0).astype(BF16)
            outs, lses = [], []
            for hh in range(2):
                acc = acc_sc[hh]
                lsum = acc[:, LANE_ROWSUM_P:LANE_ROWSUM_P + 1]
                outs.append(acc[:, :HEAD_DIM] / lsum)
                lse_cols = m_sc[hh] + jnp.log2(lsum)
                lses.append(_exact_dot_nt(lane0, lse_cols)[0:1, :])
            o_ref[...] = jnp.concatenate(outs, axis=1).astype(BF16)
            lse_ref[0] = jnp.concatenate(lses, axis=0)

    q_map = lambda h, t, i_ref, j_ref: (i_ref[t], h)
    kv_map = lambda h, t, i_ref, j_ref: (j_ref[t], h)
    return pl.pallas_call(
        body, name=name,
        grid_spec=pltpu.PrefetchScalarGridSpec(
            num_scalar_prefetch=2, grid=(npair, i_of.shape[0]),
            in_specs=[pl.BlockSpec((tq, PAIR_SLOT), q_map), pl.BlockSpec((tk, PAIR_SLOT), kv_map),
                      pl.BlockSpec((tk, PAIR_SLOT), kv_map),
                      pl.BlockSpec((1, 2, tk), lambda h, t, i_ref, j_ref: (h, 0, j_ref[t]))],
            out_specs=[pl.BlockSpec((tq, HEAD_PAIR), q_map),
                       pl.BlockSpec((1, 2, tq), lambda h, t, i_ref, j_ref: (h, 0, i_ref[t]))],
            scratch_shapes=[pltpu.VMEM((2, tq, LANES), F32), pltpu.VMEM((2, tq, HEAD_SLOT), F32)]),
        out_shape=[jax.ShapeDtypeStruct((l, d), BF16), jax.ShapeDtypeStruct((npair, 2, l), F32)],
        compiler_params=_cp("parallel", "arbitrary"),
    )(i_of, j_of, qx, kx, vx, f2_rows)


def _flash_bwd(qx, kx, vx, f2_rep, do, lse_rows, delta_rows, name):
    l = qx.shape[0]
    npair = qx.shape[1] // PAIR_SLOT
    d = npair * HEAD_PAIR
    tq = _tile(l, ATTN_TILE)
    tk = tq
    i_of, j_of = _causal_tiles(l // tq, by_query=False)

    def body(i_ref, j_ref, q_ref, k_ref, v_ref, f_ref, do_ref, lse_ref, dl_ref, dq_ref, dk_ref, dv_ref):
        t = pl.program_id(1)
        i = i_ref[t]
        j = j_ref[t]

        @pl.when(t == 0)
        def _():
            dq_ref[...] = jnp.zeros_like(dq_ref)

        @pl.when(i == j)
        def _():
            dk_ref[...] = jnp.zeros_like(dk_ref)
            dv_ref[...] = jnp.zeros_like(dv_ref)

        def tile(on_diagonal):
            dqs, dks, dvs = [], [], []
            for hh in range(2):
                hs = slice(hh * HEAD_SLOT, (hh + 1) * HEAD_SLOT)
                qh, kh = q_ref[:, hs], k_ref[:, hs]
                vh = v_ref[:, hh * HEAD_SLOT:hh * HEAD_SLOT + HEAD_DIM]
                doh = do_ref[:, hh * HEAD_DIM:(hh + 1) * HEAD_DIM]
                st = lax.dot_general(kh, qh, NT_DIMS, preferred_element_type=F32)
                st = st - jnp.concatenate([f_ref[:, hs]] * (tq // HEAD_SLOT), axis=1)
                pt = jnp.exp2(st - lse_ref[0, hh:hh + 1, :])
                if on_diagonal:
                    keep = (lax.broadcasted_iota(jnp.int32, (tk, tq), 1)
                            >= lax.broadcasted_iota(jnp.int32, (tk, tq), 0))
                    pt = jnp.where(keep, pt, 0.0)
                dpt = lax.dot_general(vh, doh, NT_DIMS, preferred_element_type=F32)
                dsb = (pt * (dpt - dl_ref[0, hh:hh + 1, :])).astype(BF16)
                dvs.append(jnp.dot(pt.astype(BF16), doh, preferred_element_type=F32))
                dks.append(jnp.dot(dsb, qh, preferred_element_type=F32))
                dqs.append(lax.dot_general(dsb, kh, TN_DIMS, preferred_element_type=F32))
            dv_ref[...] += jnp.concatenate(dvs, axis=1)
            dk_ref[...] += jnp.concatenate(dks, axis=1)
            dq_ref[pl.ds(pl.multiple_of(i * tq, tq), tq), :] += jnp.concatenate(dqs, axis=1)

        @pl.when(i > j)
        def _():
            tile(False)

        @pl.when(i == j)
        def _():
            tile(True)

    qmap = lambda h, t, i_ref, j_ref: (i_ref[t], h)
    kmap = lambda h, t, i_ref, j_ref: (j_ref[t], h)
    row_map = lambda h, t, i_ref, j_ref: (h, 0, i_ref[t])
    return pl.pallas_call(
        body, name=name,
        grid_spec=pltpu.PrefetchScalarGridSpec(
            num_scalar_prefetch=2, grid=(npair, i_of.shape[0]),
            in_specs=[pl.BlockSpec((tq, PAIR_SLOT), qmap), pl.BlockSpec((tk, PAIR_SLOT), kmap),
                      pl.BlockSpec((tk, PAIR_SLOT), kmap), pl.BlockSpec((tk, PAIR_SLOT), kmap),
                      pl.BlockSpec((tq, HEAD_PAIR), qmap),
                      pl.BlockSpec((1, 2, tq), row_map), pl.BlockSpec((1, 2, tq), row_map)],
            out_specs=[pl.BlockSpec((l, PAIR_SLOT), lambda h, t, i_ref, j_ref: (0, h)),
                       pl.BlockSpec((tk, PAIR_SLOT), kmap), pl.BlockSpec((tk, HEAD_PAIR), kmap)]),
        out_shape=[jax.ShapeDtypeStruct((l, npair * PAIR_SLOT), F32), jax.ShapeDtypeStruct((l, npair * PAIR_SLOT), F32),
                   jax.ShapeDtypeStruct((l, d), F32)],
        compiler_params=_cp("parallel", "arbitrary"),
    )(i_of, j_of, qx, kx, vx, f2_rep, do, lse_rows, delta_rows)


def _my_place():
    return lax.axis_index("x"), lax.axis_index("y"), lax.axis_index("c")


def _chip_exchange(srcs, out_meta, plan, name):
    n_src, n_out, n_plan = len(srcs), len(out_meta), len(plan)

    def body(*refs):
        src_refs = refs[:n_src]
        out_refs = refs[n_src:n_src + n_out]
        send_sems, recv_sems, local_sems = refs[n_src + n_out:]
        x, y, c = _my_place()
        me = 2 * x + y
        copies = []
        for n, (si, oi, src_view, dst_view) in enumerate(plan):
            local = pltpu.make_async_copy(src_view(src_refs[si], me), dst_view(out_refs[oi], me), local_sems.at[n])
            local.start()
            copies.append(local)
            for k in (1, 2, 3):
                peer = me ^ k
                rc = pltpu.make_async_remote_copy(
                    src_ref=src_view(src_refs[si], peer), dst_ref=dst_view(out_refs[oi], me),
                    send_sem=send_sems.at[n, k - 1], recv_sem=recv_sems.at[n, k - 1],
                    device_id=(peer >> 1, peer & 1, c), device_id_type=MESH)
                rc.start()
                copies.append(rc)
        for cp in copies:
            cp.wait()

    any_spec = pl.BlockSpec(memory_space=pl.ANY)
    return pl.pallas_call(
        body, name=name,
        in_specs=[any_spec] * n_src, out_specs=[any_spec] * n_out,
        out_shape=[jax.ShapeDtypeStruct(shape, dt) for (shape, dt) in out_meta],
        scratch_shapes=[pltpu.SemaphoreType.DMA((n_plan, 3)), pltpu.SemaphoreType.DMA((n_plan, 3)),
                        pltpu.SemaphoreType.DMA((n_plan,))],
    )(*srcs)


def _plan_copies(src_refs, land_refs, plan, send_sems, recv_sems):
    x, y, c = _my_place()
    me = 2 * x + y
    copies = []
    for n, (si, oi, src_view, dst_view) in enumerate(plan):
        for k in (1, 2, 3):
            peer = me ^ k
            copies.append(pltpu.make_async_remote_copy(
                src_ref=src_view(src_refs[si], peer), dst_ref=dst_view(land_refs[oi], me),
                send_sem=send_sems.at[3 * n + k - 1], recv_sem=recv_sems.at[3 * n + k - 1],
                device_id=(peer >> 1, peer & 1, c), device_id_type=MESH))
    return copies


def _hbm(a):
    return pltpu.HBM(a.shape, a.dtype)


def _exchange_start(srcs, lands, plan, name):
    n_src, n_land = len(srcs), len(lands)
    n_buf = n_src + n_land

    def body(*refs):
        send_sems, recv_sems = refs[n_buf], refs[n_buf + 1]
        token = refs[-1]
        for cp in _plan_copies(refs[:n_src], refs[n_src:n_buf], plan, send_sems, recv_sems):
            cp.start()
        token[...] = jnp.zeros_like(token)

    bufs = [pltpu.with_memory_space_constraint(a, pltpu.HBM) for a in (*srcs, *lands)]
    hbm = pl.BlockSpec(memory_space=pltpu.HBM)
    sem = pl.BlockSpec(memory_space=pltpu.SEMAPHORE)
    res = pl.pallas_call(
        body, name=name,
        out_shape=(pltpu.SemaphoreType.DMA((3 * len(plan),)), pltpu.SemaphoreType.DMA((3 * len(plan),)),
                   *[_hbm(a) for a in bufs], jax.ShapeDtypeStruct((SUBLANES, LANES), F32)),
        in_specs=[hbm] * n_buf, out_specs=(sem, sem, *[hbm] * n_buf, pl.BlockSpec(memory_space=pltpu.VMEM)),
        input_output_aliases={n: 2 + n for n in range(n_buf)},
        compiler_params=pltpu.CompilerParams(has_side_effects=pltpu.SideEffectType.DATAFLOW_SIDE_EFFECTING),
    )(*bufs)
    return (res[0], res[1]), list(res[2:2 + n_src]), list(res[2 + n_src:2 + n_buf]), res[-1]


def _exchange_wait(sems, srcs, lands, plan, after, name):
    n_src, n_land = len(srcs), len(lands)
    n_buf = n_src + n_land

    def body(*refs):
        send_sems, recv_sems = refs[n_buf], refs[n_buf + 1]
        for cp in _plan_copies(refs[:n_src], refs[n_src:n_buf], plan, send_sems, recv_sems):
            cp.wait_send()
            cp.wait_recv()

    hbm = pl.BlockSpec(memory_space=pltpu.HBM)
    sem = pl.BlockSpec(memory_space=pltpu.SEMAPHORE)
    res = pl.pallas_call(
        body, name=name, out_shape=tuple(_hbm(a) for a in (*srcs, *lands)),
        in_specs=[hbm] * n_buf + [sem, sem, pl.BlockSpec(memory_space=pl.ANY)], out_specs=tuple([hbm] * n_buf),
        input_output_aliases={n: n for n in range(n_buf)},
        compiler_params=pltpu.CompilerParams(has_side_effects=pltpu.SideEffectType.DATAFLOW_SIDE_EFFECTING),
    )(*srcs, *lands, sems[0], sems[1], after)
    return list(res[n_src:])


def _core_exchange(arrays, name):
    n_items = len(arrays)

    def body(*refs):
        srcs = refs[:n_items]
        outs = refs[n_items:2 * n_items]
        send_sems, recv_sems = refs[2 * n_items:]
        x, y, c = _my_place()
        copies = []
        for n in range(n_items):
            rc = pltpu.make_async_remote_copy(
                src_ref=srcs[n], dst_ref=outs[n], send_sem=send_sems.at[n], recv_sem=recv_sems.at[n],
                device_id=(x, y, 1 - c), device_id_type=MESH)
            rc.start()
            copies.append(rc)
        for cp in copies:
            cp.wait()

    any_spec = pl.BlockSpec(memory_space=pl.ANY)
    return pl.pallas_call(
        body, name=name,
        in_specs=[any_spec] * n_items, out_specs=[any_spec] * n_items,
        out_shape=[jax.ShapeDtypeStruct(a.shape, a.dtype) for a in arrays],
        scratch_shapes=[pltpu.SemaphoreType.DMA((n_items,)), pltpu.SemaphoreType.DMA((n_items,))],
    )(*arrays)


def _sum_chips(parts, name):
    _, rows, cols = parts.shape
    tm = _tile(rows, 512)

    def body(p_ref, o_ref):
        acc = p_ref[0].astype(F32)
        for s in range(1, N_CHIPS):
            acc = acc + p_ref[s].astype(F32)
        o_ref[...] = acc

    return pl.pallas_call(
        body, name=name, grid=(rows // tm,),
        in_specs=[pl.BlockSpec((N_CHIPS, tm, cols), lambda i: (0, i, 0))],
        out_specs=pl.BlockSpec((tm, cols), lambda i: (i, 0)),
        out_shape=jax.ShapeDtypeStruct((rows, cols), F32),
        compiler_params=_cp("parallel"),
    )(parts)


def _adamw(ga, gb, w, m, v, name):
    rows, cols = w.shape
    tm = _tile(rows, 512)
    c1 = 1.0 - ADAM_B1 ** ADAM_STEP
    c2 = 1.0 - ADAM_B2 ** ADAM_STEP

    def body(ga_ref, gb_ref, w_ref, m_ref, v_ref, g_ref, d_ref, nm_ref, nv_ref):
        g = ga_ref[...] + gb_ref[...]
        nm = ADAM_B1 * m_ref[...] + (1.0 - ADAM_B1) * g
        nv = ADAM_B2 * v_ref[...] + (1.0 - ADAM_B2) * (g * g)
        g_ref[...] = g
        nm_ref[...] = nm
        nv_ref[...] = nv
        d_ref[...] = -ADAM_LR * ((nm / c1) / (jnp.sqrt(nv / c2) + ADAM_EPS) + ADAM_WD * w_ref[...])

    spec = pl.BlockSpec((tm, cols), lambda i: (i, 0))
    return pl.pallas_call(
        body, name=name, grid=(rows // tm,), in_specs=[spec] * 5, out_specs=[spec] * 4,
        out_shape=[jax.ShapeDtypeStruct((rows, cols), F32)] * 4, compiler_params=_cp("parallel"),
    )(ga, gb, w, m, v)


def _ssm_discretise(log_dt, a_re, a_im, b_re, b_im):
    dt = jnp.exp(log_dt)[:, None]
    mag = jnp.exp(a_re * dt)
    lbr = mag * jnp.cos(a_im * dt)
    lbi = mag * jnp.sin(a_im * dt)
    den = a_re * a_re + a_im * a_im
    nr, ni = lbr - 1.0, lbi
    qr = (nr * a_re + ni * a_im) / den
    qi = (ni * a_re - nr * a_im) / den
    bbr = qr[..., None] * b_re - qi[..., None] * b_im
    bbi = qr[..., None] * b_im + qi[..., None] * b_re
    return lbr, lbi, bbr, bbi


def _cmul(ar, ai, br, bi):
    return ar * br - ai * bi, ar * bi + ai * br


def _scan_tables(lr, li, nj, reverse):
    lr = lr.reshape(nj, 1, -1)
    li = li.reshape(nj, 1, -1)
    if reverse:
        li = -li
    pows = [(lr, li)]
    for _ in range(7):
        pows.append(_cmul(*pows[-1], lr, li))
    r = jnp.arange(SUBLANES).reshape(1, SUBLANES, 1)
    if reverse:
        r = SUBLANES - 1 - r
    out = []
    for k in (1, 2, 4):
        pr, pi = pows[k - 1]
        keep = (r >= k).astype(F32)
        out += [pr * keep, pi * keep]
    shape = (nj, SUBLANES, lr.shape[-1])
    cr = jnp.zeros(shape, F32)
    ci = jnp.zeros(shape, F32)
    for e in range(SUBLANES):
        sel = (r == e).astype(F32)
        cr = cr + sel * pows[e][0]
        ci = ci + sel * pows[e][1]
    out += [cr, ci]
    return jnp.stack(out, axis=1)


def _group_eye(gl):
    return jnp.eye(gl, dtype=F32)


def _block_diag_in(bbr, bbi, nj):
    g, p, c = bbr.shape
    gl = g // nj
    eye = _group_eye(gl)[None, :, None, :, None]

    def one(b):
        t = b.reshape(nj, gl, p, c).transpose(0, 1, 3, 2)[:, :, :, None, :]
        return (t * eye).reshape(nj, gl * c, gl * p)

    return jnp.concatenate([one(bbr), one(bbi)], axis=2)


def _block_diag_in_grad(gmat, nj, p, c):
    gl = gmat.shape[1] // c
    n = gl * p
    eye = _group_eye(gl)[None, :, None, :, None]

    def one(m):
        t = jnp.sum(m.reshape(nj, gl, c, gl, p) * eye, axis=3)
        return t.transpose(0, 1, 3, 2).reshape(nj * gl, p, c)

    return one(gmat[:, :, :n]), one(gmat[:, :, n:])


def _block_diag_out(c_re, c_im, nj):
    g, c, p = c_re.shape
    gl = g // nj
    eye = _group_eye(gl)[None, :, None, :, None]

    def one(m):
        t = m.reshape(nj, gl, c, p).transpose(0, 1, 3, 2)[:, :, :, None, :]
        return (t * eye).reshape(nj, gl * p, gl * c)

    return jnp.concatenate([one(c_re), -one(c_im)], axis=1)


def _block_diag_out_grad(gmat, nj, p, c):
    gl = gmat.shape[2] // c
    n = gl * p
    eye = _group_eye(gl)[None, :, None, :, None]

    def one(m):
        t = jnp.sum(m.reshape(nj, gl, p, gl, c) * eye, axis=3)
        return t.transpose(0, 1, 3, 2).reshape(nj * gl, c, p)

    return one(gmat[:, :n, :]), -one(gmat[:, n:, :])


def _pad_rows(flat, cols):
    per = SUBLANES * cols
    n = flat.shape[0]
    total = -(-n // per) * per
    return jnp.pad(flat, (0, total - n)).reshape(total // cols, cols)


def _pack_small(arrs, cols):
    packed = jnp.concatenate([_pad_rows(a.reshape(-1), cols) for a in arrs], axis=0)
    rows = packed.shape[0]
    return jnp.pad(packed, ((0, -rows % 128), (0, 0)))


def _unpack_small(packed, shapes, cols):
    out = []
    row = 0
    for s in shapes:
        n = math.prod(s)
        rows = -(-n // (SUBLANES * cols)) * SUBLANES
        out.append(packed[row:row + rows].reshape(-1)[:n].reshape(s))
        row += rows
    return out


def kernel(x, mix_norm, mlp_norm, mlp_w1, mlp_w2, ssm_log_dt, ssm_a_re, ssm_a_im, ssm_b_re, ssm_b_im, ssm_c_re, ssm_c_im, ssm_d, ssm_w_glu, kv_norm, w_kvf, b_f, attn_wq, attn_wo, final_norm, loss_target, m_mix_norm, m_mlp_norm, m_mlp_w1, m_mlp_w2, m_ssm_log_dt, m_ssm_a_re, m_ssm_a_im, m_ssm_b_re, m_ssm_b_im, m_ssm_c_re, m_ssm_c_im, m_ssm_d, m_ssm_w_glu, m_kv_norm, m_w_kvf, m_b_f, m_attn_wq, m_attn_wo, m_final_norm, v_mix_norm, v_mlp_norm, v_mlp_w1, v_mlp_w2, v_ssm_log_dt, v_ssm_a_re, v_ssm_a_im, v_ssm_b_re, v_ssm_b_im, v_ssm_c_re, v_ssm_c_im, v_ssm_d, v_ssm_w_glu, v_kv_norm, v_w_kvf, v_b_f, v_attn_wq, v_attn_wo, v_final_norm):
    seq, d = x.shape[1], x.shape[2]
    depth = mix_norm.shape[0]
    n_a = ssm_log_dt.shape[0]
    n_b = depth - n_a
    ff = mlp_w1.shape[2] * N_CHIPS
    n_heads = d // HEAD_DIM
    n_groups = d // SSM_GROUP
    p_state = ssm_a_re.shape[2]
    gb = min(d, 256)
    nj = d // gb
    kvf_cols = w_kvf.shape[1]
    ds4 = d // N_CHIPS
    chip = 2 * lax.axis_index("x") + lax.axis_index("y")

    def cols_of(width):
        return lambda ref, s: ref.at[:, :, pl.ds(pl.multiple_of(s * width, LANES), width)]

    def rows_of(height):
        return lambda ref, s: ref.at[:, pl.ds(pl.multiple_of(s * height, SUBLANES), height), :]

    whole = lambda ref, s: ref
    slot = lambda ref, s: ref.at[s]
    def cols2(width):
        return lambda ref, s: ref.at[:, pl.ds(pl.multiple_of(s * width, LANES), width)]

    def rows2(height):
        return lambda ref, s: ref.at[pl.ds(pl.multiple_of(s * height, SUBLANES), height), :]

    assert n_a >= 2
    (skip_parts,) = _chip_exchange([ssm_d], [((N_CHIPS, n_a, ds4), F32)], [(0, 0, whole, slot)], "gather_skip")
    skip_all = skip_parts.transpose(1, 0, 2).reshape(n_a, d)
    w1_s, w2_s, glu_s = mlp_w1.astype(BF16), mlp_w2.astype(BF16), ssm_w_glu.astype(BF16)
    src_a = [w1_s[0], w2_s[0], glu_s[0]]
    plan_a = [(0, 0, whole, cols2(d)), (1, 1, whole, rows2(d)), (2, 2, whole, cols2(2 * ds4))]
    land_a = [lax.empty((d, ff), BF16), lax.empty((ff, d), BF16), lax.empty((d, 2 * d), BF16)]
    src_b = [w1_s[1:], w2_s[1:], glu_s[1:], w_kvf.astype(BF16), attn_wq.astype(BF16), attn_wo.astype(BF16)]
    plan_b = [(0, 0, whole, cols_of(d)), (1, 1, whole, rows_of(d)), (2, 2, whole, cols_of(2 * ds4)),
              (3, 3, whole, slot), (4, 4, whole, rows_of(ds4)), (5, 5, whole, rows_of(ds4))]
    land_b = [lax.empty((depth - 1, d, ff), BF16), lax.empty((depth - 1, ff, d), BF16),
              lax.empty((n_a - 1, d, 2 * d), BF16), lax.empty((N_CHIPS, d, kvf_cols), BF16),
              lax.empty((n_b, d, d), BF16), lax.empty((n_b, d, d), BF16)]
    def idx(*at):
        return [jnp.asarray(v, jnp.int32) for v in at]

    put = lax.dynamic_update_slice
    land_a = [put(land_a[0], src_a[0], idx(0, chip * d)), put(land_a[1], src_a[1], idx(chip * d, 0)),
              put(land_a[2], src_a[2], idx(0, chip * 2 * ds4))]
    land_b = [put(land_b[0], src_b[0], idx(0, 0, chip * d)), put(land_b[1], src_b[1], idx(0, chip * d, 0)),
              put(land_b[2], src_b[2], idx(0, 0, chip * 2 * ds4)), put(land_b[3], src_b[3][None], idx(chip, 0, 0)),
              put(land_b[4], src_b[4], idx(0, chip * ds4, 0)), put(land_b[5], src_b[5], idx(0, chip * ds4, 0))]
    sems_a, src_a, land_a, token_a = _exchange_start(src_a, land_a, plan_a, "gather_start_a")
    sems_b, src_b, land_b, token_b = _exchange_start(src_b, land_b, plan_b, "gather_start_b")
    started = token_a[0:1, 0:1] + token_b[0:1, 0:1]

    def layer_w1(i):
        return w1_0 if i == 0 else w1_rest[i - 1]

    def layer_w2(i):
        return w2_0 if i == 0 else w2_rest[i - 1]

    def layer_glu(i):
        return glu_0 if i == 0 else glu_rest[i - 1]

    h = x[0]
    target = loss_target[0]

    saved = []
    for i in range(n_a):
        lbr, lbi, bbr, bbi = _ssm_discretise(ssm_log_dt[i], ssm_a_re[i], ssm_a_im[i], ssm_b_re[i], ssm_b_im[i])
        bblk = _block_diag_in(bbr, bbi, nj)
        cblk = _block_diag_out(ssm_c_re[i], ssm_c_im[i], nj)
        rec = dict(h0=h, lam=(lbr, lbi), bblk=bblk, cblk=cblk)
        gain = mix_norm[i:i + 1] + started if i == 0 else mix_norm[i:i + 1]
        u = _norm_fwd(h, gain, f"s5_norm_{i}")
        rec["u"] = u
        dskip = rec["dskip"] = skip_all[i:i + 1]
        states, y = _scan_fwd(u, bblk.astype(BF16), cblk.astype(BF16), _scan_tables(lbr, lbi, nj, False), dskip,
                              f"s5_scan_{i}")
        rec["states"], rec["y"] = states, y
        if i == 0:
            w1_0, w2_0, glu_0 = _exchange_wait(sems_a, src_a, land_a, plan_a, y, "gather_wait_a")
        if i == 1:
            w1_rest, w2_rest, glu_rest, kvf_parts, wq_all, wo_all = _exchange_wait(
                sems_b, src_b, land_b, plan_b, y, "gather_wait_b")
        h, rec["zw"] = _s5_post_fwd(h, y, layer_glu(i), f"s5_glu_{i}")
        rec["h1"] = h
        h, rec["ap"] = _mlp_fwd(h, mlp_norm[i:i + 1], layer_w1(i), layer_w2(i), f"mlp_{i}")
        saved.append(rec)
    h_kv = h
    kvf_all = jnp.concatenate([kvf_parts[s] for s in range(N_CHIPS)], axis=1)
    wk = kvf_all[:, :d]
    wv = kvf_all[:, d:2 * d]
    wf = jnp.pad(kvf_all[:, 2 * d:], ((0, 0), (0, LANES - n_heads)))
    bf_row = jnp.pad(b_f, (0, LANES - n_heads)).reshape(1, LANES)
    wk_x = _slot_cols(wk)
    spread = (jnp.arange(LANES)[:, None] == jnp.arange(n_heads * HEAD_SLOT)[None, :] // HEAD_SLOT).astype(BF16)
    kx, vx, flog, cum, f2_rep = _kvf_fwd(h, kv_norm.reshape(1, d), wk_x, _slot_cols(wv), wf, bf_row,
                                         _slot_ones(n_heads, LANE_ROWSUM_DS), _slot_ones(n_heads, LANE_ROWSUM_P),
                                         spread, "kvf")
    f2_rows = (cum[:, :n_heads] * LOG2E).T.reshape(n_heads // 2, 2, seq)
    wq_x = [_slot_cols(wq_all[jb]) for jb in range(n_b)]
    for jb in range(n_b):
        i = n_a + jb
        rec = dict(h0=h)
        qx = _q_fwd(h, mix_norm[i:i + 1], wq_x[jb], _slot_ones(n_heads, LANE_COLSUM_DS), f"attn_q_{jb}")
        o, lse = _flash_fwd(qx, kx, vx, f2_rows, f"attn_core_{jb}")
        rec["qx"], rec["o"], rec["lse"] = qx, o, lse
        h = _o_fwd(h, o, wo_all[jb], f"attn_out_{jb}")
        rec["h1"] = h
        h, rec["ap"] = _mlp_fwd(h, mlp_norm[i:i + 1], layer_w1(i), layer_w2(i), f"mlp_{i}")
        saved.append(rec)
    dh, loss_row, g_final = _loss_head(h, target, final_norm.reshape(1, d), "loss_head")
    loss = lax.psum(loss_row[0, 0], ("x", "y", "c"))

    head_sel = (jnp.arange(n_heads)[:, None] == jnp.arange(d)[None, :] // HEAD_DIM).astype(BF16)
    slot_lane = jnp.arange(n_heads * HEAD_SLOT)[:, None]
    in_lane = jnp.arange(LANES)[None, :]
    sel_q = (slot_lane == in_lane * HEAD_SLOT + LANE_ROWSUM_DS).astype(BF16)
    sel_k = (slot_lane == in_lane * HEAD_SLOT + LANE_COLSUM_DS).astype(BF16)
    g_mix = [None] * depth
    g_mlp = [None] * depth
    g_w1 = [None] * depth
    g_w2 = [None] * depth
    g_wq = [None] * n_b
    g_wo = [None] * n_b
    g_glu = [None] * n_a
    g_ssm = [None] * n_a
    dk_parts, dv_parts, dq_parts = [], [], []

    red_waits = []

    def reduce_start(entries, name):
        numbers = sorted({e[1] for e in entries})
        lands = {}
        for (_, ln, shape, dt, _, _, _, _) in entries:
            if ln not in lands:
                lands[ln] = lax.empty(shape, dt)
        for (_, ln, _, _, _, _, own, at) in entries:
            lands[ln] = lax.dynamic_update_slice(lands[ln], own, idx(*at))
        plan = [(n, numbers.index(e[1]), e[4], e[5]) for n, e in enumerate(entries)]
        sems, srcs, lands_t, token = _exchange_start([e[0] for e in entries], [lands[ln] for ln in numbers], plan,
                                                     name + "_start")
        red_waits.append((sems, srcs, lands_t, plan, name + "_wait"))
        return token[0:1, 0:1]

    def w1_entry(i, ln, local, n_layers):
        own = lax.dynamic_slice(g_w1[i], idx(0, chip * d), (d, d))[None, None]
        return (g_w1[i], ln, (N_CHIPS, n_layers, d, d), BF16, cols2(d), into(local), own, (chip, local, 0, 0))

    def w2_entry(i, ln, local, n_layers):
        own = lax.dynamic_slice(g_w2[i], idx(chip * d, 0), (d, d))[None, None]
        return (g_w2[i], ln, (N_CHIPS, n_layers, d, d), BF16, rows2(d), into(local), own, (chip, local, 0, 0))

    def glu_entry(i, ln, local, n_layers):
        own = lax.dynamic_slice(g_glu[i], idx(0, chip * 2 * ds4), (d, 2 * ds4))[None, None]
        return (g_glu[i], ln, (N_CHIPS, n_layers, d, 2 * ds4), BF16, cols2(2 * ds4), into(local), own,
                (chip, local, 0, 0))

    def rows_entry(g, ln, local):
        own = lax.dynamic_slice(g, idx(chip * ds4, 0), (ds4, d))[None, None]
        return (g, ln, (N_CHIPS, n_b, ds4, d), BF16, rows2(ds4), into(local), own, (chip, local, 0, 0))

    def into(layer):
        return lambda ref, s: ref.at[s, layer]

    def mlp_back(dh, i, rec, tie=None):
        gain = mlp_norm[i:i + 1] if tie is None else mlp_norm[i:i + 1] + tie
        dh_in, hm, a, dap, g_mlp[i] = _mlp_bwd(dh, rec["h1"], rec["ap"], gain, layer_w1(i),
                                               layer_w2(i), f"mlp_bwd_{i}")
        g_w2[i] = _matmul_tn(a, dh, f"mlp_dw2_{i}")
        g_w1[i] = _matmul_tn(hm, dap, f"mlp_dw1_{i}")
        return dh_in

    for jb in reversed(range(n_b)):
        i = n_a + jb
        rec = saved[i]
        dh = mlp_back(dh, i, rec)
        do, delta = _o_bwd(dh, rec["o"], wo_all[jb], head_sel, f"attn_out_bwd_{jb}")
        g_wo[jb] = _matmul_tn(rec["o"], dh, f"attn_dwo_{jb}")
        dqx, dkx, dv = _flash_bwd(rec["qx"], kx, vx, f2_rep, do, rec["lse"], delta.reshape(n_heads // 2, 2, seq),
                                  f"attn_core_bwd_{jb}")
        dk_parts.append(dkx)
        dv_parts.append(dv)
        dq_parts.append(dqx)
        dh, hn, dqs, g_mix[i] = _q_bwd(dh, rec["h0"], dqx, mix_norm[i:i + 1], wq_x[jb], f"attn_q_bwd_{jb}")
        g_wq[jb] = _unslot_cols(_matmul_tn(hn, dqs, f"attn_dwq_{jb}"))

    dh, hk, dkb, dvb, dfb, g_kvn, g_bf = _kvf_bwd(dh, h_kv, dk_parts[0], dk_parts[1], dv_parts[0], dv_parts[1],
                                                  dq_parts[0], dq_parts[1], flog, kv_norm.reshape(1, d), wk_x, wv, wf,
                                                  sel_q, sel_k, "kvf_bwd")
    g_kvf = jnp.concatenate([_unslot_cols(_matmul_tn(hk, dkb, "kvf_dwk")), _matmul_tn(hk, dvb, "kvf_dwv"),
                             _matmul_tn(hk, dfb, "kvf_dwf")[:, :n_heads]], axis=1)
    kvf_send = g_kvf.reshape(d, N_CHIPS, kvf_cols).transpose(1, 0, 2)
    group = [w1_entry(n_a + jb, 0, jb, n_b) for jb in range(n_b)]
    group += [w2_entry(n_a + jb, 1, jb, n_b) for jb in range(n_b)]
    group.append((kvf_send, 2, (N_CHIPS, d, kvf_cols), BF16, slot, slot,
                  lax.dynamic_index_in_dim(kvf_send, chip, 0, keepdims=True), (chip, 0, 0)))
    group += [rows_entry(g_wq[jb], 3, jb) for jb in range(n_b)]
    group += [rows_entry(g_wo[jb], 4, jb) for jb in range(n_b)]
    tie = reduce_start(group, "reduce_attn")

    for i in reversed(range(n_a)):
        rec = saved[i]
        if i == 0:
            group = [w1_entry(l, 0, l - 1, n_a - 1) for l in range(1, n_a)]
            group += [w2_entry(l, 1, l - 1, n_a - 1) for l in range(1, n_a)]
            group += [glu_entry(l, 2, l - 1, n_a - 1) for l in range(1, n_a)]
            tie = reduce_start(group, "reduce_s5")
        dh = mlp_back(dh, i, rec, tie if i in (0, n_a - 1) else None)
        dy, z, dzw = _s5_post_bwd(dh, rec["y"], rec["zw"], layer_glu(i), f"s5_glu_bwd_{i}")
        g_glu[i] = _matmul_tn(z, dzw, f"s5_dwglu_{i}")
        skip_gain = rec["dskip"]
        if i == 0:
            skip_gain = skip_gain + reduce_start([w1_entry(0, 0, 0, 1), w2_entry(0, 1, 0, 1), glu_entry(0, 2, 0, 1)],
                                                 "reduce_first")
        lbr, lbi = rec["lam"]
        bblk_t = rec["bblk"].transpose(0, 2, 1).astype(BF16)
        cblk_t = rec["cblk"].transpose(0, 2, 1).astype(BF16)
        du, glam8, gd8, gbblk, gcblk = _scan_bwd(dy, rec["u"], rec["states"], bblk_t, cblk_t,
                                                 _scan_tables(lbr, lbi, nj, True), skip_gain, f"s5_scan_bwd_{i}")
        dh, g_mix[i] = _norm_bwd_add(dh, du, rec["h0"], mix_norm[i:i + 1], f"s5_norm_bwd_{i}")
        glam = jnp.sum(glam8, axis=1)
        n_st = glam.shape[1] // 2
        g_lbr = glam[:, :n_st].reshape(n_groups, p_state)
        g_lbi = glam[:, n_st:].reshape(n_groups, p_state)
        g_bbr, g_bbi = _block_diag_in_grad(gbblk, nj, p_state, SSM_GROUP)
        g_cre, g_cim = _block_diag_out_grad(gcblk, nj, p_state, SSM_GROUP)
        _, pull = jax.vjp(_ssm_discretise, ssm_log_dt[i], ssm_a_re[i], ssm_a_im[i], ssm_b_re[i], ssm_b_im[i])
        g_ldt, g_are, g_aim, g_bre, g_bim = pull((g_lbr, g_lbi, g_bbr, g_bbi))
        g_ssm[i] = dict(log_dt=g_ldt, a_re=g_are, a_im=g_aim, b_re=g_bre, b_im=g_bim, c_re=g_cre, c_im=g_cim,
                        d=jnp.sum(gd8, axis=1).reshape(d))
    grad_x = dh[None]

    def stack_small(key):
        return jnp.stack([g_ssm[i][key] for i in range(n_a)])

    small_grads = [jnp.concatenate(g_mix, axis=0), jnp.concatenate(g_mlp, axis=0), stack_small("log_dt"),
                   stack_small("a_re"), stack_small("a_im"), stack_small("b_re"), stack_small("b_im"),
                   stack_small("c_re"), stack_small("c_im"), stack_small("d"), g_kvn.reshape(d),
                   g_bf[0, :n_heads], g_final.reshape(d)]
    small_w = [mix_norm, mlp_norm, ssm_log_dt, ssm_a_re, ssm_a_im, ssm_b_re, ssm_b_im, ssm_c_re, ssm_c_im,
               ssm_d, kv_norm, b_f, final_norm]
    small_m = [m_mix_norm, m_mlp_norm, m_ssm_log_dt, m_ssm_a_re, m_ssm_a_im, m_ssm_b_re, m_ssm_b_im, m_ssm_c_re,
               m_ssm_c_im, m_ssm_d, m_kv_norm, m_b_f, m_final_norm]
    small_v = [v_mix_norm, v_mlp_norm, v_ssm_log_dt, v_ssm_a_re, v_ssm_a_im, v_ssm_b_re, v_ssm_b_im, v_ssm_c_re,
               v_ssm_c_im, v_ssm_d, v_kv_norm, v_b_f, v_final_norm]
    skip_at = 9

    def widen_skip(part):
        return lax.dynamic_update_slice(jnp.zeros((n_a, d), F32), part, (0, chip * ds4))

    small_shapes = [a.shape for a in small_grads]
    pcols = 1024 if d >= 1024 else LANES
    g_small = _pack_small(small_grads, pcols)
    expand = lambda lst: _pack_small([widen_skip(a) if n == skip_at else a for n, a in enumerate(lst)], pcols)
    w_small, m_small, v_small = expand(small_w), expand(small_m), expand(small_v)
    srows = g_small.shape[0]

    assert n_b == 2
    reduce_start([(g_small, 0, (N_CHIPS, srows, pcols), F32, whole, slot, g_small[None], (chip, 0, 0))], "reduce_small")
    landed = [_exchange_wait(sems, srcs, lands, plan, dh, name) for (sems, srcs, lands, plan, name) in red_waits[:3]]
    (a_w1, a_w2, a_kvf, a_wq, a_wo), (s_w1, s_w2, s_glu), (f_w1, f_w2, f_glu) = landed

    def chip_sum(r, name):
        return _sum_chips(r.reshape(N_CHIPS, -1, r.shape[-1]), name)

    sums = [jnp.concatenate([chip_sum(f_w1, "sum_w1_first"), chip_sum(s_w1, "sum_w1_s5"),
                             chip_sum(a_w1, "sum_w1_attn")], axis=0),
            jnp.concatenate([chip_sum(f_w2, "sum_w2_first"), chip_sum(s_w2, "sum_w2_s5"),
                             chip_sum(a_w2, "sum_w2_attn")], axis=0),
            jnp.concatenate([chip_sum(f_glu, "sum_glu_first"), chip_sum(s_glu, "sum_glu_s5")], axis=0),
            chip_sum(a_kvf, "sum_kvf"), chip_sum(a_wq, "sum_wq"), chip_sum(a_wo, "sum_wo")]
    others = _core_exchange(sums, "reduce_cores")

    def two(a):
        return a.reshape(-1, a.shape[-1])

    big_w = [(mlp_w1, m_mlp_w1, v_mlp_w1), (mlp_w2, m_mlp_w2, v_mlp_w2), (ssm_w_glu, m_ssm_w_glu, v_ssm_w_glu),
             (w_kvf, m_w_kvf, v_w_kvf), (attn_wq, m_attn_wq, v_attn_wq), (attn_wo, m_attn_wo, v_attn_wo)]
    big_out = []
    for n, (w, m, v) in enumerate(big_w):
        res = _adamw(sums[n], others[n], two(w), two(m), two(v), f"adamw_{n}")
        big_out.append([r.reshape(w.shape) for r in res])
    sems, srcs, lands, plan, name = red_waits[3]
    (r_small,) = _exchange_wait(sems, srcs, lands, plan, big_out[-1][1], name)
    sum_small = chip_sum(r_small, "sum_small")
    (other_small,) = _core_exchange([sum_small], "reduce_cores_small")
    small_out = _adamw(sum_small, other_small, w_small, m_small, v_small, "adamw_small")

    def narrow_skip(a):
        return lax.dynamic_slice(a, (0, chip * ds4), (n_a, ds4))

    unpacked = []
    for packed in small_out:
        parts = _unpack_small(packed, small_shapes, pcols)
        parts[skip_at] = narrow_skip(parts[skip_at])
        unpacked.append(parts)

    order = ["mix_norm", "mlp_norm", "mlp_w1", "mlp_w2", "ssm_log_dt", "ssm_a_re", "ssm_a_im", "ssm_b_re",
             "ssm_b_im", "ssm_c_re", "ssm_c_im", "ssm_d", "ssm_w_glu", "kv_norm", "w_kvf", "b_f", "attn_wq",
             "attn_wo", "final_norm"]
    small_names = ["mix_norm", "mlp_norm", "ssm_log_dt", "ssm_a_re", "ssm_a_im", "ssm_b_re", "ssm_b_im",
                   "ssm_c_re", "ssm_c_im", "ssm_d", "kv_norm", "b_f", "final_norm"]
    big_names = ["mlp_w1", "mlp_w2", "ssm_w_glu", "w_kvf", "attn_wq", "attn_wo"]
    outs = [loss, grad_x]
    for kind in range(4):
        for name in order:
            if name in big_names:
                outs.append(big_out[big_names.index(name)][kind])
            else:
                outs.append(unpacked[kind][small_names.index(name)])
    return tuple(outs)
```

```python
import functools
import math

import jax
import jax.numpy as jnp
from jax import lax
from jax.experimental import pallas as pl
from jax.experimental.pallas import tpu as pltpu

F32 = jnp.float32
BF16 = jnp.bfloat16

RMS_EPS = 1e-6
SSM_GROUP = 16
SSM_STATE = 64
HEAD_DIM = 64
HEAD_PAIR = 2 * HEAD_DIM
LANES = 128
SUBLANES = 8
N_CHIPS = 4
ADAM_LR = 0.001
ADAM_B1 = 0.9
ADAM_B2 = 0.999
ADAM_EPS = 1e-08
ADAM_WD = 0.01
ADAM_STEP = 10
GELU_C = math.sqrt(2.0 / math.pi)
GELU_A = 0.044715
NEG = -1e30
LN2 = math.log(2.0)
LOG2E = 1.0 / LN2
VMEM_LIMIT = 56 * 1024 * 1024
MESH = pl.DeviceIdType.MESH

NT_DIMS = (((1,), (1,)), ((), ()))
TN_DIMS = (((0,), (0,)), ((), ()))


def _cp(*sem):
    return pltpu.CompilerParams(dimension_semantics=sem if sem else None, vmem_limit_bytes=VMEM_LIMIT)


def _zero_idx(nd, *_):
    return (0,) * nd


def _tile(n, t):
    if n <= t:
        return n
    for cand in range(t - t % SUBLANES, 0, -SUBLANES):
        if n % cand == 0:
            return cand
    raise ValueError((n, t))


def _rms_fwd(h, g):
    r = lax.rsqrt(jnp.mean(h * h, axis=-1, keepdims=True) + RMS_EPS)
    hhat = h * r
    return hhat * g, hhat, r


def _rms_bwd(du, hhat, r, g):
    dhh = du * g
    dh = r * (dhh - hhat * jnp.mean(dhh * hhat, axis=-1, keepdims=True))
    return dh, du * hhat


def _sigmoid(x):
    return 1.0 / (1.0 + jnp.exp(-x))


def _gelu(x):
    t = jnp.tanh(GELU_C * (x + GELU_A * x * x * x))
    return 0.5 * x * (1.0 + t)


def _gelu_grad(x):
    t = jnp.tanh(GELU_C * (x + GELU_A * x * x * x))
    return 0.5 * (1.0 + t) + 0.5 * x * (1.0 - t * t) * GELU_C * (1.0 + 3.0 * GELU_A * x * x)


def _row_fold(x):
    tm, w = x.shape
    return jnp.sum(x.reshape(tm // SUBLANES, SUBLANES, w), axis=0)


def _split3(x):
    hi = x.astype(BF16)
    r1 = x - hi.astype(F32)
    mid = r1.astype(BF16)
    lo = (r1 - mid.astype(F32)).astype(BF16)
    return hi, mid, lo


def _exact_dot(ones_mat, x):
    hi, mid, lo = _split3(x)
    d = functools.partial(jnp.dot, preferred_element_type=F32)
    return d(ones_mat, hi) + d(ones_mat, mid) + d(ones_mat, lo)


def _rows_call(body, name, tm, row_ins, const_ins, row_outs, acc_outs=(), scratch=(), reverse=False, col_outs=()):
    n = row_ins[0].shape[0]
    nb = n // tm
    if reverse:
        ridx = lambda i: (nb - 1 - i, 0)
        cidx = lambda i: (0, nb - 1 - i)
    else:
        ridx = lambda i: (i, 0)
        cidx = lambda i: (0, i)
    in_specs = [pl.BlockSpec((tm, a.shape[1]), ridx) for a in row_ins]
    in_specs += [pl.BlockSpec(a.shape, functools.partial(_zero_idx, a.ndim), pipeline_mode=pl.Buffered(1))
                 for a in const_ins]
    out_shape = [jax.ShapeDtypeStruct((n, w), dt) for (w, dt) in row_outs]
    out_shape += [jax.ShapeDtypeStruct(s, dt) for (s, dt) in acc_outs]
    out_shape += [jax.ShapeDtypeStruct((r, n), dt) for (r, dt) in col_outs]
    out_specs = [pl.BlockSpec((tm, w), ridx) for (w, dt) in row_outs]
    out_specs += [pl.BlockSpec(s, functools.partial(_zero_idx, len(s))) for (s, dt) in acc_outs]
    out_specs += [pl.BlockSpec((r, tm), cidx) for (r, dt) in col_outs]
    return pl.pallas_call(
        body, name=name, grid=(nb,), in_specs=in_specs, out_specs=out_specs, out_shape=out_shape,
        scratch_shapes=list(scratch), compiler_params=_cp("arbitrary"),
    )(*row_ins, *const_ins)


def _norm_fwd(h, g, name):
    n, d = h.shape
    tm = _tile(n, 512)

    def body(h_ref, g_ref, u_ref):
        u_ref[...] = _rms_fwd(h_ref[...], g_ref[...])[0]

    return _rows_call(body, name, tm, [h], [g], [(d, F32)])[0]


def _norm_bwd_add(dh, du, h, g, name):
    n, d = h.shape
    tm = _tile(n, 512)
    nb = n // tm

    def body(dh_ref, du_ref, h_ref, g_ref, o_ref, dg_ref, acc):
        i = pl.program_id(0)

        @pl.when(i == 0)
        def _():
            acc[...] = jnp.zeros_like(acc)

        gain = g_ref[...]
        _, hhat, r = _rms_fwd(h_ref[...], gain)
        dhn, dgr = _rms_bwd(du_ref[...], hhat, r, gain)
        o_ref[...] = dh_ref[...] + dhn
        acc[...] += _row_fold(dgr)

        @pl.when(i == nb - 1)
        def _():
            dg_ref[...] = jnp.sum(acc[...], axis=0, keepdims=True)

    return _rows_call(body, name, tm, [dh, du, h], [g], [(d, F32)], [((1, d), F32)],
                      [pltpu.VMEM((SUBLANES, d), F32)])


def _mlp_fwd(h, g, w1, w2, name):
    n, d = h.shape
    ff = w1.shape[1]
    tm = _tile(n, 256)
    fc = _tile(ff, 1024)

    def body(h_ref, g_ref, w1_ref, w2_ref, o_ref, ap_ref):
        hin = h_ref[...]
        hb = _rms_fwd(hin, g_ref[...])[0].astype(BF16)
        acc = hin
        for c in range(ff // fc):
            cs = slice(c * fc, (c + 1) * fc)
            ap = jnp.dot(hb, w1_ref[:, cs], preferred_element_type=F32)
            ap_ref[:, cs] = ap.astype(BF16)
            rl = jnp.maximum(ap, 0.0)
            acc = acc + jnp.dot((rl * rl).astype(BF16), w2_ref[cs, :], preferred_element_type=F32)
        o_ref[...] = acc

    return _rows_call(body, name, tm, [h], [g, w1, w2], [(d, F32), (ff, BF16)])


def _mlp_bwd(dh, h, ap, g, w1, w2, name):
    n, d = h.shape
    ff = w1.shape[1]
    tm = _tile(n, 256)
    nb = n // tm
    fc = _tile(ff, 1024)

    def body(dh_ref, h_ref, ap_ref, g_ref, w1_ref, w2_ref, o_ref, hm_ref, a_ref, dap_ref, dg_ref, acc):
        i = pl.program_id(0)

        @pl.when(i == 0)
        def _():
            acc[...] = jnp.zeros_like(acc)

        gain = g_ref[...]
        dhv = dh_ref[...]
        hm, hhat, r = _rms_fwd(h_ref[...], gain)
        hm_ref[...] = hm.astype(BF16)
        dhb = dhv.astype(BF16)
        dhm = jnp.zeros((tm, d), F32)
        for c in range(ff // fc):
            cs = slice(c * fc, (c + 1) * fc)
            rl = jnp.maximum(ap_ref[:, cs].astype(F32), 0.0)
            a_ref[:, cs] = (rl * rl).astype(BF16)
            da = lax.dot_general(dhb, w2_ref[cs, :], NT_DIMS, preferred_element_type=F32)
            dap = (da * (2.0 * rl)).astype(BF16)
            dap_ref[:, cs] = dap
            dhm = dhm + lax.dot_general(dap, w1_ref[:, cs], NT_DIMS, preferred_element_type=F32)
        dhn, dgr = _rms_bwd(dhm, hhat, r, gain)
        o_ref[...] = dhv + dhn
        acc[...] += _row_fold(dgr)

        @pl.when(i == nb - 1)
        def _():
            dg_ref[...] = jnp.sum(acc[...], axis=0, keepdims=True)

    return _rows_call(body, name, tm, [dh, h, ap], [g, w1, w2],
                      [(d, F32), (d, BF16), (ff, BF16), (ff, BF16)], [((1, d), F32)],
                      [pltpu.VMEM((SUBLANES, d), F32)])


def _s5_post_fwd(h, y, w_glu, name):
    n, d = h.shape
    tm = _tile(n, 512)

    def body(h_ref, y_ref, w_ref, o_ref, zw_ref):
        z = _gelu(y_ref[...]).astype(BF16)
        zw = jnp.dot(z, w_ref[...], preferred_element_type=F32)
        zw_ref[...] = zw.astype(BF16)
        o_ref[...] = h_ref[...] + zw[:, :d] * _sigmoid(zw[:, d:])

    return _rows_call(body, name, tm, [h, y], [w_glu], [(d, F32), (2 * d, BF16)])


def _s5_post_bwd(dh, y, zw, w_glu, name):
    n, d = dh.shape
    tm = _tile(n, 512)

    def body(dh_ref, y_ref, zw_ref, w_ref, dy_ref, z_ref, dzw_ref):
        dhv = dh_ref[...]
        yv = y_ref[...]
        val = zw_ref[:, :d].astype(F32)
        sg = _sigmoid(zw_ref[:, d:].astype(F32))
        dzw = jnp.concatenate([dhv * sg, dhv * val * sg * (1.0 - sg)], axis=1).astype(BF16)
        dzw_ref[...] = dzw
        dz = lax.dot_general(dzw, w_ref[...], NT_DIMS, preferred_element_type=F32)
        dy_ref[...] = dz * _gelu_grad(yv)
        z_ref[...] = _gelu(yv).astype(BF16)

    return _rows_call(body, name, tm, [dh, y, zw], [w_glu], [(d, F32), (d, BF16), (2 * d, BF16)])


def _q_fwd(h, g, wq_x, ones_x, name):
    n, d = h.shape
    tm = _tile(n, 512)
    scale = LOG2E * HEAD_DIM ** -0.5

    def body(h_ref, g_ref, w_ref, one_ref, q_ref):
        hb = _rms_fwd(h_ref[...], g_ref[...])[0].astype(BF16)
        q_ref[...] = (jnp.dot(hb, w_ref[...], preferred_element_type=F32) * scale + one_ref[...]).astype(BF16)

    return _rows_call(body, name, tm, [h], [g, wq_x, ones_x], [(wq_x.shape[1], BF16)])[0]


def _q_bwd(dh, h, dq, g, wq, name):
    n, d = h.shape
    tm = _tile(n, 512)
    nb = n // tm
    scale = HEAD_DIM ** -0.5

    def body(dh_ref, h_ref, dq_ref, g_ref, w_ref, o_ref, hn_ref, dqs_ref, dg_ref, acc):
        i = pl.program_id(0)

        @pl.when(i == 0)
        def _():
            acc[...] = jnp.zeros_like(acc)

        gain = g_ref[...]
        hn, hhat, r = _rms_fwd(h_ref[...], gain)
        hn_ref[...] = hn.astype(BF16)
        dqs = (dq_ref[...] * scale).astype(BF16)
        dqs_ref[...] = dqs
        dhn = lax.dot_general(dqs, w_ref[...], NT_DIMS, preferred_element_type=F32)
        dhi, dgr = _rms_bwd(dhn, hhat, r, gain)
        o_ref[...] = dh_ref[...] + dhi
        acc[...] += _row_fold(dgr)

        @pl.when(i == nb - 1)
        def _():
            dg_ref[...] = jnp.sum(acc[...], axis=0, keepdims=True)

    return _rows_call(body, name, tm, [dh, h, dq], [g, wq], [(d, F32), (d, BF16), (wq.shape[1], BF16)],
                      [((1, d), F32)], [pltpu.VMEM((SUBLANES, d), F32)])


def _o_fwd(h, o, wo, name):
    n, d = h.shape
    tm = _tile(n, 512)

    def body(h_ref, o_ref, w_ref, out_ref):
        out_ref[...] = h_ref[...] + jnp.dot(o_ref[...], w_ref[...], preferred_element_type=F32)

    return _rows_call(body, name, tm, [h, o], [wo], [(d, F32)])[0]


def _exact_dot_nt(ones_mat, x):
    hi, mid, lo = _split3(x)
    d = functools.partial(lax.dot_general, dimension_numbers=NT_DIMS, preferred_element_type=F32)
    return d(ones_mat, hi) + d(ones_mat, mid) + d(ones_mat, lo)


def _o_bwd(dh, o, wo, head_sel, name):
    n, d = dh.shape
    tm = _tile(n, 512)

    def body(dh_ref, o_ref, w_ref, e_ref, do_ref, dl_ref):
        do = lax.dot_general(dh_ref[...].astype(BF16), w_ref[...], NT_DIMS, preferred_element_type=F32).astype(BF16)
        do_ref[...] = do
        dl_ref[...] = _exact_dot_nt(e_ref[...], do.astype(F32) * o_ref[...].astype(F32))

    return _rows_call(body, name, tm, [dh, o], [wo, head_sel], [(wo.shape[0], BF16)],
                      col_outs=[(head_sel.shape[0], F32)])


def _exact_dot_rhs(x, ones_mat):
    hi, mid, lo = _split3(x)
    d = functools.partial(jnp.dot, preferred_element_type=F32)
    return d(hi, ones_mat) + d(mid, ones_mat) + d(lo, ones_mat)


def _kvf_fwd(h, g, wk, wv, wf, bf, k_ones, v_ones, spread, name):
    n, d = h.shape
    tm = _tile(n, 512)

    def body(h_ref, g_ref, wk_ref, wv_ref, wf_ref, bf_ref, ko_ref, vo_ref, sp_ref,
             k_ref, v_ref, fl_ref, cum_ref, rep_ref, carry):
        i = pl.program_id(0)

        @pl.when(i == 0)
        def _():
            carry[...] = jnp.zeros_like(carry)

        hb = _rms_fwd(h_ref[...], g_ref[...])[0].astype(BF16)
        k_ref[...] = (jnp.dot(hb, wk_ref[...], preferred_element_type=F32) + ko_ref[...]).astype(BF16)
        v_ref[...] = (jnp.dot(hb, wv_ref[...], preferred_element_type=F32) + vo_ref[...]).astype(BF16)
        fl = jnp.dot(hb, wf_ref[...], preferred_element_type=F32) + bf_ref[...]
        fl_ref[...] = fl
        logf = jnp.minimum(fl, 0.0) - jnp.log(1.0 + jnp.exp(-jnp.abs(fl)))
        rows = lax.broadcasted_iota(jnp.int32, (tm, tm), 0)
        cols = lax.broadcasted_iota(jnp.int32, (tm, tm), 1)
        lower = (rows >= cols).astype(BF16)
        cum = _exact_dot(lower, logf) + carry[0:1, :]
        cum_ref[...] = cum
        rep_ref[...] = _exact_dot_rhs(cum * LOG2E, sp_ref[...])
        carry[...] = jnp.broadcast_to(cum[tm - 1:tm, :], carry.shape)

    return _rows_call(body, name, tm, [h], [g, wk, wv, wf, bf, k_ones, v_ones, spread],
                      [(wk.shape[1], BF16), (wv.shape[1], BF16), (LANES, F32), (LANES, F32), (spread.shape[1], F32)],
                      scratch=[pltpu.VMEM((SUBLANES, LANES), F32)])


def _kvf_bwd(dh, h, dk1, dk2, dv1, dv2, dq1, dq2, fl, g, wk, wv, wf, sel_q, sel_k, name):
    n, d = h.shape
    tm = _tile(n, 256)
    nb = n // tm

    def body(dh_ref, h_ref, dk1_ref, dk2_ref, dv1_ref, dv2_ref, dq1_ref, dq2_ref, fl_ref,
             g_ref, wk_ref, wv_ref, wf_ref, sq_ref, sk_ref,
             o_ref, hk_ref, dk_ref, dv_ref, df_ref, dg_ref, db_ref, acc, bacc, carry):
        i = pl.program_id(0)

        @pl.when(i == 0)
        def _():
            acc[...] = jnp.zeros_like(acc)
            bacc[...] = jnp.zeros_like(bacc)
            carry[...] = jnp.zeros_like(carry)

        dkx = dk1_ref[...] + dk2_ref[...]
        dcum = _exact_dot_rhs(dq1_ref[...] + dq2_ref[...], sq_ref[...]) - _exact_dot_rhs(dkx, sk_ref[...])
        rows = lax.broadcasted_iota(jnp.int32, (tm, tm), 0)
        cols = lax.broadcasted_iota(jnp.int32, (tm, tm), 1)
        upper = (rows <= cols).astype(BF16)
        dlogf = _exact_dot(upper, dcum) + carry[0:1, :]
        carry[...] = jnp.broadcast_to(dlogf[0:1, :], carry.shape)
        df = dlogf / (1.0 + jnp.exp(fl_ref[...]))
        dfb = df.astype(BF16)
        df_ref[...] = dfb
        bacc[...] += _row_fold(df)
        dkb = (dkx * LN2).astype(BF16)
        dvb = (dv1_ref[...] + dv2_ref[...]).astype(BF16)
        dk_ref[...] = dkb
        dv_ref[...] = dvb
        gain = g_ref[...]
        hk, hhat, r = _rms_fwd(h_ref[...], gain)
        hk_ref[...] = hk.astype(BF16)
        dhk = lax.dot_general(dkb, wk_ref[...], NT_DIMS, preferred_element_type=F32)
        dhk = dhk + lax.dot_general(dvb, wv_ref[...], NT_DIMS, preferred_element_type=F32)
        dhk = dhk + lax.dot_general(dfb, wf_ref[...], NT_DIMS, preferred_element_type=F32)
        dhi, dgr = _rms_bwd(dhk, hhat, r, gain)
        o_ref[...] = dh_ref[...] + dhi
        acc[...] += _row_fold(dgr)

        @pl.when(i == nb - 1)
        def _():
            dg_ref[...] = jnp.sum(acc[...], axis=0, keepdims=True)
            db_ref[...] = jnp.sum(bacc[...], axis=0, keepdims=True)

    return _rows_call(body, name, tm, [dh, h, dk1, dk2, dv1, dv2, dq1, dq2, fl], [g, wk, wv, wf, sel_q, sel_k],
                      [(d, F32), (d, BF16), (wk.shape[1], BF16), (wv.shape[1], BF16), (LANES, BF16)],
                      [((1, d), F32), ((1, LANES), F32)],
                      [pltpu.VMEM((SUBLANES, d), F32), pltpu.VMEM((SUBLANES, LANES), F32),
                       pltpu.VMEM((SUBLANES, LANES), F32)], reverse=True)


def _loss_head(h, target, g, name):
    n, d = h.shape
    tm = _tile(n, 512)
    nb = n // tm

    def body(h_ref, t_ref, g_ref, dh_ref, loss_ref, dg_ref, lacc, gacc):
        i = pl.program_id(0)

        @pl.when(i == 0)
        def _():
            lacc[...] = jnp.zeros_like(lacc)
            gacc[...] = jnp.zeros_like(gacc)

        gain = g_ref[...]
        yv, hhat, r = _rms_fwd(h_ref[...], gain)
        e = yv - t_ref[...]
        lacc[...] += _row_fold(e * e)
        dhv, dgr = _rms_bwd(e * (1.0 / d), hhat, r, gain)
        dh_ref[...] = dhv
        gacc[...] += _row_fold(dgr)

        @pl.when(i == nb - 1)
        def _():
            loss_ref[...] = jnp.full((1, LANES), jnp.sum(lacc[...]) * (0.5 / d), F32)
            dg_ref[...] = jnp.sum(gacc[...], axis=0, keepdims=True)

    return _rows_call(body, name, tm, [h, target], [g], [(d, F32)], [((1, LANES), F32), ((1, d), F32)],
                      [pltpu.VMEM((SUBLANES, d), F32), pltpu.VMEM((SUBLANES, d), F32)])


def _matmul_tn(a, b, name, out_dtype=BF16):
    l, m = a.shape
    n = b.shape[1]
    tl = _tile(l, 2048)
    tmm = _tile(m, 1024)
    tn = _tile(n, 1024)
    nl = l // tl

    def body(a_ref, b_ref, o_ref, acc):
        k = pl.program_id(2)

        @pl.when(k == 0)
        def _():
            acc[...] = jnp.zeros_like(acc)

        acc[...] += lax.dot_general(a_ref[...].astype(BF16), b_ref[...].astype(BF16), TN_DIMS,
                                    preferred_element_type=F32)

        @pl.when(k == nl - 1)
        def _():
            o_ref[...] = acc[...].astype(out_dtype)

    return pl.pallas_call(
        body, name=name, grid=(m // tmm, n // tn, nl),
        in_specs=[pl.BlockSpec((tl, tmm), lambda i, j, k: (k, i)), pl.BlockSpec((tl, tn), lambda i, j, k: (k, j))],
        out_specs=pl.BlockSpec((tmm, tn), lambda i, j, k: (i, j)),
        out_shape=jax.ShapeDtypeStruct((m, n), out_dtype),
        scratch_shapes=[pltpu.VMEM((tmm, tn), F32)],
        compiler_params=_cp("parallel", "parallel", "arbitrary"),
    )(a, b)


def _scan_fwd(u, bblk, cblk, tabs, dskip, name):
    l, d = u.shape
    nj, gb, n2 = bblk.shape
    n = n2 // 2
    tm = _tile(l, 512)
    nb = l // tm
    nsub = 4 if tm % (4 * SUBLANES) == 0 else 1
    sub = tm // nsub

    def body(u_ref, b_ref, c_ref, t_ref, d_ref, st_ref, y_ref, carry):
        i = pl.program_id(1)

        @pl.when(i == 0)
        def _():
            carry[...] = jnp.zeros_like(carry)

        def step(rb, c):
            cr, ci = c
            rows = pl.ds(rb * SUBLANES, SUBLANES)
            xr = st_ref[rows, 0:n]
            xi = st_ref[rows, n:n2]
            for lvl, sh in enumerate((1, 2, 4)):
                ar = t_ref[0, 2 * lvl]
                ai = t_ref[0, 2 * lvl + 1]
                sr = pltpu.roll(xr, sh, 0)
                si = pltpu.roll(xi, sh, 0)
                xr, xi = xr + ar * sr - ai * si, xi + ar * si + ai * sr
            lr = t_ref[0, 6]
            li = t_ref[0, 7]
            xr, xi = xr + lr * cr - li * ci, xi + lr * ci + li * cr
            st_ref[rows, 0:n] = xr
            st_ref[rows, n:n2] = xi
            return (jnp.broadcast_to(xr[SUBLANES - 1:SUBLANES, :], (SUBLANES, n)),
                    jnp.broadcast_to(xi[SUBLANES - 1:SUBLANES, :], (SUBLANES, n)))

        def project_in(sb):
            rows = slice(sb * sub, (sb + 1) * sub)
            st_ref[rows, :] = jnp.dot(u_ref[rows, :].astype(BF16), b_ref[0], preferred_element_type=F32)

        def project_out(sb):
            rows = slice(sb * sub, (sb + 1) * sub)
            y_ref[rows, :] = (jnp.dot(st_ref[rows, :].astype(BF16), c_ref[0], preferred_element_type=F32)
                              + d_ref[...] * u_ref[rows, :])

        c = (carry[:, 0:n], carry[:, n:n2])
        project_in(0)
        for sb in range(nsub):
            if sb + 1 < nsub:
                project_in(sb + 1)
            for rb in range(sb * sub // SUBLANES, (sb + 1) * sub // SUBLANES):
                c = step(rb, c)
            project_out(sb)
        carry[:, 0:n] = c[0]
        carry[:, n:n2] = c[1]

    return pl.pallas_call(
        body, name=name, grid=(nj, nb),
        in_specs=[pl.BlockSpec((tm, gb), lambda j, i: (i, j)),
                  pl.BlockSpec((1, gb, n2), lambda j, i: (j, 0, 0)),
                  pl.BlockSpec((1, n2, gb), lambda j, i: (j, 0, 0)),
                  pl.BlockSpec((1, 8, SUBLANES, n), lambda j, i: (j, 0, 0, 0)),
                  pl.BlockSpec((1, gb), lambda j, i: (0, j))],
        out_specs=[pl.BlockSpec((tm, n2), lambda j, i: (i, j)), pl.BlockSpec((tm, gb), lambda j, i: (i, j))],
        out_shape=[jax.ShapeDtypeStruct((l, nj * n2), F32), jax.ShapeDtypeStruct((l, d), F32)],
        scratch_shapes=[pltpu.VMEM((SUBLANES, n2), F32)],
        compiler_params=_cp("parallel", "arbitrary"),
    )(u, bblk, cblk, tabs, dskip)


def _scan_bwd(dy, u, states, bblk_t, cblk_t, tabs, dskip, name):
    l, d = u.shape
    nj, n2, gb = bblk_t.shape
    n = n2 // 2
    tm = _tile(l, 512)
    nb = l // tm
    nsub = 4 if tm % (4 * SUBLANES) == 0 else 1
    sub = tm // nsub

    def body(dy_ref, u_ref, st_ref, prev_ref, bt_ref, ct_ref, t_ref, d_ref,
             du_ref, glam_ref, gd_ref, gb_ref, gc_ref, gx, carry):
        i = pl.program_id(1)
        ib = nb - 1 - i

        @pl.when(i == 0)
        def _():
            carry[...] = jnp.zeros_like(carry)
            glam_ref[...] = jnp.zeros_like(glam_ref)
            gd_ref[...] = jnp.zeros_like(gd_ref)
            gb_ref[...] = jnp.zeros_like(gb_ref)
            gc_ref[...] = jnp.zeros_like(gc_ref)

        last_row = lax.broadcasted_iota(jnp.int32, (SUBLANES, n), 0) == SUBLANES - 1

        def block(rows, xp_r, xp_i, c):
            cr, ci = c
            gr = gx[rows, 0:n]
            gi = gx[rows, n:n2]
            for lvl, sh in enumerate((1, 2, 4)):
                ar = t_ref[0, 2 * lvl]
                ai = t_ref[0, 2 * lvl + 1]
                sr = pltpu.roll(gr, SUBLANES - sh, 0)
                si = pltpu.roll(gi, SUBLANES - sh, 0)
                gr, gi = gr + ar * sr - ai * si, gi + ar * si + ai * sr
            lr = t_ref[0, 6]
            li = t_ref[0, 7]
            gr, gi = gr + lr * cr - li * ci, gi + lr * ci + li * cr
            gx[rows, 0:n] = gr
            gx[rows, n:n2] = gi
            xs_r = pltpu.roll(jnp.where(last_row, xp_r, st_ref[rows, 0:n]), 1, 0)
            xs_i = pltpu.roll(jnp.where(last_row, xp_i, st_ref[rows, n:n2]), 1, 0)
            glam_ref[0, :, 0:n] += gr * xs_r + gi * xs_i
            glam_ref[0, :, n:n2] += gi * xs_r - gr * xs_i
            return (jnp.broadcast_to(gr[0:1, :], (SUBLANES, n)), jnp.broadcast_to(gi[0:1, :], (SUBLANES, n)))

        def project_in(sb):
            rows = slice(sb * sub, (sb + 1) * sub)
            gx[rows, :] = jnp.dot(dy_ref[rows, :].astype(BF16), ct_ref[0], preferred_element_type=F32)

        def project_out(sb):
            rows = slice(sb * sub, (sb + 1) * sub)
            dyv = dy_ref[rows, :]
            uv = u_ref[rows, :]
            gxb = gx[rows, :].astype(BF16)
            du_ref[rows, :] = jnp.dot(gxb, bt_ref[0], preferred_element_type=F32) + d_ref[...] * dyv
            gd_ref[0] += _row_fold(dyv * uv)
            gb_ref[0] += lax.dot_general(uv.astype(BF16), gxb, TN_DIMS, preferred_element_type=F32)
            gc_ref[0] += lax.dot_general(st_ref[rows, :].astype(BF16), dyv.astype(BF16), TN_DIMS,
                                         preferred_element_type=F32)

        live = (ib > 0).astype(F32)
        c = (carry[:, 0:n], carry[:, n:n2])
        project_in(nsub - 1)
        for sb in reversed(range(nsub)):
            if sb > 0:
                project_in(sb - 1)
            for rb in reversed(range(sb * sub // SUBLANES, (sb + 1) * sub // SUBLANES)):
                rows = pl.ds(rb * SUBLANES, SUBLANES)
                if rb > 0:
                    before = pl.ds((rb - 1) * SUBLANES, SUBLANES)
                    c = block(rows, st_ref[before, 0:n], st_ref[before, n:n2], c)
                else:
                    c = block(rows, prev_ref[:, 0:n] * live, prev_ref[:, n:n2] * live, c)
            project_out(sb)
        carry[:, 0:n] = c[0]
        carry[:, n:n2] = c[1]

    rpb = tm // SUBLANES
    return pl.pallas_call(
        body, name=name, grid=(nj, nb),
        in_specs=[pl.BlockSpec((tm, gb), lambda j, i: (nb - 1 - i, j)),
                  pl.BlockSpec((tm, gb), lambda j, i: (nb - 1 - i, j)),
                  pl.BlockSpec((tm, n2), lambda j, i: (nb - 1 - i, j)),
                  pl.BlockSpec((SUBLANES, n2), lambda j, i: (jnp.maximum((nb - 1 - i) * rpb - 1, 0), j)),
                  pl.BlockSpec((1, n2, gb), lambda j, i: (j, 0, 0)),
                  pl.BlockSpec((1, gb, n2), lambda j, i: (j, 0, 0)),
                  pl.BlockSpec((1, 8, SUBLANES, n), lambda j, i: (j, 0, 0, 0)),
                  pl.BlockSpec((1, gb), lambda j, i: (0, j))],
        out_specs=[pl.BlockSpec((tm, gb), lambda j, i: (nb - 1 - i, j)),
                   pl.BlockSpec((1, SUBLANES, n2), lambda j, i: (j, 0, 0)),
                   pl.BlockSpec((1, SUBLANES, gb), lambda j, i: (j, 0, 0)),
                   pl.BlockSpec((1, gb, n2), lambda j, i: (j, 0, 0)),
                   pl.BlockSpec((1, n2, gb), lambda j, i: (j, 0, 0))],
        out_shape=[jax.ShapeDtypeStruct((l, d), F32),
                   jax.ShapeDtypeStruct((nj, SUBLANES, n2), F32),
                   jax.ShapeDtypeStruct((nj, SUBLANES, gb), F32),
                   jax.ShapeDtypeStruct((nj, gb, n2), F32),
                   jax.ShapeDtypeStruct((nj, n2, gb), F32)],
        scratch_shapes=[pltpu.VMEM((tm, n2), F32), pltpu.VMEM((SUBLANES, n2), F32)],
        compiler_params=_cp("parallel", "arbitrary"),
    )(dy, u, states, states, bblk_t, cblk_t, tabs, dskip)


HEAD_SLOT = 128
PAIR_SLOT = 2 * HEAD_SLOT
ATTN_TILE = 1024
LANE_ROWSUM_P = HEAD_DIM
LANE_COLSUM_DS = HEAD_DIM
LANE_ROWSUM_DS = HEAD_DIM + 1


def _slot_cols(w):
    r, c = w.shape
    nh = c // HEAD_DIM
    return jnp.pad(w.reshape(r, nh, HEAD_DIM), ((0, 0), (0, 0), (0, HEAD_SLOT - HEAD_DIM))).reshape(r, nh * HEAD_SLOT)


def _unslot_cols(w):
    r, c = w.shape
    nh = c // HEAD_SLOT
    return w.reshape(r, nh, HEAD_SLOT)[:, :, :HEAD_DIM].reshape(r, nh * HEAD_DIM)


def _slot_ones(nh, lane):
    return jnp.tile((jnp.arange(HEAD_SLOT) == lane).astype(F32), nh).reshape(1, nh * HEAD_SLOT)


def _causal_tiles(n, by_query):
    if by_query:
        tiles = [(i, j) for i in range(n) for j in range(i + 1)]
    else:
        tiles = [(i, j) for j in range(n) for i in range(j, n)]
    return (jnp.asarray([t[0] for t in tiles], jnp.int32), jnp.asarray([t[1] for t in tiles], jnp.int32))


def _flash_fwd(qx, kx, vx, f2_rows, name):
    l = qx.shape[0]
    npair = qx.shape[1] // PAIR_SLOT
    d = npair * HEAD_PAIR
    tq = _tile(l, ATTN_TILE)
    tk = tq
    i_of, j_of = _causal_tiles(l // tq, by_query=True)

    def body(i_ref, j_ref, q_ref, k_ref, v_ref, f_ref, o_ref, lse_ref, m_sc, acc_sc):
        t = pl.program_id(1)
        i = i_ref[t]
        j = j_ref[t]

        @pl.when(j == 0)
        def _():
            m_sc[...] = jnp.full_like(m_sc, NEG)
            acc_sc[...] = jnp.zeros_like(acc_sc)

        def tile(on_diagonal):
            for hh in range(2):
                hs = slice(hh * HEAD_SLOT, (hh + 1) * HEAD_SLOT)
                s = lax.dot_general(q_ref[:, hs], k_ref[:, hs], NT_DIMS, preferred_element_type=F32)
                s = s - f_ref[0, hh:hh + 1, :]
                if on_diagonal:
                    keep = (lax.broadcasted_iota(jnp.int32, (tq, tk), 0)
                            >= lax.broadcasted_iota(jnp.int32, (tq, tk), 1))
                    s = jnp.where(keep, s, NEG)
                m_prev = m_sc[hh]
                m_new = jnp.maximum(m_prev, jnp.max(s, axis=-1, keepdims=True))
                alpha = jnp.exp2(m_prev - m_new)
                p = jnp.exp2(s - jnp.concatenate([m_new] * (tk // LANES), axis=1)).astype(BF16)
                acc_sc[hh] = alpha * acc_sc[hh] + jnp.dot(p, v_ref[:, hs], preferred_element_type=F32)
                m_sc[hh] = m_new

        @pl.when(j < i)
        def _():
            tile(False)

        @pl.when(j == i)
        def _():
            tile(True)
            lane0 = (lax.broadcasted_iota(jnp.int32, (SUBLANES, LANES), 1) == 0).astype(BF16)
            outs, lses = [], []
            for hh in range(2):
                acc = acc_sc[hh]
                lsum = acc[:, LANE_ROWSUM_P:LANE_ROWSUM_P + 1]
                outs.append(acc[:, :HEAD_DIM] / lsum)
                lse_cols = m_sc[hh] + jnp.log2(lsum)
                lses.append(_exact_dot_nt(lane0, lse_cols)[0:1, :])
            o_ref[...] = jnp.concatenate(outs, axis=1).astype(BF16)
            lse_ref[0] = jnp.concatenate(lses, axis=0)

    q_map = lambda h, t, i_ref, j_ref: (i_ref[t], h)
    kv_map = lambda h, t, i_ref, j_ref: (j_ref[t], h)
    return pl.pallas_call(
        body, name=name,
        grid_spec=pltpu.PrefetchScalarGridSpec(
            num_scalar_prefetch=2, grid=(npair, i_of.shape[0]),
            in_specs=[pl.BlockSpec((tq, PAIR_SLOT), q_map), pl.BlockSpec((tk, PAIR_SLOT), kv_map),
                      pl.BlockSpec((tk, PAIR_SLOT), kv_map),
                      pl.BlockSpec((1, 2, tk), lambda h, t, i_ref, j_ref: (h, 0, j_ref[t]))],
            out_specs=[pl.BlockSpec((tq, HEAD_PAIR), q_map),
                       pl.BlockSpec((1, 2, tq), lambda h, t, i_ref, j_ref: (h, 0, i_ref[t]))],
            scratch_shapes=[pltpu.VMEM((2, tq, LANES), F32), pltpu.VMEM((2, tq, HEAD_SLOT), F32)]),
        out_shape=[jax.ShapeDtypeStruct((l, d), BF16), jax.ShapeDtypeStruct((npair, 2, l), F32)],
        compiler_params=_cp("parallel", "arbitrary"),
    )(i_of, j_of, qx, kx, vx, f2_rows)


def _flash_bwd(qx, kx, vx, f2_rep, do, lse_rows, delta_rows, name):
    l = qx.shape[0]
    npair = qx.shape[1] // PAIR_SLOT
    d = npair * HEAD_PAIR
    tq = _tile(l, ATTN_TILE)
    tk = tq
    i_of, j_of = _causal_tiles(l // tq, by_query=False)

    def body(i_ref, j_ref, q_ref, k_ref, v_ref, f_ref, do_ref, lse_ref, dl_ref, dq_ref, dk_ref, dv_ref):
        t = pl.program_id(1)
        i = i_ref[t]
        j = j_ref[t]

        @pl.when(t == 0)
        def _():
            dq_ref[...] = jnp.zeros_like(dq_ref)

        @pl.when(i == j)
        def _():
            dk_ref[...] = jnp.zeros_like(dk_ref)
            dv_ref[...] = jnp.zeros_like(dv_ref)

        def tile(on_diagonal):
            dqs, dks, dvs = [], [], []
            for hh in range(2):
                hs = slice(hh * HEAD_SLOT, (hh + 1) * HEAD_SLOT)
                qh, kh = q_ref[:, hs], k_ref[:, hs]
                vh = v_ref[:, hh * HEAD_SLOT:hh * HEAD_SLOT + HEAD_DIM]
                doh = do_ref[:, hh * HEAD_DIM:(hh + 1) * HEAD_DIM]
                st = lax.dot_general(kh, qh, NT_DIMS, preferred_element_type=F32)
                st = st - jnp.concatenate([f_ref[:, hs]] * (tq // HEAD_SLOT), axis=1)
                pt = jnp.exp2(st - lse_ref[0, hh:hh + 1, :])
                if on_diagonal:
                    keep = (lax.broadcasted_iota(jnp.int32, (tk, tq), 1)
                            >= lax.broadcasted_iota(jnp.int32, (tk, tq), 0))
                    pt = jnp.where(keep, pt, 0.0)
                dpt = lax.dot_general(vh, doh, NT_DIMS, preferred_element_type=F32)
                dsb = (pt * (dpt - dl_ref[0, hh:hh + 1, :])).astype(BF16)
                dvs.append(jnp.dot(pt.astype(BF16), doh, preferred_element_type=F32))
                dks.append(jnp.dot(dsb, qh, preferred_element_type=F32))
                dqs.append(lax.dot_general(dsb, kh, TN_DIMS, preferred_element_type=F32))
            dv_ref[...] += jnp.concatenate(dvs, axis=1)
            dk_ref[...] += jnp.concatenate(dks, axis=1)
            dq_ref[pl.ds(pl.multiple_of(i * tq, tq), tq), :] += jnp.concatenate(dqs, axis=1)

        @pl.when(i > j)
        def _():
            tile(False)

        @pl.when(i == j)
        def _():
            tile(True)

    qmap = lambda h, t, i_ref, j_ref: (i_ref[t], h)
    kmap = lambda h, t, i_ref, j_ref: (j_ref[t], h)
    row_map = lambda h, t, i_ref, j_ref: (h, 0, i_ref[t])
    return pl.pallas_call(
        body, name=name,
        grid_spec=pltpu.PrefetchScalarGridSpec(
            num_scalar_prefetch=2, grid=(npair, i_of.shape[0]),
            in_specs=[pl.BlockSpec((tq, PAIR_SLOT), qmap), pl.BlockSpec((tk, PAIR_SLOT), kmap),
                      pl.BlockSpec((tk, PAIR_SLOT), kmap), pl.BlockSpec((tk, PAIR_SLOT), kmap),
                      pl.BlockSpec((tq, HEAD_PAIR), qmap),
                      pl.BlockSpec((1, 2, tq), row_map), pl.BlockSpec((1, 2, tq), row_map)],
            out_specs=[pl.BlockSpec((l, PAIR_SLOT), lambda h, t, i_ref, j_ref: (0, h)),
                       pl.BlockSpec((tk, PAIR_SLOT), kmap), pl.BlockSpec((tk, HEAD_PAIR), kmap)]),
        out_shape=[jax.ShapeDtypeStruct((l, npair * PAIR_SLOT), F32), jax.ShapeDtypeStruct((l, npair * PAIR_SLOT), F32),
                   jax.ShapeDtypeStruct((l, d), F32)],
        compiler_params=_cp("parallel", "arbitrary"),
    )(i_of, j_of, qx, kx, vx, f2_rep, do, lse_rows, delta_rows)


def _my_place():
    return lax.axis_index("x"), lax.axis_index("y"), lax.axis_index("c")


def _chip_exchange(srcs, out_meta, plan, name):
    n_src, n_out, n_plan = len(srcs), len(out_meta), len(plan)

    def body(*refs):
        src_refs = refs[:n_src]
        out_refs = refs[n_src:n_src + n_out]
        send_sems, recv_sems, local_sems = refs[n_src + n_out:]
        x, y, c = _my_place()
        me = 2 * x + y
        copies = []
        for n, (si, oi, src_view, dst_view) in enumerate(plan):
            local = pltpu.make_async_copy(src_view(src_refs[si], me), dst_view(out_refs[oi], me), local_sems.at[n])
            local.start()
            copies.append(local)
            for k in (1, 2, 3):
                peer = me ^ k
                rc = pltpu.make_async_remote_copy(
                    src_ref=src_view(src_refs[si], peer), dst_ref=dst_view(out_refs[oi], me),
                    send_sem=send_sems.at[n, k - 1], recv_sem=recv_sems.at[n, k - 1],
                    device_id=(peer >> 1, peer & 1, c), device_id_type=MESH)
                rc.start()
                copies.append(rc)
        for cp in copies:
            cp.wait()

    any_spec = pl.BlockSpec(memory_space=pl.ANY)
    return pl.pallas_call(
        body, name=name,
        in_specs=[any_spec] * n_src, out_specs=[any_spec] * n_out,
        out_shape=[jax.ShapeDtypeStruct(shape, dt) for (shape, dt) in out_meta],
        scratch_shapes=[pltpu.SemaphoreType.DMA((n_plan, 3)), pltpu.SemaphoreType.DMA((n_plan, 3)),
                        pltpu.SemaphoreType.DMA((n_plan,))],
    )(*srcs)


def _plan_copies(src_refs, land_refs, plan, send_sems, recv_sems):
    x, y, c = _my_place()
    me = 2 * x + y
    copies = []
    for n, (si, oi, src_view, dst_view) in enumerate(plan):
        for k in (1, 2, 3):
            peer = me ^ k
            copies.append(pltpu.make_async_remote_copy(
                src_ref=src_view(src_refs[si], peer), dst_ref=dst_view(land_refs[oi], me),
                send_sem=send_sems.at[3 * n + k - 1], recv_sem=recv_sems.at[3 * n + k - 1],
                device_id=(peer >> 1, peer & 1, c), device_id_type=MESH))
    return copies


def _hbm(a):
    return pltpu.HBM(a.shape, a.dtype)


def _exchange_start(srcs, lands, plan, name):
    n_src, n_land = len(srcs), len(lands)
    n_buf = n_src + n_land

    def body(*refs):
        send_sems, recv_sems = refs[n_buf], refs[n_buf + 1]
        token = refs[-1]
        for cp in _plan_copies(refs[:n_src], refs[n_src:n_buf], plan, send_sems, recv_sems):
            cp.start()
        token[...] = jnp.zeros_like(token)

    bufs = [pltpu.with_memory_space_constraint(a, pltpu.HBM) for a in (*srcs, *lands)]
    hbm = pl.BlockSpec(memory_space=pltpu.HBM)
    sem = pl.BlockSpec(memory_space=pltpu.SEMAPHORE)
    res = pl.pallas_call(
        body, name=name,
        out_shape=(pltpu.SemaphoreType.DMA((3 * len(plan),)), pltpu.SemaphoreType.DMA((3 * len(plan),)),
                   *[_hbm(a) for a in bufs], jax.ShapeDtypeStruct((SUBLANES, LANES), F32)),
        in_specs=[hbm] * n_buf, out_specs=(sem, sem, *[hbm] * n_buf, pl.BlockSpec(memory_space=pltpu.VMEM)),
        input_output_aliases={n: 2 + n for n in range(n_buf)},
        compiler_params=pltpu.CompilerParams(has_side_effects=pltpu.SideEffectType.DATAFLOW_SIDE_EFFECTING),
    )(*bufs)
    return (res[0], res[1]), list(res[2:2 + n_src]), list(res[2 + n_src:2 + n_buf]), res[-1]


def _exchange_wait(sems, srcs, lands, plan, after, name):
    n_src, n_land = len(srcs), len(lands)
    n_buf = n_src + n_land

    def body(*refs):
        send_sems, recv_sems = refs[n_buf], refs[n_buf + 1]
        for cp in _plan_copies(refs[:n_src], refs[n_src:n_buf], plan, send_sems, recv_sems):
            cp.wait_send()
            cp.wait_recv()

    hbm = pl.BlockSpec(memory_space=pltpu.HBM)
    sem = pl.BlockSpec(memory_space=pltpu.SEMAPHORE)
    res = pl.pallas_call(
        body, name=name, out_shape=tuple(_hbm(a) for a in (*srcs, *lands)),
        in_specs=[hbm] * n_buf + [sem, sem, pl.BlockSpec(memory_space=pl.ANY)], out_specs=tuple([hbm] * n_buf),
        input_output_aliases={n: n for n in range(n_buf)},
        compiler_params=pltpu.CompilerParams(has_side_effects=pltpu.SideEffectType.DATAFLOW_SIDE_EFFECTING),
    )(*srcs, *lands, sems[0], sems[1], after)
    return list(res[n_src:])


def _put_blocks(land, part, chip, block, in_map, out_map, name):
    grid = tuple(p // b for p, b in zip(part.shape, block))

    def body(chip_ref, part_ref, land_ref, out_ref):
        out_ref[...] = part_ref[...]

    return pl.pallas_call(
        body, name=name,
        grid_spec=pltpu.PrefetchScalarGridSpec(
            num_scalar_prefetch=1, grid=grid,
            in_specs=[pl.BlockSpec(block, in_map), pl.BlockSpec(memory_space=pl.ANY)],
            out_specs=pl.BlockSpec(block, out_map)),
        out_shape=jax.ShapeDtypeStruct(land.shape, land.dtype),
        input_output_aliases={2: 0},
        compiler_params=_cp(*["parallel"] * len(grid)),
    )(chip, part, land)


def _core_exchange(arrays, name):
    n_items = len(arrays)

    def body(*refs):
        srcs = refs[:n_items]
        outs = refs[n_items:2 * n_items]
        send_sems, recv_sems = refs[2 * n_items:]
        x, y, c = _my_place()
        copies = []
        for n in range(n_items):
            rc = pltpu.make_async_remote_copy(
                src_ref=srcs[n], dst_ref=outs[n], send_sem=send_sems.at[n], recv_sem=recv_sems.at[n],
                device_id=(x, y, 1 - c), device_id_type=MESH)
            rc.start()
            copies.append(rc)
        for cp in copies:
            cp.wait()

    any_spec = pl.BlockSpec(memory_space=pl.ANY)
    return pl.pallas_call(
        body, name=name,
        in_specs=[any_spec] * n_items, out_specs=[any_spec] * n_items,
        out_shape=[jax.ShapeDtypeStruct(a.shape, a.dtype) for a in arrays],
        scratch_shapes=[pltpu.SemaphoreType.DMA((n_items,)), pltpu.SemaphoreType.DMA((n_items,))],
    )(*arrays)


def _sum_chips(parts, name):
    _, rows, cols = parts.shape
    tm = _tile(rows, 512)

    def body(p_ref, o_ref):
        acc = p_ref[0].astype(F32)
        for s in range(1, N_CHIPS):
            acc = acc + p_ref[s].astype(F32)
        o_ref[...] = acc

    return pl.pallas_call(
        body, name=name, grid=(rows // tm,),
        in_specs=[pl.BlockSpec((N_CHIPS, tm, cols), lambda i: (0, i, 0))],
        out_specs=pl.BlockSpec((tm, cols), lambda i: (i, 0)),
        out_shape=jax.ShapeDtypeStruct((rows, cols), F32),
        compiler_params=_cp("parallel"),
    )(parts)


def _adamw(ga, gb, w, m, v, name):
    rows, cols = w.shape
    tm = _tile(rows, 512)
    c1 = 1.0 - ADAM_B1 ** ADAM_STEP
    c2 = 1.0 - ADAM_B2 ** ADAM_STEP

    def body(ga_ref, gb_ref, w_ref, m_ref, v_ref, g_ref, d_ref, nm_ref, nv_ref):
        g = ga_ref[...] + gb_ref[...]
        nm = ADAM_B1 * m_ref[...] + (1.0 - ADAM_B1) * g
        nv = ADAM_B2 * v_ref[...] + (1.0 - ADAM_B2) * (g * g)
        g_ref[...] = g
        nm_ref[...] = nm
        nv_ref[...] = nv
        d_ref[...] = -ADAM_LR * ((nm / c1) / (jnp.sqrt(nv / c2) + ADAM_EPS) + ADAM_WD * w_ref[...])

    spec = pl.BlockSpec((tm, cols), lambda i: (i, 0))
    return pl.pallas_call(
        body, name=name, grid=(rows // tm,), in_specs=[spec] * 5, out_specs=[spec] * 4,
        out_shape=[jax.ShapeDtypeStruct((rows, cols), F32)] * 4, compiler_params=_cp("parallel"),
    )(ga, gb, w, m, v)


def _ssm_discretise(log_dt, a_re, a_im, b_re, b_im):
    dt = jnp.exp(log_dt)[:, None]
    mag = jnp.exp(a_re * dt)
    lbr = mag * jnp.cos(a_im * dt)
    lbi = mag * jnp.sin(a_im * dt)
    den = a_re * a_re + a_im * a_im
    nr, ni = lbr - 1.0, lbi
    qr = (nr * a_re + ni * a_im) / den
    qi = (ni * a_re - nr * a_im) / den
    bbr = qr[..., None] * b_re - qi[..., None] * b_im
    bbi = qr[..., None] * b_im + qi[..., None] * b_re
    return lbr, lbi, bbr, bbi


def _cmul(ar, ai, br, bi):
    return ar * br - ai * bi, ar * bi + ai * br


def _scan_tables(lr, li, nj, reverse):
    lr = lr.reshape(nj, 1, -1)
    li = li.reshape(nj, 1, -1)
    if reverse:
        li = -li
    pows = [(lr, li)]
    for _ in range(7):
        pows.append(_cmul(*pows[-1], lr, li))
    r = jnp.arange(SUBLANES).reshape(1, SUBLANES, 1)
    if reverse:
        r = SUBLANES - 1 - r
    out = []
    for k in (1, 2, 4):
        pr, pi = pows[k - 1]
        keep = (r >= k).astype(F32)
        out += [pr * keep, pi * keep]
    shape = (nj, SUBLANES, lr.shape[-1])
    cr = jnp.zeros(shape, F32)
    ci = jnp.zeros(shape, F32)
    for e in range(SUBLANES):
        sel = (r == e).astype(F32)
        cr = cr + sel * pows[e][0]
        ci = ci + sel * pows[e][1]
    out += [cr, ci]
    return jnp.stack(out, axis=1)


def _group_eye(gl):
    return jnp.eye(gl, dtype=F32)


def _block_diag_in(bbr, bbi, nj):
    g, p, c = bbr.shape
    gl = g // nj
    eye = _group_eye(gl)[None, :, None, :, None]

    def one(b):
        t = b.reshape(nj, gl, p, c).transpose(0, 1, 3, 2)[:, :, :, None, :]
        return (t * eye).reshape(nj, gl * c, gl * p)

    return jnp.concatenate([one(bbr), one(bbi)], axis=2)


def _block_diag_in_grad(gmat, nj, p, c):
    gl = gmat.shape[1] // c
    n = gl * p
    eye = _group_eye(gl)[None, :, None, :, None]

    def one(m):
        t = jnp.sum(m.reshape(nj, gl, c, gl, p) * eye, axis=3)
        return t.transpose(0, 1, 3, 2).reshape(nj * gl, p, c)

    return one(gmat[:, :, :n]), one(gmat[:, :, n:])


def _block_diag_out(c_re, c_im, nj):
    g, c, p = c_re.shape
    gl = g // nj
    eye = _group_eye(gl)[None, :, None, :, None]

    def one(m):
        t = m.reshape(nj, gl, c, p).transpose(0, 1, 3, 2)[:, :, :, None, :]
        return (t * eye).reshape(nj, gl * p, gl * c)

    return jnp.concatenate([one(c_re), -one(c_im)], axis=1)


def _block_diag_out_grad(gmat, nj, p, c):
    gl = gmat.shape[2] // c
    n = gl * p
    eye = _group_eye(gl)[None, :, None, :, None]

    def one(m):
        t = jnp.sum(m.reshape(nj, gl, p, gl, c) * eye, axis=3)
        return t.transpose(0, 1, 3, 2).reshape(nj * gl, c, p)

    return one(gmat[:, :n, :]), -one(gmat[:, n:, :])


def _pad_rows(flat, cols):
    per = SUBLANES * cols
    n = flat.shape[0]
    total = -(-n // per) * per
    return jnp.pad(flat, (0, total - n)).reshape(total // cols, cols)


def _pack_small(arrs, cols):
    packed = jnp.concatenate([_pad_rows(a.reshape(-1), cols) for a in arrs], axis=0)
    rows = packed.shape[0]
    return jnp.pad(packed, ((0, -rows % 128), (0, 0)))


def _unpack_small(packed, shapes, cols):
    out = []
    row = 0
    for s in shapes:
        n = math.prod(s)
        rows = -(-n // (SUBLANES * cols)) * SUBLANES
        out.append(packed[row:row + rows].reshape(-1)[:n].reshape(s))
        row += rows
    return out


def kernel(x, mix_norm, mlp_norm, mlp_w1, mlp_w2, ssm_log_dt, ssm_a_re, ssm_a_im, ssm_b_re, ssm_b_im, ssm_c_re, ssm_c_im, ssm_d, ssm_w_glu, kv_norm, w_kvf, b_f, attn_wq, attn_wo, final_norm, loss_target, m_mix_norm, m_mlp_norm, m_mlp_w1, m_mlp_w2, m_ssm_log_dt, m_ssm_a_re, m_ssm_a_im, m_ssm_b_re, m_ssm_b_im, m_ssm_c_re, m_ssm_c_im, m_ssm_d, m_ssm_w_glu, m_kv_norm, m_w_kvf, m_b_f, m_attn_wq, m_attn_wo, m_final_norm, v_mix_norm, v_mlp_norm, v_mlp_w1, v_mlp_w2, v_ssm_log_dt, v_ssm_a_re, v_ssm_a_im, v_ssm_b_re, v_ssm_b_im, v_ssm_c_re, v_ssm_c_im, v_ssm_d, v_ssm_w_glu, v_kv_norm, v_w_kvf, v_b_f, v_attn_wq, v_attn_wo, v_final_norm):
    seq, d = x.shape[1], x.shape[2]
    depth = mix_norm.shape[0]
    n_a = ssm_log_dt.shape[0]
    n_b = depth - n_a
    ff = mlp_w1.shape[2] * N_CHIPS
    n_heads = d // HEAD_DIM
    n_groups = d // SSM_GROUP
    p_state = ssm_a_re.shape[2]
    gb = min(d, 256)
    nj = d // gb
    kvf_cols = w_kvf.shape[1]
    ds4 = d // N_CHIPS
    chip = 2 * lax.axis_index("x") + lax.axis_index("y")

    def cols_of(width):
        return lambda ref, s: ref.at[:, :, pl.ds(pl.multiple_of(s * width, LANES), width)]

    def rows_of(height):
        return lambda ref, s: ref.at[:, pl.ds(pl.multiple_of(s * height, SUBLANES), height), :]

    whole = lambda ref, s: ref
    slot = lambda ref, s: ref.at[s]
    def cols2(width):
        return lambda ref, s: ref.at[:, pl.ds(pl.multiple_of(s * width, LANES), width)]

    def rows2(height):
        return lambda ref, s: ref.at[pl.ds(pl.multiple_of(s * height, SUBLANES), height), :]

    assert n_a >= 2
    (skip_parts,) = _chip_exchange([ssm_d], [((N_CHIPS, n_a, ds4), F32)], [(0, 0, whole, slot)], "gather_skip")
    skip_all = skip_parts.transpose(1, 0, 2).reshape(n_a, d)
    w1_s, w2_s, glu_s = mlp_w1.astype(BF16), mlp_w2.astype(BF16), ssm_w_glu.astype(BF16)
    src_a = [w1_s[0], w2_s[0], glu_s[0]]
    plan_a = [(0, 0, whole, cols2(d)), (1, 1, whole, rows2(d)), (2, 2, whole, cols2(2 * ds4))]
    land_a = [lax.empty((d, ff), BF16), lax.empty((ff, d), BF16), lax.empty((d, 2 * d), BF16)]
    src_b = [w1_s[1:], w2_s[1:], glu_s[1:], w_kvf.astype(BF16), attn_wq.astype(BF16), attn_wo.astype(BF16)]
    plan_b = [(0, 0, whole, cols_of(d)), (1, 1, whole, rows_of(d)), (2, 2, whole, cols_of(2 * ds4)),
              (3, 3, whole, slot), (4, 4, whole, rows_of(ds4)), (5, 5, whole, rows_of(ds4))]
    land_b = [lax.empty((depth - 1, d, ff), BF16), lax.empty((depth - 1, ff, d), BF16),
              lax.empty((n_a - 1, d, 2 * d), BF16), lax.empty((N_CHIPS, d, kvf_cols), BF16),
              lax.empty((n_b, d, d), BF16), lax.empty((n_b, d, d), BF16)]
    def idx(*at):
        return [jnp.asarray(v, jnp.int32) for v in at]

    put = lax.dynamic_update_slice
    chip1 = chip.reshape(1).astype(jnp.int32)
    tp = _tile(d, 512)
    per = d // tp
    land_a = [_put_blocks(land_a[0], src_a[0], chip1, (tp, d), lambda r, z, c: (r, 0), lambda r, z, c: (r, c[0]),
                          "gather_own_w1_a"),
              _put_blocks(land_a[1], src_a[1], chip1, (tp, d), lambda r, z, c: (r, 0),
                          lambda r, z, c: (c[0] * per + r, 0), "gather_own_w2_a"),
              put(land_a[2], src_a[2], idx(0, chip * 2 * ds4))]
    land_b = [_put_blocks(land_b[0], src_b[0], chip1, (1, tp, d), lambda l, r, z, c: (l, r, 0),
                          lambda l, r, z, c: (l, r, c[0]), "gather_own_w1_b"),
              _put_blocks(land_b[1], src_b[1], chip1, (1, tp, d), lambda l, r, z, c: (l, r, 0),
                          lambda l, r, z, c: (l, c[0] * per + r, 0), "gather_own_w2_b"),
              put(land_b[2], src_b[2], idx(0, 0, chip * 2 * ds4)), put(land_b[3], src_b[3][None], idx(chip, 0, 0)),
              put(land_b[4], src_b[4], idx(0, chip * ds4, 0)), put(land_b[5], src_b[5], idx(0, chip * ds4, 0))]
    sems_a, src_a, land_a, token_a = _exchange_start(src_a, land_a, plan_a, "gather_start_a")
    sems_b, src_b, land_b, token_b = _exchange_start(src_b, land_b, plan_b, "gather_start_b")
    started = token_a[0:1, 0:1] + token_b[0:1, 0:1]

    def layer_w1(i):
        return w1_0 if i == 0 else w1_rest[i - 1]

    def layer_w2(i):
        return w2_0 if i == 0 else w2_rest[i - 1]

    def layer_glu(i):
        return glu_0 if i == 0 else glu_rest[i - 1]

    h = x[0]
    target = loss_target[0]

    saved = []
    for i in range(n_a):
        lbr, lbi, bbr, bbi = _ssm_discretise(ssm_log_dt[i], ssm_a_re[i], ssm_a_im[i], ssm_b_re[i], ssm_b_im[i])
        bblk = _block_diag_in(bbr, bbi, nj)
        cblk = _block_diag_out(ssm_c_re[i], ssm_c_im[i], nj)
        rec = dict(h0=h, lam=(lbr, lbi), bblk=bblk, cblk=cblk)
        gain = mix_norm[i:i + 1] + started if i == 0 else mix_norm[i:i + 1]
        u = _norm_fwd(h, gain, f"s5_norm_{i}")
        rec["u"] = u
        dskip = rec["dskip"] = skip_all[i:i + 1]
        states, y = _scan_fwd(u, bblk.astype(BF16), cblk.astype(BF16), _scan_tables(lbr, lbi, nj, False), dskip,
                              f"s5_scan_{i}")
        rec["states"], rec["y"] = states, y
        if i == 0:
            w1_0, w2_0, glu_0 = _exchange_wait(sems_a, src_a, land_a, plan_a, y, "gather_wait_a")
        if i == 1:
            w1_rest, w2_rest, glu_rest, kvf_parts, wq_all, wo_all = _exchange_wait(
                sems_b, src_b, land_b, plan_b, y, "gather_wait_b")
        h, rec["zw"] = _s5_post_fwd(h, y, layer_glu(i), f"s5_glu_{i}")
        rec["h1"] = h
        h, rec["ap"] = _mlp_fwd(h, mlp_norm[i:i + 1], layer_w1(i), layer_w2(i), f"mlp_{i}")
        saved.append(rec)
    h_kv = h
    kvf_all = jnp.concatenate([kvf_parts[s] for s in range(N_CHIPS)], axis=1)
    wk = kvf_all[:, :d]
    wv = kvf_all[:, d:2 * d]
    wf = jnp.pad(kvf_all[:, 2 * d:], ((0, 0), (0, LANES - n_heads)))
    bf_row = jnp.pad(b_f, (0, LANES - n_heads)).reshape(1, LANES)
    wk_x = _slot_cols(wk)
    spread = (jnp.arange(LANES)[:, None] == jnp.arange(n_heads * HEAD_SLOT)[None, :] // HEAD_SLOT).astype(BF16)
    kx, vx, flog, cum, f2_rep = _kvf_fwd(h, kv_norm.reshape(1, d), wk_x, _slot_cols(wv), wf, bf_row,
                                         _slot_ones(n_heads, LANE_ROWSUM_DS), _slot_ones(n_heads, LANE_ROWSUM_P),
                                         spread, "kvf")
    f2_rows = (cum[:, :n_heads] * LOG2E).T.reshape(n_heads // 2, 2, seq)
    wq_x = [_slot_cols(wq_all[jb]) for jb in range(n_b)]
    for jb in range(n_b):
        i = n_a + jb
        rec = dict(h0=h)
        qx = _q_fwd(h, mix_norm[i:i + 1], wq_x[jb], _slot_ones(n_heads, LANE_COLSUM_DS), f"attn_q_{jb}")
        o, lse = _flash_fwd(qx, kx, vx, f2_rows, f"attn_core_{jb}")
        rec["qx"], rec["o"], rec["lse"] = qx, o, lse
        h = _o_fwd(h, o, wo_all[jb], f"attn_out_{jb}")
        rec["h1"] = h
        h, rec["ap"] = _mlp_fwd(h, mlp_norm[i:i + 1], layer_w1(i), layer_w2(i), f"mlp_{i}")
        saved.append(rec)
    dh, loss_row, g_final = _loss_head(h, target, final_norm.reshape(1, d), "loss_head")
    loss = lax.psum(loss_row[0, 0], ("x", "y", "c"))

    head_sel = (jnp.arange(n_heads)[:, None] == jnp.arange(d)[None, :] // HEAD_DIM).astype(BF16)
    slot_lane = jnp.arange(n_heads * HEAD_SLOT)[:, None]
    in_lane = jnp.arange(LANES)[None, :]
    sel_q = (slot_lane == in_lane * HEAD_SLOT + LANE_ROWSUM_DS).astype(BF16)
    sel_k = (slot_lane == in_lane * HEAD_SLOT + LANE_COLSUM_DS).astype(BF16)
    g_mix = [None] * depth
    g_mlp = [None] * depth
    g_w1 = [None] * depth
    g_w2 = [None] * depth
    g_wq = [None] * n_b
    g_wo = [None] * n_b
    g_glu = [None] * n_a
    g_ssm = [None] * n_a
    dk_parts, dv_parts, dq_parts = [], [], []

    red_waits = []

    def reduce_start(entries, name):
        numbers = sorted({e[1] for e in entries})
        lands = {}
        for (_, ln, shape, dt, _, _, _, _) in entries:
            if ln not in lands:
                lands[ln] = lax.empty(shape, dt)
        for (_, ln, _, _, _, _, own, at) in entries:
            lands[ln] = lax.dynamic_update_slice(lands[ln], own, idx(*at))
        plan = [(n, numbers.index(e[1]), e[4], e[5]) for n, e in enumerate(entries)]
        sems, srcs, lands_t, token = _exchange_start([e[0] for e in entries], [lands[ln] for ln in numbers], plan,
                                                     name + "_start")
        red_waits.append((sems, srcs, lands_t, plan, name + "_wait"))
        return token[0:1, 0:1]

    def w1_entry(i, ln, local, n_layers):
        own = lax.dynamic_slice(g_w1[i], idx(0, chip * d), (d, d))[None, None]
        return (g_w1[i], ln, (N_CHIPS, n_layers, d, d), BF16, cols2(d), into(local), own, (chip, local, 0, 0))

    def w2_entry(i, ln, local, n_layers):
        own = lax.dynamic_slice(g_w2[i], idx(chip * d, 0), (d, d))[None, None]
        return (g_w2[i], ln, (N_CHIPS, n_layers, d, d), BF16, rows2(d), into(local), own, (chip, local, 0, 0))

    def glu_entry(i, ln, local, n_layers):
        own = lax.dynamic_slice(g_glu[i], idx(0, chip * 2 * ds4), (d, 2 * ds4))[None, None]
        return (g_glu[i], ln, (N_CHIPS, n_layers, d, 2 * ds4), BF16, cols2(2 * ds4), into(local), own,
                (chip, local, 0, 0))

    def rows_entry(g, ln, local):
        own = lax.dynamic_slice(g, idx(chip * ds4, 0), (ds4, d))[None, None]
        return (g, ln, (N_CHIPS, n_b, ds4, d), BF16, rows2(ds4), into(local), own, (chip, local, 0, 0))

    def into(layer):
        return lambda ref, s: ref.at[s, layer]

    def mlp_back(dh, i, rec, tie=None):
        gain = mlp_norm[i:i + 1] if tie is None else mlp_norm[i:i + 1] + tie
        dh_in, hm, a, dap, g_mlp[i] = _mlp_bwd(dh, rec["h1"], rec["ap"], gain, layer_w1(i),
                                               layer_w2(i), f"mlp_bwd_{i}")
        g_w2[i] = _matmul_tn(a, dh, f"mlp_dw2_{i}")
        g_w1[i] = _matmul_tn(hm, dap, f"mlp_dw1_{i}")
        return dh_in

    for jb in reversed(range(n_b)):
        i = n_a + jb
        rec = saved[i]
        dh = mlp_back(dh, i, rec)
        do, delta = _o_bwd(dh, rec["o"], wo_all[jb], head_sel, f"attn_out_bwd_{jb}")
        g_wo[jb] = _matmul_tn(rec["o"], dh, f"attn_dwo_{jb}")
        dqx, dkx, dv = _flash_bwd(rec["qx"], kx, vx, f2_rep, do, rec["lse"], delta.reshape(n_heads // 2, 2, seq),
                                  f"attn_core_bwd_{jb}")
        dk_parts.append(dkx)
        dv_parts.append(dv)
        dq_parts.append(dqx)
        dh, hn, dqs, g_mix[i] = _q_bwd(dh, rec["h0"], dqx, mix_norm[i:i + 1], wq_x[jb], f"attn_q_bwd_{jb}")
        g_wq[jb] = _unslot_cols(_matmul_tn(hn, dqs, f"attn_dwq_{jb}"))

    dh, hk, dkb, dvb, dfb, g_kvn, g_bf = _kvf_bwd(dh, h_kv, dk_parts[0], dk_parts[1], dv_parts[0], dv_parts[1],
                                                  dq_parts[0], dq_parts[1], flog, kv_norm.reshape(1, d), wk_x, wv, wf,
                                                  sel_q, sel_k, "kvf_bwd")
    g_kvf = jnp.concatenate([_unslot_cols(_matmul_tn(hk, dkb, "kvf_dwk")), _matmul_tn(hk, dvb, "kvf_dwv"),
                             _matmul_tn(hk, dfb, "kvf_dwf")[:, :n_heads]], axis=1)
    kvf_send = g_kvf.reshape(d, N_CHIPS, kvf_cols).transpose(1, 0, 2)
    group = [w1_entry(n_a + jb, 0, jb, n_b) for jb in range(n_b)]
    group += [w2_entry(n_a + jb, 1, jb, n_b) for jb in range(n_b)]
    group.append((kvf_send, 2, (N_CHIPS, d, kvf_cols), BF16, slot, slot,
                  lax.dynamic_index_in_dim(kvf_send, chip, 0, keepdims=True), (chip, 0, 0)))
    group += [rows_entry(g_wq[jb], 3, jb) for jb in range(n_b)]
    group += [rows_entry(g_wo[jb], 4, jb) for jb in range(n_b)]
    tie = reduce_start(group, "reduce_attn")

    for i in reversed(range(n_a)):
        rec = saved[i]
        if i == 0:
            group = [w1_entry(l, 0, l - 1, n_a - 1) for l in range(1, n_a)]
            group += [w2_entry(l, 1, l - 1, n_a - 1) for l in range(1, n_a)]
            group += [glu_entry(l, 2, l - 1, n_a - 1) for l in range(1, n_a)]
            tie = reduce_start(group, "reduce_s5")
        dh = mlp_back(dh, i, rec, tie if i in (0, n_a - 1) else None)
        dy, z, dzw = _s5_post_bwd(dh, rec["y"], rec["zw"], layer_glu(i), f"s5_glu_bwd_{i}")
        g_glu[i] = _matmul_tn(z, dzw, f"s5_dwglu_{i}")
        skip_gain = rec["dskip"]
        if i == 0:
            skip_gain = skip_gain + reduce_start([w1_entry(0, 0, 0, 1), w2_entry(0, 1, 0, 1), glu_entry(0, 2, 0, 1)],
                                                 "reduce_first")
        lbr, lbi = rec["lam"]
        bblk_t = rec["bblk"].transpose(0, 2, 1).astype(BF16)
        cblk_t = rec["cblk"].transpose(0, 2, 1).astype(BF16)
        du, glam8, gd8, gbblk, gcblk = _scan_bwd(dy, rec["u"], rec["states"], bblk_t, cblk_t,
                                                 _scan_tables(lbr, lbi, nj, True), skip_gain, f"s5_scan_bwd_{i}")
        dh, g_mix[i] = _norm_bwd_add(dh, du, rec["h0"], mix_norm[i:i + 1], f"s5_norm_bwd_{i}")
        glam = jnp.sum(glam8, axis=1)
        n_st = glam.shape[1] // 2
        g_lbr = glam[:, :n_st].reshape(n_groups, p_state)
        g_lbi = glam[:, n_st:].reshape(n_groups, p_state)
        g_bbr, g_bbi = _block_diag_in_grad(gbblk, nj, p_state, SSM_GROUP)
        g_cre, g_cim = _block_diag_out_grad(gcblk, nj, p_state, SSM_GROUP)
        _, pull = jax.vjp(_ssm_discretise, ssm_log_dt[i], ssm_a_re[i], ssm_a_im[i], ssm_b_re[i], ssm_b_im[i])
        g_ldt, g_are, g_aim, g_bre, g_bim = pull((g_lbr, g_lbi, g_bbr, g_bbi))
        g_ssm[i] = dict(log_dt=g_ldt, a_re=g_are, a_im=g_aim, b_re=g_bre, b_im=g_bim, c_re=g_cre, c_im=g_cim,
                        d=jnp.sum(gd8, axis=1).reshape(d))
    grad_x = dh[None]

    def stack_small(key):
        return jnp.stack([g_ssm[i][key] for i in range(n_a)])

    small_grads = [jnp.concatenate(g_mix, axis=0), jnp.concatenate(g_mlp, axis=0), stack_small("log_dt"),
                   stack_small("a_re"), stack_small("a_im"), stack_small("b_re"), stack_small("b_im"),
                   stack_small("c_re"), stack_small("c_im"), stack_small("d"), g_kvn.reshape(d),
                   g_bf[0, :n_heads], g_final.reshape(d)]
    small_w = [mix_norm, mlp_norm, ssm_log_dt, ssm_a_re, ssm_a_im, ssm_b_re, ssm_b_im, ssm_c_re, ssm_c_im,
               ssm_d, kv_norm, b_f, final_norm]
    small_m = [m_mix_norm, m_mlp_norm, m_ssm_log_dt, m_ssm_a_re, m_ssm_a_im, m_ssm_b_re, m_ssm_b_im, m_ssm_c_re,
               m_ssm_c_im, m_ssm_d, m_kv_norm, m_b_f, m_final_norm]
    small_v = [v_mix_norm, v_mlp_norm, v_ssm_log_dt, v_ssm_a_re, v_ssm_a_im, v_ssm_b_re, v_ssm_b_im, v_ssm_c_re,
               v_ssm_c_im, v_ssm_d, v_kv_norm, v_b_f, v_final_norm]
    skip_at = 9

    def widen_skip(part):
        return lax.dynamic_update_slice(jnp.zeros((n_a, d), F32), part, (0, chip * ds4))

    small_shapes = [a.shape for a in small_grads]
    pcols = 1024 if d >= 1024 else LANES
    g_small = _pack_small(small_grads, pcols)
    expand = lambda lst: _pack_small([widen_skip(a) if n == skip_at else a for n, a in enumerate(lst)], pcols)
    w_small, m_small, v_small = expand(small_w), expand(small_m), expand(small_v)
    srows = g_small.shape[0]

    assert n_b == 2
    reduce_start([(g_small, 0, (N_CHIPS, srows, pcols), F32, whole, slot, g_small[None], (chip, 0, 0))], "reduce_small")
    landed = [_exchange_wait(sems, srcs, lands, plan, dh, name) for (sems, srcs, lands, plan, name) in red_waits[:3]]
    (a_w1, a_w2, a_kvf, a_wq, a_wo), (s_w1, s_w2, s_glu), (f_w1, f_w2, f_glu) = landed

    def chip_sum(r, name):
        return _sum_chips(r.reshape(N_CHIPS, -1, r.shape[-1]), name)

    sums = [jnp.concatenate([chip_sum(f_w1, "sum_w1_first"), chip_sum(s_w1, "sum_w1_s5"),
                             chip_sum(a_w1, "sum_w1_attn")], axis=0),
            jnp.concatenate([chip_sum(f_w2, "sum_w2_first"), chip_sum(s_w2, "sum_w2_s5"),
                             chip_sum(a_w2, "sum_w2_attn")], axis=0),
            jnp.concatenate([chip_sum(f_glu, "sum_glu_first"), chip_sum(s_glu, "sum_glu_s5")], axis=0),
            chip_sum(a_kvf, "sum_kvf"), chip_sum(a_wq, "sum_wq"), chip_sum(a_wo, "sum_wo")]
    others = _core_exchange(sums, "reduce_cores")

    def two(a):
        return a.reshape(-1, a.shape[-1])

    big_w = [(mlp_w1, m_mlp_w1, v_mlp_w1), (mlp_w2, m_mlp_w2, v_mlp_w2), (ssm_w_glu, m_ssm_w_glu, v_ssm_w_glu),
             (w_kvf, m_w_kvf, v_w_kvf), (attn_wq, m_attn_wq, v_attn_wq), (attn_wo, m_attn_wo, v_attn_wo)]
    big_out = []
    for n, (w, m, v) in enumerate(big_w):
        res = _adamw(sums[n], others[n], two(w), two(m), two(v), f"adamw_{n}")
        big_out.append([r.reshape(w.shape) for r in res])
    sems, srcs, lands, plan, name = red_waits[3]
    (r_small,) = _exchange_wait(sems, srcs, lands, plan, big_out[-1][1], name)
    sum_small = chip_sum(r_small, "sum_small")
    (other_small,) = _core_exchange([sum_small], "reduce_cores_small")
    small_out = _adamw(sum_small, other_small, w_small, m_small, v_small, "adamw_small")

    def narrow_skip(a):
        return lax.dynamic_slice(a, (0, chip * ds4), (n_a, ds4))

    unpacked = []
    for packed in small_out:
        parts = _unpack_small(packed, small_shapes, pcols)
        parts[skip_at] = narrow_skip(parts[skip_at])
        unpacked.append(parts)

    order = ["mix_norm", "mlp_norm", "mlp_w1", "mlp_w2", "ssm_log_dt", "ssm_a_re", "ssm_a_im", "ssm_b_re",
             "ssm_b_im", "ssm_c_re", "ssm_c_im", "ssm_d", "ssm_w_glu", "kv_norm", "w_kvf", "b_f", "attn_wq",
             "attn_wo", "final_norm"]
    small_names = ["mix_norm", "mlp_norm", "ssm_log_dt", "ssm_a_re", "ssm_a_im", "ssm_b_re", "ssm_b_im",
                   "ssm_c_re", "ssm_c_im", "ssm_d", "kv_norm", "b_f", "final_norm"]
    big_names = ["mlp_w1", "mlp_w2", "ssm_w_glu", "w_kvf", "attn_wq", "attn_wo"]
    outs = [loss, grad_x]
    for kind in range(4):
        for name in order:
            if name in big_names:
                outs.append(big_out[big_names.index(name)][kind])
            else:
                outs.append(unpacked[kind][small_names.index(name)])
    return tuple(outs)
```

```python
import functools
import math

import jax
import jax.numpy as jnp
from jax import lax
from jax.experimental import pallas as pl
from jax.experimental.pallas import tpu as pltpu

F32 = jnp.float32
BF16 = jnp.bfloat16

RMS_EPS = 1e-6
SSM_GROUP = 16
SSM_STATE = 64
HEAD_DIM = 64
HEAD_PAIR = 2 * HEAD_DIM
LANES = 128
SUBLANES = 8
N_CHIPS = 4
ADAM_LR = 0.001
ADAM_B1 = 0.9
ADAM_B2 = 0.999
ADAM_EPS = 1e-08
ADAM_WD = 0.01
ADAM_STEP = 10
GELU_C = math.sqrt(2.0 / math.pi)
GELU_A = 0.044715
NEG = -1e30
LN2 = math.log(2.0)
LOG2E = 1.0 / LN2
VMEM_LIMIT = 56 * 1024 * 1024
MESH = pl.DeviceIdType.MESH

NT_DIMS = (((1,), (1,)), ((), ()))
TN_DIMS = (((0,), (0,)), ((), ()))


def _cp(*sem):
    return pltpu.CompilerParams(dimension_semantics=sem if sem else None, vmem_limit_bytes=VMEM_LIMIT)


def _zero_idx(nd, *_):
    return (0,) * nd


def _tile(n, t):
    if n <= t:
        return n
    for cand in range(t - t % SUBLANES, 0, -SUBLANES):
        if n % cand == 0:
            return cand
    raise ValueError((n, t))


def _rms_fwd(h, g):
    r = lax.rsqrt(jnp.mean(h * h, axis=-1, keepdims=True) + RMS_EPS)
    hhat = h * r
    return hhat * g, hhat, r


def _rms_bwd(du, hhat, r, g):
    dhh = du * g
    dh = r * (dhh - hhat * jnp.mean(dhh * hhat, axis=-1, keepdims=True))
    return dh, du * hhat


def _sigmoid(x):
    return 1.0 / (1.0 + jnp.exp(-x))


def _gelu(x):
    t = jnp.tanh(GELU_C * (x + GELU_A * x * x * x))
    return 0.5 * x * (1.0 + t)


def _gelu_grad(x):
    t = jnp.tanh(GELU_C * (x + GELU_A * x * x * x))
    return 0.5 * (1.0 + t) + 0.5 * x * (1.0 - t * t) * GELU_C * (1.0 + 3.0 * GELU_A * x * x)


def _row_fold(x):
    tm, w = x.shape
    return jnp.sum(x.reshape(tm // SUBLANES, SUBLANES, w), axis=0)


def _split3(x):
    hi = x.astype(BF16)
    r1 = x - hi.astype(F32)
    mid = r1.astype(BF16)
    lo = (r1 - mid.astype(F32)).astype(BF16)
    return hi, mid, lo


def _exact_dot(ones_mat, x):
    hi, mid, lo = _split3(x)
    d = functools.partial(jnp.dot, preferred_element_type=F32)
    return d(ones_mat, hi) + d(ones_mat, mid) + d(ones_mat, lo)


def _rows_call(body, name, tm, row_ins, const_ins, row_outs, acc_outs=(), scratch=(), reverse=False, col_outs=()):
    n = row_ins[0].shape[0]
    nb = n // tm
    if reverse:
        ridx = lambda i: (nb - 1 - i, 0)
        cidx = lambda i: (0, nb - 1 - i)
    else:
        ridx = lambda i: (i, 0)
        cidx = lambda i: (0, i)
    in_specs = [pl.BlockSpec((tm, a.shape[1]), ridx) for a in row_ins]
    in_specs += [pl.BlockSpec(a.shape, functools.partial(_zero_idx, a.ndim), pipeline_mode=pl.Buffered(1))
                 for a in const_ins]
    out_shape = [jax.ShapeDtypeStruct((n, w), dt) for (w, dt) in row_outs]
    out_shape += [jax.ShapeDtypeStruct(s, dt) for (s, dt) in acc_outs]
    out_shape += [jax.ShapeDtypeStruct((r, n), dt) for (r, dt) in col_outs]
    out_specs = [pl.BlockSpec((tm, w), ridx) for (w, dt) in row_outs]
    out_specs += [pl.BlockSpec(s, functools.partial(_zero_idx, len(s))) for (s, dt) in acc_outs]
    out_specs += [pl.BlockSpec((r, tm), cidx) for (r, dt) in col_outs]
    return pl.pallas_call(
        body, name=name, grid=(nb,), in_specs=in_specs, out_specs=out_specs, out_shape=out_shape,
        scratch_shapes=list(scratch), compiler_params=_cp("arbitrary"),
    )(*row_ins, *const_ins)


def _norm_fwd(h, g, name):
    n, d = h.shape
    tm = _tile(n, 512)

    def body(h_ref, g_ref, u_ref):
        u_ref[...] = _rms_fwd(h_ref[...], g_ref[...])[0]

    return _rows_call(body, name, tm, [h], [g], [(d, F32)])[0]


def _norm_bwd_add(dh, du, h, g, name):
    n, d = h.shape
    tm = _tile(n, 512)
    nb = n // tm

    def body(dh_ref, du_ref, h_ref, g_ref, o_ref, dg_ref, acc):
        i = pl.program_id(0)

        @pl.when(i == 0)
        def _():
            acc[...] = jnp.zeros_like(acc)

        gain = g_ref[...]
        _, hhat, r = _rms_fwd(h_ref[...], gain)
        dhn, dgr = _rms_bwd(du_ref[...], hhat, r, gain)
        o_ref[...] = dh_ref[...] + dhn
        acc[...] += _row_fold(dgr)

        @pl.when(i == nb - 1)
        def _():
            dg_ref[...] = jnp.sum(acc[...], axis=0, keepdims=True)

    return _rows_call(body, name, tm, [dh, du, h], [g], [(d, F32)], [((1, d), F32)],
                      [pltpu.VMEM((SUBLANES, d), F32)])


def _mlp_fwd(h, g, w1, w2, name):
    n, d = h.shape
    ff = w1.shape[1]
    tm = _tile(n, 256)
    fc = _tile(ff, 1024)

    def body(h_ref, g_ref, w1_ref, w2_ref, o_ref, ap_ref):
        hin = h_ref[...]
        hb = _rms_fwd(hin, g_ref[...])[0].astype(BF16)
        acc = hin
        for c in range(ff // fc):
            cs = slice(c * fc, (c + 1) * fc)
            ap = jnp.dot(hb, w1_ref[:, cs], preferred_element_type=F32)
            ap_ref[:, cs] = ap.astype(BF16)
            rl = jnp.maximum(ap, 0.0)
            acc = acc + jnp.dot((rl * rl).astype(BF16), w2_ref[cs, :], preferred_element_type=F32)
        o_ref[...] = acc

    return _rows_call(body, name, tm, [h], [g, w1, w2], [(d, F32), (ff, BF16)])


def _mlp_bwd(dh, h, ap, g, w1, w2, name):
    n, d = h.shape
    ff = w1.shape[1]
    tm = _tile(n, 256)
    nb = n // tm
    fc = _tile(ff, 1024)

    def body(dh_ref, h_ref, ap_ref, g_ref, w1_ref, w2_ref, o_ref, hm_ref, a_ref, dap_ref, dg_ref, acc):
        i = pl.program_id(0)

        @pl.when(i == 0)
        def _():
            acc[...] = jnp.zeros_like(acc)

        gain = g_ref[...]
        dhv = dh_ref[...]
        hm, hhat, r = _rms_fwd(h_ref[...], gain)
        hm_ref[...] = hm.astype(BF16)
        dhb = dhv.astype(BF16)
        dhm = jnp.zeros((tm, d), F32)
        for c in range(ff // fc):
            cs = slice(c * fc, (c + 1) * fc)
            rl = jnp.maximum(ap_ref[:, cs].astype(F32), 0.0)
            a_ref[:, cs] = (rl * rl).astype(BF16)
            da = lax.dot_general(dhb, w2_ref[cs, :], NT_DIMS, preferred_element_type=F32)
            dap = (da * (2.0 * rl)).astype(BF16)
            dap_ref[:, cs] = dap
            dhm = dhm + lax.dot_general(dap, w1_ref[:, cs], NT_DIMS, preferred_element_type=F32)
        dhn, dgr = _rms_bwd(dhm, hhat, r, gain)
        o_ref[...] = dhv + dhn
        acc[...] += _row_fold(dgr)

        @pl.when(i == nb - 1)
        def _():
            dg_ref[...] = jnp.sum(acc[...], axis=0, keepdims=True)

    return _rows_call(body, name, tm, [dh, h, ap], [g, w1, w2],
                      [(d, F32), (d, BF16), (ff, BF16), (ff, BF16)], [((1, d), F32)],
                      [pltpu.VMEM((SUBLANES, d), F32)])


def _s5_post_fwd(h, y, w_glu, name):
    n, d = h.shape
    tm = _tile(n, 512)

    def body(h_ref, y_ref, w_ref, o_ref, zw_ref):
        z = _gelu(y_ref[...]).astype(BF16)
        zw = jnp.dot(z, w_ref[...], preferred_element_type=F32)
        zw_ref[...] = zw.astype(BF16)
        o_ref[...] = h_ref[...] + zw[:, :d] * _sigmoid(zw[:, d:])

    return _rows_call(body, name, tm, [h, y], [w_glu], [(d, F32), (2 * d, BF16)])


def _s5_post_bwd(dh, y, zw, w_glu, name):
    n, d = dh.shape
    tm = _tile(n, 512)

    def body(dh_ref, y_ref, zw_ref, w_ref, dy_ref, z_ref, dzw_ref):
        dhv = dh_ref[...]
        yv = y_ref[...]
        val = zw_ref[:, :d].astype(F32)
        sg = _sigmoid(zw_ref[:, d:].astype(F32))
        dzw = jnp.concatenate([dhv * sg, dhv * val * sg * (1.0 - sg)], axis=1).astype(BF16)
        dzw_ref[...] = dzw
        dz = lax.dot_general(dzw, w_ref[...], NT_DIMS, preferred_element_type=F32)
        dy_ref[...] = dz * _gelu_grad(yv)
        z_ref[...] = _gelu(yv).astype(BF16)

    return _rows_call(body, name, tm, [dh, y, zw], [w_glu], [(d, F32), (d, BF16), (2 * d, BF16)])


def _q_fwd(h, g, wq_x, ones_x, name):
    n, d = h.shape
    tm = _tile(n, 512)
    scale = LOG2E * HEAD_DIM ** -0.5

    def body(h_ref, g_ref, w_ref, one_ref, q_ref):
        hb = _rms_fwd(h_ref[...], g_ref[...])[0].astype(BF16)
        q_ref[...] = (jnp.dot(hb, w_ref[...], preferred_element_type=F32) * scale + one_ref[...]).astype(BF16)

    return _rows_call(body, name, tm, [h], [g, wq_x, ones_x], [(wq_x.shape[1], BF16)])[0]


def _q_bwd(dh, h, dq, g, wq, name):
    n, d = h.shape
    tm = _tile(n, 512)
    nb = n // tm
    scale = HEAD_DIM ** -0.5

    def body(dh_ref, h_ref, dq_ref, g_ref, w_ref, o_ref, hn_ref, dqs_ref, dg_ref, acc):
        i = pl.program_id(0)

        @pl.when(i == 0)
        def _():
            acc[...] = jnp.zeros_like(acc)

        gain = g_ref[...]
        hn, hhat, r = _rms_fwd(h_ref[...], gain)
        hn_ref[...] = hn.astype(BF16)
        dqs = (dq_ref[...] * scale).astype(BF16)
        dqs_ref[...] = dqs
        dhn = lax.dot_general(dqs, w_ref[...], NT_DIMS, preferred_element_type=F32)
        dhi, dgr = _rms_bwd(dhn, hhat, r, gain)
        o_ref[...] = dh_ref[...] + dhi
        acc[...] += _row_fold(dgr)

        @pl.when(i == nb - 1)
        def _():
            dg_ref[...] = jnp.sum(acc[...], axis=0, keepdims=True)

    return _rows_call(body, name, tm, [dh, h, dq], [g, wq], [(d, F32), (d, BF16), (wq.shape[1], BF16)],
                      [((1, d), F32)], [pltpu.VMEM((SUBLANES, d), F32)])


def _o_fwd(h, o, wo, name):
    n, d = h.shape
    tm = _tile(n, 512)

    def body(h_ref, o_ref, w_ref, out_ref):
        out_ref[...] = h_ref[...] + jnp.dot(o_ref[...], w_ref[...], preferred_element_type=F32)

    return _rows_call(body, name, tm, [h, o], [wo], [(d, F32)])[0]


def _exact_dot_nt(ones_mat, x):
    hi, mid, lo = _split3(x)
    d = functools.partial(lax.dot_general, dimension_numbers=NT_DIMS, preferred_element_type=F32)
    return d(ones_mat, hi) + d(ones_mat, mid) + d(ones_mat, lo)


def _o_bwd(dh, o, wo, head_sel, name):
    n, d = dh.shape
    tm = _tile(n, 512)

    def body(dh_ref, o_ref, w_ref, e_ref, do_ref, dl_ref):
        do = lax.dot_general(dh_ref[...].astype(BF16), w_ref[...], NT_DIMS, preferred_element_type=F32).astype(BF16)
        do_ref[...] = do
        dl_ref[...] = _exact_dot_nt(e_ref[...], do.astype(F32) * o_ref[...].astype(F32))

    return _rows_call(body, name, tm, [dh, o], [wo, head_sel], [(wo.shape[0], BF16)],
                      col_outs=[(head_sel.shape[0], F32)])


def _exact_dot_rhs(x, ones_mat):
    hi, mid, lo = _split3(x)
    d = functools.partial(jnp.dot, preferred_element_type=F32)
    return d(hi, ones_mat) + d(mid, ones_mat) + d(lo, ones_mat)


def _kvf_fwd(h, g, wk, wv, wf, bf, k_ones, v_ones, spread, name):
    n, d = h.shape
    tm = _tile(n, 512)

    def body(h_ref, g_ref, wk_ref, wv_ref, wf_ref, bf_ref, ko_ref, vo_ref, sp_ref,
             k_ref, v_ref, fl_ref, cum_ref, rep_ref, carry):
        i = pl.program_id(0)

        @pl.when(i == 0)
        def _():
            carry[...] = jnp.zeros_like(carry)

        hb = _rms_fwd(h_ref[...], g_ref[...])[0].astype(BF16)
        k_ref[...] = (jnp.dot(hb, wk_ref[...], preferred_element_type=F32) + ko_ref[...]).astype(BF16)
        v_ref[...] = (jnp.dot(hb, wv_ref[...], preferred_element_type=F32) + vo_ref[...]).astype(BF16)
        fl = jnp.dot(hb, wf_ref[...], preferred_element_type=F32) + bf_ref[...]
        fl_ref[...] = fl
        logf = jnp.minimum(fl, 0.0) - jnp.log(1.0 + jnp.exp(-jnp.abs(fl)))
        rows = lax.broadcasted_iota(jnp.int32, (tm, tm), 0)
        cols = lax.broadcasted_iota(jnp.int32, (tm, tm), 1)
        lower = (rows >= cols).astype(BF16)
        cum = _exact_dot(lower, logf) + carry[0:1, :]
        cum_ref[...] = cum
        rep_ref[...] = _exact_dot_rhs(cum * LOG2E, sp_ref[...])
        carry[...] = jnp.broadcast_to(cum[tm - 1:tm, :], carry.shape)

    return _rows_call(body, name, tm, [h], [g, wk, wv, wf, bf, k_ones, v_ones, spread],
                      [(wk.shape[1], BF16), (wv.shape[1], BF16), (LANES, F32), (LANES, F32), (spread.shape[1], F32)],
                      scratch=[pltpu.VMEM((SUBLANES, LANES), F32)])


def _kvf_bwd(dh, h, dk1, dk2, dv1, dv2, dq1, dq2, fl, g, wk, wv, wf, sel_q, sel_k, name):
    n, d = h.shape
    tm = _tile(n, 256)
    nb = n // tm

    def body(dh_ref, h_ref, dk1_ref, dk2_ref, dv1_ref, dv2_ref, dq1_ref, dq2_ref, fl_ref,
             g_ref, wk_ref, wv_ref, wf_ref, sq_ref, sk_ref,
             o_ref, hk_ref, dk_ref, dv_ref, df_ref, dg_ref, db_ref, acc, bacc, carry):
        i = pl.program_id(0)

        @pl.when(i == 0)
        def _():
            acc[...] = jnp.zeros_like(acc)
            bacc[...] = jnp.zeros_like(bacc)
            carry[...] = jnp.zeros_like(carry)

        dkx = dk1_ref[...] + dk2_ref[...]
        dcum = _exact_dot_rhs(dq1_ref[...] + dq2_ref[...], sq_ref[...]) - _exact_dot_rhs(dkx, sk_ref[...])
        rows = lax.broadcasted_iota(jnp.int32, (tm, tm), 0)
        cols = lax.broadcasted_iota(jnp.int32, (tm, tm), 1)
        upper = (rows <= cols).astype(BF16)
        dlogf = _exact_dot(upper, dcum) + carry[0:1, :]
        carry[...] = jnp.broadcast_to(dlogf[0:1, :], carry.shape)
        df = dlogf / (1.0 + jnp.exp(fl_ref[...]))
        dfb = df.astype(BF16)
        df_ref[...] = dfb
        bacc[...] += _row_fold(df)
        dkb = (dkx * LN2).astype(BF16)
        dvb = (dv1_ref[...] + dv2_ref[...]).astype(BF16)
        dk_ref[...] = dkb
        dv_ref[...] = dvb
        gain = g_ref[...]
        hk, hhat, r = _rms_fwd(h_ref[...], gain)
        hk_ref[...] = hk.astype(BF16)
        dhk = lax.dot_general(dkb, wk_ref[...], NT_DIMS, preferred_element_type=F32)
        dhk = dhk + lax.dot_general(dvb, wv_ref[...], NT_DIMS, preferred_element_type=F32)
        dhk = dhk + lax.dot_general(dfb, wf_ref[...], NT_DIMS, preferred_element_type=F32)
        dhi, dgr = _rms_bwd(dhk, hhat, r, gain)
        o_ref[...] = dh_ref[...] + dhi
        acc[...] += _row_fold(dgr)

        @pl.when(i == nb - 1)
        def _():
            dg_ref[...] = jnp.sum(acc[...], axis=0, keepdims=True)
            db_ref[...] = jnp.sum(bacc[...], axis=0, keepdims=True)

    return _rows_call(body, name, tm, [dh, h, dk1, dk2, dv1, dv2, dq1, dq2, fl], [g, wk, wv, wf, sel_q, sel_k],
                      [(d, F32), (d, BF16), (wk.shape[1], BF16), (wv.shape[1], BF16), (LANES, BF16)],
                      [((1, d), F32), ((1, LANES), F32)],
                      [pltpu.VMEM((SUBLANES, d), F32), pltpu.VMEM((SUBLANES, LANES), F32),
                       pltpu.VMEM((SUBLANES, LANES), F32)], reverse=True)


def _loss_head(h, target, g, name):
    n, d = h.shape
    tm = _tile(n, 512)
    nb = n // tm

    def body(h_ref, t_ref, g_ref, dh_ref, loss_ref, dg_ref, lacc, gacc):
        i = pl.program_id(0)

        @pl.when(i == 0)
        def _():
            lacc[...] = jnp.zeros_like(lacc)
            gacc[...] = jnp.zeros_like(gacc)

        gain = g_ref[...]
        yv, hhat, r = _rms_fwd(h_ref[...], gain)
        e = yv - t_ref[...]
        lacc[...] += _row_fold(e * e)
        dhv, dgr = _rms_bwd(e * (1.0 / d), hhat, r, gain)
        dh_ref[...] = dhv
        gacc[...] += _row_fold(dgr)

        @pl.when(i == nb - 1)
        def _():
            loss_ref[...] = jnp.full((1, LANES), jnp.sum(lacc[...]) * (0.5 / d), F32)
            dg_ref[...] = jnp.sum(gacc[...], axis=0, keepdims=True)

    return _rows_call(body, name, tm, [h, target], [g], [(d, F32)], [((1, LANES), F32), ((1, d), F32)],
                      [pltpu.VMEM((SUBLANES, d), F32), pltpu.VMEM((SUBLANES, d), F32)])


def _matmul_tn(a, b, name, out_dtype=BF16):
    l, m = a.shape
    n = b.shape[1]
    tl = _tile(l, 2048)
    tmm = _tile(m, 1024)
    tn = _tile(n, 1024)
    nl = l // tl

    def body(a_ref, b_ref, o_ref, acc):
        k = pl.program_id(2)

        @pl.when(k == 0)
        def _():
            acc[...] = jnp.zeros_like(acc)

        acc[...] += lax.dot_general(a_ref[...].astype(BF16), b_ref[...].astype(BF16), TN_DIMS,
                                    preferred_element_type=F32)

        @pl.when(k == nl - 1)
        def _():
            o_ref[...] = acc[...].astype(out_dtype)

    return pl.pallas_call(
        body, name=name, grid=(m // tmm, n // tn, nl),
        in_specs=[pl.BlockSpec((tl, tmm), lambda i, j, k: (k, i)), pl.BlockSpec((tl, tn), lambda i, j, k: (k, j))],
        out_specs=pl.BlockSpec((tmm, tn), lambda i, j, k: (i, j)),
        out_shape=jax.ShapeDtypeStruct((m, n), out_dtype),
        scratch_shapes=[pltpu.VMEM((tmm, tn), F32)],
        compiler_params=_cp("parallel", "parallel", "arbitrary"),
    )(a, b)


def _scan_fwd(u, bblk, cblk, tabs, dskip, name, layer=0):
    l, d = u.shape
    _, gb, n2 = bblk.shape
    nj = d // gb
    first = layer * nj
    n = n2 // 2
    tm = _tile(l, 512)
    nb = l // tm
    nsub = 4 if tm % (4 * SUBLANES) == 0 else 1
    sub = tm // nsub

    def body(u_ref, b_ref, c_ref, t_ref, d_ref, st_ref, y_ref, carry):
        i = pl.program_id(1)

        @pl.when(i == 0)
        def _():
            carry[...] = jnp.zeros_like(carry)

        def step(rb, c):
            cr, ci = c
            rows = pl.ds(rb * SUBLANES, SUBLANES)
            xr = st_ref[rows, 0:n]
            xi = st_ref[rows, n:n2]
            for lvl, sh in enumerate((1, 2, 4)):
                ar = t_ref[0, 2 * lvl]
                ai = t_ref[0, 2 * lvl + 1]
                sr = pltpu.roll(xr, sh, 0)
                si = pltpu.roll(xi, sh, 0)
                xr, xi = xr + ar * sr - ai * si, xi + ar * si + ai * sr
            lr = t_ref[0, 6]
            li = t_ref[0, 7]
            xr, xi = xr + lr * cr - li * ci, xi + lr * ci + li * cr
            st_ref[rows, 0:n] = xr
            st_ref[rows, n:n2] = xi
            return (jnp.broadcast_to(xr[SUBLANES - 1:SUBLANES, :], (SUBLANES, n)),
                    jnp.broadcast_to(xi[SUBLANES - 1:SUBLANES, :], (SUBLANES, n)))

        def project_in(sb):
            rows = slice(sb * sub, (sb + 1) * sub)
            st_ref[rows, :] = jnp.dot(u_ref[rows, :].astype(BF16), b_ref[0], preferred_element_type=F32)

        def project_out(sb):
            rows = slice(sb * sub, (sb + 1) * sub)
            y_ref[rows, :] = (jnp.dot(st_ref[rows, :].astype(BF16), c_ref[0], preferred_element_type=F32)
                              + d_ref[0] * u_ref[rows, :])

        c = (carry[:, 0:n], carry[:, n:n2])
        project_in(0)
        for sb in range(nsub):
            if sb + 1 < nsub:
                project_in(sb + 1)
            for rb in range(sb * sub // SUBLANES, (sb + 1) * sub // SUBLANES):
                c = step(rb, c)
            project_out(sb)
        carry[:, 0:n] = c[0]
        carry[:, n:n2] = c[1]

    return pl.pallas_call(
        body, name=name, grid=(nj, nb),
        in_specs=[pl.BlockSpec((tm, gb), lambda j, i: (i, j)),
                  pl.BlockSpec((1, gb, n2), lambda j, i: (first + j, 0, 0)),
                  pl.BlockSpec((1, n2, gb), lambda j, i: (first + j, 0, 0)),
                  pl.BlockSpec((1, 8, SUBLANES, n), lambda j, i: (first + j, 0, 0, 0)),
                  pl.BlockSpec((1, 1, gb), lambda j, i: (layer, 0, j))],
        out_specs=[pl.BlockSpec((tm, n2), lambda j, i: (i, j)), pl.BlockSpec((tm, gb), lambda j, i: (i, j))],
        out_shape=[jax.ShapeDtypeStruct((l, nj * n2), F32), jax.ShapeDtypeStruct((l, d), F32)],
        scratch_shapes=[pltpu.VMEM((SUBLANES, n2), F32)],
        compiler_params=_cp("parallel", "arbitrary"),
    )(u, bblk, cblk, tabs, dskip)


def _scan_bwd(dy, u, states, bblk_t, cblk_t, tabs, dskip, name, layer=0):
    l, d = u.shape
    _, n2, gb = bblk_t.shape
    nj = d // gb
    first = layer * nj
    n = n2 // 2
    tm = _tile(l, 512)
    nb = l // tm
    nsub = 4 if tm % (4 * SUBLANES) == 0 else 1
    sub = tm // nsub

    def body(dy_ref, u_ref, st_ref, prev_ref, bt_ref, ct_ref, t_ref, d_ref,
             du_ref, glam_ref, gd_ref, gb_ref, gc_ref, gx, carry):
        i = pl.program_id(1)
        ib = nb - 1 - i

        @pl.when(i == 0)
        def _():
            carry[...] = jnp.zeros_like(carry)
            glam_ref[...] = jnp.zeros_like(glam_ref)
            gd_ref[...] = jnp.zeros_like(gd_ref)
            gb_ref[...] = jnp.zeros_like(gb_ref)
            gc_ref[...] = jnp.zeros_like(gc_ref)

        last_row = lax.broadcasted_iota(jnp.int32, (SUBLANES, n), 0) == SUBLANES - 1

        def block(rows, xp_r, xp_i, c):
            cr, ci = c
            gr = gx[rows, 0:n]
            gi = gx[rows, n:n2]
            for lvl, sh in enumerate((1, 2, 4)):
                ar = t_ref[0, 2 * lvl]
                ai = t_ref[0, 2 * lvl + 1]
                sr = pltpu.roll(gr, SUBLANES - sh, 0)
                si = pltpu.roll(gi, SUBLANES - sh, 0)
                gr, gi = gr + ar * sr - ai * si, gi + ar * si + ai * sr
            lr = t_ref[0, 6]
            li = t_ref[0, 7]
            gr, gi = gr + lr * cr - li * ci, gi + lr * ci + li * cr
            gx[rows, 0:n] = gr
            gx[rows, n:n2] = gi
            xs_r = pltpu.roll(jnp.where(last_row, xp_r, st_ref[rows, 0:n]), 1, 0)
            xs_i = pltpu.roll(jnp.where(last_row, xp_i, st_ref[rows, n:n2]), 1, 0)
            glam_ref[0, :, 0:n] += gr * xs_r + gi * xs_i
            glam_ref[0, :, n:n2] += gi * xs_r - gr * xs_i
            return (jnp.broadcast_to(gr[0:1, :], (SUBLANES, n)), jnp.broadcast_to(gi[0:1, :], (SUBLANES, n)))

        def project_in(sb):
            rows = slice(sb * sub, (sb + 1) * sub)
            gx[rows, :] = jnp.dot(dy_ref[rows, :].astype(BF16), ct_ref[0], preferred_element_type=F32)

        def project_out(sb):
            rows = slice(sb * sub, (sb + 1) * sub)
            dyv = dy_ref[rows, :]
            uv = u_ref[rows, :]
            gxb = gx[rows, :].astype(BF16)
            du_ref[rows, :] = jnp.dot(gxb, bt_ref[0], preferred_element_type=F32) + d_ref[0] * dyv
            gd_ref[0] += _row_fold(dyv * uv)
            gb_ref[0] += lax.dot_general(uv.astype(BF16), gxb, TN_DIMS, preferred_element_type=F32)
            gc_ref[0] += lax.dot_general(st_ref[rows, :].astype(BF16), dyv.astype(BF16), TN_DIMS,
                                         preferred_element_type=F32)

        live = (ib > 0).astype(F32)
        c = (carry[:, 0:n], carry[:, n:n2])
        project_in(nsub - 1)
        for sb in reversed(range(nsub)):
            if sb > 0:
                project_in(sb - 1)
            for rb in reversed(range(sb * sub // SUBLANES, (sb + 1) * sub // SUBLANES)):
                rows = pl.ds(rb * SUBLANES, SUBLANES)
                if rb > 0:
                    before = pl.ds((rb - 1) * SUBLANES, SUBLANES)
                    c = block(rows, st_ref[before, 0:n], st_ref[before, n:n2], c)
                else:
                    c = block(rows, prev_ref[:, 0:n] * live, prev_ref[:, n:n2] * live, c)
            project_out(sb)
        carry[:, 0:n] = c[0]
        carry[:, n:n2] = c[1]

    rpb = tm // SUBLANES
    return pl.pallas_call(
        body, name=name, grid=(nj, nb),
        in_specs=[pl.BlockSpec((tm, gb), lambda j, i: (nb - 1 - i, j)),
                  pl.BlockSpec((tm, gb), lambda j, i: (nb - 1 - i, j)),
                  pl.BlockSpec((tm, n2), lambda j, i: (nb - 1 - i, j)),
                  pl.BlockSpec((SUBLANES, n2), lambda j, i: (jnp.maximum((nb - 1 - i) * rpb - 1, 0), j)),
                  pl.BlockSpec((1, n2, gb), lambda j, i: (first + j, 0, 0)),
                  pl.BlockSpec((1, gb, n2), lambda j, i: (first + j, 0, 0)),
                  pl.BlockSpec((1, 8, SUBLANES, n), lambda j, i: (first + j, 0, 0, 0)),
                  pl.BlockSpec((1, 1, gb), lambda j, i: (layer, 0, j))],
        out_specs=[pl.BlockSpec((tm, gb), lambda j, i: (nb - 1 - i, j)),
                   pl.BlockSpec((1, SUBLANES, n2), lambda j, i: (j, 0, 0)),
                   pl.BlockSpec((1, SUBLANES, gb), lambda j, i: (j, 0, 0)),
                   pl.BlockSpec((1, gb, n2), lambda j, i: (j, 0, 0)),
                   pl.BlockSpec((1, n2, gb), lambda j, i: (j, 0, 0))],
        out_shape=[jax.ShapeDtypeStruct((l, d), F32),
                   jax.ShapeDtypeStruct((nj, SUBLANES, n2), F32),
                   jax.ShapeDtypeStruct((nj, SUBLANES, gb), F32),
                   jax.ShapeDtypeStruct((nj, gb, n2), F32),
                   jax.ShapeDtypeStruct((nj, n2, gb), F32)],
        scratch_shapes=[pltpu.VMEM((tm, n2), F32), pltpu.VMEM((SUBLANES, n2), F32)],
        compiler_params=_cp("parallel", "arbitrary"),
    )(dy, u, states, states, bblk_t, cblk_t, tabs, dskip)


HEAD_SLOT = 128
PAIR_SLOT = 2 * HEAD_SLOT
ATTN_TILE = 1024
LANE_ROWSUM_P = HEAD_DIM
LANE_COLSUM_DS = HEAD_DIM
LANE_ROWSUM_DS = HEAD_DIM + 1


def _slot_cols(w):
    r, c = w.shape
    nh = c // HEAD_DIM
    return jnp.pad(w.reshape(r, nh, HEAD_DIM), ((0, 0), (0, 0), (0, HEAD_SLOT - HEAD_DIM))).reshape(r, nh * HEAD_SLOT)


def _unslot_cols(w):
    r, c = w.shape
    nh = c // HEAD_SLOT
    return w.reshape(r, nh, HEAD_SLOT)[:, :, :HEAD_DIM].reshape(r, nh * HEAD_DIM)


def _slot_ones(nh, lane):
    return jnp.tile((jnp.arange(HEAD_SLOT) == lane).astype(F32), nh).reshape(1, nh * HEAD_SLOT)


def _causal_tiles(n, by_query):
    if by_query:
        tiles = [(i, j) for i in range(n) for j in range(i + 1)]
    else:
        tiles = [(i, j) for j in range(n) for i in range(j, n)]
    return (jnp.asarray([t[0] for t in tiles], jnp.int32), jnp.asarray([t[1] for t in tiles], jnp.int32))


def _flash_fwd(qx, kx, vx, f2_rows, name):
    l = qx.shape[0]
    npair = qx.shape[1] // PAIR_SLOT
    d = npair * HEAD_PAIR
    tq = _tile(l, ATTN_TILE)
    tk = tq
    i_of, j_of = _causal_tiles(l // tq, by_query=True)

    def body(i_ref, j_ref, q_ref, k_ref, v_ref, f_ref, o_ref, lse_ref, m_sc, acc_sc):
        t = pl.program_id(1)
        i = i_ref[t]
        j = j_ref[t]

        @pl.when(j == 0)
        def _():
            m_sc[...] = jnp.full_like(m_sc, NEG)
            acc_sc[...] = jnp.zeros_like(acc_sc)

        def tile(on_diagonal):
            for hh in range(2):
                hs = slice(hh * HEAD_SLOT, (hh + 1) * HEAD_SLOT)
                s = lax.dot_general(q_ref[:, hs], k_ref[:, hs], NT_DIMS, preferred_element_type=F32)
                s = s - f_ref[0, hh:hh + 1, :]
                if on_diagonal:
                    keep = (lax.broadcasted_iota(jnp.int32, (tq, tk), 0)
                            >= lax.broadcasted_iota(jnp.int32, (tq, tk), 1))
                    s = jnp.where(keep, s, NEG)
                m_prev = m_sc[hh]
                m_new = jnp.maximum(m_prev, jnp.max(s, axis=-1, keepdims=True))
                alpha = jnp.exp2(m_prev - m_new)
                p = jnp.exp2(s - jnp.concatenate([m_new] * (tk // LANES), axis=1)).astype(BF16)
                acc_sc[hh] = alpha * acc_sc[hh] + jnp.dot(p, v_ref[:, hs], preferred_element_type=F32)
                m_sc[hh] = m_new

        @pl.when(j < i)
        def _():
            tile(False)

        @pl.when(j == i)
        def _():
            tile(True)
            lane0 = (lax.broadcasted_iota(jnp.int32, (SUBLANES, LANES), 1) == 0).astype(BF16)
            outs, lses = [], []
            for hh in range(2):
                acc = acc_sc[hh]
                lsum = acc[:, LANE_ROWSUM_P:LANE_ROWSUM_P + 1]
                outs.append(acc[:, :HEAD_DIM] / lsum)
                lse_cols = m_sc[hh] + jnp.log2(lsum)
                lses.append(_exact_dot_nt(lane0, lse_cols)[0:1, :])
            o_ref[...] = jnp.concatenate(outs, axis=1).astype(BF16)
            lse_ref[0] = jnp.concatenate(lses, axis=0)

    q_map = lambda h, t, i_ref, j_ref: (i_ref[t], h)
    kv_map = lambda h, t, i_ref, j_ref: (j_ref[t], h)
    return pl.pallas_call(
        body, name=name,
        grid_spec=pltpu.PrefetchScalarGridSpec(
            num_scalar_prefetch=2, grid=(npair, i_of.shape[0]),
            in_specs=[pl.BlockSpec((tq, PAIR_SLOT), q_map), pl.BlockSpec((tk, PAIR_SLOT), kv_map),
                      pl.BlockSpec((tk, PAIR_SLOT), kv_map),
                      pl.BlockSpec((1, 2, tk), lambda h, t, i_ref, j_ref: (h, 0, j_ref[t]))],
            out_specs=[pl.BlockSpec((tq, HEAD_PAIR), q_map),
                       pl.BlockSpec((1, 2, tq), lambda h, t, i_ref, j_ref: (h, 0, i_ref[t]))],
            scratch_shapes=[pltpu.VMEM((2, tq, LANES), F32), pltpu.VMEM((2, tq, HEAD_SLOT), F32)]),
        out_shape=[jax.ShapeDtypeStruct((l, d), BF16), jax.ShapeDtypeStruct((npair, 2, l), F32)],
        compiler_params=_cp("parallel", "arbitrary"),
    )(i_of, j_of, qx, kx, vx, f2_rows)


def _flash_bwd(qx, kx, vx, f2_rep, do, lse_rows, delta_rows, name):
    l = qx.shape[0]
    npair = qx.shape[1] // PAIR_SLOT
    d = npair * HEAD_PAIR
    tq = _tile(l, ATTN_TILE)
    tk = tq
    i_of, j_of = _causal_tiles(l // tq, by_query=False)

    def body(i_ref, j_ref, q_ref, k_ref, v_ref, f_ref, do_ref, lse_ref, dl_ref, dq_ref, dk_ref, dv_ref):
        t = pl.program_id(1)
        i = i_ref[t]
        j = j_ref[t]

        @pl.when(t == 0)
        def _():
            dq_ref[...] = jnp.zeros_like(dq_ref)

        @pl.when(i == j)
        def _():
            dk_ref[...] = jnp.zeros_like(dk_ref)
            dv_ref[...] = jnp.zeros_like(dv_ref)

        def tile(on_diagonal):
            dqs, dks, dvs = [], [], []
            for hh in range(2):
                hs = slice(hh * HEAD_SLOT, (hh + 1) * HEAD_SLOT)
                qh, kh = q_ref[:, hs], k_ref[:, hs]
                vh = v_ref[:, hh * HEAD_SLOT:hh * HEAD_SLOT + HEAD_DIM]
                doh = do_ref[:, hh * HEAD_DIM:(hh + 1) * HEAD_DIM]
                st = lax.dot_general(kh, qh, NT_DIMS, preferred_element_type=F32)
                st = st - jnp.concatenate([f_ref[:, hs]] * (tq // HEAD_SLOT), axis=1)
                pt = jnp.exp2(st - lse_ref[0, hh:hh + 1, :])
                if on_diagonal:
                    keep = (lax.broadcasted_iota(jnp.int32, (tk, tq), 1)
                            >= lax.broadcasted_iota(jnp.int32, (tk, tq), 0))
                    pt = jnp.where(keep, pt, 0.0)
                dpt = lax.dot_general(vh, doh, NT_DIMS, preferred_element_type=F32)
                dsb = (pt * (dpt - dl_ref[0, hh:hh + 1, :])).astype(BF16)
                dvs.append(jnp.dot(pt.astype(BF16), doh, preferred_element_type=F32))
                dks.append(jnp.dot(dsb, qh, preferred_element_type=F32))
                dqs.append(lax.dot_general(dsb, kh, TN_DIMS, preferred_element_type=F32))
            dv_ref[...] += jnp.concatenate(dvs, axis=1)
            dk_ref[...] += jnp.concatenate(dks, axis=1)
            dq_ref[pl.ds(pl.multiple_of(i * tq, tq), tq), :] += jnp.concatenate(dqs, axis=1)

        @pl.when(i > j)
        def _():
            tile(False)

        @pl.when(i == j)
        def _():
            tile(True)

    qmap = lambda h, t, i_ref, j_ref: (i_ref[t], h)
    kmap = lambda h, t, i_ref, j_ref: (j_ref[t], h)
    row_map = lambda h, t, i_ref, j_ref: (h, 0, i_ref[t])
    return pl.pallas_call(
        body, name=name,
        grid_spec=pltpu.PrefetchScalarGridSpec(
            num_scalar_prefetch=2, grid=(npair, i_of.shape[0]),
            in_specs=[pl.BlockSpec((tq, PAIR_SLOT), qmap), pl.BlockSpec((tk, PAIR_SLOT), kmap),
                      pl.BlockSpec((tk, PAIR_SLOT), kmap), pl.BlockSpec((tk, PAIR_SLOT), kmap),
                      pl.BlockSpec((tq, HEAD_PAIR), qmap),
                      pl.BlockSpec((1, 2, tq), row_map), pl.BlockSpec((1, 2, tq), row_map)],
            out_specs=[pl.BlockSpec((l, PAIR_SLOT), lambda h, t, i_ref, j_ref: (0, h)),
                       pl.BlockSpec((tk, PAIR_SLOT), kmap), pl.BlockSpec((tk, HEAD_PAIR), kmap)]),
        out_shape=[jax.ShapeDtypeStruct((l, npair * PAIR_SLOT), F32), jax.ShapeDtypeStruct((l, npair * PAIR_SLOT), F32),
                   jax.ShapeDtypeStruct((l, d), F32)],
        compiler_params=_cp("parallel", "arbitrary"),
    )(i_of, j_of, qx, kx, vx, f2_rep, do, lse_rows, delta_rows)


def _my_place():
    return lax.axis_index("x"), lax.axis_index("y"), lax.axis_index("c")


def _chip_exchange(srcs, out_meta, plan, name):
    n_src, n_out, n_plan = len(srcs), len(out_meta), len(plan)

    def body(*refs):
        src_refs = refs[:n_src]
        out_refs = refs[n_src:n_src + n_out]
        send_sems, recv_sems, local_sems = refs[n_src + n_out:]
        x, y, c = _my_place()
        me = 2 * x + y
        copies = []
        for n, (si, oi, src_view, dst_view) in enumerate(plan):
            local = pltpu.make_async_copy(src_view(src_refs[si], me), dst_view(out_refs[oi], me), local_sems.at[n])
            local.start()
            copies.append(local)
            for k in (1, 2, 3):
                peer = me ^ k
                rc = pltpu.make_async_remote_copy(
                    src_ref=src_view(src_refs[si], peer), dst_ref=dst_view(out_refs[oi], me),
                    send_sem=send_sems.at[n, k - 1], recv_sem=recv_sems.at[n, k - 1],
                    device_id=(peer >> 1, peer & 1, c), device_id_type=MESH)
                rc.start()
                copies.append(rc)
        for cp in copies:
            cp.wait()

    any_spec = pl.BlockSpec(memory_space=pl.ANY)
    return pl.pallas_call(
        body, name=name,
        in_specs=[any_spec] * n_src, out_specs=[any_spec] * n_out,
        out_shape=[jax.ShapeDtypeStruct(shape, dt) for (shape, dt) in out_meta],
        scratch_shapes=[pltpu.SemaphoreType.DMA((n_plan, 3)), pltpu.SemaphoreType.DMA((n_plan, 3)),
                        pltpu.SemaphoreType.DMA((n_plan,))],
    )(*srcs)


def _plan_copies(src_refs, land_refs, plan, send_sems, recv_sems):
    x, y, c = _my_place()
    me = 2 * x + y
    copies = []
    for n, (si, oi, src_view, dst_view) in enumerate(plan):
        for k in (1, 2, 3):
            peer = me ^ k
            copies.append(pltpu.make_async_remote_copy(
                src_ref=src_view(src_refs[si], peer), dst_ref=dst_view(land_refs[oi], me),
                send_sem=send_sems.at[3 * n + k - 1], recv_sem=recv_sems.at[3 * n + k - 1],
                device_id=(peer >> 1, peer & 1, c), device_id_type=MESH))
    return copies


def _hbm(a):
    return pltpu.HBM(a.shape, a.dtype)


def _exchange_start(srcs, lands, plan, name):
    n_src, n_land = len(srcs), len(lands)
    n_buf = n_src + n_land

    def body(*refs):
        send_sems, recv_sems = refs[n_buf], refs[n_buf + 1]
        token = refs[-1]
        for cp in _plan_copies(refs[:n_src], refs[n_src:n_buf], plan, send_sems, recv_sems):
            cp.start()
        token[...] = jnp.zeros_like(token)

    bufs = [pltpu.with_memory_space_constraint(a, pltpu.HBM) for a in (*srcs, *lands)]
    hbm = pl.BlockSpec(memory_space=pltpu.HBM)
    sem = pl.BlockSpec(memory_space=pltpu.SEMAPHORE)
    res = pl.pallas_call(
        body, name=name,
        out_shape=(pltpu.SemaphoreType.DMA((3 * len(plan),)), pltpu.SemaphoreType.DMA((3 * len(plan),)),
                   *[_hbm(a) for a in bufs], jax.ShapeDtypeStruct((SUBLANES, LANES), F32)),
        in_specs=[hbm] * n_buf, out_specs=(sem, sem, *[hbm] * n_buf, pl.BlockSpec(memory_space=pltpu.VMEM)),
        input_output_aliases={n: 2 + n for n in range(n_buf)},
        compiler_params=pltpu.CompilerParams(has_side_effects=pltpu.SideEffectType.DATAFLOW_SIDE_EFFECTING),
    )(*bufs)
    return (res[0], res[1]), list(res[2:2 + n_src]), list(res[2 + n_src:2 + n_buf]), res[-1]


def _exchange_wait(sems, srcs, lands, plan, after, name):
    n_src, n_land = len(srcs), len(lands)
    n_buf = n_src + n_land

    def body(*refs):
        send_sems, recv_sems = refs[n_buf], refs[n_buf + 1]
        for cp in _plan_copies(refs[:n_src], refs[n_src:n_buf], plan, send_sems, recv_sems):
            cp.wait_send()
            cp.wait_recv()

    hbm = pl.BlockSpec(memory_space=pltpu.HBM)
    sem = pl.BlockSpec(memory_space=pltpu.SEMAPHORE)
    res = pl.pallas_call(
        body, name=name, out_shape=tuple(_hbm(a) for a in (*srcs, *lands)),
        in_specs=[hbm] * n_buf + [sem, sem, pl.BlockSpec(memory_space=pl.ANY)], out_specs=tuple([hbm] * n_buf),
        input_output_aliases={n: n for n in range(n_buf)},
        compiler_params=pltpu.CompilerParams(has_side_effects=pltpu.SideEffectType.DATAFLOW_SIDE_EFFECTING),
    )(*srcs, *lands, sems[0], sems[1], after)
    return list(res[n_src:])


def _put_blocks(land, part, chip, block, in_map, out_map, name):
    grid = tuple(p // b for p, b in zip(part.shape, block))

    def body(chip_ref, part_ref, land_ref, out_ref):
        out_ref[...] = part_ref[...]

    return pl.pallas_call(
        body, name=name,
        grid_spec=pltpu.PrefetchScalarGridSpec(
            num_scalar_prefetch=1, grid=grid,
            in_specs=[pl.BlockSpec(block, in_map), pl.BlockSpec(memory_space=pl.ANY)],
            out_specs=pl.BlockSpec(block, out_map)),
        out_shape=jax.ShapeDtypeStruct(land.shape, land.dtype),
        input_output_aliases={2: 0},
        compiler_params=_cp(*["parallel"] * len(grid)),
    )(chip, part, land)


def _core_exchange(arrays, name):
    n_items = len(arrays)

    def body(*refs):
        srcs = refs[:n_items]
        outs = refs[n_items:2 * n_items]
        send_sems, recv_sems = refs[2 * n_items:]
        x, y, c = _my_place()
        copies = []
        for n in range(n_items):
            rc = pltpu.make_async_remote_copy(
                src_ref=srcs[n], dst_ref=outs[n], send_sem=send_sems.at[n], recv_sem=recv_sems.at[n],
                device_id=(x, y, 1 - c), device_id_type=MESH)
            rc.start()
            copies.append(rc)
        for cp in copies:
            cp.wait()

    any_spec = pl.BlockSpec(memory_space=pl.ANY)
    return pl.pallas_call(
        body, name=name,
        in_specs=[any_spec] * n_items, out_specs=[any_spec] * n_items,
        out_shape=[jax.ShapeDtypeStruct(a.shape, a.dtype) for a in arrays],
        scratch_shapes=[pltpu.SemaphoreType.DMA((n_items,)), pltpu.SemaphoreType.DMA((n_items,))],
    )(*arrays)


def _sum_chips(parts, name):
    _, rows, cols = parts.shape
    tm = _tile(rows, 512)

    def body(p_ref, o_ref):
        acc = p_ref[0].astype(F32)
        for s in range(1, N_CHIPS):
            acc = acc + p_ref[s].astype(F32)
        o_ref[...] = acc

    return pl.pallas_call(
        body, name=name, grid=(rows // tm,),
        in_specs=[pl.BlockSpec((N_CHIPS, tm, cols), lambda i: (0, i, 0))],
        out_specs=pl.BlockSpec((tm, cols), lambda i: (i, 0)),
        out_shape=jax.ShapeDtypeStruct((rows, cols), F32),
        compiler_params=_cp("parallel"),
    )(parts)


def _adamw(ga, gb, w, m, v, name):
    rows, cols = w.shape
    tm = _tile(rows, 512)
    c1 = 1.0 - ADAM_B1 ** ADAM_STEP
    c2 = 1.0 - ADAM_B2 ** ADAM_STEP

    def body(ga_ref, gb_ref, w_ref, m_ref, v_ref, g_ref, d_ref, nm_ref, nv_ref):
        g = ga_ref[...] + gb_ref[...]
        nm = ADAM_B1 * m_ref[...] + (1.0 - ADAM_B1) * g
        nv = ADAM_B2 * v_ref[...] + (1.0 - ADAM_B2) * (g * g)
        g_ref[...] = g
        nm_ref[...] = nm
        nv_ref[...] = nv
        d_ref[...] = -ADAM_LR * ((nm / c1) / (jnp.sqrt(nv / c2) + ADAM_EPS) + ADAM_WD * w_ref[...])

    spec = pl.BlockSpec((tm, cols), lambda i: (i, 0))
    return pl.pallas_call(
        body, name=name, grid=(rows // tm,), in_specs=[spec] * 5, out_specs=[spec] * 4,
        out_shape=[jax.ShapeDtypeStruct((rows, cols), F32)] * 4, compiler_params=_cp("parallel"),
    )(ga, gb, w, m, v)


def _ssm_discretise(log_dt, a_re, a_im, b_re, b_im):
    dt = jnp.exp(log_dt)[:, None]
    mag = jnp.exp(a_re * dt)
    lbr = mag * jnp.cos(a_im * dt)
    lbi = mag * jnp.sin(a_im * dt)
    den = a_re * a_re + a_im * a_im
    nr, ni = lbr - 1.0, lbi
    qr = (nr * a_re + ni * a_im) / den
    qi = (ni * a_re - nr * a_im) / den
    bbr = qr[..., None] * b_re - qi[..., None] * b_im
    bbi = qr[..., None] * b_im + qi[..., None] * b_re
    return lbr, lbi, bbr, bbi


def _cmul(ar, ai, br, bi):
    return ar * br - ai * bi, ar * bi + ai * br


def _scan_tables(lr, li, nj, reverse):
    lr = lr.reshape(nj, 1, -1)
    li = li.reshape(nj, 1, -1)
    if reverse:
        li = -li
    pows = [(lr, li)]
    for _ in range(7):
        pows.append(_cmul(*pows[-1], lr, li))
    r = jnp.arange(SUBLANES).reshape(1, SUBLANES, 1)
    if reverse:
        r = SUBLANES - 1 - r
    out = []
    for k in (1, 2, 4):
        pr, pi = pows[k - 1]
        keep = (r >= k).astype(F32)
        out += [pr * keep, pi * keep]
    shape = (nj, SUBLANES, lr.shape[-1])
    cr = jnp.zeros(shape, F32)
    ci = jnp.zeros(shape, F32)
    for e in range(SUBLANES):
        sel = (r == e).astype(F32)
        cr = cr + sel * pows[e][0]
        ci = ci + sel * pows[e][1]
    out += [cr, ci]
    return jnp.stack(out, axis=1)


def _group_eye(gl):
    return jnp.eye(gl, dtype=F32)


def _block_diag_in(bbr, bbi, nj):
    g, p, c = bbr.shape
    gl = g // nj
    eye = _group_eye(gl)[None, :, None, :, None]

    def one(b):
        t = b.reshape(nj, gl, p, c).transpose(0, 1, 3, 2)[:, :, :, None, :]
        return (t * eye).reshape(nj, gl * c, gl * p)

    return jnp.concatenate([one(bbr), one(bbi)], axis=2)


def _block_diag_in_grad(gmat, nj, p, c):
    gl = gmat.shape[1] // c
    n = gl * p
    eye = _group_eye(gl)[None, :, None, :, None]

    def one(m):
        t = jnp.sum(m.reshape(nj, gl, c, gl, p) * eye, axis=3)
        return t.transpose(0, 1, 3, 2).reshape(nj * gl, p, c)

    return one(gmat[:, :, :n]), one(gmat[:, :, n:])


def _block_diag_out(c_re, c_im, nj):
    g, c, p = c_re.shape
    gl = g // nj
    eye = _group_eye(gl)[None, :, None, :, None]

    def one(m):
        t = m.reshape(nj, gl, c, p).transpose(0, 1, 3, 2)[:, :, :, None, :]
        return (t * eye).reshape(nj, gl * p, gl * c)

    return jnp.concatenate([one(c_re), -one(c_im)], axis=1)


def _block_diag_out_grad(gmat, nj, p, c):
    gl = gmat.shape[2] // c
    n = gl * p
    eye = _group_eye(gl)[None, :, None, :, None]

    def one(m):
        t = jnp.sum(m.reshape(nj, gl, p, gl, c) * eye, axis=3)
        return t.transpose(0, 1, 3, 2).reshape(nj * gl, c, p)

    return one(gmat[:, :n, :]), -one(gmat[:, n:, :])


def _pad_rows(flat, cols):
    per = SUBLANES * cols
    n = flat.shape[0]
    total = -(-n // per) * per
    return jnp.pad(flat, (0, total - n)).reshape(total // cols, cols)


def _pack_small(arrs, cols):
    packed = jnp.concatenate([_pad_rows(a.reshape(-1), cols) for a in arrs], axis=0)
    rows = packed.shape[0]
    return jnp.pad(packed, ((0, -rows % 128), (0, 0)))


def _unpack_small(packed, shapes, cols):
    out = []
    row = 0
    for s in shapes:
        n = math.prod(s)
        rows = -(-n // (SUBLANES * cols)) * SUBLANES
        out.append(packed[row:row + rows].reshape(-1)[:n].reshape(s))
        row += rows
    return out


def kernel(x, mix_norm, mlp_norm, mlp_w1, mlp_w2, ssm_log_dt, ssm_a_re, ssm_a_im, ssm_b_re, ssm_b_im, ssm_c_re, ssm_c_im, ssm_d, ssm_w_glu, kv_norm, w_kvf, b_f, attn_wq, attn_wo, final_norm, loss_target, m_mix_norm, m_mlp_norm, m_mlp_w1, m_mlp_w2, m_ssm_log_dt, m_ssm_a_re, m_ssm_a_im, m_ssm_b_re, m_ssm_b_im, m_ssm_c_re, m_ssm_c_im, m_ssm_d, m_ssm_w_glu, m_kv_norm, m_w_kvf, m_b_f, m_attn_wq, m_attn_wo, m_final_norm, v_mix_norm, v_mlp_norm, v_mlp_w1, v_mlp_w2, v_ssm_log_dt, v_ssm_a_re, v_ssm_a_im, v_ssm_b_re, v_ssm_b_im, v_ssm_c_re, v_ssm_c_im, v_ssm_d, v_ssm_w_glu, v_kv_norm, v_w_kvf, v_b_f, v_attn_wq, v_attn_wo, v_final_norm):
    seq, d = x.shape[1], x.shape[2]
    depth = mix_norm.shape[0]
    n_a = ssm_log_dt.shape[0]
    n_b = depth - n_a
    ff = mlp_w1.shape[2] * N_CHIPS
    n_heads = d // HEAD_DIM
    n_groups = d // SSM_GROUP
    p_state = ssm_a_re.shape[2]
    gb = min(d, 256)
    nj = d // gb
    kvf_cols = w_kvf.shape[1]
    ds4 = d // N_CHIPS
    chip = 2 * lax.axis_index("x") + lax.axis_index("y")

    def cols_of(width):
        return lambda ref, s: ref.at[:, :, pl.ds(pl.multiple_of(s * width, LANES), width)]

    def rows_of(height):
        return lambda ref, s: ref.at[:, pl.ds(pl.multiple_of(s * height, SUBLANES), height), :]

    whole = lambda ref, s: ref
    slot = lambda ref, s: ref.at[s]
    def cols2(width):
        return lambda ref, s: ref.at[:, pl.ds(pl.multiple_of(s * width, LANES), width)]

    def rows2(height):
        return lambda ref, s: ref.at[pl.ds(pl.multiple_of(s * height, SUBLANES), height), :]

    assert n_a >= 2
    (skip_parts,) = _chip_exchange([ssm_d], [((N_CHIPS, n_a, ds4), F32)], [(0, 0, whole, slot)], "gather_skip")
    skip_all = skip_parts.transpose(1, 0, 2).reshape(n_a, d)
    w1_s, w2_s, glu_s = mlp_w1.astype(BF16), mlp_w2.astype(BF16), ssm_w_glu.astype(BF16)
    src_a = [w1_s[0], w2_s[0], glu_s[0]]
    plan_a = [(0, 0, whole, cols2(d)), (1, 1, whole, rows2(d)), (2, 2, whole, cols2(2 * ds4))]
    land_a = [lax.empty((d, ff), BF16), lax.empty((ff, d), BF16), lax.empty((d, 2 * d), BF16)]
    src_b = [w1_s[1:], w2_s[1:], glu_s[1:], w_kvf.astype(BF16), attn_wq.astype(BF16), attn_wo.astype(BF16)]
    plan_b = [(0, 0, whole, cols_of(d)), (1, 1, whole, rows_of(d)), (2, 2, whole, cols_of(2 * ds4)),
              (3, 3, whole, slot), (4, 4, whole, rows_of(ds4)), (5, 5, whole, rows_of(ds4))]
    land_b = [lax.empty((depth - 1, d, ff), BF16), lax.empty((depth - 1, ff, d), BF16),
              lax.empty((n_a - 1, d, 2 * d), BF16), lax.empty((N_CHIPS, d, kvf_cols), BF16),
              lax.empty((n_b, d, d), BF16), lax.empty((n_b, d, d), BF16)]
    def idx(*at):
        return [jnp.asarray(v, jnp.int32) for v in at]

    put = lax.dynamic_update_slice
    chip1 = chip.reshape(1).astype(jnp.int32)
    tp = _tile(d, 512)
    per = d // tp
    land_a = [_put_blocks(land_a[0], src_a[0], chip1, (tp, d), lambda r, z, c: (r, 0), lambda r, z, c: (r, c[0]),
                          "gather_own_w1_a"),
              _put_blocks(land_a[1], src_a[1], chip1, (tp, d), lambda r, z, c: (r, 0),
                          lambda r, z, c: (c[0] * per + r, 0), "gather_own_w2_a"),
              put(land_a[2], src_a[2], idx(0, chip * 2 * ds4))]
    land_b = [_put_blocks(land_b[0], src_b[0], chip1, (1, tp, d), lambda l, r, z, c: (l, r, 0),
                          lambda l, r, z, c: (l, r, c[0]), "gather_own_w1_b"),
              _put_blocks(land_b[1], src_b[1], chip1, (1, tp, d), lambda l, r, z, c: (l, r, 0),
                          lambda l, r, z, c: (l, c[0] * per + r, 0), "gather_own_w2_b"),
              put(land_b[2], src_b[2], idx(0, 0, chip * 2 * ds4)), put(land_b[3], src_b[3][None], idx(chip, 0, 0)),
              put(land_b[4], src_b[4], idx(0, chip * ds4, 0)), put(land_b[5], src_b[5], idx(0, chip * ds4, 0))]
    sems_a, src_a, land_a, token_a = _exchange_start(src_a, land_a, plan_a, "gather_start_a")
    sems_b, src_b, land_b, token_b = _exchange_start(src_b, land_b, plan_b, "gather_start_b")
    started = token_a[0:1, 0:1] + token_b[0:1, 0:1]

    def layer_w1(i):
        return w1_0 if i == 0 else w1_rest[i - 1]

    def layer_w2(i):
        return w2_0 if i == 0 else w2_rest[i - 1]

    def layer_glu(i):
        return glu_0 if i == 0 else glu_rest[i - 1]

    h = x[0]
    target = loss_target[0]

    lbr_l, lbi_l, bbr_l, bbi_l = jax.vmap(_ssm_discretise)(ssm_log_dt, ssm_a_re, ssm_a_im, ssm_b_re, ssm_b_im)
    bblk_l = jax.vmap(lambda r, m: _block_diag_in(r, m, nj))(bbr_l, bbi_l)
    cblk_l = jax.vmap(lambda r, m: _block_diag_out(r, m, nj))(ssm_c_re, ssm_c_im)

    def stacked(a):
        return a.reshape((n_a * nj,) + a.shape[2:])

    bblk_all, cblk_all = stacked(bblk_l).astype(BF16), stacked(cblk_l).astype(BF16)
    bblk_t_all = stacked(bblk_l.transpose(0, 1, 3, 2)).astype(BF16)
    cblk_t_all = stacked(cblk_l.transpose(0, 1, 3, 2)).astype(BF16)
    tabs_fwd = stacked(jax.vmap(lambda r, m: _scan_tables(r, m, nj, False))(lbr_l, lbi_l))
    tabs_bwd = stacked(jax.vmap(lambda r, m: _scan_tables(r, m, nj, True))(lbr_l, lbi_l))
    skip_rows = skip_all.reshape(n_a, 1, d)

    saved = []
    for i in range(n_a):
        rec = dict(h0=h)
        gain = mix_norm[i:i + 1] + started if i == 0 else mix_norm[i:i + 1]
        u = _norm_fwd(h, gain, f"s5_norm_{i}")
        rec["u"] = u
        states, y = _scan_fwd(u, bblk_all, cblk_all, tabs_fwd, skip_rows, f"s5_scan_{i}", layer=i)
        rec["states"], rec["y"] = states, y
        if i == 0:
            w1_0, w2_0, glu_0 = _exchange_wait(sems_a, src_a, land_a, plan_a, y, "gather_wait_a")
        if i == 1:
            w1_rest, w2_rest, glu_rest, kvf_parts, wq_all, wo_all = _exchange_wait(
                sems_b, src_b, land_b, plan_b, y, "gather_wait_b")
        h, rec["zw"] = _s5_post_fwd(h, y, layer_glu(i), f"s5_glu_{i}")
        rec["h1"] = h
        h, rec["ap"] = _mlp_fwd(h, mlp_norm[i:i + 1], layer_w1(i), layer_w2(i), f"mlp_{i}")
        saved.append(rec)
    h_kv = h
    kvf_all = jnp.concatenate([kvf_parts[s] for s in range(N_CHIPS)], axis=1)
    wk = kvf_all[:, :d]
    wv = kvf_all[:, d:2 * d]
    wf = jnp.pad(kvf_all[:, 2 * d:], ((0, 0), (0, LANES - n_heads)))
    bf_row = jnp.pad(b_f, (0, LANES - n_heads)).reshape(1, LANES)
    wk_x = _slot_cols(wk)
    spread = (jnp.arange(LANES)[:, None] == jnp.arange(n_heads * HEAD_SLOT)[None, :] // HEAD_SLOT).astype(BF16)
    kx, vx, flog, cum, f2_rep = _kvf_fwd(h, kv_norm.reshape(1, d), wk_x, _slot_cols(wv), wf, bf_row,
                                         _slot_ones(n_heads, LANE_ROWSUM_DS), _slot_ones(n_heads, LANE_ROWSUM_P),
                                         spread, "kvf")
    f2_rows = (cum[:, :n_heads] * LOG2E).T.reshape(n_heads // 2, 2, seq)
    wq_x = [_slot_cols(wq_all[jb]) for jb in range(n_b)]
    for jb in range(n_b):
        i = n_a + jb
        rec = dict(h0=h)
        qx = _q_fwd(h, mix_norm[i:i + 1], wq_x[jb], _slot_ones(n_heads, LANE_COLSUM_DS), f"attn_q_{jb}")
        o, lse = _flash_fwd(qx, kx, vx, f2_rows, f"attn_core_{jb}")
        rec["qx"], rec["o"], rec["lse"] = qx, o, lse
        h = _o_fwd(h, o, wo_all[jb], f"attn_out_{jb}")
        rec["h1"] = h
        h, rec["ap"] = _mlp_fwd(h, mlp_norm[i:i + 1], layer_w1(i), layer_w2(i), f"mlp_{i}")
        saved.append(rec)
    dh, loss_row, g_final = _loss_head(h, target, final_norm.reshape(1, d), "loss_head")
    loss = lax.psum(loss_row[0, 0], ("x", "y", "c"))

    head_sel = (jnp.arange(n_heads)[:, None] == jnp.arange(d)[None, :] // HEAD_DIM).astype(BF16)
    slot_lane = jnp.arange(n_heads * HEAD_SLOT)[:, None]
    in_lane = jnp.arange(LANES)[None, :]
    sel_q = (slot_lane == in_lane * HEAD_SLOT + LANE_ROWSUM_DS).astype(BF16)
    sel_k = (slot_lane == in_lane * HEAD_SLOT + LANE_COLSUM_DS).astype(BF16)
    g_mix = [None] * depth
    g_mlp = [None] * depth
    g_w1 = [None] * depth
    g_w2 = [None] * depth
    g_wq = [None] * n_b
    g_wo = [None] * n_b
    g_glu = [None] * n_a
    g_ssm = [None] * n_a
    dk_parts, dv_parts, dq_parts = [], [], []

    red_waits = []

    def reduce_start(entries, name):
        numbers = sorted({e[1] for e in entries})
        lands = {}
        for (_, ln, shape, dt, _, _, _, _) in entries:
            if ln not in lands:
                lands[ln] = lax.empty(shape, dt)
        for (_, ln, _, _, _, _, own, at) in entries:
            lands[ln] = lax.dynamic_update_slice(lands[ln], own, idx(*at))
        plan = [(n, numbers.index(e[1]), e[4], e[5]) for n, e in enumerate(entries)]
        sems, srcs, lands_t, token = _exchange_start([e[0] for e in entries], [lands[ln] for ln in numbers], plan,
                                                     name + "_start")
        red_waits.append((sems, srcs, lands_t, plan, name + "_wait"))
        return token[0:1, 0:1]

    def w1_entry(i, ln, local, n_layers):
        own = lax.dynamic_slice(g_w1[i], idx(0, chip * d), (d, d))[None, None]
        return (g_w1[i], ln, (N_CHIPS, n_layers, d, d), BF16, cols2(d), into(local), own, (chip, local, 0, 0))

    def w2_entry(i, ln, local, n_layers):
        own = lax.dynamic_slice(g_w2[i], idx(chip * d, 0), (d, d))[None, None]
        return (g_w2[i], ln, (N_CHIPS, n_layers, d, d), BF16, rows2(d), into(local), own, (chip, local, 0, 0))

    def glu_entry(i, ln, local, n_layers):
        own = lax.dynamic_slice(g_glu[i], idx(0, chip * 2 * ds4), (d, 2 * ds4))[None, None]
        return (g_glu[i], ln, (N_CHIPS, n_layers, d, 2 * ds4), BF16, cols2(2 * ds4), into(local), own,
                (chip, local, 0, 0))

    def rows_entry(g, ln, local):
        own = lax.dynamic_slice(g, idx(chip * ds4, 0), (ds4, d))[None, None]
        return (g, ln, (N_CHIPS, n_b, ds4, d), BF16, rows2(ds4), into(local), own, (chip, local, 0, 0))

    def into(layer):
        return lambda ref, s: ref.at[s, layer]

    def mlp_back(dh, i, rec, tie=None):
        gain = mlp_norm[i:i + 1] if tie is None else mlp_norm[i:i + 1] + tie
        dh_in, hm, a, dap, g_mlp[i] = _mlp_bwd(dh, rec["h1"], rec["ap"], gain, layer_w1(i),
                                               layer_w2(i), f"mlp_bwd_{i}")
        g_w2[i] = _matmul_tn(a, dh, f"mlp_dw2_{i}")
        g_w1[i] = _matmul_tn(hm, dap, f"mlp_dw1_{i}")
        return dh_in

    for jb in reversed(range(n_b)):
        i = n_a + jb
        rec = saved[i]
        dh = mlp_back(dh, i, rec)
        do, delta = _o_bwd(dh, rec["o"], wo_all[jb], head_sel, f"attn_out_bwd_{jb}")
        g_wo[jb] = _matmul_tn(rec["o"], dh, f"attn_dwo_{jb}")
        dqx, dkx, dv = _flash_bwd(rec["qx"], kx, vx, f2_rep, do, rec["lse"], delta.reshape(n_heads // 2, 2, seq),
                                  f"attn_core_bwd_{jb}")
        dk_parts.append(dkx)
        dv_parts.append(dv)
        dq_parts.append(dqx)
        dh, hn, dqs, g_mix[i] = _q_bwd(dh, rec["h0"], dqx, mix_norm[i:i + 1], wq_x[jb], f"attn_q_bwd_{jb}")
        g_wq[jb] = _unslot_cols(_matmul_tn(hn, dqs, f"attn_dwq_{jb}"))

    dh, hk, dkb, dvb, dfb, g_kvn, g_bf = _kvf_bwd(dh, h_kv, dk_parts[0], dk_parts[1], dv_parts[0], dv_parts[1],
                                                  dq_parts[0], dq_parts[1], flog, kv_norm.reshape(1, d), wk_x, wv, wf,
                                                  sel_q, sel_k, "kvf_bwd")
    g_kvf = jnp.concatenate([_unslot_cols(_matmul_tn(hk, dkb, "kvf_dwk")), _matmul_tn(hk, dvb, "kvf_dwv"),
                             _matmul_tn(hk, dfb, "kvf_dwf")[:, :n_heads]], axis=1)
    kvf_send = g_kvf.reshape(d, N_CHIPS, kvf_cols).transpose(1, 0, 2)
    group = [w1_entry(n_a + jb, 0, jb, n_b) for jb in range(n_b)]
    group += [w2_entry(n_a + jb, 1, jb, n_b) for jb in range(n_b)]
    group.append((kvf_send, 2, (N_CHIPS, d, kvf_cols), BF16, slot, slot,
                  lax.dynamic_index_in_dim(kvf_send, chip, 0, keepdims=True), (chip, 0, 0)))
    group += [rows_entry(g_wq[jb], 3, jb) for jb in range(n_b)]
    group += [rows_entry(g_wo[jb], 4, jb) for jb in range(n_b)]
    tie = reduce_start(group, "reduce_attn")

    for i in reversed(range(n_a)):
        rec = saved[i]
        if i == 0:
            group = [w1_entry(l, 0, l - 1, n_a - 1) for l in range(1, n_a)]
            group += [w2_entry(l, 1, l - 1, n_a - 1) for l in range(1, n_a)]
            group += [glu_entry(l, 2, l - 1, n_a - 1) for l in range(1, n_a)]
            tie = reduce_start(group, "reduce_s5")
        dh = mlp_back(dh, i, rec, tie if i in (0, n_a - 1) else None)
        dy, z, dzw = _s5_post_bwd(dh, rec["y"], rec["zw"], layer_glu(i), f"s5_glu_bwd_{i}")
        g_glu[i] = _matmul_tn(z, dzw, f"s5_dwglu_{i}")
        skip_gain = skip_rows
        if i == 0:
            skip_gain = skip_gain + reduce_start([w1_entry(0, 0, 0, 1), w2_entry(0, 1, 0, 1), glu_entry(0, 2, 0, 1)],
                                                 "reduce_first")
        du, glam8, gd8, gbblk, gcblk = _scan_bwd(dy, rec["u"], rec["states"], bblk_t_all, cblk_t_all, tabs_bwd,
                                                 skip_gain, f"s5_scan_bwd_{i}", layer=i)
        dh, g_mix[i] = _norm_bwd_add(dh, du, rec["h0"], mix_norm[i:i + 1], f"s5_norm_bwd_{i}")
        glam = jnp.sum(glam8, axis=1)
        n_st = glam.shape[1] // 2
        g_lbr = glam[:, :n_st].reshape(n_groups, p_state)
        g_lbi = glam[:, n_st:].reshape(n_groups, p_state)
        g_bbr, g_bbi = _block_diag_in_grad(gbblk, nj, p_state, SSM_GROUP)
        g_cre, g_cim = _block_diag_out_grad(gcblk, nj, p_state, SSM_GROUP)
        _, pull = jax.vjp(_ssm_discretise, ssm_log_dt[i], ssm_a_re[i], ssm_a_im[i], ssm_b_re[i], ssm_b_im[i])
        g_ldt, g_are, g_aim, g_bre, g_bim = pull((g_lbr, g_lbi, g_bbr, g_bbi))
        g_ssm[i] = dict(log_dt=g_ldt, a_re=g_are, a_im=g_aim, b_re=g_bre, b_im=g_bim, c_re=g_cre, c_im=g_cim,
                        d=jnp.sum(gd8, axis=1).reshape(d))
    grad_x = dh[None]

    def stack_small(key):
        return jnp.stack([g_ssm[i][key] for i in range(n_a)])

    small_grads = [jnp.concatenate(g_mix, axis=0), jnp.concatenate(g_mlp, axis=0), stack_small("log_dt"),
                   stack_small("a_re"), stack_small("a_im"), stack_small("b_re"), stack_small("b_im"),
                   stack_small("c_re"), stack_small("c_im"), stack_small("d"), g_kvn.reshape(d),
                   g_bf[0, :n_heads], g_final.reshape(d)]
    small_w = [mix_norm, mlp_norm, ssm_log_dt, ssm_a_re, ssm_a_im, ssm_b_re, ssm_b_im, ssm_c_re, ssm_c_im,
               ssm_d, kv_norm, b_f, final_norm]
    small_m = [m_mix_norm, m_mlp_norm, m_ssm_log_dt, m_ssm_a_re, m_ssm_a_im, m_ssm_b_re, m_ssm_b_im, m_ssm_c_re,
               m_ssm_c_im, m_ssm_d, m_kv_norm, m_b_f, m_final_norm]
    small_v = [v_mix_norm, v_mlp_norm, v_ssm_log_dt, v_ssm_a_re, v_ssm_a_im, v_ssm_b_re, v_ssm_b_im, v_ssm_c_re,
               v_ssm_c_im, v_ssm_d, v_kv_norm, v_b_f, v_final_norm]
    skip_at = 9

    def widen_skip(part):
        return lax.dynamic_update_slice(jnp.zeros((n_a, d), F32), part, (0, chip * ds4))

    small_shapes = [a.shape for a in small_grads]
    pcols = 1024 if d >= 1024 else LANES
    g_small = _pack_small(small_grads, pcols)
    expand = lambda lst: _pack_small([widen_skip(a) if n == skip_at else a for n, a in enumerate(lst)], pcols)
    w_small, m_small, v_small = expand(small_w), expand(small_m), expand(small_v)
    srows = g_small.shape[0]

    assert n_b == 2
    reduce_start([(g_small, 0, (N_CHIPS, srows, pcols), F32, whole, slot, g_small[None], (chip, 0, 0))], "reduce_small")
    landed = [_exchange_wait(sems, srcs, lands, plan, dh, name) for (sems, srcs, lands, plan, name) in red_waits[:3]]
    (a_w1, a_w2, a_kvf, a_wq, a_wo), (s_w1, s_w2, s_glu), (f_w1, f_w2, f_glu) = landed

    def chip_sum(r, name):
        return _sum_chips(r.reshape(N_CHIPS, -1, r.shape[-1]), name)

    sums = [jnp.concatenate([chip_sum(f_w1, "sum_w1_first"), chip_sum(s_w1, "sum_w1_s5"),
                             chip_sum(a_w1, "sum_w1_attn")], axis=0),
            jnp.concatenate([chip_sum(f_w2, "sum_w2_first"), chip_sum(s_w2, "sum_w2_s5"),
                             chip_sum(a_w2, "sum_w2_attn")], axis=0),
            jnp.concatenate([chip_sum(f_glu, "sum_glu_first"), chip_sum(s_glu, "sum_glu_s5")], axis=0),
            chip_sum(a_kvf, "sum_kvf"), chip_sum(a_wq, "sum_wq"), chip_sum(a_wo, "sum_wo")]
    others = _core_exchange(sums, "reduce_cores")

    def two(a):
        return a.reshape(-1, a.shape[-1])

    big_w = [(mlp_w1, m_mlp_w1, v_mlp_w1), (mlp_w2, m_mlp_w2, v_mlp_w2), (ssm_w_glu, m_ssm_w_glu, v_ssm_w_glu),
             (w_kvf, m_w_kvf, v_w_kvf), (attn_wq, m_attn_wq, v_attn_wq), (attn_wo, m_attn_wo, v_attn_wo)]
    big_out = []
    for n, (w, m, v) in enumerate(big_w):
        res = _adamw(sums[n], others[n], two(w), two(m), two(v), f"adamw_{n}")
        big_out.append([r.reshape(w.shape) for r in res])
    sems, srcs, lands, plan, name = red_waits[3]
    (r_small,) = _exchange_wait(sems, srcs, lands, plan, big_out[-1][1], name)
    sum_small = chip_sum(r_small, "sum_small")
    (other_small,) = _core_exchange([sum_small], "reduce_cores_small")
    small_out = _adamw(sum_small, other_small, w_small, m_small, v_small, "adamw_small")

    def narrow_skip(a):
        return lax.dynamic_slice(a, (0, chip * ds4), (n_a, ds4))

    unpacked = []
    for packed in small_out:
        parts = _unpack_small(packed, small_shapes, pcols)
        parts[skip_at] = narrow_skip(parts[skip_at])
        unpacked.append(parts)

    order = ["mix_norm", "mlp_norm", "mlp_w1", "mlp_w2", "ssm_log_dt", "ssm_a_re", "ssm_a_im", "ssm_b_re",
             "ssm_b_im", "ssm_c_re", "ssm_c_im", "ssm_d", "ssm_w_glu", "kv_norm", "w_kvf", "b_f", "attn_wq",
             "attn_wo", "final_norm"]
    small_names = ["mix_norm", "mlp_norm", "ssm_log_dt", "ssm_a_re", "ssm_a_im", "ssm_b_re", "ssm_b_im",
                   "ssm_c_re", "ssm_c_im", "ssm_d", "kv_norm", "b_f", "final_norm"]
    big_names = ["mlp_w1", "mlp_w2", "ssm_w_glu", "w_kvf", "attn_wq", "attn_wo"]
    outs = [loss, grad_x]
    for kind in range(4):
        for name in order:
            if name in big_names:
                outs.append(big_out[big_names.index(name)][kind])
            else:
                outs.append(unpacked[kind][small_names.index(name)])
    return tuple(outs)
```
